```python
import math
import jax, jax.numpy as jnp
from jax import lax
import numpy as np

D_MODEL = 1024
BATCH = 8
SEQ = 4096
DEPTH = 2

GRID_W = 64
D_FF = 2816
NORM_EPS = 1e-6
N_BRANCHES = 3
S5_GROUPS = 32
S5_GROUP_CH = 16
S5_STATE = 64
S5_WIDTH = S5_GROUPS * S5_GROUP_CH
S5_DT_MIN = 1e-3
S5_DT_MAX = 1e-1
GLA_HEADS = 4
GLA_HEAD_DIM = 128
GLA_WIDTH = GLA_HEADS * GLA_HEAD_DIM
GLA_LOWRANK = 16
GLA_TAU = 16.0
GLA_CHUNK = 64
ATTN_Q_HEADS = 8
ATTN_KV_HEADS = 2
ATTN_HEAD_DIM = 64
ATTN_WIDTH = ATTN_Q_HEADS * ATTN_HEAD_DIM
ATTN_KV_WIDTH = ATTN_KV_HEADS * ATTN_HEAD_DIM
ATTN_BLOCK = 128
ROPE_BASE = 10000.0
IN_SPLITS = (S5_WIDTH, GLA_WIDTH, GLA_WIDTH, GLA_WIDTH, GLA_WIDTH, GLA_LOWRANK, GLA_LOWRANK, ATTN_WIDTH, ATTN_KV_WIDTH, ATTN_KV_WIDTH)
IN_WIDTH = sum(IN_SPLITS)

kernel_name = 'hybrid_s5_gla_gqa_macaron_encoder'

F32 = jnp.float32


def rms_norm(x, gain):
    x32 = x.astype(F32)
    y = x32 * lax.rsqrt(jnp.mean(x32 * x32, axis=-1, keepdims=True) + NORM_EPS)
    return (y * gain.astype(F32)).astype(x.dtype)


def swiglu_ffn(h, w_gate, w_up, w_down):
    return (jax.nn.silu(h @ w_gate) * (h @ w_up)) @ w_down


def _linear_recurrence(left, right):
    a_l, b_l = left
    a_r, b_r = right
    return a_r * a_l, a_r * b_l + b_r


def s5_scan_dir(u32, lam_re, lam_im, log_dt, b_re, b_im, c_re, c_im, reverse):
    lam = lax.complex(lam_re.astype(F32), lam_im.astype(F32))
    dt = jnp.exp(log_dt.astype(F32))[:, None]
    lam_bar = jnp.exp(lam * dt)
    b = lax.complex(b_re.astype(F32), b_im.astype(F32))
    b_bar = ((lam_bar - 1.0) / lam)[..., None] * b
    bu = jnp.einsum('blgh,gph->blgp', u32.astype(jnp.complex64), b_bar)
    a = jnp.broadcast_to(lam_bar, bu.shape)
    _, states = lax.associative_scan(_linear_recurrence, (a, bu), axis=1, reverse=reverse)
    c = lax.complex(c_re.astype(F32), c_im.astype(F32))
    return jnp.real(jnp.einsum('blgp,ghp->blgh', states, c))


def s5_branch(u, lam_re, lam_im, log_dt, b_re, b_im, c_re, c_im, d_skip, w_glu):
    bsz, seq_len, _ = u.shape
    u32 = u.astype(F32)
    ug = u32.reshape(bsz, seq_len, S5_GROUPS, S5_GROUP_CH)
    y = (s5_scan_dir(ug, lam_re[0], lam_im[0], log_dt[0], b_re[0], b_im[0], c_re[0], c_im[0], False)
         + s5_scan_dir(ug, lam_re[1], lam_im[1], log_dt[1], b_re[1], b_im[1], c_re[1], c_im[1], True))
    y = y.reshape(bsz, seq_len, S5_WIDTH) + d_skip.astype(F32) * u32
    y = jax.nn.gelu(y).astype(u.dtype)
    return y * jax.nn.sigmoid(y @ w_glu)


def gla_chunked(q, k, v, log_a):
    bsz, seq_len, nh, dk = q.shape
    dv = v.shape[-1]
    n_chunks = seq_len // GLA_CHUNK

    def chunks(t):
        return t.reshape(bsz, n_chunks, GLA_CHUNK, nh, t.shape[-1])

    q, k, v, log_a = chunks(q), chunks(k), chunks(v), chunks(log_a)
    b = jnp.cumsum(log_a, axis=2)
    b_last = b[:, :, -1]
    q_dec = q * jnp.exp(b)
    k_dec = k * jnp.exp(-b)
    mask = jnp.tril(jnp.ones((GLA_CHUNK, GLA_CHUNK), dtype=bool))
    scores = jnp.where(mask, jnp.einsum('bnihd,bnjhd->bnhij', q_dec, k_dec), 0.0)
    o_intra = jnp.einsum('bnhij,bnjhe->bnihe', scores, v)
    k_to_end = k * jnp.exp(b_last[:, :, None] - b)
    chunk_kv = jnp.einsum('bnjhd,bnjhe->nbhde', k_to_end, v)
    chunk_decay = jnp.exp(jnp.moveaxis(b_last, 1, 0))

    def step(state, inp):
        decay, kv = inp
        return decay[..., None] * state + kv, state

    init = jnp.zeros((bsz, nh, dk, dv), F32)
    _, prev_states = lax.scan(step, init, (chunk_decay, chunk_kv))
    o_inter = jnp.einsum('bnihd,nbhde->bnihe', q_dec, prev_states)
    return (o_intra + o_inter).reshape(bsz, seq_len, nh, dv)


def gla_branch(q, k, v, gate, z_f, z_b, w_alpha, b_alpha, norm_gain):
    bsz, seq_len, _ = q.shape

    def heads(t):
        return t.astype(F32).reshape(bsz, seq_len, GLA_HEADS, GLA_HEAD_DIM)

    qh = heads(q) * GLA_HEAD_DIM ** -0.5
    kh = heads(k)
    vh = heads(v)

    def log_gate(z, w, bias):
        logits = (z @ w + bias).astype(F32)
        return heads(jax.nn.log_sigmoid(logits) / GLA_TAU)

    la_f = log_gate(z_f, w_alpha[0], b_alpha[0])
    la_b = log_gate(z_b, w_alpha[1], b_alpha[1])

    def flip(t):
        return jnp.flip(t, axis=1)

    o_f = gla_chunked(qh, kh, vh, la_f)
    o_b = flip(gla_chunked(flip(qh), flip(kh), flip(vh), flip(la_b)))
    o = rms_norm(o_f + o_b, norm_gain).reshape(bsz, seq_len, GLA_WIDTH).astype(q.dtype)
    return o * jax.nn.silu(gate)


def rope_1d(x, pos):
    d = x.shape[-1]
    half = d // 2
    inv_freq = ROPE_BASE ** (-jnp.arange(half, dtype=F32) * 2.0 / d)
    ang = pos.astype(F32)[:, None] * inv_freq[None, :]
    cos = jnp.cos(ang)[:, None, :]
    sin = jnp.sin(ang)[:, None, :]
    x1, x2 = x[..., :half], x[..., half:]
    return jnp.concatenate([x1 * cos - x2 * sin, x2 * cos + x1 * sin], axis=-1)


def axial_rope(x, rows, cols):
    half = x.shape[-1] // 2
    return jnp.concatenate([rope_1d(x[..., :half], rows), rope_1d(x[..., half:], cols)], axis=-1)


def attn_branch(q, k, v, q_gain, k_gain):
    bsz, seq_len, _ = q.shape
    n_rows = seq_len // GRID_W
    rows = jnp.repeat(jnp.arange(n_rows, dtype=jnp.int32), GRID_W)
    cols = jnp.tile(jnp.arange(GRID_W, dtype=jnp.int32), n_rows)
    qh = q.astype(F32).reshape(bsz, seq_len, ATTN_Q_HEADS, ATTN_HEAD_DIM)
    kh = k.astype(F32).reshape(bsz, seq_len, ATTN_KV_HEADS, ATTN_HEAD_DIM)
    vh = v.astype(F32).reshape(bsz, seq_len, ATTN_KV_HEADS, ATTN_HEAD_DIM)
    qh = axial_rope(rms_norm(qh, q_gain), rows, cols) * ATTN_HEAD_DIM ** -0.5
    kh = axial_rope(rms_norm(kh, k_gain), rows, cols)
    group = ATTN_Q_HEADS // ATTN_KV_HEADS
    n_blocks = seq_len // ATTN_BLOCK
    q_blocks = qh.reshape(bsz, n_blocks, ATTN_BLOCK, ATTN_KV_HEADS, group, ATTN_HEAD_DIM)
    q_blocks = jnp.moveaxis(q_blocks, 1, 0)

    def attend(qb):
        s = jnp.einsum('bqkgd,bskd->bkgqs', qb, kh)
        p = jax.nn.softmax(s, axis=-1)
        return jnp.einsum('bkgqs,bskd->bqkgd', p, vh)

    out = lax.map(attend, q_blocks)
    out = jnp.moveaxis(out, 0, 1).reshape(bsz, seq_len, ATTN_WIDTH)
    return out.astype(q.dtype)


def _fwd_setup_inputs(seed: int = 0) -> dict:
    key = jax.random.key(seed)
    keys = iter(jax.random.split(key, 40))

    def normal(shape, scale):
        return scale * jax.random.normal(next(keys), shape, F32)

    def gain(shape):
        return 1.0 + normal(shape, 0.02)

    G, H, P = S5_GROUPS, S5_GROUP_CH, S5_STATE
    x = normal((BATCH, SEQ, D_MODEL), 1.0)
    ffn1_norm = gain((DEPTH, D_MODEL))
    ffn1_w_gate = normal((DEPTH, D_MODEL, D_FF), D_MODEL ** -0.5)
    ffn1_w_up = normal((DEPTH, D_MODEL, D_FF), D_MODEL ** -0.5)
    ffn1_w_down = normal((DEPTH, D_FF, D_MODEL), D_FF ** -0.5)
    mix_norm = gain((DEPTH, D_MODEL))
    w_in = normal((DEPTH, D_MODEL, IN_WIDTH), D_MODEL ** -0.5)
    s5_lambda_re = -0.5 + normal((DEPTH, 2, G, P), 0.01)
    s5_lambda_im = math.pi * jnp.arange(P, dtype=F32) + normal((DEPTH, 2, G, P), 0.01)
    s5_log_dt = jax.random.uniform(next(keys), (DEPTH, 2, G), F32, math.log(S5_DT_MIN), math.log(S5_DT_MAX))
    s5_b_re = normal((DEPTH, 2, G, P, H), (0.5 / H) ** 0.5)
    s5_b_im = normal((DEPTH, 2, G, P, H), (0.5 / H) ** 0.5)
    s5_c_re = normal((DEPTH, 2, G, H, P), (0.5 / P) ** 0.5)
    s5_c_im = normal((DEPTH, 2, G, H, P), (0.5 / P) ** 0.5)
    s5_d = normal((DEPTH, S5_WIDTH), 1.0)
    s5_w_glu = normal((DEPTH, S5_WIDTH, S5_WIDTH), S5_WIDTH ** -0.5)
    gla_w_alpha = normal((DEPTH, 2, GLA_LOWRANK, GLA_WIDTH), GLA_LOWRANK ** -0.5)
    gla_b_alpha = normal((DEPTH, 2, GLA_WIDTH), 0.1)
    gla_norm = gain((DEPTH, GLA_HEAD_DIM))
    attn_q_norm = gain((DEPTH, ATTN_HEAD_DIM))
    attn_k_norm = gain((DEPTH, ATTN_HEAD_DIM))
    w_branch_s5 = normal((DEPTH, S5_WIDTH, D_MODEL), S5_WIDTH ** -0.5)
    w_branch_gla = normal((DEPTH, GLA_WIDTH, D_MODEL), GLA_WIDTH ** -0.5)
    w_branch_attn = normal((DEPTH, ATTN_WIDTH, D_MODEL), ATTN_WIDTH ** -0.5)
    w_merge_gate = normal((DEPTH, D_MODEL, N_BRANCHES * D_MODEL), D_MODEL ** -0.5)
    b_merge_gate = normal((DEPTH, N_BRANCHES * D_MODEL), 0.01)
    w_out = normal((DEPTH, D_MODEL, D_MODEL), D_MODEL ** -0.5)
    ffn2_norm = gain((DEPTH, D_MODEL))
    ffn2_w_gate = normal((DEPTH, D_MODEL, D_FF), D_MODEL ** -0.5)
    ffn2_w_up = normal((DEPTH, D_MODEL, D_FF), D_MODEL ** -0.5)
    ffn2_w_down = normal((DEPTH, D_FF, D_MODEL), D_FF ** -0.5)
    final_norm = gain((D_MODEL,))
    return {'x': x, 'ffn1_norm': ffn1_norm, 'ffn1_w_gate': ffn1_w_gate, 'ffn1_w_up': ffn1_w_up,
            'ffn1_w_down': ffn1_w_down, 'mix_norm': mix_norm, 'w_in': w_in,
            's5_lambda_re': s5_lambda_re, 's5_lambda_im': s5_lambda_im, 's5_log_dt': s5_log_dt,
            's5_b_re': s5_b_re, 's5_b_im': s5_b_im, 's5_c_re': s5_c_re, 's5_c_im': s5_c_im,
            's5_d': s5_d, 's5_w_glu': s5_w_glu, 'gla_w_alpha': gla_w_alpha, 'gla_b_alpha': gla_b_alpha,
            'gla_norm': gla_norm, 'attn_q_norm': attn_q_norm, 'attn_k_norm': attn_k_norm,
            'w_branch_s5': w_branch_s5, 'w_branch_gla': w_branch_gla, 'w_branch_attn': w_branch_attn,
            'w_merge_gate': w_merge_gate, 'b_merge_gate': b_merge_gate, 'w_out': w_out,
            'ffn2_norm': ffn2_norm, 'ffn2_w_gate': ffn2_w_gate, 'ffn2_w_up': ffn2_w_up,
            'ffn2_w_down': ffn2_w_down, 'final_norm': final_norm}


def _fwd_reference(x, ffn1_norm, ffn1_w_gate, ffn1_w_up, ffn1_w_down, mix_norm, w_in,
              s5_lambda_re, s5_lambda_im, s5_log_dt, s5_b_re, s5_b_im, s5_c_re, s5_c_im,
              s5_d, s5_w_glu, gla_w_alpha, gla_b_alpha, gla_norm, attn_q_norm, attn_k_norm,
              w_branch_s5, w_branch_gla, w_branch_attn, w_merge_gate, b_merge_gate, w_out,
              ffn2_norm, ffn2_w_gate, ffn2_w_up, ffn2_w_down, final_norm):
    bsz, seq_len, _ = x.shape
    split_at = [int(c) for c in np.cumsum(IN_SPLITS)[:-1]]
    for i in range(DEPTH):
        h = rms_norm(x, ffn1_norm[i])
        x = x + 0.5 * swiglu_ffn(h, ffn1_w_gate[i], ffn1_w_up[i], ffn1_w_down[i])

        h = rms_norm(x, mix_norm[i])
        (s5_u, gla_q, gla_k, gla_v, gla_g, gla_zf, gla_zb,
         at_q, at_k, at_v) = jnp.split(h @ w_in[i], split_at, axis=-1)
        y_s5 = s5_branch(s5_u, s5_lambda_re[i], s5_lambda_im[i], s5_log_dt[i], s5_b_re[i], s5_b_im[i],
                         s5_c_re[i], s5_c_im[i], s5_d[i], s5_w_glu[i])
        y_gla = gla_branch(gla_q, gla_k, gla_v, gla_g, gla_zf, gla_zb,
                           gla_w_alpha[i], gla_b_alpha[i], gla_norm[i])
        y_attn = attn_branch(at_q, at_k, at_v, attn_q_norm[i], attn_k_norm[i])
        gates = jax.nn.sigmoid(h @ w_merge_gate[i] + b_merge_gate[i])
        gates = gates.reshape(bsz, seq_len, N_BRANCHES, D_MODEL)
        merged = (gates[:, :, 0] * (y_s5 @ w_branch_s5[i])
                  + gates[:, :, 1] * (y_gla @ w_branch_gla[i])
                  + gates[:, :, 2] * (y_attn @ w_branch_attn[i]))
        x = x + merged @ w_out[i]

        h = rms_norm(x, ffn2_norm[i])
        x = x + 0.5 * swiglu_ffn(h, ffn2_w_gate[i], ffn2_w_up[i], ffn2_w_down[i])
    return rms_norm(x, final_norm)


import jax as _jax
import jax.numpy as _jnp

TWIN_FORMAT = 'train_step'
FWD_PARAMS = ['x', 'ffn1_norm', 'ffn1_w_gate', 'ffn1_w_up', 'ffn1_w_down', 'mix_norm', 'w_in', 's5_lambda_re', 's5_lambda_im', 's5_log_dt', 's5_b_re', 's5_b_im', 's5_c_re', 's5_c_im', 's5_d', 's5_w_glu', 'gla_w_alpha', 'gla_b_alpha', 'gla_norm', 'attn_q_norm', 'attn_k_norm', 'w_branch_s5', 'w_branch_gla', 'w_branch_attn', 'w_merge_gate', 'b_merge_gate', 'w_out', 'ffn2_norm', 'ffn2_w_gate', 'ffn2_w_up', 'ffn2_w_down', 'final_norm']
TWIN_WEIGHTS = ['ffn1_norm', 'ffn1_w_gate', 'ffn1_w_up', 'ffn1_w_down', 'mix_norm', 'w_in', 's5_lambda_re', 's5_lambda_im', 's5_log_dt', 's5_b_re', 's5_b_im', 's5_c_re', 's5_c_im', 's5_d', 's5_w_glu', 'gla_w_alpha', 'gla_b_alpha', 'gla_norm', 'attn_q_norm', 'attn_k_norm', 'w_branch_s5', 'w_branch_gla', 'w_branch_attn', 'w_merge_gate', 'b_merge_gate', 'w_out', 'ffn2_norm', 'ffn2_w_gate', 'ffn2_w_up', 'ffn2_w_down', 'final_norm']
TWIN_DIFF_INPUT = 'x'
TWIN_INPUTS = ['x', 'ffn1_norm', 'ffn1_w_gate', 'ffn1_w_up', 'ffn1_w_down', 'mix_norm', 'w_in', 's5_lambda_re', 's5_lambda_im', 's5_log_dt', 's5_b_re', 's5_b_im', 's5_c_re', 's5_c_im', 's5_d', 's5_w_glu', 'gla_w_alpha', 'gla_b_alpha', 'gla_norm', 'attn_q_norm', 'attn_k_norm', 'w_branch_s5', 'w_branch_gla', 'w_branch_attn', 'w_merge_gate', 'b_merge_gate', 'w_out', 'ffn2_norm', 'ffn2_w_gate', 'ffn2_w_up', 'ffn2_w_down', 'final_norm', 'loss_target', 'm_ffn1_norm', 'm_ffn1_w_gate', 'm_ffn1_w_up', 'm_ffn1_w_down', 'm_mix_norm', 'm_w_in', 'm_s5_lambda_re', 'm_s5_lambda_im', 'm_s5_log_dt', 'm_s5_b_re', 'm_s5_b_im', 'm_s5_c_re', 'm_s5_c_im', 'm_s5_d', 'm_s5_w_glu', 'm_gla_w_alpha', 'm_gla_b_alpha', 'm_gla_norm', 'm_attn_q_norm', 'm_attn_k_norm', 'm_w_branch_s5', 'm_w_branch_gla', 'm_w_branch_attn', 'm_w_merge_gate', 'm_b_merge_gate', 'm_w_out', 'm_ffn2_norm', 'm_ffn2_w_gate', 'm_ffn2_w_up', 'm_ffn2_w_down', 'm_final_norm', 'v_ffn1_norm', 'v_ffn1_w_gate', 'v_ffn1_w_up', 'v_ffn1_w_down', 'v_mix_norm', 'v_w_in', 'v_s5_lambda_re', 'v_s5_lambda_im', 'v_s5_log_dt', 'v_s5_b_re', 'v_s5_b_im', 'v_s5_c_re', 'v_s5_c_im', 'v_s5_d', 'v_s5_w_glu', 'v_gla_w_alpha', 'v_gla_b_alpha', 'v_gla_norm', 'v_attn_q_norm', 'v_attn_k_norm', 'v_w_branch_s5', 'v_w_branch_gla', 'v_w_branch_attn', 'v_w_merge_gate', 'v_b_merge_gate', 'v_w_out', 'v_ffn2_norm', 'v_ffn2_w_gate', 'v_ffn2_w_up', 'v_ffn2_w_down', 'v_final_norm']
TWIN_OUTPUTS = ['loss', 'grad_x', 'grad_ffn1_norm', 'grad_ffn1_w_gate', 'grad_ffn1_w_up', 'grad_ffn1_w_down', 'grad_mix_norm', 'grad_w_in', 'grad_s5_lambda_re', 'grad_s5_lambda_im', 'grad_s5_log_dt', 'grad_s5_b_re', 'grad_s5_b_im', 'grad_s5_c_re', 'grad_s5_c_im', 'grad_s5_d', 'grad_s5_w_glu', 'grad_gla_w_alpha', 'grad_gla_b_alpha', 'grad_gla_norm', 'grad_attn_q_norm', 'grad_attn_k_norm', 'grad_w_branch_s5', 'grad_w_branch_gla', 'grad_w_branch_attn', 'grad_w_merge_gate', 'grad_b_merge_gate', 'grad_w_out', 'grad_ffn2_norm', 'grad_ffn2_w_gate', 'grad_ffn2_w_up', 'grad_ffn2_w_down', 'grad_final_norm', 'delta_ffn1_norm', 'delta_ffn1_w_gate', 'delta_ffn1_w_up', 'delta_ffn1_w_down', 'delta_mix_norm', 'delta_w_in', 'delta_s5_lambda_re', 'delta_s5_lambda_im', 'delta_s5_log_dt', 'delta_s5_b_re', 'delta_s5_b_im', 'delta_s5_c_re', 'delta_s5_c_im', 'delta_s5_d', 'delta_s5_w_glu', 'delta_gla_w_alpha', 'delta_gla_b_alpha', 'delta_gla_norm', 'delta_attn_q_norm', 'delta_attn_k_norm', 'delta_w_branch_s5', 'delta_w_branch_gla', 'delta_w_branch_attn', 'delta_w_merge_gate', 'delta_b_merge_gate', 'delta_w_out', 'delta_ffn2_norm', 'delta_ffn2_w_gate', 'delta_ffn2_w_up', 'delta_ffn2_w_down', 'delta_final_norm', 'new_m_ffn1_norm', 'new_m_ffn1_w_gate', 'new_m_ffn1_w_up', 'new_m_ffn1_w_down', 'new_m_mix_norm', 'new_m_w_in', 'new_m_s5_lambda_re', 'new_m_s5_lambda_im', 'new_m_s5_log_dt', 'new_m_s5_b_re', 'new_m_s5_b_im', 'new_m_s5_c_re', 'new_m_s5_c_im', 'new_m_s5_d', 'new_m_s5_w_glu', 'new_m_gla_w_alpha', 'new_m_gla_b_alpha', 'new_m_gla_norm', 'new_m_attn_q_norm', 'new_m_attn_k_norm', 'new_m_w_branch_s5', 'new_m_w_branch_gla', 'new_m_w_branch_attn', 'new_m_w_merge_gate', 'new_m_b_merge_gate', 'new_m_w_out', 'new_m_ffn2_norm', 'new_m_ffn2_w_gate', 'new_m_ffn2_w_up', 'new_m_ffn2_w_down', 'new_m_final_norm', 'new_v_ffn1_norm', 'new_v_ffn1_w_gate', 'new_v_ffn1_w_up', 'new_v_ffn1_w_down', 'new_v_mix_norm', 'new_v_w_in', 'new_v_s5_lambda_re', 'new_v_s5_lambda_im', 'new_v_s5_log_dt', 'new_v_s5_b_re', 'new_v_s5_b_im', 'new_v_s5_c_re', 'new_v_s5_c_im', 'new_v_s5_d', 'new_v_s5_w_glu', 'new_v_gla_w_alpha', 'new_v_gla_b_alpha', 'new_v_gla_norm', 'new_v_attn_q_norm', 'new_v_attn_k_norm', 'new_v_w_branch_s5', 'new_v_w_branch_gla', 'new_v_w_branch_attn', 'new_v_w_merge_gate', 'new_v_b_merge_gate', 'new_v_w_out', 'new_v_ffn2_norm', 'new_v_ffn2_w_gate', 'new_v_ffn2_w_up', 'new_v_ffn2_w_down', 'new_v_final_norm']
TWIN_LEAF_KINDS = {'loss': 'loss', 'grad_x': 'grad_x', 'grad_ffn1_norm': 'grad_w', 'grad_ffn1_w_gate': 'grad_w', 'grad_ffn1_w_up': 'grad_w', 'grad_ffn1_w_down': 'grad_w', 'grad_mix_norm': 'grad_w', 'grad_w_in': 'grad_w', 'grad_s5_lambda_re': 'grad_w', 'grad_s5_lambda_im': 'grad_w', 'grad_s5_log_dt': 'grad_w', 'grad_s5_b_re': 'grad_w', 'grad_s5_b_im': 'grad_w', 'grad_s5_c_re': 'grad_w', 'grad_s5_c_im': 'grad_w', 'grad_s5_d': 'grad_w', 'grad_s5_w_glu': 'grad_w', 'grad_gla_w_alpha': 'grad_w', 'grad_gla_b_alpha': 'grad_w', 'grad_gla_norm': 'grad_w', 'grad_attn_q_norm': 'grad_w', 'grad_attn_k_norm': 'grad_w', 'grad_w_branch_s5': 'grad_w', 'grad_w_branch_gla': 'grad_w', 'grad_w_branch_attn': 'grad_w', 'grad_w_merge_gate': 'grad_w', 'grad_b_merge_gate': 'grad_w', 'grad_w_out': 'grad_w', 'grad_ffn2_norm': 'grad_w', 'grad_ffn2_w_gate': 'grad_w', 'grad_ffn2_w_up': 'grad_w', 'grad_ffn2_w_down': 'grad_w', 'grad_final_norm': 'grad_w', 'delta_ffn1_norm': 'delta_w', 'delta_ffn1_w_gate': 'delta_w', 'delta_ffn1_w_up': 'delta_w', 'delta_ffn1_w_down': 'delta_w', 'delta_mix_norm': 'delta_w', 'delta_w_in': 'delta_w', 'delta_s5_lambda_re': 'delta_w', 'delta_s5_lambda_im': 'delta_w', 'delta_s5_log_dt': 'delta_w', 'delta_s5_b_re': 'delta_w', 'delta_s5_b_im': 'delta_w', 'delta_s5_c_re': 'delta_w', 'delta_s5_c_im': 'delta_w', 'delta_s5_d': 'delta_w', 'delta_s5_w_glu': 'delta_w', 'delta_gla_w_alpha': 'delta_w', 'delta_gla_b_alpha': 'delta_w', 'delta_gla_norm': 'delta_w', 'delta_attn_q_norm': 'delta_w', 'delta_attn_k_norm': 'delta_w', 'delta_w_branch_s5': 'delta_w', 'delta_w_branch_gla': 'delta_w', 'delta_w_branch_attn': 'delta_w', 'delta_w_merge_gate': 'delta_w', 'delta_b_merge_gate': 'delta_w', 'delta_w_out': 'delta_w', 'delta_ffn2_norm': 'delta_w', 'delta_ffn2_w_gate': 'delta_w', 'delta_ffn2_w_up': 'delta_w', 'delta_ffn2_w_down': 'delta_w', 'delta_final_norm': 'delta_w', 'new_m_ffn1_norm': 'new_m', 'new_m_ffn1_w_gate': 'new_m', 'new_m_ffn1_w_up': 'new_m', 'new_m_ffn1_w_down': 'new_m', 'new_m_mix_norm': 'new_m', 'new_m_w_in': 'new_m', 'new_m_s5_lambda_re': 'new_m', 'new_m_s5_lambda_im': 'new_m', 'new_m_s5_log_dt': 'new_m', 'new_m_s5_b_re': 'new_m', 'new_m_s5_b_im': 'new_m', 'new_m_s5_c_re': 'new_m', 'new_m_s5_c_im': 'new_m', 'new_m_s5_d': 'new_m', 'new_m_s5_w_glu': 'new_m', 'new_m_gla_w_alpha': 'new_m', 'new_m_gla_b_alpha': 'new_m', 'new_m_gla_norm': 'new_m', 'new_m_attn_q_norm': 'new_m', 'new_m_attn_k_norm': 'new_m', 'new_m_w_branch_s5': 'new_m', 'new_m_w_branch_gla': 'new_m', 'new_m_w_branch_attn': 'new_m', 'new_m_w_merge_gate': 'new_m', 'new_m_b_merge_gate': 'new_m', 'new_m_w_out': 'new_m', 'new_m_ffn2_norm': 'new_m', 'new_m_ffn2_w_gate': 'new_m', 'new_m_ffn2_w_up': 'new_m', 'new_m_ffn2_w_down': 'new_m', 'new_m_final_norm': 'new_m', 'new_v_ffn1_norm': 'new_v', 'new_v_ffn1_w_gate': 'new_v', 'new_v_ffn1_w_up': 'new_v', 'new_v_ffn1_w_down': 'new_v', 'new_v_mix_norm': 'new_v', 'new_v_w_in': 'new_v', 'new_v_s5_lambda_re': 'new_v', 'new_v_s5_lambda_im': 'new_v', 'new_v_s5_log_dt': 'new_v', 'new_v_s5_b_re': 'new_v', 'new_v_s5_b_im': 'new_v', 'new_v_s5_c_re': 'new_v', 'new_v_s5_c_im': 'new_v', 'new_v_s5_d': 'new_v', 'new_v_s5_w_glu': 'new_v', 'new_v_gla_w_alpha': 'new_v', 'new_v_gla_b_alpha': 'new_v', 'new_v_gla_norm': 'new_v', 'new_v_attn_q_norm': 'new_v', 'new_v_attn_k_norm': 'new_v', 'new_v_w_branch_s5': 'new_v', 'new_v_w_branch_gla': 'new_v', 'new_v_w_branch_attn': 'new_v', 'new_v_w_merge_gate': 'new_v', 'new_v_b_merge_gate': 'new_v', 'new_v_w_out': 'new_v', 'new_v_ffn2_norm': 'new_v', 'new_v_ffn2_w_gate': 'new_v', 'new_v_ffn2_w_up': 'new_v', 'new_v_ffn2_w_down': 'new_v', 'new_v_final_norm': 'new_v'}


def _forward(args):
    return _fwd_reference(*[args[k] for k in FWD_PARAMS])


def _output_shape():
    def fwd():
        inp = _fwd_setup_inputs(0)
        return _fwd_reference(*[inp[k] for k in FWD_PARAMS])
    out = _jax.eval_shape(fwd)
    return out.shape, out.dtype

N_MICROBATCH = 1
ADAM_LR = 0.001
ADAM_B1 = 0.9
ADAM_B2 = 0.999
ADAM_EPS = 1e-08
ADAM_WD = 0.01
ADAM_STEP = 10
PER_EXAMPLE_BATCH_AXIS = {'x': 0, 'loss_target': 0}
SHARED_INPUTS = []
_WEIGHT_DTYPES = {'ffn1_norm': _jnp.float32, 'ffn1_w_gate': _jnp.float32, 'ffn1_w_up': _jnp.float32, 'ffn1_w_down': _jnp.float32, 'mix_norm': _jnp.float32, 'w_in': _jnp.float32, 's5_lambda_re': _jnp.float32, 's5_lambda_im': _jnp.float32, 's5_log_dt': _jnp.float32, 's5_b_re': _jnp.float32, 's5_b_im': _jnp.float32, 's5_c_re': _jnp.float32, 's5_c_im': _jnp.float32, 's5_d': _jnp.float32, 's5_w_glu': _jnp.float32, 'gla_w_alpha': _jnp.float32, 'gla_b_alpha': _jnp.float32, 'gla_norm': _jnp.float32, 'attn_q_norm': _jnp.float32, 'attn_k_norm': _jnp.float32, 'w_branch_s5': _jnp.float32, 'w_branch_gla': _jnp.float32, 'w_branch_attn': _jnp.float32, 'w_merge_gate': _jnp.float32, 'b_merge_gate': _jnp.float32, 'w_out': _jnp.float32, 'ffn2_norm': _jnp.float32, 'ffn2_w_gate': _jnp.float32, 'ffn2_w_up': _jnp.float32, 'ffn2_w_down': _jnp.float32, 'final_norm': _jnp.float32}
MOMENT_SCALE = {'ffn1_norm': 8.975034e-02, 'ffn1_w_gate': 3.802947e-02, 'ffn1_w_up': 3.686782e-02, 'ffn1_w_down': 6.113379e-02, 'mix_norm': 1.272773e-01, 'w_in': 6.659453e-02, 's5_lambda_re': 2.597115e-03, 's5_lambda_im': 2.330884e-03, 's5_log_dt': 1.605293e+00, 's5_b_re': 1.573407e-03, 's5_b_im': 1.598195e-03, 's5_c_re': 3.188365e-03, 's5_c_im': 3.143106e-03, 's5_d': 5.443599e-02, 's5_w_glu': 1.366117e-02, 'gla_w_alpha': 6.084894e-03, 'gla_b_alpha': 2.483067e-02, 'gla_norm': 1.673682e-01, 'attn_q_norm': 3.815254e-02, 'attn_k_norm': 4.153133e-02, 'w_branch_s5': 3.453745e-02, 'w_branch_gla': 5.575542e-02, 'w_branch_attn': 1.048406e-02, 'w_merge_gate': 1.474344e-02, 'b_merge_gate': 1.471685e-02, 'w_out': 6.525183e-02, 'ffn2_norm': 6.938876e-02, 'ffn2_w_gate': 2.990571e-02, 'ffn2_w_up': 2.895507e-02, 'ffn2_w_down': 4.815333e-02, 'final_norm': 3.201350e+01}


def _to_microbatches(a, axis):
    t = _jnp.moveaxis(a, axis, 0)
    t = t.reshape((N_MICROBATCH, t.shape[0] // N_MICROBATCH) + t.shape[1:])
    return _jnp.moveaxis(t, 1, axis + 1)


def setup_inputs(seed: int = 0) -> dict:
    inp = _fwd_setup_inputs(seed)
    key = _jax.random.fold_in(_jax.random.key(seed), 7919)
    shape, _ = _output_shape()
    out = dict(inp)
    out["loss_target"] = _jax.random.normal(_jax.random.fold_in(key, 0), shape, _jnp.float32)
    for i, name in enumerate(TWIN_WEIGHTS):
        w = inp[name].astype(_jnp.float32)
        if MOMENT_SCALE is None:
            s = _jnp.sqrt(_jnp.mean(_jnp.square(w)) + 1e-30)
        else:
            s = MOMENT_SCALE[name]
        km, kv = _jax.random.split(_jax.random.fold_in(key, i + 1))
        out[name] = w
        out["m_" + name] = s * _jax.random.normal(km, w.shape, _jnp.float32)
        out["v_" + name] = (s * s) * _jax.random.uniform(kv, w.shape, _jnp.float32, 0.5, 1.5)
    if N_MICROBATCH > 1:
        for name, axis in PER_EXAMPLE_BATCH_AXIS.items():
            out[name] = _to_microbatches(out[name], axis)
    return {'x': out['x'], 'ffn1_norm': out['ffn1_norm'], 'ffn1_w_gate': out['ffn1_w_gate'], 'ffn1_w_up': out['ffn1_w_up'], 'ffn1_w_down': out['ffn1_w_down'], 'mix_norm': out['mix_norm'], 'w_in': out['w_in'], 's5_lambda_re': out['s5_lambda_re'], 's5_lambda_im': out['s5_lambda_im'], 's5_log_dt': out['s5_log_dt'], 's5_b_re': out['s5_b_re'], 's5_b_im': out['s5_b_im'], 's5_c_re': out['s5_c_re'], 's5_c_im': out['s5_c_im'], 's5_d': out['s5_d'], 's5_w_glu': out['s5_w_glu'], 'gla_w_alpha': out['gla_w_alpha'], 'gla_b_alpha': out['gla_b_alpha'], 'gla_norm': out['gla_norm'], 'attn_q_norm': out['attn_q_norm'], 'attn_k_norm': out['attn_k_norm'], 'w_branch_s5': out['w_branch_s5'], 'w_branch_gla': out['w_branch_gla'], 'w_branch_attn': out['w_branch_attn'], 'w_merge_gate': out['w_merge_gate'], 'b_merge_gate': out['b_merge_gate'], 'w_out': out['w_out'], 'ffn2_norm': out['ffn2_norm'], 'ffn2_w_gate': out['ffn2_w_gate'], 'ffn2_w_up': out['ffn2_w_up'], 'ffn2_w_down': out['ffn2_w_down'], 'final_norm': out['final_norm'], 'loss_target': out['loss_target'], 'm_ffn1_norm': out['m_ffn1_norm'], 'm_ffn1_w_gate': out['m_ffn1_w_gate'], 'm_ffn1_w_up': out['m_ffn1_w_up'], 'm_ffn1_w_down': out['m_ffn1_w_down'], 'm_mix_norm': out['m_mix_norm'], 'm_w_in': out['m_w_in'], 'm_s5_lambda_re': out['m_s5_lambda_re'], 'm_s5_lambda_im': out['m_s5_lambda_im'], 'm_s5_log_dt': out['m_s5_log_dt'], 'm_s5_b_re': out['m_s5_b_re'], 'm_s5_b_im': out['m_s5_b_im'], 'm_s5_c_re': out['m_s5_c_re'], 'm_s5_c_im': out['m_s5_c_im'], 'm_s5_d': out['m_s5_d'], 'm_s5_w_glu': out['m_s5_w_glu'], 'm_gla_w_alpha': out['m_gla_w_alpha'], 'm_gla_b_alpha': out['m_gla_b_alpha'], 'm_gla_norm': out['m_gla_norm'], 'm_attn_q_norm': out['m_attn_q_norm'], 'm_attn_k_norm': out['m_attn_k_norm'], 'm_w_branch_s5': out['m_w_branch_s5'], 'm_w_branch_gla': out['m_w_branch_gla'], 'm_w_branch_attn': out['m_w_branch_attn'], 'm_w_merge_gate': out['m_w_merge_gate'], 'm_b_merge_gate': out['m_b_merge_gate'], 'm_w_out': out['m_w_out'], 'm_ffn2_norm': out['m_ffn2_norm'], 'm_ffn2_w_gate': out['m_ffn2_w_gate'], 'm_ffn2_w_up': out['m_ffn2_w_up'], 'm_ffn2_w_down': out['m_ffn2_w_down'], 'm_final_norm': out['m_final_norm'], 'v_ffn1_norm': out['v_ffn1_norm'], 'v_ffn1_w_gate': out['v_ffn1_w_gate'], 'v_ffn1_w_up': out['v_ffn1_w_up'], 'v_ffn1_w_down': out['v_ffn1_w_down'], 'v_mix_norm': out['v_mix_norm'], 'v_w_in': out['v_w_in'], 'v_s5_lambda_re': out['v_s5_lambda_re'], 'v_s5_lambda_im': out['v_s5_lambda_im'], 'v_s5_log_dt': out['v_s5_log_dt'], 'v_s5_b_re': out['v_s5_b_re'], 'v_s5_b_im': out['v_s5_b_im'], 'v_s5_c_re': out['v_s5_c_re'], 'v_s5_c_im': out['v_s5_c_im'], 'v_s5_d': out['v_s5_d'], 'v_s5_w_glu': out['v_s5_w_glu'], 'v_gla_w_alpha': out['v_gla_w_alpha'], 'v_gla_b_alpha': out['v_gla_b_alpha'], 'v_gla_norm': out['v_gla_norm'], 'v_attn_q_norm': out['v_attn_q_norm'], 'v_attn_k_norm': out['v_attn_k_norm'], 'v_w_branch_s5': out['v_w_branch_s5'], 'v_w_branch_gla': out['v_w_branch_gla'], 'v_w_branch_attn': out['v_w_branch_attn'], 'v_w_merge_gate': out['v_w_merge_gate'], 'v_b_merge_gate': out['v_b_merge_gate'], 'v_w_out': out['v_w_out'], 'v_ffn2_norm': out['v_ffn2_norm'], 'v_ffn2_w_gate': out['v_ffn2_w_gate'], 'v_ffn2_w_up': out['v_ffn2_w_up'], 'v_ffn2_w_down': out['v_ffn2_w_down'], 'v_final_norm': out['v_final_norm']}


def _loss(weights, diff, rest, loss_target):
    with _jax.named_scope("forward"):
        args = {**rest, TWIN_DIFF_INPUT: diff, **{k: w.astype(_WEIGHT_DTYPES[k]) for k, w in weights.items()}}
        y = _forward(args)
    with _jax.named_scope("loss_head"):
        err = _jnp.square(y.astype(_jnp.float32) - loss_target)
        return 0.5 * _jnp.sum(_jnp.mean(err, axis=-1)) if err.ndim else 0.5 * err


def _adamw(w, g, m, v):
    m = ADAM_B1 * m + (1.0 - ADAM_B1) * g
    v = ADAM_B2 * v + (1.0 - ADAM_B2) * _jnp.square(g)
    m_hat = m / (1.0 - ADAM_B1 ** ADAM_STEP)
    v_hat = v / (1.0 - ADAM_B2 ** ADAM_STEP)
    delta = -ADAM_LR * (m_hat / (_jnp.sqrt(v_hat) + ADAM_EPS) + ADAM_WD * w)
    return delta, m, v


def reference(x, ffn1_norm, ffn1_w_gate, ffn1_w_up, ffn1_w_down, mix_norm, w_in, s5_lambda_re, s5_lambda_im, s5_log_dt, s5_b_re, s5_b_im, s5_c_re, s5_c_im, s5_d, s5_w_glu, gla_w_alpha, gla_b_alpha, gla_norm, attn_q_norm, attn_k_norm, w_branch_s5, w_branch_gla, w_branch_attn, w_merge_gate, b_merge_gate, w_out, ffn2_norm, ffn2_w_gate, ffn2_w_up, ffn2_w_down, final_norm, loss_target, m_ffn1_norm, m_ffn1_w_gate, m_ffn1_w_up, m_ffn1_w_down, m_mix_norm, m_w_in, m_s5_lambda_re, m_s5_lambda_im, m_s5_log_dt, m_s5_b_re, m_s5_b_im, m_s5_c_re, m_s5_c_im, m_s5_d, m_s5_w_glu, m_gla_w_alpha, m_gla_b_alpha, m_gla_norm, m_attn_q_norm, m_attn_k_norm, m_w_branch_s5, m_w_branch_gla, m_w_branch_attn, m_w_merge_gate, m_b_merge_gate, m_w_out, m_ffn2_norm, m_ffn2_w_gate, m_ffn2_w_up, m_ffn2_w_down, m_final_norm, v_ffn1_norm, v_ffn1_w_gate, v_ffn1_w_up, v_ffn1_w_down, v_mix_norm, v_w_in, v_s5_lambda_re, v_s5_lambda_im, v_s5_log_dt, v_s5_b_re, v_s5_b_im, v_s5_c_re, v_s5_c_im, v_s5_d, v_s5_w_glu, v_gla_w_alpha, v_gla_b_alpha, v_gla_norm, v_attn_q_norm, v_attn_k_norm, v_w_branch_s5, v_w_branch_gla, v_w_branch_attn, v_w_merge_gate, v_b_merge_gate, v_w_out, v_ffn2_norm, v_ffn2_w_gate, v_ffn2_w_up, v_ffn2_w_down, v_final_norm):
    given = dict(x=x, ffn1_norm=ffn1_norm, ffn1_w_gate=ffn1_w_gate, ffn1_w_up=ffn1_w_up, ffn1_w_down=ffn1_w_down, mix_norm=mix_norm, w_in=w_in, s5_lambda_re=s5_lambda_re, s5_lambda_im=s5_lambda_im, s5_log_dt=s5_log_dt, s5_b_re=s5_b_re, s5_b_im=s5_b_im, s5_c_re=s5_c_re, s5_c_im=s5_c_im, s5_d=s5_d, s5_w_glu=s5_w_glu, gla_w_alpha=gla_w_alpha, gla_b_alpha=gla_b_alpha, gla_norm=gla_norm, attn_q_norm=attn_q_norm, attn_k_norm=attn_k_norm, w_branch_s5=w_branch_s5, w_branch_gla=w_branch_gla, w_branch_attn=w_branch_attn, w_merge_gate=w_merge_gate, b_merge_gate=b_merge_gate, w_out=w_out, ffn2_norm=ffn2_norm, ffn2_w_gate=ffn2_w_gate, ffn2_w_up=ffn2_w_up, ffn2_w_down=ffn2_w_down, final_norm=final_norm, loss_target=loss_target, m_ffn1_norm=m_ffn1_norm, m_ffn1_w_gate=m_ffn1_w_gate, m_ffn1_w_up=m_ffn1_w_up, m_ffn1_w_down=m_ffn1_w_down, m_mix_norm=m_mix_norm, m_w_in=m_w_in, m_s5_lambda_re=m_s5_lambda_re, m_s5_lambda_im=m_s5_lambda_im, m_s5_log_dt=m_s5_log_dt, m_s5_b_re=m_s5_b_re, m_s5_b_im=m_s5_b_im, m_s5_c_re=m_s5_c_re, m_s5_c_im=m_s5_c_im, m_s5_d=m_s5_d, m_s5_w_glu=m_s5_w_glu, m_gla_w_alpha=m_gla_w_alpha, m_gla_b_alpha=m_gla_b_alpha, m_gla_norm=m_gla_norm, m_attn_q_norm=m_attn_q_norm, m_attn_k_norm=m_attn_k_norm, m_w_branch_s5=m_w_branch_s5, m_w_branch_gla=m_w_branch_gla, m_w_branch_attn=m_w_branch_attn, m_w_merge_gate=m_w_merge_gate, m_b_merge_gate=m_b_merge_gate, m_w_out=m_w_out, m_ffn2_norm=m_ffn2_norm, m_ffn2_w_gate=m_ffn2_w_gate, m_ffn2_w_up=m_ffn2_w_up, m_ffn2_w_down=m_ffn2_w_down, m_final_norm=m_final_norm, v_ffn1_norm=v_ffn1_norm, v_ffn1_w_gate=v_ffn1_w_gate, v_ffn1_w_up=v_ffn1_w_up, v_ffn1_w_down=v_ffn1_w_down, v_mix_norm=v_mix_norm, v_w_in=v_w_in, v_s5_lambda_re=v_s5_lambda_re, v_s5_lambda_im=v_s5_lambda_im, v_s5_log_dt=v_s5_log_dt, v_s5_b_re=v_s5_b_re, v_s5_b_im=v_s5_b_im, v_s5_c_re=v_s5_c_re, v_s5_c_im=v_s5_c_im, v_s5_d=v_s5_d, v_s5_w_glu=v_s5_w_glu, v_gla_w_alpha=v_gla_w_alpha, v_gla_b_alpha=v_gla_b_alpha, v_gla_norm=v_gla_norm, v_attn_q_norm=v_attn_q_norm, v_attn_k_norm=v_attn_k_norm, v_w_branch_s5=v_w_branch_s5, v_w_branch_gla=v_w_branch_gla, v_w_branch_attn=v_w_branch_attn, v_w_merge_gate=v_w_merge_gate, v_b_merge_gate=v_b_merge_gate, v_w_out=v_w_out, v_ffn2_norm=v_ffn2_norm, v_ffn2_w_gate=v_ffn2_w_gate, v_ffn2_w_up=v_ffn2_w_up, v_ffn2_w_down=v_ffn2_w_down, v_final_norm=v_final_norm)
    weights = {n: given[n] for n in TWIN_WEIGHTS}
    shared = {n: given[n] for n in SHARED_INPUTS}
    per_example = {n: given[n] for n in ['x']}
    grad_fn = _jax.value_and_grad(_loss, argnums=(0, 1))

    def one_microbatch(ex, loss_target):
        ex = dict(ex)
        diff = ex.pop(TWIN_DIFF_INPUT)
        return grad_fn(weights, diff, {**shared, **ex}, loss_target)

    if N_MICROBATCH == 1:
        loss, (grad_w, grad_x) = one_microbatch(per_example, given["loss_target"])
    else:
        def body(carry, xs):
            loss_sum, grad_sum = carry
            l_k, (gw_k, gx_k) = one_microbatch(xs[0], xs[1])
            with _jax.named_scope("update"):
                return (loss_sum + l_k, _jax.tree.map(_jnp.add, grad_sum, gw_k)), gx_k

        init = (_jnp.zeros((), _jnp.float32), _jax.tree.map(_jnp.zeros_like, weights))
        (loss, grad_w), grad_x = _jax.lax.scan(body, init, (per_example, given["loss_target"]))
    with _jax.named_scope("update"):
        delta_w, new_m, new_v = {}, {}, {}
        for n in TWIN_WEIGHTS:
            delta_w[n], new_m[n], new_v[n] = _adamw(weights[n], grad_w[n], given["m_" + n], given["v_" + n])
    return (loss, grad_x, *[grad_w[n] for n in TWIN_WEIGHTS], *[delta_w[n] for n in TWIN_WEIGHTS],
            *[new_m[n] for n in TWIN_WEIGHTS], *[new_v[n] for n in TWIN_WEIGHTS])
```

```python
import functools
import math

import jax
import jax.numpy as jnp
import numpy as np
from jax import lax
from jax.experimental import pallas as pl
from jax.experimental.pallas import tpu as pltpu

F32 = jnp.float32
BF16 = jnp.bfloat16

N_DEV = 8
D_MODEL = 1024
DEPTH = 2
GRID_W = 64
D_FF = 2816
NORM_EPS = 1e-6
S5_GROUPS = 32
S5_GROUP_CH = 16
S5_STATE = 64
S5_WIDTH = 512
S5_NSTATE = S5_GROUPS * S5_STATE
S5_LANE_BLOCK = 512
GLA_HEADS = 4
GLA_HEAD_DIM = 128
GLA_WIDTH = 512
GLA_LOWRANK = 16
GLA_TAU = 16.0
GLA_CHUNK = 64
ATTN_Q_HEADS = 8
ATTN_KV_HEADS = 2
ATTN_HEAD_DIM = 64
ATTN_WIDTH = 512
ATTN_KV_WIDTH = 128
ROPE_BASE = 10000.0
IN_SPLITS = (512, 512, 512, 512, 512, 16, 16, 512, 128, 128)
IN_WIDTH = sum(IN_SPLITS)
IN_PAD = 3584
GATE_WIDTH = 3 * D_MODEL
PG_WIDTH = GATE_WIDTH + IN_PAD
P_OFF = GATE_WIDTH
CB_U, CB_GQ, CB_GK, CB_GV, CB_GG, CB_AQ = (P_OFF // 512 + i for i in range(6))
CB_AK, CB_AV, CB_Z = (P_OFF + 3072) // 128, (P_OFF + 3200) // 128, (P_OFF + 3328) // 128
ADAM_LR = 0.001
ADAM_B1 = 0.9
ADAM_B2 = 0.999
ADAM_EPS = 1e-08
ADAM_WD = 0.01
ADAM_STEP = 10
PACK_COLS = 1024

W_NAMES = ['ffn1_norm', 'ffn1_w_gate', 'ffn1_w_up', 'ffn1_w_down', 'mix_norm', 'w_in', 's5_lambda_re', 's5_lambda_im',
           's5_log_dt', 's5_b_re', 's5_b_im', 's5_c_re', 's5_c_im', 's5_d', 's5_w_glu', 'gla_w_alpha', 'gla_b_alpha',
           'gla_norm', 'attn_q_norm', 'attn_k_norm', 'w_branch_s5', 'w_branch_gla', 'w_branch_attn', 'w_merge_gate',
           'b_merge_gate', 'w_out', 'ffn2_norm', 'ffn2_w_gate', 'ffn2_w_up', 'ffn2_w_down', 'final_norm']
SHARD_AXIS = {'ffn1_w_gate': 2, 'ffn1_w_up': 2, 'ffn1_w_down': 1, 'w_in': 2, 's5_w_glu': 1, 'gla_w_alpha': 3,
              'gla_b_alpha': 2, 'w_branch_s5': 2, 'w_branch_gla': 2, 'w_branch_attn': 2, 'w_merge_gate': 2,
              'w_out': 1, 'ffn2_w_gate': 2, 'ffn2_w_up': 2, 'ffn2_w_down': 1}
SHARDED = [n for n in W_NAMES if n in SHARD_AXIS]
REPLICATED = [n for n in W_NAMES if n not in SHARD_AXIS]


def _pick(dim, prefs):
    for p in prefs:
        if dim % p == 0:
            return p
    return dim


def _sigmoid(x):
    return 1.0 / (1.0 + jnp.exp(-x))


def _mm(a, b, *, ta=False, tb=False, out_dtype=F32, scale=None, name):
    a, a_cb, a_w = a if isinstance(a, tuple) else (a, 0, a.shape[1])
    b, b_cb, b_w = b if isinstance(b, tuple) else (b, 0, b.shape[1])
    m, k = (a_w, a.shape[0]) if ta else (a.shape[0], a_w)
    n = b.shape[0] if tb else b_w
    assert (b_w if tb else b.shape[0]) == k, (a.shape, b.shape, ta, tb)
    tm = _pick(m, (1024, 512, 256, 128))
    tn = _pick(n, (512, 256, 128))
    tk = _pick(k, (1024, 512, 256, 128))
    nk = k // tk
    dims = (((0 if ta else 1,), (1 if tb else 0,)), ((), ()))
    a_off = a_cb * (a_w // (tm if ta else tk))
    b_off = b_cb * (b_w // (tk if tb else tn))

    def body(a_ref, b_ref, o_ref, acc_ref):
        kk = pl.program_id(2)
        part = lax.dot_general(a_ref[...].astype(BF16), b_ref[...].astype(BF16), dims, preferred_element_type=F32)

        @pl.when(kk == 0)
        def _():
            acc_ref[...] = part

        @pl.when(kk > 0)
        def _():
            acc_ref[...] += part

        @pl.when(kk == nk - 1)
        def _():
            res = acc_ref[...]
            if scale is not None:
                res = res * scale
            o_ref[...] = res.astype(out_dtype)

    a_spec = (pl.BlockSpec((tk, tm), lambda i, j, kk: (kk, i + a_off)) if ta
              else pl.BlockSpec((tm, tk), lambda i, j, kk: (i, kk + a_off)))
    b_spec = (pl.BlockSpec((tn, tk), lambda i, j, kk: (j, kk + b_off)) if tb
              else pl.BlockSpec((tk, tn), lambda i, j, kk: (kk, j + b_off)))
    return pl.pallas_call(
        body, name=name, grid=(m // tm, n // tn, nk), in_specs=[a_spec, b_spec],
        out_specs=pl.BlockSpec((tm, tn), lambda i, j, kk: (i, j)), out_shape=jax.ShapeDtypeStruct((m, n), out_dtype),
        scratch_shapes=[pltpu.VMEM((tm, tn), F32)],
        compiler_params=pltpu.CompilerParams(dimension_semantics=("parallel", "parallel", "arbitrary")),
    )(a, b)


def _rowmap(fn, rows, consts, outs, reds=(), *, tl, name):
    rows = [r if isinstance(r, tuple) else (r, 0, r.shape[1]) for r in rows]
    length = rows[0][0].shape[0]
    tl = min(tl, length)
    nr, nc, no = len(rows), len(consts), len(outs)

    def body(*refs):
        res = fn(*[r[...] for r in refs[:nr + nc]])
        res = res if isinstance(res, tuple) else (res,)
        for o_ref, val in zip(refs[nr + nc:nr + nc + no], res[:no]):
            o_ref[...] = val.astype(o_ref.dtype)
        if reds:
            step = pl.program_id(0)
            red_refs = refs[nr + nc + no:]

            @pl.when(step == 0)
            def _():
                for d_ref, val in zip(red_refs, res[no:]):
                    d_ref[...] = val.astype(F32)

            @pl.when(step > 0)
            def _():
                for d_ref, val in zip(red_refs, res[no:]):
                    d_ref[...] += val.astype(F32)

    in_specs = [pl.BlockSpec((tl, w), lambda i, cb=cb: (i, cb)) for (_, cb, w) in rows]
    in_specs += [pl.BlockSpec(c.shape, lambda i, nd=c.ndim: (0,) * nd) for c in consts]
    out_specs = [pl.BlockSpec((tl, w), lambda i: (i, 0)) for (w, _) in outs]
    out_specs += [pl.BlockSpec(s, lambda i, nd=len(s): (0,) * nd) for s in reds]
    out_shape = [jax.ShapeDtypeStruct((length, w), dt) for (w, dt) in outs]
    out_shape += [jax.ShapeDtypeStruct(s, F32) for s in reds]
    res = pl.pallas_call(
        body, name=name, grid=(length // tl,), in_specs=in_specs, out_specs=out_specs, out_shape=out_shape,
        compiler_params=pltpu.CompilerParams(dimension_semantics=("arbitrary" if reds else "parallel",)),
    )(*[r[0] for r in rows], *consts)
    return res


def _rms(x):
    return lax.rsqrt(jnp.mean(x * x, axis=-1, keepdims=True) + NORM_EPS)


def _rmsnorm_fwd(x, gain):
    def fn(xv, g):
        return xv * _rms(xv) * g
    return _rowmap(fn, [x], [gain.reshape(1, -1)], [(x.shape[1], BF16)], tl=256, name="rmsnorm_fwd")[0]


def _rmsnorm_bwd(x, gain, dh, dres):
    def fn(xv, dhv, drv, g):
        r = _rms(xv)
        gd = dhv * g
        dx = r * gd - xv * (r * r * r) * jnp.mean(xv * gd, axis=-1, keepdims=True)
        return drv + dx, jnp.sum(dhv * xv * r, axis=0, keepdims=True)
    dx, dg = _rowmap(fn, [x, dh, dres], [gain.reshape(1, -1)], [(x.shape[1], F32)], [(1, x.shape[1])], tl=256,
                     name="rmsnorm_bwd")
    return dx, dg[0]


def _ffn_fwd(x, gain, w_gu, w_down):
    h = _rmsnorm_fwd(x, gain)
    gu = _mm(h, w_gu, name="ffn_gu")

    def act(g, u):
        return g * _sigmoid(g) * u
    a = _rowmap(act, [(gu, 0, D_FF), (gu, 1, D_FF)], [], [(D_FF, BF16)], tl=256, name="ffn_act")[0]
    y = _mm(a, w_down, scale=0.5, name="ffn_down")

    def add(xv, yv):
        return xv + yv
    x_out = _rowmap(add, [x, y], [], [(D_MODEL, F32)], tl=512, name="residual_add")[0]
    return x_out, (x, h, gu, a)


def _ffn_bwd(saved, gain, w_gu, w_down, dx_out):
    x, h, gu, a = saved
    dxo = dx_out.astype(BF16)
    d_wdown = _mm(a, dxo, ta=True, scale=0.5, name="ffn_dwdown")
    da = _mm(dxo, w_down, tb=True, scale=0.5, name="ffn_da")

    def act_bwd(g, u, dav):
        s = _sigmoid(g)
        return jnp.concatenate([dav * u * (s * (1.0 + g * (1.0 - s))), dav * (g * s)], axis=1)
    dgu = _rowmap(act_bwd, [(gu, 0, D_FF), (gu, 1, D_FF), da], [], [(2 * D_FF, BF16)], tl=256, name="ffn_act_bwd")[0]
    d_wgu = _mm(h, dgu, ta=True, name="ffn_dwgu")
    dh = _mm(dgu, w_gu, tb=True, name="ffn_dh")
    dx, dgain = _rmsnorm_bwd(x, gain, dh, dx_out)
    return dx, dgain, d_wgu, d_wdown


def _s5_col(n):
    return (n // S5_LANE_BLOCK) * 2 * S5_LANE_BLOCK + n % S5_LANE_BLOCK


def _s5_blocked(re, im):
    lead = re.shape[:-1]
    nb = S5_NSTATE // S5_LANE_BLOCK
    both = jnp.stack([re.reshape(*lead, nb, S5_LANE_BLOCK), im.reshape(*lead, nb, S5_LANE_BLOCK)], axis=-2)
    return both.reshape(*lead, 2 * S5_NSTATE)


def _s5_unblocked(z):
    lead = z.shape[:-1]
    nb = S5_NSTATE // S5_LANE_BLOCK
    both = z.reshape(*lead, nb, 2, S5_LANE_BLOCK)
    return both[..., 0, :].reshape(*lead, S5_NSTATE), both[..., 1, :].reshape(*lead, S5_NSTATE)


def _s5_tables(a_re, a_im, reverse):
    a = lax.complex(a_re, a_im)
    a2 = a * a
    a4 = a2 * a2
    rows = jnp.arange(8)
    pw = [a]
    for _ in range(7):
        pw.append(pw[-1] * a)
    pw = jnp.stack(pw)
    if reverse:
        pw = pw[::-1]
    tabs = []
    for coef, s in ((a, 1), (a2, 2), (a4, 4)):
        live = (rows <= 7 - s) if reverse else (rows >= s)
        tabs.append(jnp.where(live[:, None], coef[None, :], 0.0))
    tabs.append(pw)
    tabs = jnp.stack(tabs)
    return _s5_blocked(jnp.real(tabs), jnp.imag(tabs))


def _s5_scan_tile(v, tab_ref, prev, reverse):
    lb = S5_LANE_BLOCK
    vr, vi = v[:, :lb], v[:, lb:]
    for idx, s in enumerate((1, 2, 4)):
        cr, ci = tab_ref[idx, :, :lb], tab_ref[idx, :, lb:]
        sh = 8 - s if reverse else s
        sr, si = pltpu.roll(vr, sh, 0), pltpu.roll(vi, sh, 0)
        vr, vi = vr + cr * sr - ci * si, vi + cr * si + ci * sr
    row = 0 if reverse else 7
    pr = jnp.broadcast_to(prev[row:row + 1, :lb], (8, lb))
    pi = jnp.broadcast_to(prev[row:row + 1, lb:], (8, lb))
    cr, ci = tab_ref[3, :, :lb], tab_ref[3, :, lb:]
    return jnp.concatenate([vr + cr * pr - ci * pi, vi + cr * pi + ci * pr], axis=1)


def _s5_scan(v, tabs, reverse, *, name):
    length = v.shape[0]
    tb = min(512, length)
    ntb = length // tb
    nlb = S5_NSTATE // S5_LANE_BLOCK
    wb = 2 * S5_LANE_BLOCK
    ntile = tb // 8

    def body(tab_ref, v_ref, x_ref, carry_ref):
        @pl.when(pl.program_id(1) == 0)
        def _():
            carry_ref[...] = jnp.zeros_like(carry_ref)

        def step(i, prev):
            r0 = pl.multiple_of((ntile - 1 - i if reverse else i) * 8, 8)
            x = _s5_scan_tile(v_ref[pl.ds(r0, 8), :], tab_ref, prev, reverse)
            x_ref[pl.ds(r0, 8), :] = x
            return x

        carry_ref[...] = lax.fori_loop(0, ntile, step, carry_ref[...])

    tmap = (lambda c, t: (ntb - 1 - t, c)) if reverse else (lambda c, t: (t, c))
    return pl.pallas_call(
        body, name=name, grid=(nlb, ntb),
        in_specs=[pl.BlockSpec((4, 8, wb), lambda c, t: (0, 0, c)), pl.BlockSpec((tb, wb), tmap)],
        out_specs=pl.BlockSpec((tb, wb), tmap), out_shape=jax.ShapeDtypeStruct(v.shape, F32),
        scratch_shapes=[pltpu.VMEM((8, wb), F32)],
        compiler_params=pltpu.CompilerParams(dimension_semantics=("parallel", "arbitrary")),
    )(tabs, v)


def _s5_scan_adjoint(g, xs, tabs_conj, reverse, *, name):
    length = g.shape[0]
    tb = min(512, length)
    ntb = length // tb
    nlb = S5_NSTATE // S5_LANE_BLOCK
    lb = S5_LANE_BLOCK
    wb = 2 * lb
    ntile = tb // 8
    adj_rev = not reverse
    if reverse:
        edge = jnp.concatenate([xs[tb::tb], jnp.zeros((1, xs.shape[1]), F32)], axis=0)
    else:
        edge = jnp.concatenate([jnp.zeros((1, xs.shape[1]), F32), xs[tb - 1:length - 1:tb]], axis=0)
    edge = edge.reshape(ntb, 1, xs.shape[1])

    def body(tab_ref, g_ref, x_ref, edge_ref, lam_ref, da_ref, carry_ref):
        @pl.when(pl.program_id(1) == 0)
        def _():
            carry_ref[...] = jnp.zeros_like(carry_ref)
            da_ref[...] = jnp.zeros_like(da_ref)

        rows = lax.broadcasted_iota(jnp.int32, (8, wb), 0)

        def step(i, carry):
            prev, acc = carry
            k = ntile - 1 - i if adj_rev else i
            r0 = pl.multiple_of(k * 8, 8)
            lam = _s5_scan_tile(g_ref[pl.ds(r0, 8), :], tab_ref, prev, adj_rev)
            lam_ref[pl.ds(r0, 8), :] = lam
            x = x_ref[pl.ds(r0, 8), :]
            if reverse:
                kn = jnp.minimum(k + 1, ntile - 1)
                nb = x_ref[pl.ds(pl.multiple_of(kn * 8, 8), 8), :][0:1, :]
                nb = jnp.where(k == ntile - 1, edge_ref[0], nb)
                xp = jnp.where(rows == 7, jnp.broadcast_to(nb, (8, wb)), pltpu.roll(x, 7, 0))
            else:
                kn = jnp.maximum(k - 1, 0)
                nb = x_ref[pl.ds(pl.multiple_of(kn * 8, 8), 8), :][7:8, :]
                nb = jnp.where(k == 0, edge_ref[0], nb)
                xp = jnp.where(rows == 0, jnp.broadcast_to(nb, (8, wb)), pltpu.roll(x, 1, 0))
            xr, xi, lr, li = xp[:, :lb], xp[:, lb:], lam[:, :lb], lam[:, lb:]
            acc = acc + jnp.concatenate([xr * lr + xi * li, xr * li - xi * lr], axis=1)
            return lam, acc

        last, acc = lax.fori_loop(0, ntile, step, (carry_ref[...], da_ref[...]))
        carry_ref[...] = last
        da_ref[...] = acc

    tmap = (lambda c, t: (ntb - 1 - t, c)) if adj_rev else (lambda c, t: (t, c))
    emap = (lambda c, t: (ntb - 1 - t, 0, c)) if adj_rev else (lambda c, t: (t, 0, c))
    return pl.pallas_call(
        body, name=name, grid=(nlb, ntb),
        in_specs=[pl.BlockSpec((4, 8, wb), lambda c, t: (0, 0, c)), pl.BlockSpec((tb, wb), tmap),
                  pl.BlockSpec((tb, wb), tmap), pl.BlockSpec((1, 1, wb), emap)],
        out_specs=[pl.BlockSpec((tb, wb), tmap), pl.BlockSpec((8, wb), lambda c, t: (0, c))],
        out_shape=[jax.ShapeDtypeStruct(g.shape, F32), jax.ShapeDtypeStruct((8, g.shape[1]), F32)],
        scratch_shapes=[pltpu.VMEM((8, wb), F32)],
        compiler_params=pltpu.CompilerParams(dimension_semantics=("parallel", "arbitrary")),
    )(tabs_conj, g, xs, edge)


def _s5_prep(lam_re, lam_im, log_dt, b_re, b_im):
    lam = lax.complex(lam_re, lam_im)
    dt = jnp.exp(log_dt)[:, None]
    lam_bar = jnp.exp(lam * dt)
    b_bar = ((lam_bar - 1.0) / lam)[..., None] * lax.complex(b_re, b_im)
    return (jnp.real(lam_bar).reshape(-1), jnp.imag(lam_bar).reshape(-1), jnp.real(b_bar), jnp.imag(b_bar))


def _s5_in_matrix(bb_re, bb_im):
    eye = jnp.eye(S5_GROUPS, dtype=F32)
    def dense(bb):
        return jnp.einsum('gph,gk->ghkp', bb, eye).reshape(S5_WIDTH, S5_NSTATE)
    return _s5_blocked(dense(bb_re), dense(bb_im))


def _s5_in_matrix_grad(d_mat):
    d_re, d_im = _s5_unblocked(d_mat)
    def diag(d):
        d4 = d.reshape(S5_GROUPS, S5_GROUP_CH, S5_GROUPS, S5_STATE)
        return jnp.einsum('ghgp->gph', d4)
    return diag(d_re), diag(d_im)


def _s5_out_matrix(c_re, c_im):
    eye = jnp.eye(S5_GROUPS, dtype=F32)
    def dense(c):
        return jnp.einsum('ghp,gk->gpkh', c, eye).reshape(S5_NSTATE, S5_WIDTH)
    return jnp.swapaxes(_s5_blocked(jnp.swapaxes(dense(c_re), 0, 1), jnp.swapaxes(dense(-c_im), 0, 1)), 0, 1)


def _s5_out_matrix_grad(d_mat):
    d_re, d_nim = _s5_unblocked(jnp.swapaxes(d_mat, 0, 1))
    def diag(d):
        d4 = d.reshape(S5_GROUPS, S5_GROUP_CH, S5_GROUPS, S5_STATE)
        return jnp.einsum('ghgp->ghp', d4)
    return diag(d_re), -diag(d_nim)


def _gelu_parts(x):
    k = math.sqrt(2.0 / math.pi)
    inner = k * (x + 0.044715 * x * x * x)
    th = jnp.tanh(inner)
    return th, k * (1.0 + 3.0 * 0.044715 * x * x)


def _s5_fwd(p_in, prm, w_glu):
    dirs = []
    ys = []
    for d, reverse in ((0, False), (1, True)):
        a_re, a_im, bb_re, bb_im = _s5_prep(prm['lambda_re'][d], prm['lambda_im'][d], prm['log_dt'][d],
                                            prm['b_re'][d], prm['b_im'][d])
        b_mat = _s5_in_matrix(bb_re, bb_im).astype(BF16)
        c_mat = _s5_out_matrix(prm['c_re'][d], prm['c_im'][d]).astype(BF16)
        bu = _mm((p_in, CB_U, S5_WIDTH), b_mat, name="s5_bu")
        xs = _s5_scan(bu, _s5_tables(a_re, a_im, reverse), reverse, name="s5_scan_rev" if reverse else "s5_scan_fwd")
        ys.append(_mm(xs, c_mat, name="s5_y"))
        dirs.append((a_re, a_im, b_mat, c_mat, xs))

    def post(yf, yb, u, dskip):
        ypre = yf + yb + dskip * u
        th, _ = _gelu_parts(ypre)
        return ypre, 0.5 * ypre * (1.0 + th)
    ypre, yg = _rowmap(post, [ys[0], ys[1], (p_in, CB_U, S5_WIDTH)], [prm['d'].reshape(1, -1)],
                       [(S5_WIDTH, F32), (S5_WIDTH, F32)], tl=512, name="s5_post")
    t = _mm(yg, w_glu, name="s5_glu_mm")

    def glu(ygv, tv):
        return ygv * _sigmoid(tv)
    y = _rowmap(glu, [yg, t], [], [(S5_WIDTH, BF16)], tl=512, name="s5_glu")[0]
    return y, (dirs, ypre, yg, t)


def _s5_bwd(pg, prm, w_glu, saved, dy):
    dirs, ypre, yg, t = saved

    def glu_bwd(dyv, ygv, tv):
        s = _sigmoid(tv)
        return dyv * ygv * s * (1.0 - s), dyv * s
    dt, dyg_direct = _rowmap(glu_bwd, [dy, yg, t], [], [(S5_WIDTH, BF16), (S5_WIDTH, F32)], tl=512, name="s5_glu_bwd")
    grads = {'w_glu': _mm(yg, dt, ta=True, name="s5_dwglu")}
    dyg_mm = _mm(dt, w_glu, tb=True, name="s5_dyg")

    def post_bwd(dyd, dym, yp, u, dskip):
        th, dinner = _gelu_parts(yp)
        dyp = (dyd + dym) * (0.5 * (1.0 + th) + 0.5 * yp * (1.0 - th * th) * dinner)
        return dyp, dyp * dskip, jnp.sum(dyp * u, axis=0, keepdims=True)
    dyp, du_skip, dd = _rowmap(post_bwd, [dyg_direct, dyg_mm, ypre, (pg, CB_U, S5_WIDTH)], [prm['d'].reshape(1, -1)],
                               [(S5_WIDTH, F32), (S5_WIDTH, F32)], [(1, S5_WIDTH)], tl=512, name="s5_post_bwd")
    grads['d'] = dd[0]
    du = [du_skip]
    per_dir = []
    for d, reverse in ((0, False), (1, True)):
        a_re, a_im, b_mat, c_mat, xs = dirs[d]
        dxs = _mm(dyp, c_mat, tb=True, name="s5_dxs")
        lam, da = _s5_scan_adjoint(dxs, xs, _s5_tables(a_re, -a_im, not reverse), reverse,
                                   name="s5_adjoint_rev" if reverse else "s5_adjoint_fwd")
        du.append(_mm(lam, b_mat, tb=True, name="s5_du"))
        dbb_re, dbb_im = _s5_in_matrix_grad(_mm((pg, CB_U, S5_WIDTH), lam, ta=True, name="s5_dbmat"))
        dc_re, dc_im = _s5_out_matrix_grad(_mm(xs, dyp, ta=True, name="s5_dcmat"))
        da_re, da_im = _s5_unblocked(jnp.sum(da, axis=0))
        _, vjp = jax.vjp(_s5_prep, prm['lambda_re'][d], prm['lambda_im'][d], prm['log_dt'][d], prm['b_re'][d], prm['b_im'][d])
        per_dir.append(vjp((da_re, da_im, dbb_re, dbb_im)) + (dc_re, dc_im))
    for i, key in enumerate(('lambda_re', 'lambda_im', 'log_dt', 'b_re', 'b_im', 'c_re', 'c_im')):
        grads[key] = jnp.stack([per_dir[0][i], per_dir[1][i]])
    return du, grads


def _split3(x):
    hi = x.astype(BF16)
    r = x - hi.astype(F32)
    mid = r.astype(BF16)
    return hi, mid, (r - mid.astype(F32)).astype(BF16)


def _exact_dot(ones, x, dims):
    parts = [lax.dot_general(ones, p, dims, preferred_element_type=F32) for p in _split3(x)]
    return parts[0] + parts[1] + parts[2]


_NN = (((1,), (0,)), ((), ()))
_NT = (((1,), (1,)), ((), ()))
_TN = (((0,), (0,)), ((), ()))


def _dot(a, b, dims=_NN):
    return lax.dot_general(a.astype(BF16), b.astype(BF16), dims, preferred_element_type=F32)


def _gla_chunk_mask(reverse):
    rows = lax.broadcasted_iota(jnp.int32, (GLA_CHUNK, GLA_CHUNK), 0)
    cols = lax.broadcasted_iota(jnp.int32, (GLA_CHUNK, GLA_CHUNK), 1)
    return (cols >= rows) if reverse else (cols <= rows)


def _gla_fwd(pg, la, reverse, *, name):
    length = la.shape[0]
    nch = length // GLA_CHUNK
    scale = GLA_HEAD_DIM ** -0.5
    last = 0 if reverse else GLA_CHUNK - 1
    hd = GLA_HEAD_DIM

    def body(q_ref, k_ref, v_ref, la_ref, o_ref, sp_ref, st_ref):
        @pl.when(pl.program_id(0) == 0)
        def _():
            st_ref[...] = jnp.zeros_like(st_ref)

        mask = _gla_chunk_mask(reverse)
        b = _exact_dot(mask.astype(BF16), la_ref[...], _NN)
        sp_ref[0] = st_ref[...]
        outs = []
        for h in range(GLA_HEADS):
            sl = slice(h * hd, (h + 1) * hd)
            bh = b[:, sl]
            bl = bh[last:last + 1, :]
            k = k_ref[:, sl]
            v = v_ref[:, sl]
            qd = q_ref[:, sl] * scale * jnp.exp(bh)
            kd = k * jnp.exp(-bh)
            ke = k * jnp.exp(bl - bh)
            st = st_ref[sl, :]
            p = jnp.where(mask, _dot(qd, kd, _NT), 0.0)
            outs.append(_dot(p, v) + _dot(qd, st, _NT))
            st_ref[sl, :] = st * jnp.exp(bl) + _dot(v, ke, _TN)
        o_ref[...] = jnp.concatenate(outs, axis=1)

    cmap = (lambda n: nch - 1 - n) if reverse else (lambda n: n)
    col = lambda cb: pl.BlockSpec((GLA_CHUNK, GLA_WIDTH), lambda n, cb=cb: (cmap(n), cb))
    return pl.pallas_call(
        body, name=name, grid=(nch,),
        in_specs=[col(CB_GQ), col(CB_GK), col(CB_GV), col(0)],
        out_specs=[col(0), pl.BlockSpec((1, GLA_WIDTH, hd), lambda n: (cmap(n), 0, 0))],
        out_shape=[jax.ShapeDtypeStruct((length, GLA_WIDTH), F32), jax.ShapeDtypeStruct((nch, GLA_WIDTH, hd), F32)],
        scratch_shapes=[pltpu.VMEM((GLA_WIDTH, hd), F32)],
        compiler_params=pltpu.CompilerParams(dimension_semantics=("arbitrary",)),
    )(pg, pg, pg, la)


def _gla_bwd(pg, la, do, sprev, reverse, *, name):
    length = la.shape[0]
    nch = length // GLA_CHUNK
    scale = GLA_HEAD_DIM ** -0.5
    last = 0 if reverse else GLA_CHUNK - 1
    hd = GLA_HEAD_DIM

    def body(q_ref, k_ref, v_ref, la_ref, do_ref, sp_ref, dq_ref, dk_ref, dv_ref, dla_ref, dst_ref):
        @pl.when(pl.program_id(0) == 0)
        def _():
            dst_ref[...] = jnp.zeros_like(dst_ref)

        mask = _gla_chunk_mask(reverse)
        tri = mask.astype(BF16)
        b = _exact_dot(tri, la_ref[...], _NN)
        is_last = lax.broadcasted_iota(jnp.int32, (GLA_CHUNK, hd), 0) == last
        dqs, dks, dvs, dbs = [], [], [], []
        for h in range(GLA_HEADS):
            sl = slice(h * hd, (h + 1) * hd)
            bh = b[:, sl]
            bl = bh[last:last + 1, :]
            eb, enb, ebl, el = jnp.exp(bh), jnp.exp(-bh), jnp.exp(bl - bh), jnp.exp(bl)
            k = k_ref[:, sl]
            v = v_ref[:, sl]
            dov = do_ref[:, sl]
            qd = q_ref[:, sl] * scale * eb
            kd = k * enb
            ke = k * ebl
            st = sp_ref[0, sl, :]
            dst = dst_ref[sl, :]
            p = jnp.where(mask, _dot(qd, kd, _NT), 0.0)
            dp = jnp.where(mask, _dot(dov, v, _NT), 0.0)
            dqd = _dot(dp, kd) + _dot(dov, st)
            dkd = _dot(dp, qd, _TN)
            dvs.append(_dot(p, dov, _TN) + _dot(ke, dst, _NT))
            dke = _dot(v, dst)
            dst_ref[sl, :] = dst * el + _dot(dov, qd, _TN)
            dbl = el * jnp.sum(dst * st, axis=0, keepdims=True) + jnp.sum(dke * ke, axis=0, keepdims=True)
            db = dqd * qd - dkd * kd - dke * ke
            dbs.append(jnp.where(is_last, db + dbl, db))
            dqs.append(dqd * eb * scale)
            dks.append(dkd * enb + dke * ebl)
        dq_ref[...] = jnp.concatenate(dqs, axis=1)
        dk_ref[...] = jnp.concatenate(dks, axis=1)
        dv_ref[...] = jnp.concatenate(dvs, axis=1)
        tri_t = _gla_chunk_mask(not reverse).astype(BF16)
        dla_ref[...] = _exact_dot(tri_t, jnp.concatenate(dbs, axis=1), _NN)

    cmap = (lambda n: n) if reverse else (lambda n: nch - 1 - n)
    col = lambda cb: pl.BlockSpec((GLA_CHUNK, GLA_WIDTH), lambda n, cb=cb: (cmap(n), cb))
    wide = jax.ShapeDtypeStruct((length, GLA_WIDTH), F32)
    return pl.pallas_call(
        body, name=name, grid=(nch,),
        in_specs=[col(CB_GQ), col(CB_GK), col(CB_GV), col(0), col(0),
                  pl.BlockSpec((1, GLA_WIDTH, hd), lambda n: (cmap(n), 0, 0))],
        out_specs=[col(0)] * 4, out_shape=[wide] * 4,
        scratch_shapes=[pltpu.VMEM((GLA_WIDTH, hd), F32)],
        compiler_params=pltpu.CompilerParams(dimension_semantics=("arbitrary",)),
    )(pg, pg, pg, la, do, sprev)


def _log_sigmoid(x):
    return jnp.minimum(x, 0.0) - jnp.log(1.0 + jnp.exp(-jnp.abs(x)))


def _gla_alpha_padded(w_alpha):
    w = jnp.zeros((2, 128, GLA_WIDTH), w_alpha.dtype)
    w = w.at[0, 0:GLA_LOWRANK].set(w_alpha[0])
    return w.at[1, GLA_LOWRANK:2 * GLA_LOWRANK].set(w_alpha[1])


def _gla_branch_fwd(pg, w_alpha, b_alpha, norm_gain):
    wa = _gla_alpha_padded(w_alpha).astype(BF16)

    def gates(z, w, bias):
        return (_log_sigmoid(_dot(z, w[0]) + bias[0:1]) / GLA_TAU, _log_sigmoid(_dot(z, w[1]) + bias[1:2]) / GLA_TAU)
    la_f, la_b = _rowmap(gates, [(pg, CB_Z, 128)], [wa, b_alpha], [(GLA_WIDTH, F32), (GLA_WIDTH, F32)], tl=512,
                         name="gla_gates")
    o_f, sp_f = _gla_fwd(pg, la_f, False, name="gla_fwd")
    o_b, sp_b = _gla_fwd(pg, la_b, True, name="gla_fwd_rev")

    def post(of, ob, gate, gn):
        o = of + ob
        on = jnp.concatenate([o[:, s:s + GLA_HEAD_DIM] * _rms(o[:, s:s + GLA_HEAD_DIM]) * gn
                              for s in range(0, GLA_WIDTH, GLA_HEAD_DIM)], axis=1)
        return o, on * (gate * _sigmoid(gate))
    o, y = _rowmap(post, [o_f, o_b, (pg, CB_GG, GLA_WIDTH)], [norm_gain.reshape(1, -1)],
                   [(GLA_WIDTH, F32), (GLA_WIDTH, BF16)], tl=512, name="gla_post")
    return y, (wa, la_f, la_b, sp_f, sp_b, o)


def _gla_branch_bwd(pg, w_alpha, b_alpha, norm_gain, saved, dy):
    wa, la_f, la_b, sp_f, sp_b, o = saved

    def post_bwd(dyv, ov, gate, gn):
        s = _sigmoid(gate)
        dos, dgn, ons = [], [], []
        for c in range(0, GLA_WIDTH, GLA_HEAD_DIM):
            oh = ov[:, c:c + GLA_HEAD_DIM]
            r = _rms(oh)
            don = dyv[:, c:c + GLA_HEAD_DIM] * (gate[:, c:c + GLA_HEAD_DIM] * s[:, c:c + GLA_HEAD_DIM])
            gd = don * gn
            dos.append(r * gd - oh * (r * r * r) * jnp.mean(oh * gd, axis=-1, keepdims=True))
            dgn.append(jnp.sum(don * oh * r, axis=0, keepdims=True))
            ons.append(oh * r * gn)
        on = jnp.concatenate(ons, axis=1)
        dgate = dyv * on * (s * (1.0 + gate * (1.0 - s)))
        return jnp.concatenate(dos, axis=1), dgate, jnp.concatenate(dgn, axis=1)
    do, dgate, dgn = _rowmap(post_bwd, [dy, o, (pg, CB_GG, GLA_WIDTH)], [norm_gain.reshape(1, -1)],
                             [(GLA_WIDTH, F32), (GLA_WIDTH, F32)], [(1, GLA_WIDTH)], tl=512, name="gla_post_bwd")
    dq_f, dk_f, dv_f, dla_f = _gla_bwd(pg, la_f, do, sp_f, False, name="gla_bwd")
    dq_b, dk_b, dv_b, dla_b = _gla_bwd(pg, la_b, do, sp_b, True, name="gla_bwd_rev")

    def gates_bwd(z, dlf, dlb, w, bias):
        dz = jnp.zeros_like(z)
        dlogits, dbs = [], []
        for d, dl in ((0, dlf), (1, dlb)):
            logit = _dot(z, w[d]) + bias[d:d + 1]
            dlogit = dl * (1.0 / GLA_TAU) * _sigmoid(-logit)
            dz = dz + _dot(dlogit, w[d], _NT)
            dlogits.append(dlogit)
            dbs.append(jnp.sum(dlogit, axis=0, keepdims=True))
        return dz, dlogits[0], dlogits[1], dbs[0], dbs[1]
    dz, dlg_f, dlg_b, dba_f, dba_b = _rowmap(
        gates_bwd, [(pg, CB_Z, 128), dla_f, dla_b], [wa, b_alpha], [(128, F32), (GLA_WIDTH, BF16), (GLA_WIDTH, BF16)],
        [(1, GLA_WIDTH), (1, GLA_WIDTH)], tl=512, name="gla_gates_bwd")
    dwa_f = _mm(dlg_f, (pg, CB_Z, 128), ta=True, name="gla_dwalpha")
    dwa_b = _mm(dlg_b, (pg, CB_Z, 128), ta=True, name="gla_dwalpha")
    grads = {'w_alpha': jnp.stack([dwa_f[:, 0:GLA_LOWRANK].T, dwa_b[:, GLA_LOWRANK:2 * GLA_LOWRANK].T]),
             'b_alpha': jnp.concatenate([dba_f, dba_b], axis=0),
             'norm': jnp.sum(dgn.reshape(GLA_HEADS, GLA_HEAD_DIM), axis=0)}
    return [dq_f, dq_b], [dk_f, dk_b], [dv_f, dv_b], dgate, dz, grads


def _rope_tables(length):
    half = ATTN_HEAD_DIM // 2
    inv_freq = ROPE_BASE ** (-jnp.arange(half // 2, dtype=F32) * 2.0 / half)
    t = jnp.arange(length, dtype=jnp.int32)
    def one(pos):
        ang = pos.astype(F32)[:, None] * inv_freq[None, :]
        c, s = jnp.cos(ang), jnp.sin(ang)
        return jnp.concatenate([c, c], axis=1), jnp.concatenate([-s, s], axis=1)
    c_r, s_r = one(t // GRID_W)
    c_c, s_c = one(t % GRID_W)
    return jnp.concatenate([c_r, c_c], axis=1), jnp.concatenate([s_r, s_c], axis=1)


def _rope_swap(y):
    w = y.shape[1]
    lane = lax.broadcasted_iota(jnp.int32, y.shape, 1)
    return jnp.where(lane % 32 < 16, pltpu.roll(y, w - 16, 1), pltpu.roll(y, 16, 1))


def _head_sums(x, ones):
    parts = [lax.dot_general(p, ones, _NN, preferred_element_type=F32) for p in _split3(x)]
    return parts[0] + parts[1] + parts[2]


def _head_ones(width):
    seg = np.arange(width) // ATTN_HEAD_DIM
    return jnp.asarray(seg[:, None] == seg[None, :], BF16)


def _qk_prep_fwd(pg, cb, width, gain, cos, sin, scale, *, name):
    heads = width // ATTN_HEAD_DIM
    def fn(x, c, s, g, ones):
        r = lax.rsqrt(_head_sums(x * x, ones) * (1.0 / ATTN_HEAD_DIM) + NORM_EPS)
        y = x * r * g
        return (y * c + _rope_swap(y) * s) * scale
    return _rowmap(fn, [(pg, cb, width), jnp.tile(cos, (1, heads)), jnp.tile(sin, (1, heads))],
                   [jnp.tile(gain, heads).reshape(1, -1), _head_ones(width)], [(width, BF16)], tl=512, name=name)[0]


def _qk_prep_bwd(pg, cb, width, gain, cos, sin, scale, dout, *, name):
    heads = width // ATTN_HEAD_DIM
    def fn(x, dov, c, s, g, ones):
        r = lax.rsqrt(_head_sums(x * x, ones) * (1.0 / ATTN_HEAD_DIM) + NORM_EPS)
        dos = dov * scale
        dy = dos * c + _rope_swap(dos * s)
        gd = dy * g
        dx = r * gd - x * (r * r * r) * (_head_sums(x * gd, ones) * (1.0 / ATTN_HEAD_DIM))
        return dx, jnp.sum(dy * x * r, axis=0, keepdims=True)
    dx, dg = _rowmap(fn, [(pg, cb, width), dout, jnp.tile(cos, (1, heads)), jnp.tile(sin, (1, heads))],
                     [jnp.tile(gain, heads).reshape(1, -1), _head_ones(width)], [(width, F32)], [(1, width)], tl=512,
                     name=name)
    return dx, jnp.sum(dg.reshape(heads, ATTN_HEAD_DIM), axis=0)


def _to_heads(x, heads):
    return jnp.transpose(x.reshape(x.shape[0], heads, ATTN_HEAD_DIM), (1, 0, 2))


def _from_heads(x):
    return jnp.transpose(x, (1, 0, 2)).reshape(x.shape[1], x.shape[0] * ATTN_HEAD_DIM)


ATTN_GROUP = ATTN_Q_HEADS // ATTN_KV_HEADS
ATTN_TQ = 256


def _attn_fwd(q, k, v):
    length = q.shape[1]
    tq = min(ATTN_TQ, length)

    def body(q_ref, k_ref, v_ref, o_ref):
        kk, vv = k_ref[0], v_ref[0]
        for g in range(ATTN_GROUP):
            s = _dot(q_ref[g], kk, _NT)
            p = jnp.exp(s - jnp.max(s, axis=-1, keepdims=True))
            o_ref[g] = _dot(p, vv) / jnp.sum(p, axis=-1, keepdims=True)

    kv_spec = pl.BlockSpec((1, length, ATTN_HEAD_DIM), lambda h, i: (h, 0, 0))
    q_spec = pl.BlockSpec((ATTN_GROUP, tq, ATTN_HEAD_DIM), lambda h, i: (h, i, 0))
    return pl.pallas_call(
        body, name="attn_fwd", grid=(ATTN_KV_HEADS, length // tq), in_specs=[q_spec, kv_spec, kv_spec],
        out_specs=q_spec, out_shape=jax.ShapeDtypeStruct(q.shape, F32),
        compiler_params=pltpu.CompilerParams(dimension_semantics=("parallel", "parallel")),
    )(q, k, v)


def _attn_bwd(q, k, v, o, do):
    length = q.shape[1]
    tq = min(ATTN_TQ, length)

    def body(q_ref, k_ref, v_ref, o_ref, do_ref, dq_ref, dk_ref, dv_ref):
        @pl.when(pl.program_id(1) == 0)
        def _():
            dk_ref[...] = jnp.zeros_like(dk_ref)
            dv_ref[...] = jnp.zeros_like(dv_ref)

        kk, vv = k_ref[0], v_ref[0]
        for g in range(ATTN_GROUP):
            qg, dog = q_ref[g], do_ref[g]
            s = _dot(qg, kk, _NT)
            p = jnp.exp(s - jnp.max(s, axis=-1, keepdims=True))
            p = p / jnp.sum(p, axis=-1, keepdims=True)
            dp = _dot(dog, vv, _NT)
            ds = p * (dp - jnp.sum(dog * o_ref[g], axis=-1, keepdims=True))
            dq_ref[g] = _dot(ds, kk)
            dk_ref[0] += _dot(ds, qg, _TN)
            dv_ref[0] += _dot(p, dog, _TN)

    kv_spec = pl.BlockSpec((1, length, ATTN_HEAD_DIM), lambda h, i: (h, 0, 0))
    q_spec = pl.BlockSpec((ATTN_GROUP, tq, ATTN_HEAD_DIM), lambda h, i: (h, i, 0))
    return pl.pallas_call(
        body, name="attn_bwd", grid=(ATTN_KV_HEADS, length // tq),
        in_specs=[q_spec, kv_spec, kv_spec, q_spec, q_spec], out_specs=[q_spec, kv_spec, kv_spec],
        out_shape=[jax.ShapeDtypeStruct(q.shape, F32), jax.ShapeDtypeStruct(k.shape, F32),
                   jax.ShapeDtypeStruct(k.shape, F32)],
        compiler_params=pltpu.CompilerParams(dimension_semantics=("parallel", "arbitrary")),
    )(q, k, v, o, do)


def _attn_branch_fwd(pg, q_gain, k_gain):
    cos, sin = _rope_tables(pg.shape[0])
    qp = _qk_prep_fwd(pg, CB_AQ, ATTN_WIDTH, q_gain, cos, sin, ATTN_HEAD_DIM ** -0.5, name="attn_q_prep")
    kp = _qk_prep_fwd(pg, CB_AK, ATTN_KV_WIDTH, k_gain, cos, sin, 1.0, name="attn_k_prep")
    qh, kh = _to_heads(qp, ATTN_Q_HEADS), _to_heads(kp, ATTN_KV_HEADS)
    vh = _to_heads(pg[:, P_OFF + 3200:P_OFF + 3328].astype(BF16), ATTN_KV_HEADS)
    oh = _attn_fwd(qh, kh, vh)
    return _from_heads(oh).astype(BF16), (cos, sin, qh, kh, vh, oh)


def _attn_branch_bwd(pg, q_gain, k_gain, saved, dy):
    cos, sin, qh, kh, vh, oh = saved
    dqh, dkh, dvh = _attn_bwd(qh, kh, vh, oh, _to_heads(dy, ATTN_Q_HEADS))
    dq, dqg = _qk_prep_bwd(pg, CB_AQ, ATTN_WIDTH, q_gain, cos, sin, ATTN_HEAD_DIM ** -0.5, _from_heads(dqh),
                           name="attn_q_prep_bwd")
    dk, dkg = _qk_prep_bwd(pg, CB_AK, ATTN_KV_WIDTH, k_gain, cos, sin, 1.0, _from_heads(dkh), name="attn_k_prep_bwd")
    return dq, dk, _from_heads(dvh), {'q_norm': dqg, 'k_norm': dkg}


def _gate_cols():
    return [slice(i * D_MODEL, (i + 1) * D_MODEL) for i in range(3)]


def _mixer_fwd(x, lw):
    h = _rmsnorm_fwd(x, lw['mix_norm'])
    pg = _mm(h, lw['w_pg'], name="mix_in")
    y_s5, s_s5 = _s5_fwd(pg, lw['s5'], lw['s5_w_glu'])
    y_gla, s_gla = _gla_branch_fwd(pg, lw['gla_w_alpha'], lw['gla_b_alpha'], lw['gla_norm'])
    y_att, s_att = _attn_branch_fwd(pg, lw['attn_q_norm'], lw['attn_k_norm'])
    ys = (y_s5, y_gla, y_att)
    br = [_mm(y, lw[n], name="mix_branch") for y, n in zip(ys, ('w_branch_s5', 'w_branch_gla', 'w_branch_attn'))]

    def merge(g0, g1, g2, b0, b1, b2, bias):
        acc = None
        for g, b, c in zip((g0, g1, g2), (b0, b1, b2), _gate_cols()):
            term = _sigmoid(g + bias[:, c]) * b
            acc = term if acc is None else acc + term
        return acc
    merged = _rowmap(merge, [(pg, 0, D_MODEL), (pg, 1, D_MODEL), (pg, 2, D_MODEL)] + br,
                     [lw['b_merge_gate'].reshape(1, -1)], [(D_MODEL, BF16)], tl=256, name="mix_merge")[0]
    y = _mm(merged, lw['w_out'], name="mix_out")
    x_out = _rowmap(lambda a, b: a + b, [x, y], [], [(D_MODEL, F32)], tl=512, name="residual_add")[0]
    return x_out, (x, h, pg, ys, (s_s5, s_gla, s_att), br, merged)


def _mixer_bwd(saved, lw, dx_out):
    x, h, pg, ys, (s_s5, s_gla, s_att), br, merged = saved
    grads = {'w_out': _mm(merged, dx_out, ta=True, name="mix_dwout")}
    dmerged = _mm(dx_out, lw['w_out'], tb=True, name="mix_dmerged")

    def merge_bwd(g0, g1, g2, b0, b1, b2, dm, bias):
        dbr, dgp = [], []
        for g, b, c in zip((g0, g1, g2), (b0, b1, b2), _gate_cols()):
            s = _sigmoid(g + bias[:, c])
            dbr.append(dm * s)
            dgp.append(dm * b * (s * (1.0 - s)))
        dgp = jnp.concatenate(dgp, axis=1)
        return dbr[0], dbr[1], dbr[2], dgp, jnp.sum(dgp, axis=0, keepdims=True)
    d0, d1, d2, dgpre, dbias = _rowmap(
        merge_bwd, [(pg, 0, D_MODEL), (pg, 1, D_MODEL), (pg, 2, D_MODEL)] + br + [dmerged],
        [lw['b_merge_gate'].reshape(1, -1)], [(D_MODEL, BF16)] * 3 + [(GATE_WIDTH, BF16)], [(1, GATE_WIDTH)], tl=256,
        name="mix_merge_bwd")
    grads['b_merge_gate'] = dbias[0]
    dys = []
    for y, dbr, n in zip(ys, (d0, d1, d2), ('w_branch_s5', 'w_branch_gla', 'w_branch_attn')):
        grads[n] = _mm(y, dbr, ta=True, name="mix_dwbranch")
        dys.append(_mm(dbr, lw[n], tb=True, name="mix_dy"))
    du, g_s5 = _s5_bwd(pg, lw['s5'], lw['s5_w_glu'], s_s5, dys[0])
    dgq, dgk, dgv, dgg, dz, g_gla = _gla_branch_bwd(pg, lw['gla_w_alpha'], lw['gla_b_alpha'], lw['gla_norm'], s_gla, dys[1])
    daq, dak, dav, g_att = _attn_branch_bwd(pg, lw['attn_q_norm'], lw['attn_k_norm'], s_att, dys[2])

    def assemble(dgp, u0, u1, u2, q0, q1, k0, k1, v0, v1, gg, aq, ak, av, z):
        pad = jnp.zeros((dgp.shape[0], IN_PAD - 3456), F32)
        parts = [dgp.astype(F32), u0 + u1 + u2, q0 + q1, k0 + k1, v0 + v1, gg, aq, ak, av, z, pad]
        return jnp.concatenate(parts, axis=1)
    dpg = _rowmap(assemble, [dgpre] + du + dgq + dgk + dgv + [dgg, daq, dak, dav, dz], [], [(PG_WIDTH, BF16)], tl=256,
                  name="mix_dpg")[0]
    grads['w_pg'] = _mm(h, dpg, ta=True, name="mix_dwpg")
    dh = _mm(dpg, lw['w_pg'], tb=True, name="mix_dh")
    dx, grads['mix_norm'] = _rmsnorm_bwd(x, lw['mix_norm'], dh, dx_out)
    grads['s5'], grads['gla'], grads['attn'] = g_s5, g_gla, g_att
    return dx, grads


def _loss_head(x, gain, target):
    width = x.shape[1]

    def fn(xv, tv, g):
        r = _rms(xv)
        err = xv * r * g - tv
        dy = err * (1.0 / width)
        gd = dy * g
        dx = r * gd - xv * (r * r * r) * jnp.mean(xv * gd, axis=-1, keepdims=True)
        loss = jnp.sum(0.5 * jnp.mean(err * err, axis=-1, keepdims=True), axis=0, keepdims=True)
        return dx, jnp.broadcast_to(loss, (1, 128)), jnp.sum(dy * xv * r, axis=0, keepdims=True)
    dx, loss, dgain = _rowmap(fn, [x, target], [gain.reshape(1, -1)], [(width, F32)], [(1, 128), (1, width)], tl=256,
                              name="loss_head")
    return loss[0, 0], dx, dgain[0]


def _adamw(w, g, m, v):
    def fn(wv, gv, mv, vv):
        m2 = ADAM_B1 * mv + (1.0 - ADAM_B1) * gv
        v2 = ADAM_B2 * vv + (1.0 - ADAM_B2) * (gv * gv)
        m_hat = m2 / (1.0 - ADAM_B1 ** ADAM_STEP)
        v_hat = v2 / (1.0 - ADAM_B2 ** ADAM_STEP)
        return -ADAM_LR * (m_hat / (jnp.sqrt(v_hat) + ADAM_EPS) + ADAM_WD * wv), m2, v2
    return _rowmap(fn, [w, g, m, v], [], [(w.shape[1], F32)] * 3, tl=256, name="adamw")


def _all_gather(block, *, name):
    def body(x_ref, out_ref, send_sems, recv_sems, local_sem):
        x, y, c = lax.axis_index("x"), lax.axis_index("y"), lax.axis_index("c")
        me, sibling = (x, y, c), (x, y, 1 - c)
        chips = [(1 - x, y), (x, 1 - y), (1 - x, 1 - y)]

        def slot(px, py, pc):
            return out_ref.at[4 * px + 2 * py + pc]

        def copy(k, blk, to, src=None):
            return pltpu.make_async_remote_copy(
                src_ref=slot(*blk) if src is None else src, dst_ref=slot(*blk), send_sem=send_sems.at[k],
                recv_sem=recv_sems.at[k], device_id=to, device_id_type=pl.DeviceIdType.MESH)

        mine = pltpu.make_async_copy(x_ref, slot(*me), local_sem)
        mine.start()
        first = [copy(0, me, sibling, src=x_ref)]
        first += [copy(1 + j, me, (*chip, c), src=x_ref) for j, chip in enumerate(chips)]
        for cp in first:
            cp.start()
        passed = [copy(4 + j, (*chip, c), sibling) for j, chip in enumerate(chips)]
        for j, chip in enumerate(chips):
            copy(1 + j, (*chip, c), me).wait_recv()
            passed[j].start()
        copy(0, sibling, me).wait_recv()
        for j, chip in enumerate(chips):
            copy(4 + j, (*chip, 1 - c), me).wait_recv()
        for cp in first + passed:
            cp.wait_send()
        mine.wait()

    return pl.pallas_call(
        body, name=name, out_shape=jax.ShapeDtypeStruct((N_DEV,) + block.shape, block.dtype),
        in_specs=[pl.BlockSpec(memory_space=pl.ANY)], out_specs=pl.BlockSpec(memory_space=pl.ANY),
        scratch_shapes=[pltpu.SemaphoreType.DMA((7,)), pltpu.SemaphoreType.DMA((7,)), pltpu.SemaphoreType.DMA(())],
    )(block)


def _all_to_all(blocks, *, name):
    def body(g_ref, out_ref, send_sems, recv_sems, local_sem):
        x, y, c = lax.axis_index("x"), lax.axis_index("y"), lax.axis_index("c")
        me = 4 * x + 2 * y + c
        mine = pltpu.make_async_copy(g_ref.at[me], out_ref.at[me], local_sem)
        mine.start()
        copies = []
        for k in range(1, N_DEV):
            px, py, pc = x ^ (k >> 2 & 1), y ^ (k >> 1 & 1), c ^ (k & 1)
            copies.append(pltpu.make_async_remote_copy(
                src_ref=g_ref.at[4 * px + 2 * py + pc], dst_ref=out_ref.at[me], send_sem=send_sems.at[k - 1],
                recv_sem=recv_sems.at[k - 1], device_id=(px, py, pc), device_id_type=pl.DeviceIdType.MESH))
        for cp in copies:
            cp.start()
        for cp in copies:
            cp.wait_recv()
        for cp in copies:
            cp.wait_send()
        mine.wait()

    return pl.pallas_call(
        body, name=name, out_shape=jax.ShapeDtypeStruct(blocks.shape, blocks.dtype),
        in_specs=[pl.BlockSpec(memory_space=pl.ANY)], out_specs=pl.BlockSpec(memory_space=pl.ANY),
        scratch_shapes=[pltpu.SemaphoreType.DMA((7,)), pltpu.SemaphoreType.DMA((7,)), pltpu.SemaphoreType.DMA(())],
    )(blocks)


def _sum_slots(stack, *, name):
    _, r, c = stack.shape
    tr = _pick(r, (256, 128, 64, 32, 16))

    def body(s_ref, o_ref):
        acc = s_ref[0].astype(F32)
        for j in range(1, N_DEV):
            acc = acc + s_ref[j].astype(F32)
        o_ref[...] = acc

    return pl.pallas_call(
        body, name=name, grid=(r // tr,), in_specs=[pl.BlockSpec((N_DEV, tr, c), lambda i: (0, i, 0))],
        out_specs=pl.BlockSpec((tr, c), lambda i: (i, 0)), out_shape=jax.ShapeDtypeStruct((r, c), F32),
        compiler_params=pltpu.CompilerParams(dimension_semantics=("parallel",)),
    )(stack)


def _pack_rows(total):
    return -(-total // (PACK_COLS * 256)) * 256


def _pack(arrays, dtype):
    flat = jnp.concatenate([a.reshape(-1).astype(dtype) for a in arrays])
    rows = _pack_rows(flat.shape[0])
    return jnp.pad(flat, (0, rows * PACK_COLS - flat.shape[0])).reshape(rows, PACK_COLS)


def _unpack(packed, shapes):
    flat = packed.reshape(-1)
    out, off = [], 0
    for s in shapes:
        n = math.prod(s)
        out.append(flat[off:off + n].reshape(s))
        off += n
    return out


def _split_shards(full, axis):
    shape = full.shape
    split = full.reshape(shape[:axis] + (N_DEV, shape[axis] // N_DEV) + shape[axis + 1:])
    return jnp.moveaxis(split, axis, 0)


def _join_shards(stack, axis):
    moved = jnp.moveaxis(stack, 0, axis)
    shape = moved.shape
    return moved.reshape(shape[:axis] + (shape[axis] * shape[axis + 1],) + shape[axis + 2:])


def _w_in_padded(w_in):
    pad = jnp.zeros(w_in.shape[:-1] + (IN_PAD - IN_WIDTH,), w_in.dtype)
    return jnp.concatenate([w_in[..., :2560], w_in[..., 2592:], w_in[..., 2560:2592], pad], axis=-1)


def _w_in_unpadded(w):
    return jnp.concatenate([w[..., :2560], w[..., 3328:3360], w[..., 2560:3328]], axis=-1)


S5_KEYS = ('lambda_re', 'lambda_im', 'log_dt', 'b_re', 'b_im', 'c_re', 'c_im', 'd')


def _layer_weights(full, w, i):
    lw = {n: w[n][i] for n in ('ffn1_norm', 'mix_norm', 'gla_norm', 'attn_q_norm', 'attn_k_norm', 'b_merge_gate', 'ffn2_norm')}
    lw['s5'] = {k: w['s5_' + k][i] for k in S5_KEYS}
    for f in ('ffn1', 'ffn2'):
        lw[f + '_w_gu'] = jnp.concatenate([full[f + '_w_gate'][i], full[f + '_w_up'][i]], axis=1)
        lw[f + '_w_down'] = full[f + '_w_down'][i]
    lw['w_pg'] = jnp.concatenate([full['w_merge_gate'][i], _w_in_padded(full['w_in'][i])], axis=1)
    for n in ('s5_w_glu', 'gla_w_alpha', 'w_branch_s5', 'w_branch_gla', 'w_branch_attn', 'w_out'):
        lw[n] = full[n][i]
    lw['gla_b_alpha'] = full['gla_b_alpha'][i].astype(F32)
    return lw


def _step_local(x, target, w, full):
    lws = [_layer_weights(full, w, i) for i in range(DEPTH)]
    saved = []
    for lw in lws:
        x, s1 = _ffn_fwd(x, lw['ffn1_norm'], lw['ffn1_w_gu'], lw['ffn1_w_down'])
        x, s2 = _mixer_fwd(x, lw)
        x, s3 = _ffn_fwd(x, lw['ffn2_norm'], lw['ffn2_w_gu'], lw['ffn2_w_down'])
        saved.append((s1, s2, s3))
    loss, dx, d_final = _loss_head(x, w['final_norm'], target)
    per_layer = []
    for lw, (s1, s2, s3) in reversed(list(zip(lws, saved))):
        g = {}
        dx, g['ffn2_norm'], dgu, g['ffn2_w_down'] = _ffn_bwd(s3, lw['ffn2_norm'], lw['ffn2_w_gu'], lw['ffn2_w_down'], dx)
        g['ffn2_w_gate'], g['ffn2_w_up'] = dgu[:, :D_FF], dgu[:, D_FF:]
        dx, gm = _mixer_bwd(s2, lw, dx)
        dx, g['ffn1_norm'], dgu, g['ffn1_w_down'] = _ffn_bwd(s1, lw['ffn1_norm'], lw['ffn1_w_gu'], lw['ffn1_w_down'], dx)
        g['ffn1_w_gate'], g['ffn1_w_up'] = dgu[:, :D_FF], dgu[:, D_FF:]
        g['w_merge_gate'] = gm['w_pg'][:, :GATE_WIDTH]
        g['w_in'] = _w_in_unpadded(gm['w_pg'][:, GATE_WIDTH:])
        for n in ('w_out', 'b_merge_gate', 'w_branch_s5', 'w_branch_gla', 'w_branch_attn', 'mix_norm'):
            g[n] = gm[n]
        for k in S5_KEYS:
            g['s5_' + k] = gm['s5'][k]
        g['s5_w_glu'] = gm['s5']['w_glu']
        g['gla_w_alpha'], g['gla_b_alpha'], g['gla_norm'] = gm['gla']['w_alpha'], gm['gla']['b_alpha'], gm['gla']['norm']
        g['attn_q_norm'], g['attn_k_norm'] = gm['attn']['q_norm'], gm['attn']['k_norm']
        per_layer.append(g)
    per_layer.reverse()
    grads = {n: jnp.stack([per_layer[i][n] for i in range(DEPTH)]) for n in W_NAMES if n != 'final_norm'}
    grads['final_norm'] = d_final
    return loss, dx, grads


def kernel(x, ffn1_norm, ffn1_w_gate, ffn1_w_up, ffn1_w_down, mix_norm, w_in, s5_lambda_re, s5_lambda_im, s5_log_dt, s5_b_re, s5_b_im, s5_c_re, s5_c_im, s5_d, s5_w_glu, gla_w_alpha, gla_b_alpha, gla_norm, attn_q_norm, attn_k_norm, w_branch_s5, w_branch_gla, w_branch_attn, w_merge_gate, b_merge_gate, w_out, ffn2_norm, ffn2_w_gate, ffn2_w_up, ffn2_w_down, final_norm, loss_target, m_ffn1_norm, m_ffn1_w_gate, m_ffn1_w_up, m_ffn1_w_down, m_mix_norm, m_w_in, m_s5_lambda_re, m_s5_lambda_im, m_s5_log_dt, m_s5_b_re, m_s5_b_im, m_s5_c_re, m_s5_c_im, m_s5_d, m_s5_w_glu, m_gla_w_alpha, m_gla_b_alpha, m_gla_norm, m_attn_q_norm, m_attn_k_norm, m_w_branch_s5, m_w_branch_gla, m_w_branch_attn, m_w_merge_gate, m_b_merge_gate, m_w_out, m_ffn2_norm, m_ffn2_w_gate, m_ffn2_w_up, m_ffn2_w_down, m_final_norm, v_ffn1_norm, v_ffn1_w_gate, v_ffn1_w_up, v_ffn1_w_down, v_mix_norm, v_w_in, v_s5_lambda_re, v_s5_lambda_im, v_s5_log_dt, v_s5_b_re, v_s5_b_im, v_s5_c_re, v_s5_c_im, v_s5_d, v_s5_w_glu, v_gla_w_alpha, v_gla_b_alpha, v_gla_norm, v_attn_q_norm, v_attn_k_norm, v_w_branch_s5, v_w_branch_gla, v_w_branch_attn, v_w_merge_gate, v_b_merge_gate, v_w_out, v_ffn2_norm, v_ffn2_w_gate, v_ffn2_w_up, v_ffn2_w_down, v_final_norm):
    return _train_step(x, ffn1_norm, ffn1_w_gate, ffn1_w_up, ffn1_w_down, mix_norm, w_in, s5_lambda_re, s5_lambda_im, s5_log_dt, s5_b_re, s5_b_im, s5_c_re, s5_c_im, s5_d, s5_w_glu, gla_w_alpha, gla_b_alpha, gla_norm, attn_q_norm, attn_k_norm, w_branch_s5, w_branch_gla, w_branch_attn, w_merge_gate, b_merge_gate, w_out, ffn2_norm, ffn2_w_gate, ffn2_w_up, ffn2_w_down, final_norm, loss_target, m_ffn1_norm, m_ffn1_w_gate, m_ffn1_w_up, m_ffn1_w_down, m_mix_norm, m_w_in, m_s5_lambda_re, m_s5_lambda_im, m_s5_log_dt, m_s5_b_re, m_s5_b_im, m_s5_c_re, m_s5_c_im, m_s5_d, m_s5_w_glu, m_gla_w_alpha, m_gla_b_alpha, m_gla_norm, m_attn_q_norm, m_attn_k_norm, m_w_branch_s5, m_w_branch_gla, m_w_branch_attn, m_w_merge_gate, m_b_merge_gate, m_w_out, m_ffn2_norm, m_ffn2_w_gate, m_ffn2_w_up, m_ffn2_w_down, m_final_norm, v_ffn1_norm, v_ffn1_w_gate, v_ffn1_w_up, v_ffn1_w_down, v_mix_norm, v_w_in, v_s5_lambda_re, v_s5_lambda_im, v_s5_log_dt, v_s5_b_re, v_s5_b_im, v_s5_c_re, v_s5_c_im, v_s5_d, v_s5_w_glu, v_gla_w_alpha, v_gla_b_alpha, v_gla_norm, v_attn_q_norm, v_attn_k_norm, v_w_branch_s5, v_w_branch_gla, v_w_branch_attn, v_w_merge_gate, v_b_merge_gate, v_w_out, v_ffn2_norm, v_ffn2_w_gate, v_ffn2_w_up, v_ffn2_w_down, v_final_norm)


def _train_step(*args):
    nw = len(W_NAMES)
    x, target = args[0][0], args[1 + nw][0]
    w = dict(zip(W_NAMES, args[1:1 + nw]))
    m = dict(zip(W_NAMES, args[2 + nw:2 + 2 * nw]))
    v = dict(zip(W_NAMES, args[2 + 2 * nw:2 + 3 * nw]))

    shard_shapes = [w[n].shape for n in SHARDED]
    gathered = _all_gather(_pack([w[n] for n in SHARDED], BF16), name="gather_weights")
    stacks = _unpack_stack(gathered, shard_shapes)
    full = {n: _join_shards(s, SHARD_AXIS[n]) for n, s in zip(SHARDED, stacks)}

    loss, dx, grads = _step_local(x, target, w, full)
    loss = lax.psum(loss, ("x", "y", "c"))

    outgoing = jnp.concatenate(
        [_split_shards(grads[n], SHARD_AXIS[n]).reshape(N_DEV, -1).astype(BF16) for n in SHARDED], axis=1)
    rows = _pack_rows(outgoing.shape[1])
    outgoing = jnp.pad(outgoing, ((0, 0), (0, rows * PACK_COLS - outgoing.shape[1]))).reshape(N_DEV, rows, PACK_COLS)
    g_sharded = _sum_slots(_all_to_all(outgoing, name="exchange_grads"), name="sum_grads")
    rep_shapes = [w[n].shape for n in REPLICATED]
    g_rep = _sum_slots(_all_gather(_pack([grads[n] for n in REPLICATED], F32), name="gather_small_grads"),
                       name="sum_small_grads")

    out = {}
    for names, shapes, g_pack in ((SHARDED, shard_shapes, g_sharded), (REPLICATED, rep_shapes, g_rep)):
        delta, new_m, new_v = _adamw(_pack([w[n] for n in names], F32), g_pack, _pack([m[n] for n in names], F32),
                                     _pack([v[n] for n in names], F32))
        for kind, packed in (('grad', g_pack), ('delta', delta), ('new_m', new_m), ('new_v', new_v)):
            for n, a in zip(names, _unpack(packed, shapes)):
                out[kind + '_' + n] = a
    return (loss, dx[None]) + tuple(out[kind + '_' + n] for kind in ('grad', 'delta', 'new_m', 'new_v') for n in W_NAMES)


def _unpack_stack(gathered, shapes):
    flat = gathered.reshape(N_DEV, -1)
    out, off = [], 0
    for s in shapes:
        n = math.prod(s)
        out.append(flat[:, off:off + n].reshape((N_DEV,) + tuple(s)))
        off += n
    return out
```

```python
import functools
import math

import jax
import jax.numpy as jnp
import numpy as np
from jax import lax
from jax.experimental import pallas as pl
from jax.experimental.pallas import tpu as pltpu

F32 = jnp.float32
BF16 = jnp.bfloat16

N_DEV = 8
D_MODEL = 1024
DEPTH = 2
GRID_W = 64
D_FF = 2816
NORM_EPS = 1e-6
S5_GROUPS = 32
S5_GROUP_CH = 16
S5_STATE = 64
S5_WIDTH = 512
S5_NSTATE = S5_GROUPS * S5_STATE
S5_LANE_BLOCK = 512
GLA_HEADS = 4
GLA_HEAD_DIM = 128
GLA_WIDTH = 512
GLA_LOWRANK = 16
GLA_TAU = 16.0
GLA_CHUNK = 64
ATTN_Q_HEADS = 8
ATTN_KV_HEADS = 2
ATTN_HEAD_DIM = 64
ATTN_WIDTH = 512
ATTN_KV_WIDTH = 128
ROPE_BASE = 10000.0
IN_SPLITS = (512, 512, 512, 512, 512, 16, 16, 512, 128, 128)
IN_WIDTH = sum(IN_SPLITS)
IN_PAD = 3584
GATE_WIDTH = 3 * D_MODEL
PG_WIDTH = GATE_WIDTH + IN_PAD
P_OFF = GATE_WIDTH
CB_U, CB_GQ, CB_GK, CB_GV, CB_GG, CB_AQ = (P_OFF // 512 + i for i in range(6))
CB_AK, CB_AV, CB_Z = (P_OFF + 3072) // 128, (P_OFF + 3200) // 128, (P_OFF + 3328) // 128
ADAM_LR = 0.001
ADAM_B1 = 0.9
ADAM_B2 = 0.999
ADAM_EPS = 1e-08
ADAM_WD = 0.01
ADAM_STEP = 10
PACK_COLS = 1024

W_NAMES = ['ffn1_norm', 'ffn1_w_gate', 'ffn1_w_up', 'ffn1_w_down', 'mix_norm', 'w_in', 's5_lambda_re', 's5_lambda_im',
           's5_log_dt', 's5_b_re', 's5_b_im', 's5_c_re', 's5_c_im', 's5_d', 's5_w_glu', 'gla_w_alpha', 'gla_b_alpha',
           'gla_norm', 'attn_q_norm', 'attn_k_norm', 'w_branch_s5', 'w_branch_gla', 'w_branch_attn', 'w_merge_gate',
           'b_merge_gate', 'w_out', 'ffn2_norm', 'ffn2_w_gate', 'ffn2_w_up', 'ffn2_w_down', 'final_norm']
SHARD_AXIS = {'ffn1_w_gate': 2, 'ffn1_w_up': 2, 'ffn1_w_down': 1, 'w_in': 2, 's5_w_glu': 1, 'gla_w_alpha': 3,
              'gla_b_alpha': 2, 'w_branch_s5': 2, 'w_branch_gla': 2, 'w_branch_attn': 2, 'w_merge_gate': 2,
              'w_out': 1, 'ffn2_w_gate': 2, 'ffn2_w_up': 2, 'ffn2_w_down': 1}
SHARDED = [n for n in W_NAMES if n in SHARD_AXIS]
REPLICATED = [n for n in W_NAMES if n not in SHARD_AXIS]


def _pick(dim, prefs):
    for p in prefs:
        if dim % p == 0:
            return p
    return dim


def _sigmoid(x):
    return 1.0 / (1.0 + jnp.exp(-x))


def _mm(a, b, *, ta=False, tb=False, out_dtype=F32, scale=None, name):
    a, a_cb, a_w = a if isinstance(a, tuple) else (a, 0, a.shape[1])
    b, b_cb, b_w = b if isinstance(b, tuple) else (b, 0, b.shape[1])
    m, k = (a_w, a.shape[0]) if ta else (a.shape[0], a_w)
    n = b.shape[0] if tb else b_w
    assert (b_w if tb else b.shape[0]) == k, (a.shape, b.shape, ta, tb)
    tm = _pick(m, (1024, 512, 256, 128))
    tn = _pick(n, (512, 256, 128))
    tk = _pick(k, (1024, 512, 256, 128))
    nk = k // tk
    dims = (((0 if ta else 1,), (1 if tb else 0,)), ((), ()))
    a_off = a_cb * (a_w // (tm if ta else tk))
    b_off = b_cb * (b_w // (tk if tb else tn))

    def body(a_ref, b_ref, o_ref, acc_ref):
        kk = pl.program_id(2)
        part = lax.dot_general(a_ref[...].astype(BF16), b_ref[...].astype(BF16), dims, preferred_element_type=F32)

        @pl.when(kk == 0)
        def _():
            acc_ref[...] = part

        @pl.when(kk > 0)
        def _():
            acc_ref[...] += part

        @pl.when(kk == nk - 1)
        def _():
            res = acc_ref[...]
            if scale is not None:
                res = res * scale
            o_ref[...] = res.astype(out_dtype)

    a_spec = (pl.BlockSpec((tk, tm), lambda i, j, kk: (kk, i + a_off)) if ta
              else pl.BlockSpec((tm, tk), lambda i, j, kk: (i, kk + a_off)))
    b_spec = (pl.BlockSpec((tn, tk), lambda i, j, kk: (j, kk + b_off)) if tb
              else pl.BlockSpec((tk, tn), lambda i, j, kk: (kk, j + b_off)))
    return pl.pallas_call(
        body, name=name, grid=(m // tm, n // tn, nk), in_specs=[a_spec, b_spec],
        out_specs=pl.BlockSpec((tm, tn), lambda i, j, kk: (i, j)), out_shape=jax.ShapeDtypeStruct((m, n), out_dtype),
        scratch_shapes=[pltpu.VMEM((tm, tn), F32)],
        compiler_params=pltpu.CompilerParams(dimension_semantics=("parallel", "parallel", "arbitrary")),
    )(a, b)


def _rowmap(fn, rows, consts, outs, reds=(), *, tl, name):
    rows = [r if isinstance(r, tuple) else (r, 0, r.shape[1]) for r in rows]
    length = rows[0][0].shape[0]
    tl = min(tl, length)
    nr, nc, no = len(rows), len(consts), len(outs)

    def body(*refs):
        res = fn(*[r[...] for r in refs[:nr + nc]])
        res = res if isinstance(res, tuple) else (res,)
        for o_ref, val in zip(refs[nr + nc:nr + nc + no], res[:no]):
            o_ref[...] = val.astype(o_ref.dtype)
        if reds:
            step = pl.program_id(0)
            red_refs = refs[nr + nc + no:]

            @pl.when(step == 0)
            def _():
                for d_ref, val in zip(red_refs, res[no:]):
                    d_ref[...] = val.astype(F32)

            @pl.when(step > 0)
            def _():
                for d_ref, val in zip(red_refs, res[no:]):
                    d_ref[...] += val.astype(F32)

    in_specs = [pl.BlockSpec((tl, w), lambda i, cb=cb: (i, cb)) for (_, cb, w) in rows]
    in_specs += [pl.BlockSpec(c.shape, lambda i, nd=c.ndim: (0,) * nd) for c in consts]
    out_specs = [pl.BlockSpec((tl, w), lambda i: (i, 0)) for (w, _) in outs]
    out_specs += [pl.BlockSpec(s, lambda i, nd=len(s): (0,) * nd) for s in reds]
    out_shape = [jax.ShapeDtypeStruct((length, w), dt) for (w, dt) in outs]
    out_shape += [jax.ShapeDtypeStruct(s, F32) for s in reds]
    res = pl.pallas_call(
        body, name=name, grid=(length // tl,), in_specs=in_specs, out_specs=out_specs, out_shape=out_shape,
        compiler_params=pltpu.CompilerParams(dimension_semantics=("arbitrary" if reds else "parallel",)),
    )(*[r[0] for r in rows], *consts)
    return res


def _rms(x):
    return lax.rsqrt(jnp.mean(x * x, axis=-1, keepdims=True) + NORM_EPS)


def _rmsnorm_fwd(x, gain):
    def fn(xv, g):
        return xv * _rms(xv) * g
    return _rowmap(fn, [x], [gain.reshape(1, -1)], [(x.shape[1], BF16)], tl=256, name="rmsnorm_fwd")[0]


def _rmsnorm_bwd(x, gain, dh, dres):
    def fn(xv, dhv, drv, g):
        r = _rms(xv)
        gd = dhv * g
        dx = r * gd - xv * (r * r * r) * jnp.mean(xv * gd, axis=-1, keepdims=True)
        return drv + dx, jnp.sum(dhv * xv * r, axis=0, keepdims=True)
    dx, dg = _rowmap(fn, [x, dh, dres], [gain.reshape(1, -1)], [(x.shape[1], F32)], [(1, x.shape[1])], tl=256,
                     name="rmsnorm_bwd")
    return dx, dg[0]


def _ffn_fwd(x, gain, w_gu, w_down):
    h = _rmsnorm_fwd(x, gain)
    gu = _mm(h, w_gu, name="ffn_gu")

    def act(g, u):
        return g * _sigmoid(g) * u
    a = _rowmap(act, [(gu, 0, D_FF), (gu, 1, D_FF)], [], [(D_FF, BF16)], tl=256, name="ffn_act")[0]
    y = _mm(a, w_down, scale=0.5, name="ffn_down")

    def add(xv, yv):
        return xv + yv
    x_out = _rowmap(add, [x, y], [], [(D_MODEL, F32)], tl=512, name="residual_add")[0]
    return x_out, (x, h, gu, a)


def _ffn_bwd(saved, gain, w_gu, w_down, dx_out):
    x, h, gu, a = saved
    dxo = dx_out.astype(BF16)
    d_wdown = _mm(a, dxo, ta=True, scale=0.5, name="ffn_dwdown")
    da = _mm(dxo, w_down, tb=True, scale=0.5, name="ffn_da")

    def act_bwd(g, u, dav):
        s = _sigmoid(g)
        return jnp.concatenate([dav * u * (s * (1.0 + g * (1.0 - s))), dav * (g * s)], axis=1)
    dgu = _rowmap(act_bwd, [(gu, 0, D_FF), (gu, 1, D_FF), da], [], [(2 * D_FF, BF16)], tl=256, name="ffn_act_bwd")[0]
    d_wgu = _mm(h, dgu, ta=True, name="ffn_dwgu")
    dh = _mm(dgu, w_gu, tb=True, name="ffn_dh")
    dx, dgain = _rmsnorm_bwd(x, gain, dh, dx_out)
    return dx, dgain, d_wgu, d_wdown


def _s5_col(n):
    return (n // S5_LANE_BLOCK) * 2 * S5_LANE_BLOCK + n % S5_LANE_BLOCK


def _s5_blocked(re, im):
    lead = re.shape[:-1]
    nb = S5_NSTATE // S5_LANE_BLOCK
    both = jnp.stack([re.reshape(*lead, nb, S5_LANE_BLOCK), im.reshape(*lead, nb, S5_LANE_BLOCK)], axis=-2)
    return both.reshape(*lead, 2 * S5_NSTATE)


def _s5_unblocked(z):
    lead = z.shape[:-1]
    nb = S5_NSTATE // S5_LANE_BLOCK
    both = z.reshape(*lead, nb, 2, S5_LANE_BLOCK)
    return both[..., 0, :].reshape(*lead, S5_NSTATE), both[..., 1, :].reshape(*lead, S5_NSTATE)


def _s5_tables(a_re, a_im, reverse):
    a = lax.complex(a_re, a_im)
    a2 = a * a
    a4 = a2 * a2
    rows = jnp.arange(8)
    pw = [a]
    for _ in range(7):
        pw.append(pw[-1] * a)
    pw = jnp.stack(pw)
    if reverse:
        pw = pw[::-1]
    tabs = []
    for coef, s in ((a, 1), (a2, 2), (a4, 4)):
        live = (rows <= 7 - s) if reverse else (rows >= s)
        tabs.append(jnp.where(live[:, None], coef[None, :], 0.0))
    tabs.append(pw)
    tabs = jnp.stack(tabs)
    return _s5_blocked(jnp.real(tabs), jnp.imag(tabs))


def _s5_scan_tile(v, tab_ref, prev, reverse):
    lb = S5_LANE_BLOCK
    vr, vi = v[:, :lb], v[:, lb:]
    for idx, s in enumerate((1, 2, 4)):
        cr, ci = tab_ref[idx, :, :lb], tab_ref[idx, :, lb:]
        sh = 8 - s if reverse else s
        sr, si = pltpu.roll(vr, sh, 0), pltpu.roll(vi, sh, 0)
        vr, vi = vr + cr * sr - ci * si, vi + cr * si + ci * sr
    row = 0 if reverse else 7
    pr = jnp.broadcast_to(prev[row:row + 1, :lb], (8, lb))
    pi = jnp.broadcast_to(prev[row:row + 1, lb:], (8, lb))
    cr, ci = tab_ref[3, :, :lb], tab_ref[3, :, lb:]
    return jnp.concatenate([vr + cr * pr - ci * pi, vi + cr * pi + ci * pr], axis=1)


def _s5_scan(v, tabs, reverse, *, name):
    length = v.shape[0]
    tb = min(512, length)
    ntb = length // tb
    nlb = S5_NSTATE // S5_LANE_BLOCK
    wb = 2 * S5_LANE_BLOCK
    ntile = tb // 8

    def body(tab_ref, v_ref, x_ref, carry_ref):
        @pl.when(pl.program_id(1) == 0)
        def _():
            carry_ref[...] = jnp.zeros_like(carry_ref)

        def step(i, prev):
            r0 = pl.multiple_of((ntile - 1 - i if reverse else i) * 8, 8)
            x = _s5_scan_tile(v_ref[pl.ds(r0, 8), :], tab_ref, prev, reverse)
            x_ref[pl.ds(r0, 8), :] = x
            return x

        carry_ref[...] = lax.fori_loop(0, ntile, step, carry_ref[...])

    tmap = (lambda c, t: (ntb - 1 - t, c)) if reverse else (lambda c, t: (t, c))
    return pl.pallas_call(
        body, name=name, grid=(nlb, ntb),
        in_specs=[pl.BlockSpec((4, 8, wb), lambda c, t: (0, 0, c)), pl.BlockSpec((tb, wb), tmap)],
        out_specs=pl.BlockSpec((tb, wb), tmap), out_shape=jax.ShapeDtypeStruct(v.shape, F32),
        scratch_shapes=[pltpu.VMEM((8, wb), F32)],
        compiler_params=pltpu.CompilerParams(dimension_semantics=("parallel", "arbitrary")),
    )(tabs, v)


def _s5_scan_adjoint(g, xs, tabs_conj, reverse, *, name):
    length = g.shape[0]
    tb = min(512, length)
    ntb = length // tb
    nlb = S5_NSTATE // S5_LANE_BLOCK
    lb = S5_LANE_BLOCK
    wb = 2 * lb
    ntile = tb // 8
    adj_rev = not reverse
    if reverse:
        edge = jnp.concatenate([xs[tb::tb], jnp.zeros((1, xs.shape[1]), F32)], axis=0)
    else:
        edge = jnp.concatenate([jnp.zeros((1, xs.shape[1]), F32), xs[tb - 1:length - 1:tb]], axis=0)
    edge = edge.reshape(ntb, 1, xs.shape[1])

    def body(tab_ref, g_ref, x_ref, edge_ref, lam_ref, da_ref, carry_ref):
        @pl.when(pl.program_id(1) == 0)
        def _():
            carry_ref[...] = jnp.zeros_like(carry_ref)
            da_ref[...] = jnp.zeros_like(da_ref)

        rows = lax.broadcasted_iota(jnp.int32, (8, wb), 0)

        def step(i, carry):
            prev, acc = carry
            k = ntile - 1 - i if adj_rev else i
            r0 = pl.multiple_of(k * 8, 8)
            lam = _s5_scan_tile(g_ref[pl.ds(r0, 8), :], tab_ref, prev, adj_rev)
            lam_ref[pl.ds(r0, 8), :] = lam
            x = x_ref[pl.ds(r0, 8), :]
            if reverse:
                kn = jnp.minimum(k + 1, ntile - 1)
                nb = x_ref[pl.ds(pl.multiple_of(kn * 8, 8), 8), :][0:1, :]
                nb = jnp.where(k == ntile - 1, edge_ref[0], nb)
                xp = jnp.where(rows == 7, jnp.broadcast_to(nb, (8, wb)), pltpu.roll(x, 7, 0))
            else:
                kn = jnp.maximum(k - 1, 0)
                nb = x_ref[pl.ds(pl.multiple_of(kn * 8, 8), 8), :][7:8, :]
                nb = jnp.where(k == 0, edge_ref[0], nb)
                xp = jnp.where(rows == 0, jnp.broadcast_to(nb, (8, wb)), pltpu.roll(x, 1, 0))
            xr, xi, lr, li = xp[:, :lb], xp[:, lb:], lam[:, :lb], lam[:, lb:]
            acc = acc + jnp.concatenate([xr * lr + xi * li, xr * li - xi * lr], axis=1)
            return lam, acc

        last, acc = lax.fori_loop(0, ntile, step, (carry_ref[...], da_ref[...]))
        carry_ref[...] = last
        da_ref[...] = acc

    tmap = (lambda c, t: (ntb - 1 - t, c)) if adj_rev else (lambda c, t: (t, c))
    emap = (lambda c, t: (ntb - 1 - t, 0, c)) if adj_rev else (lambda c, t: (t, 0, c))
    return pl.pallas_call(
        body, name=name, grid=(nlb, ntb),
        in_specs=[pl.BlockSpec((4, 8, wb), lambda c, t: (0, 0, c)), pl.BlockSpec((tb, wb), tmap),
                  pl.BlockSpec((tb, wb), tmap), pl.BlockSpec((1, 1, wb), emap)],
        out_specs=[pl.BlockSpec((tb, wb), tmap), pl.BlockSpec((8, wb), lambda c, t: (0, c))],
        out_shape=[jax.ShapeDtypeStruct(g.shape, F32), jax.ShapeDtypeStruct((8, g.shape[1]), F32)],
        scratch_shapes=[pltpu.VMEM((8, wb), F32)],
        compiler_params=pltpu.CompilerParams(dimension_semantics=("parallel", "arbitrary")),
    )(tabs_conj, g, xs, edge)


def _s5_prep(lam_re, lam_im, log_dt, b_re, b_im):
    lam = lax.complex(lam_re, lam_im)
    dt = jnp.exp(log_dt)[:, None]
    lam_bar = jnp.exp(lam * dt)
    b_bar = ((lam_bar - 1.0) / lam)[..., None] * lax.complex(b_re, b_im)
    return (jnp.real(lam_bar).reshape(-1), jnp.imag(lam_bar).reshape(-1), jnp.real(b_bar), jnp.imag(b_bar))


def _s5_in_matrix(bb_re, bb_im):
    eye = jnp.eye(S5_GROUPS, dtype=F32)
    def dense(bb):
        return jnp.einsum('gph,gk->ghkp', bb, eye).reshape(S5_WIDTH, S5_NSTATE)
    return _s5_blocked(dense(bb_re), dense(bb_im))


def _s5_in_matrix_grad(d_mat):
    d_re, d_im = _s5_unblocked(d_mat)
    def diag(d):
        d4 = d.reshape(S5_GROUPS, S5_GROUP_CH, S5_GROUPS, S5_STATE)
        return jnp.swapaxes(jnp.sum(d4 * jnp.eye(S5_GROUPS, dtype=F32)[:, None, :, None], axis=2), 1, 2)
    return diag(d_re), diag(d_im)


def _s5_out_matrix(c_re, c_im):
    eye = jnp.eye(S5_GROUPS, dtype=F32)
    def dense(c):
        return jnp.einsum('ghp,gk->gpkh', c, eye).reshape(S5_NSTATE, S5_WIDTH)
    return jnp.swapaxes(_s5_blocked(jnp.swapaxes(dense(c_re), 0, 1), jnp.swapaxes(dense(-c_im), 0, 1)), 0, 1)


def _s5_out_matrix_grad(d_mat):
    d_re, d_nim = _s5_unblocked(jnp.swapaxes(d_mat, 0, 1))
    def diag(d):
        d4 = d.reshape(S5_GROUPS, S5_GROUP_CH, S5_GROUPS, S5_STATE)
        return jnp.sum(d4 * jnp.eye(S5_GROUPS, dtype=F32)[:, None, :, None], axis=2)
    return diag(d_re), -diag(d_nim)


def _gelu_parts(x):
    k = math.sqrt(2.0 / math.pi)
    inner = k * (x + 0.044715 * x * x * x)
    th = jnp.tanh(inner)
    return th, k * (1.0 + 3.0 * 0.044715 * x * x)


def _s5_fwd(p_in, prm, w_glu):
    dirs = []
    ys = []
    for d, reverse in ((0, False), (1, True)):
        a_re, a_im, bb_re, bb_im = _s5_prep(prm['lambda_re'][d], prm['lambda_im'][d], prm['log_dt'][d],
                                            prm['b_re'][d], prm['b_im'][d])
        b_mat = _s5_in_matrix(bb_re, bb_im).astype(BF16)
        c_mat = _s5_out_matrix(prm['c_re'][d], prm['c_im'][d]).astype(BF16)
        bu = _mm((p_in, CB_U, S5_WIDTH), b_mat, name="s5_bu")
        xs = _s5_scan(bu, _s5_tables(a_re, a_im, reverse), reverse, name="s5_scan_rev" if reverse else "s5_scan_fwd")
        ys.append(_mm(xs, c_mat, name="s5_y"))
        dirs.append((a_re, a_im, b_mat, c_mat, xs))

    def post(yf, yb, u, dskip):
        ypre = yf + yb + dskip * u
        th, _ = _gelu_parts(ypre)
        return ypre, 0.5 * ypre * (1.0 + th)
    ypre, yg = _rowmap(post, [ys[0], ys[1], (p_in, CB_U, S5_WIDTH)], [prm['d'].reshape(1, -1)],
                       [(S5_WIDTH, F32), (S5_WIDTH, F32)], tl=512, name="s5_post")
    t = _mm(yg, w_glu, name="s5_glu_mm")

    def glu(ygv, tv):
        return ygv * _sigmoid(tv)
    y = _rowmap(glu, [yg, t], [], [(S5_WIDTH, BF16)], tl=512, name="s5_glu")[0]
    return y, (dirs, ypre, yg, t)


def _s5_bwd(pg, prm, w_glu, saved, dy):
    dirs, ypre, yg, t = saved

    def glu_bwd(dyv, ygv, tv):
        s = _sigmoid(tv)
        return dyv * ygv * s * (1.0 - s), dyv * s
    dt, dyg_direct = _rowmap(glu_bwd, [dy, yg, t], [], [(S5_WIDTH, BF16), (S5_WIDTH, F32)], tl=512, name="s5_glu_bwd")
    grads = {'w_glu': _mm(yg, dt, ta=True, name="s5_dwglu")}
    dyg_mm = _mm(dt, w_glu, tb=True, name="s5_dyg")

    def post_bwd(dyd, dym, yp, u, dskip):
        th, dinner = _gelu_parts(yp)
        dyp = (dyd + dym) * (0.5 * (1.0 + th) + 0.5 * yp * (1.0 - th * th) * dinner)
        return dyp, dyp * dskip, jnp.sum(dyp * u, axis=0, keepdims=True)
    dyp, du_skip, dd = _rowmap(post_bwd, [dyg_direct, dyg_mm, ypre, (pg, CB_U, S5_WIDTH)], [prm['d'].reshape(1, -1)],
                               [(S5_WIDTH, F32), (S5_WIDTH, F32)], [(1, S5_WIDTH)], tl=512, name="s5_post_bwd")
    grads['d'] = dd[0]
    du = [du_skip]
    per_dir = []
    for d, reverse in ((0, False), (1, True)):
        a_re, a_im, b_mat, c_mat, xs = dirs[d]
        dxs = _mm(dyp, c_mat, tb=True, name="s5_dxs")
        lam, da = _s5_scan_adjoint(dxs, xs, _s5_tables(a_re, -a_im, not reverse), reverse,
                                   name="s5_adjoint_rev" if reverse else "s5_adjoint_fwd")
        du.append(_mm(lam, b_mat, tb=True, name="s5_du"))
        dbb_re, dbb_im = _s5_in_matrix_grad(_mm((pg, CB_U, S5_WIDTH), lam, ta=True, name="s5_dbmat"))
        dc_re, dc_im = _s5_out_matrix_grad(_mm(xs, dyp, ta=True, name="s5_dcmat"))
        da_re, da_im = _s5_unblocked(jnp.sum(da, axis=0))
        _, vjp = jax.vjp(_s5_prep, prm['lambda_re'][d], prm['lambda_im'][d], prm['log_dt'][d], prm['b_re'][d], prm['b_im'][d])
        per_dir.append(vjp((da_re, da_im, dbb_re, dbb_im)) + (dc_re, dc_im))
    for i, key in enumerate(('lambda_re', 'lambda_im', 'log_dt', 'b_re', 'b_im', 'c_re', 'c_im')):
        grads[key] = jnp.stack([per_dir[0][i], per_dir[1][i]])
    return du, grads


def _split3(x):
    hi = x.astype(BF16)
    r = x - hi.astype(F32)
    mid = r.astype(BF16)
    return hi, mid, (r - mid.astype(F32)).astype(BF16)


def _exact_dot(ones, x, dims):
    parts = [lax.dot_general(ones, p, dims, preferred_element_type=F32) for p in _split3(x)]
    return parts[0] + parts[1] + parts[2]


_NN = (((1,), (0,)), ((), ()))
_NT = (((1,), (1,)), ((), ()))
_TN = (((0,), (0,)), ((), ()))


def _dot(a, b, dims=_NN):
    return lax.dot_general(a.astype(BF16), b.astype(BF16), dims, preferred_element_type=F32)


def _gla_chunk_mask(reverse):
    rows = lax.broadcasted_iota(jnp.int32, (GLA_CHUNK, GLA_CHUNK), 0)
    cols = lax.broadcasted_iota(jnp.int32, (GLA_CHUNK, GLA_CHUNK), 1)
    return (cols >= rows) if reverse else (cols <= rows)


def _gla_fwd(pg, la, reverse, *, name):
    length = la.shape[0]
    nch = length // GLA_CHUNK
    scale = GLA_HEAD_DIM ** -0.5
    last = 0 if reverse else GLA_CHUNK - 1
    hd = GLA_HEAD_DIM

    def body(q_ref, k_ref, v_ref, la_ref, o_ref, sp_ref, st_ref):
        @pl.when(pl.program_id(0) == 0)
        def _():
            st_ref[...] = jnp.zeros_like(st_ref)

        mask = _gla_chunk_mask(reverse)
        b = _exact_dot(mask.astype(BF16), la_ref[...], _NN)
        sp_ref[0] = st_ref[...]
        outs = []
        for h in range(GLA_HEADS):
            sl = slice(h * hd, (h + 1) * hd)
            bh = b[:, sl]
            bl = bh[last:last + 1, :]
            k = k_ref[:, sl]
            v = v_ref[:, sl]
            qd = q_ref[:, sl] * scale * jnp.exp(bh)
            kd = k * jnp.exp(-bh)
            ke = k * jnp.exp(bl - bh)
            st = st_ref[sl, :]
            p = jnp.where(mask, _dot(qd, kd, _NT), 0.0)
            outs.append(_dot(p, v) + _dot(qd, st, _NT))
            st_ref[sl, :] = st * jnp.exp(bl) + _dot(v, ke, _TN)
        o_ref[...] = jnp.concatenate(outs, axis=1)

    cmap = (lambda n: nch - 1 - n) if reverse else (lambda n: n)
    col = lambda cb: pl.BlockSpec((GLA_CHUNK, GLA_WIDTH), lambda n, cb=cb: (cmap(n), cb))
    return pl.pallas_call(
        body, name=name, grid=(nch,),
        in_specs=[col(CB_GQ), col(CB_GK), col(CB_GV), col(0)],
        out_specs=[col(0), pl.BlockSpec((1, GLA_WIDTH, hd), lambda n: (cmap(n), 0, 0))],
        out_shape=[jax.ShapeDtypeStruct((length, GLA_WIDTH), F32), jax.ShapeDtypeStruct((nch, GLA_WIDTH, hd), F32)],
        scratch_shapes=[pltpu.VMEM((GLA_WIDTH, hd), F32)],
        compiler_params=pltpu.CompilerParams(dimension_semantics=("arbitrary",)),
    )(pg, pg, pg, la)


def _gla_bwd(pg, la, do, sprev, reverse, *, name):
    length = la.shape[0]
    nch = length // GLA_CHUNK
    scale = GLA_HEAD_DIM ** -0.5
    last = 0 if reverse else GLA_CHUNK - 1
    hd = GLA_HEAD_DIM

    def body(q_ref, k_ref, v_ref, la_ref, do_ref, sp_ref, dq_ref, dk_ref, dv_ref, dla_ref, dst_ref):
        @pl.when(pl.program_id(0) == 0)
        def _():
            dst_ref[...] = jnp.zeros_like(dst_ref)

        mask = _gla_chunk_mask(reverse)
        tri = mask.astype(BF16)
        b = _exact_dot(tri, la_ref[...], _NN)
        is_last = lax.broadcasted_iota(jnp.int32, (GLA_CHUNK, hd), 0) == last
        dqs, dks, dvs, dbs = [], [], [], []
        for h in range(GLA_HEADS):
            sl = slice(h * hd, (h + 1) * hd)
            bh = b[:, sl]
            bl = bh[last:last + 1, :]
            eb, enb, ebl, el = jnp.exp(bh), jnp.exp(-bh), jnp.exp(bl - bh), jnp.exp(bl)
            k = k_ref[:, sl]
            v = v_ref[:, sl]
            dov = do_ref[:, sl]
            qd = q_ref[:, sl] * scale * eb
            kd = k * enb
            ke = k * ebl
            st = sp_ref[0, sl, :]
            dst = dst_ref[sl, :]
            p = jnp.where(mask, _dot(qd, kd, _NT), 0.0)
            dp = jnp.where(mask, _dot(dov, v, _NT), 0.0)
            dqd = _dot(dp, kd) + _dot(dov, st)
            dkd = _dot(dp, qd, _TN)
            dvs.append(_dot(p, dov, _TN) + _dot(ke, dst, _NT))
            dke = _dot(v, dst)
            dst_ref[sl, :] = dst * el + _dot(dov, qd, _TN)
            dbl = el * jnp.sum(dst * st, axis=0, keepdims=True) + jnp.sum(dke * ke, axis=0, keepdims=True)
            db = dqd * qd - dkd * kd - dke * ke
            dbs.append(jnp.where(is_last, db + dbl, db))
            dqs.append(dqd * eb * scale)
            dks.append(dkd * enb + dke * ebl)
        dq_ref[...] = jnp.concatenate(dqs, axis=1)
        dk_ref[...] = jnp.concatenate(dks, axis=1)
        dv_ref[...] = jnp.concatenate(dvs, axis=1)
        tri_t = _gla_chunk_mask(not reverse).astype(BF16)
        dla_ref[...] = _exact_dot(tri_t, jnp.concatenate(dbs, axis=1), _NN)

    cmap = (lambda n: n) if reverse else (lambda n: nch - 1 - n)
    col = lambda cb: pl.BlockSpec((GLA_CHUNK, GLA_WIDTH), lambda n, cb=cb: (cmap(n), cb))
    wide = jax.ShapeDtypeStruct((length, GLA_WIDTH), F32)
    return pl.pallas_call(
        body, name=name, grid=(nch,),
        in_specs=[col(CB_GQ), col(CB_GK), col(CB_GV), col(0), col(0),
                  pl.BlockSpec((1, GLA_WIDTH, hd), lambda n: (cmap(n), 0, 0))],
        out_specs=[col(0)] * 4, out_shape=[wide] * 4,
        scratch_shapes=[pltpu.VMEM((GLA_WIDTH, hd), F32)],
        compiler_params=pltpu.CompilerParams(dimension_semantics=("arbitrary",)),
    )(pg, pg, pg, la, do, sprev)


def _log_sigmoid(x):
    return jnp.minimum(x, 0.0) - jnp.log(1.0 + jnp.exp(-jnp.abs(x)))


def _gla_alpha_padded(w_alpha):
    w = jnp.zeros((2, 128, GLA_WIDTH), w_alpha.dtype)
    w = w.at[0, 0:GLA_LOWRANK].set(w_alpha[0])
    return w.at[1, GLA_LOWRANK:2 * GLA_LOWRANK].set(w_alpha[1])


def _gla_branch_fwd(pg, w_alpha, b_alpha, norm_gain):
    wa = _gla_alpha_padded(w_alpha).astype(BF16)

    def gates(z, w, bias):
        return (_log_sigmoid(_dot(z, w[0]) + bias[0:1]) / GLA_TAU, _log_sigmoid(_dot(z, w[1]) + bias[1:2]) / GLA_TAU)
    la_f, la_b = _rowmap(gates, [(pg, CB_Z, 128)], [wa, b_alpha], [(GLA_WIDTH, F32), (GLA_WIDTH, F32)], tl=512,
                         name="gla_gates")
    o_f, sp_f = _gla_fwd(pg, la_f, False, name="gla_fwd")
    o_b, sp_b = _gla_fwd(pg, la_b, True, name="gla_fwd_rev")

    def post(of, ob, gate, gn):
        o = of + ob
        on = jnp.concatenate([o[:, s:s + GLA_HEAD_DIM] * _rms(o[:, s:s + GLA_HEAD_DIM]) * gn
                              for s in range(0, GLA_WIDTH, GLA_HEAD_DIM)], axis=1)
        return o, on * (gate * _sigmoid(gate))
    o, y = _rowmap(post, [o_f, o_b, (pg, CB_GG, GLA_WIDTH)], [norm_gain.reshape(1, -1)],
                   [(GLA_WIDTH, F32), (GLA_WIDTH, BF16)], tl=512, name="gla_post")
    return y, (wa, la_f, la_b, sp_f, sp_b, o)


def _gla_branch_bwd(pg, w_alpha, b_alpha, norm_gain, saved, dy):
    wa, la_f, la_b, sp_f, sp_b, o = saved

    def post_bwd(dyv, ov, gate, gn):
        s = _sigmoid(gate)
        dos, dgn, ons = [], [], []
        for c in range(0, GLA_WIDTH, GLA_HEAD_DIM):
            oh = ov[:, c:c + GLA_HEAD_DIM]
            r = _rms(oh)
            don = dyv[:, c:c + GLA_HEAD_DIM] * (gate[:, c:c + GLA_HEAD_DIM] * s[:, c:c + GLA_HEAD_DIM])
            gd = don * gn
            dos.append(r * gd - oh * (r * r * r) * jnp.mean(oh * gd, axis=-1, keepdims=True))
            dgn.append(jnp.sum(don * oh * r, axis=0, keepdims=True))
            ons.append(oh * r * gn)
        on = jnp.concatenate(ons, axis=1)
        dgate = dyv * on * (s * (1.0 + gate * (1.0 - s)))
        return jnp.concatenate(dos, axis=1), dgate, jnp.concatenate(dgn, axis=1)
    do, dgate, dgn = _rowmap(post_bwd, [dy, o, (pg, CB_GG, GLA_WIDTH)], [norm_gain.reshape(1, -1)],
                             [(GLA_WIDTH, F32), (GLA_WIDTH, F32)], [(1, GLA_WIDTH)], tl=512, name="gla_post_bwd")
    dq_f, dk_f, dv_f, dla_f = _gla_bwd(pg, la_f, do, sp_f, False, name="gla_bwd")
    dq_b, dk_b, dv_b, dla_b = _gla_bwd(pg, la_b, do, sp_b, True, name="gla_bwd_rev")

    def gates_bwd(z, dlf, dlb, w, bias):
        dz = jnp.zeros_like(z)
        dlogits, dbs = [], []
        for d, dl in ((0, dlf), (1, dlb)):
            logit = _dot(z, w[d]) + bias[d:d + 1]
            dlogit = dl * (1.0 / GLA_TAU) * _sigmoid(-logit)
            dz = dz + _dot(dlogit, w[d], _NT)
            dlogits.append(dlogit)
            dbs.append(jnp.sum(dlogit, axis=0, keepdims=True))
        return dz, dlogits[0], dlogits[1], dbs[0], dbs[1]
    dz, dlg_f, dlg_b, dba_f, dba_b = _rowmap(
        gates_bwd, [(pg, CB_Z, 128), dla_f, dla_b], [wa, b_alpha], [(128, F32), (GLA_WIDTH, BF16), (GLA_WIDTH, BF16)],
        [(1, GLA_WIDTH), (1, GLA_WIDTH)], tl=512, name="gla_gates_bwd")
    dwa_f = _mm(dlg_f, (pg, CB_Z, 128), ta=True, name="gla_dwalpha")
    dwa_b = _mm(dlg_b, (pg, CB_Z, 128), ta=True, name="gla_dwalpha")
    grads = {'w_alpha': jnp.stack([dwa_f[:, 0:GLA_LOWRANK].T, dwa_b[:, GLA_LOWRANK:2 * GLA_LOWRANK].T]),
             'b_alpha': jnp.concatenate([dba_f, dba_b], axis=0),
             'norm': jnp.sum(dgn.reshape(GLA_HEADS, GLA_HEAD_DIM), axis=0)}
    return [dq_f, dq_b], [dk_f, dk_b], [dv_f, dv_b], dgate, dz, grads


def _rope_tables(length):
    half = ATTN_HEAD_DIM // 2
    inv_freq = ROPE_BASE ** (-jnp.arange(half // 2, dtype=F32) * 2.0 / half)
    t = jnp.arange(length, dtype=jnp.int32)
    def one(pos):
        ang = pos.astype(F32)[:, None] * inv_freq[None, :]
        c, s = jnp.cos(ang), jnp.sin(ang)
        return jnp.concatenate([c, c], axis=1), jnp.concatenate([-s, s], axis=1)
    c_r, s_r = one(t // GRID_W)
    c_c, s_c = one(t % GRID_W)
    return jnp.concatenate([c_r, c_c], axis=1), jnp.concatenate([s_r, s_c], axis=1)


def _rope_swap(y):
    w = y.shape[1]
    lane = lax.broadcasted_iota(jnp.int32, y.shape, 1)
    return jnp.where(lane % 32 < 16, pltpu.roll(y, w - 16, 1), pltpu.roll(y, 16, 1))


def _head_sums(x, ones):
    parts = [lax.dot_general(p, ones, _NN, preferred_element_type=F32) for p in _split3(x)]
    return parts[0] + parts[1] + parts[2]


def _head_ones(width):
    seg = np.arange(width) // ATTN_HEAD_DIM
    return jnp.asarray(seg[:, None] == seg[None, :], BF16)


def _qk_prep_fwd(pg, cb, width, gain, cos, sin, scale, *, name):
    heads = width // ATTN_HEAD_DIM
    def fn(x, c, s, g, ones):
        r = lax.rsqrt(_head_sums(x * x, ones) * (1.0 / ATTN_HEAD_DIM) + NORM_EPS)
        y = x * r * g
        return (y * c + _rope_swap(y) * s) * scale
    return _rowmap(fn, [(pg, cb, width), jnp.tile(cos, (1, heads)), jnp.tile(sin, (1, heads))],
                   [jnp.tile(gain, heads).reshape(1, -1), _head_ones(width)], [(width, BF16)], tl=512, name=name)[0]


def _qk_prep_bwd(pg, cb, width, gain, cos, sin, scale, dout, *, name):
    heads = width // ATTN_HEAD_DIM
    def fn(x, dov, c, s, g, ones):
        r = lax.rsqrt(_head_sums(x * x, ones) * (1.0 / ATTN_HEAD_DIM) + NORM_EPS)
        dos = dov * scale
        dy = dos * c + _rope_swap(dos * s)
        gd = dy * g
        dx = r * gd - x * (r * r * r) * (_head_sums(x * gd, ones) * (1.0 / ATTN_HEAD_DIM))
        return dx, jnp.sum(dy * x * r, axis=0, keepdims=True)
    dx, dg = _rowmap(fn, [(pg, cb, width), dout, jnp.tile(cos, (1, heads)), jnp.tile(sin, (1, heads))],
                     [jnp.tile(gain, heads).reshape(1, -1), _head_ones(width)], [(width, F32)], [(1, width)], tl=512,
                     name=name)
    return dx, jnp.sum(dg.reshape(heads, ATTN_HEAD_DIM), axis=0)


def _to_heads(x, heads):
    return jnp.transpose(x.reshape(x.shape[0], heads, ATTN_HEAD_DIM), (1, 0, 2))


def _from_heads(x):
    return jnp.transpose(x, (1, 0, 2)).reshape(x.shape[1], x.shape[0] * ATTN_HEAD_DIM)


ATTN_GROUP = ATTN_Q_HEADS // ATTN_KV_HEADS
ATTN_TQ = 256


def _attn_fwd(q, k, v):
    length = q.shape[1]
    tq = min(ATTN_TQ, length)

    def body(q_ref, k_ref, v_ref, o_ref):
        kk, vv = k_ref[0], v_ref[0]
        for g in range(ATTN_GROUP):
            s = _dot(q_ref[g], kk, _NT)
            p = jnp.exp(s - jnp.max(s, axis=-1, keepdims=True))
            o_ref[g] = _dot(p, vv) / jnp.sum(p, axis=-1, keepdims=True)

    kv_spec = pl.BlockSpec((1, length, ATTN_HEAD_DIM), lambda h, i: (h, 0, 0))
    q_spec = pl.BlockSpec((ATTN_GROUP, tq, ATTN_HEAD_DIM), lambda h, i: (h, i, 0))
    return pl.pallas_call(
        body, name="attn_fwd", grid=(ATTN_KV_HEADS, length // tq), in_specs=[q_spec, kv_spec, kv_spec],
        out_specs=q_spec, out_shape=jax.ShapeDtypeStruct(q.shape, F32),
        compiler_params=pltpu.CompilerParams(dimension_semantics=("parallel", "parallel")),
    )(q, k, v)


def _attn_bwd(q, k, v, o, do):
    length = q.shape[1]
    tq = min(ATTN_TQ, length)

    def body(q_ref, k_ref, v_ref, o_ref, do_ref, dq_ref, dk_ref, dv_ref):
        @pl.when(pl.program_id(1) == 0)
        def _():
            dk_ref[...] = jnp.zeros_like(dk_ref)
            dv_ref[...] = jnp.zeros_like(dv_ref)

        kk, vv = k_ref[0], v_ref[0]
        for g in range(ATTN_GROUP):
            qg, dog = q_ref[g], do_ref[g]
            s = _dot(qg, kk, _NT)
            p = jnp.exp(s - jnp.max(s, axis=-1, keepdims=True))
            p = p / jnp.sum(p, axis=-1, keepdims=True)
            dp = _dot(dog, vv, _NT)
            ds = p * (dp - jnp.sum(dog * o_ref[g], axis=-1, keepdims=True))
            dq_ref[g] = _dot(ds, kk)
            dk_ref[0] += _dot(ds, qg, _TN)
            dv_ref[0] += _dot(p, dog, _TN)

    kv_spec = pl.BlockSpec((1, length, ATTN_HEAD_DIM), lambda h, i: (h, 0, 0))
    q_spec = pl.BlockSpec((ATTN_GROUP, tq, ATTN_HEAD_DIM), lambda h, i: (h, i, 0))
    return pl.pallas_call(
        body, name="attn_bwd", grid=(ATTN_KV_HEADS, length // tq),
        in_specs=[q_spec, kv_spec, kv_spec, q_spec, q_spec], out_specs=[q_spec, kv_spec, kv_spec],
        out_shape=[jax.ShapeDtypeStruct(q.shape, F32), jax.ShapeDtypeStruct(k.shape, F32),
                   jax.ShapeDtypeStruct(k.shape, F32)],
        compiler_params=pltpu.CompilerParams(dimension_semantics=("parallel", "arbitrary")),
    )(q, k, v, o, do)


def _attn_branch_fwd(pg, q_gain, k_gain):
    cos, sin = _rope_tables(pg.shape[0])
    qp = _qk_prep_fwd(pg, CB_AQ, ATTN_WIDTH, q_gain, cos, sin, ATTN_HEAD_DIM ** -0.5, name="attn_q_prep")
    kp = _qk_prep_fwd(pg, CB_AK, ATTN_KV_WIDTH, k_gain, cos, sin, 1.0, name="attn_k_prep")
    qh, kh = _to_heads(qp, ATTN_Q_HEADS), _to_heads(kp, ATTN_KV_HEADS)
    vh = _to_heads(pg[:, P_OFF + 3200:P_OFF + 3328].astype(BF16), ATTN_KV_HEADS)
    oh = _attn_fwd(qh, kh, vh)
    return _from_heads(oh).astype(BF16), (cos, sin, qh, kh, vh, oh)


def _attn_branch_bwd(pg, q_gain, k_gain, saved, dy):
    cos, sin, qh, kh, vh, oh = saved
    dqh, dkh, dvh = _attn_bwd(qh, kh, vh, oh, _to_heads(dy, ATTN_Q_HEADS))
    dq, dqg = _qk_prep_bwd(pg, CB_AQ, ATTN_WIDTH, q_gain, cos, sin, ATTN_HEAD_DIM ** -0.5, _from_heads(dqh),
                           name="attn_q_prep_bwd")
    dk, dkg = _qk_prep_bwd(pg, CB_AK, ATTN_KV_WIDTH, k_gain, cos, sin, 1.0, _from_heads(dkh), name="attn_k_prep_bwd")
    return dq, dk, _from_heads(dvh), {'q_norm': dqg, 'k_norm': dkg}


def _gate_cols():
    return [slice(i * D_MODEL, (i + 1) * D_MODEL) for i in range(3)]


def _mixer_fwd(x, lw):
    h = _rmsnorm_fwd(x, lw['mix_norm'])
    pg = _mm(h, lw['w_pg'], name="mix_in")
    y_s5, s_s5 = _s5_fwd(pg, lw['s5'], lw['s5_w_glu'])
    y_gla, s_gla = _gla_branch_fwd(pg, lw['gla_w_alpha'], lw['gla_b_alpha'], lw['gla_norm'])
    y_att, s_att = _attn_branch_fwd(pg, lw['attn_q_norm'], lw['attn_k_norm'])
    ys = (y_s5, y_gla, y_att)
    br = [_mm(y, lw[n], name="mix_branch") for y, n in zip(ys, ('w_branch_s5', 'w_branch_gla', 'w_branch_attn'))]

    def merge(g0, g1, g2, b0, b1, b2, bias):
        acc = None
        for g, b, c in zip((g0, g1, g2), (b0, b1, b2), _gate_cols()):
            term = _sigmoid(g + bias[:, c]) * b
            acc = term if acc is None else acc + term
        return acc
    merged = _rowmap(merge, [(pg, 0, D_MODEL), (pg, 1, D_MODEL), (pg, 2, D_MODEL)] + br,
                     [lw['b_merge_gate'].reshape(1, -1)], [(D_MODEL, BF16)], tl=256, name="mix_merge")[0]
    y = _mm(merged, lw['w_out'], name="mix_out")
    x_out = _rowmap(lambda a, b: a + b, [x, y], [], [(D_MODEL, F32)], tl=512, name="residual_add")[0]
    return x_out, (x, h, pg, ys, (s_s5, s_gla, s_att), br, merged)


def _mixer_bwd(saved, lw, dx_out):
    x, h, pg, ys, (s_s5, s_gla, s_att), br, merged = saved
    grads = {'w_out': _mm(merged, dx_out, ta=True, name="mix_dwout")}
    dmerged = _mm(dx_out, lw['w_out'], tb=True, name="mix_dmerged")

    def merge_bwd(g0, g1, g2, b0, b1, b2, dm, bias):
        dbr, dgp = [], []
        for g, b, c in zip((g0, g1, g2), (b0, b1, b2), _gate_cols()):
            s = _sigmoid(g + bias[:, c])
            dbr.append(dm * s)
            dgp.append(dm * b * (s * (1.0 - s)))
        dgp = jnp.concatenate(dgp, axis=1)
        return dbr[0], dbr[1], dbr[2], dgp, jnp.sum(dgp, axis=0, keepdims=True)
    d0, d1, d2, dgpre, dbias = _rowmap(
        merge_bwd, [(pg, 0, D_MODEL), (pg, 1, D_MODEL), (pg, 2, D_MODEL)] + br + [dmerged],
        [lw['b_merge_gate'].reshape(1, -1)], [(D_MODEL, BF16)] * 3 + [(GATE_WIDTH, BF16)], [(1, GATE_WIDTH)], tl=256,
        name="mix_merge_bwd")
    grads['b_merge_gate'] = dbias[0]
    dys = []
    for y, dbr, n in zip(ys, (d0, d1, d2), ('w_branch_s5', 'w_branch_gla', 'w_branch_attn')):
        grads[n] = _mm(y, dbr, ta=True, name="mix_dwbranch")
        dys.append(_mm(dbr, lw[n], tb=True, name="mix_dy"))
    du, g_s5 = _s5_bwd(pg, lw['s5'], lw['s5_w_glu'], s_s5, dys[0])
    dgq, dgk, dgv, dgg, dz, g_gla = _gla_branch_bwd(pg, lw['gla_w_alpha'], lw['gla_b_alpha'], lw['gla_norm'], s_gla, dys[1])
    daq, dak, dav, g_att = _attn_branch_bwd(pg, lw['attn_q_norm'], lw['attn_k_norm'], s_att, dys[2])

    def assemble(dgp, u0, u1, u2, q0, q1, k0, k1, v0, v1, gg, aq, ak, av, z):
        pad = jnp.zeros((dgp.shape[0], IN_PAD - 3456), F32)
        parts = [dgp.astype(F32), u0 + u1 + u2, q0 + q1, k0 + k1, v0 + v1, gg, aq, ak, av, z, pad]
        return jnp.concatenate(parts, axis=1)
    dpg = _rowmap(assemble, [dgpre] + du + dgq + dgk + dgv + [dgg, daq, dak, dav, dz], [], [(PG_WIDTH, BF16)], tl=256,
                  name="mix_dpg")[0]
    grads['w_pg'] = _mm(h, dpg, ta=True, name="mix_dwpg")
    dh = _mm(dpg, lw['w_pg'], tb=True, name="mix_dh")
    dx, grads['mix_norm'] = _rmsnorm_bwd(x, lw['mix_norm'], dh, dx_out)
    grads['s5'], grads['gla'], grads['attn'] = g_s5, g_gla, g_att
    return dx, grads


def _loss_head(x, gain, target):
    width = x.shape[1]

    def fn(xv, tv, g):
        r = _rms(xv)
        err = xv * r * g - tv
        dy = err * (1.0 / width)
        gd = dy * g
        dx = r * gd - xv * (r * r * r) * jnp.mean(xv * gd, axis=-1, keepdims=True)
        loss = jnp.sum(0.5 * jnp.mean(err * err, axis=-1, keepdims=True), axis=0, keepdims=True)
        return dx, jnp.broadcast_to(loss, (1, 128)), jnp.sum(dy * xv * r, axis=0, keepdims=True)
    dx, loss, dgain = _rowmap(fn, [x, target], [gain.reshape(1, -1)], [(width, F32)], [(1, 128), (1, width)], tl=256,
                              name="loss_head")
    return loss[0, 0], dx, dgain[0]


def _row_tile(rows, cap=256):
    for t in range(cap - cap % 16, 0, -16):
        if rows % t == 0:
            return t
    return rows


def _reduce_adamw(parts, w, m, v, *, name):
    _, r, c = parts.shape
    tr = _row_tile(r)

    def body(p_ref, w_ref, m_ref, v_ref, g_ref, d_ref, m2_ref, v2_ref):
        g = p_ref[0].astype(F32)
        for j in range(1, N_DEV):
            g = g + p_ref[j].astype(F32)
        m2 = ADAM_B1 * m_ref[...] + (1.0 - ADAM_B1) * g
        v2 = ADAM_B2 * v_ref[...] + (1.0 - ADAM_B2) * (g * g)
        m_hat = m2 / (1.0 - ADAM_B1 ** ADAM_STEP)
        v_hat = v2 / (1.0 - ADAM_B2 ** ADAM_STEP)
        g_ref[...] = g
        d_ref[...] = -ADAM_LR * (m_hat / (jnp.sqrt(v_hat) + ADAM_EPS) + ADAM_WD * w_ref[...])
        m2_ref[...] = m2
        v2_ref[...] = v2

    flat = pl.BlockSpec((tr, c), lambda i: (i, 0))
    return pl.pallas_call(
        body, name=name, grid=(r // tr,), in_specs=[pl.BlockSpec((N_DEV, tr, c), lambda i: (0, i, 0)), flat, flat, flat],
        out_specs=[flat] * 4, out_shape=[jax.ShapeDtypeStruct((r, c), F32)] * 4,
        compiler_params=pltpu.CompilerParams(dimension_semantics=("parallel",)),
    )(parts, w, m, v)


def _all_gather(blocks, *, name):
    n = len(blocks)

    def body(*refs):
        x_refs, out_refs = refs[:n], refs[n:2 * n]
        send_sems, recv_sems, local_sems = refs[2 * n:]
        x, y, c = lax.axis_index("x"), lax.axis_index("y"), lax.axis_index("c")
        me, sibling = (x, y, c), (x, y, 1 - c)
        chips = [(1 - x, y), (x, 1 - y), (1 - x, 1 - y)]

        def slot(t, px, py, pc):
            return out_refs[t].at[4 * px + 2 * py + pc]

        def copy(t, k, blk, to, own=False):
            return pltpu.make_async_remote_copy(
                src_ref=x_refs[t] if own else slot(t, *blk), dst_ref=slot(t, *blk), send_sem=send_sems.at[t, k],
                recv_sem=recv_sems.at[t, k], device_id=to, device_id_type=pl.DeviceIdType.MESH)

        mine = [pltpu.make_async_copy(x_refs[t], slot(t, *me), local_sems.at[t]) for t in range(n)]
        for cp in mine:
            cp.start()
        first = []
        for t in range(n):
            first.append(copy(t, 0, me, sibling, own=True))
            first += [copy(t, 1 + j, me, (*chip, c), own=True) for j, chip in enumerate(chips)]
        for cp in first:
            cp.start()
        passed = []
        for j, chip in enumerate(chips):
            for t in range(n):
                copy(t, 1 + j, (*chip, c), me).wait_recv()
                passed.append(copy(t, 4 + j, (*chip, c), sibling))
                passed[-1].start()
        for t in range(n):
            copy(t, 0, sibling, me).wait_recv()
        for j, chip in enumerate(chips):
            for t in range(n):
                copy(t, 4 + j, (*chip, 1 - c), me).wait_recv()
        for cp in first + passed:
            cp.wait_send()
        for cp in mine:
            cp.wait()

    hbm = pl.BlockSpec(memory_space=pl.ANY)
    return pl.pallas_call(
        body, name=name, out_shape=[jax.ShapeDtypeStruct((N_DEV,) + b.shape, b.dtype) for b in blocks],
        in_specs=[hbm] * n, out_specs=[hbm] * n,
        scratch_shapes=[pltpu.SemaphoreType.DMA((n, 7)), pltpu.SemaphoreType.DMA((n, 7)), pltpu.SemaphoreType.DMA((n,))],
    )(*blocks)


def _all_to_all(stacks, *, name):
    n = len(stacks)

    def body(*refs):
        g_refs, out_refs = refs[:n], refs[n:2 * n]
        send_sems, recv_sems, local_sems = refs[2 * n:]
        x, y, c = lax.axis_index("x"), lax.axis_index("y"), lax.axis_index("c")
        me = 4 * x + 2 * y + c
        mine = [pltpu.make_async_copy(g_refs[t].at[me], out_refs[t].at[me], local_sems.at[t]) for t in range(n)]
        for cp in mine:
            cp.start()
        copies = []
        for k in range(1, N_DEV):
            px, py, pc = x ^ (k >> 2 & 1), y ^ (k >> 1 & 1), c ^ (k & 1)
            for t in range(n):
                copies.append(pltpu.make_async_remote_copy(
                    src_ref=g_refs[t].at[4 * px + 2 * py + pc], dst_ref=out_refs[t].at[me], send_sem=send_sems.at[t, k - 1],
                    recv_sem=recv_sems.at[t, k - 1], device_id=(px, py, pc), device_id_type=pl.DeviceIdType.MESH))
        for cp in copies:
            cp.start()
        for cp in copies:
            cp.wait_recv()
        for cp in copies:
            cp.wait_send()
        for cp in mine:
            cp.wait()

    hbm = pl.BlockSpec(memory_space=pl.ANY)
    return pl.pallas_call(
        body, name=name, out_shape=[jax.ShapeDtypeStruct(s.shape, s.dtype) for s in stacks],
        in_specs=[hbm] * n, out_specs=[hbm] * n,
        scratch_shapes=[pltpu.SemaphoreType.DMA((n, 7)), pltpu.SemaphoreType.DMA((n, 7)), pltpu.SemaphoreType.DMA((n,))],
    )(*stacks)


SMALL_COLS = 128


def _pack_small(arrays):
    flat = jnp.concatenate([a.astype(F32).reshape(-1, SMALL_COLS) for a in arrays], axis=0)
    return jnp.pad(flat, ((0, -flat.shape[0] % 256), (0, 0)))


def _unpack_small(packed, shapes):
    out, off = [], 0
    for s in shapes:
        r = math.prod(s) // SMALL_COLS
        out.append(packed[off:off + r].reshape(s))
        off += r
    return out


def _split_shards(full, axis):
    shape = full.shape
    split = full.reshape(shape[:axis] + (N_DEV, shape[axis] // N_DEV) + shape[axis + 1:])
    return jnp.moveaxis(split, axis, 0)


def _join_shards(stack, axis):
    moved = jnp.moveaxis(stack, 0, axis)
    shape = moved.shape
    return moved.reshape(shape[:axis] + (shape[axis] * shape[axis + 1],) + shape[axis + 2:])


def _w_in_padded(w_in):
    pad = jnp.zeros(w_in.shape[:-1] + (IN_PAD - IN_WIDTH,), w_in.dtype)
    return jnp.concatenate([w_in[..., :2560], w_in[..., 2592:], w_in[..., 2560:2592], pad], axis=-1)


def _w_in_unpadded(w):
    return jnp.concatenate([w[..., :2560], w[..., 3328:3360], w[..., 2560:3328]], axis=-1)


S5_KEYS = ('lambda_re', 'lambda_im', 'log_dt', 'b_re', 'b_im', 'c_re', 'c_im', 'd')


def _layer_weights(full, w, i):
    lw = {n: w[n][i] for n in ('ffn1_norm', 'mix_norm', 'gla_norm', 'attn_q_norm', 'attn_k_norm', 'b_merge_gate', 'ffn2_norm')}
    lw['s5'] = {k: w['s5_' + k][i] for k in S5_KEYS}
    for f in ('ffn1', 'ffn2'):
        lw[f + '_w_gu'] = jnp.concatenate([full[f + '_w_gate'][i], full[f + '_w_up'][i]], axis=1)
        lw[f + '_w_down'] = full[f + '_w_down'][i]
    lw['w_pg'] = jnp.concatenate([full['w_merge_gate'][i], _w_in_padded(full['w_in'][i])], axis=1)
    for n in ('s5_w_glu', 'gla_w_alpha', 'w_branch_s5', 'w_branch_gla', 'w_branch_attn', 'w_out'):
        lw[n] = full[n][i]
    lw['gla_b_alpha'] = full['gla_b_alpha'][i].astype(F32)
    return lw


def _step_local(x, target, w, full):
    lws = [_layer_weights(full, w, i) for i in range(DEPTH)]
    saved = []
    for lw in lws:
        x, s1 = _ffn_fwd(x, lw['ffn1_norm'], lw['ffn1_w_gu'], lw['ffn1_w_down'])
        x, s2 = _mixer_fwd(x, lw)
        x, s3 = _ffn_fwd(x, lw['ffn2_norm'], lw['ffn2_w_gu'], lw['ffn2_w_down'])
        saved.append((s1, s2, s3))
    loss, dx, d_final = _loss_head(x, w['final_norm'], target)
    per_layer = []
    for lw, (s1, s2, s3) in reversed(list(zip(lws, saved))):
        g = {}
        dx, g['ffn2_norm'], dgu, g['ffn2_w_down'] = _ffn_bwd(s3, lw['ffn2_norm'], lw['ffn2_w_gu'], lw['ffn2_w_down'], dx)
        g['ffn2_w_gate'], g['ffn2_w_up'] = dgu[:, :D_FF], dgu[:, D_FF:]
        dx, gm = _mixer_bwd(s2, lw, dx)
        dx, g['ffn1_norm'], dgu, g['ffn1_w_down'] = _ffn_bwd(s1, lw['ffn1_norm'], lw['ffn1_w_gu'], lw['ffn1_w_down'], dx)
        g['ffn1_w_gate'], g['ffn1_w_up'] = dgu[:, :D_FF], dgu[:, D_FF:]
        g['w_merge_gate'] = gm['w_pg'][:, :GATE_WIDTH]
        g['w_in'] = _w_in_unpadded(gm['w_pg'][:, GATE_WIDTH:])
        for n in ('w_out', 'b_merge_gate', 'w_branch_s5', 'w_branch_gla', 'w_branch_attn', 'mix_norm'):
            g[n] = gm[n]
        for k in S5_KEYS:
            g['s5_' + k] = gm['s5'][k]
        g['s5_w_glu'] = gm['s5']['w_glu']
        g['gla_w_alpha'], g['gla_b_alpha'], g['gla_norm'] = gm['gla']['w_alpha'], gm['gla']['b_alpha'], gm['gla']['norm']
        g['attn_q_norm'], g['attn_k_norm'] = gm['attn']['q_norm'], gm['attn']['k_norm']
        per_layer.append(g)
    per_layer.reverse()
    return loss, dx, per_layer, d_final


def kernel(x, ffn1_norm, ffn1_w_gate, ffn1_w_up, ffn1_w_down, mix_norm, w_in, s5_lambda_re, s5_lambda_im, s5_log_dt, s5_b_re, s5_b_im, s5_c_re, s5_c_im, s5_d, s5_w_glu, gla_w_alpha, gla_b_alpha, gla_norm, attn_q_norm, attn_k_norm, w_branch_s5, w_branch_gla, w_branch_attn, w_merge_gate, b_merge_gate, w_out, ffn2_norm, ffn2_w_gate, ffn2_w_up, ffn2_w_down, final_norm, loss_target, m_ffn1_norm, m_ffn1_w_gate, m_ffn1_w_up, m_ffn1_w_down, m_mix_norm, m_w_in, m_s5_lambda_re, m_s5_lambda_im, m_s5_log_dt, m_s5_b_re, m_s5_b_im, m_s5_c_re, m_s5_c_im, m_s5_d, m_s5_w_glu, m_gla_w_alpha, m_gla_b_alpha, m_gla_norm, m_attn_q_norm, m_attn_k_norm, m_w_branch_s5, m_w_branch_gla, m_w_branch_attn, m_w_merge_gate, m_b_merge_gate, m_w_out, m_ffn2_norm, m_ffn2_w_gate, m_ffn2_w_up, m_ffn2_w_down, m_final_norm, v_ffn1_norm, v_ffn1_w_gate, v_ffn1_w_up, v_ffn1_w_down, v_mix_norm, v_w_in, v_s5_lambda_re, v_s5_lambda_im, v_s5_log_dt, v_s5_b_re, v_s5_b_im, v_s5_c_re, v_s5_c_im, v_s5_d, v_s5_w_glu, v_gla_w_alpha, v_gla_b_alpha, v_gla_norm, v_attn_q_norm, v_attn_k_norm, v_w_branch_s5, v_w_branch_gla, v_w_branch_attn, v_w_merge_gate, v_b_merge_gate, v_w_out, v_ffn2_norm, v_ffn2_w_gate, v_ffn2_w_up, v_ffn2_w_down, v_final_norm):
    return _train_step(x, ffn1_norm, ffn1_w_gate, ffn1_w_up, ffn1_w_down, mix_norm, w_in, s5_lambda_re, s5_lambda_im, s5_log_dt, s5_b_re, s5_b_im, s5_c_re, s5_c_im, s5_d, s5_w_glu, gla_w_alpha, gla_b_alpha, gla_norm, attn_q_norm, attn_k_norm, w_branch_s5, w_branch_gla, w_branch_attn, w_merge_gate, b_merge_gate, w_out, ffn2_norm, ffn2_w_gate, ffn2_w_up, ffn2_w_down, final_norm, loss_target, m_ffn1_norm, m_ffn1_w_gate, m_ffn1_w_up, m_ffn1_w_down, m_mix_norm, m_w_in, m_s5_lambda_re, m_s5_lambda_im, m_s5_log_dt, m_s5_b_re, m_s5_b_im, m_s5_c_re, m_s5_c_im, m_s5_d, m_s5_w_glu, m_gla_w_alpha, m_gla_b_alpha, m_gla_norm, m_attn_q_norm, m_attn_k_norm, m_w_branch_s5, m_w_branch_gla, m_w_branch_attn, m_w_merge_gate, m_b_merge_gate, m_w_out, m_ffn2_norm, m_ffn2_w_gate, m_ffn2_w_up, m_ffn2_w_down, m_final_norm, v_ffn1_norm, v_ffn1_w_gate, v_ffn1_w_up, v_ffn1_w_down, v_mix_norm, v_w_in, v_s5_lambda_re, v_s5_lambda_im, v_s5_log_dt, v_s5_b_re, v_s5_b_im, v_s5_c_re, v_s5_c_im, v_s5_d, v_s5_w_glu, v_gla_w_alpha, v_gla_b_alpha, v_gla_norm, v_attn_q_norm, v_attn_k_norm, v_w_branch_s5, v_w_branch_gla, v_w_branch_attn, v_w_merge_gate, v_b_merge_gate, v_w_out, v_ffn2_norm, v_ffn2_w_gate, v_ffn2_w_up, v_ffn2_w_down, v_final_norm)


def _train_step(*args):
    nw = len(W_NAMES)
    x, target = args[0][0], args[1 + nw][0]
    w = dict(zip(W_NAMES, args[1:1 + nw]))
    m = dict(zip(W_NAMES, args[2 + nw:2 + 2 * nw]))
    v = dict(zip(W_NAMES, args[2 + 2 * nw:2 + 3 * nw]))

    gathered = _all_gather([w[n].astype(BF16) for n in SHARDED], name="gather_weights")
    full = {n: _join_shards(g, SHARD_AXIS[n]) for n, g in zip(SHARDED, gathered)}

    loss, dx, per_layer, d_final = _step_local(x, target, w, full)
    loss = lax.psum(loss, ("x", "y", "c"))

    out = {}
    kinds = ('grad', 'delta', 'new_m', 'new_v')
    outgoing = [jnp.stack([_split_shards(g[n], SHARD_AXIS[n] - 1) for g in per_layer], axis=1).astype(BF16) for n in SHARDED]
    incoming = _all_to_all(outgoing, name="exchange_grads")
    for n, parts in zip(SHARDED, incoming):
        shape = w[n].shape
        flat = lambda a: a.reshape(-1, shape[-1])
        res = _reduce_adamw(parts.reshape(N_DEV, -1, shape[-1]), flat(w[n]), flat(m[n]), flat(v[n]), name="adamw_sharded")
        for kind, a in zip(kinds, res):
            out[kind + '_' + n] = a.reshape(shape)
    small = [jnp.stack([g[n] for g in per_layer]) if n != 'final_norm' else d_final for n in REPLICATED]
    parts = _all_gather([_pack_small(small)], name="gather_small_grads")[0]
    res = _reduce_adamw(parts, *[_pack_small([d[n] for n in REPLICATED]) for d in (w, m, v)], name="adamw_replicated")
    for kind, packed in zip(kinds, res):
        for n, a in zip(REPLICATED, _unpack_small(packed, [w[n].shape for n in REPLICATED])):
            out[kind + '_' + n] = a
    return (loss, dx[None]) + tuple(out[kind + '_' + n] for kind in kinds for n in W_NAMES)
```

```python
import functools
import math

import jax
import jax.numpy as jnp
import numpy as np
from jax import lax
from jax.experimental import pallas as pl
from jax.experimental.pallas import tpu as pltpu

F32 = jnp.float32
BF16 = jnp.bfloat16

N_DEV = 8
D_MODEL = 1024
DEPTH = 2
GRID_W = 64
D_FF = 2816
NORM_EPS = 1e-6
S5_GROUPS = 32
S5_GROUP_CH = 16
S5_STATE = 64
S5_WIDTH = 512
S5_NSTATE = S5_GROUPS * S5_STATE
S5_LANE_BLOCK = 512
GLA_HEADS = 4
GLA_HEAD_DIM = 128
GLA_WIDTH = 512
GLA_LOWRANK = 16
GLA_TAU = 16.0
GLA_CHUNK = 64
ATTN_Q_HEADS = 8
ATTN_KV_HEADS = 2
ATTN_HEAD_DIM = 64
ATTN_WIDTH = 512
ATTN_KV_WIDTH = 128
ROPE_BASE = 10000.0
IN_SPLITS = (512, 512, 512, 512, 512, 16, 16, 512, 128, 128)
IN_WIDTH = sum(IN_SPLITS)
IN_PAD = 3584
GATE_WIDTH = 3 * D_MODEL
PG_WIDTH = GATE_WIDTH + IN_PAD
P_OFF = GATE_WIDTH
CB_U, CB_GQ, CB_GK, CB_GV, CB_GG, CB_AQ = (P_OFF // 512 + i for i in range(6))
CB_AK, CB_AV, CB_Z = (P_OFF + 3072) // 128, (P_OFF + 3200) // 128, (P_OFF + 3328) // 128
ADAM_LR = 0.001
ADAM_B1 = 0.9
ADAM_B2 = 0.999
ADAM_EPS = 1e-08
ADAM_WD = 0.01
ADAM_STEP = 10
PACK_COLS = 1024

W_NAMES = ['ffn1_norm', 'ffn1_w_gate', 'ffn1_w_up', 'ffn1_w_down', 'mix_norm', 'w_in', 's5_lambda_re', 's5_lambda_im',
           's5_log_dt', 's5_b_re', 's5_b_im', 's5_c_re', 's5_c_im', 's5_d', 's5_w_glu', 'gla_w_alpha', 'gla_b_alpha',
           'gla_norm', 'attn_q_norm', 'attn_k_norm', 'w_branch_s5', 'w_branch_gla', 'w_branch_attn', 'w_merge_gate',
           'b_merge_gate', 'w_out', 'ffn2_norm', 'ffn2_w_gate', 'ffn2_w_up', 'ffn2_w_down', 'final_norm']
SHARD_AXIS = {'ffn1_w_gate': 2, 'ffn1_w_up': 2, 'ffn1_w_down': 1, 'w_in': 2, 's5_w_glu': 1, 'gla_w_alpha': 3,
              'gla_b_alpha': 2, 'w_branch_s5': 2, 'w_branch_gla': 2, 'w_branch_attn': 2, 'w_merge_gate': 2,
              'w_out': 1, 'ffn2_w_gate': 2, 'ffn2_w_up': 2, 'ffn2_w_down': 1}
SHARDED = [n for n in W_NAMES if n in SHARD_AXIS]
REPLICATED = [n for n in W_NAMES if n not in SHARD_AXIS]


def _pick(dim, prefs):
    for p in prefs:
        if dim % p == 0:
            return p
    return dim


def _sigmoid(x):
    return 1.0 / (1.0 + jnp.exp(-x))


def _mm(a, b, *, ta=False, tb=False, out_dtype=F32, scale=None, add=None, name):
    a, a_cb, a_w = a if isinstance(a, tuple) else (a, 0, a.shape[1])
    b, b_cb, b_w = b if isinstance(b, tuple) else (b, 0, b.shape[1])
    m, k = (a_w, a.shape[0]) if ta else (a.shape[0], a_w)
    n = b.shape[0] if tb else b_w
    assert (b_w if tb else b.shape[0]) == k, (a.shape, b.shape, ta, tb)
    tm, tn, tk = _mm_tiles(m, n, k, a.dtype.itemsize, b.dtype.itemsize, jnp.dtype(out_dtype).itemsize)
    nk = k // tk
    dims = (((0 if ta else 1,), (1 if tb else 0,)), ((), ()))
    a_off = a_cb * (a_w // (tm if ta else tk))
    b_off = b_cb * (b_w // (tk if tb else tn))

    def body(a_ref, b_ref, *rest):
        add_ref = rest[0] if add is not None else None
        o_ref, *acc = rest[1:] if add is not None else rest

        def finish(res):
            res = res if scale is None else res * scale
            return (res if add_ref is None else res + add_ref[...]).astype(out_dtype)

        part = lax.dot_general(a_ref[...].astype(BF16), b_ref[...].astype(BF16), dims, preferred_element_type=F32)
        if nk == 1:
            o_ref[...] = finish(part)
            return
        acc_ref, = acc
        kk = pl.program_id(2)

        @pl.when(kk == 0)
        def _():
            acc_ref[...] = part

        @pl.when(kk > 0)
        def _():
            acc_ref[...] += part

        @pl.when(kk == nk - 1)
        def _():
            o_ref[...] = finish(acc_ref[...])

    a_spec = (pl.BlockSpec((tk, tm), lambda i, j, kk: (kk, i + a_off)) if ta
              else pl.BlockSpec((tm, tk), lambda i, j, kk: (i, kk + a_off)))
    b_spec = (pl.BlockSpec((tn, tk), lambda i, j, kk: (j, kk + b_off)) if tb
              else pl.BlockSpec((tk, tn), lambda i, j, kk: (kk, j + b_off)))
    o_spec = pl.BlockSpec((tm, tn), lambda i, j, kk: (i, j))
    return pl.pallas_call(
        body, name=name, grid=(m // tm, n // tn, nk), in_specs=[a_spec, b_spec] + ([o_spec] if add is not None else []),
        out_specs=o_spec, out_shape=jax.ShapeDtypeStruct((m, n), out_dtype),
        scratch_shapes=[pltpu.VMEM((tm, tn), F32)] if nk > 1 else [],
        compiler_params=pltpu.CompilerParams(dimension_semantics=("parallel", "parallel", "arbitrary")),
    )(a, b, *([add] if add is not None else []))


MM_VMEM_BUDGET = 40 * 1024 * 1024


def _mm_tiles(m, n, k, a_bytes, b_bytes, out_bytes):
    tms = [t for t in (1024, 1408, 512, 256, 128) if m % t == 0] or [m]
    tns = [t for t in (512, 256, 128) if n % t == 0] or [n]
    tks = [k] + [t for t in (2048, 1024, 512, 256, 128) if k % t == 0 and t < k]
    for tk in tks:
        for tm in tms:
            for tn in tns:
                use = 2 * (tm * tk * a_bytes + tk * tn * b_bytes + tm * tn * out_bytes) + 2 * tm * tn * 4
                if use <= MM_VMEM_BUDGET:
                    return tm, tn, tk
    return tms[-1], tns[-1], tks[-1]


def _rowmap(fn, rows, consts, outs, reds=(), *, tl, name):
    rows = [r if isinstance(r, tuple) else (r, 0, r.shape[1]) for r in rows]
    length = rows[0][0].shape[0]
    tl = min(tl, length)
    nr, nc, no = len(rows), len(consts), len(outs)

    def body(*refs):
        res = fn(*[r[...] for r in refs[:nr + nc]])
        res = res if isinstance(res, tuple) else (res,)
        for o_ref, val in zip(refs[nr + nc:nr + nc + no], res[:no]):
            o_ref[...] = val.astype(o_ref.dtype)
        if reds:
            step = pl.program_id(0)
            red_refs = refs[nr + nc + no:]

            @pl.when(step == 0)
            def _():
                for d_ref, val in zip(red_refs, res[no:]):
                    d_ref[...] = val.astype(F32)

            @pl.when(step > 0)
            def _():
                for d_ref, val in zip(red_refs, res[no:]):
                    d_ref[...] += val.astype(F32)

    in_specs = [pl.BlockSpec((tl, w), lambda i, cb=cb: (i, cb)) for (_, cb, w) in rows]
    in_specs += [pl.BlockSpec(c.shape, lambda i, nd=c.ndim: (0,) * nd) for c in consts]
    out_specs = [pl.BlockSpec((tl, w), lambda i: (i, 0)) for (w, _) in outs]
    out_specs += [pl.BlockSpec(s, lambda i, nd=len(s): (0,) * nd) for s in reds]
    out_shape = [jax.ShapeDtypeStruct((length, w), dt) for (w, dt) in outs]
    out_shape += [jax.ShapeDtypeStruct(s, F32) for s in reds]
    res = pl.pallas_call(
        body, name=name, grid=(length // tl,), in_specs=in_specs, out_specs=out_specs, out_shape=out_shape,
        compiler_params=pltpu.CompilerParams(dimension_semantics=("arbitrary" if reds else "parallel",)),
    )(*[r[0] for r in rows], *consts)
    return res


def _rms(x):
    return lax.rsqrt(jnp.mean(x * x, axis=-1, keepdims=True) + NORM_EPS)


def _rmsnorm_fwd(x, gain):
    def fn(xv, g):
        return xv * _rms(xv) * g
    return _rowmap(fn, [x], [gain.reshape(1, -1)], [(x.shape[1], BF16)], tl=256, name="rmsnorm_fwd")[0]


def _rmsnorm_bwd(x, gain, dh, dres):
    def fn(xv, dhv, drv, g):
        r = _rms(xv)
        gd = dhv * g
        dx = r * gd - xv * (r * r * r) * jnp.mean(xv * gd, axis=-1, keepdims=True)
        return drv + dx, jnp.sum(dhv * xv * r, axis=0, keepdims=True)
    dx, dg = _rowmap(fn, [x, dh, dres], [gain.reshape(1, -1)], [(x.shape[1], F32)], [(1, x.shape[1])], tl=256,
                     name="rmsnorm_bwd")
    return dx, dg[0]


def _ffn_fwd(x, gain, w_gu, w_down):
    h = _rmsnorm_fwd(x, gain)
    gu = _mm(h, w_gu, name="ffn_gu")

    def act(g, u):
        return g * _sigmoid(g) * u
    a = _rowmap(act, [(gu, 0, D_FF), (gu, 1, D_FF)], [], [(D_FF, BF16)], tl=256, name="ffn_act")[0]
    x_out = _mm(a, w_down, scale=0.5, add=x, name="ffn_down")
    return x_out, (x, h, gu, a)


def _ffn_bwd(saved, gain, w_gu, w_down, dx_out):
    x, h, gu, a = saved
    dxo = dx_out.astype(BF16)
    d_wdown = _mm(a, dxo, ta=True, scale=0.5, name="ffn_dwdown")
    da = _mm(dxo, w_down, tb=True, scale=0.5, name="ffn_da")

    def act_bwd(g, u, dav):
        s = _sigmoid(g)
        return jnp.concatenate([dav * u * (s * (1.0 + g * (1.0 - s))), dav * (g * s)], axis=1)
    dgu = _rowmap(act_bwd, [(gu, 0, D_FF), (gu, 1, D_FF), da], [], [(2 * D_FF, BF16)], tl=256, name="ffn_act_bwd")[0]
    d_wgu = _mm(h, dgu, ta=True, name="ffn_dwgu")
    dh = _mm(dgu, w_gu, tb=True, name="ffn_dh")
    dx, dgain = _rmsnorm_bwd(x, gain, dh, dx_out)
    return dx, dgain, d_wgu, d_wdown


def _s5_col(n):
    return (n // S5_LANE_BLOCK) * 2 * S5_LANE_BLOCK + n % S5_LANE_BLOCK


def _s5_blocked(re, im):
    lead = re.shape[:-1]
    nb = S5_NSTATE // S5_LANE_BLOCK
    both = jnp.stack([re.reshape(*lead, nb, S5_LANE_BLOCK), im.reshape(*lead, nb, S5_LANE_BLOCK)], axis=-2)
    return both.reshape(*lead, 2 * S5_NSTATE)


def _s5_unblocked(z):
    lead = z.shape[:-1]
    nb = S5_NSTATE // S5_LANE_BLOCK
    both = z.reshape(*lead, nb, 2, S5_LANE_BLOCK)
    return both[..., 0, :].reshape(*lead, S5_NSTATE), both[..., 1, :].reshape(*lead, S5_NSTATE)


def _s5_tables(a_re, a_im, reverse):
    a = lax.complex(a_re, a_im)
    a2 = a * a
    a4 = a2 * a2
    rows = jnp.arange(8)
    pw = [a]
    for _ in range(7):
        pw.append(pw[-1] * a)
    pw = jnp.stack(pw)
    if reverse:
        pw = pw[::-1]
    tabs = []
    for coef, s in ((a, 1), (a2, 2), (a4, 4)):
        live = (rows <= 7 - s) if reverse else (rows >= s)
        tabs.append(jnp.where(live[:, None], coef[None, :], 0.0))
    tabs.append(pw)
    tabs = jnp.stack(tabs)
    return _s5_blocked(jnp.real(tabs), jnp.imag(tabs))


def _s5_scan_tile(v, tab_ref, prev, reverse):
    lb = S5_LANE_BLOCK
    vr, vi = v[:, :lb], v[:, lb:]
    for idx, s in enumerate((1, 2, 4)):
        cr, ci = tab_ref[idx, :, :lb], tab_ref[idx, :, lb:]
        sh = 8 - s if reverse else s
        sr, si = pltpu.roll(vr, sh, 0), pltpu.roll(vi, sh, 0)
        vr, vi = vr + cr * sr - ci * si, vi + cr * si + ci * sr
    row = 0 if reverse else 7
    pr = jnp.broadcast_to(prev[row:row + 1, :lb], (8, lb))
    pi = jnp.broadcast_to(prev[row:row + 1, lb:], (8, lb))
    cr, ci = tab_ref[3, :, :lb], tab_ref[3, :, lb:]
    return jnp.concatenate([vr + cr * pr - ci * pi, vi + cr * pi + ci * pr], axis=1)


def _s5_scan(v, tabs, reverse, *, name):
    length = v.shape[0]
    tb = min(512, length)
    ntb = length // tb
    nlb = S5_NSTATE // S5_LANE_BLOCK
    wb = 2 * S5_LANE_BLOCK
    ntile = tb // 8

    def body(tab_ref, v_ref, x_ref, carry_ref):
        @pl.when(pl.program_id(1) == 0)
        def _():
            carry_ref[...] = jnp.zeros_like(carry_ref)

        def step(i, prev):
            r0 = pl.multiple_of((ntile - 1 - i if reverse else i) * 8, 8)
            x = _s5_scan_tile(v_ref[pl.ds(r0, 8), :], tab_ref, prev, reverse)
            x_ref[pl.ds(r0, 8), :] = x
            return x

        carry_ref[...] = lax.fori_loop(0, ntile, step, carry_ref[...])

    tmap = (lambda c, t: (ntb - 1 - t, c)) if reverse else (lambda c, t: (t, c))
    return pl.pallas_call(
        body, name=name, grid=(nlb, ntb),
        in_specs=[pl.BlockSpec((4, 8, wb), lambda c, t: (0, 0, c)), pl.BlockSpec((tb, wb), tmap)],
        out_specs=pl.BlockSpec((tb, wb), tmap), out_shape=jax.ShapeDtypeStruct(v.shape, F32),
        scratch_shapes=[pltpu.VMEM((8, wb), F32)],
        compiler_params=pltpu.CompilerParams(dimension_semantics=("parallel", "arbitrary")),
    )(tabs, v)


def _s5_scan_adjoint(g, xs, tabs_conj, reverse, *, name):
    length = g.shape[0]
    tb = min(512, length)
    ntb = length // tb
    nlb = S5_NSTATE // S5_LANE_BLOCK
    lb = S5_LANE_BLOCK
    wb = 2 * lb
    ntile = tb // 8
    adj_rev = not reverse
    if reverse:
        edge = jnp.concatenate([xs[tb::tb], jnp.zeros((1, xs.shape[1]), F32)], axis=0)
    else:
        edge = jnp.concatenate([jnp.zeros((1, xs.shape[1]), F32), xs[tb - 1:length - 1:tb]], axis=0)
    edge = edge.reshape(ntb, 1, xs.shape[1])

    def body(tab_ref, g_ref, x_ref, edge_ref, lam_ref, da_ref, carry_ref):
        @pl.when(pl.program_id(1) == 0)
        def _():
            carry_ref[...] = jnp.zeros_like(carry_ref)
            da_ref[...] = jnp.zeros_like(da_ref)

        rows = lax.broadcasted_iota(jnp.int32, (8, wb), 0)

        def step(i, carry):
            prev, acc = carry
            k = ntile - 1 - i if adj_rev else i
            r0 = pl.multiple_of(k * 8, 8)
            lam = _s5_scan_tile(g_ref[pl.ds(r0, 8), :], tab_ref, prev, adj_rev)
            lam_ref[pl.ds(r0, 8), :] = lam
            x = x_ref[pl.ds(r0, 8), :]
            if reverse:
                kn = jnp.minimum(k + 1, ntile - 1)
                nb = x_ref[pl.ds(pl.multiple_of(kn * 8, 8), 8), :][0:1, :]
                nb = jnp.where(k == ntile - 1, edge_ref[0], nb)
                xp = jnp.where(rows == 7, jnp.broadcast_to(nb, (8, wb)), pltpu.roll(x, 7, 0))
            else:
                kn = jnp.maximum(k - 1, 0)
                nb = x_ref[pl.ds(pl.multiple_of(kn * 8, 8), 8), :][7:8, :]
                nb = jnp.where(k == 0, edge_ref[0], nb)
                xp = jnp.where(rows == 0, jnp.broadcast_to(nb, (8, wb)), pltpu.roll(x, 1, 0))
            xr, xi, lr, li = xp[:, :lb], xp[:, lb:], lam[:, :lb], lam[:, lb:]
            acc = acc + jnp.concatenate([xr * lr + xi * li, xr * li - xi * lr], axis=1)
            return lam, acc

        last, acc = lax.fori_loop(0, ntile, step, (carry_ref[...], da_ref[...]))
        carry_ref[...] = last
        da_ref[...] = acc

    tmap = (lambda c, t: (ntb - 1 - t, c)) if adj_rev else (lambda c, t: (t, c))
    emap = (lambda c, t: (ntb - 1 - t, 0, c)) if adj_rev else (lambda c, t: (t, 0, c))
    return pl.pallas_call(
        body, name=name, grid=(nlb, ntb),
        in_specs=[pl.BlockSpec((4, 8, wb), lambda c, t: (0, 0, c)), pl.BlockSpec((tb, wb), tmap),
                  pl.BlockSpec((tb, wb), tmap), pl.BlockSpec((1, 1, wb), emap)],
        out_specs=[pl.BlockSpec((tb, wb), tmap), pl.BlockSpec((8, wb), lambda c, t: (0, c))],
        out_shape=[jax.ShapeDtypeStruct(g.shape, F32), jax.ShapeDtypeStruct((8, g.shape[1]), F32)],
        scratch_shapes=[pltpu.VMEM((8, wb), F32)],
        compiler_params=pltpu.CompilerParams(dimension_semantics=("parallel", "arbitrary")),
    )(tabs_conj, g, xs, edge)


def _s5_prep(lam_re, lam_im, log_dt, b_re, b_im):
    lam = lax.complex(lam_re, lam_im)
    dt = jnp.exp(log_dt)[:, None]
    lam_bar = jnp.exp(lam * dt)
    b_bar = ((lam_bar - 1.0) / lam)[..., None] * lax.complex(b_re, b_im)
    return (jnp.real(lam_bar).reshape(-1), jnp.imag(lam_bar).reshape(-1), jnp.real(b_bar), jnp.imag(b_bar))


S5_NBLK = S5_NSTATE // S5_LANE_BLOCK
S5_BLK_GROUPS = S5_GROUPS // S5_NBLK
S5_BLK_CH = S5_BLK_GROUPS * S5_GROUP_CH


def _s5_in_matrix(bb_re, bb_im):
    eye = jnp.eye(S5_BLK_GROUPS, dtype=F32)
    def dense(bb):
        b4 = bb.reshape(S5_NBLK, S5_BLK_GROUPS, S5_STATE, S5_GROUP_CH)
        return jnp.einsum('cgph,gk->cghkp', b4, eye).reshape(S5_NBLK, S5_BLK_CH, S5_LANE_BLOCK)
    return jnp.concatenate([dense(bb_re), dense(bb_im)], axis=-1)


def _s5_block_diagonal(d):
    d5 = d.reshape(S5_NBLK, S5_BLK_GROUPS, S5_GROUP_CH, S5_BLK_GROUPS, S5_STATE)
    eye = jnp.eye(S5_BLK_GROUPS, dtype=F32)
    return jnp.swapaxes(jnp.sum(d5 * eye[None, :, None, :, None], axis=1), 1, 2)


def _s5_in_matrix_grad(d_mat):
    def diag(d):
        return jnp.swapaxes(_s5_block_diagonal(d), 2, 3).reshape(S5_GROUPS, S5_STATE, S5_GROUP_CH)
    return diag(d_mat[..., :S5_LANE_BLOCK]), diag(d_mat[..., S5_LANE_BLOCK:])


def _s5_out_matrix(c_re, c_im):
    eye = jnp.eye(S5_BLK_GROUPS, dtype=F32)
    def dense(cc):
        c4 = cc.reshape(S5_NBLK, S5_BLK_GROUPS, S5_GROUP_CH, S5_STATE)
        return jnp.einsum('cghp,gk->cgpkh', c4, eye).reshape(S5_NBLK, S5_LANE_BLOCK, S5_BLK_CH)
    return jnp.concatenate([dense(c_re), dense(-c_im)], axis=1)


def _s5_out_matrix_grad(d_mat_t):
    def diag(d):
        return _s5_block_diagonal(d).reshape(S5_GROUPS, S5_GROUP_CH, S5_STATE)
    return diag(d_mat_t[..., :S5_LANE_BLOCK]), -diag(d_mat_t[..., S5_LANE_BLOCK:])


def _gmm(a, b, *, tb=False, name):
    arr, cb0, wa = a
    nblk = b.shape[0]
    wn = b.shape[1] if tb else b.shape[2]
    length = arr.shape[0]
    tm = _pick(length, (1024, 512, 256, 128))
    dims = (((1,), (1 if tb else 0,)), ((), ()))

    def body(a_ref, b_ref, o_ref):
        o_ref[...] = lax.dot_general(a_ref[...].astype(BF16), b_ref[0].astype(BF16), dims, preferred_element_type=F32)

    return pl.pallas_call(
        body, name=name, grid=(nblk, length // tm),
        in_specs=[pl.BlockSpec((tm, wa), lambda c, i: (i, cb0 + c)), pl.BlockSpec((1,) + b.shape[1:], lambda c, i: (c, 0, 0))],
        out_specs=pl.BlockSpec((tm, wn), lambda c, i: (i, c)), out_shape=jax.ShapeDtypeStruct((length, nblk * wn), F32),
        compiler_params=pltpu.CompilerParams(dimension_semantics=("parallel", "parallel")),
    )(arr, b)


def _gmm_tn(a, g, *, nblk, name):
    arr_a, cb_a, wa = a
    arr_g, cb_g, wg = g
    length = arr_a.shape[0]
    dims = (((0,), (0,)), ((), ()))

    def body(a_ref, g_ref, o_ref):
        o_ref[0] = lax.dot_general(a_ref[...].astype(BF16), g_ref[...].astype(BF16), dims, preferred_element_type=F32)

    return pl.pallas_call(
        body, name=name, grid=(nblk,),
        in_specs=[pl.BlockSpec((length, wa), lambda c: (0, cb_a + c)), pl.BlockSpec((length, wg), lambda c: (0, cb_g + c))],
        out_specs=pl.BlockSpec((1, wa, wg), lambda c: (c, 0, 0)), out_shape=jax.ShapeDtypeStruct((nblk, wa, wg), F32),
        compiler_params=pltpu.CompilerParams(dimension_semantics=("parallel",)),
    )(arr_a, arr_g)


def _gelu_parts(x):
    k = math.sqrt(2.0 / math.pi)
    inner = k * (x + 0.044715 * x * x * x)
    th = jnp.tanh(inner)
    return th, k * (1.0 + 3.0 * 0.044715 * x * x)


def _s5_fwd(p_in, prm, w_glu):
    dirs = []
    ys = []
    for d, reverse in ((0, False), (1, True)):
        a_re, a_im, bb_re, bb_im = _s5_prep(prm['lambda_re'][d], prm['lambda_im'][d], prm['log_dt'][d],
                                            prm['b_re'][d], prm['b_im'][d])
        b_mat = _s5_in_matrix(bb_re, bb_im).astype(BF16)
        c_mat = _s5_out_matrix(prm['c_re'][d], prm['c_im'][d]).astype(BF16)
        bu = _gmm((p_in, CB_U * (512 // S5_BLK_CH), S5_BLK_CH), b_mat, name="s5_bu")
        xs = _s5_scan(bu, _s5_tables(a_re, a_im, reverse), reverse, name="s5_scan_rev" if reverse else "s5_scan_fwd")
        ys.append(_gmm((xs, 0, 2 * S5_LANE_BLOCK), c_mat, name="s5_y"))
        dirs.append((a_re, a_im, b_mat, c_mat, xs))

    def post(yf, yb, u, dskip):
        ypre = yf + yb + dskip * u
        th, _ = _gelu_parts(ypre)
        return ypre, 0.5 * ypre * (1.0 + th)
    ypre, yg = _rowmap(post, [ys[0], ys[1], (p_in, CB_U, S5_WIDTH)], [prm['d'].reshape(1, -1)],
                       [(S5_WIDTH, F32), (S5_WIDTH, F32)], tl=512, name="s5_post")
    t = _mm(yg, w_glu, name="s5_glu_mm")

    def glu(ygv, tv):
        return ygv * _sigmoid(tv)
    y = _rowmap(glu, [yg, t], [], [(S5_WIDTH, BF16)], tl=512, name="s5_glu")[0]
    return y, (dirs, ypre, yg, t)


def _s5_bwd(pg, prm, w_glu, saved, dy):
    dirs, ypre, yg, t = saved

    def glu_bwd(dyv, ygv, tv):
        s = _sigmoid(tv)
        return dyv * ygv * s * (1.0 - s), dyv * s
    dt, dyg_direct = _rowmap(glu_bwd, [dy, yg, t], [], [(S5_WIDTH, BF16), (S5_WIDTH, F32)], tl=512, name="s5_glu_bwd")
    grads = {'w_glu': _mm(yg, dt, ta=True, name="s5_dwglu")}
    dyg_mm = _mm(dt, w_glu, tb=True, name="s5_dyg")

    def post_bwd(dyd, dym, yp, u, dskip):
        th, dinner = _gelu_parts(yp)
        dyp = (dyd + dym) * (0.5 * (1.0 + th) + 0.5 * yp * (1.0 - th * th) * dinner)
        return dyp, dyp * dskip, jnp.sum(dyp * u, axis=0, keepdims=True)
    dyp, du_skip, dd = _rowmap(post_bwd, [dyg_direct, dyg_mm, ypre, (pg, CB_U, S5_WIDTH)], [prm['d'].reshape(1, -1)],
                               [(S5_WIDTH, F32), (S5_WIDTH, F32)], [(1, S5_WIDTH)], tl=512, name="s5_post_bwd")
    grads['d'] = dd[0]
    du = [du_skip]
    per_dir = []
    for d, reverse in ((0, False), (1, True)):
        a_re, a_im, b_mat, c_mat, xs = dirs[d]
        dxs = _gmm((dyp, 0, S5_BLK_CH), c_mat, tb=True, name="s5_dxs")
        lam, da = _s5_scan_adjoint(dxs, xs, _s5_tables(a_re, -a_im, not reverse), reverse,
                                   name="s5_adjoint_rev" if reverse else "s5_adjoint_fwd")
        du.append(_gmm((lam, 0, 2 * S5_LANE_BLOCK), b_mat, tb=True, name="s5_du"))
        dbb_re, dbb_im = _s5_in_matrix_grad(_gmm_tn((pg, CB_U * (512 // S5_BLK_CH), S5_BLK_CH), (lam, 0, 2 * S5_LANE_BLOCK),
                                                    nblk=S5_NBLK, name="s5_dbmat"))
        dc_re, dc_im = _s5_out_matrix_grad(_gmm_tn((dyp, 0, S5_BLK_CH), (xs, 0, 2 * S5_LANE_BLOCK), nblk=S5_NBLK,
                                                   name="s5_dcmat"))
        da_re, da_im = _s5_unblocked(jnp.sum(da, axis=0))
        _, vjp = jax.vjp(_s5_prep, prm['lambda_re'][d], prm['lambda_im'][d], prm['log_dt'][d], prm['b_re'][d], prm['b_im'][d])
        per_dir.append(vjp((da_re, da_im, dbb_re, dbb_im)) + (dc_re, dc_im))
    for i, key in enumerate(('lambda_re', 'lambda_im', 'log_dt', 'b_re', 'b_im', 'c_re', 'c_im')):
        grads[key] = jnp.stack([per_dir[0][i], per_dir[1][i]])
    return du, grads


def _split3(x):
    hi = x.astype(BF16)
    r = x - hi.astype(F32)
    mid = r.astype(BF16)
    return hi, mid, (r - mid.astype(F32)).astype(BF16)


def _exact_dot(ones, x, dims):
    parts = [lax.dot_general(ones, p, dims, preferred_element_type=F32) for p in _split3(x)]
    return parts[0] + parts[1] + parts[2]


_NN = (((1,), (0,)), ((), ()))
_NT = (((1,), (1,)), ((), ()))
_TN = (((0,), (0,)), ((), ()))


def _dot(a, b, dims=_NN):
    return lax.dot_general(a.astype(BF16), b.astype(BF16), dims, preferred_element_type=F32)


def _gla_chunk_mask(reverse):
    rows = lax.broadcasted_iota(jnp.int32, (GLA_CHUNK, GLA_CHUNK), 0)
    cols = lax.broadcasted_iota(jnp.int32, (GLA_CHUNK, GLA_CHUNK), 1)
    return (cols >= rows) if reverse else (cols <= rows)


def _gla_fwd(pg, la, reverse, *, name):
    length = la.shape[0]
    nch = length // GLA_CHUNK
    scale = GLA_HEAD_DIM ** -0.5
    last = 0 if reverse else GLA_CHUNK - 1
    hd = GLA_HEAD_DIM

    def body(q_ref, k_ref, v_ref, la_ref, o_ref, sp_ref, st_ref):
        @pl.when(pl.program_id(0) == 0)
        def _():
            st_ref[...] = jnp.zeros_like(st_ref)

        mask = _gla_chunk_mask(reverse)
        b = _exact_dot(mask.astype(BF16), la_ref[...], _NN)
        sp_ref[0] = st_ref[...]
        outs = []
        for h in range(GLA_HEADS):
            sl = slice(h * hd, (h + 1) * hd)
            bh = b[:, sl]
            bl = bh[last:last + 1, :]
            k = k_ref[:, sl]
            v = v_ref[:, sl]
            qd = q_ref[:, sl] * scale * jnp.exp(bh)
            kd = k * jnp.exp(-bh)
            ke = k * jnp.exp(bl - bh)
            st = st_ref[sl, :]
            p = jnp.where(mask, _dot(qd, kd, _NT), 0.0)
            outs.append(_dot(p, v) + _dot(qd, st, _NT))
            st_ref[sl, :] = st * jnp.exp(bl) + _dot(v, ke, _TN)
        o_ref[...] = jnp.concatenate(outs, axis=1)

    cmap = (lambda n: nch - 1 - n) if reverse else (lambda n: n)
    col = lambda cb: pl.BlockSpec((GLA_CHUNK, GLA_WIDTH), lambda n, cb=cb: (cmap(n), cb))
    return pl.pallas_call(
        body, name=name, grid=(nch,),
        in_specs=[col(CB_GQ), col(CB_GK), col(CB_GV), col(0)],
        out_specs=[col(0), pl.BlockSpec((1, GLA_WIDTH, hd), lambda n: (cmap(n), 0, 0))],
        out_shape=[jax.ShapeDtypeStruct((length, GLA_WIDTH), F32), jax.ShapeDtypeStruct((nch, GLA_WIDTH, hd), F32)],
        scratch_shapes=[pltpu.VMEM((GLA_WIDTH, hd), F32)],
        compiler_params=pltpu.CompilerParams(dimension_semantics=("arbitrary",)),
    )(pg, pg, pg, la)


def _gla_bwd(pg, la, do, sprev, reverse, *, name):
    length = la.shape[0]
    nch = length // GLA_CHUNK
    scale = GLA_HEAD_DIM ** -0.5
    last = 0 if reverse else GLA_CHUNK - 1
    hd = GLA_HEAD_DIM

    def body(q_ref, k_ref, v_ref, la_ref, do_ref, sp_ref, dq_ref, dk_ref, dv_ref, dla_ref, dst_ref):
        @pl.when(pl.program_id(0) == 0)
        def _():
            dst_ref[...] = jnp.zeros_like(dst_ref)

        mask = _gla_chunk_mask(reverse)
        tri = mask.astype(BF16)
        b = _exact_dot(tri, la_ref[...], _NN)
        is_last = lax.broadcasted_iota(jnp.int32, (GLA_CHUNK, hd), 0) == last
        dqs, dks, dvs, dbs = [], [], [], []
        for h in range(GLA_HEADS):
            sl = slice(h * hd, (h + 1) * hd)
            bh = b[:, sl]
            bl = bh[last:last + 1, :]
            eb, enb, ebl, el = jnp.exp(bh), jnp.exp(-bh), jnp.exp(bl - bh), jnp.exp(bl)
            k = k_ref[:, sl]
            v = v_ref[:, sl]
            dov = do_ref[:, sl]
            qd = q_ref[:, sl] * scale * eb
            kd = k * enb
            ke = k * ebl
            st = sp_ref[0, sl, :]
            dst = dst_ref[sl, :]
            p = jnp.where(mask, _dot(qd, kd, _NT), 0.0)
            dp = jnp.where(mask, _dot(dov, v, _NT), 0.0)
            dqd = _dot(dp, kd) + _dot(dov, st)
            dkd = _dot(dp, qd, _TN)
            dvs.append(_dot(p, dov, _TN) + _dot(ke, dst, _NT))
            dke = _dot(v, dst)
            dst_ref[sl, :] = dst * el + _dot(dov, qd, _TN)
            dbl = el * jnp.sum(dst * st, axis=0, keepdims=True) + jnp.sum(dke * ke, axis=0, keepdims=True)
            db = dqd * qd - dkd * kd - dke * ke
            dbs.append(jnp.where(is_last, db + dbl, db))
            dqs.append(dqd * eb * scale)
            dks.append(dkd * enb + dke * ebl)
        dq_ref[...] = jnp.concatenate(dqs, axis=1)
        dk_ref[...] = jnp.concatenate(dks, axis=1)
        dv_ref[...] = jnp.concatenate(dvs, axis=1)
        tri_t = _gla_chunk_mask(not reverse).astype(BF16)
        dla_ref[...] = _exact_dot(tri_t, jnp.concatenate(dbs, axis=1), _NN)

    cmap = (lambda n: n) if reverse else (lambda n: nch - 1 - n)
    col = lambda cb: pl.BlockSpec((GLA_CHUNK, GLA_WIDTH), lambda n, cb=cb: (cmap(n), cb))
    wide = jax.ShapeDtypeStruct((length, GLA_WIDTH), F32)
    return pl.pallas_call(
        body, name=name, grid=(nch,),
        in_specs=[col(CB_GQ), col(CB_GK), col(CB_GV), col(0), col(0),
                  pl.BlockSpec((1, GLA_WIDTH, hd), lambda n: (cmap(n), 0, 0))],
        out_specs=[col(0)] * 4, out_shape=[wide] * 4,
        scratch_shapes=[pltpu.VMEM((GLA_WIDTH, hd), F32)],
        compiler_params=pltpu.CompilerParams(dimension_semantics=("arbitrary",)),
    )(pg, pg, pg, la, do, sprev)


def _log_sigmoid(x):
    return jnp.minimum(x, 0.0) - jnp.log(1.0 + jnp.exp(-jnp.abs(x)))


def _gla_alpha_padded(w_alpha):
    w = jnp.zeros((2, 128, GLA_WIDTH), w_alpha.dtype)
    w = w.at[0, 0:GLA_LOWRANK].set(w_alpha[0])
    return w.at[1, GLA_LOWRANK:2 * GLA_LOWRANK].set(w_alpha[1])


def _gla_branch_fwd(pg, w_alpha, b_alpha, norm_gain):
    wa = _gla_alpha_padded(w_alpha).astype(BF16)

    def gates(z, w, bias):
        return (_log_sigmoid(_dot(z, w[0]) + bias[0:1]) / GLA_TAU, _log_sigmoid(_dot(z, w[1]) + bias[1:2]) / GLA_TAU)
    la_f, la_b = _rowmap(gates, [(pg, CB_Z, 128)], [wa, b_alpha], [(GLA_WIDTH, F32), (GLA_WIDTH, F32)], tl=512,
                         name="gla_gates")
    o_f, sp_f = _gla_fwd(pg, la_f, False, name="gla_fwd")
    o_b, sp_b = _gla_fwd(pg, la_b, True, name="gla_fwd_rev")

    def post(of, ob, gate, gn):
        o = of + ob
        on = jnp.concatenate([o[:, s:s + GLA_HEAD_DIM] * _rms(o[:, s:s + GLA_HEAD_DIM]) * gn
                              for s in range(0, GLA_WIDTH, GLA_HEAD_DIM)], axis=1)
        return o, on * (gate * _sigmoid(gate))
    o, y = _rowmap(post, [o_f, o_b, (pg, CB_GG, GLA_WIDTH)], [norm_gain.reshape(1, -1)],
                   [(GLA_WIDTH, F32), (GLA_WIDTH, BF16)], tl=512, name="gla_post")
    return y, (wa, la_f, la_b, sp_f, sp_b, o)


def _gla_branch_bwd(pg, w_alpha, b_alpha, norm_gain, saved, dy):
    wa, la_f, la_b, sp_f, sp_b, o = saved

    def post_bwd(dyv, ov, gate, gn):
        s = _sigmoid(gate)
        dos, dgn, ons = [], [], []
        for c in range(0, GLA_WIDTH, GLA_HEAD_DIM):
            oh = ov[:, c:c + GLA_HEAD_DIM]
            r = _rms(oh)
            don = dyv[:, c:c + GLA_HEAD_DIM] * (gate[:, c:c + GLA_HEAD_DIM] * s[:, c:c + GLA_HEAD_DIM])
            gd = don * gn
            dos.append(r * gd - oh * (r * r * r) * jnp.mean(oh * gd, axis=-1, keepdims=True))
            dgn.append(jnp.sum(don * oh * r, axis=0, keepdims=True))
            ons.append(oh * r * gn)
        on = jnp.concatenate(ons, axis=1)
        dgate = dyv * on * (s * (1.0 + gate * (1.0 - s)))
        return jnp.concatenate(dos, axis=1), dgate, jnp.concatenate(dgn, axis=1)
    do, dgate, dgn = _rowmap(post_bwd, [dy, o, (pg, CB_GG, GLA_WIDTH)], [norm_gain.reshape(1, -1)],
                             [(GLA_WIDTH, F32), (GLA_WIDTH, F32)], [(1, GLA_WIDTH)], tl=512, name="gla_post_bwd")
    dq_f, dk_f, dv_f, dla_f = _gla_bwd(pg, la_f, do, sp_f, False, name="gla_bwd")
    dq_b, dk_b, dv_b, dla_b = _gla_bwd(pg, la_b, do, sp_b, True, name="gla_bwd_rev")

    def gates_bwd(z, dlf, dlb, w, bias):
        dz = jnp.zeros_like(z)
        dlogits, dbs = [], []
        for d, dl in ((0, dlf), (1, dlb)):
            logit = _dot(z, w[d]) + bias[d:d + 1]
            dlogit = dl * (1.0 / GLA_TAU) * _sigmoid(-logit)
            dz = dz + _dot(dlogit, w[d], _NT)
            dlogits.append(dlogit)
            dbs.append(jnp.sum(dlogit, axis=0, keepdims=True))
        return dz, dlogits[0], dlogits[1], dbs[0], dbs[1]
    dz, dlg_f, dlg_b, dba_f, dba_b = _rowmap(
        gates_bwd, [(pg, CB_Z, 128), dla_f, dla_b], [wa, b_alpha], [(128, F32), (GLA_WIDTH, BF16), (GLA_WIDTH, BF16)],
        [(1, GLA_WIDTH), (1, GLA_WIDTH)], tl=512, name="gla_gates_bwd")
    dwa_f = _mm(dlg_f, (pg, CB_Z, 128), ta=True, name="gla_dwalpha")
    dwa_b = _mm(dlg_b, (pg, CB_Z, 128), ta=True, name="gla_dwalpha")
    grads = {'w_alpha': jnp.stack([dwa_f[:, 0:GLA_LOWRANK].T, dwa_b[:, GLA_LOWRANK:2 * GLA_LOWRANK].T]),
             'b_alpha': jnp.concatenate([dba_f, dba_b], axis=0),
             'norm': jnp.sum(dgn.reshape(GLA_HEADS, GLA_HEAD_DIM), axis=0)}
    return [dq_f, dq_b], [dk_f, dk_b], [dv_f, dv_b], dgate, dz, grads


def _rope_tables(length):
    half = ATTN_HEAD_DIM // 2
    inv_freq = ROPE_BASE ** (-jnp.arange(half // 2, dtype=F32) * 2.0 / half)
    t = jnp.arange(length, dtype=jnp.int32)
    def one(pos):
        ang = pos.astype(F32)[:, None] * inv_freq[None, :]
        c, s = jnp.cos(ang), jnp.sin(ang)
        return jnp.concatenate([c, c], axis=1), jnp.concatenate([-s, s], axis=1)
    c_r, s_r = one(t // GRID_W)
    c_c, s_c = one(t % GRID_W)
    return jnp.concatenate([c_r, c_c], axis=1), jnp.concatenate([s_r, s_c], axis=1)


def _rope_swap(y):
    w = y.shape[1]
    lane = lax.broadcasted_iota(jnp.int32, y.shape, 1)
    return jnp.where(lane % 32 < 16, pltpu.roll(y, w - 16, 1), pltpu.roll(y, 16, 1))


def _head_sums(x, ones):
    parts = [lax.dot_general(p, ones, _NN, preferred_element_type=F32) for p in _split3(x)]
    return parts[0] + parts[1] + parts[2]


def _head_ones(width):
    seg = np.arange(width) // ATTN_HEAD_DIM
    return jnp.asarray(seg[:, None] == seg[None, :], BF16)


def _qk_prep_fwd(pg, cb, width, gain, cos, sin, scale, *, name):
    heads = width // ATTN_HEAD_DIM
    def fn(x, c, s, g, ones):
        r = lax.rsqrt(_head_sums(x * x, ones) * (1.0 / ATTN_HEAD_DIM) + NORM_EPS)
        y = x * r * g
        return (y * c + _rope_swap(y) * s) * scale
    return _rowmap(fn, [(pg, cb, width), jnp.tile(cos, (1, heads)), jnp.tile(sin, (1, heads))],
                   [jnp.tile(gain, heads).reshape(1, -1), _head_ones(width)], [(width, BF16)], tl=512, name=name)[0]


def _qk_prep_bwd(pg, cb, width, gain, cos, sin, scale, dout, *, name):
    heads = width // ATTN_HEAD_DIM
    def fn(x, dov, c, s, g, ones):
        r = lax.rsqrt(_head_sums(x * x, ones) * (1.0 / ATTN_HEAD_DIM) + NORM_EPS)
        dos = dov * scale
        dy = dos * c + _rope_swap(dos * s)
        gd = dy * g
        dx = r * gd - x * (r * r * r) * (_head_sums(x * gd, ones) * (1.0 / ATTN_HEAD_DIM))
        return dx, jnp.sum(dy * x * r, axis=0, keepdims=True)
    dx, dg = _rowmap(fn, [(pg, cb, width), dout, jnp.tile(cos, (1, heads)), jnp.tile(sin, (1, heads))],
                     [jnp.tile(gain, heads).reshape(1, -1), _head_ones(width)], [(width, F32)], [(1, width)], tl=512,
                     name=name)
    return dx, jnp.sum(dg.reshape(heads, ATTN_HEAD_DIM), axis=0)


def _to_heads(x, heads):
    return jnp.transpose(x.reshape(x.shape[0], heads, ATTN_HEAD_DIM), (1, 0, 2))


def _from_heads(x):
    return jnp.transpose(x, (1, 0, 2)).reshape(x.shape[1], x.shape[0] * ATTN_HEAD_DIM)


ATTN_GROUP = ATTN_Q_HEADS // ATTN_KV_HEADS
ATTN_TQ = 256


def _attn_fwd(q, k, v):
    length = q.shape[1]
    tq = min(ATTN_TQ, length)

    def body(q_ref, k_ref, v_ref, o_ref):
        kk, vv = k_ref[0], v_ref[0]
        for g in range(ATTN_GROUP):
            s = _dot(q_ref[g], kk, _NT)
            p = jnp.exp(s - jnp.max(s, axis=-1, keepdims=True))
            o_ref[g] = _dot(p, vv) / jnp.sum(p, axis=-1, keepdims=True)

    kv_spec = pl.BlockSpec((1, length, ATTN_HEAD_DIM), lambda h, i: (h, 0, 0))
    q_spec = pl.BlockSpec((ATTN_GROUP, tq, ATTN_HEAD_DIM), lambda h, i: (h, i, 0))
    return pl.pallas_call(
        body, name="attn_fwd", grid=(ATTN_KV_HEADS, length // tq), in_specs=[q_spec, kv_spec, kv_spec],
        out_specs=q_spec, out_shape=jax.ShapeDtypeStruct(q.shape, F32),
        compiler_params=pltpu.CompilerParams(dimension_semantics=("parallel", "parallel")),
    )(q, k, v)


def _attn_bwd(q, k, v, o, do):
    length = q.shape[1]
    tq = min(ATTN_TQ, length)

    def body(q_ref, k_ref, v_ref, o_ref, do_ref, dq_ref, dk_ref, dv_ref):
        @pl.when(pl.program_id(1) == 0)
        def _():
            dk_ref[...] = jnp.zeros_like(dk_ref)
            dv_ref[...] = jnp.zeros_like(dv_ref)

        kk, vv = k_ref[0], v_ref[0]
        for g in range(ATTN_GROUP):
            qg, dog = q_ref[g], do_ref[g]
            s = _dot(qg, kk, _NT)
            p = jnp.exp(s - jnp.max(s, axis=-1, keepdims=True))
            p = p / jnp.sum(p, axis=-1, keepdims=True)
            dp = _dot(dog, vv, _NT)
            ds = p * (dp - jnp.sum(dog * o_ref[g], axis=-1, keepdims=True))
            dq_ref[g] = _dot(ds, kk)
            dk_ref[0] += _dot(ds, qg, _TN)
            dv_ref[0] += _dot(p, dog, _TN)

    kv_spec = pl.BlockSpec((1, length, ATTN_HEAD_DIM), lambda h, i: (h, 0, 0))
    q_spec = pl.BlockSpec((ATTN_GROUP, tq, ATTN_HEAD_DIM), lambda h, i: (h, i, 0))
    return pl.pallas_call(
        body, name="attn_bwd", grid=(ATTN_KV_HEADS, length // tq),
        in_specs=[q_spec, kv_spec, kv_spec, q_spec, q_spec], out_specs=[q_spec, kv_spec, kv_spec],
        out_shape=[jax.ShapeDtypeStruct(q.shape, F32), jax.ShapeDtypeStruct(k.shape, F32),
                   jax.ShapeDtypeStruct(k.shape, F32)],
        compiler_params=pltpu.CompilerParams(dimension_semantics=("parallel", "arbitrary")),
    )(q, k, v, o, do)


def _attn_branch_fwd(pg, q_gain, k_gain):
    cos, sin = _rope_tables(pg.shape[0])
    qp = _qk_prep_fwd(pg, CB_AQ, ATTN_WIDTH, q_gain, cos, sin, ATTN_HEAD_DIM ** -0.5, name="attn_q_prep")
    kp = _qk_prep_fwd(pg, CB_AK, ATTN_KV_WIDTH, k_gain, cos, sin, 1.0, name="attn_k_prep")
    qh, kh = _to_heads(qp, ATTN_Q_HEADS), _to_heads(kp, ATTN_KV_HEADS)
    vh = _to_heads(pg[:, P_OFF + 3200:P_OFF + 3328].astype(BF16), ATTN_KV_HEADS)
    oh = _attn_fwd(qh, kh, vh)
    return _from_heads(oh).astype(BF16), (cos, sin, qh, kh, vh, oh)


def _attn_branch_bwd(pg, q_gain, k_gain, saved, dy):
    cos, sin, qh, kh, vh, oh = saved
    dqh, dkh, dvh = _attn_bwd(qh, kh, vh, oh, _to_heads(dy, ATTN_Q_HEADS))
    dq, dqg = _qk_prep_bwd(pg, CB_AQ, ATTN_WIDTH, q_gain, cos, sin, ATTN_HEAD_DIM ** -0.5, _from_heads(dqh),
                           name="attn_q_prep_bwd")
    dk, dkg = _qk_prep_bwd(pg, CB_AK, ATTN_KV_WIDTH, k_gain, cos, sin, 1.0, _from_heads(dkh), name="attn_k_prep_bwd")
    return dq, dk, _from_heads(dvh), {'q_norm': dqg, 'k_norm': dkg}


def _gate_cols():
    return [slice(i * D_MODEL, (i + 1) * D_MODEL) for i in range(3)]


def _mixer_fwd(x, lw):
    h = _rmsnorm_fwd(x, lw['mix_norm'])
    pg = _mm(h, lw['w_pg'], name="mix_in")
    y_s5, s_s5 = _s5_fwd(pg, lw['s5'], lw['s5_w_glu'])
    y_gla, s_gla = _gla_branch_fwd(pg, lw['gla_w_alpha'], lw['gla_b_alpha'], lw['gla_norm'])
    y_att, s_att = _attn_branch_fwd(pg, lw['attn_q_norm'], lw['attn_k_norm'])
    ys = (y_s5, y_gla, y_att)
    br = [_mm(y, lw[n], name="mix_branch") for y, n in zip(ys, ('w_branch_s5', 'w_branch_gla', 'w_branch_attn'))]

    def merge(g0, g1, g2, b0, b1, b2, bias):
        acc = None
        for g, b, c in zip((g0, g1, g2), (b0, b1, b2), _gate_cols()):
            term = _sigmoid(g + bias[:, c]) * b
            acc = term if acc is None else acc + term
        return acc
    merged = _rowmap(merge, [(pg, 0, D_MODEL), (pg, 1, D_MODEL), (pg, 2, D_MODEL)] + br,
                     [lw['b_merge_gate'].reshape(1, -1)], [(D_MODEL, BF16)], tl=256, name="mix_merge")[0]
    x_out = _mm(merged, lw['w_out'], add=x, name="mix_out")
    return x_out, (x, h, pg, ys, (s_s5, s_gla, s_att), br, merged)


def _mixer_bwd(saved, lw, dx_out):
    x, h, pg, ys, (s_s5, s_gla, s_att), br, merged = saved
    grads = {'w_out': _mm(merged, dx_out, ta=True, name="mix_dwout")}
    dmerged = _mm(dx_out, lw['w_out'], tb=True, name="mix_dmerged")

    def merge_bwd(g0, g1, g2, b0, b1, b2, dm, bias):
        dbr, dgp = [], []
        for g, b, c in zip((g0, g1, g2), (b0, b1, b2), _gate_cols()):
            s = _sigmoid(g + bias[:, c])
            dbr.append(dm * s)
            dgp.append(dm * b * (s * (1.0 - s)))
        dgp = jnp.concatenate(dgp, axis=1)
        return dbr[0], dbr[1], dbr[2], dgp, jnp.sum(dgp, axis=0, keepdims=True)
    d0, d1, d2, dgpre, dbias = _rowmap(
        merge_bwd, [(pg, 0, D_MODEL), (pg, 1, D_MODEL), (pg, 2, D_MODEL)] + br + [dmerged],
        [lw['b_merge_gate'].reshape(1, -1)], [(D_MODEL, BF16)] * 3 + [(GATE_WIDTH, BF16)], [(1, GATE_WIDTH)], tl=256,
        name="mix_merge_bwd")
    grads['b_merge_gate'] = dbias[0]
    dys = []
    for y, dbr, n in zip(ys, (d0, d1, d2), ('w_branch_s5', 'w_branch_gla', 'w_branch_attn')):
        grads[n] = _mm(y, dbr, ta=True, name="mix_dwbranch")
        dys.append(_mm(dbr, lw[n], tb=True, name="mix_dy"))
    du, g_s5 = _s5_bwd(pg, lw['s5'], lw['s5_w_glu'], s_s5, dys[0])
    dgq, dgk, dgv, dgg, dz, g_gla = _gla_branch_bwd(pg, lw['gla_w_alpha'], lw['gla_b_alpha'], lw['gla_norm'], s_gla, dys[1])
    daq, dak, dav, g_att = _attn_branch_bwd(pg, lw['attn_q_norm'], lw['attn_k_norm'], s_att, dys[2])

    def assemble(dgp, u0, u1, u2, q0, q1, k0, k1, v0, v1, gg, aq, ak, av, z):
        pad = jnp.zeros((dgp.shape[0], IN_PAD - 3456), F32)
        parts = [dgp.astype(F32), u0 + u1 + u2, q0 + q1, k0 + k1, v0 + v1, gg, aq, ak, av, z, pad]
        return jnp.concatenate(parts, axis=1)
    dpg = _rowmap(assemble, [dgpre] + du + dgq + dgk + dgv + [dgg, daq, dak, dav, dz], [], [(PG_WIDTH, BF16)], tl=256,
                  name="mix_dpg")[0]
    grads['w_pg'] = _mm(h, dpg, ta=True, name="mix_dwpg")
    dh = _mm(dpg, lw['w_pg'], tb=True, name="mix_dh")
    dx, grads['mix_norm'] = _rmsnorm_bwd(x, lw['mix_norm'], dh, dx_out)
    grads['s5'], grads['gla'], grads['attn'] = g_s5, g_gla, g_att
    return dx, grads


def _loss_head(x, gain, target):
    width = x.shape[1]

    def fn(xv, tv, g):
        r = _rms(xv)
        err = xv * r * g - tv
        dy = err * (1.0 / width)
        gd = dy * g
        dx = r * gd - xv * (r * r * r) * jnp.mean(xv * gd, axis=-1, keepdims=True)
        loss = jnp.sum(0.5 * jnp.mean(err * err, axis=-1, keepdims=True), axis=0, keepdims=True)
        return dx, jnp.broadcast_to(loss, (1, 128)), jnp.sum(dy * xv * r, axis=0, keepdims=True)
    dx, loss, dgain = _rowmap(fn, [x, target], [gain.reshape(1, -1)], [(width, F32)], [(1, 128), (1, width)], tl=256,
                              name="loss_head")
    return loss[0, 0], dx, dgain[0]


def _row_tile(rows, cap=256):
    for t in range(cap - cap % 16, 0, -16):
        if rows % t == 0:
            return t
    return rows


def _reduce_adamw(parts, w, m, v, *, name):
    _, r, c = parts.shape
    tr = _row_tile(r)

    def body(p_ref, w_ref, m_ref, v_ref, g_ref, d_ref, m2_ref, v2_ref):
        g = p_ref[0].astype(F32)
        for j in range(1, N_DEV):
            g = g + p_ref[j].astype(F32)
        m2 = ADAM_B1 * m_ref[...] + (1.0 - ADAM_B1) * g
        v2 = ADAM_B2 * v_ref[...] + (1.0 - ADAM_B2) * (g * g)
        m_hat = m2 / (1.0 - ADAM_B1 ** ADAM_STEP)
        v_hat = v2 / (1.0 - ADAM_B2 ** ADAM_STEP)
        g_ref[...] = g
        d_ref[...] = -ADAM_LR * (m_hat / (jnp.sqrt(v_hat) + ADAM_EPS) + ADAM_WD * w_ref[...])
        m2_ref[...] = m2
        v2_ref[...] = v2

    flat = pl.BlockSpec((tr, c), lambda i: (i, 0))
    return pl.pallas_call(
        body, name=name, grid=(r // tr,), in_specs=[pl.BlockSpec((N_DEV, tr, c), lambda i: (0, i, 0)), flat, flat, flat],
        out_specs=[flat] * 4, out_shape=[jax.ShapeDtypeStruct((r, c), F32)] * 4,
        compiler_params=pltpu.CompilerParams(dimension_semantics=("parallel",)),
    )(parts, w, m, v)


def _all_gather(blocks, *, name):
    n = len(blocks)

    def body(*refs):
        x_refs, out_refs = refs[:n], refs[n:2 * n]
        send_sems, recv_sems, local_sems = refs[2 * n:]
        x, y, c = lax.axis_index("x"), lax.axis_index("y"), lax.axis_index("c")
        me, sibling = (x, y, c), (x, y, 1 - c)
        chips = [(1 - x, y), (x, 1 - y), (1 - x, 1 - y)]

        def slot(t, px, py, pc):
            return out_refs[t].at[4 * px + 2 * py + pc]

        def copy(t, k, blk, to, own=False):
            return pltpu.make_async_remote_copy(
                src_ref=x_refs[t] if own else slot(t, *blk), dst_ref=slot(t, *blk), send_sem=send_sems.at[t, k],
                recv_sem=recv_sems.at[t, k], device_id=to, device_id_type=pl.DeviceIdType.MESH)

        mine = [pltpu.make_async_copy(x_refs[t], slot(t, *me), local_sems.at[t]) for t in range(n)]
        for cp in mine:
            cp.start()
        first = []
        for t in range(n):
            first.append(copy(t, 0, me, sibling, own=True))
            first += [copy(t, 1 + j, me, (*chip, c), own=True) for j, chip in enumerate(chips)]
        for cp in first:
            cp.start()
        passed = []
        for j, chip in enumerate(chips):
            for t in range(n):
                copy(t, 1 + j, (*chip, c), me).wait_recv()
                passed.append(copy(t, 4 + j, (*chip, c), sibling))
                passed[-1].start()
        for t in range(n):
            copy(t, 0, sibling, me).wait_recv()
        for j, chip in enumerate(chips):
            for t in range(n):
                copy(t, 4 + j, (*chip, 1 - c), me).wait_recv()
        for cp in first + passed:
            cp.wait_send()
        for cp in mine:
            cp.wait()

    hbm = pl.BlockSpec(memory_space=pl.ANY)
    return pl.pallas_call(
        body, name=name, out_shape=[jax.ShapeDtypeStruct((N_DEV,) + b.shape, b.dtype) for b in blocks],
        in_specs=[hbm] * n, out_specs=[hbm] * n,
        scratch_shapes=[pltpu.SemaphoreType.DMA((n, 7)), pltpu.SemaphoreType.DMA((n, 7)), pltpu.SemaphoreType.DMA((n,))],
    )(*blocks)


def _all_to_all(stacks, *, name):
    n = len(stacks)

    def body(*refs):
        g_refs, out_refs = refs[:n], refs[n:2 * n]
        send_sems, recv_sems, local_sems = refs[2 * n:]
        x, y, c = lax.axis_index("x"), lax.axis_index("y"), lax.axis_index("c")
        me = 4 * x + 2 * y + c
        mine = [pltpu.make_async_copy(g_refs[t].at[me], out_refs[t].at[me], local_sems.at[t]) for t in range(n)]
        for cp in mine:
            cp.start()
        copies = []
        for k in range(1, N_DEV):
            px, py, pc = x ^ (k >> 2 & 1), y ^ (k >> 1 & 1), c ^ (k & 1)
            for t in range(n):
                copies.append(pltpu.make_async_remote_copy(
                    src_ref=g_refs[t].at[4 * px + 2 * py + pc], dst_ref=out_refs[t].at[me], send_sem=send_sems.at[t, k - 1],
                    recv_sem=recv_sems.at[t, k - 1], device_id=(px, py, pc), device_id_type=pl.DeviceIdType.MESH))
        for cp in copies:
            cp.start()
        for cp in copies:
            cp.wait_recv()
        for cp in copies:
            cp.wait_send()
        for cp in mine:
            cp.wait()

    hbm = pl.BlockSpec(memory_space=pl.ANY)
    return pl.pallas_call(
        body, name=name, out_shape=[jax.ShapeDtypeStruct(s.shape, s.dtype) for s in stacks],
        in_specs=[hbm] * n, out_specs=[hbm] * n,
        scratch_shapes=[pltpu.SemaphoreType.DMA((n, 7)), pltpu.SemaphoreType.DMA((n, 7)), pltpu.SemaphoreType.DMA((n,))],
    )(*stacks)


SMALL_COLS = 128


def _pack_small(arrays):
    flat = jnp.concatenate([a.astype(F32).reshape(-1, SMALL_COLS) for a in arrays], axis=0)
    return jnp.pad(flat, ((0, -flat.shape[0] % 256), (0, 0)))


def _unpack_small(packed, shapes):
    out, off = [], 0
    for s in shapes:
        r = math.prod(s) // SMALL_COLS
        out.append(packed[off:off + r].reshape(s))
        off += r
    return out


def _split_shards(full, axis):
    shape = full.shape
    split = full.reshape(shape[:axis] + (N_DEV, shape[axis] // N_DEV) + shape[axis + 1:])
    return jnp.moveaxis(split, axis, 0)


def _join_shards(stack, axis):
    moved = jnp.moveaxis(stack, 0, axis)
    shape = moved.shape
    return moved.reshape(shape[:axis] + (shape[axis] * shape[axis + 1],) + shape[axis + 2:])


def _w_in_padded(w_in):
    pad = jnp.zeros(w_in.shape[:-1] + (IN_PAD - IN_WIDTH,), w_in.dtype)
    return jnp.concatenate([w_in[..., :2560], w_in[..., 2592:], w_in[..., 2560:2592], pad], axis=-1)


def _w_in_unpadded(w):
    return jnp.concatenate([w[..., :2560], w[..., 3328:3360], w[..., 2560:3328]], axis=-1)


S5_KEYS = ('lambda_re', 'lambda_im', 'log_dt', 'b_re', 'b_im', 'c_re', 'c_im', 'd')


def _layer_weights(full, w, i):
    lw = {n: w[n][i] for n in ('ffn1_norm', 'mix_norm', 'gla_norm', 'attn_q_norm', 'attn_k_norm', 'b_merge_gate', 'ffn2_norm')}
    lw['s5'] = {k: w['s5_' + k][i] for k in S5_KEYS}
    for f in ('ffn1', 'ffn2'):
        lw[f + '_w_gu'] = jnp.concatenate([full[f + '_w_gate'][i], full[f + '_w_up'][i]], axis=1)
        lw[f + '_w_down'] = full[f + '_w_down'][i]
    lw['w_pg'] = jnp.concatenate([full['w_merge_gate'][i], _w_in_padded(full['w_in'][i])], axis=1)
    for n in ('s5_w_glu', 'gla_w_alpha', 'w_branch_s5', 'w_branch_gla', 'w_branch_attn', 'w_out'):
        lw[n] = full[n][i]
    lw['gla_b_alpha'] = full['gla_b_alpha'][i].astype(F32)
    return lw


def _step_local(x, target, w, full):
    lws = [_layer_weights(full, w, i) for i in range(DEPTH)]
    saved = []
    for lw in lws:
        x, s1 = _ffn_fwd(x, lw['ffn1_norm'], lw['ffn1_w_gu'], lw['ffn1_w_down'])
        x, s2 = _mixer_fwd(x, lw)
        x, s3 = _ffn_fwd(x, lw['ffn2_norm'], lw['ffn2_w_gu'], lw['ffn2_w_down'])
        saved.append((s1, s2, s3))
    loss, dx, d_final = _loss_head(x, w['final_norm'], target)
    per_layer = []
    for lw, (s1, s2, s3) in reversed(list(zip(lws, saved))):
        g = {}
        dx, g['ffn2_norm'], dgu, g['ffn2_w_down'] = _ffn_bwd(s3, lw['ffn2_norm'], lw['ffn2_w_gu'], lw['ffn2_w_down'], dx)
        g['ffn2_w_gate'], g['ffn2_w_up'] = dgu[:, :D_FF], dgu[:, D_FF:]
        dx, gm = _mixer_bwd(s2, lw, dx)
        dx, g['ffn1_norm'], dgu, g['ffn1_w_down'] = _ffn_bwd(s1, lw['ffn1_norm'], lw['ffn1_w_gu'], lw['ffn1_w_down'], dx)
        g['ffn1_w_gate'], g['ffn1_w_up'] = dgu[:, :D_FF], dgu[:, D_FF:]
        g['w_merge_gate'] = gm['w_pg'][:, :GATE_WIDTH]
        g['w_in'] = _w_in_unpadded(gm['w_pg'][:, GATE_WIDTH:])
        for n in ('w_out', 'b_merge_gate', 'w_branch_s5', 'w_branch_gla', 'w_branch_attn', 'mix_norm'):
            g[n] = gm[n]
        for k in S5_KEYS:
            g['s5_' + k] = gm['s5'][k]
        g['s5_w_glu'] = gm['s5']['w_glu']
        g['gla_w_alpha'], g['gla_b_alpha'], g['gla_norm'] = gm['gla']['w_alpha'], gm['gla']['b_alpha'], gm['gla']['norm']
        g['attn_q_norm'], g['attn_k_norm'] = gm['attn']['q_norm'], gm['attn']['k_norm']
        per_layer.append(g)
    per_layer.reverse()
    return loss, dx, per_layer, d_final


def kernel(x, ffn1_norm, ffn1_w_gate, ffn1_w_up, ffn1_w_down, mix_norm, w_in, s5_lambda_re, s5_lambda_im, s5_log_dt, s5_b_re, s5_b_im, s5_c_re, s5_c_im, s5_d, s5_w_glu, gla_w_alpha, gla_b_alpha, gla_norm, attn_q_norm, attn_k_norm, w_branch_s5, w_branch_gla, w_branch_attn, w_merge_gate, b_merge_gate, w_out, ffn2_norm, ffn2_w_gate, ffn2_w_up, ffn2_w_down, final_norm, loss_target, m_ffn1_norm, m_ffn1_w_gate, m_ffn1_w_up, m_ffn1_w_down, m_mix_norm, m_w_in, m_s5_lambda_re, m_s5_lambda_im, m_s5_log_dt, m_s5_b_re, m_s5_b_im, m_s5_c_re, m_s5_c_im, m_s5_d, m_s5_w_glu, m_gla_w_alpha, m_gla_b_alpha, m_gla_norm, m_attn_q_norm, m_attn_k_norm, m_w_branch_s5, m_w_branch_gla, m_w_branch_attn, m_w_merge_gate, m_b_merge_gate, m_w_out, m_ffn2_norm, m_ffn2_w_gate, m_ffn2_w_up, m_ffn2_w_down, m_final_norm, v_ffn1_norm, v_ffn1_w_gate, v_ffn1_w_up, v_ffn1_w_down, v_mix_norm, v_w_in, v_s5_lambda_re, v_s5_lambda_im, v_s5_log_dt, v_s5_b_re, v_s5_b_im, v_s5_c_re, v_s5_c_im, v_s5_d, v_s5_w_glu, v_gla_w_alpha, v_gla_b_alpha, v_gla_norm, v_attn_q_norm, v_attn_k_norm, v_w_branch_s5, v_w_branch_gla, v_w_branch_attn, v_w_merge_gate, v_b_merge_gate, v_w_out, v_ffn2_norm, v_ffn2_w_gate, v_ffn2_w_up, v_ffn2_w_down, v_final_norm):
    return _train_step(x, ffn1_norm, ffn1_w_gate, ffn1_w_up, ffn1_w_down, mix_norm, w_in, s5_lambda_re, s5_lambda_im, s5_log_dt, s5_b_re, s5_b_im, s5_c_re, s5_c_im, s5_d, s5_w_glu, gla_w_alpha, gla_b_alpha, gla_norm, attn_q_norm, attn_k_norm, w_branch_s5, w_branch_gla, w_branch_attn, w_merge_gate, b_merge_gate, w_out, ffn2_norm, ffn2_w_gate, ffn2_w_up, ffn2_w_down, final_norm, loss_target, m_ffn1_norm, m_ffn1_w_gate, m_ffn1_w_up, m_ffn1_w_down, m_mix_norm, m_w_in, m_s5_lambda_re, m_s5_lambda_im, m_s5_log_dt, m_s5_b_re, m_s5_b_im, m_s5_c_re, m_s5_c_im, m_s5_d, m_s5_w_glu, m_gla_w_alpha, m_gla_b_alpha, m_gla_norm, m_attn_q_norm, m_attn_k_norm, m_w_branch_s5, m_w_branch_gla, m_w_branch_attn, m_w_merge_gate, m_b_merge_gate, m_w_out, m_ffn2_norm, m_ffn2_w_gate, m_ffn2_w_up, m_ffn2_w_down, m_final_norm, v_ffn1_norm, v_ffn1_w_gate, v_ffn1_w_up, v_ffn1_w_down, v_mix_norm, v_w_in, v_s5_lambda_re, v_s5_lambda_im, v_s5_log_dt, v_s5_b_re, v_s5_b_im, v_s5_c_re, v_s5_c_im, v_s5_d, v_s5_w_glu, v_gla_w_alpha, v_gla_b_alpha, v_gla_norm, v_attn_q_norm, v_attn_k_norm, v_w_branch_s5, v_w_branch_gla, v_w_branch_attn, v_w_merge_gate, v_b_merge_gate, v_w_out, v_ffn2_norm, v_ffn2_w_gate, v_ffn2_w_up, v_ffn2_w_down, v_final_norm)


def _train_step(*args):
    nw = len(W_NAMES)
    x, target = args[0][0], args[1 + nw][0]
    w = dict(zip(W_NAMES, args[1:1 + nw]))
    m = dict(zip(W_NAMES, args[2 + nw:2 + 2 * nw]))
    v = dict(zip(W_NAMES, args[2 + 2 * nw:2 + 3 * nw]))

    gathered = _all_gather([w[n].astype(BF16) for n in SHARDED], name="gather_weights")
    full = {n: _join_shards(g, SHARD_AXIS[n]) for n, g in zip(SHARDED, gathered)}

    loss, dx, per_layer, d_final = _step_local(x, target, w, full)
    loss = lax.psum(loss, ("x", "y", "c"))

    out = {}
    kinds = ('grad', 'delta', 'new_m', 'new_v')
    outgoing = [jnp.stack([_split_shards(g[n], SHARD_AXIS[n] - 1) for g in per_layer], axis=1).astype(BF16) for n in SHARDED]
    incoming = _all_to_all(outgoing, name="exchange_grads")
    for n, parts in zip(SHARDED, incoming):
        shape = w[n].shape
        flat = lambda a: a.reshape(-1, shape[-1])
        res = _reduce_adamw(parts.reshape(N_DEV, -1, shape[-1]), flat(w[n]), flat(m[n]), flat(v[n]), name="adamw_sharded")
        for kind, a in zip(kinds, res):
            out[kind + '_' + n] = a.reshape(shape)
    small = [jnp.stack([g[n] for g in per_layer]) if n != 'final_norm' else d_final for n in REPLICATED]
    parts = _all_gather([_pack_small(small)], name="gather_small_grads")[0]
    res = _reduce_adamw(parts, *[_pack_small([d[n] for n in REPLICATED]) for d in (w, m, v)], name="adamw_replicated")
    for kind, packed in zip(kinds, res):
        for n, a in zip(REPLICATED, _unpack_small(packed, [w[n].shape for n in REPLICATED])):
            out[kind + '_' + n] = a
    return (loss, dx[None]) + tuple(out[kind + '_' + n] for kind in kinds for n in W_NAMES)
```

```python
import functools
import math

import jax
import jax.numpy as jnp
import numpy as np
from jax import lax
from jax.experimental import pallas as pl
from jax.experimental.pallas import tpu as pltpu

F32 = jnp.float32
BF16 = jnp.bfloat16

N_DEV = 8
D_MODEL = 1024
DEPTH = 2
GRID_W = 64
D_FF = 2816
NORM_EPS = 1e-6
S5_GROUPS = 32
S5_GROUP_CH = 16
S5_STATE = 64
S5_WIDTH = 512
S5_NSTATE = S5_GROUPS * S5_STATE
S5_LANE_BLOCK = 512
GLA_HEADS = 4
GLA_HEAD_DIM = 128
GLA_WIDTH = 512
GLA_LOWRANK = 16
GLA_TAU = 16.0
GLA_CHUNK = 64
ATTN_Q_HEADS = 8
ATTN_KV_HEADS = 2
ATTN_HEAD_DIM = 64
ATTN_WIDTH = 512
ATTN_KV_WIDTH = 128
ROPE_BASE = 10000.0
IN_SPLITS = (512, 512, 512, 512, 512, 16, 16, 512, 128, 128)
IN_WIDTH = sum(IN_SPLITS)
IN_PAD = 3584
GATE_WIDTH = 3 * D_MODEL
PG_WIDTH = GATE_WIDTH + IN_PAD
P_OFF = GATE_WIDTH
CB_U, CB_GQ, CB_GK, CB_GV, CB_GG, CB_AQ = (P_OFF // 512 + i for i in range(6))
CB_AK, CB_AV, CB_Z = (P_OFF + 3072) // 128, (P_OFF + 3200) // 128, (P_OFF + 3328) // 128
ADAM_LR = 0.001
ADAM_B1 = 0.9
ADAM_B2 = 0.999
ADAM_EPS = 1e-08
ADAM_WD = 0.01
ADAM_STEP = 10
PACK_COLS = 1024

W_NAMES = ['ffn1_norm', 'ffn1_w_gate', 'ffn1_w_up', 'ffn1_w_down', 'mix_norm', 'w_in', 's5_lambda_re', 's5_lambda_im',
           's5_log_dt', 's5_b_re', 's5_b_im', 's5_c_re', 's5_c_im', 's5_d', 's5_w_glu', 'gla_w_alpha', 'gla_b_alpha',
           'gla_norm', 'attn_q_norm', 'attn_k_norm', 'w_branch_s5', 'w_branch_gla', 'w_branch_attn', 'w_merge_gate',
           'b_merge_gate', 'w_out', 'ffn2_norm', 'ffn2_w_gate', 'ffn2_w_up', 'ffn2_w_down', 'final_norm']
SHARD_AXIS = {'ffn1_w_gate': 2, 'ffn1_w_up': 2, 'ffn1_w_down': 1, 'w_in': 2, 's5_w_glu': 1, 'gla_w_alpha': 3,
              'gla_b_alpha': 2, 'w_branch_s5': 2, 'w_branch_gla': 2, 'w_branch_attn': 2, 'w_merge_gate': 2,
              'w_out': 1, 'ffn2_w_gate': 2, 'ffn2_w_up': 2, 'ffn2_w_down': 1}
SHARDED = [n for n in W_NAMES if n in SHARD_AXIS]
REPLICATED = [n for n in W_NAMES if n not in SHARD_AXIS]


def _pick(dim, prefs):
    for p in prefs:
        if dim % p == 0:
            return p
    return dim


def _sigmoid(x):
    return 1.0 / (1.0 + jnp.exp(-x))


def _mm(a, b, *, ta=False, tb=False, out_dtype=F32, scale=None, add=None, name):
    a, a_cb, a_w = a if isinstance(a, tuple) else (a, 0, a.shape[1])
    b, b_cb, b_w = b if isinstance(b, tuple) else (b, 0, b.shape[1])
    m, k = (a_w, a.shape[0]) if ta else (a.shape[0], a_w)
    n = b.shape[0] if tb else b_w
    assert (b_w if tb else b.shape[0]) == k, (a.shape, b.shape, ta, tb)
    tm, tn, tk = _mm_tiles(m, n, k, a.dtype.itemsize, b.dtype.itemsize, jnp.dtype(out_dtype).itemsize)
    nk = k // tk
    dims = (((0 if ta else 1,), (1 if tb else 0,)), ((), ()))
    a_off = a_cb * (a_w // (tm if ta else tk))
    b_off = b_cb * (b_w // (tk if tb else tn))

    def body(a_ref, b_ref, *rest):
        add_ref = rest[0] if add is not None else None
        o_ref, *acc = rest[1:] if add is not None else rest

        def finish(res):
            res = res if scale is None else res * scale
            return (res if add_ref is None else res + add_ref[...]).astype(out_dtype)

        part = lax.dot_general(a_ref[...].astype(BF16), b_ref[...].astype(BF16), dims, preferred_element_type=F32)
        if nk == 1:
            o_ref[...] = finish(part)
            return
        acc_ref, = acc
        kk = pl.program_id(2)

        @pl.when(kk == 0)
        def _():
            acc_ref[...] = part

        @pl.when(kk > 0)
        def _():
            acc_ref[...] += part

        @pl.when(kk == nk - 1)
        def _():
            o_ref[...] = finish(acc_ref[...])

    a_spec = (pl.BlockSpec((tk, tm), lambda i, j, kk: (kk, i + a_off)) if ta
              else pl.BlockSpec((tm, tk), lambda i, j, kk: (i, kk + a_off)))
    b_spec = (pl.BlockSpec((tn, tk), lambda i, j, kk: (j, kk + b_off)) if tb
              else pl.BlockSpec((tk, tn), lambda i, j, kk: (kk, j + b_off)))
    o_spec = pl.BlockSpec((tm, tn), lambda i, j, kk: (i, j))
    return pl.pallas_call(
        body, name=name, grid=(m // tm, n // tn, nk), in_specs=[a_spec, b_spec] + ([o_spec] if add is not None else []),
        out_specs=o_spec, out_shape=jax.ShapeDtypeStruct((m, n), out_dtype),
        scratch_shapes=[pltpu.VMEM((tm, tn), F32)] if nk > 1 else [],
        compiler_params=pltpu.CompilerParams(dimension_semantics=("parallel", "parallel", "arbitrary")),
    )(a, b, *([add] if add is not None else []))


MM_VMEM_BUDGET = 40 * 1024 * 1024


def _mm_tiles(m, n, k, a_bytes, b_bytes, out_bytes):
    tms = [t for t in (1024, 1408, 512, 256, 128) if m % t == 0] or [m]
    tns = [t for t in (512, 1408, 256, 128) if n % t == 0] or [n]
    tks = [k] + [t for t in (2048, 1024, 512, 256, 128) if k % t == 0 and t < k]
    for tk in tks:
        for tm in tms:
            for tn in tns:
                use = 2 * (tm * tk * a_bytes + tk * tn * b_bytes + tm * tn * out_bytes) + 2 * tm * tn * 4
                if use <= MM_VMEM_BUDGET:
                    return tm, tn, tk
    return tms[-1], tns[-1], tks[-1]


def _rowmap(fn, rows, consts, outs, reds=(), *, tl, name):
    rows = [r if isinstance(r, tuple) else (r, 0, r.shape[1]) for r in rows]
    length = rows[0][0].shape[0]
    tl = min(tl, length)
    nr, nc, no = len(rows), len(consts), len(outs)

    def body(*refs):
        res = fn(*[r[...] for r in refs[:nr + nc]])
        res = res if isinstance(res, tuple) else (res,)
        for o_ref, val in zip(refs[nr + nc:nr + nc + no], res[:no]):
            o_ref[...] = val.astype(o_ref.dtype)
        if reds:
            step = pl.program_id(0)
            red_refs = refs[nr + nc + no:]

            @pl.when(step == 0)
            def _():
                for d_ref, val in zip(red_refs, res[no:]):
                    d_ref[...] = val.astype(F32)

            @pl.when(step > 0)
            def _():
                for d_ref, val in zip(red_refs, res[no:]):
                    d_ref[...] += val.astype(F32)

    in_specs = [pl.BlockSpec((tl, w), lambda i, cb=cb: (i, cb)) for (_, cb, w) in rows]
    in_specs += [pl.BlockSpec(c.shape, lambda i, nd=c.ndim: (0,) * nd) for c in consts]
    out_specs = [pl.BlockSpec((tl, w), lambda i: (i, 0)) for (w, _) in outs]
    out_specs += [pl.BlockSpec(s, lambda i, nd=len(s): (0,) * nd) for s in reds]
    out_shape = [jax.ShapeDtypeStruct((length, w), dt) for (w, dt) in outs]
    out_shape += [jax.ShapeDtypeStruct(s, F32) for s in reds]
    res = pl.pallas_call(
        body, name=name, grid=(length // tl,), in_specs=in_specs, out_specs=out_specs, out_shape=out_shape,
        compiler_params=pltpu.CompilerParams(dimension_semantics=("arbitrary" if reds else "parallel",)),
    )(*[r[0] for r in rows], *consts)
    return res


def _rms(x):
    return lax.rsqrt(jnp.mean(x * x, axis=-1, keepdims=True) + NORM_EPS)


def _rmsnorm_fwd(x, gain):
    def fn(xv, g):
        return xv * _rms(xv) * g
    return _rowmap(fn, [x], [gain.reshape(1, -1)], [(x.shape[1], BF16)], tl=256, name="rmsnorm_fwd")[0]


def _rmsnorm_bwd(x, gain, dh, dres):
    def fn(xv, dhv, drv, g):
        r = _rms(xv)
        gd = dhv * g
        dx = r * gd - xv * (r * r * r) * jnp.mean(xv * gd, axis=-1, keepdims=True)
        return drv + dx, jnp.sum(dhv * xv * r, axis=0, keepdims=True)
    dx, dg = _rowmap(fn, [x, dh, dres], [gain.reshape(1, -1)], [(x.shape[1], F32)], [(1, x.shape[1])], tl=256,
                     name="rmsnorm_bwd")
    return dx, dg[0]


FFN_UNIT = 256


def _ffn_interleave(w_gate, w_up):
    k = w_gate.shape[0]
    units = D_FF // FFN_UNIT
    both = jnp.stack([w_gate.reshape(k, units, FFN_UNIT), w_up.reshape(k, units, FFN_UNIT)], axis=2)
    return both.reshape(k, 2 * D_FF)


def _ffn_deinterleave(w):
    k = w.shape[0]
    both = w.reshape(k, D_FF // FFN_UNIT, 2, FFN_UNIT)
    return both[:, :, 0].reshape(k, D_FF), both[:, :, 1].reshape(k, D_FF)


def _ffn_up(h, w_gu):
    length, k = h.shape
    tm = _pick(length, (1024, 512, 256, 128))

    def body(h_ref, w_ref, a_ref, gu_ref):
        part = jnp.dot(h_ref[...], w_ref[...], preferred_element_type=F32)
        g, u = part[:, :FFN_UNIT], part[:, FFN_UNIT:]
        a_ref[...] = (g * _sigmoid(g) * u).astype(BF16)
        gu_ref[...] = part.astype(BF16)

    return pl.pallas_call(
        body, name="ffn_up", grid=(length // tm, D_FF // FFN_UNIT),
        in_specs=[pl.BlockSpec((tm, k), lambda i, j: (i, 0)), pl.BlockSpec((k, 2 * FFN_UNIT), lambda i, j: (0, j))],
        out_specs=[pl.BlockSpec((tm, FFN_UNIT), lambda i, j: (i, j)), pl.BlockSpec((tm, 2 * FFN_UNIT), lambda i, j: (i, j))],
        out_shape=[jax.ShapeDtypeStruct((length, D_FF), BF16), jax.ShapeDtypeStruct((length, 2 * D_FF), BF16)],
        compiler_params=pltpu.CompilerParams(dimension_semantics=("parallel", "parallel")),
    )(h, w_gu)


def _ffn_dgu(dxo, w_down, gu):
    length, k = dxo.shape
    tm = _pick(length, (1024, 512, 256, 128))

    def body(d_ref, w_ref, gu_ref, o_ref):
        da = 0.5 * lax.dot_general(d_ref[...], w_ref[...], _NT, preferred_element_type=F32)
        g = gu_ref[:, :FFN_UNIT].astype(F32)
        u = gu_ref[:, FFN_UNIT:].astype(F32)
        s = _sigmoid(g)
        o_ref[...] = jnp.concatenate([da * u * (s * (1.0 + g * (1.0 - s))), da * (g * s)], axis=1).astype(BF16)

    return pl.pallas_call(
        body, name="ffn_dgu", grid=(length // tm, D_FF // FFN_UNIT),
        in_specs=[pl.BlockSpec((tm, k), lambda i, j: (i, 0)), pl.BlockSpec((FFN_UNIT, k), lambda i, j: (j, 0)),
                  pl.BlockSpec((tm, 2 * FFN_UNIT), lambda i, j: (i, j))],
        out_specs=pl.BlockSpec((tm, 2 * FFN_UNIT), lambda i, j: (i, j)),
        out_shape=jax.ShapeDtypeStruct((length, 2 * D_FF), BF16),
        compiler_params=pltpu.CompilerParams(dimension_semantics=("parallel", "parallel")),
    )(dxo, w_down, gu)


def _ffn_fwd(x, gain, w_gu, w_down):
    h = _rmsnorm_fwd(x, gain)
    a, gu = _ffn_up(h, w_gu)
    x_out = _mm(a, w_down, scale=0.5, add=x, name="ffn_down")
    return x_out, (x, h, gu, a)


def _ffn_bwd(saved, gain, w_gu, w_down, dx_out):
    x, h, gu, a = saved
    dxo = dx_out.astype(BF16)
    d_wdown = _mm(a, dxo, ta=True, scale=0.5, name="ffn_dwdown")
    dgu = _ffn_dgu(dxo, w_down, gu)
    d_wgu = _mm(h, dgu, ta=True, name="ffn_dwgu")
    dh = _mm(dgu, w_gu, tb=True, name="ffn_dh")
    dx, dgain = _rmsnorm_bwd(x, gain, dh, dx_out)
    return dx, dgain, d_wgu, d_wdown


def _s5_col(n):
    return (n // S5_LANE_BLOCK) * 2 * S5_LANE_BLOCK + n % S5_LANE_BLOCK


def _s5_blocked(re, im):
    lead = re.shape[:-1]
    nb = S5_NSTATE // S5_LANE_BLOCK
    both = jnp.stack([re.reshape(*lead, nb, S5_LANE_BLOCK), im.reshape(*lead, nb, S5_LANE_BLOCK)], axis=-2)
    return both.reshape(*lead, 2 * S5_NSTATE)


def _s5_unblocked(z):
    lead = z.shape[:-1]
    nb = S5_NSTATE // S5_LANE_BLOCK
    both = z.reshape(*lead, nb, 2, S5_LANE_BLOCK)
    return both[..., 0, :].reshape(*lead, S5_NSTATE), both[..., 1, :].reshape(*lead, S5_NSTATE)


def _s5_tables(a_re, a_im, reverse):
    a = lax.complex(a_re, a_im)
    a2 = a * a
    a4 = a2 * a2
    rows = jnp.arange(8)
    pw = [a]
    for _ in range(7):
        pw.append(pw[-1] * a)
    pw = jnp.stack(pw)
    if reverse:
        pw = pw[::-1]
    tabs = []
    for coef, s in ((a, 1), (a2, 2), (a4, 4)):
        live = (rows <= 7 - s) if reverse else (rows >= s)
        tabs.append(jnp.where(live[:, None], coef[None, :], 0.0))
    tabs.append(pw)
    tabs = jnp.stack(tabs)
    return _s5_blocked(jnp.real(tabs), jnp.imag(tabs))


def _s5_scan_tile(v, tab_ref, prev, reverse):
    lb = S5_LANE_BLOCK
    vr, vi = v[:, :lb], v[:, lb:]
    for idx, s in enumerate((1, 2, 4)):
        cr, ci = tab_ref[idx, :, :lb], tab_ref[idx, :, lb:]
        sh = 8 - s if reverse else s
        sr, si = pltpu.roll(vr, sh, 0), pltpu.roll(vi, sh, 0)
        vr, vi = vr + cr * sr - ci * si, vi + cr * si + ci * sr
    row = 0 if reverse else 7
    pr = jnp.broadcast_to(prev[row:row + 1, :lb], (8, lb))
    pi = jnp.broadcast_to(prev[row:row + 1, lb:], (8, lb))
    cr, ci = tab_ref[3, :, :lb], tab_ref[3, :, lb:]
    return jnp.concatenate([vr + cr * pr - ci * pi, vi + cr * pi + ci * pr], axis=1)


def _s5_scan(v, tabs, reverse, *, name):
    length = v.shape[0]
    tb = min(512, length)
    ntb = length // tb
    nlb = S5_NSTATE // S5_LANE_BLOCK
    wb = 2 * S5_LANE_BLOCK
    ntile = tb // 8

    def body(tab_ref, v_ref, x_ref, carry_ref):
        @pl.when(pl.program_id(1) == 0)
        def _():
            carry_ref[...] = jnp.zeros_like(carry_ref)

        def step(i, prev):
            r0 = pl.multiple_of((ntile - 1 - i if reverse else i) * 8, 8)
            x = _s5_scan_tile(v_ref[pl.ds(r0, 8), :], tab_ref, prev, reverse)
            x_ref[pl.ds(r0, 8), :] = x
            return x

        carry_ref[...] = lax.fori_loop(0, ntile, step, carry_ref[...])

    tmap = (lambda c, t: (ntb - 1 - t, c)) if reverse else (lambda c, t: (t, c))
    return pl.pallas_call(
        body, name=name, grid=(nlb, ntb),
        in_specs=[pl.BlockSpec((4, 8, wb), lambda c, t: (0, 0, c)), pl.BlockSpec((tb, wb), tmap)],
        out_specs=pl.BlockSpec((tb, wb), tmap), out_shape=jax.ShapeDtypeStruct(v.shape, F32),
        scratch_shapes=[pltpu.VMEM((8, wb), F32)],
        compiler_params=pltpu.CompilerParams(dimension_semantics=("parallel", "arbitrary")),
    )(tabs, v)


def _s5_scan_adjoint(g, xs, tabs_conj, reverse, *, name):
    length = g.shape[0]
    tb = min(512, length)
    ntb = length // tb
    nlb = S5_NSTATE // S5_LANE_BLOCK
    lb = S5_LANE_BLOCK
    wb = 2 * lb
    ntile = tb // 8
    adj_rev = not reverse
    if reverse:
        edge = jnp.concatenate([xs[tb::tb], jnp.zeros((1, xs.shape[1]), F32)], axis=0)
    else:
        edge = jnp.concatenate([jnp.zeros((1, xs.shape[1]), F32), xs[tb - 1:length - 1:tb]], axis=0)
    edge = edge.reshape(ntb, 1, xs.shape[1])

    def body(tab_ref, g_ref, x_ref, edge_ref, lam_ref, da_ref, carry_ref):
        @pl.when(pl.program_id(1) == 0)
        def _():
            carry_ref[...] = jnp.zeros_like(carry_ref)
            da_ref[...] = jnp.zeros_like(da_ref)

        rows = lax.broadcasted_iota(jnp.int32, (8, wb), 0)

        def step(i, carry):
            prev, acc = carry
            k = ntile - 1 - i if adj_rev else i
            r0 = pl.multiple_of(k * 8, 8)
            lam = _s5_scan_tile(g_ref[pl.ds(r0, 8), :], tab_ref, prev, adj_rev)
            lam_ref[pl.ds(r0, 8), :] = lam
            x = x_ref[pl.ds(r0, 8), :]
            if reverse:
                kn = jnp.minimum(k + 1, ntile - 1)
                nb = x_ref[pl.ds(pl.multiple_of(kn * 8, 8), 8), :][0:1, :]
                nb = jnp.where(k == ntile - 1, edge_ref[0], nb)
                xp = jnp.where(rows == 7, jnp.broadcast_to(nb, (8, wb)), pltpu.roll(x, 7, 0))
            else:
                kn = jnp.maximum(k - 1, 0)
                nb = x_ref[pl.ds(pl.multiple_of(kn * 8, 8), 8), :][7:8, :]
                nb = jnp.where(k == 0, edge_ref[0], nb)
                xp = jnp.where(rows == 0, jnp.broadcast_to(nb, (8, wb)), pltpu.roll(x, 1, 0))
            xr, xi, lr, li = xp[:, :lb], xp[:, lb:], lam[:, :lb], lam[:, lb:]
            acc = acc + jnp.concatenate([xr * lr + xi * li, xr * li - xi * lr], axis=1)
            return lam, acc

        last, acc = lax.fori_loop(0, ntile, step, (carry_ref[...], da_ref[...]))
        carry_ref[...] = last
        da_ref[...] = acc

    tmap = (lambda c, t: (ntb - 1 - t, c)) if adj_rev else (lambda c, t: (t, c))
    emap = (lambda c, t: (ntb - 1 - t, 0, c)) if adj_rev else (lambda c, t: (t, 0, c))
    return pl.pallas_call(
        body, name=name, grid=(nlb, ntb),
        in_specs=[pl.BlockSpec((4, 8, wb), lambda c, t: (0, 0, c)), pl.BlockSpec((tb, wb), tmap),
                  pl.BlockSpec((tb, wb), tmap), pl.BlockSpec((1, 1, wb), emap)],
        out_specs=[pl.BlockSpec((tb, wb), tmap), pl.BlockSpec((8, wb), lambda c, t: (0, c))],
        out_shape=[jax.ShapeDtypeStruct(g.shape, F32), jax.ShapeDtypeStruct((8, g.shape[1]), F32)],
        scratch_shapes=[pltpu.VMEM((8, wb), F32)],
        compiler_params=pltpu.CompilerParams(dimension_semantics=("parallel", "arbitrary")),
    )(tabs_conj, g, xs, edge)


def _s5_prep(lam_re, lam_im, log_dt, b_re, b_im):
    lam = lax.complex(lam_re, lam_im)
    dt = jnp.exp(log_dt)[:, None]
    lam_bar = jnp.exp(lam * dt)
    b_bar = ((lam_bar - 1.0) / lam)[..., None] * lax.complex(b_re, b_im)
    return (jnp.real(lam_bar).reshape(-1), jnp.imag(lam_bar).reshape(-1), jnp.real(b_bar), jnp.imag(b_bar))


S5_NBLK = S5_NSTATE // S5_LANE_BLOCK
S5_BLK_GROUPS = S5_GROUPS // S5_NBLK
S5_BLK_CH = S5_BLK_GROUPS * S5_GROUP_CH


def _s5_in_matrix(bb_re, bb_im):
    eye = jnp.eye(S5_BLK_GROUPS, dtype=F32)
    def dense(bb):
        b4 = bb.reshape(S5_NBLK, S5_BLK_GROUPS, S5_STATE, S5_GROUP_CH)
        return jnp.einsum('cgph,gk->cghkp', b4, eye).reshape(S5_NBLK, S5_BLK_CH, S5_LANE_BLOCK)
    return jnp.concatenate([dense(bb_re), dense(bb_im)], axis=-1)


def _s5_block_diagonal(d):
    d5 = d.reshape(S5_NBLK, S5_BLK_GROUPS, S5_GROUP_CH, S5_BLK_GROUPS, S5_STATE)
    eye = jnp.eye(S5_BLK_GROUPS, dtype=F32)
    return jnp.swapaxes(jnp.sum(d5 * eye[None, :, None, :, None], axis=1), 1, 2)


def _s5_in_matrix_grad(d_mat):
    def diag(d):
        return jnp.swapaxes(_s5_block_diagonal(d), 2, 3).reshape(S5_GROUPS, S5_STATE, S5_GROUP_CH)
    return diag(d_mat[..., :S5_LANE_BLOCK]), diag(d_mat[..., S5_LANE_BLOCK:])


def _s5_out_matrix(c_re, c_im):
    eye = jnp.eye(S5_BLK_GROUPS, dtype=F32)
    def dense(cc):
        c4 = cc.reshape(S5_NBLK, S5_BLK_GROUPS, S5_GROUP_CH, S5_STATE)
        return jnp.einsum('cghp,gk->cgpkh', c4, eye).reshape(S5_NBLK, S5_LANE_BLOCK, S5_BLK_CH)
    return jnp.concatenate([dense(c_re), dense(-c_im)], axis=1)


def _s5_out_matrix_grad(d_mat_t):
    def diag(d):
        return _s5_block_diagonal(d).reshape(S5_GROUPS, S5_GROUP_CH, S5_STATE)
    return diag(d_mat_t[..., :S5_LANE_BLOCK]), -diag(d_mat_t[..., S5_LANE_BLOCK:])


def _gmm(a, b, *, tb=False, name):
    arr, cb0, wa = a
    nblk = b.shape[0]
    wn = b.shape[1] if tb else b.shape[2]
    length = arr.shape[0]
    tm = _pick(length, (1024, 512, 256, 128))
    dims = (((1,), (1 if tb else 0,)), ((), ()))

    def body(a_ref, b_ref, o_ref):
        o_ref[...] = lax.dot_general(a_ref[...].astype(BF16), b_ref[0].astype(BF16), dims, preferred_element_type=F32)

    return pl.pallas_call(
        body, name=name, grid=(nblk, length // tm),
        in_specs=[pl.BlockSpec((tm, wa), lambda c, i: (i, cb0 + c)), pl.BlockSpec((1,) + b.shape[1:], lambda c, i: (c, 0, 0))],
        out_specs=pl.BlockSpec((tm, wn), lambda c, i: (i, c)), out_shape=jax.ShapeDtypeStruct((length, nblk * wn), F32),
        compiler_params=pltpu.CompilerParams(dimension_semantics=("parallel", "parallel")),
    )(arr, b)


def _gmm_tn(a, g, *, nblk, name):
    arr_a, cb_a, wa = a
    arr_g, cb_g, wg = g
    length = arr_a.shape[0]
    dims = (((0,), (0,)), ((), ()))

    def body(a_ref, g_ref, o_ref):
        o_ref[0] = lax.dot_general(a_ref[...].astype(BF16), g_ref[...].astype(BF16), dims, preferred_element_type=F32)

    return pl.pallas_call(
        body, name=name, grid=(nblk,),
        in_specs=[pl.BlockSpec((length, wa), lambda c: (0, cb_a + c)), pl.BlockSpec((length, wg), lambda c: (0, cb_g + c))],
        out_specs=pl.BlockSpec((1, wa, wg), lambda c: (c, 0, 0)), out_shape=jax.ShapeDtypeStruct((nblk, wa, wg), F32),
        compiler_params=pltpu.CompilerParams(dimension_semantics=("parallel",)),
    )(arr_a, arr_g)


def _gelu_parts(x):
    k = math.sqrt(2.0 / math.pi)
    inner = k * (x + 0.044715 * x * x * x)
    th = jnp.tanh(inner)
    return th, k * (1.0 + 3.0 * 0.044715 * x * x)


S5_CB_U = CB_U * (512 // S5_BLK_CH)


def _s5_direction_fwd(pg, b_mat, c_mat, tabs, reverse, *, name):
    length = pg.shape[0]
    tb = min(512, length)
    ntb = length // tb
    wb = 2 * S5_LANE_BLOCK
    ntile = tb // 8

    def body(tab_ref, u_ref, b_ref, c_ref, x_ref, y_ref, carry_ref, bu_ref):
        @pl.when(pl.program_id(1) == 0)
        def _():
            carry_ref[...] = jnp.zeros_like(carry_ref)

        bu_ref[...] = jnp.dot(u_ref[...].astype(BF16), b_ref[0], preferred_element_type=F32)

        def step(i, prev):
            r0 = pl.multiple_of((ntile - 1 - i if reverse else i) * 8, 8)
            x = _s5_scan_tile(bu_ref[pl.ds(r0, 8), :], tab_ref, prev, reverse)
            x_ref[pl.ds(r0, 8), :] = x
            return x

        carry_ref[...] = lax.fori_loop(0, ntile, step, carry_ref[...])
        y_ref[...] = jnp.dot(x_ref[...].astype(BF16), c_ref[0], preferred_element_type=F32)

    tix = (lambda t: ntb - 1 - t) if reverse else (lambda t: t)
    return pl.pallas_call(
        body, name=name, grid=(S5_NBLK, ntb),
        in_specs=[pl.BlockSpec((4, 8, wb), lambda c, t: (0, 0, c)),
                  pl.BlockSpec((tb, S5_BLK_CH), lambda c, t: (tix(t), S5_CB_U + c)),
                  pl.BlockSpec((1, S5_BLK_CH, wb), lambda c, t: (c, 0, 0)),
                  pl.BlockSpec((1, wb, S5_BLK_CH), lambda c, t: (c, 0, 0))],
        out_specs=[pl.BlockSpec((tb, wb), lambda c, t: (tix(t), c)), pl.BlockSpec((tb, S5_BLK_CH), lambda c, t: (tix(t), c))],
        out_shape=[jax.ShapeDtypeStruct((length, S5_NBLK * wb), F32), jax.ShapeDtypeStruct((length, S5_WIDTH), F32)],
        scratch_shapes=[pltpu.VMEM((8, wb), F32), pltpu.VMEM((tb, wb), F32)],
        compiler_params=pltpu.CompilerParams(dimension_semantics=("parallel", "arbitrary")),
    )(tabs, pg, b_mat, c_mat)


def _s5_direction_bwd(pg, dy, xs, b_mat, c_mat, tabs_conj, reverse, *, name):
    length = pg.shape[0]
    tb = min(512, length)
    ntb = length // tb
    lb = S5_LANE_BLOCK
    wb = 2 * lb
    ntile = tb // 8
    adj_rev = not reverse
    if reverse:
        edge = jnp.concatenate([xs[tb::tb], jnp.zeros((1, xs.shape[1]), F32)], axis=0)
    else:
        edge = jnp.concatenate([jnp.zeros((1, xs.shape[1]), F32), xs[tb - 1:length - 1:tb]], axis=0)
    edge = edge.reshape(ntb, 1, xs.shape[1])

    def body(tab_ref, u_ref, dy_ref, x_ref, edge_ref, b_ref, c_ref, du_ref, db_ref, dc_ref, da_ref, carry_ref, g_ref, lam_ref):
        @pl.when(pl.program_id(1) == 0)
        def _():
            carry_ref[...] = jnp.zeros_like(carry_ref)
            da_ref[...] = jnp.zeros_like(da_ref)
            db_ref[...] = jnp.zeros_like(db_ref)
            dc_ref[...] = jnp.zeros_like(dc_ref)

        dyb = dy_ref[...].astype(BF16)
        g_ref[...] = lax.dot_general(dyb, c_ref[0], _NT, preferred_element_type=F32)
        rows = lax.broadcasted_iota(jnp.int32, (8, wb), 0)

        def step(i, carry):
            prev, acc = carry
            k = ntile - 1 - i if adj_rev else i
            r0 = pl.multiple_of(k * 8, 8)
            lam = _s5_scan_tile(g_ref[pl.ds(r0, 8), :], tab_ref, prev, adj_rev)
            lam_ref[pl.ds(r0, 8), :] = lam
            x = x_ref[pl.ds(r0, 8), :]
            if reverse:
                kn = jnp.minimum(k + 1, ntile - 1)
                nb = x_ref[pl.ds(pl.multiple_of(kn * 8, 8), 8), :][0:1, :]
                nb = jnp.where(k == ntile - 1, edge_ref[0], nb)
                xp = jnp.where(rows == 7, jnp.broadcast_to(nb, (8, wb)), pltpu.roll(x, 7, 0))
            else:
                kn = jnp.maximum(k - 1, 0)
                nb = x_ref[pl.ds(pl.multiple_of(kn * 8, 8), 8), :][7:8, :]
                nb = jnp.where(k == 0, edge_ref[0], nb)
                xp = jnp.where(rows == 0, jnp.broadcast_to(nb, (8, wb)), pltpu.roll(x, 1, 0))
            xr, xi, lr, li = xp[:, :lb], xp[:, lb:], lam[:, :lb], lam[:, lb:]
            return lam, acc + jnp.concatenate([xr * lr + xi * li, xr * li - xi * lr], axis=1)

        last, acc = lax.fori_loop(0, ntile, step, (carry_ref[...], da_ref[...]))
        carry_ref[...] = last
        da_ref[...] = acc
        lamb = lam_ref[...].astype(BF16)
        du_ref[...] = lax.dot_general(lamb, b_ref[0], _NT, preferred_element_type=F32)
        db_ref[0] += lax.dot_general(u_ref[...].astype(BF16), lamb, _TN, preferred_element_type=F32)
        dc_ref[0] += lax.dot_general(dyb, x_ref[...].astype(BF16), _TN, preferred_element_type=F32)

    tix = (lambda t: ntb - 1 - t) if adj_rev else (lambda t: t)
    wide = pl.BlockSpec((tb, wb), lambda c, t: (tix(t), c))
    mat = pl.BlockSpec((1, S5_BLK_CH, wb), lambda c, t: (c, 0, 0))
    return pl.pallas_call(
        body, name=name, grid=(S5_NBLK, ntb),
        in_specs=[pl.BlockSpec((4, 8, wb), lambda c, t: (0, 0, c)),
                  pl.BlockSpec((tb, S5_BLK_CH), lambda c, t: (tix(t), S5_CB_U + c)),
                  pl.BlockSpec((tb, S5_BLK_CH), lambda c, t: (tix(t), c)), wide,
                  pl.BlockSpec((1, 1, wb), lambda c, t: (tix(t), 0, c)), mat,
                  pl.BlockSpec((1, wb, S5_BLK_CH), lambda c, t: (c, 0, 0))],
        out_specs=[pl.BlockSpec((tb, S5_BLK_CH), lambda c, t: (tix(t), c)), mat, mat, pl.BlockSpec((8, wb), lambda c, t: (0, c))],
        out_shape=[jax.ShapeDtypeStruct((length, S5_WIDTH), F32), jax.ShapeDtypeStruct((S5_NBLK, S5_BLK_CH, wb), F32),
                   jax.ShapeDtypeStruct((S5_NBLK, S5_BLK_CH, wb), F32), jax.ShapeDtypeStruct((8, S5_NBLK * wb), F32)],
        scratch_shapes=[pltpu.VMEM((8, wb), F32), pltpu.VMEM((tb, wb), F32), pltpu.VMEM((tb, wb), F32)],
        compiler_params=pltpu.CompilerParams(dimension_semantics=("parallel", "arbitrary")),
    )(tabs_conj, pg, dy, xs, edge, b_mat, c_mat)


def _s5_fwd(p_in, prm, w_glu):
    dirs = []
    ys = []
    for d, reverse in ((0, False), (1, True)):
        a_re, a_im, bb_re, bb_im = _s5_prep(prm['lambda_re'][d], prm['lambda_im'][d], prm['log_dt'][d],
                                            prm['b_re'][d], prm['b_im'][d])
        b_mat = _s5_in_matrix(bb_re, bb_im).astype(BF16)
        c_mat = _s5_out_matrix(prm['c_re'][d], prm['c_im'][d]).astype(BF16)
        xs, y_dir = _s5_direction_fwd(p_in, b_mat, c_mat, _s5_tables(a_re, a_im, reverse), reverse,
                                      name="s5_fwd_rev" if reverse else "s5_fwd")
        ys.append(y_dir)
        dirs.append((a_re, a_im, b_mat, c_mat, xs))

    def post(yf, yb, u, dskip):
        ypre = yf + yb + dskip * u
        th, _ = _gelu_parts(ypre)
        return ypre, 0.5 * ypre * (1.0 + th)
    ypre, yg = _rowmap(post, [ys[0], ys[1], (p_in, CB_U, S5_WIDTH)], [prm['d'].reshape(1, -1)],
                       [(S5_WIDTH, F32), (S5_WIDTH, F32)], tl=512, name="s5_post")
    t = _mm(yg, w_glu, name="s5_glu_mm")

    def glu(ygv, tv):
        return ygv * _sigmoid(tv)
    y = _rowmap(glu, [yg, t], [], [(S5_WIDTH, BF16)], tl=512, name="s5_glu")[0]
    return y, (dirs, ypre, yg, t)


def _s5_bwd(pg, prm, w_glu, saved, dy):
    dirs, ypre, yg, t = saved

    def glu_bwd(dyv, ygv, tv):
        s = _sigmoid(tv)
        return dyv * ygv * s * (1.0 - s), dyv * s
    dt, dyg_direct = _rowmap(glu_bwd, [dy, yg, t], [], [(S5_WIDTH, BF16), (S5_WIDTH, F32)], tl=512, name="s5_glu_bwd")
    grads = {'w_glu': _mm(yg, dt, ta=True, name="s5_dwglu")}
    dyg_mm = _mm(dt, w_glu, tb=True, name="s5_dyg")

    def post_bwd(dyd, dym, yp, u, dskip):
        th, dinner = _gelu_parts(yp)
        dyp = (dyd + dym) * (0.5 * (1.0 + th) + 0.5 * yp * (1.0 - th * th) * dinner)
        return dyp, dyp * dskip, jnp.sum(dyp * u, axis=0, keepdims=True)
    dyp, du_skip, dd = _rowmap(post_bwd, [dyg_direct, dyg_mm, ypre, (pg, CB_U, S5_WIDTH)], [prm['d'].reshape(1, -1)],
                               [(S5_WIDTH, F32), (S5_WIDTH, F32)], [(1, S5_WIDTH)], tl=512, name="s5_post_bwd")
    grads['d'] = dd[0]
    du = [du_skip]
    per_dir = []
    for d, reverse in ((0, False), (1, True)):
        a_re, a_im, b_mat, c_mat, xs = dirs[d]
        du_dir, d_bmat, d_cmat_t, da = _s5_direction_bwd(pg, dyp, xs, b_mat, c_mat, _s5_tables(a_re, -a_im, not reverse),
                                                         reverse, name="s5_bwd_rev" if reverse else "s5_bwd")
        du.append(du_dir)
        dbb_re, dbb_im = _s5_in_matrix_grad(d_bmat)
        dc_re, dc_im = _s5_out_matrix_grad(d_cmat_t)
        da_re, da_im = _s5_unblocked(jnp.sum(da, axis=0))
        _, vjp = jax.vjp(_s5_prep, prm['lambda_re'][d], prm['lambda_im'][d], prm['log_dt'][d], prm['b_re'][d], prm['b_im'][d])
        per_dir.append(vjp((da_re, da_im, dbb_re, dbb_im)) + (dc_re, dc_im))
    for i, key in enumerate(('lambda_re', 'lambda_im', 'log_dt', 'b_re', 'b_im', 'c_re', 'c_im')):
        grads[key] = jnp.stack([per_dir[0][i], per_dir[1][i]])
    return du, grads


def _split3(x):
    hi = x.astype(BF16)
    r = x - hi.astype(F32)
    mid = r.astype(BF16)
    return hi, mid, (r - mid.astype(F32)).astype(BF16)


def _exact_dot(ones, x, dims):
    parts = [lax.dot_general(ones, p, dims, preferred_element_type=F32) for p in _split3(x)]
    return parts[0] + parts[1] + parts[2]


_NN = (((1,), (0,)), ((), ()))
_NT = (((1,), (1,)), ((), ()))
_TN = (((0,), (0,)), ((), ()))


def _dot(a, b, dims=_NN):
    return lax.dot_general(a.astype(BF16), b.astype(BF16), dims, preferred_element_type=F32)


def _gla_chunk_mask(reverse):
    rows = lax.broadcasted_iota(jnp.int32, (GLA_CHUNK, GLA_CHUNK), 0)
    cols = lax.broadcasted_iota(jnp.int32, (GLA_CHUNK, GLA_CHUNK), 1)
    return (cols >= rows) if reverse else (cols <= rows)


def _gla_fwd(pg, la, reverse, *, name):
    length = la.shape[0]
    nch = length // GLA_CHUNK
    scale = GLA_HEAD_DIM ** -0.5
    last = 0 if reverse else GLA_CHUNK - 1
    hd = GLA_HEAD_DIM

    def body(q_ref, k_ref, v_ref, la_ref, o_ref, sp_ref, st_ref):
        @pl.when(pl.program_id(0) == 0)
        def _():
            st_ref[...] = jnp.zeros_like(st_ref)

        mask = _gla_chunk_mask(reverse)
        b = _exact_dot(mask.astype(BF16), la_ref[...], _NN)
        sp_ref[0] = st_ref[...]
        outs = []
        for h in range(GLA_HEADS):
            sl = slice(h * hd, (h + 1) * hd)
            bh = b[:, sl]
            bl = bh[last:last + 1, :]
            k = k_ref[:, sl]
            v = v_ref[:, sl]
            qd = q_ref[:, sl] * scale * jnp.exp(bh)
            kd = k * jnp.exp(-bh)
            ke = k * jnp.exp(bl - bh)
            st = st_ref[sl, :]
            p = jnp.where(mask, _dot(qd, kd, _NT), 0.0)
            outs.append(_dot(p, v) + _dot(qd, st, _NT))
            st_ref[sl, :] = st * jnp.exp(bl) + _dot(v, ke, _TN)
        o_ref[...] = jnp.concatenate(outs, axis=1)

    cmap = (lambda n: nch - 1 - n) if reverse else (lambda n: n)
    col = lambda cb: pl.BlockSpec((GLA_CHUNK, GLA_WIDTH), lambda n, cb=cb: (cmap(n), cb))
    return pl.pallas_call(
        body, name=name, grid=(nch,),
        in_specs=[col(CB_GQ), col(CB_GK), col(CB_GV), col(0)],
        out_specs=[col(0), pl.BlockSpec((1, GLA_WIDTH, hd), lambda n: (cmap(n), 0, 0))],
        out_shape=[jax.ShapeDtypeStruct((length, GLA_WIDTH), F32), jax.ShapeDtypeStruct((nch, GLA_WIDTH, hd), F32)],
        scratch_shapes=[pltpu.VMEM((GLA_WIDTH, hd), F32)],
        compiler_params=pltpu.CompilerParams(dimension_semantics=("arbitrary",)),
    )(pg, pg, pg, la)


def _gla_bwd(pg, la, do, sprev, reverse, *, name):
    length = la.shape[0]
    nch = length // GLA_CHUNK
    scale = GLA_HEAD_DIM ** -0.5
    last = 0 if reverse else GLA_CHUNK - 1
    hd = GLA_HEAD_DIM

    def body(q_ref, k_ref, v_ref, la_ref, do_ref, sp_ref, dq_ref, dk_ref, dv_ref, dla_ref, dst_ref):
        @pl.when(pl.program_id(0) == 0)
        def _():
            dst_ref[...] = jnp.zeros_like(dst_ref)

        mask = _gla_chunk_mask(reverse)
        tri = mask.astype(BF16)
        b = _exact_dot(tri, la_ref[...], _NN)
        is_last = lax.broadcasted_iota(jnp.int32, (GLA_CHUNK, hd), 0) == last
        dqs, dks, dvs, dbs = [], [], [], []
        for h in range(GLA_HEADS):
            sl = slice(h * hd, (h + 1) * hd)
            bh = b[:, sl]
            bl = bh[last:last + 1, :]
            eb, enb, ebl, el = jnp.exp(bh), jnp.exp(-bh), jnp.exp(bl - bh), jnp.exp(bl)
            k = k_ref[:, sl]
            v = v_ref[:, sl]
            dov = do_ref[:, sl]
            qd = q_ref[:, sl] * scale * eb
            kd = k * enb
            ke = k * ebl
            st = sp_ref[0, sl, :]
            dst = dst_ref[sl, :]
            p = jnp.where(mask, _dot(qd, kd, _NT), 0.0)
            dp = jnp.where(mask, _dot(dov, v, _NT), 0.0)
            dqd = _dot(dp, kd) + _dot(dov, st)
            dkd = _dot(dp, qd, _TN)
            dvs.append(_dot(p, dov, _TN) + _dot(ke, dst, _NT))
            dke = _dot(v, dst)
            dst_ref[sl, :] = dst * el + _dot(dov, qd, _TN)
            dbl = el * jnp.sum(dst * st, axis=0, keepdims=True) + jnp.sum(dke * ke, axis=0, keepdims=True)
            db = dqd * qd - dkd * kd - dke * ke
            dbs.append(jnp.where(is_last, db + dbl, db))
            dqs.append(dqd * eb * scale)
            dks.append(dkd * enb + dke * ebl)
        dq_ref[...] = jnp.concatenate(dqs, axis=1)
        dk_ref[...] = jnp.concatenate(dks, axis=1)
        dv_ref[...] = jnp.concatenate(dvs, axis=1)
        tri_t = _gla_chunk_mask(not reverse).astype(BF16)
        dla_ref[...] = _exact_dot(tri_t, jnp.concatenate(dbs, axis=1), _NN)

    cmap = (lambda n: n) if reverse else (lambda n: nch - 1 - n)
    col = lambda cb: pl.BlockSpec((GLA_CHUNK, GLA_WIDTH), lambda n, cb=cb: (cmap(n), cb))
    wide = jax.ShapeDtypeStruct((length, GLA_WIDTH), F32)
    return pl.pallas_call(
        body, name=name, grid=(nch,),
        in_specs=[col(CB_GQ), col(CB_GK), col(CB_GV), col(0), col(0),
                  pl.BlockSpec((1, GLA_WIDTH, hd), lambda n: (cmap(n), 0, 0))],
        out_specs=[col(0)] * 4, out_shape=[wide] * 4,
        scratch_shapes=[pltpu.VMEM((GLA_WIDTH, hd), F32)],
        compiler_params=pltpu.CompilerParams(dimension_semantics=("arbitrary",)),
    )(pg, pg, pg, la, do, sprev)


def _log_sigmoid(x):
    return jnp.minimum(x, 0.0) - jnp.log(1.0 + jnp.exp(-jnp.abs(x)))


def _gla_alpha_padded(w_alpha):
    w = jnp.zeros((2, 128, GLA_WIDTH), w_alpha.dtype)
    w = w.at[0, 0:GLA_LOWRANK].set(w_alpha[0])
    return w.at[1, GLA_LOWRANK:2 * GLA_LOWRANK].set(w_alpha[1])


def _gla_branch_fwd(pg, w_alpha, b_alpha, norm_gain):
    wa = _gla_alpha_padded(w_alpha).astype(BF16)

    def gates(z, w, bias):
        return (_log_sigmoid(_dot(z, w[0]) + bias[0:1]) / GLA_TAU, _log_sigmoid(_dot(z, w[1]) + bias[1:2]) / GLA_TAU)
    la_f, la_b = _rowmap(gates, [(pg, CB_Z, 128)], [wa, b_alpha], [(GLA_WIDTH, F32), (GLA_WIDTH, F32)], tl=512,
                         name="gla_gates")
    o_f, sp_f = _gla_fwd(pg, la_f, False, name="gla_fwd")
    o_b, sp_b = _gla_fwd(pg, la_b, True, name="gla_fwd_rev")

    def post(of, ob, gate, gn):
        o = of + ob
        on = jnp.concatenate([o[:, s:s + GLA_HEAD_DIM] * _rms(o[:, s:s + GLA_HEAD_DIM]) * gn
                              for s in range(0, GLA_WIDTH, GLA_HEAD_DIM)], axis=1)
        return o, on * (gate * _sigmoid(gate))
    o, y = _rowmap(post, [o_f, o_b, (pg, CB_GG, GLA_WIDTH)], [norm_gain.reshape(1, -1)],
                   [(GLA_WIDTH, F32), (GLA_WIDTH, BF16)], tl=512, name="gla_post")
    return y, (wa, la_f, la_b, sp_f, sp_b, o)


def _gla_branch_bwd(pg, w_alpha, b_alpha, norm_gain, saved, dy):
    wa, la_f, la_b, sp_f, sp_b, o = saved

    def post_bwd(dyv, ov, gate, gn):
        s = _sigmoid(gate)
        dos, dgn, ons = [], [], []
        for c in range(0, GLA_WIDTH, GLA_HEAD_DIM):
            oh = ov[:, c:c + GLA_HEAD_DIM]
            r = _rms(oh)
            don = dyv[:, c:c + GLA_HEAD_DIM] * (gate[:, c:c + GLA_HEAD_DIM] * s[:, c:c + GLA_HEAD_DIM])
            gd = don * gn
            dos.append(r * gd - oh * (r * r * r) * jnp.mean(oh * gd, axis=-1, keepdims=True))
            dgn.append(jnp.sum(don * oh * r, axis=0, keepdims=True))
            ons.append(oh * r * gn)
        on = jnp.concatenate(ons, axis=1)
        dgate = dyv * on * (s * (1.0 + gate * (1.0 - s)))
        return jnp.concatenate(dos, axis=1), dgate, jnp.concatenate(dgn, axis=1)
    do, dgate, dgn = _rowmap(post_bwd, [dy, o, (pg, CB_GG, GLA_WIDTH)], [norm_gain.reshape(1, -1)],
                             [(GLA_WIDTH, F32), (GLA_WIDTH, F32)], [(1, GLA_WIDTH)], tl=512, name="gla_post_bwd")
    dq_f, dk_f, dv_f, dla_f = _gla_bwd(pg, la_f, do, sp_f, False, name="gla_bwd")
    dq_b, dk_b, dv_b, dla_b = _gla_bwd(pg, la_b, do, sp_b, True, name="gla_bwd_rev")

    def gates_bwd(z, dlf, dlb, w, bias):
        dz = jnp.zeros_like(z)
        dlogits, dbs = [], []
        for d, dl in ((0, dlf), (1, dlb)):
            logit = _dot(z, w[d]) + bias[d:d + 1]
            dlogit = dl * (1.0 / GLA_TAU) * _sigmoid(-logit)
            dz = dz + _dot(dlogit, w[d], _NT)
            dlogits.append(dlogit)
            dbs.append(jnp.sum(dlogit, axis=0, keepdims=True))
        return dz, dlogits[0], dlogits[1], dbs[0], dbs[1]
    dz, dlg_f, dlg_b, dba_f, dba_b = _rowmap(
        gates_bwd, [(pg, CB_Z, 128), dla_f, dla_b], [wa, b_alpha], [(128, F32), (GLA_WIDTH, BF16), (GLA_WIDTH, BF16)],
        [(1, GLA_WIDTH), (1, GLA_WIDTH)], tl=512, name="gla_gates_bwd")
    dwa_f = _mm(dlg_f, (pg, CB_Z, 128), ta=True, name="gla_dwalpha")
    dwa_b = _mm(dlg_b, (pg, CB_Z, 128), ta=True, name="gla_dwalpha")
    grads = {'w_alpha': jnp.stack([dwa_f[:, 0:GLA_LOWRANK].T, dwa_b[:, GLA_LOWRANK:2 * GLA_LOWRANK].T]),
             'b_alpha': jnp.concatenate([dba_f, dba_b], axis=0),
             'norm': jnp.sum(dgn.reshape(GLA_HEADS, GLA_HEAD_DIM), axis=0)}
    return [dq_f, dq_b], [dk_f, dk_b], [dv_f, dv_b], dgate, dz, grads


def _rope_tables(length):
    half = ATTN_HEAD_DIM // 2
    inv_freq = ROPE_BASE ** (-jnp.arange(half // 2, dtype=F32) * 2.0 / half)
    t = jnp.arange(length, dtype=jnp.int32)
    def one(pos):
        ang = pos.astype(F32)[:, None] * inv_freq[None, :]
        c, s = jnp.cos(ang), jnp.sin(ang)
        return jnp.concatenate([c, c], axis=1), jnp.concatenate([-s, s], axis=1)
    c_r, s_r = one(t // GRID_W)
    c_c, s_c = one(t % GRID_W)
    return jnp.concatenate([c_r, c_c], axis=1), jnp.concatenate([s_r, s_c], axis=1)


def _rope_swap(y):
    w = y.shape[1]
    lane = lax.broadcasted_iota(jnp.int32, y.shape, 1)
    return jnp.where(lane % 32 < 16, pltpu.roll(y, w - 16, 1), pltpu.roll(y, 16, 1))


def _head_sums(x, ones):
    parts = [lax.dot_general(p, ones, _NN, preferred_element_type=F32) for p in _split3(x)]
    return parts[0] + parts[1] + parts[2]


def _head_ones(width):
    seg = np.arange(width) // ATTN_HEAD_DIM
    return jnp.asarray(seg[:, None] == seg[None, :], BF16)


def _qk_prep_fwd(pg, cb, width, gain, cos, sin, scale, *, name):
    heads = width // ATTN_HEAD_DIM
    def fn(x, c, s, g, ones):
        r = lax.rsqrt(_head_sums(x * x, ones) * (1.0 / ATTN_HEAD_DIM) + NORM_EPS)
        y = x * r * g
        return (y * c + _rope_swap(y) * s) * scale
    return _rowmap(fn, [(pg, cb, width), jnp.tile(cos, (1, heads)), jnp.tile(sin, (1, heads))],
                   [jnp.tile(gain, heads).reshape(1, -1), _head_ones(width)], [(width, BF16)], tl=512, name=name)[0]


def _qk_prep_bwd(pg, cb, width, gain, cos, sin, scale, dout, *, name):
    heads = width // ATTN_HEAD_DIM
    def fn(x, dov, c, s, g, ones):
        r = lax.rsqrt(_head_sums(x * x, ones) * (1.0 / ATTN_HEAD_DIM) + NORM_EPS)
        dos = dov * scale
        dy = dos * c + _rope_swap(dos * s)
        gd = dy * g
        dx = r * gd - x * (r * r * r) * (_head_sums(x * gd, ones) * (1.0 / ATTN_HEAD_DIM))
        return dx, jnp.sum(dy * x * r, axis=0, keepdims=True)
    dx, dg = _rowmap(fn, [(pg, cb, width), dout, jnp.tile(cos, (1, heads)), jnp.tile(sin, (1, heads))],
                     [jnp.tile(gain, heads).reshape(1, -1), _head_ones(width)], [(width, F32)], [(1, width)], tl=512,
                     name=name)
    return dx, jnp.sum(dg.reshape(heads, ATTN_HEAD_DIM), axis=0)


def _to_heads(x, heads):
    return jnp.transpose(x.reshape(x.shape[0], heads, ATTN_HEAD_DIM), (1, 0, 2))


def _from_heads(x):
    return jnp.transpose(x, (1, 0, 2)).reshape(x.shape[1], x.shape[0] * ATTN_HEAD_DIM)


ATTN_GROUP = ATTN_Q_HEADS // ATTN_KV_HEADS
ATTN_TQ = 256


def _attn_fwd(q, k, v):
    length = q.shape[1]
    tq = min(ATTN_TQ, length)

    def body(q_ref, k_ref, v_ref, o_ref):
        kk, vv = k_ref[0], v_ref[0]
        for g in range(ATTN_GROUP):
            s = _dot(q_ref[g], kk, _NT)
            p = jnp.exp(s - jnp.max(s, axis=-1, keepdims=True))
            o_ref[g] = _dot(p, vv) / jnp.sum(p, axis=-1, keepdims=True)

    kv_spec = pl.BlockSpec((1, length, ATTN_HEAD_DIM), lambda h, i: (h, 0, 0))
    q_spec = pl.BlockSpec((ATTN_GROUP, tq, ATTN_HEAD_DIM), lambda h, i: (h, i, 0))
    return pl.pallas_call(
        body, name="attn_fwd", grid=(ATTN_KV_HEADS, length // tq), in_specs=[q_spec, kv_spec, kv_spec],
        out_specs=q_spec, out_shape=jax.ShapeDtypeStruct(q.shape, F32),
        compiler_params=pltpu.CompilerParams(dimension_semantics=("parallel", "parallel")),
    )(q, k, v)


def _attn_bwd(q, k, v, o, do):
    length = q.shape[1]
    tq = min(ATTN_TQ, length)

    def body(q_ref, k_ref, v_ref, o_ref, do_ref, dq_ref, dk_ref, dv_ref):
        @pl.when(pl.program_id(1) == 0)
        def _():
            dk_ref[...] = jnp.zeros_like(dk_ref)
            dv_ref[...] = jnp.zeros_like(dv_ref)

        kk, vv = k_ref[0], v_ref[0]
        for g in range(ATTN_GROUP):
            qg, dog = q_ref[g], do_ref[g]
            s = _dot(qg, kk, _NT)
            p = jnp.exp(s - jnp.max(s, axis=-1, keepdims=True))
            p = p / jnp.sum(p, axis=-1, keepdims=True)
            dp = _dot(dog, vv, _NT)
            ds = p * (dp - jnp.sum(dog * o_ref[g], axis=-1, keepdims=True))
            dq_ref[g] = _dot(ds, kk)
            dk_ref[0] += _dot(ds, qg, _TN)
            dv_ref[0] += _dot(p, dog, _TN)

    kv_spec = pl.BlockSpec((1, length, ATTN_HEAD_DIM), lambda h, i: (h, 0, 0))
    q_spec = pl.BlockSpec((ATTN_GROUP, tq, ATTN_HEAD_DIM), lambda h, i: (h, i, 0))
    return pl.pallas_call(
        body, name="attn_bwd", grid=(ATTN_KV_HEADS, length // tq),
        in_specs=[q_spec, kv_spec, kv_spec, q_spec, q_spec], out_specs=[q_spec, kv_spec, kv_spec],
        out_shape=[jax.ShapeDtypeStruct(q.shape, F32), jax.ShapeDtypeStruct(k.shape, F32),
                   jax.ShapeDtypeStruct(k.shape, F32)],
        compiler_params=pltpu.CompilerParams(dimension_semantics=("parallel", "arbitrary")),
    )(q, k, v, o, do)


def _attn_branch_fwd(pg, q_gain, k_gain):
    cos, sin = _rope_tables(pg.shape[0])
    qp = _qk_prep_fwd(pg, CB_AQ, ATTN_WIDTH, q_gain, cos, sin, ATTN_HEAD_DIM ** -0.5, name="attn_q_prep")
    kp = _qk_prep_fwd(pg, CB_AK, ATTN_KV_WIDTH, k_gain, cos, sin, 1.0, name="attn_k_prep")
    qh, kh = _to_heads(qp, ATTN_Q_HEADS), _to_heads(kp, ATTN_KV_HEADS)
    vh = _to_heads(pg[:, P_OFF + 3200:P_OFF + 3328].astype(BF16), ATTN_KV_HEADS)
    oh = _attn_fwd(qh, kh, vh)
    return _from_heads(oh).astype(BF16), (cos, sin, qh, kh, vh, oh)


def _attn_branch_bwd(pg, q_gain, k_gain, saved, dy):
    cos, sin, qh, kh, vh, oh = saved
    dqh, dkh, dvh = _attn_bwd(qh, kh, vh, oh, _to_heads(dy, ATTN_Q_HEADS))
    dq, dqg = _qk_prep_bwd(pg, CB_AQ, ATTN_WIDTH, q_gain, cos, sin, ATTN_HEAD_DIM ** -0.5, _from_heads(dqh),
                           name="attn_q_prep_bwd")
    dk, dkg = _qk_prep_bwd(pg, CB_AK, ATTN_KV_WIDTH, k_gain, cos, sin, 1.0, _from_heads(dkh), name="attn_k_prep_bwd")
    return dq, dk, _from_heads(dvh), {'q_norm': dqg, 'k_norm': dkg}


def _gate_cols():
    return [slice(i * D_MODEL, (i + 1) * D_MODEL) for i in range(3)]


def _mixer_fwd(x, lw):
    h = _rmsnorm_fwd(x, lw['mix_norm'])
    pg = _mm(h, lw['w_pg'], name="mix_in")
    y_s5, s_s5 = _s5_fwd(pg, lw['s5'], lw['s5_w_glu'])
    y_gla, s_gla = _gla_branch_fwd(pg, lw['gla_w_alpha'], lw['gla_b_alpha'], lw['gla_norm'])
    y_att, s_att = _attn_branch_fwd(pg, lw['attn_q_norm'], lw['attn_k_norm'])
    ys = (y_s5, y_gla, y_att)
    br = [_mm(y, lw[n], name="mix_branch") for y, n in zip(ys, ('w_branch_s5', 'w_branch_gla', 'w_branch_attn'))]

    def merge(g0, g1, g2, b0, b1, b2, bias):
        acc = None
        for g, b, c in zip((g0, g1, g2), (b0, b1, b2), _gate_cols()):
            term = _sigmoid(g + bias[:, c]) * b
            acc = term if acc is None else acc + term
        return acc
    merged = _rowmap(merge, [(pg, 0, D_MODEL), (pg, 1, D_MODEL), (pg, 2, D_MODEL)] + br,
                     [lw['b_merge_gate'].reshape(1, -1)], [(D_MODEL, BF16)], tl=256, name="mix_merge")[0]
    x_out = _mm(merged, lw['w_out'], add=x, name="mix_out")
    return x_out, (x, h, pg, ys, (s_s5, s_gla, s_att), br, merged)


def _mixer_bwd(saved, lw, dx_out):
    x, h, pg, ys, (s_s5, s_gla, s_att), br, merged = saved
    grads = {'w_out': _mm(merged, dx_out, ta=True, name="mix_dwout")}
    dmerged = _mm(dx_out, lw['w_out'], tb=True, name="mix_dmerged")

    def merge_bwd(g0, g1, g2, b0, b1, b2, dm, bias):
        dbr, dgp = [], []
        for g, b, c in zip((g0, g1, g2), (b0, b1, b2), _gate_cols()):
            s = _sigmoid(g + bias[:, c])
            dbr.append(dm * s)
            dgp.append(dm * b * (s * (1.0 - s)))
        dgp = jnp.concatenate(dgp, axis=1)
        return dbr[0], dbr[1], dbr[2], dgp, jnp.sum(dgp, axis=0, keepdims=True)
    d0, d1, d2, dgpre, dbias = _rowmap(
        merge_bwd, [(pg, 0, D_MODEL), (pg, 1, D_MODEL), (pg, 2, D_MODEL)] + br + [dmerged],
        [lw['b_merge_gate'].reshape(1, -1)], [(D_MODEL, BF16)] * 3 + [(GATE_WIDTH, BF16)], [(1, GATE_WIDTH)], tl=256,
        name="mix_merge_bwd")
    grads['b_merge_gate'] = dbias[0]
    dys = []
    for y, dbr, n in zip(ys, (d0, d1, d2), ('w_branch_s5', 'w_branch_gla', 'w_branch_attn')):
        grads[n] = _mm(y, dbr, ta=True, name="mix_dwbranch")
        dys.append(_mm(dbr, lw[n], tb=True, name="mix_dy"))
    du, g_s5 = _s5_bwd(pg, lw['s5'], lw['s5_w_glu'], s_s5, dys[0])
    dgq, dgk, dgv, dgg, dz, g_gla = _gla_branch_bwd(pg, lw['gla_w_alpha'], lw['gla_b_alpha'], lw['gla_norm'], s_gla, dys[1])
    daq, dak, dav, g_att = _attn_branch_bwd(pg, lw['attn_q_norm'], lw['attn_k_norm'], s_att, dys[2])

    def assemble(dgp, u0, u1, u2, q0, q1, k0, k1, v0, v1, gg, aq, ak, av, z):
        pad = jnp.zeros((dgp.shape[0], IN_PAD - 3456), F32)
        parts = [dgp.astype(F32), u0 + u1 + u2, q0 + q1, k0 + k1, v0 + v1, gg, aq, ak, av, z, pad]
        return jnp.concatenate(parts, axis=1)
    dpg = _rowmap(assemble, [dgpre] + du + dgq + dgk + dgv + [dgg, daq, dak, dav, dz], [], [(PG_WIDTH, BF16)], tl=256,
                  name="mix_dpg")[0]
    grads['w_pg'] = _mm(h, dpg, ta=True, name="mix_dwpg")
    dh = _mm(dpg, lw['w_pg'], tb=True, name="mix_dh")
    dx, grads['mix_norm'] = _rmsnorm_bwd(x, lw['mix_norm'], dh, dx_out)
    grads['s5'], grads['gla'], grads['attn'] = g_s5, g_gla, g_att
    return dx, grads


def _loss_head(x, gain, target):
    width = x.shape[1]

    def fn(xv, tv, g):
        r = _rms(xv)
        err = xv * r * g - tv
        dy = err * (1.0 / width)
        gd = dy * g
        dx = r * gd - xv * (r * r * r) * jnp.mean(xv * gd, axis=-1, keepdims=True)
        loss = jnp.sum(0.5 * jnp.mean(err * err, axis=-1, keepdims=True), axis=0, keepdims=True)
        return dx, jnp.broadcast_to(loss, (1, 128)), jnp.sum(dy * xv * r, axis=0, keepdims=True)
    dx, loss, dgain = _rowmap(fn, [x, target], [gain.reshape(1, -1)], [(width, F32)], [(1, 128), (1, width)], tl=256,
                              name="loss_head")
    return loss[0, 0], dx, dgain[0]


def _row_tile(rows, cap=256):
    for t in range(cap - cap % 16, 0, -16):
        if rows % t == 0:
            return t
    return rows


def _reduce_adamw(parts, w, m, v, *, name):
    _, r, c = parts.shape
    tr = _row_tile(r)

    def body(p_ref, w_ref, m_ref, v_ref, g_ref, d_ref, m2_ref, v2_ref):
        g = p_ref[0].astype(F32)
        for j in range(1, N_DEV):
            g = g + p_ref[j].astype(F32)
        m2 = ADAM_B1 * m_ref[...] + (1.0 - ADAM_B1) * g
        v2 = ADAM_B2 * v_ref[...] + (1.0 - ADAM_B2) * (g * g)
        m_hat = m2 / (1.0 - ADAM_B1 ** ADAM_STEP)
        v_hat = v2 / (1.0 - ADAM_B2 ** ADAM_STEP)
        g_ref[...] = g
        d_ref[...] = -ADAM_LR * (m_hat / (jnp.sqrt(v_hat) + ADAM_EPS) + ADAM_WD * w_ref[...])
        m2_ref[...] = m2
        v2_ref[...] = v2

    flat = pl.BlockSpec((tr, c), lambda i: (i, 0))
    return pl.pallas_call(
        body, name=name, grid=(r // tr,), in_specs=[pl.BlockSpec((N_DEV, tr, c), lambda i: (0, i, 0)), flat, flat, flat],
        out_specs=[flat] * 4, out_shape=[jax.ShapeDtypeStruct((r, c), F32)] * 4,
        compiler_params=pltpu.CompilerParams(dimension_semantics=("parallel",)),
    )(parts, w, m, v)


def _all_gather(blocks, *, name):
    n = len(blocks)

    def body(*refs):
        x_refs, out_refs = refs[:n], refs[n:2 * n]
        send_sems, recv_sems, local_sems = refs[2 * n:]
        x, y, c = lax.axis_index("x"), lax.axis_index("y"), lax.axis_index("c")
        me, sibling = (x, y, c), (x, y, 1 - c)
        chips = [(1 - x, y), (x, 1 - y), (1 - x, 1 - y)]

        def slot(t, px, py, pc):
            return out_refs[t].at[4 * px + 2 * py + pc]

        def copy(t, k, blk, to, own=False):
            return pltpu.make_async_remote_copy(
                src_ref=x_refs[t] if own else slot(t, *blk), dst_ref=slot(t, *blk), send_sem=send_sems.at[t, k],
                recv_sem=recv_sems.at[t, k], device_id=to, device_id_type=pl.DeviceIdType.MESH)

        mine = [pltpu.make_async_copy(x_refs[t], slot(t, *me), local_sems.at[t]) for t in range(n)]
        for cp in mine:
            cp.start()
        first = []
        for t in range(n):
            first.append(copy(t, 0, me, sibling, own=True))
            first += [copy(t, 1 + j, me, (*chip, c), own=True) for j, chip in enumerate(chips)]
        for cp in first:
            cp.start()
        passed = []
        for j, chip in enumerate(chips):
            for t in range(n):
                copy(t, 1 + j, (*chip, c), me).wait_recv()
                passed.append(copy(t, 4 + j, (*chip, c), sibling))
                passed[-1].start()
        for t in range(n):
            copy(t, 0, sibling, me).wait_recv()
        for j, chip in enumerate(chips):
            for t in range(n):
                copy(t, 4 + j, (*chip, 1 - c), me).wait_recv()
        for cp in first + passed:
            cp.wait_send()
        for cp in mine:
            cp.wait()

    hbm = pl.BlockSpec(memory_space=pl.ANY)
    return pl.pallas_call(
        body, name=name, out_shape=[jax.ShapeDtypeStruct((N_DEV,) + b.shape, b.dtype) for b in blocks],
        in_specs=[hbm] * n, out_specs=[hbm] * n,
        scratch_shapes=[pltpu.SemaphoreType.DMA((n, 7)), pltpu.SemaphoreType.DMA((n, 7)), pltpu.SemaphoreType.DMA((n,))],
    )(*blocks)


def _all_to_all(stacks, *, name):
    n = len(stacks)

    def body(*refs):
        g_refs, out_refs = refs[:n], refs[n:2 * n]
        send_sems, recv_sems, local_sems = refs[2 * n:]
        x, y, c = lax.axis_index("x"), lax.axis_index("y"), lax.axis_index("c")
        me = 4 * x + 2 * y + c
        mine = [pltpu.make_async_copy(g_refs[t].at[me], out_refs[t].at[me], local_sems.at[t]) for t in range(n)]
        for cp in mine:
            cp.start()
        copies = []
        for k in range(1, N_DEV):
            px, py, pc = x ^ (k >> 2 & 1), y ^ (k >> 1 & 1), c ^ (k & 1)
            for t in range(n):
                copies.append(pltpu.make_async_remote_copy(
                    src_ref=g_refs[t].at[4 * px + 2 * py + pc], dst_ref=out_refs[t].at[me], send_sem=send_sems.at[t, k - 1],
                    recv_sem=recv_sems.at[t, k - 1], device_id=(px, py, pc), device_id_type=pl.DeviceIdType.MESH))
        for cp in copies:
            cp.start()
        for cp in copies:
            cp.wait_recv()
        for cp in copies:
            cp.wait_send()
        for cp in mine:
            cp.wait()

    hbm = pl.BlockSpec(memory_space=pl.ANY)
    return pl.pallas_call(
        body, name=name, out_shape=[jax.ShapeDtypeStruct(s.shape, s.dtype) for s in stacks],
        in_specs=[hbm] * n, out_specs=[hbm] * n,
        scratch_shapes=[pltpu.SemaphoreType.DMA((n, 7)), pltpu.SemaphoreType.DMA((n, 7)), pltpu.SemaphoreType.DMA((n,))],
    )(*stacks)


SMALL_COLS = 128


def _pack_small(arrays):
    flat = jnp.concatenate([a.astype(F32).reshape(-1, SMALL_COLS) for a in arrays], axis=0)
    return jnp.pad(flat, ((0, -flat.shape[0] % 256), (0, 0)))


def _unpack_small(packed, shapes):
    out, off = [], 0
    for s in shapes:
        r = math.prod(s) // SMALL_COLS
        out.append(packed[off:off + r].reshape(s))
        off += r
    return out


def _split_shards(full, axis):
    shape = full.shape
    split = full.reshape(shape[:axis] + (N_DEV, shape[axis] // N_DEV) + shape[axis + 1:])
    return jnp.moveaxis(split, axis, 0)


def _join_shards(stack, axis):
    moved = jnp.moveaxis(stack, 0, axis)
    shape = moved.shape
    return moved.reshape(shape[:axis] + (shape[axis] * shape[axis + 1],) + shape[axis + 2:])


def _w_in_padded(w_in):
    pad = jnp.zeros(w_in.shape[:-1] + (IN_PAD - IN_WIDTH,), w_in.dtype)
    return jnp.concatenate([w_in[..., :2560], w_in[..., 2592:], w_in[..., 2560:2592], pad], axis=-1)


def _w_in_unpadded(w):
    return jnp.concatenate([w[..., :2560], w[..., 3328:3360], w[..., 2560:3328]], axis=-1)


S5_KEYS = ('lambda_re', 'lambda_im', 'log_dt', 'b_re', 'b_im', 'c_re', 'c_im', 'd')


def _layer_weights(full, w, i):
    lw = {n: w[n][i] for n in ('ffn1_norm', 'mix_norm', 'gla_norm', 'attn_q_norm', 'attn_k_norm', 'b_merge_gate', 'ffn2_norm')}
    lw['s5'] = {k: w['s5_' + k][i] for k in S5_KEYS}
    for f in ('ffn1', 'ffn2'):
        lw[f + '_w_gu'] = _ffn_interleave(full[f + '_w_gate'][i], full[f + '_w_up'][i])
        lw[f + '_w_down'] = full[f + '_w_down'][i]
    lw['w_pg'] = jnp.concatenate([full['w_merge_gate'][i], _w_in_padded(full['w_in'][i])], axis=1)
    for n in ('s5_w_glu', 'gla_w_alpha', 'w_branch_s5', 'w_branch_gla', 'w_branch_attn', 'w_out'):
        lw[n] = full[n][i]
    lw['gla_b_alpha'] = full['gla_b_alpha'][i].astype(F32)
    return lw


def _step_local(x, target, w, full):
    lws = [_layer_weights(full, w, i) for i in range(DEPTH)]
    saved = []
    for lw in lws:
        x, s1 = _ffn_fwd(x, lw['ffn1_norm'], lw['ffn1_w_gu'], lw['ffn1_w_down'])
        x, s2 = _mixer_fwd(x, lw)
        x, s3 = _ffn_fwd(x, lw['ffn2_norm'], lw['ffn2_w_gu'], lw['ffn2_w_down'])
        saved.append((s1, s2, s3))
    loss, dx, d_final = _loss_head(x, w['final_norm'], target)
    per_layer = []
    for lw, (s1, s2, s3) in reversed(list(zip(lws, saved))):
        g = {}
        dx, g['ffn2_norm'], dgu, g['ffn2_w_down'] = _ffn_bwd(s3, lw['ffn2_norm'], lw['ffn2_w_gu'], lw['ffn2_w_down'], dx)
        g['ffn2_w_gate'], g['ffn2_w_up'] = _ffn_deinterleave(dgu)
        dx, gm = _mixer_bwd(s2, lw, dx)
        dx, g['ffn1_norm'], dgu, g['ffn1_w_down'] = _ffn_bwd(s1, lw['ffn1_norm'], lw['ffn1_w_gu'], lw['ffn1_w_down'], dx)
        g['ffn1_w_gate'], g['ffn1_w_up'] = _ffn_deinterleave(dgu)
        g['w_merge_gate'] = gm['w_pg'][:, :GATE_WIDTH]
        g['w_in'] = _w_in_unpadded(gm['w_pg'][:, GATE_WIDTH:])
        for n in ('w_out', 'b_merge_gate', 'w_branch_s5', 'w_branch_gla', 'w_branch_attn', 'mix_norm'):
            g[n] = gm[n]
        for k in S5_KEYS:
            g['s5_' + k] = gm['s5'][k]
        g['s5_w_glu'] = gm['s5']['w_glu']
        g['gla_w_alpha'], g['gla_b_alpha'], g['gla_norm'] = gm['gla']['w_alpha'], gm['gla']['b_alpha'], gm['gla']['norm']
        g['attn_q_norm'], g['attn_k_norm'] = gm['attn']['q_norm'], gm['attn']['k_norm']
        per_layer.append(g)
    per_layer.reverse()
    return loss, dx, per_layer, d_final


def kernel(x, ffn1_norm, ffn1_w_gate, ffn1_w_up, ffn1_w_down, mix_norm, w_in, s5_lambda_re, s5_lambda_im, s5_log_dt, s5_b_re, s5_b_im, s5_c_re, s5_c_im, s5_d, s5_w_glu, gla_w_alpha, gla_b_alpha, gla_norm, attn_q_norm, attn_k_norm, w_branch_s5, w_branch_gla, w_branch_attn, w_merge_gate, b_merge_gate, w_out, ffn2_norm, ffn2_w_gate, ffn2_w_up, ffn2_w_down, final_norm, loss_target, m_ffn1_norm, m_ffn1_w_gate, m_ffn1_w_up, m_ffn1_w_down, m_mix_norm, m_w_in, m_s5_lambda_re, m_s5_lambda_im, m_s5_log_dt, m_s5_b_re, m_s5_b_im, m_s5_c_re, m_s5_c_im, m_s5_d, m_s5_w_glu, m_gla_w_alpha, m_gla_b_alpha, m_gla_norm, m_attn_q_norm, m_attn_k_norm, m_w_branch_s5, m_w_branch_gla, m_w_branch_attn, m_w_merge_gate, m_b_merge_gate, m_w_out, m_ffn2_norm, m_ffn2_w_gate, m_ffn2_w_up, m_ffn2_w_down, m_final_norm, v_ffn1_norm, v_ffn1_w_gate, v_ffn1_w_up, v_ffn1_w_down, v_mix_norm, v_w_in, v_s5_lambda_re, v_s5_lambda_im, v_s5_log_dt, v_s5_b_re, v_s5_b_im, v_s5_c_re, v_s5_c_im, v_s5_d, v_s5_w_glu, v_gla_w_alpha, v_gla_b_alpha, v_gla_norm, v_attn_q_norm, v_attn_k_norm, v_w_branch_s5, v_w_branch_gla, v_w_branch_attn, v_w_merge_gate, v_b_merge_gate, v_w_out, v_ffn2_norm, v_ffn2_w_gate, v_ffn2_w_up, v_ffn2_w_down, v_final_norm):
    return _train_step(x, ffn1_norm, ffn1_w_gate, ffn1_w_up, ffn1_w_down, mix_norm, w_in, s5_lambda_re, s5_lambda_im, s5_log_dt, s5_b_re, s5_b_im, s5_c_re, s5_c_im, s5_d, s5_w_glu, gla_w_alpha, gla_b_alpha, gla_norm, attn_q_norm, attn_k_norm, w_branch_s5, w_branch_gla, w_branch_attn, w_merge_gate, b_merge_gate, w_out, ffn2_norm, ffn2_w_gate, ffn2_w_up, ffn2_w_down, final_norm, loss_target, m_ffn1_norm, m_ffn1_w_gate, m_ffn1_w_up, m_ffn1_w_down, m_mix_norm, m_w_in, m_s5_lambda_re, m_s5_lambda_im, m_s5_log_dt, m_s5_b_re, m_s5_b_im, m_s5_c_re, m_s5_c_im, m_s5_d, m_s5_w_glu, m_gla_w_alpha, m_gla_b_alpha, m_gla_norm, m_attn_q_norm, m_attn_k_norm, m_w_branch_s5, m_w_branch_gla, m_w_branch_attn, m_w_merge_gate, m_b_merge_gate, m_w_out, m_ffn2_norm, m_ffn2_w_gate, m_ffn2_w_up, m_ffn2_w_down, m_final_norm, v_ffn1_norm, v_ffn1_w_gate, v_ffn1_w_up, v_ffn1_w_down, v_mix_norm, v_w_in, v_s5_lambda_re, v_s5_lambda_im, v_s5_log_dt, v_s5_b_re, v_s5_b_im, v_s5_c_re, v_s5_c_im, v_s5_d, v_s5_w_glu, v_gla_w_alpha, v_gla_b_alpha, v_gla_norm, v_attn_q_norm, v_attn_k_norm, v_w_branch_s5, v_w_branch_gla, v_w_branch_attn, v_w_merge_gate, v_b_merge_gate, v_w_out, v_ffn2_norm, v_ffn2_w_gate, v_ffn2_w_up, v_ffn2_w_down, v_final_norm)


def _train_step(*args):
    nw = len(W_NAMES)
    x, target = args[0][0], args[1 + nw][0]
    w = dict(zip(W_NAMES, args[1:1 + nw]))
    m = dict(zip(W_NAMES, args[2 + nw:2 + 2 * nw]))
    v = dict(zip(W_NAMES, args[2 + 2 * nw:2 + 3 * nw]))

    gathered = _all_gather([w[n].astype(BF16) for n in SHARDED], name="gather_weights")
    full = {n: _join_shards(g, SHARD_AXIS[n]) for n, g in zip(SHARDED, gathered)}

    loss, dx, per_layer, d_final = _step_local(x, target, w, full)
    loss = lax.psum(loss, ("x", "y", "c"))

    out = {}
    kinds = ('grad', 'delta', 'new_m', 'new_v')
    outgoing = [jnp.stack([_split_shards(g[n], SHARD_AXIS[n] - 1) for g in per_layer], axis=1).astype(BF16) for n in SHARDED]
    incoming = _all_to_all(outgoing, name="exchange_grads")
    for n, parts in zip(SHARDED, incoming):
        shape = w[n].shape
        flat = lambda a: a.reshape(-1, shape[-1])
        res = _reduce_adamw(parts.reshape(N_DEV, -1, shape[-1]), flat(w[n]), flat(m[n]), flat(v[n]), name="adamw_sharded")
        for kind, a in zip(kinds, res):
            out[kind + '_' + n] = a.reshape(shape)
    small = [jnp.stack([g[n] for g in per_layer]) if n != 'final_norm' else d_final for n in REPLICATED]
    parts = _all_gather([_pack_small(small)], name="gather_small_grads")[0]
    res = _reduce_adamw(parts, *[_pack_small([d[n] for n in REPLICATED]) for d in (w, m, v)], name="adamw_replicated")
    for kind, packed in zip(kinds, res):
        for n, a in zip(REPLICATED, _unpack_small(packed, [w[n].shape for n in REPLICATED])):
            out[kind + '_' + n] = a
    return (loss, dx[None]) + tuple(out[kind + '_' + n] for kind in kinds for n in W_NAMES)
```

```python
import functools
import math

import jax
import jax.numpy as jnp
import numpy as np
from jax import lax
from jax.experimental import pallas as pl
from jax.experimental.pallas import tpu as pltpu

F32 = jnp.float32
BF16 = jnp.bfloat16

N_DEV = 8
D_MODEL = 1024
DEPTH = 2
GRID_W = 64
D_FF = 2816
NORM_EPS = 1e-6
S5_GROUPS = 32
S5_GROUP_CH = 16
S5_STATE = 64
S5_WIDTH = 512
S5_NSTATE = S5_GROUPS * S5_STATE
S5_LANE_BLOCK = 512
GLA_HEADS = 4
GLA_HEAD_DIM = 128
GLA_WIDTH = 512
GLA_LOWRANK = 16
GLA_TAU = 16.0
GLA_CHUNK = 64
ATTN_Q_HEADS = 8
ATTN_KV_HEADS = 2
ATTN_HEAD_DIM = 64
ATTN_WIDTH = 512
ATTN_KV_WIDTH = 128
ROPE_BASE = 10000.0
IN_SPLITS = (512, 512, 512, 512, 512, 16, 16, 512, 128, 128)
IN_WIDTH = sum(IN_SPLITS)
IN_PAD = 3584
GATE_WIDTH = 3 * D_MODEL
PG_WIDTH = GATE_WIDTH + IN_PAD
P_OFF = GATE_WIDTH
CB_U, CB_GQ, CB_GK, CB_GV, CB_GG, CB_AQ = (P_OFF // 512 + i for i in range(6))
CB_AK, CB_AV, CB_Z = (P_OFF + 3072) // 128, (P_OFF + 3200) // 128, (P_OFF + 3328) // 128
ADAM_LR = 0.001
ADAM_B1 = 0.9
ADAM_B2 = 0.999
ADAM_EPS = 1e-08
ADAM_WD = 0.01
ADAM_STEP = 10
PACK_COLS = 1024

W_NAMES = ['ffn1_norm', 'ffn1_w_gate', 'ffn1_w_up', 'ffn1_w_down', 'mix_norm', 'w_in', 's5_lambda_re', 's5_lambda_im',
           's5_log_dt', 's5_b_re', 's5_b_im', 's5_c_re', 's5_c_im', 's5_d', 's5_w_glu', 'gla_w_alpha', 'gla_b_alpha',
           'gla_norm', 'attn_q_norm', 'attn_k_norm', 'w_branch_s5', 'w_branch_gla', 'w_branch_attn', 'w_merge_gate',
           'b_merge_gate', 'w_out', 'ffn2_norm', 'ffn2_w_gate', 'ffn2_w_up', 'ffn2_w_down', 'final_norm']
SHARD_AXIS = {'ffn1_w_gate': 2, 'ffn1_w_up': 2, 'ffn1_w_down': 1, 'w_in': 2, 's5_w_glu': 1, 'gla_w_alpha': 3,
              'gla_b_alpha': 2, 'w_branch_s5': 2, 'w_branch_gla': 2, 'w_branch_attn': 2, 'w_merge_gate': 2,
              'w_out': 1, 'ffn2_w_gate': 2, 'ffn2_w_up': 2, 'ffn2_w_down': 1}
SHARDED = [n for n in W_NAMES if n in SHARD_AXIS]
REPLICATED = [n for n in W_NAMES if n not in SHARD_AXIS]


def _pick(dim, prefs):
    for p in prefs:
        if dim % p == 0:
            return p
    return dim


def _sigmoid(x):
    return 1.0 / (1.0 + jnp.exp(-x))


def _mm(a, b, *, ta=False, tb=False, out_dtype=F32, scale=None, add=None, name):
    a, a_cb, a_w = a if isinstance(a, tuple) else (a, 0, a.shape[1])
    b, b_cb, b_w = b if isinstance(b, tuple) else (b, 0, b.shape[1])
    m, k = (a_w, a.shape[0]) if ta else (a.shape[0], a_w)
    n = b.shape[0] if tb else b_w
    assert (b_w if tb else b.shape[0]) == k, (a.shape, b.shape, ta, tb)
    tm, tn, tk = _mm_tiles(m, n, k, a.dtype.itemsize, b.dtype.itemsize, jnp.dtype(out_dtype).itemsize)
    nk = k // tk
    dims = (((0 if ta else 1,), (1 if tb else 0,)), ((), ()))
    a_off = a_cb * (a_w // (tm if ta else tk))
    b_off = b_cb * (b_w // (tk if tb else tn))

    def body(a_ref, b_ref, *rest):
        add_ref = rest[0] if add is not None else None
        o_ref, *acc = rest[1:] if add is not None else rest

        def finish(res):
            res = res if scale is None else res * scale
            return (res if add_ref is None else res + add_ref[...]).astype(out_dtype)

        part = lax.dot_general(a_ref[...].astype(BF16), b_ref[...].astype(BF16), dims, preferred_element_type=F32)
        if nk == 1:
            o_ref[...] = finish(part)
            return
        acc_ref, = acc
        kk = pl.program_id(2)

        @pl.when(kk == 0)
        def _():
            acc_ref[...] = part

        @pl.when(kk > 0)
        def _():
            acc_ref[...] += part

        @pl.when(kk == nk - 1)
        def _():
            o_ref[...] = finish(acc_ref[...])

    a_spec = (pl.BlockSpec((tk, tm), lambda i, j, kk: (kk, i + a_off)) if ta
              else pl.BlockSpec((tm, tk), lambda i, j, kk: (i, kk + a_off)))
    b_spec = (pl.BlockSpec((tn, tk), lambda i, j, kk: (j, kk + b_off)) if tb
              else pl.BlockSpec((tk, tn), lambda i, j, kk: (kk, j + b_off)))
    o_spec = pl.BlockSpec((tm, tn), lambda i, j, kk: (i, j))
    return pl.pallas_call(
        body, name=name, grid=(m // tm, n // tn, nk), in_specs=[a_spec, b_spec] + ([o_spec] if add is not None else []),
        out_specs=o_spec, out_shape=jax.ShapeDtypeStruct((m, n), out_dtype),
        scratch_shapes=[pltpu.VMEM((tm, tn), F32)] if nk > 1 else [],
        compiler_params=pltpu.CompilerParams(dimension_semantics=("parallel", "parallel", "arbitrary")),
    )(a, b, *([add] if add is not None else []))


MM_VMEM_BUDGET = 40 * 1024 * 1024


def _mm_tiles(m, n, k, a_bytes, b_bytes, out_bytes):
    tms = [t for t in (1024, 1408, 512, 256, 128) if m % t == 0] or [m]
    tns = [t for t in (512, 1408, 256, 128) if n % t == 0] or [n]
    tks = [k] + [t for t in (2048, 1024, 512, 256, 128) if k % t == 0 and t < k]
    for tk in tks:
        for tm in tms:
            for tn in tns:
                use = 2 * (tm * tk * a_bytes + tk * tn * b_bytes + tm * tn * out_bytes) + 2 * tm * tn * 4
                if use <= MM_VMEM_BUDGET:
                    return tm, tn, tk
    return tms[-1], tns[-1], tks[-1]


def _rowmap(fn, rows, consts, outs, reds=(), *, tl, name):
    rows = [r if isinstance(r, tuple) else (r, 0, r.shape[1]) for r in rows]
    length = rows[0][0].shape[0]
    tl = min(tl, length)
    nr, nc, no = len(rows), len(consts), len(outs)

    def body(*refs):
        res = fn(*[r[...] for r in refs[:nr + nc]])
        res = res if isinstance(res, tuple) else (res,)
        for o_ref, val in zip(refs[nr + nc:nr + nc + no], res[:no]):
            o_ref[...] = val.astype(o_ref.dtype)
        if reds:
            step = pl.program_id(0)
            red_refs = refs[nr + nc + no:]

            @pl.when(step == 0)
            def _():
                for d_ref, val in zip(red_refs, res[no:]):
                    d_ref[...] = val.astype(F32)

            @pl.when(step > 0)
            def _():
                for d_ref, val in zip(red_refs, res[no:]):
                    d_ref[...] += val.astype(F32)

    in_specs = [pl.BlockSpec((tl, w), lambda i, cb=cb: (i, cb)) for (_, cb, w) in rows]
    in_specs += [pl.BlockSpec(c.shape, lambda i, nd=c.ndim: (0,) * nd) for c in consts]
    out_specs = [pl.BlockSpec((tl, w), lambda i: (i, 0)) for (w, _) in outs]
    out_specs += [pl.BlockSpec(s, lambda i, nd=len(s): (0,) * nd) for s in reds]
    out_shape = [jax.ShapeDtypeStruct((length, w), dt) for (w, dt) in outs]
    out_shape += [jax.ShapeDtypeStruct(s, F32) for s in reds]
    res = pl.pallas_call(
        body, name=name, grid=(length // tl,), in_specs=in_specs, out_specs=out_specs, out_shape=out_shape,
        compiler_params=pltpu.CompilerParams(dimension_semantics=("arbitrary" if reds else "parallel",)),
    )(*[r[0] for r in rows], *consts)
    return res


def _rms(x):
    return lax.rsqrt(jnp.mean(x * x, axis=-1, keepdims=True) + NORM_EPS)


def _rmsnorm_fwd(x, gain):
    def fn(xv, g):
        return xv * _rms(xv) * g
    return _rowmap(fn, [x], [gain.reshape(1, -1)], [(x.shape[1], BF16)], tl=256, name="rmsnorm_fwd")[0]


def _rmsnorm_bwd(x, gain, dh, dres):
    def fn(xv, dhv, drv, g):
        r = _rms(xv)
        gd = dhv * g
        dx = r * gd - xv * (r * r * r) * jnp.mean(xv * gd, axis=-1, keepdims=True)
        return drv + dx, jnp.sum(dhv * xv * r, axis=0, keepdims=True)
    dx, dg = _rowmap(fn, [x, dh, dres], [gain.reshape(1, -1)], [(x.shape[1], F32)], [(1, x.shape[1])], tl=256,
                     name="rmsnorm_bwd")
    return dx, dg[0]


FFN_UNIT = D_FF // 2


def _ffn_up(h, w_gate, w_up):
    length, k = h.shape
    tm = _pick(length, (512, 256, 128))

    def body(h_ref, wg_ref, wu_ref, a_ref, g_ref, u_ref):
        hv = h_ref[...]
        g = jnp.dot(hv, wg_ref[...], preferred_element_type=F32)
        u = jnp.dot(hv, wu_ref[...], preferred_element_type=F32)
        a_ref[...] = (g * _sigmoid(g) * u).astype(BF16)
        g_ref[...] = g.astype(BF16)
        u_ref[...] = u.astype(BF16)

    w_spec = pl.BlockSpec((k, FFN_UNIT), lambda j, i: (0, j))
    o_spec = pl.BlockSpec((tm, FFN_UNIT), lambda j, i: (i, j))
    return pl.pallas_call(
        body, name="ffn_up", grid=(D_FF // FFN_UNIT, length // tm),
        in_specs=[pl.BlockSpec((tm, k), lambda j, i: (i, 0)), w_spec, w_spec], out_specs=[o_spec] * 3,
        out_shape=[jax.ShapeDtypeStruct((length, D_FF), BF16)] * 3,
        compiler_params=pltpu.CompilerParams(dimension_semantics=("parallel", "parallel")),
    )(h, w_gate, w_up)


def _ffn_dgu(dxo, w_down, g, u):
    length, k = dxo.shape
    tm = _pick(length, (512, 256, 128))

    def body(d_ref, w_ref, g_ref, u_ref, dg_ref, du_ref):
        da = 0.5 * lax.dot_general(d_ref[...], w_ref[...], _NT, preferred_element_type=F32)
        gv = g_ref[...].astype(F32)
        s = _sigmoid(gv)
        dg_ref[...] = (da * u_ref[...].astype(F32) * (s * (1.0 + gv * (1.0 - s)))).astype(BF16)
        du_ref[...] = (da * (gv * s)).astype(BF16)

    o_spec = pl.BlockSpec((tm, FFN_UNIT), lambda j, i: (i, j))
    return pl.pallas_call(
        body, name="ffn_dgu", grid=(D_FF // FFN_UNIT, length // tm),
        in_specs=[pl.BlockSpec((tm, k), lambda j, i: (i, 0)), pl.BlockSpec((FFN_UNIT, k), lambda j, i: (j, 0)), o_spec, o_spec],
        out_specs=[o_spec] * 2, out_shape=[jax.ShapeDtypeStruct((length, D_FF), BF16)] * 2,
        compiler_params=pltpu.CompilerParams(dimension_semantics=("parallel", "parallel")),
    )(dxo, w_down, g, u)


def _ffn_fwd(x, gain, w_gate, w_up, w_down):
    h = _rmsnorm_fwd(x, gain)
    a, g, u = _ffn_up(h, w_gate, w_up)
    x_out = _mm(a, w_down, scale=0.5, add=x, name="ffn_down")
    return x_out, (x, h, g, u, a)


def _ffn_bwd(saved, gain, w_gate, w_up, w_down, dx_out):
    x, h, g, u, a = saved
    dxo = dx_out.astype(BF16)
    d_wdown = _mm(a, dxo, ta=True, scale=0.5, out_dtype=BF16, name="ffn_dwdown")
    dg, du = _ffn_dgu(dxo, w_down, g, u)
    d_wgate = _mm(h, dg, ta=True, out_dtype=BF16, name="ffn_dwgu")
    d_wup = _mm(h, du, ta=True, out_dtype=BF16, name="ffn_dwgu")
    dh = _mm(du, w_up, tb=True, add=_mm(dg, w_gate, tb=True, name="ffn_dh"), name="ffn_dh_add")
    dx, dgain = _rmsnorm_bwd(x, gain, dh, dx_out)
    return dx, dgain, d_wgate, d_wup, d_wdown


def _s5_col(n):
    return (n // S5_LANE_BLOCK) * 2 * S5_LANE_BLOCK + n % S5_LANE_BLOCK


def _s5_blocked(re, im):
    lead = re.shape[:-1]
    nb = S5_NSTATE // S5_LANE_BLOCK
    both = jnp.stack([re.reshape(*lead, nb, S5_LANE_BLOCK), im.reshape(*lead, nb, S5_LANE_BLOCK)], axis=-2)
    return both.reshape(*lead, 2 * S5_NSTATE)


def _s5_unblocked(z):
    lead = z.shape[:-1]
    nb = S5_NSTATE // S5_LANE_BLOCK
    both = z.reshape(*lead, nb, 2, S5_LANE_BLOCK)
    return both[..., 0, :].reshape(*lead, S5_NSTATE), both[..., 1, :].reshape(*lead, S5_NSTATE)


def _s5_tables(a_re, a_im, reverse):
    a = lax.complex(a_re, a_im)
    a2 = a * a
    a4 = a2 * a2
    rows = jnp.arange(8)
    pw = [a]
    for _ in range(7):
        pw.append(pw[-1] * a)
    pw = jnp.stack(pw)
    if reverse:
        pw = pw[::-1]
    tabs = []
    for coef, s in ((a, 1), (a2, 2), (a4, 4)):
        live = (rows <= 7 - s) if reverse else (rows >= s)
        tabs.append(jnp.where(live[:, None], coef[None, :], 0.0))
    tabs.append(pw)
    tabs = jnp.stack(tabs)
    return _s5_blocked(jnp.real(tabs), jnp.imag(tabs))


def _s5_scan_tile(v, tab_ref, prev, reverse):
    lb = S5_LANE_BLOCK
    vr, vi = v[:, :lb], v[:, lb:]
    for idx, s in enumerate((1, 2, 4)):
        cr, ci = tab_ref[idx, :, :lb], tab_ref[idx, :, lb:]
        sh = 8 - s if reverse else s
        sr, si = pltpu.roll(vr, sh, 0), pltpu.roll(vi, sh, 0)
        vr, vi = vr + cr * sr - ci * si, vi + cr * si + ci * sr
    row = 0 if reverse else 7
    pr = jnp.broadcast_to(prev[row:row + 1, :lb], (8, lb))
    pi = jnp.broadcast_to(prev[row:row + 1, lb:], (8, lb))
    cr, ci = tab_ref[3, :, :lb], tab_ref[3, :, lb:]
    return jnp.concatenate([vr + cr * pr - ci * pi, vi + cr * pi + ci * pr], axis=1)


def _s5_scan(v, tabs, reverse, *, name):
    length = v.shape[0]
    tb = min(512, length)
    ntb = length // tb
    nlb = S5_NSTATE // S5_LANE_BLOCK
    wb = 2 * S5_LANE_BLOCK
    ntile = tb // 8

    def body(tab_ref, v_ref, x_ref, carry_ref):
        @pl.when(pl.program_id(1) == 0)
        def _():
            carry_ref[...] = jnp.zeros_like(carry_ref)

        def step(i, prev):
            r0 = pl.multiple_of((ntile - 1 - i if reverse else i) * 8, 8)
            x = _s5_scan_tile(v_ref[pl.ds(r0, 8), :], tab_ref, prev, reverse)
            x_ref[pl.ds(r0, 8), :] = x
            return x

        carry_ref[...] = lax.fori_loop(0, ntile, step, carry_ref[...])

    tmap = (lambda c, t: (ntb - 1 - t, c)) if reverse else (lambda c, t: (t, c))
    return pl.pallas_call(
        body, name=name, grid=(nlb, ntb),
        in_specs=[pl.BlockSpec((4, 8, wb), lambda c, t: (0, 0, c)), pl.BlockSpec((tb, wb), tmap)],
        out_specs=pl.BlockSpec((tb, wb), tmap), out_shape=jax.ShapeDtypeStruct(v.shape, F32),
        scratch_shapes=[pltpu.VMEM((8, wb), F32)],
        compiler_params=pltpu.CompilerParams(dimension_semantics=("parallel", "arbitrary")),
    )(tabs, v)


def _s5_scan_adjoint(g, xs, tabs_conj, reverse, *, name):
    length = g.shape[0]
    tb = min(512, length)
    ntb = length // tb
    nlb = S5_NSTATE // S5_LANE_BLOCK
    lb = S5_LANE_BLOCK
    wb = 2 * lb
    ntile = tb // 8
    adj_rev = not reverse
    if reverse:
        edge = jnp.concatenate([xs[tb::tb], jnp.zeros((1, xs.shape[1]), F32)], axis=0)
    else:
        edge = jnp.concatenate([jnp.zeros((1, xs.shape[1]), F32), xs[tb - 1:length - 1:tb]], axis=0)
    edge = edge.reshape(ntb, 1, xs.shape[1])

    def body(tab_ref, g_ref, x_ref, edge_ref, lam_ref, da_ref, carry_ref):
        @pl.when(pl.program_id(1) == 0)
        def _():
            carry_ref[...] = jnp.zeros_like(carry_ref)
            da_ref[...] = jnp.zeros_like(da_ref)

        rows = lax.broadcasted_iota(jnp.int32, (8, wb), 0)

        def step(i, carry):
            prev, acc = carry
            k = ntile - 1 - i if adj_rev else i
            r0 = pl.multiple_of(k * 8, 8)
            lam = _s5_scan_tile(g_ref[pl.ds(r0, 8), :], tab_ref, prev, adj_rev)
            lam_ref[pl.ds(r0, 8), :] = lam
            x = x_ref[pl.ds(r0, 8), :]
            if reverse:
                kn = jnp.minimum(k + 1, ntile - 1)
                nb = x_ref[pl.ds(pl.multiple_of(kn * 8, 8), 8), :][0:1, :]
                nb = jnp.where(k == ntile - 1, edge_ref[0], nb)
                xp = jnp.where(rows == 7, jnp.broadcast_to(nb, (8, wb)), pltpu.roll(x, 7, 0))
            else:
                kn = jnp.maximum(k - 1, 0)
                nb = x_ref[pl.ds(pl.multiple_of(kn * 8, 8), 8), :][7:8, :]
                nb = jnp.where(k == 0, edge_ref[0], nb)
                xp = jnp.where(rows == 0, jnp.broadcast_to(nb, (8, wb)), pltpu.roll(x, 1, 0))
            xr, xi, lr, li = xp[:, :lb], xp[:, lb:], lam[:, :lb], lam[:, lb:]
            acc = acc + jnp.concatenate([xr * lr + xi * li, xr * li - xi * lr], axis=1)
            return lam, acc

        last, acc = lax.fori_loop(0, ntile, step, (carry_ref[...], da_ref[...]))
        carry_ref[...] = last
        da_ref[...] = acc

    tmap = (lambda c, t: (ntb - 1 - t, c)) if adj_rev else (lambda c, t: (t, c))
    emap = (lambda c, t: (ntb - 1 - t, 0, c)) if adj_rev else (lambda c, t: (t, 0, c))
    return pl.pallas_call(
        body, name=name, grid=(nlb, ntb),
        in_specs=[pl.BlockSpec((4, 8, wb), lambda c, t: (0, 0, c)), pl.BlockSpec((tb, wb), tmap),
                  pl.BlockSpec((tb, wb), tmap), pl.BlockSpec((1, 1, wb), emap)],
        out_specs=[pl.BlockSpec((tb, wb), tmap), pl.BlockSpec((8, wb), lambda c, t: (0, c))],
        out_shape=[jax.ShapeDtypeStruct(g.shape, F32), jax.ShapeDtypeStruct((8, g.shape[1]), F32)],
        scratch_shapes=[pltpu.VMEM((8, wb), F32)],
        compiler_params=pltpu.CompilerParams(dimension_semantics=("parallel", "arbitrary")),
    )(tabs_conj, g, xs, edge)


def _s5_prep(lam_re, lam_im, log_dt, b_re, b_im):
    lam = lax.complex(lam_re, lam_im)
    dt = jnp.exp(log_dt)[:, None]
    lam_bar = jnp.exp(lam * dt)
    b_bar = ((lam_bar - 1.0) / lam)[..., None] * lax.complex(b_re, b_im)
    return (jnp.real(lam_bar).reshape(-1), jnp.imag(lam_bar).reshape(-1), jnp.real(b_bar), jnp.imag(b_bar))


S5_NBLK = S5_NSTATE // S5_LANE_BLOCK
S5_BLK_GROUPS = S5_GROUPS // S5_NBLK
S5_BLK_CH = S5_BLK_GROUPS * S5_GROUP_CH


def _s5_in_matrix(bb_re, bb_im):
    eye = jnp.eye(S5_BLK_GROUPS, dtype=F32)
    def dense(bb):
        b4 = bb.reshape(S5_NBLK, S5_BLK_GROUPS, S5_STATE, S5_GROUP_CH)
        return jnp.einsum('cgph,gk->cghkp', b4, eye).reshape(S5_NBLK, S5_BLK_CH, S5_LANE_BLOCK)
    return jnp.concatenate([dense(bb_re), dense(bb_im)], axis=-1)


def _s5_block_diagonal(d):
    d5 = d.reshape(S5_NBLK, S5_BLK_GROUPS, S5_GROUP_CH, S5_BLK_GROUPS, S5_STATE)
    eye = jnp.eye(S5_BLK_GROUPS, dtype=F32)
    return jnp.swapaxes(jnp.sum(d5 * eye[None, :, None, :, None], axis=1), 1, 2)


def _s5_in_matrix_grad(d_mat):
    def diag(d):
        return jnp.swapaxes(_s5_block_diagonal(d), 2, 3).reshape(S5_GROUPS, S5_STATE, S5_GROUP_CH)
    return diag(d_mat[..., :S5_LANE_BLOCK]), diag(d_mat[..., S5_LANE_BLOCK:])


def _s5_out_matrix(c_re, c_im):
    eye = jnp.eye(S5_BLK_GROUPS, dtype=F32)
    def dense(cc):
        c4 = cc.reshape(S5_NBLK, S5_BLK_GROUPS, S5_GROUP_CH, S5_STATE)
        return jnp.einsum('cghp,gk->cgpkh', c4, eye).reshape(S5_NBLK, S5_LANE_BLOCK, S5_BLK_CH)
    return jnp.concatenate([dense(c_re), dense(-c_im)], axis=1)


def _s5_out_matrix_grad(d_mat_t):
    def diag(d):
        return _s5_block_diagonal(d).reshape(S5_GROUPS, S5_GROUP_CH, S5_STATE)
    return diag(d_mat_t[..., :S5_LANE_BLOCK]), -diag(d_mat_t[..., S5_LANE_BLOCK:])


def _gmm(a, b, *, tb=False, name):
    arr, cb0, wa = a
    nblk = b.shape[0]
    wn = b.shape[1] if tb else b.shape[2]
    length = arr.shape[0]
    tm = _pick(length, (1024, 512, 256, 128))
    dims = (((1,), (1 if tb else 0,)), ((), ()))

    def body(a_ref, b_ref, o_ref):
        o_ref[...] = lax.dot_general(a_ref[...].astype(BF16), b_ref[0].astype(BF16), dims, preferred_element_type=F32)

    return pl.pallas_call(
        body, name=name, grid=(nblk, length // tm),
        in_specs=[pl.BlockSpec((tm, wa), lambda c, i: (i, cb0 + c)), pl.BlockSpec((1,) + b.shape[1:], lambda c, i: (c, 0, 0))],
        out_specs=pl.BlockSpec((tm, wn), lambda c, i: (i, c)), out_shape=jax.ShapeDtypeStruct((length, nblk * wn), F32),
        compiler_params=pltpu.CompilerParams(dimension_semantics=("parallel", "parallel")),
    )(arr, b)


def _gmm_tn(a, g, *, nblk, name):
    arr_a, cb_a, wa = a
    arr_g, cb_g, wg = g
    length = arr_a.shape[0]
    dims = (((0,), (0,)), ((), ()))

    def body(a_ref, g_ref, o_ref):
        o_ref[0] = lax.dot_general(a_ref[...].astype(BF16), g_ref[...].astype(BF16), dims, preferred_element_type=F32)

    return pl.pallas_call(
        body, name=name, grid=(nblk,),
        in_specs=[pl.BlockSpec((length, wa), lambda c: (0, cb_a + c)), pl.BlockSpec((length, wg), lambda c: (0, cb_g + c))],
        out_specs=pl.BlockSpec((1, wa, wg), lambda c: (c, 0, 0)), out_shape=jax.ShapeDtypeStruct((nblk, wa, wg), F32),
        compiler_params=pltpu.CompilerParams(dimension_semantics=("parallel",)),
    )(arr_a, arr_g)


def _gelu_parts(x):
    k = math.sqrt(2.0 / math.pi)
    inner = k * (x + 0.044715 * x * x * x)
    th = jnp.tanh(inner)
    return th, k * (1.0 + 3.0 * 0.044715 * x * x)


S5_CB_U = CB_U * (512 // S5_BLK_CH)


def _s5_direction_fwd(pg, b_mat, c_mat, tabs, reverse, *, name):
    length = pg.shape[0]
    tb = min(512, length)
    ntb = length // tb
    wb = 2 * S5_LANE_BLOCK
    ntile = tb // 8

    def body(tab_ref, u_ref, b_ref, c_ref, x_ref, y_ref, carry_ref, bu_ref):
        @pl.when(pl.program_id(1) == 0)
        def _():
            carry_ref[...] = jnp.zeros_like(carry_ref)

        bu_ref[...] = jnp.dot(u_ref[...].astype(BF16), b_ref[0], preferred_element_type=F32)

        def step(i, prev):
            r0 = pl.multiple_of((ntile - 1 - i if reverse else i) * 8, 8)
            x = _s5_scan_tile(bu_ref[pl.ds(r0, 8), :], tab_ref, prev, reverse)
            x_ref[pl.ds(r0, 8), :] = x
            return x

        carry_ref[...] = lax.fori_loop(0, ntile, step, carry_ref[...])
        y_ref[...] = jnp.dot(x_ref[...].astype(BF16), c_ref[0], preferred_element_type=F32)

    tix = (lambda t: ntb - 1 - t) if reverse else (lambda t: t)
    return pl.pallas_call(
        body, name=name, grid=(S5_NBLK, ntb),
        in_specs=[pl.BlockSpec((4, 8, wb), lambda c, t: (0, 0, c)),
                  pl.BlockSpec((tb, S5_BLK_CH), lambda c, t: (tix(t), S5_CB_U + c)),
                  pl.BlockSpec((1, S5_BLK_CH, wb), lambda c, t: (c, 0, 0)),
                  pl.BlockSpec((1, wb, S5_BLK_CH), lambda c, t: (c, 0, 0))],
        out_specs=[pl.BlockSpec((tb, wb), lambda c, t: (tix(t), c)), pl.BlockSpec((tb, S5_BLK_CH), lambda c, t: (tix(t), c))],
        out_shape=[jax.ShapeDtypeStruct((length, S5_NBLK * wb), F32), jax.ShapeDtypeStruct((length, S5_WIDTH), F32)],
        scratch_shapes=[pltpu.VMEM((8, wb), F32), pltpu.VMEM((tb, wb), F32)],
        compiler_params=pltpu.CompilerParams(dimension_semantics=("parallel", "arbitrary")),
    )(tabs, pg, b_mat, c_mat)


def _s5_direction_bwd(pg, dy, xs, b_mat, c_mat, tabs_conj, reverse, *, name):
    length = pg.shape[0]
    tb = min(512, length)
    ntb = length // tb
    lb = S5_LANE_BLOCK
    wb = 2 * lb
    ntile = tb // 8
    adj_rev = not reverse
    if reverse:
        edge = jnp.concatenate([xs[tb::tb], jnp.zeros((1, xs.shape[1]), F32)], axis=0)
    else:
        edge = jnp.concatenate([jnp.zeros((1, xs.shape[1]), F32), xs[tb - 1:length - 1:tb]], axis=0)
    edge = edge.reshape(ntb, 1, xs.shape[1])

    def body(tab_ref, u_ref, dy_ref, x_ref, edge_ref, b_ref, c_ref, du_ref, db_ref, dc_ref, da_ref, carry_ref, g_ref, lam_ref):
        @pl.when(pl.program_id(1) == 0)
        def _():
            carry_ref[...] = jnp.zeros_like(carry_ref)
            da_ref[...] = jnp.zeros_like(da_ref)
            db_ref[...] = jnp.zeros_like(db_ref)
            dc_ref[...] = jnp.zeros_like(dc_ref)

        dyb = dy_ref[...].astype(BF16)
        g_ref[...] = lax.dot_general(dyb, c_ref[0], _NT, preferred_element_type=F32)
        rows = lax.broadcasted_iota(jnp.int32, (8, wb), 0)

        def step(i, carry):
            prev, acc = carry
            k = ntile - 1 - i if adj_rev else i
            r0 = pl.multiple_of(k * 8, 8)
            lam = _s5_scan_tile(g_ref[pl.ds(r0, 8), :], tab_ref, prev, adj_rev)
            lam_ref[pl.ds(r0, 8), :] = lam
            x = x_ref[pl.ds(r0, 8), :]
            if reverse:
                kn = jnp.minimum(k + 1, ntile - 1)
                nb = x_ref[pl.ds(pl.multiple_of(kn * 8, 8), 8), :][0:1, :]
                nb = jnp.where(k == ntile - 1, edge_ref[0], nb)
                xp = jnp.where(rows == 7, jnp.broadcast_to(nb, (8, wb)), pltpu.roll(x, 7, 0))
            else:
                kn = jnp.maximum(k - 1, 0)
                nb = x_ref[pl.ds(pl.multiple_of(kn * 8, 8), 8), :][7:8, :]
                nb = jnp.where(k == 0, edge_ref[0], nb)
                xp = jnp.where(rows == 0, jnp.broadcast_to(nb, (8, wb)), pltpu.roll(x, 1, 0))
            xr, xi, lr, li = xp[:, :lb], xp[:, lb:], lam[:, :lb], lam[:, lb:]
            return lam, acc + jnp.concatenate([xr * lr + xi * li, xr * li - xi * lr], axis=1)

        last, acc = lax.fori_loop(0, ntile, step, (carry_ref[...], da_ref[...]))
        carry_ref[...] = last
        da_ref[...] = acc
        lamb = lam_ref[...].astype(BF16)
        du_ref[...] = lax.dot_general(lamb, b_ref[0], _NT, preferred_element_type=F32)
        db_ref[0] += lax.dot_general(u_ref[...].astype(BF16), lamb, _TN, preferred_element_type=F32)
        dc_ref[0] += lax.dot_general(dyb, x_ref[...].astype(BF16), _TN, preferred_element_type=F32)

    tix = (lambda t: ntb - 1 - t) if adj_rev else (lambda t: t)
    wide = pl.BlockSpec((tb, wb), lambda c, t: (tix(t), c))
    mat = pl.BlockSpec((1, S5_BLK_CH, wb), lambda c, t: (c, 0, 0))
    return pl.pallas_call(
        body, name=name, grid=(S5_NBLK, ntb),
        in_specs=[pl.BlockSpec((4, 8, wb), lambda c, t: (0, 0, c)),
                  pl.BlockSpec((tb, S5_BLK_CH), lambda c, t: (tix(t), S5_CB_U + c)),
                  pl.BlockSpec((tb, S5_BLK_CH), lambda c, t: (tix(t), c)), wide,
                  pl.BlockSpec((1, 1, wb), lambda c, t: (tix(t), 0, c)), mat,
                  pl.BlockSpec((1, wb, S5_BLK_CH), lambda c, t: (c, 0, 0))],
        out_specs=[pl.BlockSpec((tb, S5_BLK_CH), lambda c, t: (tix(t), c)), mat, mat, pl.BlockSpec((8, wb), lambda c, t: (0, c))],
        out_shape=[jax.ShapeDtypeStruct((length, S5_WIDTH), F32), jax.ShapeDtypeStruct((S5_NBLK, S5_BLK_CH, wb), F32),
                   jax.ShapeDtypeStruct((S5_NBLK, S5_BLK_CH, wb), F32), jax.ShapeDtypeStruct((8, S5_NBLK * wb), F32)],
        scratch_shapes=[pltpu.VMEM((8, wb), F32), pltpu.VMEM((tb, wb), F32), pltpu.VMEM((tb, wb), F32)],
        compiler_params=pltpu.CompilerParams(dimension_semantics=("parallel", "arbitrary")),
    )(tabs_conj, pg, dy, xs, edge, b_mat, c_mat)


def _s5_fwd(p_in, prm, w_glu):
    dirs = []
    ys = []
    for d, reverse in ((0, False), (1, True)):
        a_re, a_im, bb_re, bb_im = _s5_prep(prm['lambda_re'][d], prm['lambda_im'][d], prm['log_dt'][d],
                                            prm['b_re'][d], prm['b_im'][d])
        b_mat = _s5_in_matrix(bb_re, bb_im).astype(BF16)
        c_mat = _s5_out_matrix(prm['c_re'][d], prm['c_im'][d]).astype(BF16)
        xs, y_dir = _s5_direction_fwd(p_in, b_mat, c_mat, _s5_tables(a_re, a_im, reverse), reverse,
                                      name="s5_fwd_rev" if reverse else "s5_fwd")
        ys.append(y_dir)
        dirs.append((a_re, a_im, b_mat, c_mat, xs))

    def post(yf, yb, u, dskip):
        ypre = yf + yb + dskip * u
        th, _ = _gelu_parts(ypre)
        return ypre, 0.5 * ypre * (1.0 + th)
    ypre, yg = _rowmap(post, [ys[0], ys[1], (p_in, CB_U, S5_WIDTH)], [prm['d'].reshape(1, -1)],
                       [(S5_WIDTH, F32), (S5_WIDTH, F32)], tl=512, name="s5_post")
    t = _mm(yg, w_glu, name="s5_glu_mm")

    def glu(ygv, tv):
        return ygv * _sigmoid(tv)
    y = _rowmap(glu, [yg, t], [], [(S5_WIDTH, BF16)], tl=512, name="s5_glu")[0]
    return y, (dirs, ypre, yg, t)


def _s5_bwd(pg, prm, w_glu, saved, dy):
    dirs, ypre, yg, t = saved

    def glu_bwd(dyv, ygv, tv):
        s = _sigmoid(tv)
        return dyv * ygv * s * (1.0 - s), dyv * s
    dt, dyg_direct = _rowmap(glu_bwd, [dy, yg, t], [], [(S5_WIDTH, BF16), (S5_WIDTH, F32)], tl=512, name="s5_glu_bwd")
    grads = {'w_glu': _mm(yg, dt, ta=True, out_dtype=BF16, name="s5_dwglu")}
    dyg_mm = _mm(dt, w_glu, tb=True, name="s5_dyg")

    def post_bwd(dyd, dym, yp, u, dskip):
        th, dinner = _gelu_parts(yp)
        dyp = (dyd + dym) * (0.5 * (1.0 + th) + 0.5 * yp * (1.0 - th * th) * dinner)
        return dyp, dyp * dskip, jnp.sum(dyp * u, axis=0, keepdims=True)
    dyp, du_skip, dd = _rowmap(post_bwd, [dyg_direct, dyg_mm, ypre, (pg, CB_U, S5_WIDTH)], [prm['d'].reshape(1, -1)],
                               [(S5_WIDTH, F32), (S5_WIDTH, F32)], [(1, S5_WIDTH)], tl=512, name="s5_post_bwd")
    grads['d'] = dd[0]
    du = [du_skip]
    per_dir = []
    for d, reverse in ((0, False), (1, True)):
        a_re, a_im, b_mat, c_mat, xs = dirs[d]
        du_dir, d_bmat, d_cmat_t, da = _s5_direction_bwd(pg, dyp, xs, b_mat, c_mat, _s5_tables(a_re, -a_im, not reverse),
                                                         reverse, name="s5_bwd_rev" if reverse else "s5_bwd")
        du.append(du_dir)
        dbb_re, dbb_im = _s5_in_matrix_grad(d_bmat)
        dc_re, dc_im = _s5_out_matrix_grad(d_cmat_t)
        da_re, da_im = _s5_unblocked(jnp.sum(da, axis=0))
        _, vjp = jax.vjp(_s5_prep, prm['lambda_re'][d], prm['lambda_im'][d], prm['log_dt'][d], prm['b_re'][d], prm['b_im'][d])
        per_dir.append(vjp((da_re, da_im, dbb_re, dbb_im)) + (dc_re, dc_im))
    for i, key in enumerate(('lambda_re', 'lambda_im', 'log_dt', 'b_re', 'b_im', 'c_re', 'c_im')):
        grads[key] = jnp.stack([per_dir[0][i], per_dir[1][i]])
    return du, grads


def _split3(x):
    hi = x.astype(BF16)
    r = x - hi.astype(F32)
    mid = r.astype(BF16)
    return hi, mid, (r - mid.astype(F32)).astype(BF16)


def _exact_dot(ones, x, dims):
    parts = [lax.dot_general(ones, p, dims, preferred_element_type=F32) for p in _split3(x)]
    return parts[0] + parts[1] + parts[2]


_NN = (((1,), (0,)), ((), ()))
_NT = (((1,), (1,)), ((), ()))
_TN = (((0,), (0,)), ((), ()))


def _dot(a, b, dims=_NN):
    return lax.dot_general(a.astype(BF16), b.astype(BF16), dims, preferred_element_type=F32)


def _gla_chunk_mask(reverse):
    rows = lax.broadcasted_iota(jnp.int32, (GLA_CHUNK, GLA_CHUNK), 0)
    cols = lax.broadcasted_iota(jnp.int32, (GLA_CHUNK, GLA_CHUNK), 1)
    return (cols >= rows) if reverse else (cols <= rows)


def _gla_fwd(pg, la, reverse, *, name):
    length = la.shape[0]
    nch = length // GLA_CHUNK
    scale = GLA_HEAD_DIM ** -0.5
    last = 0 if reverse else GLA_CHUNK - 1
    hd = GLA_HEAD_DIM

    def body(q_ref, k_ref, v_ref, la_ref, o_ref, sp_ref, st_ref):
        @pl.when(pl.program_id(0) == 0)
        def _():
            st_ref[...] = jnp.zeros_like(st_ref)

        mask = _gla_chunk_mask(reverse)
        b = _exact_dot(mask.astype(BF16), la_ref[...], _NN)
        sp_ref[0] = st_ref[...]
        outs = []
        for h in range(GLA_HEADS):
            sl = slice(h * hd, (h + 1) * hd)
            bh = b[:, sl]
            bl = bh[last:last + 1, :]
            k = k_ref[:, sl]
            v = v_ref[:, sl]
            qd = q_ref[:, sl] * scale * jnp.exp(bh)
            kd = k * jnp.exp(-bh)
            ke = k * jnp.exp(bl - bh)
            st = st_ref[sl, :]
            p = jnp.where(mask, _dot(qd, kd, _NT), 0.0)
            outs.append(_dot(p, v) + _dot(qd, st, _NT))
            st_ref[sl, :] = st * jnp.exp(bl) + _dot(v, ke, _TN)
        o_ref[...] = jnp.concatenate(outs, axis=1)

    cmap = (lambda n: nch - 1 - n) if reverse else (lambda n: n)
    col = lambda cb: pl.BlockSpec((GLA_CHUNK, GLA_WIDTH), lambda n, cb=cb: (cmap(n), cb))
    return pl.pallas_call(
        body, name=name, grid=(nch,),
        in_specs=[col(CB_GQ), col(CB_GK), col(CB_GV), col(0)],
        out_specs=[col(0), pl.BlockSpec((1, GLA_WIDTH, hd), lambda n: (cmap(n), 0, 0))],
        out_shape=[jax.ShapeDtypeStruct((length, GLA_WIDTH), F32), jax.ShapeDtypeStruct((nch, GLA_WIDTH, hd), F32)],
        scratch_shapes=[pltpu.VMEM((GLA_WIDTH, hd), F32)],
        compiler_params=pltpu.CompilerParams(dimension_semantics=("arbitrary",)),
    )(pg, pg, pg, la)


def _gla_bwd(pg, la, do, sprev, reverse, *, name):
    length = la.shape[0]
    nch = length // GLA_CHUNK
    scale = GLA_HEAD_DIM ** -0.5
    last = 0 if reverse else GLA_CHUNK - 1
    hd = GLA_HEAD_DIM

    def body(q_ref, k_ref, v_ref, la_ref, do_ref, sp_ref, dq_ref, dk_ref, dv_ref, dla_ref, dst_ref):
        @pl.when(pl.program_id(0) == 0)
        def _():
            dst_ref[...] = jnp.zeros_like(dst_ref)

        mask = _gla_chunk_mask(reverse)
        tri = mask.astype(BF16)
        b = _exact_dot(tri, la_ref[...], _NN)
        is_last = lax.broadcasted_iota(jnp.int32, (GLA_CHUNK, hd), 0) == last
        dqs, dks, dvs, dbs = [], [], [], []
        for h in range(GLA_HEADS):
            sl = slice(h * hd, (h + 1) * hd)
            bh = b[:, sl]
            bl = bh[last:last + 1, :]
            eb, enb, ebl, el = jnp.exp(bh), jnp.exp(-bh), jnp.exp(bl - bh), jnp.exp(bl)
            k = k_ref[:, sl]
            v = v_ref[:, sl]
            dov = do_ref[:, sl]
            qd = q_ref[:, sl] * scale * eb
            kd = k * enb
            ke = k * ebl
            st = sp_ref[0, sl, :]
            dst = dst_ref[sl, :]
            p = jnp.where(mask, _dot(qd, kd, _NT), 0.0)
            dp = jnp.where(mask, _dot(dov, v, _NT), 0.0)
            dqd = _dot(dp, kd) + _dot(dov, st)
            dkd = _dot(dp, qd, _TN)
            dvs.append(_dot(p, dov, _TN) + _dot(ke, dst, _NT))
            dke = _dot(v, dst)
            dst_ref[sl, :] = dst * el + _dot(dov, qd, _TN)
            dbl = el * jnp.sum(dst * st, axis=0, keepdims=True) + jnp.sum(dke * ke, axis=0, keepdims=True)
            db = dqd * qd - dkd * kd - dke * ke
            dbs.append(jnp.where(is_last, db + dbl, db))
            dqs.append(dqd * eb * scale)
            dks.append(dkd * enb + dke * ebl)
        dq_ref[...] = jnp.concatenate(dqs, axis=1)
        dk_ref[...] = jnp.concatenate(dks, axis=1)
        dv_ref[...] = jnp.concatenate(dvs, axis=1)
        tri_t = _gla_chunk_mask(not reverse).astype(BF16)
        dla_ref[...] = _exact_dot(tri_t, jnp.concatenate(dbs, axis=1), _NN)

    cmap = (lambda n: n) if reverse else (lambda n: nch - 1 - n)
    col = lambda cb: pl.BlockSpec((GLA_CHUNK, GLA_WIDTH), lambda n, cb=cb: (cmap(n), cb))
    wide = jax.ShapeDtypeStruct((length, GLA_WIDTH), F32)
    return pl.pallas_call(
        body, name=name, grid=(nch,),
        in_specs=[col(CB_GQ), col(CB_GK), col(CB_GV), col(0), col(0),
                  pl.BlockSpec((1, GLA_WIDTH, hd), lambda n: (cmap(n), 0, 0))],
        out_specs=[col(0)] * 4, out_shape=[wide] * 4,
        scratch_shapes=[pltpu.VMEM((GLA_WIDTH, hd), F32)],
        compiler_params=pltpu.CompilerParams(dimension_semantics=("arbitrary",)),
    )(pg, pg, pg, la, do, sprev)


def _log_sigmoid(x):
    return jnp.minimum(x, 0.0) - jnp.log(1.0 + jnp.exp(-jnp.abs(x)))


def _gla_alpha_padded(w_alpha):
    w = jnp.zeros((2, 128, GLA_WIDTH), w_alpha.dtype)
    w = w.at[0, 0:GLA_LOWRANK].set(w_alpha[0])
    return w.at[1, GLA_LOWRANK:2 * GLA_LOWRANK].set(w_alpha[1])


def _gla_branch_fwd(pg, w_alpha, b_alpha, norm_gain):
    wa = _gla_alpha_padded(w_alpha).astype(BF16)

    def gates(z, w, bias):
        return (_log_sigmoid(_dot(z, w[0]) + bias[0:1]) / GLA_TAU, _log_sigmoid(_dot(z, w[1]) + bias[1:2]) / GLA_TAU)
    la_f, la_b = _rowmap(gates, [(pg, CB_Z, 128)], [wa, b_alpha], [(GLA_WIDTH, F32), (GLA_WIDTH, F32)], tl=512,
                         name="gla_gates")
    o_f, sp_f = _gla_fwd(pg, la_f, False, name="gla_fwd")
    o_b, sp_b = _gla_fwd(pg, la_b, True, name="gla_fwd_rev")

    def post(of, ob, gate, gn):
        o = of + ob
        on = jnp.concatenate([o[:, s:s + GLA_HEAD_DIM] * _rms(o[:, s:s + GLA_HEAD_DIM]) * gn
                              for s in range(0, GLA_WIDTH, GLA_HEAD_DIM)], axis=1)
        return o, on * (gate * _sigmoid(gate))
    o, y = _rowmap(post, [o_f, o_b, (pg, CB_GG, GLA_WIDTH)], [norm_gain.reshape(1, -1)],
                   [(GLA_WIDTH, F32), (GLA_WIDTH, BF16)], tl=512, name="gla_post")
    return y, (wa, la_f, la_b, sp_f, sp_b, o)


def _gla_branch_bwd(pg, w_alpha, b_alpha, norm_gain, saved, dy):
    wa, la_f, la_b, sp_f, sp_b, o = saved

    def post_bwd(dyv, ov, gate, gn):
        s = _sigmoid(gate)
        dos, dgn, ons = [], [], []
        for c in range(0, GLA_WIDTH, GLA_HEAD_DIM):
            oh = ov[:, c:c + GLA_HEAD_DIM]
            r = _rms(oh)
            don = dyv[:, c:c + GLA_HEAD_DIM] * (gate[:, c:c + GLA_HEAD_DIM] * s[:, c:c + GLA_HEAD_DIM])
            gd = don * gn
            dos.append(r * gd - oh * (r * r * r) * jnp.mean(oh * gd, axis=-1, keepdims=True))
            dgn.append(jnp.sum(don * oh * r, axis=0, keepdims=True))
            ons.append(oh * r * gn)
        on = jnp.concatenate(ons, axis=1)
        dgate = dyv * on * (s * (1.0 + gate * (1.0 - s)))
        return jnp.concatenate(dos, axis=1), dgate, jnp.concatenate(dgn, axis=1)
    do, dgate, dgn = _rowmap(post_bwd, [dy, o, (pg, CB_GG, GLA_WIDTH)], [norm_gain.reshape(1, -1)],
                             [(GLA_WIDTH, F32), (GLA_WIDTH, F32)], [(1, GLA_WIDTH)], tl=512, name="gla_post_bwd")
    dq_f, dk_f, dv_f, dla_f = _gla_bwd(pg, la_f, do, sp_f, False, name="gla_bwd")
    dq_b, dk_b, dv_b, dla_b = _gla_bwd(pg, la_b, do, sp_b, True, name="gla_bwd_rev")

    def gates_bwd(z, dlf, dlb, w, bias):
        dz = jnp.zeros_like(z)
        dlogits, dbs = [], []
        for d, dl in ((0, dlf), (1, dlb)):
            logit = _dot(z, w[d]) + bias[d:d + 1]
            dlogit = dl * (1.0 / GLA_TAU) * _sigmoid(-logit)
            dz = dz + _dot(dlogit, w[d], _NT)
            dlogits.append(dlogit)
            dbs.append(jnp.sum(dlogit, axis=0, keepdims=True))
        return dz, dlogits[0], dlogits[1], dbs[0], dbs[1]
    dz, dlg_f, dlg_b, dba_f, dba_b = _rowmap(
        gates_bwd, [(pg, CB_Z, 128), dla_f, dla_b], [wa, b_alpha], [(128, F32), (GLA_WIDTH, BF16), (GLA_WIDTH, BF16)],
        [(1, GLA_WIDTH), (1, GLA_WIDTH)], tl=512, name="gla_gates_bwd")
    dwa_f = _mm(dlg_f, (pg, CB_Z, 128), ta=True, name="gla_dwalpha")
    dwa_b = _mm(dlg_b, (pg, CB_Z, 128), ta=True, name="gla_dwalpha")
    grads = {'w_alpha': jnp.stack([dwa_f[:, 0:GLA_LOWRANK].T, dwa_b[:, GLA_LOWRANK:2 * GLA_LOWRANK].T]),
             'b_alpha': jnp.concatenate([dba_f, dba_b], axis=0),
             'norm': jnp.sum(dgn.reshape(GLA_HEADS, GLA_HEAD_DIM), axis=0)}
    return [dq_f, dq_b], [dk_f, dk_b], [dv_f, dv_b], dgate, dz, grads


def _rope_tables(length):
    half = ATTN_HEAD_DIM // 2
    inv_freq = ROPE_BASE ** (-jnp.arange(half // 2, dtype=F32) * 2.0 / half)
    t = jnp.arange(length, dtype=jnp.int32)
    def one(pos):
        ang = pos.astype(F32)[:, None] * inv_freq[None, :]
        c, s = jnp.cos(ang), jnp.sin(ang)
        return jnp.concatenate([c, c], axis=1), jnp.concatenate([-s, s], axis=1)
    c_r, s_r = one(t // GRID_W)
    c_c, s_c = one(t % GRID_W)
    return jnp.concatenate([c_r, c_c], axis=1), jnp.concatenate([s_r, s_c], axis=1)


def _rope_swap(y):
    w = y.shape[1]
    lane = lax.broadcasted_iota(jnp.int32, y.shape, 1)
    return jnp.where(lane % 32 < 16, pltpu.roll(y, w - 16, 1), pltpu.roll(y, 16, 1))


def _head_sums(x, ones):
    parts = [lax.dot_general(p, ones, _NN, preferred_element_type=F32) for p in _split3(x)]
    return parts[0] + parts[1] + parts[2]


def _head_ones(width):
    seg = np.arange(width) // ATTN_HEAD_DIM
    return jnp.asarray(seg[:, None] == seg[None, :], BF16)


def _qk_prep_fwd(pg, cb, width, gain, cos, sin, scale, *, name):
    heads = width // ATTN_HEAD_DIM
    def fn(x, c, s, g, ones):
        r = lax.rsqrt(_head_sums(x * x, ones) * (1.0 / ATTN_HEAD_DIM) + NORM_EPS)
        y = x * r * g
        return (y * c + _rope_swap(y) * s) * scale
    return _rowmap(fn, [(pg, cb, width), jnp.tile(cos, (1, heads)), jnp.tile(sin, (1, heads))],
                   [jnp.tile(gain, heads).reshape(1, -1), _head_ones(width)], [(width, BF16)], tl=512, name=name)[0]


def _qk_prep_bwd(pg, cb, width, gain, cos, sin, scale, dout, *, name):
    heads = width // ATTN_HEAD_DIM
    def fn(x, dov, c, s, g, ones):
        r = lax.rsqrt(_head_sums(x * x, ones) * (1.0 / ATTN_HEAD_DIM) + NORM_EPS)
        dos = dov * scale
        dy = dos * c + _rope_swap(dos * s)
        gd = dy * g
        dx = r * gd - x * (r * r * r) * (_head_sums(x * gd, ones) * (1.0 / ATTN_HEAD_DIM))
        return dx, jnp.sum(dy * x * r, axis=0, keepdims=True)
    dx, dg = _rowmap(fn, [(pg, cb, width), dout, jnp.tile(cos, (1, heads)), jnp.tile(sin, (1, heads))],
                     [jnp.tile(gain, heads).reshape(1, -1), _head_ones(width)], [(width, F32)], [(1, width)], tl=512,
                     name=name)
    return dx, jnp.sum(dg.reshape(heads, ATTN_HEAD_DIM), axis=0)


def _to_heads(x, heads):
    return jnp.transpose(x.reshape(x.shape[0], heads, ATTN_HEAD_DIM), (1, 0, 2))


def _from_heads(x):
    return jnp.transpose(x, (1, 0, 2)).reshape(x.shape[1], x.shape[0] * ATTN_HEAD_DIM)


ATTN_GROUP = ATTN_Q_HEADS // ATTN_KV_HEADS
ATTN_TQ = 256


def _attn_fwd(q, k, v):
    length = q.shape[1]
    tq = min(ATTN_TQ, length)

    def body(q_ref, k_ref, v_ref, o_ref):
        kk, vv = k_ref[0], v_ref[0]
        for g in range(ATTN_GROUP):
            s = _dot(q_ref[g], kk, _NT)
            p = jnp.exp(s - jnp.max(s, axis=-1, keepdims=True))
            o_ref[g] = _dot(p, vv) / jnp.sum(p, axis=-1, keepdims=True)

    kv_spec = pl.BlockSpec((1, length, ATTN_HEAD_DIM), lambda h, i: (h, 0, 0))
    q_spec = pl.BlockSpec((ATTN_GROUP, tq, ATTN_HEAD_DIM), lambda h, i: (h, i, 0))
    return pl.pallas_call(
        body, name="attn_fwd", grid=(ATTN_KV_HEADS, length // tq), in_specs=[q_spec, kv_spec, kv_spec],
        out_specs=q_spec, out_shape=jax.ShapeDtypeStruct(q.shape, F32),
        compiler_params=pltpu.CompilerParams(dimension_semantics=("parallel", "parallel")),
    )(q, k, v)


def _attn_bwd(q, k, v, o, do):
    length = q.shape[1]
    tq = min(ATTN_TQ, length)

    def body(q_ref, k_ref, v_ref, o_ref, do_ref, dq_ref, dk_ref, dv_ref):
        @pl.when(pl.program_id(1) == 0)
        def _():
            dk_ref[...] = jnp.zeros_like(dk_ref)
            dv_ref[...] = jnp.zeros_like(dv_ref)

        kk, vv = k_ref[0], v_ref[0]
        for g in range(ATTN_GROUP):
            qg, dog = q_ref[g], do_ref[g]
            s = _dot(qg, kk, _NT)
            p = jnp.exp(s - jnp.max(s, axis=-1, keepdims=True))
            p = p / jnp.sum(p, axis=-1, keepdims=True)
            dp = _dot(dog, vv, _NT)
            ds = p * (dp - jnp.sum(dog * o_ref[g], axis=-1, keepdims=True))
            dq_ref[g] = _dot(ds, kk)
            dk_ref[0] += _dot(ds, qg, _TN)
            dv_ref[0] += _dot(p, dog, _TN)

    kv_spec = pl.BlockSpec((1, length, ATTN_HEAD_DIM), lambda h, i: (h, 0, 0))
    q_spec = pl.BlockSpec((ATTN_GROUP, tq, ATTN_HEAD_DIM), lambda h, i: (h, i, 0))
    return pl.pallas_call(
        body, name="attn_bwd", grid=(ATTN_KV_HEADS, length // tq),
        in_specs=[q_spec, kv_spec, kv_spec, q_spec, q_spec], out_specs=[q_spec, kv_spec, kv_spec],
        out_shape=[jax.ShapeDtypeStruct(q.shape, F32), jax.ShapeDtypeStruct(k.shape, F32),
                   jax.ShapeDtypeStruct(k.shape, F32)],
        compiler_params=pltpu.CompilerParams(dimension_semantics=("parallel", "arbitrary")),
    )(q, k, v, o, do)


def _attn_branch_fwd(pg, q_gain, k_gain):
    cos, sin = _rope_tables(pg.shape[0])
    qp = _qk_prep_fwd(pg, CB_AQ, ATTN_WIDTH, q_gain, cos, sin, ATTN_HEAD_DIM ** -0.5, name="attn_q_prep")
    kp = _qk_prep_fwd(pg, CB_AK, ATTN_KV_WIDTH, k_gain, cos, sin, 1.0, name="attn_k_prep")
    qh, kh = _to_heads(qp, ATTN_Q_HEADS), _to_heads(kp, ATTN_KV_HEADS)
    vh = _to_heads(pg[:, P_OFF + 3200:P_OFF + 3328].astype(BF16), ATTN_KV_HEADS)
    oh = _attn_fwd(qh, kh, vh)
    return _from_heads(oh).astype(BF16), (cos, sin, qh, kh, vh, oh)


def _attn_branch_bwd(pg, q_gain, k_gain, saved, dy):
    cos, sin, qh, kh, vh, oh = saved
    dqh, dkh, dvh = _attn_bwd(qh, kh, vh, oh, _to_heads(dy, ATTN_Q_HEADS))
    dq, dqg = _qk_prep_bwd(pg, CB_AQ, ATTN_WIDTH, q_gain, cos, sin, ATTN_HEAD_DIM ** -0.5, _from_heads(dqh),
                           name="attn_q_prep_bwd")
    dk, dkg = _qk_prep_bwd(pg, CB_AK, ATTN_KV_WIDTH, k_gain, cos, sin, 1.0, _from_heads(dkh), name="attn_k_prep_bwd")
    return dq, dk, _from_heads(dvh), {'q_norm': dqg, 'k_norm': dkg}


def _gate_cols():
    return [slice(i * D_MODEL, (i + 1) * D_MODEL) for i in range(3)]


def _mixer_fwd(x, lw):
    h = _rmsnorm_fwd(x, lw['mix_norm'])
    pg = _mm(h, lw['w_pg'], name="mix_in")
    y_s5, s_s5 = _s5_fwd(pg, lw['s5'], lw['s5_w_glu'])
    y_gla, s_gla = _gla_branch_fwd(pg, lw['gla_w_alpha'], lw['gla_b_alpha'], lw['gla_norm'])
    y_att, s_att = _attn_branch_fwd(pg, lw['attn_q_norm'], lw['attn_k_norm'])
    ys = (y_s5, y_gla, y_att)
    br = [_mm(y, lw[n], name="mix_branch") for y, n in zip(ys, ('w_branch_s5', 'w_branch_gla', 'w_branch_attn'))]

    def merge(g0, g1, g2, b0, b1, b2, bias):
        acc = None
        for g, b, c in zip((g0, g1, g2), (b0, b1, b2), _gate_cols()):
            term = _sigmoid(g + bias[:, c]) * b
            acc = term if acc is None else acc + term
        return acc
    merged = _rowmap(merge, [(pg, 0, D_MODEL), (pg, 1, D_MODEL), (pg, 2, D_MODEL)] + br,
                     [lw['b_merge_gate'].reshape(1, -1)], [(D_MODEL, BF16)], tl=256, name="mix_merge")[0]
    x_out = _mm(merged, lw['w_out'], add=x, name="mix_out")
    return x_out, (x, h, pg, ys, (s_s5, s_gla, s_att), br, merged)


def _mixer_bwd(saved, lw, dx_out):
    x, h, pg, ys, (s_s5, s_gla, s_att), br, merged = saved
    grads = {'w_out': _mm(merged, dx_out, ta=True, out_dtype=BF16, name="mix_dwout")}
    dmerged = _mm(dx_out, lw['w_out'], tb=True, name="mix_dmerged")

    def merge_bwd(g0, g1, g2, b0, b1, b2, dm, bias):
        dbr, dgp = [], []
        for g, b, c in zip((g0, g1, g2), (b0, b1, b2), _gate_cols()):
            s = _sigmoid(g + bias[:, c])
            dbr.append(dm * s)
            dgp.append(dm * b * (s * (1.0 - s)))
        dgp = jnp.concatenate(dgp, axis=1)
        return dbr[0], dbr[1], dbr[2], dgp, jnp.sum(dgp, axis=0, keepdims=True)
    d0, d1, d2, dgpre, dbias = _rowmap(
        merge_bwd, [(pg, 0, D_MODEL), (pg, 1, D_MODEL), (pg, 2, D_MODEL)] + br + [dmerged],
        [lw['b_merge_gate'].reshape(1, -1)], [(D_MODEL, BF16)] * 3 + [(GATE_WIDTH, BF16)], [(1, GATE_WIDTH)], tl=256,
        name="mix_merge_bwd")
    grads['b_merge_gate'] = dbias[0]
    dys = []
    for y, dbr, n in zip(ys, (d0, d1, d2), ('w_branch_s5', 'w_branch_gla', 'w_branch_attn')):
        grads[n] = _mm(y, dbr, ta=True, out_dtype=BF16, name="mix_dwbranch")
        dys.append(_mm(dbr, lw[n], tb=True, name="mix_dy"))
    du, g_s5 = _s5_bwd(pg, lw['s5'], lw['s5_w_glu'], s_s5, dys[0])
    dgq, dgk, dgv, dgg, dz, g_gla = _gla_branch_bwd(pg, lw['gla_w_alpha'], lw['gla_b_alpha'], lw['gla_norm'], s_gla, dys[1])
    daq, dak, dav, g_att = _attn_branch_bwd(pg, lw['attn_q_norm'], lw['attn_k_norm'], s_att, dys[2])

    def assemble(dgp, u0, u1, u2, q0, q1, k0, k1, v0, v1, gg, aq, ak, av, z):
        pad = jnp.zeros((dgp.shape[0], IN_PAD - 3456), F32)
        parts = [dgp.astype(F32), u0 + u1 + u2, q0 + q1, k0 + k1, v0 + v1, gg, aq, ak, av, z, pad]
        return jnp.concatenate(parts, axis=1)
    dpg = _rowmap(assemble, [dgpre] + du + dgq + dgk + dgv + [dgg, daq, dak, dav, dz], [], [(PG_WIDTH, BF16)], tl=256,
                  name="mix_dpg")[0]
    grads['w_pg'] = _mm(h, dpg, ta=True, out_dtype=BF16, name="mix_dwpg")
    dh = _mm(dpg, lw['w_pg'], tb=True, name="mix_dh")
    dx, grads['mix_norm'] = _rmsnorm_bwd(x, lw['mix_norm'], dh, dx_out)
    grads['s5'], grads['gla'], grads['attn'] = g_s5, g_gla, g_att
    return dx, grads


def _loss_head(x, gain, target):
    width = x.shape[1]

    def fn(xv, tv, g):
        r = _rms(xv)
        err = xv * r * g - tv
        dy = err * (1.0 / width)
        gd = dy * g
        dx = r * gd - xv * (r * r * r) * jnp.mean(xv * gd, axis=-1, keepdims=True)
        loss = jnp.sum(0.5 * jnp.mean(err * err, axis=-1, keepdims=True), axis=0, keepdims=True)
        return dx, jnp.broadcast_to(loss, (1, 128)), jnp.sum(dy * xv * r, axis=0, keepdims=True)
    dx, loss, dgain = _rowmap(fn, [x, target], [gain.reshape(1, -1)], [(width, F32)], [(1, 128), (1, width)], tl=256,
                              name="loss_head")
    return loss[0, 0], dx, dgain[0]


def _row_tile(rows, cap=256):
    for t in range(cap - cap % 16, 0, -16):
        if rows % t == 0:
            return t
    return rows


def _reduce_adamw(parts, w, m, v, *, name):
    nparts, r, c = parts.shape
    tr = _row_tile(r)

    def body(p_ref, w_ref, m_ref, v_ref, g_ref, d_ref, m2_ref, v2_ref):
        g = p_ref[0].astype(F32)
        for j in range(1, nparts):
            g = g + p_ref[j].astype(F32)
        m2 = ADAM_B1 * m_ref[...] + (1.0 - ADAM_B1) * g
        v2 = ADAM_B2 * v_ref[...] + (1.0 - ADAM_B2) * (g * g)
        m_hat = m2 / (1.0 - ADAM_B1 ** ADAM_STEP)
        v_hat = v2 / (1.0 - ADAM_B2 ** ADAM_STEP)
        g_ref[...] = g
        d_ref[...] = -ADAM_LR * (m_hat / (jnp.sqrt(v_hat) + ADAM_EPS) + ADAM_WD * w_ref[...])
        m2_ref[...] = m2
        v2_ref[...] = v2

    flat = pl.BlockSpec((tr, c), lambda i: (i, 0))
    return pl.pallas_call(
        body, name=name, grid=(r // tr,), in_specs=[pl.BlockSpec((nparts, tr, c), lambda i: (0, i, 0)), flat, flat, flat],
        out_specs=[flat] * 4, out_shape=[jax.ShapeDtypeStruct((r, c), F32)] * 4,
        compiler_params=pltpu.CompilerParams(dimension_semantics=("parallel",)),
    )(parts, w, m, v)


def _all_gather(blocks, *, name):
    n = len(blocks)

    def body(*refs):
        x_refs, out_refs = refs[:n], refs[n:2 * n]
        send_sems, recv_sems, local_sems = refs[2 * n:]
        x, y, c = lax.axis_index("x"), lax.axis_index("y"), lax.axis_index("c")
        me, sibling = (x, y, c), (x, y, 1 - c)
        chips = [(1 - x, y), (x, 1 - y), (1 - x, 1 - y)]

        def slot(t, px, py, pc):
            return out_refs[t].at[4 * px + 2 * py + pc]

        def copy(t, k, blk, to, own=False):
            return pltpu.make_async_remote_copy(
                src_ref=x_refs[t] if own else slot(t, *blk), dst_ref=slot(t, *blk), send_sem=send_sems.at[t, k],
                recv_sem=recv_sems.at[t, k], device_id=to, device_id_type=pl.DeviceIdType.MESH)

        mine = [pltpu.make_async_copy(x_refs[t], slot(t, *me), local_sems.at[t]) for t in range(n)]
        for cp in mine:
            cp.start()
        first = []
        for t in range(n):
            first.append(copy(t, 0, me, sibling, own=True))
            first += [copy(t, 1 + j, me, (*chip, c), own=True) for j, chip in enumerate(chips)]
        for cp in first:
            cp.start()
        passed = []
        for j, chip in enumerate(chips):
            for t in range(n):
                copy(t, 1 + j, (*chip, c), me).wait_recv()
                passed.append(copy(t, 4 + j, (*chip, c), sibling))
                passed[-1].start()
        for t in range(n):
            copy(t, 0, sibling, me).wait_recv()
        for j, chip in enumerate(chips):
            for t in range(n):
                copy(t, 4 + j, (*chip, 1 - c), me).wait_recv()
        for cp in first + passed:
            cp.wait_send()
        for cp in mine:
            cp.wait()

    hbm = pl.BlockSpec(memory_space=pl.ANY)
    return pl.pallas_call(
        body, name=name, out_shape=[jax.ShapeDtypeStruct((N_DEV,) + b.shape, b.dtype) for b in blocks],
        in_specs=[hbm] * n, out_specs=[hbm] * n,
        scratch_shapes=[pltpu.SemaphoreType.DMA((n, 7)), pltpu.SemaphoreType.DMA((n, 7)), pltpu.SemaphoreType.DMA((n,))],
    )(*blocks)


N_CHIP = N_DEV // 2


def _swap_with_sibling(arrays, *, name):
    n = len(arrays)

    def body(*refs):
        src_refs, out_refs = refs[:n], refs[n:2 * n]
        send_sems, recv_sems = refs[2 * n:]
        sibling = (lax.axis_index("x"), lax.axis_index("y"), 1 - lax.axis_index("c"))
        copies = [pltpu.make_async_remote_copy(
            src_ref=src_refs[t], dst_ref=out_refs[t], send_sem=send_sems.at[t], recv_sem=recv_sems.at[t],
            device_id=sibling, device_id_type=pl.DeviceIdType.MESH) for t in range(n)]
        for cp in copies:
            cp.start()
        for cp in copies:
            cp.wait()

    hbm = pl.BlockSpec(memory_space=pl.ANY)
    return pl.pallas_call(
        body, name=name, out_shape=[jax.ShapeDtypeStruct(a.shape, a.dtype) for a in arrays],
        in_specs=[hbm] * n, out_specs=[hbm] * n,
        scratch_shapes=[pltpu.SemaphoreType.DMA((n,)), pltpu.SemaphoreType.DMA((n,))],
    )(*arrays)


def _exchange_chips(stacks, *, name):
    n = len(stacks)

    def body(*refs):
        g_refs, out_refs = refs[:n], refs[n:2 * n]
        send_sems, recv_sems, local_sems = refs[2 * n:]
        x, y, c = lax.axis_index("x"), lax.axis_index("y"), lax.axis_index("c")
        me = 2 * x + y
        mine = [pltpu.make_async_copy(g_refs[t].at[me], out_refs[t].at[me], local_sems.at[t]) for t in range(n)]
        for cp in mine:
            cp.start()
        copies = []
        for k in range(1, N_CHIP):
            px, py = x ^ (k >> 1 & 1), y ^ (k & 1)
            for t in range(n):
                copies.append(pltpu.make_async_remote_copy(
                    src_ref=g_refs[t].at[2 * px + py], dst_ref=out_refs[t].at[me], send_sem=send_sems.at[t, k - 1],
                    recv_sem=recv_sems.at[t, k - 1], device_id=(px, py, c), device_id_type=pl.DeviceIdType.MESH))
        for cp in copies:
            cp.start()
        for cp in copies:
            cp.wait_recv()
        for cp in copies:
            cp.wait_send()
        for cp in mine:
            cp.wait()

    hbm = pl.BlockSpec(memory_space=pl.ANY)
    return pl.pallas_call(
        body, name=name, out_shape=[jax.ShapeDtypeStruct(s.shape, s.dtype) for s in stacks],
        in_specs=[hbm] * n, out_specs=[hbm] * n,
        scratch_shapes=[pltpu.SemaphoreType.DMA((n, N_CHIP - 1)), pltpu.SemaphoreType.DMA((n, N_CHIP - 1)),
                        pltpu.SemaphoreType.DMA((n,))],
    )(*stacks)


def _pair_sum(a, b):
    return _rowmap(lambda u, v: u.astype(F32) + v.astype(F32), [a, b], [], [(a.shape[1], BF16)], tl=_row_tile(a.shape[0], 512),
                   name="pair_sum")[0]


SMALL_COLS = 128


def _pack_small(arrays):
    flat = jnp.concatenate([a.astype(F32).reshape(-1, SMALL_COLS) for a in arrays], axis=0)
    return jnp.pad(flat, ((0, -flat.shape[0] % 256), (0, 0)))


def _unpack_small(packed, shapes):
    out, off = [], 0
    for s in shapes:
        r = math.prod(s) // SMALL_COLS
        out.append(packed[off:off + r].reshape(s))
        off += r
    return out


def _split_shards(full, axis):
    shape = full.shape
    split = full.reshape(shape[:axis] + (N_DEV, shape[axis] // N_DEV) + shape[axis + 1:])
    return jnp.moveaxis(split, axis, 0)


def _join_shards(stack, axis):
    moved = jnp.moveaxis(stack, 0, axis)
    shape = moved.shape
    return moved.reshape(shape[:axis] + (shape[axis] * shape[axis + 1],) + shape[axis + 2:])


def _w_in_padded(w_in):
    pad = jnp.zeros(w_in.shape[:-1] + (IN_PAD - IN_WIDTH,), w_in.dtype)
    return jnp.concatenate([w_in[..., :2560], w_in[..., 2592:], w_in[..., 2560:2592], pad], axis=-1)


def _w_in_unpadded(w):
    return jnp.concatenate([w[..., :2560], w[..., 3328:3360], w[..., 2560:3328]], axis=-1)


S5_KEYS = ('lambda_re', 'lambda_im', 'log_dt', 'b_re', 'b_im', 'c_re', 'c_im', 'd')


def _layer_weights(full, w, i):
    lw = {n: w[n][i] for n in ('ffn1_norm', 'mix_norm', 'gla_norm', 'attn_q_norm', 'attn_k_norm', 'b_merge_gate', 'ffn2_norm')}
    lw['s5'] = {k: w['s5_' + k][i] for k in S5_KEYS}
    for f in ('ffn1', 'ffn2'):
        for part in ('_w_gate', '_w_up', '_w_down'):
            lw[f + part] = full[f + part][i]
    lw['w_pg'] = jnp.concatenate([full['w_merge_gate'][i], _w_in_padded(full['w_in'][i])], axis=1)
    for n in ('s5_w_glu', 'gla_w_alpha', 'w_branch_s5', 'w_branch_gla', 'w_branch_attn', 'w_out'):
        lw[n] = full[n][i]
    lw['gla_b_alpha'] = full['gla_b_alpha'][i].astype(F32)
    return lw


def _step_local(x, target, w, full):
    lws = [_layer_weights(full, w, i) for i in range(DEPTH)]
    saved = []
    for lw in lws:
        x, s1 = _ffn_fwd(x, lw['ffn1_norm'], lw['ffn1_w_gate'], lw['ffn1_w_up'], lw['ffn1_w_down'])
        x, s2 = _mixer_fwd(x, lw)
        x, s3 = _ffn_fwd(x, lw['ffn2_norm'], lw['ffn2_w_gate'], lw['ffn2_w_up'], lw['ffn2_w_down'])
        saved.append((s1, s2, s3))
    loss, dx, d_final = _loss_head(x, w['final_norm'], target)
    per_layer = []
    for lw, (s1, s2, s3) in reversed(list(zip(lws, saved))):
        g = {}
        dx, g['ffn2_norm'], g['ffn2_w_gate'], g['ffn2_w_up'], g['ffn2_w_down'] = _ffn_bwd(
            s3, lw['ffn2_norm'], lw['ffn2_w_gate'], lw['ffn2_w_up'], lw['ffn2_w_down'], dx)
        dx, gm = _mixer_bwd(s2, lw, dx)
        dx, g['ffn1_norm'], g['ffn1_w_gate'], g['ffn1_w_up'], g['ffn1_w_down'] = _ffn_bwd(
            s1, lw['ffn1_norm'], lw['ffn1_w_gate'], lw['ffn1_w_up'], lw['ffn1_w_down'], dx)
        g['w_merge_gate'] = gm['w_pg'][:, :GATE_WIDTH]
        g['w_in'] = _w_in_unpadded(gm['w_pg'][:, GATE_WIDTH:])
        for n in ('w_out', 'b_merge_gate', 'w_branch_s5', 'w_branch_gla', 'w_branch_attn', 'mix_norm'):
            g[n] = gm[n]
        for k in S5_KEYS:
            g['s5_' + k] = gm['s5'][k]
        g['s5_w_glu'] = gm['s5']['w_glu']
        g['gla_w_alpha'], g['gla_b_alpha'], g['gla_norm'] = gm['gla']['w_alpha'], gm['gla']['b_alpha'], gm['gla']['norm']
        g['attn_q_norm'], g['attn_k_norm'] = gm['attn']['q_norm'], gm['attn']['k_norm']
        per_layer.append(g)
    per_layer.reverse()
    return loss, dx, per_layer, d_final


def kernel(x, ffn1_norm, ffn1_w_gate, ffn1_w_up, ffn1_w_down, mix_norm, w_in, s5_lambda_re, s5_lambda_im, s5_log_dt, s5_b_re, s5_b_im, s5_c_re, s5_c_im, s5_d, s5_w_glu, gla_w_alpha, gla_b_alpha, gla_norm, attn_q_norm, attn_k_norm, w_branch_s5, w_branch_gla, w_branch_attn, w_merge_gate, b_merge_gate, w_out, ffn2_norm, ffn2_w_gate, ffn2_w_up, ffn2_w_down, final_norm, loss_target, m_ffn1_norm, m_ffn1_w_gate, m_ffn1_w_up, m_ffn1_w_down, m_mix_norm, m_w_in, m_s5_lambda_re, m_s5_lambda_im, m_s5_log_dt, m_s5_b_re, m_s5_b_im, m_s5_c_re, m_s5_c_im, m_s5_d, m_s5_w_glu, m_gla_w_alpha, m_gla_b_alpha, m_gla_norm, m_attn_q_norm, m_attn_k_norm, m_w_branch_s5, m_w_branch_gla, m_w_branch_attn, m_w_merge_gate, m_b_merge_gate, m_w_out, m_ffn2_norm, m_ffn2_w_gate, m_ffn2_w_up, m_ffn2_w_down, m_final_norm, v_ffn1_norm, v_ffn1_w_gate, v_ffn1_w_up, v_ffn1_w_down, v_mix_norm, v_w_in, v_s5_lambda_re, v_s5_lambda_im, v_s5_log_dt, v_s5_b_re, v_s5_b_im, v_s5_c_re, v_s5_c_im, v_s5_d, v_s5_w_glu, v_gla_w_alpha, v_gla_b_alpha, v_gla_norm, v_attn_q_norm, v_attn_k_norm, v_w_branch_s5, v_w_branch_gla, v_w_branch_attn, v_w_merge_gate, v_b_merge_gate, v_w_out, v_ffn2_norm, v_ffn2_w_gate, v_ffn2_w_up, v_ffn2_w_down, v_final_norm):
    return _train_step(x, ffn1_norm, ffn1_w_gate, ffn1_w_up, ffn1_w_down, mix_norm, w_in, s5_lambda_re, s5_lambda_im, s5_log_dt, s5_b_re, s5_b_im, s5_c_re, s5_c_im, s5_d, s5_w_glu, gla_w_alpha, gla_b_alpha, gla_norm, attn_q_norm, attn_k_norm, w_branch_s5, w_branch_gla, w_branch_attn, w_merge_gate, b_merge_gate, w_out, ffn2_norm, ffn2_w_gate, ffn2_w_up, ffn2_w_down, final_norm, loss_target, m_ffn1_norm, m_ffn1_w_gate, m_ffn1_w_up, m_ffn1_w_down, m_mix_norm, m_w_in, m_s5_lambda_re, m_s5_lambda_im, m_s5_log_dt, m_s5_b_re, m_s5_b_im, m_s5_c_re, m_s5_c_im, m_s5_d, m_s5_w_glu, m_gla_w_alpha, m_gla_b_alpha, m_gla_norm, m_attn_q_norm, m_attn_k_norm, m_w_branch_s5, m_w_branch_gla, m_w_branch_attn, m_w_merge_gate, m_b_merge_gate, m_w_out, m_ffn2_norm, m_ffn2_w_gate, m_ffn2_w_up, m_ffn2_w_down, m_final_norm, v_ffn1_norm, v_ffn1_w_gate, v_ffn1_w_up, v_ffn1_w_down, v_mix_norm, v_w_in, v_s5_lambda_re, v_s5_lambda_im, v_s5_log_dt, v_s5_b_re, v_s5_b_im, v_s5_c_re, v_s5_c_im, v_s5_d, v_s5_w_glu, v_gla_w_alpha, v_gla_b_alpha, v_gla_norm, v_attn_q_norm, v_attn_k_norm, v_w_branch_s5, v_w_branch_gla, v_w_branch_attn, v_w_merge_gate, v_b_merge_gate, v_w_out, v_ffn2_norm, v_ffn2_w_gate, v_ffn2_w_up, v_ffn2_w_down, v_final_norm)


def _train_step(*args):
    nw = len(W_NAMES)
    x, target = args[0][0], args[1 + nw][0]
    w = dict(zip(W_NAMES, args[1:1 + nw]))
    m = dict(zip(W_NAMES, args[2 + nw:2 + 2 * nw]))
    v = dict(zip(W_NAMES, args[2 + 2 * nw:2 + 3 * nw]))

    gathered = _all_gather([w[n].astype(BF16) for n in SHARDED], name="gather_weights")
    full = {n: _join_shards(g, SHARD_AXIS[n]) for n, g in zip(SHARDED, gathered)}

    loss, dx, per_layer, d_final = _step_local(x, target, w, full)
    loss = lax.psum(loss, ("x", "y", "c"))

    out = {}
    kinds = ('grad', 'delta', 'new_m', 'new_v')
    core = lax.axis_index("c")
    own, for_sibling = [], []
    for n in SHARDED:
        by_owner = jnp.stack([_split_shards(g[n], SHARD_AXIS[n] - 1) for g in per_layer], axis=1).astype(BF16)
        by_owner = by_owner.reshape((N_CHIP, 2) + by_owner.shape[1:])
        own.append(lax.dynamic_index_in_dim(by_owner, core, axis=1, keepdims=False))
        for_sibling.append(lax.dynamic_index_in_dim(by_owner, 1 - core, axis=1, keepdims=False))
    from_sibling = _swap_with_sibling(for_sibling, name="exchange_grads_sibling")
    chip_sums = [_pair_sum(a.reshape(-1, a.shape[-1]), b.reshape(-1, b.shape[-1])).reshape(a.shape)
                 for a, b in zip(own, from_sibling)]
    incoming = _exchange_chips(chip_sums, name="exchange_grads_chips")
    for n, parts in zip(SHARDED, incoming):
        shape = w[n].shape
        flat = lambda a: a.reshape(-1, shape[-1])
        res = _reduce_adamw(parts.reshape(N_CHIP, -1, shape[-1]), flat(w[n]), flat(m[n]), flat(v[n]), name="adamw_sharded")
        for kind, a in zip(kinds, res):
            out[kind + '_' + n] = a.reshape(shape)
    small = [jnp.stack([g[n] for g in per_layer]) if n != 'final_norm' else d_final for n in REPLICATED]
    parts = _all_gather([_pack_small(small)], name="gather_small_grads")[0]
    res = _reduce_adamw(parts, *[_pack_small([d[n] for n in REPLICATED]) for d in (w, m, v)], name="adamw_replicated")
    for kind, packed in zip(kinds, res):
        for n, a in zip(REPLICATED, _unpack_small(packed, [w[n].shape for n in REPLICATED])):
            out[kind + '_' + n] = a
    return (loss, dx[None]) + tuple(out[kind + '_' + n] for kind in kinds for n in W_NAMES)
```

```python
import functools
import math

import jax
import jax.numpy as jnp
import numpy as np
from jax import lax
from jax.experimental import pallas as pl
from jax.experimental.pallas import tpu as pltpu

F32 = jnp.float32
BF16 = jnp.bfloat16

N_DEV = 8
D_MODEL = 1024
DEPTH = 2
GRID_W = 64
D_FF = 2816
NORM_EPS = 1e-6
S5_GROUPS = 32
S5_GROUP_CH = 16
S5_STATE = 64
S5_WIDTH = 512
S5_NSTATE = S5_GROUPS * S5_STATE
S5_LANE_BLOCK = 512
GLA_HEADS = 4
GLA_HEAD_DIM = 128
GLA_WIDTH = 512
GLA_LOWRANK = 16
GLA_TAU = 16.0
GLA_CHUNK = 64
ATTN_Q_HEADS = 8
ATTN_KV_HEADS = 2
ATTN_HEAD_DIM = 64
ATTN_WIDTH = 512
ATTN_KV_WIDTH = 128
ROPE_BASE = 10000.0
IN_SPLITS = (512, 512, 512, 512, 512, 16, 16, 512, 128, 128)
IN_WIDTH = sum(IN_SPLITS)
IN_PAD = 3584
GATE_WIDTH = 3 * D_MODEL
PG_WIDTH = GATE_WIDTH + IN_PAD
P_OFF = GATE_WIDTH
CB_U, CB_GQ, CB_GK, CB_GV, CB_GG, CB_AQ = (P_OFF // 512 + i for i in range(6))
CB_AK, CB_AV, CB_Z = (P_OFF + 3072) // 128, (P_OFF + 3200) // 128, (P_OFF + 3328) // 128
ADAM_LR = 0.001
ADAM_B1 = 0.9
ADAM_B2 = 0.999
ADAM_EPS = 1e-08
ADAM_WD = 0.01
ADAM_STEP = 10
PACK_COLS = 1024

W_NAMES = ['ffn1_norm', 'ffn1_w_gate', 'ffn1_w_up', 'ffn1_w_down', 'mix_norm', 'w_in', 's5_lambda_re', 's5_lambda_im',
           's5_log_dt', 's5_b_re', 's5_b_im', 's5_c_re', 's5_c_im', 's5_d', 's5_w_glu', 'gla_w_alpha', 'gla_b_alpha',
           'gla_norm', 'attn_q_norm', 'attn_k_norm', 'w_branch_s5', 'w_branch_gla', 'w_branch_attn', 'w_merge_gate',
           'b_merge_gate', 'w_out', 'ffn2_norm', 'ffn2_w_gate', 'ffn2_w_up', 'ffn2_w_down', 'final_norm']
SHARD_AXIS = {'ffn1_w_gate': 2, 'ffn1_w_up': 2, 'ffn1_w_down': 1, 'w_in': 2, 's5_w_glu': 1, 'gla_w_alpha': 3,
              'gla_b_alpha': 2, 'w_branch_s5': 2, 'w_branch_gla': 2, 'w_branch_attn': 2, 'w_merge_gate': 2,
              'w_out': 1, 'ffn2_w_gate': 2, 'ffn2_w_up': 2, 'ffn2_w_down': 1}
SHARDED = [n for n in W_NAMES if n in SHARD_AXIS]
REPLICATED = [n for n in W_NAMES if n not in SHARD_AXIS]


def _pick(dim, prefs):
    for p in prefs:
        if dim % p == 0:
            return p
    return dim


def _sigmoid(x):
    return 0.5 * jnp.tanh(0.5 * x) + 0.5


def _mm(a, b, *, ta=False, tb=False, out_dtype=F32, scale=None, add=None, name):
    a, a_cb, a_w = a if isinstance(a, tuple) else (a, 0, a.shape[1])
    b, b_cb, b_w = b if isinstance(b, tuple) else (b, 0, b.shape[1])
    m, k = (a_w, a.shape[0]) if ta else (a.shape[0], a_w)
    n = b.shape[0] if tb else b_w
    assert (b_w if tb else b.shape[0]) == k, (a.shape, b.shape, ta, tb)
    tm, tn, tk = _mm_tiles(m, n, k, a.dtype.itemsize, b.dtype.itemsize, jnp.dtype(out_dtype).itemsize)
    nk = k // tk
    dims = (((0 if ta else 1,), (1 if tb else 0,)), ((), ()))
    a_off = a_cb * (a_w // (tm if ta else tk))
    b_off = b_cb * (b_w // (tk if tb else tn))

    def body(a_ref, b_ref, *rest):
        add_ref = rest[0] if add is not None else None
        o_ref, *acc = rest[1:] if add is not None else rest

        def finish(res):
            res = res if scale is None else res * scale
            return (res if add_ref is None else res + add_ref[...]).astype(out_dtype)

        part = lax.dot_general(a_ref[...].astype(BF16), b_ref[...].astype(BF16), dims, preferred_element_type=F32)
        if nk == 1:
            o_ref[...] = finish(part)
            return
        acc_ref, = acc
        kk = pl.program_id(2)

        @pl.when(kk == 0)
        def _():
            acc_ref[...] = part

        @pl.when(kk > 0)
        def _():
            acc_ref[...] += part

        @pl.when(kk == nk - 1)
        def _():
            o_ref[...] = finish(acc_ref[...])

    a_spec = (pl.BlockSpec((tk, tm), lambda i, j, kk: (kk, i + a_off)) if ta
              else pl.BlockSpec((tm, tk), lambda i, j, kk: (i, kk + a_off)))
    b_spec = (pl.BlockSpec((tn, tk), lambda i, j, kk: (j, kk + b_off)) if tb
              else pl.BlockSpec((tk, tn), lambda i, j, kk: (kk, j + b_off)))
    o_spec = pl.BlockSpec((tm, tn), lambda i, j, kk: (i, j))
    return pl.pallas_call(
        body, name=name, grid=(m // tm, n // tn, nk), in_specs=[a_spec, b_spec] + ([o_spec] if add is not None else []),
        out_specs=o_spec, out_shape=jax.ShapeDtypeStruct((m, n), out_dtype),
        scratch_shapes=[pltpu.VMEM((tm, tn), F32)] if nk > 1 else [],
        compiler_params=pltpu.CompilerParams(dimension_semantics=("parallel", "parallel", "arbitrary")),
    )(a, b, *([add] if add is not None else []))


MM_VMEM_BUDGET = 40 * 1024 * 1024


def _mm_tiles(m, n, k, a_bytes, b_bytes, out_bytes):
    tms = [t for t in (1024, 1408, 512, 256, 128) if m % t == 0] or [m]
    tns = [t for t in (512, 1408, 256, 128) if n % t == 0] or [n]
    tks = [k] + [t for t in (2048, 1024, 512, 256, 128) if k % t == 0 and t < k]
    for tk in tks:
        for tm in tms:
            for tn in tns:
                use = 2 * (tm * tk * a_bytes + tk * tn * b_bytes + tm * tn * out_bytes) + 2 * tm * tn * 4
                if use <= MM_VMEM_BUDGET:
                    return tm, tn, tk
    return tms[-1], tns[-1], tks[-1]


def _rowmap(fn, rows, consts, outs, reds=(), *, tl, name):
    rows = [r if isinstance(r, tuple) else (r, 0, r.shape[1]) for r in rows]
    length = rows[0][0].shape[0]
    tl = min(tl, length)
    nr, nc, no = len(rows), len(consts), len(outs)

    def body(*refs):
        res = fn(*[r[...] for r in refs[:nr + nc]])
        res = res if isinstance(res, tuple) else (res,)
        for o_ref, val in zip(refs[nr + nc:nr + nc + no], res[:no]):
            o_ref[...] = val.astype(o_ref.dtype)
        if reds:
            step = pl.program_id(0)
            red_refs = refs[nr + nc + no:]

            @pl.when(step == 0)
            def _():
                for d_ref, val in zip(red_refs, res[no:]):
                    d_ref[...] = val.astype(F32)

            @pl.when(step > 0)
            def _():
                for d_ref, val in zip(red_refs, res[no:]):
                    d_ref[...] += val.astype(F32)

    in_specs = [pl.BlockSpec((tl, w), lambda i, cb=cb: (i, cb)) for (_, cb, w) in rows]
    in_specs += [pl.BlockSpec(c.shape, lambda i, nd=c.ndim: (0,) * nd) for c in consts]
    out_specs = [pl.BlockSpec((tl, w), lambda i: (i, 0)) for (w, _) in outs]
    out_specs += [pl.BlockSpec(s, lambda i, nd=len(s): (0,) * nd) for s in reds]
    out_shape = [jax.ShapeDtypeStruct((length, w), dt) for (w, dt) in outs]
    out_shape += [jax.ShapeDtypeStruct(s, F32) for s in reds]
    res = pl.pallas_call(
        body, name=name, grid=(length // tl,), in_specs=in_specs, out_specs=out_specs, out_shape=out_shape,
        compiler_params=pltpu.CompilerParams(dimension_semantics=("arbitrary" if reds else "parallel",)),
    )(*[r[0] for r in rows], *consts)
    return res


def _rms(x):
    return lax.rsqrt(jnp.mean(x * x, axis=-1, keepdims=True) + NORM_EPS)


def _rmsnorm_fwd(x, gain):
    def fn(xv, g):
        return xv * _rms(xv) * g
    return _rowmap(fn, [x], [gain.reshape(1, -1)], [(x.shape[1], BF16)], tl=256, name="rmsnorm_fwd")[0]


def _rmsnorm_bwd(x, gain, dh, dres):
    def fn(xv, dhv, drv, g):
        r = _rms(xv)
        gd = dhv * g
        dx = r * gd - xv * (r * r * r) * jnp.mean(xv * gd, axis=-1, keepdims=True)
        return drv + dx, jnp.sum(dhv * xv * r, axis=0, keepdims=True)
    dx, dg = _rowmap(fn, [x, dh, dres], [gain.reshape(1, -1)], [(x.shape[1], F32)], [(1, x.shape[1])], tl=256,
                     name="rmsnorm_bwd")
    return dx, dg[0]


FFN_UNIT = D_FF // 2


def _ffn_up(h, w_gate, w_up):
    length, k = h.shape
    tm = _pick(length, (512, 256, 128))

    def body(h_ref, wg_ref, wu_ref, a_ref, g_ref, u_ref):
        hv = h_ref[...]
        g = jnp.dot(hv, wg_ref[...], preferred_element_type=F32)
        u = jnp.dot(hv, wu_ref[...], preferred_element_type=F32)
        a_ref[...] = (g * _sigmoid(g) * u).astype(BF16)
        g_ref[...] = g.astype(BF16)
        u_ref[...] = u.astype(BF16)

    w_spec = pl.BlockSpec((k, FFN_UNIT), lambda j, i: (0, j))
    o_spec = pl.BlockSpec((tm, FFN_UNIT), lambda j, i: (i, j))
    return pl.pallas_call(
        body, name="ffn_up", grid=(D_FF // FFN_UNIT, length // tm),
        in_specs=[pl.BlockSpec((tm, k), lambda j, i: (i, 0)), w_spec, w_spec], out_specs=[o_spec] * 3,
        out_shape=[jax.ShapeDtypeStruct((length, D_FF), BF16)] * 3,
        compiler_params=pltpu.CompilerParams(dimension_semantics=("parallel", "parallel")),
    )(h, w_gate, w_up)


def _ffn_dgu(dxo, w_down, g, u):
    length, k = dxo.shape
    tm = _pick(length, (512, 256, 128))

    def body(d_ref, w_ref, g_ref, u_ref, dg_ref, du_ref):
        da = 0.5 * lax.dot_general(d_ref[...], w_ref[...], _NT, preferred_element_type=F32)
        gv = g_ref[...].astype(F32)
        s = _sigmoid(gv)
        dg_ref[...] = (da * u_ref[...].astype(F32) * (s * (1.0 + gv * (1.0 - s)))).astype(BF16)
        du_ref[...] = (da * (gv * s)).astype(BF16)

    o_spec = pl.BlockSpec((tm, FFN_UNIT), lambda j, i: (i, j))
    return pl.pallas_call(
        body, name="ffn_dgu", grid=(D_FF // FFN_UNIT, length // tm),
        in_specs=[pl.BlockSpec((tm, k), lambda j, i: (i, 0)), pl.BlockSpec((FFN_UNIT, k), lambda j, i: (j, 0)), o_spec, o_spec],
        out_specs=[o_spec] * 2, out_shape=[jax.ShapeDtypeStruct((length, D_FF), BF16)] * 2,
        compiler_params=pltpu.CompilerParams(dimension_semantics=("parallel", "parallel")),
    )(dxo, w_down, g, u)


def _ffn_fwd(x, gain, w_gate, w_up, w_down):
    h = _rmsnorm_fwd(x, gain)
    a, g, u = _ffn_up(h, w_gate, w_up)
    x_out = _mm(a, w_down, scale=0.5, add=x, name="ffn_down")
    return x_out, (x, h, g, u, a)


def _ffn_bwd(saved, gain, w_gate, w_up, w_down, dx_out):
    x, h, g, u, a = saved
    dxo = dx_out.astype(BF16)
    d_wdown = _mm(a, dxo, ta=True, scale=0.5, out_dtype=BF16, name="ffn_dwdown")
    dg, du = _ffn_dgu(dxo, w_down, g, u)
    d_wgate = _mm(h, dg, ta=True, out_dtype=BF16, name="ffn_dwgu")
    d_wup = _mm(h, du, ta=True, out_dtype=BF16, name="ffn_dwgu")
    dh = _mm(du, w_up, tb=True, add=_mm(dg, w_gate, tb=True, name="ffn_dh"), name="ffn_dh_add")
    dx, dgain = _rmsnorm_bwd(x, gain, dh, dx_out)
    return dx, dgain, d_wgate, d_wup, d_wdown


def _s5_col(n):
    return (n // S5_LANE_BLOCK) * 2 * S5_LANE_BLOCK + n % S5_LANE_BLOCK


def _s5_blocked(re, im):
    lead = re.shape[:-1]
    nb = S5_NSTATE // S5_LANE_BLOCK
    both = jnp.stack([re.reshape(*lead, nb, S5_LANE_BLOCK), im.reshape(*lead, nb, S5_LANE_BLOCK)], axis=-2)
    return both.reshape(*lead, 2 * S5_NSTATE)


def _s5_unblocked(z):
    lead = z.shape[:-1]
    nb = S5_NSTATE // S5_LANE_BLOCK
    both = z.reshape(*lead, nb, 2, S5_LANE_BLOCK)
    return both[..., 0, :].reshape(*lead, S5_NSTATE), both[..., 1, :].reshape(*lead, S5_NSTATE)


def _s5_tables(a_re, a_im, reverse):
    a = lax.complex(a_re, a_im)
    a2 = a * a
    a4 = a2 * a2
    rows = jnp.arange(8)
    pw = [a]
    for _ in range(7):
        pw.append(pw[-1] * a)
    pw = jnp.stack(pw)
    if reverse:
        pw = pw[::-1]
    tabs = []
    for coef, s in ((a, 1), (a2, 2), (a4, 4)):
        live = (rows <= 7 - s) if reverse else (rows >= s)
        tabs.append(jnp.where(live[:, None], coef[None, :], 0.0))
    tabs.append(pw)
    tabs = jnp.stack(tabs)
    return _s5_blocked(jnp.real(tabs), jnp.imag(tabs))


def _s5_scan_tile(v, tab_ref, prev, reverse):
    lb = S5_LANE_BLOCK
    vr, vi = v[:, :lb], v[:, lb:]
    for idx, s in enumerate((1, 2, 4)):
        cr, ci = tab_ref[idx, :, :lb], tab_ref[idx, :, lb:]
        sh = 8 - s if reverse else s
        sr, si = pltpu.roll(vr, sh, 0), pltpu.roll(vi, sh, 0)
        vr, vi = vr + cr * sr - ci * si, vi + cr * si + ci * sr
    row = 0 if reverse else 7
    pr = jnp.broadcast_to(prev[row:row + 1, :lb], (8, lb))
    pi = jnp.broadcast_to(prev[row:row + 1, lb:], (8, lb))
    cr, ci = tab_ref[3, :, :lb], tab_ref[3, :, lb:]
    return jnp.concatenate([vr + cr * pr - ci * pi, vi + cr * pi + ci * pr], axis=1)


def _s5_scan(v, tabs, reverse, *, name):
    length = v.shape[0]
    tb = min(512, length)
    ntb = length // tb
    nlb = S5_NSTATE // S5_LANE_BLOCK
    wb = 2 * S5_LANE_BLOCK
    ntile = tb // 8

    def body(tab_ref, v_ref, x_ref, carry_ref):
        @pl.when(pl.program_id(1) == 0)
        def _():
            carry_ref[...] = jnp.zeros_like(carry_ref)

        def step(i, prev):
            r0 = pl.multiple_of((ntile - 1 - i if reverse else i) * 8, 8)
            x = _s5_scan_tile(v_ref[pl.ds(r0, 8), :], tab_ref, prev, reverse)
            x_ref[pl.ds(r0, 8), :] = x
            return x

        carry_ref[...] = lax.fori_loop(0, ntile, step, carry_ref[...])

    tmap = (lambda c, t: (ntb - 1 - t, c)) if reverse else (lambda c, t: (t, c))
    return pl.pallas_call(
        body, name=name, grid=(nlb, ntb),
        in_specs=[pl.BlockSpec((4, 8, wb), lambda c, t: (0, 0, c)), pl.BlockSpec((tb, wb), tmap)],
        out_specs=pl.BlockSpec((tb, wb), tmap), out_shape=jax.ShapeDtypeStruct(v.shape, F32),
        scratch_shapes=[pltpu.VMEM((8, wb), F32)],
        compiler_params=pltpu.CompilerParams(dimension_semantics=("parallel", "arbitrary")),
    )(tabs, v)


def _s5_scan_adjoint(g, xs, tabs_conj, reverse, *, name):
    length = g.shape[0]
    tb = min(512, length)
    ntb = length // tb
    nlb = S5_NSTATE // S5_LANE_BLOCK
    lb = S5_LANE_BLOCK
    wb = 2 * lb
    ntile = tb // 8
    adj_rev = not reverse
    if reverse:
        edge = jnp.concatenate([xs[tb::tb], jnp.zeros((1, xs.shape[1]), F32)], axis=0)
    else:
        edge = jnp.concatenate([jnp.zeros((1, xs.shape[1]), F32), xs[tb - 1:length - 1:tb]], axis=0)
    edge = edge.reshape(ntb, 1, xs.shape[1])

    def body(tab_ref, g_ref, x_ref, edge_ref, lam_ref, da_ref, carry_ref):
        @pl.when(pl.program_id(1) == 0)
        def _():
            carry_ref[...] = jnp.zeros_like(carry_ref)
            da_ref[...] = jnp.zeros_like(da_ref)

        rows = lax.broadcasted_iota(jnp.int32, (8, wb), 0)

        def step(i, carry):
            prev, acc = carry
            k = ntile - 1 - i if adj_rev else i
            r0 = pl.multiple_of(k * 8, 8)
            lam = _s5_scan_tile(g_ref[pl.ds(r0, 8), :], tab_ref, prev, adj_rev)
            lam_ref[pl.ds(r0, 8), :] = lam
            x = x_ref[pl.ds(r0, 8), :]
            if reverse:
                kn = jnp.minimum(k + 1, ntile - 1)
                nb = x_ref[pl.ds(pl.multiple_of(kn * 8, 8), 8), :][0:1, :]
                nb = jnp.where(k == ntile - 1, edge_ref[0], nb)
                xp = jnp.where(rows == 7, jnp.broadcast_to(nb, (8, wb)), pltpu.roll(x, 7, 0))
            else:
                kn = jnp.maximum(k - 1, 0)
                nb = x_ref[pl.ds(pl.multiple_of(kn * 8, 8), 8), :][7:8, :]
                nb = jnp.where(k == 0, edge_ref[0], nb)
                xp = jnp.where(rows == 0, jnp.broadcast_to(nb, (8, wb)), pltpu.roll(x, 1, 0))
            xr, xi, lr, li = xp[:, :lb], xp[:, lb:], lam[:, :lb], lam[:, lb:]
            acc = acc + jnp.concatenate([xr * lr + xi * li, xr * li - xi * lr], axis=1)
            return lam, acc

        last, acc = lax.fori_loop(0, ntile, step, (carry_ref[...], da_ref[...]))
        carry_ref[...] = last
        da_ref[...] = acc

    tmap = (lambda c, t: (ntb - 1 - t, c)) if adj_rev else (lambda c, t: (t, c))
    emap = (lambda c, t: (ntb - 1 - t, 0, c)) if adj_rev else (lambda c, t: (t, 0, c))
    return pl.pallas_call(
        body, name=name, grid=(nlb, ntb),
        in_specs=[pl.BlockSpec((4, 8, wb), lambda c, t: (0, 0, c)), pl.BlockSpec((tb, wb), tmap),
                  pl.BlockSpec((tb, wb), tmap), pl.BlockSpec((1, 1, wb), emap)],
        out_specs=[pl.BlockSpec((tb, wb), tmap), pl.BlockSpec((8, wb), lambda c, t: (0, c))],
        out_shape=[jax.ShapeDtypeStruct(g.shape, F32), jax.ShapeDtypeStruct((8, g.shape[1]), F32)],
        scratch_shapes=[pltpu.VMEM((8, wb), F32)],
        compiler_params=pltpu.CompilerParams(dimension_semantics=("parallel", "arbitrary")),
    )(tabs_conj, g, xs, edge)


def _s5_prep(lam_re, lam_im, log_dt, b_re, b_im):
    lam = lax.complex(lam_re, lam_im)
    dt = jnp.exp(log_dt)[:, None]
    lam_bar = jnp.exp(lam * dt)
    b_bar = ((lam_bar - 1.0) / lam)[..., None] * lax.complex(b_re, b_im)
    return (jnp.real(lam_bar).reshape(-1), jnp.imag(lam_bar).reshape(-1), jnp.real(b_bar), jnp.imag(b_bar))


S5_NBLK = S5_NSTATE // S5_LANE_BLOCK
S5_BLK_GROUPS = S5_GROUPS // S5_NBLK
S5_BLK_CH = S5_BLK_GROUPS * S5_GROUP_CH


def _s5_in_matrix(bb_re, bb_im):
    eye = jnp.eye(S5_BLK_GROUPS, dtype=F32)
    def dense(bb):
        b4 = bb.reshape(S5_NBLK, S5_BLK_GROUPS, S5_STATE, S5_GROUP_CH)
        return jnp.einsum('cgph,gk->cghkp', b4, eye).reshape(S5_NBLK, S5_BLK_CH, S5_LANE_BLOCK)
    return jnp.concatenate([dense(bb_re), dense(bb_im)], axis=-1)


def _s5_block_diagonal(d):
    d5 = d.reshape(S5_NBLK, S5_BLK_GROUPS, S5_GROUP_CH, S5_BLK_GROUPS, S5_STATE)
    eye = jnp.eye(S5_BLK_GROUPS, dtype=F32)
    return jnp.swapaxes(jnp.sum(d5 * eye[None, :, None, :, None], axis=1), 1, 2)


def _s5_in_matrix_grad(d_mat):
    def diag(d):
        return jnp.swapaxes(_s5_block_diagonal(d), 2, 3).reshape(S5_GROUPS, S5_STATE, S5_GROUP_CH)
    return diag(d_mat[..., :S5_LANE_BLOCK]), diag(d_mat[..., S5_LANE_BLOCK:])


def _s5_out_matrix(c_re, c_im):
    eye = jnp.eye(S5_BLK_GROUPS, dtype=F32)
    def dense(cc):
        c4 = cc.reshape(S5_NBLK, S5_BLK_GROUPS, S5_GROUP_CH, S5_STATE)
        return jnp.einsum('cghp,gk->cgpkh', c4, eye).reshape(S5_NBLK, S5_LANE_BLOCK, S5_BLK_CH)
    return jnp.concatenate([dense(c_re), dense(-c_im)], axis=1)


def _s5_out_matrix_grad(d_mat_t):
    def diag(d):
        return _s5_block_diagonal(d).reshape(S5_GROUPS, S5_GROUP_CH, S5_STATE)
    return diag(d_mat_t[..., :S5_LANE_BLOCK]), -diag(d_mat_t[..., S5_LANE_BLOCK:])


def _gmm(a, b, *, tb=False, name):
    arr, cb0, wa = a
    nblk = b.shape[0]
    wn = b.shape[1] if tb else b.shape[2]
    length = arr.shape[0]
    tm = _pick(length, (1024, 512, 256, 128))
    dims = (((1,), (1 if tb else 0,)), ((), ()))

    def body(a_ref, b_ref, o_ref):
        o_ref[...] = lax.dot_general(a_ref[...].astype(BF16), b_ref[0].astype(BF16), dims, preferred_element_type=F32)

    return pl.pallas_call(
        body, name=name, grid=(nblk, length // tm),
        in_specs=[pl.BlockSpec((tm, wa), lambda c, i: (i, cb0 + c)), pl.BlockSpec((1,) + b.shape[1:], lambda c, i: (c, 0, 0))],
        out_specs=pl.BlockSpec((tm, wn), lambda c, i: (i, c)), out_shape=jax.ShapeDtypeStruct((length, nblk * wn), F32),
        compiler_params=pltpu.CompilerParams(dimension_semantics=("parallel", "parallel")),
    )(arr, b)


def _gmm_tn(a, g, *, nblk, name):
    arr_a, cb_a, wa = a
    arr_g, cb_g, wg = g
    length = arr_a.shape[0]
    dims = (((0,), (0,)), ((), ()))

    def body(a_ref, g_ref, o_ref):
        o_ref[0] = lax.dot_general(a_ref[...].astype(BF16), g_ref[...].astype(BF16), dims, preferred_element_type=F32)

    return pl.pallas_call(
        body, name=name, grid=(nblk,),
        in_specs=[pl.BlockSpec((length, wa), lambda c: (0, cb_a + c)), pl.BlockSpec((length, wg), lambda c: (0, cb_g + c))],
        out_specs=pl.BlockSpec((1, wa, wg), lambda c: (c, 0, 0)), out_shape=jax.ShapeDtypeStruct((nblk, wa, wg), F32),
        compiler_params=pltpu.CompilerParams(dimension_semantics=("parallel",)),
    )(arr_a, arr_g)


def _gelu_parts(x):
    k = math.sqrt(2.0 / math.pi)
    inner = k * (x + 0.044715 * x * x * x)
    th = jnp.tanh(inner)
    return th, k * (1.0 + 3.0 * 0.044715 * x * x)


S5_CB_U = CB_U * (512 // S5_BLK_CH)


def _s5_direction_fwd(pg, b_mat, c_mat, tabs, reverse, *, name):
    length = pg.shape[0]
    tb = min(512, length)
    ntb = length // tb
    wb = 2 * S5_LANE_BLOCK
    ntile = tb // 8

    def body(tab_ref, u_ref, b_ref, c_ref, x_ref, y_ref, carry_ref, bu_ref):
        @pl.when(pl.program_id(1) == 0)
        def _():
            carry_ref[...] = jnp.zeros_like(carry_ref)

        bu_ref[...] = jnp.dot(u_ref[...].astype(BF16), b_ref[0], preferred_element_type=F32)

        def step(i, prev):
            r0 = pl.multiple_of((ntile - 1 - i if reverse else i) * 8, 8)
            x = _s5_scan_tile(bu_ref[pl.ds(r0, 8), :], tab_ref, prev, reverse)
            x_ref[pl.ds(r0, 8), :] = x
            return x

        carry_ref[...] = lax.fori_loop(0, ntile, step, carry_ref[...])
        y_ref[...] = jnp.dot(x_ref[...].astype(BF16), c_ref[0], preferred_element_type=F32)

    tix = (lambda t: ntb - 1 - t) if reverse else (lambda t: t)
    return pl.pallas_call(
        body, name=name, grid=(S5_NBLK, ntb),
        in_specs=[pl.BlockSpec((4, 8, wb), lambda c, t: (0, 0, c)),
                  pl.BlockSpec((tb, S5_BLK_CH), lambda c, t: (tix(t), S5_CB_U + c)),
                  pl.BlockSpec((1, S5_BLK_CH, wb), lambda c, t: (c, 0, 0)),
                  pl.BlockSpec((1, wb, S5_BLK_CH), lambda c, t: (c, 0, 0))],
        out_specs=[pl.BlockSpec((tb, wb), lambda c, t: (tix(t), c)), pl.BlockSpec((tb, S5_BLK_CH), lambda c, t: (tix(t), c))],
        out_shape=[jax.ShapeDtypeStruct((length, S5_NBLK * wb), F32), jax.ShapeDtypeStruct((length, S5_WIDTH), F32)],
        scratch_shapes=[pltpu.VMEM((8, wb), F32), pltpu.VMEM((tb, wb), F32)],
        compiler_params=pltpu.CompilerParams(dimension_semantics=("parallel", "arbitrary")),
    )(tabs, pg, b_mat, c_mat)


def _s5_direction_bwd(pg, dy, xs, b_mat, c_mat, tabs_conj, reverse, *, name):
    length = pg.shape[0]
    tb = min(512, length)
    ntb = length // tb
    lb = S5_LANE_BLOCK
    wb = 2 * lb
    ntile = tb // 8
    adj_rev = not reverse
    if reverse:
        edge = jnp.concatenate([xs[tb::tb], jnp.zeros((1, xs.shape[1]), F32)], axis=0)
    else:
        edge = jnp.concatenate([jnp.zeros((1, xs.shape[1]), F32), xs[tb - 1:length - 1:tb]], axis=0)
    edge = edge.reshape(ntb, 1, xs.shape[1])

    def body(tab_ref, u_ref, dy_ref, x_ref, edge_ref, b_ref, c_ref, du_ref, db_ref, dc_ref, da_ref, carry_ref, g_ref, lam_ref):
        @pl.when(pl.program_id(1) == 0)
        def _():
            carry_ref[...] = jnp.zeros_like(carry_ref)
            da_ref[...] = jnp.zeros_like(da_ref)
            db_ref[...] = jnp.zeros_like(db_ref)
            dc_ref[...] = jnp.zeros_like(dc_ref)

        dyb = dy_ref[...].astype(BF16)
        g_ref[...] = lax.dot_general(dyb, c_ref[0], _NT, preferred_element_type=F32)
        rows = lax.broadcasted_iota(jnp.int32, (8, wb), 0)

        def step(i, carry):
            prev, acc = carry
            k = ntile - 1 - i if adj_rev else i
            r0 = pl.multiple_of(k * 8, 8)
            lam = _s5_scan_tile(g_ref[pl.ds(r0, 8), :], tab_ref, prev, adj_rev)
            lam_ref[pl.ds(r0, 8), :] = lam
            x = x_ref[pl.ds(r0, 8), :]
            if reverse:
                kn = jnp.minimum(k + 1, ntile - 1)
                nb = x_ref[pl.ds(pl.multiple_of(kn * 8, 8), 8), :][0:1, :]
                nb = jnp.where(k == ntile - 1, edge_ref[0], nb)
                xp = jnp.where(rows == 7, jnp.broadcast_to(nb, (8, wb)), pltpu.roll(x, 7, 0))
            else:
                kn = jnp.maximum(k - 1, 0)
                nb = x_ref[pl.ds(pl.multiple_of(kn * 8, 8), 8), :][7:8, :]
                nb = jnp.where(k == 0, edge_ref[0], nb)
                xp = jnp.where(rows == 0, jnp.broadcast_to(nb, (8, wb)), pltpu.roll(x, 1, 0))
            xr, xi, lr, li = xp[:, :lb], xp[:, lb:], lam[:, :lb], lam[:, lb:]
            return lam, acc + jnp.concatenate([xr * lr + xi * li, xr * li - xi * lr], axis=1)

        last, acc = lax.fori_loop(0, ntile, step, (carry_ref[...], da_ref[...]))
        carry_ref[...] = last
        da_ref[...] = acc
        lamb = lam_ref[...].astype(BF16)
        du_ref[...] = lax.dot_general(lamb, b_ref[0], _NT, preferred_element_type=F32)
        db_ref[0] += lax.dot_general(u_ref[...].astype(BF16), lamb, _TN, preferred_element_type=F32)
        dc_ref[0] += lax.dot_general(dyb, x_ref[...].astype(BF16), _TN, preferred_element_type=F32)

    tix = (lambda t: ntb - 1 - t) if adj_rev else (lambda t: t)
    wide = pl.BlockSpec((tb, wb), lambda c, t: (tix(t), c))
    mat = pl.BlockSpec((1, S5_BLK_CH, wb), lambda c, t: (c, 0, 0))
    return pl.pallas_call(
        body, name=name, grid=(S5_NBLK, ntb),
        in_specs=[pl.BlockSpec((4, 8, wb), lambda c, t: (0, 0, c)),
                  pl.BlockSpec((tb, S5_BLK_CH), lambda c, t: (tix(t), S5_CB_U + c)),
                  pl.BlockSpec((tb, S5_BLK_CH), lambda c, t: (tix(t), c)), wide,
                  pl.BlockSpec((1, 1, wb), lambda c, t: (tix(t), 0, c)), mat,
                  pl.BlockSpec((1, wb, S5_BLK_CH), lambda c, t: (c, 0, 0))],
        out_specs=[pl.BlockSpec((tb, S5_BLK_CH), lambda c, t: (tix(t), c)), mat, mat, pl.BlockSpec((8, wb), lambda c, t: (0, c))],
        out_shape=[jax.ShapeDtypeStruct((length, S5_WIDTH), F32), jax.ShapeDtypeStruct((S5_NBLK, S5_BLK_CH, wb), F32),
                   jax.ShapeDtypeStruct((S5_NBLK, S5_BLK_CH, wb), F32), jax.ShapeDtypeStruct((8, S5_NBLK * wb), F32)],
        scratch_shapes=[pltpu.VMEM((8, wb), F32), pltpu.VMEM((tb, wb), F32), pltpu.VMEM((tb, wb), F32)],
        compiler_params=pltpu.CompilerParams(dimension_semantics=("parallel", "arbitrary")),
    )(tabs_conj, pg, dy, xs, edge, b_mat, c_mat)


def _both(fn):
    return jax.vmap(jax.vmap(fn))


def _s5_setup(w):
    a_re, a_im, bb_re, bb_im = _both(_s5_prep)(w['s5_lambda_re'], w['s5_lambda_im'], w['s5_log_dt'], w['s5_b_re'], w['s5_b_im'])

    def tables(d, conj, reverse):
        return jax.vmap(lambda r, i: _s5_tables(r, -i if conj else i, reverse))(a_re[:, d], a_im[:, d])
    return {'b_mat': _both(_s5_in_matrix)(bb_re, bb_im).astype(BF16),
            'c_mat': _both(_s5_out_matrix)(w['s5_c_re'], w['s5_c_im']).astype(BF16),
            'tabs': [tables(0, False, False), tables(1, False, True)],
            'tabs_adj': [tables(0, True, True), tables(1, True, False)]}


def _s5_param_grads(w, raws):
    def stacked(k):
        return jnp.stack([jnp.stack([raws[i][d][k] for d in range(2)]) for i in range(DEPTH)])
    dbb_re, dbb_im = _both(_s5_in_matrix_grad)(stacked(0))
    dc_re, dc_im = _both(_s5_out_matrix_grad)(stacked(1))
    da_re, da_im = _s5_unblocked(jnp.sum(stacked(2), axis=2))
    _, vjp = jax.vjp(_both(_s5_prep), w['s5_lambda_re'], w['s5_lambda_im'], w['s5_log_dt'], w['s5_b_re'], w['s5_b_im'])
    g = vjp((da_re, da_im, dbb_re, dbb_im))
    return {'s5_lambda_re': g[0], 's5_lambda_im': g[1], 's5_log_dt': g[2], 's5_b_re': g[3], 's5_b_im': g[4],
            's5_c_re': dc_re, 's5_c_im': dc_im}


def _s5_fwd(p_in, prm, w_glu):
    dirs = []
    ys = []
    for d, reverse in ((0, False), (1, True)):
        xs, y_dir = _s5_direction_fwd(p_in, prm['b_mat'][d], prm['c_mat'][d], prm['tabs'][d], reverse,
                                      name="s5_fwd_rev" if reverse else "s5_fwd")
        ys.append(y_dir)
        dirs.append(xs)

    def post(yf, yb, u, dskip):
        ypre = yf + yb + dskip * u
        th, _ = _gelu_parts(ypre)
        return ypre, 0.5 * ypre * (1.0 + th)
    ypre, yg = _rowmap(post, [ys[0], ys[1], (p_in, CB_U, S5_WIDTH)], [prm['d'].reshape(1, -1)],
                       [(S5_WIDTH, F32), (S5_WIDTH, F32)], tl=512, name="s5_post")
    t = _mm(yg, w_glu, name="s5_glu_mm")

    def glu(ygv, tv):
        return ygv * _sigmoid(tv)
    y = _rowmap(glu, [yg, t], [], [(S5_WIDTH, BF16)], tl=512, name="s5_glu")[0]
    return y, (dirs, ypre, yg, t)


def _s5_bwd(pg, prm, w_glu, saved, dy):
    dirs, ypre, yg, t = saved

    def glu_bwd(dyv, ygv, tv):
        s = _sigmoid(tv)
        return dyv * ygv * s * (1.0 - s), dyv * s
    dt, dyg_direct = _rowmap(glu_bwd, [dy, yg, t], [], [(S5_WIDTH, BF16), (S5_WIDTH, F32)], tl=512, name="s5_glu_bwd")
    grads = {'w_glu': _mm(yg, dt, ta=True, out_dtype=BF16, name="s5_dwglu")}
    dyg_mm = _mm(dt, w_glu, tb=True, name="s5_dyg")

    def post_bwd(dyd, dym, yp, u, dskip):
        th, dinner = _gelu_parts(yp)
        dyp = (dyd + dym) * (0.5 * (1.0 + th) + 0.5 * yp * (1.0 - th * th) * dinner)
        return dyp, dyp * dskip, jnp.sum(dyp * u, axis=0, keepdims=True)
    dyp, du_skip, dd = _rowmap(post_bwd, [dyg_direct, dyg_mm, ypre, (pg, CB_U, S5_WIDTH)], [prm['d'].reshape(1, -1)],
                               [(S5_WIDTH, F32), (S5_WIDTH, F32)], [(1, S5_WIDTH)], tl=512, name="s5_post_bwd")
    grads['d'] = dd[0]
    du = [du_skip]
    grads['raw'] = []
    for d, reverse in ((0, False), (1, True)):
        du_dir, d_bmat, d_cmat_t, da = _s5_direction_bwd(pg, dyp, dirs[d], prm['b_mat'][d], prm['c_mat'][d], prm['tabs_adj'][d],
                                                         reverse, name="s5_bwd_rev" if reverse else "s5_bwd")
        du.append(du_dir)
        grads['raw'].append((d_bmat, d_cmat_t, da))
    return du, grads


def _split3(x):
    hi = x.astype(BF16)
    r = x - hi.astype(F32)
    mid = r.astype(BF16)
    return hi, mid, (r - mid.astype(F32)).astype(BF16)


def _exact_dot(ones, x, dims):
    parts = [lax.dot_general(ones, p, dims, preferred_element_type=F32) for p in _split3(x)]
    return parts[0] + parts[1] + parts[2]


_NN = (((1,), (0,)), ((), ()))
_NT = (((1,), (1,)), ((), ()))
_TN = (((0,), (0,)), ((), ()))


def _dot(a, b, dims=_NN):
    return lax.dot_general(a.astype(BF16), b.astype(BF16), dims, preferred_element_type=F32)


def _gla_chunk_mask(reverse):
    rows = lax.broadcasted_iota(jnp.int32, (GLA_CHUNK, GLA_CHUNK), 0)
    cols = lax.broadcasted_iota(jnp.int32, (GLA_CHUNK, GLA_CHUNK), 1)
    return (cols >= rows) if reverse else (cols <= rows)


def _gla_fwd(pg, la, reverse, *, name):
    length = la.shape[0]
    nch = length // GLA_CHUNK
    scale = GLA_HEAD_DIM ** -0.5
    last = 0 if reverse else GLA_CHUNK - 1
    hd = GLA_HEAD_DIM

    def body(q_ref, k_ref, v_ref, la_ref, o_ref, sp_ref, st_ref):
        @pl.when(pl.program_id(0) == 0)
        def _():
            st_ref[...] = jnp.zeros_like(st_ref)

        mask = _gla_chunk_mask(reverse)
        b = _exact_dot(mask.astype(BF16), la_ref[...], _NN)
        sp_ref[0] = st_ref[...]
        outs = []
        for h in range(GLA_HEADS):
            sl = slice(h * hd, (h + 1) * hd)
            bh = b[:, sl]
            bl = bh[last:last + 1, :]
            k = k_ref[:, sl]
            v = v_ref[:, sl]
            qd = q_ref[:, sl] * scale * jnp.exp(bh)
            kd = k * jnp.exp(-bh)
            ke = k * jnp.exp(bl - bh)
            st = st_ref[sl, :]
            p = jnp.where(mask, _dot(qd, kd, _NT), 0.0)
            outs.append(_dot(p, v) + _dot(qd, st, _NT))
            st_ref[sl, :] = st * jnp.exp(bl) + _dot(v, ke, _TN)
        o_ref[...] = jnp.concatenate(outs, axis=1)

    cmap = (lambda n: nch - 1 - n) if reverse else (lambda n: n)
    col = lambda cb: pl.BlockSpec((GLA_CHUNK, GLA_WIDTH), lambda n, cb=cb: (cmap(n), cb))
    return pl.pallas_call(
        body, name=name, grid=(nch,),
        in_specs=[col(CB_GQ), col(CB_GK), col(CB_GV), col(0)],
        out_specs=[col(0), pl.BlockSpec((1, GLA_WIDTH, hd), lambda n: (cmap(n), 0, 0))],
        out_shape=[jax.ShapeDtypeStruct((length, GLA_WIDTH), F32), jax.ShapeDtypeStruct((nch, GLA_WIDTH, hd), F32)],
        scratch_shapes=[pltpu.VMEM((GLA_WIDTH, hd), F32)],
        compiler_params=pltpu.CompilerParams(dimension_semantics=("arbitrary",)),
    )(pg, pg, pg, la)


def _gla_bwd(pg, la, do, sprev, reverse, *, name):
    length = la.shape[0]
    nch = length // GLA_CHUNK
    scale = GLA_HEAD_DIM ** -0.5
    last = 0 if reverse else GLA_CHUNK - 1
    hd = GLA_HEAD_DIM

    def body(q_ref, k_ref, v_ref, la_ref, do_ref, sp_ref, dq_ref, dk_ref, dv_ref, dla_ref, dst_ref):
        @pl.when(pl.program_id(0) == 0)
        def _():
            dst_ref[...] = jnp.zeros_like(dst_ref)

        mask = _gla_chunk_mask(reverse)
        tri = mask.astype(BF16)
        b = _exact_dot(tri, la_ref[...], _NN)
        is_last = lax.broadcasted_iota(jnp.int32, (GLA_CHUNK, hd), 0) == last
        dqs, dks, dvs, dbs = [], [], [], []
        for h in range(GLA_HEADS):
            sl = slice(h * hd, (h + 1) * hd)
            bh = b[:, sl]
            bl = bh[last:last + 1, :]
            eb, enb, ebl, el = jnp.exp(bh), jnp.exp(-bh), jnp.exp(bl - bh), jnp.exp(bl)
            k = k_ref[:, sl]
            v = v_ref[:, sl]
            dov = do_ref[:, sl]
            qd = q_ref[:, sl] * scale * eb
            kd = k * enb
            ke = k * ebl
            st = sp_ref[0, sl, :]
            dst = dst_ref[sl, :]
            p = jnp.where(mask, _dot(qd, kd, _NT), 0.0)
            dp = jnp.where(mask, _dot(dov, v, _NT), 0.0)
            dqd = _dot(dp, kd) + _dot(dov, st)
            dkd = _dot(dp, qd, _TN)
            dvs.append(_dot(p, dov, _TN) + _dot(ke, dst, _NT))
            dke = _dot(v, dst)
            dst_ref[sl, :] = dst * el + _dot(dov, qd, _TN)
            dbl = el * jnp.sum(dst * st, axis=0, keepdims=True) + jnp.sum(dke * ke, axis=0, keepdims=True)
            db = dqd * qd - dkd * kd - dke * ke
            dbs.append(jnp.where(is_last, db + dbl, db))
            dqs.append(dqd * eb * scale)
            dks.append(dkd * enb + dke * ebl)
        dq_ref[...] = jnp.concatenate(dqs, axis=1)
        dk_ref[...] = jnp.concatenate(dks, axis=1)
        dv_ref[...] = jnp.concatenate(dvs, axis=1)
        tri_t = _gla_chunk_mask(not reverse).astype(BF16)
        dla_ref[...] = _exact_dot(tri_t, jnp.concatenate(dbs, axis=1), _NN)

    cmap = (lambda n: n) if reverse else (lambda n: nch - 1 - n)
    col = lambda cb: pl.BlockSpec((GLA_CHUNK, GLA_WIDTH), lambda n, cb=cb: (cmap(n), cb))
    wide = jax.ShapeDtypeStruct((length, GLA_WIDTH), F32)
    return pl.pallas_call(
        body, name=name, grid=(nch,),
        in_specs=[col(CB_GQ), col(CB_GK), col(CB_GV), col(0), col(0),
                  pl.BlockSpec((1, GLA_WIDTH, hd), lambda n: (cmap(n), 0, 0))],
        out_specs=[col(0)] * 4, out_shape=[wide] * 4,
        scratch_shapes=[pltpu.VMEM((GLA_WIDTH, hd), F32)],
        compiler_params=pltpu.CompilerParams(dimension_semantics=("arbitrary",)),
    )(pg, pg, pg, la, do, sprev)


def _log_sigmoid(x):
    return jnp.minimum(x, 0.0) - jnp.log(1.0 + jnp.exp(-jnp.abs(x)))


def _gla_alpha_padded(w_alpha):
    w = jnp.zeros((2, 128, GLA_WIDTH), w_alpha.dtype)
    w = w.at[0, 0:GLA_LOWRANK].set(w_alpha[0])
    return w.at[1, GLA_LOWRANK:2 * GLA_LOWRANK].set(w_alpha[1])


def _gla_branch_fwd(pg, w_alpha, b_alpha, norm_gain):
    wa = _gla_alpha_padded(w_alpha).astype(BF16)

    def gates(z, w, bias):
        return (_log_sigmoid(_dot(z, w[0]) + bias[0:1]) / GLA_TAU, _log_sigmoid(_dot(z, w[1]) + bias[1:2]) / GLA_TAU)
    la_f, la_b = _rowmap(gates, [(pg, CB_Z, 128)], [wa, b_alpha], [(GLA_WIDTH, F32), (GLA_WIDTH, F32)], tl=512,
                         name="gla_gates")
    o_f, sp_f = _gla_fwd(pg, la_f, False, name="gla_fwd")
    o_b, sp_b = _gla_fwd(pg, la_b, True, name="gla_fwd_rev")

    def post(of, ob, gate, gn):
        o = of + ob
        on = jnp.concatenate([o[:, s:s + GLA_HEAD_DIM] * _rms(o[:, s:s + GLA_HEAD_DIM]) * gn
                              for s in range(0, GLA_WIDTH, GLA_HEAD_DIM)], axis=1)
        return o, on * (gate * _sigmoid(gate))
    o, y = _rowmap(post, [o_f, o_b, (pg, CB_GG, GLA_WIDTH)], [norm_gain.reshape(1, -1)],
                   [(GLA_WIDTH, F32), (GLA_WIDTH, BF16)], tl=512, name="gla_post")
    return y, (wa, la_f, la_b, sp_f, sp_b, o)


def _gla_branch_bwd(pg, w_alpha, b_alpha, norm_gain, saved, dy):
    wa, la_f, la_b, sp_f, sp_b, o = saved

    def post_bwd(dyv, ov, gate, gn):
        s = _sigmoid(gate)
        dos, dgn, ons = [], [], []
        for c in range(0, GLA_WIDTH, GLA_HEAD_DIM):
            oh = ov[:, c:c + GLA_HEAD_DIM]
            r = _rms(oh)
            don = dyv[:, c:c + GLA_HEAD_DIM] * (gate[:, c:c + GLA_HEAD_DIM] * s[:, c:c + GLA_HEAD_DIM])
            gd = don * gn
            dos.append(r * gd - oh * (r * r * r) * jnp.mean(oh * gd, axis=-1, keepdims=True))
            dgn.append(jnp.sum(don * oh * r, axis=0, keepdims=True))
            ons.append(oh * r * gn)
        on = jnp.concatenate(ons, axis=1)
        dgate = dyv * on * (s * (1.0 + gate * (1.0 - s)))
        return jnp.concatenate(dos, axis=1), dgate, jnp.concatenate(dgn, axis=1)
    do, dgate, dgn = _rowmap(post_bwd, [dy, o, (pg, CB_GG, GLA_WIDTH)], [norm_gain.reshape(1, -1)],
                             [(GLA_WIDTH, F32), (GLA_WIDTH, F32)], [(1, GLA_WIDTH)], tl=512, name="gla_post_bwd")
    dq_f, dk_f, dv_f, dla_f = _gla_bwd(pg, la_f, do, sp_f, False, name="gla_bwd")
    dq_b, dk_b, dv_b, dla_b = _gla_bwd(pg, la_b, do, sp_b, True, name="gla_bwd_rev")

    def gates_bwd(z, dlf, dlb, w, bias):
        dz = jnp.zeros_like(z)
        dlogits, dbs = [], []
        for d, dl in ((0, dlf), (1, dlb)):
            logit = _dot(z, w[d]) + bias[d:d + 1]
            dlogit = dl * (1.0 / GLA_TAU) * jnp.exp(_log_sigmoid(-logit))
            dz = dz + _dot(dlogit, w[d], _NT)
            dlogits.append(dlogit)
            dbs.append(jnp.sum(dlogit, axis=0, keepdims=True))
        return dz, dlogits[0], dlogits[1], dbs[0], dbs[1]
    dz, dlg_f, dlg_b, dba_f, dba_b = _rowmap(
        gates_bwd, [(pg, CB_Z, 128), dla_f, dla_b], [wa, b_alpha], [(128, F32), (GLA_WIDTH, BF16), (GLA_WIDTH, BF16)],
        [(1, GLA_WIDTH), (1, GLA_WIDTH)], tl=512, name="gla_gates_bwd")
    dwa_f = _mm(dlg_f, (pg, CB_Z, 128), ta=True, name="gla_dwalpha")
    dwa_b = _mm(dlg_b, (pg, CB_Z, 128), ta=True, name="gla_dwalpha")
    grads = {'w_alpha': jnp.stack([dwa_f[:, 0:GLA_LOWRANK].T, dwa_b[:, GLA_LOWRANK:2 * GLA_LOWRANK].T]),
             'b_alpha': jnp.concatenate([dba_f, dba_b], axis=0),
             'norm': jnp.sum(dgn.reshape(GLA_HEADS, GLA_HEAD_DIM), axis=0)}
    return [dq_f, dq_b], [dk_f, dk_b], [dv_f, dv_b], dgate, dz, grads


def _rope_tables(length):
    half = ATTN_HEAD_DIM // 2
    inv_freq = ROPE_BASE ** (-jnp.arange(half // 2, dtype=F32) * 2.0 / half)
    t = jnp.arange(length, dtype=jnp.int32)
    def one(pos):
        ang = pos.astype(F32)[:, None] * inv_freq[None, :]
        c, s = jnp.cos(ang), jnp.sin(ang)
        return jnp.concatenate([c, c], axis=1), jnp.concatenate([-s, s], axis=1)
    c_r, s_r = one(t // GRID_W)
    c_c, s_c = one(t % GRID_W)
    return jnp.concatenate([c_r, c_c], axis=1), jnp.concatenate([s_r, s_c], axis=1)


def _rope_swap(y):
    w = y.shape[1]
    lane = lax.broadcasted_iota(jnp.int32, y.shape, 1)
    return jnp.where(lane % 32 < 16, pltpu.roll(y, w - 16, 1), pltpu.roll(y, 16, 1))


def _head_sums(x, ones):
    parts = [lax.dot_general(p, ones, _NN, preferred_element_type=F32) for p in _split3(x)]
    return parts[0] + parts[1] + parts[2]


def _head_ones(width):
    seg = np.arange(width) // ATTN_HEAD_DIM
    return jnp.asarray(seg[:, None] == seg[None, :], BF16)


def _qk_prep_fwd(pg, cb, width, gain, cos, sin, scale, *, name):
    heads = width // ATTN_HEAD_DIM
    def fn(x, c, s, g, ones):
        r = lax.rsqrt(_head_sums(x * x, ones) * (1.0 / ATTN_HEAD_DIM) + NORM_EPS)
        y = x * r * g
        return (y * c + _rope_swap(y) * s) * scale
    return _rowmap(fn, [(pg, cb, width), jnp.tile(cos, (1, heads)), jnp.tile(sin, (1, heads))],
                   [jnp.tile(gain, heads).reshape(1, -1), _head_ones(width)], [(width, BF16)], tl=512, name=name)[0]


def _qk_prep_bwd(pg, cb, width, gain, cos, sin, scale, dout, *, name):
    heads = width // ATTN_HEAD_DIM
    def fn(x, dov, c, s, g, ones):
        r = lax.rsqrt(_head_sums(x * x, ones) * (1.0 / ATTN_HEAD_DIM) + NORM_EPS)
        dos = dov * scale
        dy = dos * c + _rope_swap(dos * s)
        gd = dy * g
        dx = r * gd - x * (r * r * r) * (_head_sums(x * gd, ones) * (1.0 / ATTN_HEAD_DIM))
        return dx, jnp.sum(dy * x * r, axis=0, keepdims=True)
    dx, dg = _rowmap(fn, [(pg, cb, width), dout, jnp.tile(cos, (1, heads)), jnp.tile(sin, (1, heads))],
                     [jnp.tile(gain, heads).reshape(1, -1), _head_ones(width)], [(width, F32)], [(1, width)], tl=512,
                     name=name)
    return dx, jnp.sum(dg.reshape(heads, ATTN_HEAD_DIM), axis=0)


def _to_heads(x, heads):
    return jnp.transpose(x.reshape(x.shape[0], heads, ATTN_HEAD_DIM), (1, 0, 2))


def _from_heads(x):
    return jnp.transpose(x, (1, 0, 2)).reshape(x.shape[1], x.shape[0] * ATTN_HEAD_DIM)


ATTN_GROUP = ATTN_Q_HEADS // ATTN_KV_HEADS
ATTN_TQ = 256


def _attn_fwd(q, k, v):
    length = q.shape[1]
    tq = min(ATTN_TQ, length)

    def body(q_ref, k_ref, v_ref, o_ref):
        kk, vv = k_ref[0], v_ref[0]
        for g in range(ATTN_GROUP):
            s = _dot(q_ref[g], kk, _NT)
            p = jnp.exp(s - jnp.max(s, axis=-1, keepdims=True))
            o_ref[g] = _dot(p, vv) / jnp.sum(p, axis=-1, keepdims=True)

    kv_spec = pl.BlockSpec((1, length, ATTN_HEAD_DIM), lambda h, i: (h, 0, 0))
    q_spec = pl.BlockSpec((ATTN_GROUP, tq, ATTN_HEAD_DIM), lambda h, i: (h, i, 0))
    return pl.pallas_call(
        body, name="attn_fwd", grid=(ATTN_KV_HEADS, length // tq), in_specs=[q_spec, kv_spec, kv_spec],
        out_specs=q_spec, out_shape=jax.ShapeDtypeStruct(q.shape, F32),
        compiler_params=pltpu.CompilerParams(dimension_semantics=("parallel", "parallel")),
    )(q, k, v)


def _attn_bwd(q, k, v, o, do):
    length = q.shape[1]
    tq = min(ATTN_TQ, length)

    def body(q_ref, k_ref, v_ref, o_ref, do_ref, dq_ref, dk_ref, dv_ref):
        @pl.when(pl.program_id(1) == 0)
        def _():
            dk_ref[...] = jnp.zeros_like(dk_ref)
            dv_ref[...] = jnp.zeros_like(dv_ref)

        kk, vv = k_ref[0], v_ref[0]
        for g in range(ATTN_GROUP):
            qg, dog = q_ref[g], do_ref[g]
            s = _dot(qg, kk, _NT)
            p = jnp.exp(s - jnp.max(s, axis=-1, keepdims=True))
            p = p / jnp.sum(p, axis=-1, keepdims=True)
            dp = _dot(dog, vv, _NT)
            ds = p * (dp - jnp.sum(dog * o_ref[g], axis=-1, keepdims=True))
            dq_ref[g] = _dot(ds, kk)
            dk_ref[0] += _dot(ds, qg, _TN)
            dv_ref[0] += _dot(p, dog, _TN)

    kv_spec = pl.BlockSpec((1, length, ATTN_HEAD_DIM), lambda h, i: (h, 0, 0))
    q_spec = pl.BlockSpec((ATTN_GROUP, tq, ATTN_HEAD_DIM), lambda h, i: (h, i, 0))
    return pl.pallas_call(
        body, name="attn_bwd", grid=(ATTN_KV_HEADS, length // tq),
        in_specs=[q_spec, kv_spec, kv_spec, q_spec, q_spec], out_specs=[q_spec, kv_spec, kv_spec],
        out_shape=[jax.ShapeDtypeStruct(q.shape, F32), jax.ShapeDtypeStruct(k.shape, F32),
                   jax.ShapeDtypeStruct(k.shape, F32)],
        compiler_params=pltpu.CompilerParams(dimension_semantics=("parallel", "arbitrary")),
    )(q, k, v, o, do)


def _attn_branch_fwd(pg, q_gain, k_gain):
    cos, sin = _rope_tables(pg.shape[0])
    qp = _qk_prep_fwd(pg, CB_AQ, ATTN_WIDTH, q_gain, cos, sin, ATTN_HEAD_DIM ** -0.5, name="attn_q_prep")
    kp = _qk_prep_fwd(pg, CB_AK, ATTN_KV_WIDTH, k_gain, cos, sin, 1.0, name="attn_k_prep")
    qh, kh = _to_heads(qp, ATTN_Q_HEADS), _to_heads(kp, ATTN_KV_HEADS)
    vh = _to_heads(pg[:, P_OFF + 3200:P_OFF + 3328].astype(BF16), ATTN_KV_HEADS)
    oh = _attn_fwd(qh, kh, vh)
    return _from_heads(oh).astype(BF16), (cos, sin, qh, kh, vh, oh)


def _attn_branch_bwd(pg, q_gain, k_gain, saved, dy):
    cos, sin, qh, kh, vh, oh = saved
    dqh, dkh, dvh = _attn_bwd(qh, kh, vh, oh, _to_heads(dy, ATTN_Q_HEADS))
    dq, dqg = _qk_prep_bwd(pg, CB_AQ, ATTN_WIDTH, q_gain, cos, sin, ATTN_HEAD_DIM ** -0.5, _from_heads(dqh),
                           name="attn_q_prep_bwd")
    dk, dkg = _qk_prep_bwd(pg, CB_AK, ATTN_KV_WIDTH, k_gain, cos, sin, 1.0, _from_heads(dkh), name="attn_k_prep_bwd")
    return dq, dk, _from_heads(dvh), {'q_norm': dqg, 'k_norm': dkg}


def _gate_cols():
    return [slice(i * D_MODEL, (i + 1) * D_MODEL) for i in range(3)]


def _mixer_fwd(x, lw):
    h = _rmsnorm_fwd(x, lw['mix_norm'])
    pg = _mm(h, lw['w_pg'], name="mix_in")
    y_s5, s_s5 = _s5_fwd(pg, lw['s5'], lw['s5_w_glu'])
    y_gla, s_gla = _gla_branch_fwd(pg, lw['gla_w_alpha'], lw['gla_b_alpha'], lw['gla_norm'])
    y_att, s_att = _attn_branch_fwd(pg, lw['attn_q_norm'], lw['attn_k_norm'])
    ys = (y_s5, y_gla, y_att)
    br = [_mm(y, lw[n], name="mix_branch") for y, n in zip(ys, ('w_branch_s5', 'w_branch_gla', 'w_branch_attn'))]

    def merge(g0, g1, g2, b0, b1, b2, bias):
        acc = None
        for g, b, c in zip((g0, g1, g2), (b0, b1, b2), _gate_cols()):
            term = _sigmoid(g + bias[:, c]) * b
            acc = term if acc is None else acc + term
        return acc
    merged = _rowmap(merge, [(pg, 0, D_MODEL), (pg, 1, D_MODEL), (pg, 2, D_MODEL)] + br,
                     [lw['b_merge_gate'].reshape(1, -1)], [(D_MODEL, BF16)], tl=256, name="mix_merge")[0]
    x_out = _mm(merged, lw['w_out'], add=x, name="mix_out")
    return x_out, (x, h, pg, ys, (s_s5, s_gla, s_att), br, merged)


def _mixer_bwd(saved, lw, dx_out):
    x, h, pg, ys, (s_s5, s_gla, s_att), br, merged = saved
    grads = {'w_out': _mm(merged, dx_out, ta=True, out_dtype=BF16, name="mix_dwout")}
    dmerged = _mm(dx_out, lw['w_out'], tb=True, name="mix_dmerged")

    def merge_bwd(g0, g1, g2, b0, b1, b2, dm, bias):
        dbr, dgp = [], []
        for g, b, c in zip((g0, g1, g2), (b0, b1, b2), _gate_cols()):
            s = _sigmoid(g + bias[:, c])
            dbr.append(dm * s)
            dgp.append(dm * b * (s * (1.0 - s)))
        dgp = jnp.concatenate(dgp, axis=1)
        return dbr[0], dbr[1], dbr[2], dgp, jnp.sum(dgp, axis=0, keepdims=True)
    d0, d1, d2, dgpre, dbias = _rowmap(
        merge_bwd, [(pg, 0, D_MODEL), (pg, 1, D_MODEL), (pg, 2, D_MODEL)] + br + [dmerged],
        [lw['b_merge_gate'].reshape(1, -1)], [(D_MODEL, BF16)] * 3 + [(GATE_WIDTH, BF16)], [(1, GATE_WIDTH)], tl=256,
        name="mix_merge_bwd")
    grads['b_merge_gate'] = dbias[0]
    dys = []
    for y, dbr, n in zip(ys, (d0, d1, d2), ('w_branch_s5', 'w_branch_gla', 'w_branch_attn')):
        grads[n] = _mm(y, dbr, ta=True, out_dtype=BF16, name="mix_dwbranch")
        dys.append(_mm(dbr, lw[n], tb=True, name="mix_dy"))
    du, g_s5 = _s5_bwd(pg, lw['s5'], lw['s5_w_glu'], s_s5, dys[0])
    dgq, dgk, dgv, dgg, dz, g_gla = _gla_branch_bwd(pg, lw['gla_w_alpha'], lw['gla_b_alpha'], lw['gla_norm'], s_gla, dys[1])
    daq, dak, dav, g_att = _attn_branch_bwd(pg, lw['attn_q_norm'], lw['attn_k_norm'], s_att, dys[2])

    def assemble(dgp, u0, u1, u2, q0, q1, k0, k1, v0, v1, gg, aq, ak, av, z):
        pad = jnp.zeros((dgp.shape[0], IN_PAD - 3456), F32)
        parts = [dgp.astype(F32), u0 + u1 + u2, q0 + q1, k0 + k1, v0 + v1, gg, aq, ak, av, z, pad]
        return jnp.concatenate(parts, axis=1)
    dpg = _rowmap(assemble, [dgpre] + du + dgq + dgk + dgv + [dgg, daq, dak, dav, dz], [], [(PG_WIDTH, BF16)], tl=256,
                  name="mix_dpg")[0]
    grads['w_pg'] = _mm(h, dpg, ta=True, out_dtype=BF16, name="mix_dwpg")
    dh = _mm(dpg, lw['w_pg'], tb=True, name="mix_dh")
    dx, grads['mix_norm'] = _rmsnorm_bwd(x, lw['mix_norm'], dh, dx_out)
    grads['s5'], grads['gla'], grads['attn'] = g_s5, g_gla, g_att
    return dx, grads


def _loss_head(x, gain, target):
    width = x.shape[1]

    def fn(xv, tv, g):
        r = _rms(xv)
        err = xv * r * g - tv
        dy = err * (1.0 / width)
        gd = dy * g
        dx = r * gd - xv * (r * r * r) * jnp.mean(xv * gd, axis=-1, keepdims=True)
        loss = jnp.sum(0.5 * jnp.mean(err * err, axis=-1, keepdims=True), axis=0, keepdims=True)
        return dx, jnp.broadcast_to(loss, (1, 128)), jnp.sum(dy * xv * r, axis=0, keepdims=True)
    dx, loss, dgain = _rowmap(fn, [x, target], [gain.reshape(1, -1)], [(width, F32)], [(1, 128), (1, width)], tl=256,
                              name="loss_head")
    return loss[0, 0], dx, dgain[0]


def _row_tile(rows, cap=256):
    for t in range(cap - cap % 16, 0, -16):
        if rows % t == 0:
            return t
    return rows


def _reduce_adamw(parts, w, m, v, *, name):
    nparts, r, c = parts.shape
    tr = _row_tile(r)

    def body(p_ref, w_ref, m_ref, v_ref, g_ref, d_ref, m2_ref, v2_ref):
        g = p_ref[0].astype(F32)
        for j in range(1, nparts):
            g = g + p_ref[j].astype(F32)
        m2 = ADAM_B1 * m_ref[...] + (1.0 - ADAM_B1) * g
        v2 = ADAM_B2 * v_ref[...] + (1.0 - ADAM_B2) * (g * g)
        m_hat = m2 / (1.0 - ADAM_B1 ** ADAM_STEP)
        v_hat = v2 / (1.0 - ADAM_B2 ** ADAM_STEP)
        g_ref[...] = g
        d_ref[...] = -ADAM_LR * (m_hat / (jnp.sqrt(v_hat) + ADAM_EPS) + ADAM_WD * w_ref[...])
        m2_ref[...] = m2
        v2_ref[...] = v2

    flat = pl.BlockSpec((tr, c), lambda i: (i, 0))
    return pl.pallas_call(
        body, name=name, grid=(r // tr,), in_specs=[pl.BlockSpec((nparts, tr, c), lambda i: (0, i, 0)), flat, flat, flat],
        out_specs=[flat] * 4, out_shape=[jax.ShapeDtypeStruct((r, c), F32)] * 4,
        compiler_params=pltpu.CompilerParams(dimension_semantics=("parallel",)),
    )(parts, w, m, v)


def _all_gather(blocks, *, name):
    n = len(blocks)

    def body(*refs):
        x_refs, out_refs = refs[:n], refs[n:2 * n]
        send_sems, recv_sems, local_sems = refs[2 * n:]
        x, y, c = lax.axis_index("x"), lax.axis_index("y"), lax.axis_index("c")
        me, sibling = (x, y, c), (x, y, 1 - c)
        chips = [(1 - x, y), (x, 1 - y), (1 - x, 1 - y)]

        def slot(t, px, py, pc):
            return out_refs[t].at[4 * px + 2 * py + pc]

        def copy(t, k, blk, to, own=False):
            return pltpu.make_async_remote_copy(
                src_ref=x_refs[t] if own else slot(t, *blk), dst_ref=slot(t, *blk), send_sem=send_sems.at[t, k],
                recv_sem=recv_sems.at[t, k], device_id=to, device_id_type=pl.DeviceIdType.MESH)

        mine = [pltpu.make_async_copy(x_refs[t], slot(t, *me), local_sems.at[t]) for t in range(n)]
        for cp in mine:
            cp.start()
        first = []
        for t in range(n):
            first.append(copy(t, 0, me, sibling, own=True))
            first += [copy(t, 1 + j, me, (*chip, c), own=True) for j, chip in enumerate(chips)]
        for cp in first:
            cp.start()
        passed = []
        for j, chip in enumerate(chips):
            for t in range(n):
                copy(t, 1 + j, (*chip, c), me).wait_recv()
                passed.append(copy(t, 4 + j, (*chip, c), sibling))
                passed[-1].start()
        for t in range(n):
            copy(t, 0, sibling, me).wait_recv()
        for j, chip in enumerate(chips):
            for t in range(n):
                copy(t, 4 + j, (*chip, 1 - c), me).wait_recv()
        for cp in first + passed:
            cp.wait_send()
        for cp in mine:
            cp.wait()

    hbm = pl.BlockSpec(memory_space=pl.ANY)
    return pl.pallas_call(
        body, name=name, out_shape=[jax.ShapeDtypeStruct((N_DEV,) + b.shape, b.dtype) for b in blocks],
        in_specs=[hbm] * n, out_specs=[hbm] * n,
        scratch_shapes=[pltpu.SemaphoreType.DMA((n, 7)), pltpu.SemaphoreType.DMA((n, 7)), pltpu.SemaphoreType.DMA((n,))],
    )(*blocks)


N_CHIP = N_DEV // 2


def _swap_with_sibling(arrays, *, name):
    n = len(arrays)

    def body(*refs):
        src_refs, out_refs = refs[:n], refs[n:2 * n]
        send_sems, recv_sems = refs[2 * n:]
        sibling = (lax.axis_index("x"), lax.axis_index("y"), 1 - lax.axis_index("c"))
        copies = [pltpu.make_async_remote_copy(
            src_ref=src_refs[t], dst_ref=out_refs[t], send_sem=send_sems.at[t], recv_sem=recv_sems.at[t],
            device_id=sibling, device_id_type=pl.DeviceIdType.MESH) for t in range(n)]
        for cp in copies:
            cp.start()
        for cp in copies:
            cp.wait()

    hbm = pl.BlockSpec(memory_space=pl.ANY)
    return pl.pallas_call(
        body, name=name, out_shape=[jax.ShapeDtypeStruct(a.shape, a.dtype) for a in arrays],
        in_specs=[hbm] * n, out_specs=[hbm] * n,
        scratch_shapes=[pltpu.SemaphoreType.DMA((n,)), pltpu.SemaphoreType.DMA((n,))],
    )(*arrays)


def _exchange_chips(stacks, *, name):
    n = len(stacks)

    def body(*refs):
        g_refs, out_refs = refs[:n], refs[n:2 * n]
        send_sems, recv_sems, local_sems = refs[2 * n:]
        x, y, c = lax.axis_index("x"), lax.axis_index("y"), lax.axis_index("c")
        me = 2 * x + y
        mine = [pltpu.make_async_copy(g_refs[t].at[me], out_refs[t].at[me], local_sems.at[t]) for t in range(n)]
        for cp in mine:
            cp.start()
        copies = []
        for k in range(1, N_CHIP):
            px, py = x ^ (k >> 1 & 1), y ^ (k & 1)
            for t in range(n):
                copies.append(pltpu.make_async_remote_copy(
                    src_ref=g_refs[t].at[2 * px + py], dst_ref=out_refs[t].at[me], send_sem=send_sems.at[t, k - 1],
                    recv_sem=recv_sems.at[t, k - 1], device_id=(px, py, c), device_id_type=pl.DeviceIdType.MESH))
        for cp in copies:
            cp.start()
        for cp in copies:
            cp.wait_recv()
        for cp in copies:
            cp.wait_send()
        for cp in mine:
            cp.wait()

    hbm = pl.BlockSpec(memory_space=pl.ANY)
    return pl.pallas_call(
        body, name=name, out_shape=[jax.ShapeDtypeStruct(s.shape, s.dtype) for s in stacks],
        in_specs=[hbm] * n, out_specs=[hbm] * n,
        scratch_shapes=[pltpu.SemaphoreType.DMA((n, N_CHIP - 1)), pltpu.SemaphoreType.DMA((n, N_CHIP - 1)),
                        pltpu.SemaphoreType.DMA((n,))],
    )(*stacks)


def _pair_sum(a, b):
    return _rowmap(lambda u, v: u.astype(F32) + v.astype(F32), [a, b], [], [(a.shape[1], BF16)], tl=_row_tile(a.shape[0], 512),
                   name="pair_sum")[0]


SMALL_COLS = 128


def _pack_small(arrays):
    flat = jnp.concatenate([a.astype(F32).reshape(-1, SMALL_COLS) for a in arrays], axis=0)
    return jnp.pad(flat, ((0, -flat.shape[0] % 256), (0, 0)))


def _unpack_small(packed, shapes):
    out, off = [], 0
    for s in shapes:
        r = math.prod(s) // SMALL_COLS
        out.append(packed[off:off + r].reshape(s))
        off += r
    return out


def _split_shards(full, axis):
    shape = full.shape
    split = full.reshape(shape[:axis] + (N_DEV, shape[axis] // N_DEV) + shape[axis + 1:])
    return jnp.moveaxis(split, axis, 0)


def _join_shards(stack, axis):
    moved = jnp.moveaxis(stack, 0, axis)
    shape = moved.shape
    return moved.reshape(shape[:axis] + (shape[axis] * shape[axis + 1],) + shape[axis + 2:])


def _w_in_padded(w_in):
    pad = jnp.zeros(w_in.shape[:-1] + (IN_PAD - IN_WIDTH,), w_in.dtype)
    return jnp.concatenate([w_in[..., :2560], w_in[..., 2592:], w_in[..., 2560:2592], pad], axis=-1)


def _w_in_unpadded(w):
    return jnp.concatenate([w[..., :2560], w[..., 3328:3360], w[..., 2560:3328]], axis=-1)


def _layer_weights(full, w, s5, i):
    lw = {n: w[n][i] for n in ('ffn1_norm', 'mix_norm', 'gla_norm', 'attn_q_norm', 'attn_k_norm', 'b_merge_gate', 'ffn2_norm')}
    lw['s5'] = {'b_mat': s5['b_mat'][i], 'c_mat': s5['c_mat'][i], 'tabs': [t[i] for t in s5['tabs']],
                'tabs_adj': [t[i] for t in s5['tabs_adj']], 'd': w['s5_d'][i]}
    for f in ('ffn1', 'ffn2'):
        for part in ('_w_gate', '_w_up', '_w_down'):
            lw[f + part] = full[f + part][i]
    lw['w_pg'] = jnp.concatenate([full['w_merge_gate'][i], _w_in_padded(full['w_in'][i])], axis=1)
    for n in ('s5_w_glu', 'gla_w_alpha', 'w_branch_s5', 'w_branch_gla', 'w_branch_attn', 'w_out'):
        lw[n] = full[n][i]
    lw['gla_b_alpha'] = full['gla_b_alpha'][i].astype(F32)
    return lw


def _step_local(x, target, w, full):
    s5 = _s5_setup(w)
    lws = [_layer_weights(full, w, s5, i) for i in range(DEPTH)]
    saved = []
    for lw in lws:
        x, s1 = _ffn_fwd(x, lw['ffn1_norm'], lw['ffn1_w_gate'], lw['ffn1_w_up'], lw['ffn1_w_down'])
        x, s2 = _mixer_fwd(x, lw)
        x, s3 = _ffn_fwd(x, lw['ffn2_norm'], lw['ffn2_w_gate'], lw['ffn2_w_up'], lw['ffn2_w_down'])
        saved.append((s1, s2, s3))
    loss, dx, d_final = _loss_head(x, w['final_norm'], target)
    per_layer = []
    for lw, (s1, s2, s3) in reversed(list(zip(lws, saved))):
        g = {}
        dx, g['ffn2_norm'], g['ffn2_w_gate'], g['ffn2_w_up'], g['ffn2_w_down'] = _ffn_bwd(
            s3, lw['ffn2_norm'], lw['ffn2_w_gate'], lw['ffn2_w_up'], lw['ffn2_w_down'], dx)
        dx, gm = _mixer_bwd(s2, lw, dx)
        dx, g['ffn1_norm'], g['ffn1_w_gate'], g['ffn1_w_up'], g['ffn1_w_down'] = _ffn_bwd(
            s1, lw['ffn1_norm'], lw['ffn1_w_gate'], lw['ffn1_w_up'], lw['ffn1_w_down'], dx)
        g['w_merge_gate'] = gm['w_pg'][:, :GATE_WIDTH]
        g['w_in'] = _w_in_unpadded(gm['w_pg'][:, GATE_WIDTH:])
        for n in ('w_out', 'b_merge_gate', 'w_branch_s5', 'w_branch_gla', 'w_branch_attn', 'mix_norm'):
            g[n] = gm[n]
        g['s5_d'], g['s5_w_glu'], g['s5_raw'] = gm['s5']['d'], gm['s5']['w_glu'], gm['s5']['raw']
        g['gla_w_alpha'], g['gla_b_alpha'], g['gla_norm'] = gm['gla']['w_alpha'], gm['gla']['b_alpha'], gm['gla']['norm']
        g['attn_q_norm'], g['attn_k_norm'] = gm['attn']['q_norm'], gm['attn']['k_norm']
        per_layer.append(g)
    per_layer.reverse()
    stacked = _s5_param_grads(w, [g['s5_raw'] for g in per_layer])
    stacked['final_norm'] = d_final
    return loss, dx, per_layer, stacked


def kernel(x, ffn1_norm, ffn1_w_gate, ffn1_w_up, ffn1_w_down, mix_norm, w_in, s5_lambda_re, s5_lambda_im, s5_log_dt, s5_b_re, s5_b_im, s5_c_re, s5_c_im, s5_d, s5_w_glu, gla_w_alpha, gla_b_alpha, gla_norm, attn_q_norm, attn_k_norm, w_branch_s5, w_branch_gla, w_branch_attn, w_merge_gate, b_merge_gate, w_out, ffn2_norm, ffn2_w_gate, ffn2_w_up, ffn2_w_down, final_norm, loss_target, m_ffn1_norm, m_ffn1_w_gate, m_ffn1_w_up, m_ffn1_w_down, m_mix_norm, m_w_in, m_s5_lambda_re, m_s5_lambda_im, m_s5_log_dt, m_s5_b_re, m_s5_b_im, m_s5_c_re, m_s5_c_im, m_s5_d, m_s5_w_glu, m_gla_w_alpha, m_gla_b_alpha, m_gla_norm, m_attn_q_norm, m_attn_k_norm, m_w_branch_s5, m_w_branch_gla, m_w_branch_attn, m_w_merge_gate, m_b_merge_gate, m_w_out, m_ffn2_norm, m_ffn2_w_gate, m_ffn2_w_up, m_ffn2_w_down, m_final_norm, v_ffn1_norm, v_ffn1_w_gate, v_ffn1_w_up, v_ffn1_w_down, v_mix_norm, v_w_in, v_s5_lambda_re, v_s5_lambda_im, v_s5_log_dt, v_s5_b_re, v_s5_b_im, v_s5_c_re, v_s5_c_im, v_s5_d, v_s5_w_glu, v_gla_w_alpha, v_gla_b_alpha, v_gla_norm, v_attn_q_norm, v_attn_k_norm, v_w_branch_s5, v_w_branch_gla, v_w_branch_attn, v_w_merge_gate, v_b_merge_gate, v_w_out, v_ffn2_norm, v_ffn2_w_gate, v_ffn2_w_up, v_ffn2_w_down, v_final_norm):
    return _train_step(x, ffn1_norm, ffn1_w_gate, ffn1_w_up, ffn1_w_down, mix_norm, w_in, s5_lambda_re, s5_lambda_im, s5_log_dt, s5_b_re, s5_b_im, s5_c_re, s5_c_im, s5_d, s5_w_glu, gla_w_alpha, gla_b_alpha, gla_norm, attn_q_norm, attn_k_norm, w_branch_s5, w_branch_gla, w_branch_attn, w_merge_gate, b_merge_gate, w_out, ffn2_norm, ffn2_w_gate, ffn2_w_up, ffn2_w_down, final_norm, loss_target, m_ffn1_norm, m_ffn1_w_gate, m_ffn1_w_up, m_ffn1_w_down, m_mix_norm, m_w_in, m_s5_lambda_re, m_s5_lambda_im, m_s5_log_dt, m_s5_b_re, m_s5_b_im, m_s5_c_re, m_s5_c_im, m_s5_d, m_s5_w_glu, m_gla_w_alpha, m_gla_b_alpha, m_gla_norm, m_attn_q_norm, m_attn_k_norm, m_w_branch_s5, m_w_branch_gla, m_w_branch_attn, m_w_merge_gate, m_b_merge_gate, m_w_out, m_ffn2_norm, m_ffn2_w_gate, m_ffn2_w_up, m_ffn2_w_down, m_final_norm, v_ffn1_norm, v_ffn1_w_gate, v_ffn1_w_up, v_ffn1_w_down, v_mix_norm, v_w_in, v_s5_lambda_re, v_s5_lambda_im, v_s5_log_dt, v_s5_b_re, v_s5_b_im, v_s5_c_re, v_s5_c_im, v_s5_d, v_s5_w_glu, v_gla_w_alpha, v_gla_b_alpha, v_gla_norm, v_attn_q_norm, v_attn_k_norm, v_w_branch_s5, v_w_branch_gla, v_w_branch_attn, v_w_merge_gate, v_b_merge_gate, v_w_out, v_ffn2_norm, v_ffn2_w_gate, v_ffn2_w_up, v_ffn2_w_down, v_final_norm)


def _train_step(*args):
    nw = len(W_NAMES)
    x, target = args[0][0], args[1 + nw][0]
    w = dict(zip(W_NAMES, args[1:1 + nw]))
    m = dict(zip(W_NAMES, args[2 + nw:2 + 2 * nw]))
    v = dict(zip(W_NAMES, args[2 + 2 * nw:2 + 3 * nw]))

    gathered = _all_gather([w[n].astype(BF16) for n in SHARDED], name="gather_weights")
    full = {n: _join_shards(g, SHARD_AXIS[n]) for n, g in zip(SHARDED, gathered)}

    loss, dx, per_layer, stacked = _step_local(x, target, w, full)
    loss = lax.psum(loss, ("x", "y", "c"))

    out = {}
    kinds = ('grad', 'delta', 'new_m', 'new_v')
    core = lax.axis_index("c")
    own, for_sibling = [], []
    for n in SHARDED:
        by_owner = jnp.stack([_split_shards(g[n], SHARD_AXIS[n] - 1) for g in per_layer], axis=1).astype(BF16)
        by_owner = by_owner.reshape((N_CHIP, 2) + by_owner.shape[1:])
        own.append(lax.dynamic_index_in_dim(by_owner, core, axis=1, keepdims=False))
        for_sibling.append(lax.dynamic_index_in_dim(by_owner, 1 - core, axis=1, keepdims=False))
    from_sibling = _swap_with_sibling(for_sibling, name="exchange_grads_sibling")
    chip_sums = [_pair_sum(a.reshape(-1, a.shape[-1]), b.reshape(-1, b.shape[-1])).reshape(a.shape)
                 for a, b in zip(own, from_sibling)]
    incoming = _exchange_chips(chip_sums, name="exchange_grads_chips")
    for n, parts in zip(SHARDED, incoming):
        shape = w[n].shape
        flat = lambda a: a.reshape(-1, shape[-1])
        res = _reduce_adamw(parts.reshape(N_CHIP, -1, shape[-1]), flat(w[n]), flat(m[n]), flat(v[n]), name="adamw_sharded")
        for kind, a in zip(kinds, res):
            out[kind + '_' + n] = a.reshape(shape)
    small = [stacked[n] if n in stacked else jnp.stack([g[n] for g in per_layer]) for n in REPLICATED]
    parts = _all_gather([_pack_small(small)], name="gather_small_grads")[0]
    res = _reduce_adamw(parts, *[_pack_small([d[n] for n in REPLICATED]) for d in (w, m, v)], name="adamw_replicated")
    for kind, packed in zip(kinds, res):
        for n, a in zip(REPLICATED, _unpack_small(packed, [w[n].shape for n in REPLICATED])):
            out[kind + '_' + n] = a
    return (loss, dx[None]) + tuple(out[kind + '_' + n] for kind in kinds for n in W_NAMES)
```

```python
import functools
import math

import jax
import jax.numpy as jnp
import numpy as np
from jax import lax
from jax.experimental import pallas as pl
from jax.experimental.pallas import tpu as pltpu

F32 = jnp.float32
BF16 = jnp.bfloat16

N_DEV = 8
D_MODEL = 1024
DEPTH = 2
GRID_W = 64
D_FF = 2816
NORM_EPS = 1e-6
S5_GROUPS = 32
S5_GROUP_CH = 16
S5_STATE = 64
S5_WIDTH = 512
S5_NSTATE = S5_GROUPS * S5_STATE
S5_LANE_BLOCK = 512
GLA_HEADS = 4
GLA_HEAD_DIM = 128
GLA_WIDTH = 512
GLA_LOWRANK = 16
GLA_TAU = 16.0
GLA_CHUNK = 64
ATTN_Q_HEADS = 8
ATTN_KV_HEADS = 2
ATTN_HEAD_DIM = 64
ATTN_WIDTH = 512
ATTN_KV_WIDTH = 128
ROPE_BASE = 10000.0
IN_SPLITS = (512, 512, 512, 512, 512, 16, 16, 512, 128, 128)
IN_WIDTH = sum(IN_SPLITS)
IN_PAD = 3584
GATE_WIDTH = 3 * D_MODEL
PG_WIDTH = GATE_WIDTH + IN_PAD
P_OFF = GATE_WIDTH
CB_U, CB_GQ, CB_GK, CB_GV, CB_GG, CB_AQ = (P_OFF // 512 + i for i in range(6))
CB_AK, CB_AV, CB_Z = (P_OFF + 3072) // 128, (P_OFF + 3200) // 128, (P_OFF + 3328) // 128
ADAM_LR = 0.001
ADAM_B1 = 0.9
ADAM_B2 = 0.999
ADAM_EPS = 1e-08
ADAM_WD = 0.01
ADAM_STEP = 10
PACK_COLS = 1024

W_NAMES = ['ffn1_norm', 'ffn1_w_gate', 'ffn1_w_up', 'ffn1_w_down', 'mix_norm', 'w_in', 's5_lambda_re', 's5_lambda_im',
           's5_log_dt', 's5_b_re', 's5_b_im', 's5_c_re', 's5_c_im', 's5_d', 's5_w_glu', 'gla_w_alpha', 'gla_b_alpha',
           'gla_norm', 'attn_q_norm', 'attn_k_norm', 'w_branch_s5', 'w_branch_gla', 'w_branch_attn', 'w_merge_gate',
           'b_merge_gate', 'w_out', 'ffn2_norm', 'ffn2_w_gate', 'ffn2_w_up', 'ffn2_w_down', 'final_norm']
SHARD_AXIS = {'ffn1_w_gate': 2, 'ffn1_w_up': 2, 'ffn1_w_down': 1, 'w_in': 2, 's5_w_glu': 1, 'gla_w_alpha': 3,
              'gla_b_alpha': 2, 'w_branch_s5': 2, 'w_branch_gla': 2, 'w_branch_attn': 2, 'w_merge_gate': 2,
              'w_out': 1, 'ffn2_w_gate': 2, 'ffn2_w_up': 2, 'ffn2_w_down': 1}
SHARDED = [n for n in W_NAMES if n in SHARD_AXIS]
REPLICATED = [n for n in W_NAMES if n not in SHARD_AXIS]


def _pick(dim, prefs):
    for p in prefs:
        if dim % p == 0:
            return p
    return dim


def _sigmoid(x):
    return 0.5 * jnp.tanh(0.5 * x) + 0.5


def _mm(a, b, *, ta=False, tb=False, out_dtype=F32, scale=None, add=None, side=None, name):
    a, a_cb, a_w = a if isinstance(a, tuple) else (a, 0, a.shape[1])
    b, b_cb, b_w = b if isinstance(b, tuple) else (b, 0, b.shape[1])
    m, k = (a_w, a.shape[0]) if ta else (a.shape[0], a_w)
    n = b.shape[0] if tb else b_w
    assert (b_w if tb else b.shape[0]) == k, (a.shape, b.shape, ta, tb)
    tm, tn, tk = _mm_tiles(m, n, k, a.dtype.itemsize, b.dtype.itemsize, jnp.dtype(out_dtype).itemsize)
    nk = k // tk
    dims = (((0 if ta else 1,), (1 if tb else 0,)), ((), ()))
    a_off = a_cb * (a_w // (tm if ta else tk))
    b_off = b_cb * (b_w // (tk if tb else tn))

    grid = (m // tm, n // tn, nk)
    n_in = 2 if add is None else 3

    def body(*refs):
        _carried(side, grid, refs, n_in, 1, int(nk > 1), compute)

    def compute(refs):
        a_ref, b_ref, *rest = refs
        add_ref = rest[0] if add is not None else None
        o_ref, *acc = rest[1:] if add is not None else rest

        def finish(res):
            res = res if scale is None else res * scale
            return (res if add_ref is None else res + add_ref[...]).astype(out_dtype)

        part = lax.dot_general(a_ref[...].astype(BF16), b_ref[...].astype(BF16), dims, preferred_element_type=F32)
        if nk == 1:
            o_ref[...] = finish(part)
            return
        acc_ref, = acc
        kk = pl.program_id(2)

        @pl.when(kk == 0)
        def _():
            acc_ref[...] = part

        @pl.when(kk > 0)
        def _():
            acc_ref[...] += part

        @pl.when(kk == nk - 1)
        def _():
            o_ref[...] = finish(acc_ref[...])

    a_spec = (pl.BlockSpec((tk, tm), lambda i, j, kk: (kk, i + a_off)) if ta
              else pl.BlockSpec((tm, tk), lambda i, j, kk: (i, kk + a_off)))
    b_spec = (pl.BlockSpec((tn, tk), lambda i, j, kk: (j, kk + b_off)) if tb
              else pl.BlockSpec((tk, tn), lambda i, j, kk: (kk, j + b_off)))
    o_spec = pl.BlockSpec((tm, tn), lambda i, j, kk: (i, j))
    (out,), gathered = _side_call(
        body, side, name=name, grid=grid, in_specs=[a_spec, b_spec] + ([o_spec] if add is not None else []),
        out_specs=[o_spec], out_shape=[jax.ShapeDtypeStruct((m, n), out_dtype)],
        scratch=[pltpu.VMEM((tm, tn), F32)] if nk > 1 else [], args=[a, b] + ([add] if add is not None else []),
        semantics=("parallel", "parallel", "arbitrary"))
    return out if side is None else (out, gathered)


MM_VMEM_BUDGET = 40 * 1024 * 1024


def _mm_tiles(m, n, k, a_bytes, b_bytes, out_bytes):
    tms = [t for t in (1024, 1408, 512, 256, 128) if m % t == 0] or [m]
    tns = [t for t in (512, 1408, 256, 128) if n % t == 0] or [n]
    tks = [k] + [t for t in (2048, 1024, 512, 256, 128) if k % t == 0 and t < k]
    for tk in tks:
        for tm in tms:
            for tn in tns:
                use = 2 * (tm * tk * a_bytes + tk * tn * b_bytes + tm * tn * out_bytes) + 2 * tm * tn * 4
                if use <= MM_VMEM_BUDGET:
                    return tm, tn, tk
    return tms[-1], tns[-1], tks[-1]


def _rowmap(fn, rows, consts, outs, reds=(), *, tl, name):
    rows = [r if isinstance(r, tuple) else (r, 0, r.shape[1]) for r in rows]
    length = rows[0][0].shape[0]
    tl = min(tl, length)
    nr, nc, no = len(rows), len(consts), len(outs)

    def body(*refs):
        res = fn(*[r[...] for r in refs[:nr + nc]])
        res = res if isinstance(res, tuple) else (res,)
        for o_ref, val in zip(refs[nr + nc:nr + nc + no], res[:no]):
            o_ref[...] = val.astype(o_ref.dtype)
        if reds:
            step = pl.program_id(0)
            red_refs = refs[nr + nc + no:]

            @pl.when(step == 0)
            def _():
                for d_ref, val in zip(red_refs, res[no:]):
                    d_ref[...] = val.astype(F32)

            @pl.when(step > 0)
            def _():
                for d_ref, val in zip(red_refs, res[no:]):
                    d_ref[...] += val.astype(F32)

    in_specs = [pl.BlockSpec((tl, w), lambda i, cb=cb: (i, cb)) for (_, cb, w) in rows]
    in_specs += [pl.BlockSpec(c.shape, lambda i, nd=c.ndim: (0,) * nd) for c in consts]
    out_specs = [pl.BlockSpec((tl, w), lambda i: (i, 0)) for (w, _) in outs]
    out_specs += [pl.BlockSpec(s, lambda i, nd=len(s): (0,) * nd) for s in reds]
    out_shape = [jax.ShapeDtypeStruct((length, w), dt) for (w, dt) in outs]
    out_shape += [jax.ShapeDtypeStruct(s, F32) for s in reds]
    res = pl.pallas_call(
        body, name=name, grid=(length // tl,), in_specs=in_specs, out_specs=out_specs, out_shape=out_shape,
        compiler_params=pltpu.CompilerParams(dimension_semantics=("arbitrary" if reds else "parallel",)),
    )(*[r[0] for r in rows], *consts)
    return res


def _rms(x):
    return lax.rsqrt(jnp.mean(x * x, axis=-1, keepdims=True) + NORM_EPS)


def _rmsnorm_fwd(x, gain):
    def fn(xv, g):
        return xv * _rms(xv) * g
    return _rowmap(fn, [x], [gain.reshape(1, -1)], [(x.shape[1], BF16)], tl=256, name="rmsnorm_fwd")[0]


def _rmsnorm_bwd(x, gain, dh, dres):
    def fn(xv, dhv, drv, g):
        r = _rms(xv)
        gd = dhv * g
        dx = r * gd - xv * (r * r * r) * jnp.mean(xv * gd, axis=-1, keepdims=True)
        return drv + dx, jnp.sum(dhv * xv * r, axis=0, keepdims=True)
    dx, dg = _rowmap(fn, [x, dh, dres], [gain.reshape(1, -1)], [(x.shape[1], F32)], [(1, x.shape[1])], tl=256,
                     name="rmsnorm_bwd")
    return dx, dg[0]


FFN_UNIT = D_FF // 2


def _side_call(body, side, *, name, grid, in_specs, out_specs, out_shape, scratch, args, semantics):
    if side is not None:
        in_specs, out_specs = in_specs + side.in_specs, out_specs + side.out_specs
        out_shape, scratch, args = out_shape + side.out_shape, scratch + side.scratch, list(args) + side.blocks
        semantics = ("arbitrary",) * len(grid)
    res = pl.pallas_call(body, name=name, grid=grid, in_specs=in_specs, out_specs=out_specs, out_shape=out_shape,
                         scratch_shapes=scratch, compiler_params=pltpu.CompilerParams(dimension_semantics=semantics))(*args)
    n_own = len(res) - (side.n if side is not None else 0)
    return res[:n_own], res[n_own:]


def _ffn_up(h, w_gate, w_up, side=None):
    length, k = h.shape
    tm = _pick(length, (512, 256, 128))
    grid = (D_FF // FFN_UNIT, length // tm)

    def compute(refs):
        h_ref, wg_ref, wu_ref, a_ref, g_ref, u_ref = refs
        hv = h_ref[...]
        g = jnp.dot(hv, wg_ref[...], preferred_element_type=F32)
        u = jnp.dot(hv, wu_ref[...], preferred_element_type=F32)
        a_ref[...] = (g * _sigmoid(g) * u).astype(BF16)
        g_ref[...] = g.astype(BF16)
        u_ref[...] = u.astype(BF16)

    def body(*refs):
        _carried(side, grid, refs, 3, 3, 0, compute)

    w_spec = pl.BlockSpec((k, FFN_UNIT), lambda j, i: (0, j))
    o_spec = pl.BlockSpec((tm, FFN_UNIT), lambda j, i: (i, j))
    return _side_call(
        body, side, name="ffn_up", grid=grid, in_specs=[pl.BlockSpec((tm, k), lambda j, i: (i, 0)), w_spec, w_spec],
        out_specs=[o_spec] * 3, out_shape=[jax.ShapeDtypeStruct((length, D_FF), BF16)] * 3, scratch=[],
        args=[h, w_gate, w_up], semantics=("parallel", "parallel"))


def _ffn_dgu(dxo, w_down, g, u):
    length, k = dxo.shape
    tm = _pick(length, (512, 256, 128))

    def body(d_ref, w_ref, g_ref, u_ref, dg_ref, du_ref):
        da = 0.5 * lax.dot_general(d_ref[...], w_ref[...], _NT, preferred_element_type=F32)
        gv = g_ref[...].astype(F32)
        s = _sigmoid(gv)
        dg_ref[...] = (da * u_ref[...].astype(F32) * (s * (1.0 + gv * (1.0 - s)))).astype(BF16)
        du_ref[...] = (da * (gv * s)).astype(BF16)

    o_spec = pl.BlockSpec((tm, FFN_UNIT), lambda j, i: (i, j))
    return pl.pallas_call(
        body, name="ffn_dgu", grid=(D_FF // FFN_UNIT, length // tm),
        in_specs=[pl.BlockSpec((tm, k), lambda j, i: (i, 0)), pl.BlockSpec((FFN_UNIT, k), lambda j, i: (j, 0)), o_spec, o_spec],
        out_specs=[o_spec] * 2, out_shape=[jax.ShapeDtypeStruct((length, D_FF), BF16)] * 2,
        compiler_params=pltpu.CompilerParams(dimension_semantics=("parallel", "parallel")),
    )(dxo, w_down, g, u)


def _ffn_fwd(x, gain, w_gate, w_up, w_down, side=None):
    h = _rmsnorm_fwd(x, gain)
    (a, g, u), gathered = _ffn_up(h, w_gate, w_up, side)
    x_out = _mm(a, w_down, scale=0.5, add=x, name="ffn_down")
    return x_out, (x, h, g, u, a), gathered


def _ffn_bwd(saved, gain, w_gate, w_up, w_down, dx_out):
    x, h, g, u, a = saved
    dxo = dx_out.astype(BF16)
    d_wdown = _mm(a, dxo, ta=True, scale=0.5, out_dtype=BF16, name="ffn_dwdown")
    dg, du = _ffn_dgu(dxo, w_down, g, u)
    d_wgate = _mm(h, dg, ta=True, out_dtype=BF16, name="ffn_dwgu")
    d_wup = _mm(h, du, ta=True, out_dtype=BF16, name="ffn_dwgu")
    dh = _mm(du, w_up, tb=True, add=_mm(dg, w_gate, tb=True, name="ffn_dh"), name="ffn_dh_add")
    dx, dgain = _rmsnorm_bwd(x, gain, dh, dx_out)
    return dx, dgain, d_wgate, d_wup, d_wdown


def _s5_col(n):
    return (n // S5_LANE_BLOCK) * 2 * S5_LANE_BLOCK + n % S5_LANE_BLOCK


def _s5_blocked(re, im):
    lead = re.shape[:-1]
    nb = S5_NSTATE // S5_LANE_BLOCK
    both = jnp.stack([re.reshape(*lead, nb, S5_LANE_BLOCK), im.reshape(*lead, nb, S5_LANE_BLOCK)], axis=-2)
    return both.reshape(*lead, 2 * S5_NSTATE)


def _s5_unblocked(z):
    lead = z.shape[:-1]
    nb = S5_NSTATE // S5_LANE_BLOCK
    both = z.reshape(*lead, nb, 2, S5_LANE_BLOCK)
    return both[..., 0, :].reshape(*lead, S5_NSTATE), both[..., 1, :].reshape(*lead, S5_NSTATE)


def _s5_tables(a_re, a_im, reverse):
    a = lax.complex(a_re, a_im)
    a2 = a * a
    a4 = a2 * a2
    rows = jnp.arange(8)
    pw = [a]
    for _ in range(7):
        pw.append(pw[-1] * a)
    pw = jnp.stack(pw)
    if reverse:
        pw = pw[::-1]
    tabs = []
    for coef, s in ((a, 1), (a2, 2), (a4, 4)):
        live = (rows <= 7 - s) if reverse else (rows >= s)
        tabs.append(jnp.where(live[:, None], coef[None, :], 0.0))
    tabs.append(pw)
    tabs = jnp.stack(tabs)
    return _s5_blocked(jnp.real(tabs), jnp.imag(tabs))


def _s5_scan_tile(v, tab_ref, prev, reverse):
    lb = S5_LANE_BLOCK
    vr, vi = v[:, :lb], v[:, lb:]
    for idx, s in enumerate((1, 2, 4)):
        cr, ci = tab_ref[idx, :, :lb], tab_ref[idx, :, lb:]
        sh = 8 - s if reverse else s
        sr, si = pltpu.roll(vr, sh, 0), pltpu.roll(vi, sh, 0)
        vr, vi = vr + cr * sr - ci * si, vi + cr * si + ci * sr
    row = 0 if reverse else 7
    pr = jnp.broadcast_to(prev[row:row + 1, :lb], (8, lb))
    pi = jnp.broadcast_to(prev[row:row + 1, lb:], (8, lb))
    cr, ci = tab_ref[3, :, :lb], tab_ref[3, :, lb:]
    return jnp.concatenate([vr + cr * pr - ci * pi, vi + cr * pi + ci * pr], axis=1)


def _s5_scan(v, tabs, reverse, *, name):
    length = v.shape[0]
    tb = min(512, length)
    ntb = length // tb
    nlb = S5_NSTATE // S5_LANE_BLOCK
    wb = 2 * S5_LANE_BLOCK
    ntile = tb // 8

    def body(tab_ref, v_ref, x_ref, carry_ref):
        @pl.when(pl.program_id(1) == 0)
        def _():
            carry_ref[...] = jnp.zeros_like(carry_ref)

        def step(i, prev):
            r0 = pl.multiple_of((ntile - 1 - i if reverse else i) * 8, 8)
            x = _s5_scan_tile(v_ref[pl.ds(r0, 8), :], tab_ref, prev, reverse)
            x_ref[pl.ds(r0, 8), :] = x
            return x

        carry_ref[...] = lax.fori_loop(0, ntile, step, carry_ref[...])

    tmap = (lambda c, t: (ntb - 1 - t, c)) if reverse else (lambda c, t: (t, c))
    return pl.pallas_call(
        body, name=name, grid=(nlb, ntb),
        in_specs=[pl.BlockSpec((4, 8, wb), lambda c, t: (0, 0, c)), pl.BlockSpec((tb, wb), tmap)],
        out_specs=pl.BlockSpec((tb, wb), tmap), out_shape=jax.ShapeDtypeStruct(v.shape, F32),
        scratch_shapes=[pltpu.VMEM((8, wb), F32)],
        compiler_params=pltpu.CompilerParams(dimension_semantics=("parallel", "arbitrary")),
    )(tabs, v)


def _s5_scan_adjoint(g, xs, tabs_conj, reverse, *, name):
    length = g.shape[0]
    tb = min(512, length)
    ntb = length // tb
    nlb = S5_NSTATE // S5_LANE_BLOCK
    lb = S5_LANE_BLOCK
    wb = 2 * lb
    ntile = tb // 8
    adj_rev = not reverse
    if reverse:
        edge = jnp.concatenate([xs[tb::tb], jnp.zeros((1, xs.shape[1]), F32)], axis=0)
    else:
        edge = jnp.concatenate([jnp.zeros((1, xs.shape[1]), F32), xs[tb - 1:length - 1:tb]], axis=0)
    edge = edge.reshape(ntb, 1, xs.shape[1])

    def body(tab_ref, g_ref, x_ref, edge_ref, lam_ref, da_ref, carry_ref):
        @pl.when(pl.program_id(1) == 0)
        def _():
            carry_ref[...] = jnp.zeros_like(carry_ref)
            da_ref[...] = jnp.zeros_like(da_ref)

        rows = lax.broadcasted_iota(jnp.int32, (8, wb), 0)

        def step(i, carry):
            prev, acc = carry
            k = ntile - 1 - i if adj_rev else i
            r0 = pl.multiple_of(k * 8, 8)
            lam = _s5_scan_tile(g_ref[pl.ds(r0, 8), :], tab_ref, prev, adj_rev)
            lam_ref[pl.ds(r0, 8), :] = lam
            x = x_ref[pl.ds(r0, 8), :]
            if reverse:
                kn = jnp.minimum(k + 1, ntile - 1)
                nb = x_ref[pl.ds(pl.multiple_of(kn * 8, 8), 8), :][0:1, :]
                nb = jnp.where(k == ntile - 1, edge_ref[0], nb)
                xp = jnp.where(rows == 7, jnp.broadcast_to(nb, (8, wb)), pltpu.roll(x, 7, 0))
            else:
                kn = jnp.maximum(k - 1, 0)
                nb = x_ref[pl.ds(pl.multiple_of(kn * 8, 8), 8), :][7:8, :]
                nb = jnp.where(k == 0, edge_ref[0], nb)
                xp = jnp.where(rows == 0, jnp.broadcast_to(nb, (8, wb)), pltpu.roll(x, 1, 0))
            xr, xi, lr, li = xp[:, :lb], xp[:, lb:], lam[:, :lb], lam[:, lb:]
            acc = acc + jnp.concatenate([xr * lr + xi * li, xr * li - xi * lr], axis=1)
            return lam, acc

        last, acc = lax.fori_loop(0, ntile, step, (carry_ref[...], da_ref[...]))
        carry_ref[...] = last
        da_ref[...] = acc

    tmap = (lambda c, t: (ntb - 1 - t, c)) if adj_rev else (lambda c, t: (t, c))
    emap = (lambda c, t: (ntb - 1 - t, 0, c)) if adj_rev else (lambda c, t: (t, 0, c))
    return pl.pallas_call(
        body, name=name, grid=(nlb, ntb),
        in_specs=[pl.BlockSpec((4, 8, wb), lambda c, t: (0, 0, c)), pl.BlockSpec((tb, wb), tmap),
                  pl.BlockSpec((tb, wb), tmap), pl.BlockSpec((1, 1, wb), emap)],
        out_specs=[pl.BlockSpec((tb, wb), tmap), pl.BlockSpec((8, wb), lambda c, t: (0, c))],
        out_shape=[jax.ShapeDtypeStruct(g.shape, F32), jax.ShapeDtypeStruct((8, g.shape[1]), F32)],
        scratch_shapes=[pltpu.VMEM((8, wb), F32)],
        compiler_params=pltpu.CompilerParams(dimension_semantics=("parallel", "arbitrary")),
    )(tabs_conj, g, xs, edge)


def _s5_prep(lam_re, lam_im, log_dt, b_re, b_im):
    lam = lax.complex(lam_re, lam_im)
    dt = jnp.exp(log_dt)[:, None]
    lam_bar = jnp.exp(lam * dt)
    b_bar = ((lam_bar - 1.0) / lam)[..., None] * lax.complex(b_re, b_im)
    return (jnp.real(lam_bar).reshape(-1), jnp.imag(lam_bar).reshape(-1), jnp.real(b_bar), jnp.imag(b_bar))


S5_NBLK = S5_NSTATE // S5_LANE_BLOCK
S5_BLK_GROUPS = S5_GROUPS // S5_NBLK
S5_BLK_CH = S5_BLK_GROUPS * S5_GROUP_CH


def _s5_in_matrix(bb_re, bb_im):
    eye = jnp.eye(S5_BLK_GROUPS, dtype=F32)
    def dense(bb):
        b4 = bb.reshape(S5_NBLK, S5_BLK_GROUPS, S5_STATE, S5_GROUP_CH)
        return jnp.einsum('cgph,gk->cghkp', b4, eye).reshape(S5_NBLK, S5_BLK_CH, S5_LANE_BLOCK)
    return jnp.concatenate([dense(bb_re), dense(bb_im)], axis=-1)


def _s5_block_diagonal(d):
    d5 = d.reshape(S5_NBLK, S5_BLK_GROUPS, S5_GROUP_CH, S5_BLK_GROUPS, S5_STATE)
    eye = jnp.eye(S5_BLK_GROUPS, dtype=F32)
    return jnp.swapaxes(jnp.sum(d5 * eye[None, :, None, :, None], axis=1), 1, 2)


def _s5_in_matrix_grad(d_mat):
    def diag(d):
        return jnp.swapaxes(_s5_block_diagonal(d), 2, 3).reshape(S5_GROUPS, S5_STATE, S5_GROUP_CH)
    return diag(d_mat[..., :S5_LANE_BLOCK]), diag(d_mat[..., S5_LANE_BLOCK:])


def _s5_out_matrix(c_re, c_im):
    eye = jnp.eye(S5_BLK_GROUPS, dtype=F32)
    def dense(cc):
        c4 = cc.reshape(S5_NBLK, S5_BLK_GROUPS, S5_GROUP_CH, S5_STATE)
        return jnp.einsum('cghp,gk->cgpkh', c4, eye).reshape(S5_NBLK, S5_LANE_BLOCK, S5_BLK_CH)
    return jnp.concatenate([dense(c_re), dense(-c_im)], axis=1)


def _s5_out_matrix_grad(d_mat_t):
    def diag(d):
        return _s5_block_diagonal(d).reshape(S5_GROUPS, S5_GROUP_CH, S5_STATE)
    return diag(d_mat_t[..., :S5_LANE_BLOCK]), -diag(d_mat_t[..., S5_LANE_BLOCK:])


def _gmm(a, b, *, tb=False, name):
    arr, cb0, wa = a
    nblk = b.shape[0]
    wn = b.shape[1] if tb else b.shape[2]
    length = arr.shape[0]
    tm = _pick(length, (1024, 512, 256, 128))
    dims = (((1,), (1 if tb else 0,)), ((), ()))

    def body(a_ref, b_ref, o_ref):
        o_ref[...] = lax.dot_general(a_ref[...].astype(BF16), b_ref[0].astype(BF16), dims, preferred_element_type=F32)

    return pl.pallas_call(
        body, name=name, grid=(nblk, length // tm),
        in_specs=[pl.BlockSpec((tm, wa), lambda c, i: (i, cb0 + c)), pl.BlockSpec((1,) + b.shape[1:], lambda c, i: (c, 0, 0))],
        out_specs=pl.BlockSpec((tm, wn), lambda c, i: (i, c)), out_shape=jax.ShapeDtypeStruct((length, nblk * wn), F32),
        compiler_params=pltpu.CompilerParams(dimension_semantics=("parallel", "parallel")),
    )(arr, b)


def _gmm_tn(a, g, *, nblk, name):
    arr_a, cb_a, wa = a
    arr_g, cb_g, wg = g
    length = arr_a.shape[0]
    dims = (((0,), (0,)), ((), ()))

    def body(a_ref, g_ref, o_ref):
        o_ref[0] = lax.dot_general(a_ref[...].astype(BF16), g_ref[...].astype(BF16), dims, preferred_element_type=F32)

    return pl.pallas_call(
        body, name=name, grid=(nblk,),
        in_specs=[pl.BlockSpec((length, wa), lambda c: (0, cb_a + c)), pl.BlockSpec((length, wg), lambda c: (0, cb_g + c))],
        out_specs=pl.BlockSpec((1, wa, wg), lambda c: (c, 0, 0)), out_shape=jax.ShapeDtypeStruct((nblk, wa, wg), F32),
        compiler_params=pltpu.CompilerParams(dimension_semantics=("parallel",)),
    )(arr_a, arr_g)


def _gelu_parts(x):
    k = math.sqrt(2.0 / math.pi)
    inner = k * (x + 0.044715 * x * x * x)
    th = jnp.tanh(inner)
    return th, k * (1.0 + 3.0 * 0.044715 * x * x)


S5_CB_U = CB_U * (512 // S5_BLK_CH)


def _s5_direction_fwd(pg, b_mat, c_mat, tabs, reverse, *, name):
    length = pg.shape[0]
    tb = min(512, length)
    ntb = length // tb
    wb = 2 * S5_LANE_BLOCK
    ntile = tb // 8

    def body(tab_ref, u_ref, b_ref, c_ref, x_ref, y_ref, carry_ref, bu_ref):
        @pl.when(pl.program_id(1) == 0)
        def _():
            carry_ref[...] = jnp.zeros_like(carry_ref)

        bu_ref[...] = jnp.dot(u_ref[...].astype(BF16), b_ref[0], preferred_element_type=F32)

        def step(i, prev):
            r0 = pl.multiple_of((ntile - 1 - i if reverse else i) * 8, 8)
            x = _s5_scan_tile(bu_ref[pl.ds(r0, 8), :], tab_ref, prev, reverse)
            x_ref[pl.ds(r0, 8), :] = x
            return x

        carry_ref[...] = lax.fori_loop(0, ntile, step, carry_ref[...])
        y_ref[...] = jnp.dot(x_ref[...].astype(BF16), c_ref[0], preferred_element_type=F32)

    tix = (lambda t: ntb - 1 - t) if reverse else (lambda t: t)
    return pl.pallas_call(
        body, name=name, grid=(S5_NBLK, ntb),
        in_specs=[pl.BlockSpec((4, 8, wb), lambda c, t: (0, 0, c)),
                  pl.BlockSpec((tb, S5_BLK_CH), lambda c, t: (tix(t), S5_CB_U + c)),
                  pl.BlockSpec((1, S5_BLK_CH, wb), lambda c, t: (c, 0, 0)),
                  pl.BlockSpec((1, wb, S5_BLK_CH), lambda c, t: (c, 0, 0))],
        out_specs=[pl.BlockSpec((tb, wb), lambda c, t: (tix(t), c)), pl.BlockSpec((tb, S5_BLK_CH), lambda c, t: (tix(t), c))],
        out_shape=[jax.ShapeDtypeStruct((length, S5_NBLK * wb), F32), jax.ShapeDtypeStruct((length, S5_WIDTH), F32)],
        scratch_shapes=[pltpu.VMEM((8, wb), F32), pltpu.VMEM((tb, wb), F32)],
        compiler_params=pltpu.CompilerParams(dimension_semantics=("parallel", "arbitrary")),
    )(tabs, pg, b_mat, c_mat)


def _s5_direction_bwd(pg, dy, xs, b_mat, c_mat, tabs_conj, reverse, *, name):
    length = pg.shape[0]
    tb = min(512, length)
    ntb = length // tb
    lb = S5_LANE_BLOCK
    wb = 2 * lb
    ntile = tb // 8
    adj_rev = not reverse
    if reverse:
        edge = jnp.concatenate([xs[tb::tb], jnp.zeros((1, xs.shape[1]), F32)], axis=0)
    else:
        edge = jnp.concatenate([jnp.zeros((1, xs.shape[1]), F32), xs[tb - 1:length - 1:tb]], axis=0)
    edge = edge.reshape(ntb, 1, xs.shape[1])

    def body(tab_ref, u_ref, dy_ref, x_ref, edge_ref, b_ref, c_ref, du_ref, db_ref, dc_ref, da_ref, carry_ref, g_ref, lam_ref):
        @pl.when(pl.program_id(1) == 0)
        def _():
            carry_ref[...] = jnp.zeros_like(carry_ref)
            da_ref[...] = jnp.zeros_like(da_ref)
            db_ref[...] = jnp.zeros_like(db_ref)
            dc_ref[...] = jnp.zeros_like(dc_ref)

        dyb = dy_ref[...].astype(BF16)
        g_ref[...] = lax.dot_general(dyb, c_ref[0], _NT, preferred_element_type=F32)
        rows = lax.broadcasted_iota(jnp.int32, (8, wb), 0)

        def step(i, carry):
            prev, acc = carry
            k = ntile - 1 - i if adj_rev else i
            r0 = pl.multiple_of(k * 8, 8)
            lam = _s5_scan_tile(g_ref[pl.ds(r0, 8), :], tab_ref, prev, adj_rev)
            lam_ref[pl.ds(r0, 8), :] = lam
            x = x_ref[pl.ds(r0, 8), :]
            if reverse:
                kn = jnp.minimum(k + 1, ntile - 1)
                nb = x_ref[pl.ds(pl.multiple_of(kn * 8, 8), 8), :][0:1, :]
                nb = jnp.where(k == ntile - 1, edge_ref[0], nb)
                xp = jnp.where(rows == 7, jnp.broadcast_to(nb, (8, wb)), pltpu.roll(x, 7, 0))
            else:
                kn = jnp.maximum(k - 1, 0)
                nb = x_ref[pl.ds(pl.multiple_of(kn * 8, 8), 8), :][7:8, :]
                nb = jnp.where(k == 0, edge_ref[0], nb)
                xp = jnp.where(rows == 0, jnp.broadcast_to(nb, (8, wb)), pltpu.roll(x, 1, 0))
            xr, xi, lr, li = xp[:, :lb], xp[:, lb:], lam[:, :lb], lam[:, lb:]
            return lam, acc + jnp.concatenate([xr * lr + xi * li, xr * li - xi * lr], axis=1)

        last, acc = lax.fori_loop(0, ntile, step, (carry_ref[...], da_ref[...]))
        carry_ref[...] = last
        da_ref[...] = acc
        lamb = lam_ref[...].astype(BF16)
        du_ref[...] = lax.dot_general(lamb, b_ref[0], _NT, preferred_element_type=F32)
        db_ref[0] += lax.dot_general(u_ref[...].astype(BF16), lamb, _TN, preferred_element_type=F32)
        dc_ref[0] += lax.dot_general(dyb, x_ref[...].astype(BF16), _TN, preferred_element_type=F32)

    tix = (lambda t: ntb - 1 - t) if adj_rev else (lambda t: t)
    wide = pl.BlockSpec((tb, wb), lambda c, t: (tix(t), c))
    mat = pl.BlockSpec((1, S5_BLK_CH, wb), lambda c, t: (c, 0, 0))
    return pl.pallas_call(
        body, name=name, grid=(S5_NBLK, ntb),
        in_specs=[pl.BlockSpec((4, 8, wb), lambda c, t: (0, 0, c)),
                  pl.BlockSpec((tb, S5_BLK_CH), lambda c, t: (tix(t), S5_CB_U + c)),
                  pl.BlockSpec((tb, S5_BLK_CH), lambda c, t: (tix(t), c)), wide,
                  pl.BlockSpec((1, 1, wb), lambda c, t: (tix(t), 0, c)), mat,
                  pl.BlockSpec((1, wb, S5_BLK_CH), lambda c, t: (c, 0, 0))],
        out_specs=[pl.BlockSpec((tb, S5_BLK_CH), lambda c, t: (tix(t), c)), mat, mat, pl.BlockSpec((8, wb), lambda c, t: (0, c))],
        out_shape=[jax.ShapeDtypeStruct((length, S5_WIDTH), F32), jax.ShapeDtypeStruct((S5_NBLK, S5_BLK_CH, wb), F32),
                   jax.ShapeDtypeStruct((S5_NBLK, S5_BLK_CH, wb), F32), jax.ShapeDtypeStruct((8, S5_NBLK * wb), F32)],
        scratch_shapes=[pltpu.VMEM((8, wb), F32), pltpu.VMEM((tb, wb), F32), pltpu.VMEM((tb, wb), F32)],
        compiler_params=pltpu.CompilerParams(dimension_semantics=("parallel", "arbitrary")),
    )(tabs_conj, pg, dy, xs, edge, b_mat, c_mat)


def _both(fn):
    return jax.vmap(jax.vmap(fn))


def _s5_setup(w):
    a_re, a_im, bb_re, bb_im = _both(_s5_prep)(w['s5_lambda_re'], w['s5_lambda_im'], w['s5_log_dt'], w['s5_b_re'], w['s5_b_im'])

    def tables(d, conj, reverse):
        return jax.vmap(lambda r, i: _s5_tables(r, -i if conj else i, reverse))(a_re[:, d], a_im[:, d])
    return {'b_mat': _both(_s5_in_matrix)(bb_re, bb_im).astype(BF16),
            'c_mat': _both(_s5_out_matrix)(w['s5_c_re'], w['s5_c_im']).astype(BF16),
            'tabs': [tables(0, False, False), tables(1, False, True)],
            'tabs_adj': [tables(0, True, True), tables(1, True, False)]}


def _s5_param_grads(w, raws):
    def stacked(k):
        return jnp.stack([jnp.stack([raws[i][d][k] for d in range(2)]) for i in range(DEPTH)])
    dbb_re, dbb_im = _both(_s5_in_matrix_grad)(stacked(0))
    dc_re, dc_im = _both(_s5_out_matrix_grad)(stacked(1))
    da_re, da_im = _s5_unblocked(jnp.sum(stacked(2), axis=2))
    _, vjp = jax.vjp(_both(_s5_prep), w['s5_lambda_re'], w['s5_lambda_im'], w['s5_log_dt'], w['s5_b_re'], w['s5_b_im'])
    g = vjp((da_re, da_im, dbb_re, dbb_im))
    return {'s5_lambda_re': g[0], 's5_lambda_im': g[1], 's5_log_dt': g[2], 's5_b_re': g[3], 's5_b_im': g[4],
            's5_c_re': dc_re, 's5_c_im': dc_im}


def _s5_fwd(p_in, prm, w_glu):
    dirs = []
    ys = []
    for d, reverse in ((0, False), (1, True)):
        xs, y_dir = _s5_direction_fwd(p_in, prm['b_mat'][d], prm['c_mat'][d], prm['tabs'][d], reverse,
                                      name="s5_fwd_rev" if reverse else "s5_fwd")
        ys.append(y_dir)
        dirs.append(xs)

    def post(yf, yb, u, dskip):
        ypre = yf + yb + dskip * u
        th, _ = _gelu_parts(ypre)
        return ypre, 0.5 * ypre * (1.0 + th)
    ypre, yg = _rowmap(post, [ys[0], ys[1], (p_in, CB_U, S5_WIDTH)], [prm['d'].reshape(1, -1)],
                       [(S5_WIDTH, F32), (S5_WIDTH, F32)], tl=512, name="s5_post")
    t = _mm(yg, w_glu, name="s5_glu_mm")

    def glu(ygv, tv):
        return ygv * _sigmoid(tv)
    y = _rowmap(glu, [yg, t], [], [(S5_WIDTH, BF16)], tl=512, name="s5_glu")[0]
    return y, (dirs, ypre, yg, t)


def _s5_bwd(pg, prm, w_glu, saved, dy):
    dirs, ypre, yg, t = saved

    def glu_bwd(dyv, ygv, tv):
        s = _sigmoid(tv)
        return dyv * ygv * s * (1.0 - s), dyv * s
    dt, dyg_direct = _rowmap(glu_bwd, [dy, yg, t], [], [(S5_WIDTH, BF16), (S5_WIDTH, F32)], tl=512, name="s5_glu_bwd")
    grads = {'w_glu': _mm(yg, dt, ta=True, out_dtype=BF16, name="s5_dwglu")}
    dyg_mm = _mm(dt, w_glu, tb=True, name="s5_dyg")

    def post_bwd(dyd, dym, yp, u, dskip):
        th, dinner = _gelu_parts(yp)
        dyp = (dyd + dym) * (0.5 * (1.0 + th) + 0.5 * yp * (1.0 - th * th) * dinner)
        return dyp, dyp * dskip, jnp.sum(dyp * u, axis=0, keepdims=True)
    dyp, du_skip, dd = _rowmap(post_bwd, [dyg_direct, dyg_mm, ypre, (pg, CB_U, S5_WIDTH)], [prm['d'].reshape(1, -1)],
                               [(S5_WIDTH, F32), (S5_WIDTH, F32)], [(1, S5_WIDTH)], tl=512, name="s5_post_bwd")
    grads['d'] = dd[0]
    du = [du_skip]
    grads['raw'] = []
    for d, reverse in ((0, False), (1, True)):
        du_dir, d_bmat, d_cmat_t, da = _s5_direction_bwd(pg, dyp, dirs[d], prm['b_mat'][d], prm['c_mat'][d], prm['tabs_adj'][d],
                                                         reverse, name="s5_bwd_rev" if reverse else "s5_bwd")
        du.append(du_dir)
        grads['raw'].append((d_bmat, d_cmat_t, da))
    return du, grads


def _split3(x):
    hi = x.astype(BF16)
    r = x - hi.astype(F32)
    mid = r.astype(BF16)
    return hi, mid, (r - mid.astype(F32)).astype(BF16)


def _exact_dot(ones, x, dims):
    parts = [lax.dot_general(ones, p, dims, preferred_element_type=F32) for p in _split3(x)]
    return parts[0] + parts[1] + parts[2]


_NN = (((1,), (0,)), ((), ()))
_NT = (((1,), (1,)), ((), ()))
_TN = (((0,), (0,)), ((), ()))


def _dot(a, b, dims=_NN):
    return lax.dot_general(a.astype(BF16), b.astype(BF16), dims, preferred_element_type=F32)


def _gla_chunk_mask(reverse):
    rows = lax.broadcasted_iota(jnp.int32, (GLA_CHUNK, GLA_CHUNK), 0)
    cols = lax.broadcasted_iota(jnp.int32, (GLA_CHUNK, GLA_CHUNK), 1)
    return (cols >= rows) if reverse else (cols <= rows)


def _gla_fwd(pg, la, reverse, *, name):
    length = la.shape[0]
    nch = length // GLA_CHUNK
    scale = GLA_HEAD_DIM ** -0.5
    last = 0 if reverse else GLA_CHUNK - 1
    hd = GLA_HEAD_DIM

    def body(q_ref, k_ref, v_ref, la_ref, o_ref, sp_ref, st_ref):
        @pl.when(pl.program_id(0) == 0)
        def _():
            st_ref[...] = jnp.zeros_like(st_ref)

        mask = _gla_chunk_mask(reverse)
        b = _exact_dot(mask.astype(BF16), la_ref[...], _NN)
        sp_ref[0] = st_ref[...]
        outs = []
        for h in range(GLA_HEADS):
            sl = slice(h * hd, (h + 1) * hd)
            bh = b[:, sl]
            bl = bh[last:last + 1, :]
            k = k_ref[:, sl]
            v = v_ref[:, sl]
            qd = q_ref[:, sl] * scale * jnp.exp(bh)
            kd = k * jnp.exp(-bh)
            ke = k * jnp.exp(bl - bh)
            st = st_ref[sl, :]
            p = jnp.where(mask, _dot(qd, kd, _NT), 0.0)
            outs.append(_dot(p, v) + _dot(qd, st, _NT))
            st_ref[sl, :] = st * jnp.exp(bl) + _dot(v, ke, _TN)
        o_ref[...] = jnp.concatenate(outs, axis=1)

    cmap = (lambda n: nch - 1 - n) if reverse else (lambda n: n)
    col = lambda cb: pl.BlockSpec((GLA_CHUNK, GLA_WIDTH), lambda n, cb=cb: (cmap(n), cb))
    return pl.pallas_call(
        body, name=name, grid=(nch,),
        in_specs=[col(CB_GQ), col(CB_GK), col(CB_GV), col(0)],
        out_specs=[col(0), pl.BlockSpec((1, GLA_WIDTH, hd), lambda n: (cmap(n), 0, 0))],
        out_shape=[jax.ShapeDtypeStruct((length, GLA_WIDTH), F32), jax.ShapeDtypeStruct((nch, GLA_WIDTH, hd), F32)],
        scratch_shapes=[pltpu.VMEM((GLA_WIDTH, hd), F32)],
        compiler_params=pltpu.CompilerParams(dimension_semantics=("arbitrary",)),
    )(pg, pg, pg, la)


def _gla_bwd(pg, la, do, sprev, reverse, *, name):
    length = la.shape[0]
    nch = length // GLA_CHUNK
    scale = GLA_HEAD_DIM ** -0.5
    last = 0 if reverse else GLA_CHUNK - 1
    hd = GLA_HEAD_DIM

    def body(q_ref, k_ref, v_ref, la_ref, do_ref, sp_ref, dq_ref, dk_ref, dv_ref, dla_ref, dst_ref):
        @pl.when(pl.program_id(0) == 0)
        def _():
            dst_ref[...] = jnp.zeros_like(dst_ref)

        mask = _gla_chunk_mask(reverse)
        tri = mask.astype(BF16)
        b = _exact_dot(tri, la_ref[...], _NN)
        is_last = lax.broadcasted_iota(jnp.int32, (GLA_CHUNK, hd), 0) == last
        dqs, dks, dvs, dbs = [], [], [], []
        for h in range(GLA_HEADS):
            sl = slice(h * hd, (h + 1) * hd)
            bh = b[:, sl]
            bl = bh[last:last + 1, :]
            eb, enb, ebl, el = jnp.exp(bh), jnp.exp(-bh), jnp.exp(bl - bh), jnp.exp(bl)
            k = k_ref[:, sl]
            v = v_ref[:, sl]
            dov = do_ref[:, sl]
            qd = q_ref[:, sl] * scale * eb
            kd = k * enb
            ke = k * ebl
            st = sp_ref[0, sl, :]
            dst = dst_ref[sl, :]
            p = jnp.where(mask, _dot(qd, kd, _NT), 0.0)
            dp = jnp.where(mask, _dot(dov, v, _NT), 0.0)
            dqd = _dot(dp, kd) + _dot(dov, st)
            dkd = _dot(dp, qd, _TN)
            dvs.append(_dot(p, dov, _TN) + _dot(ke, dst, _NT))
            dke = _dot(v, dst)
            dst_ref[sl, :] = dst * el + _dot(dov, qd, _TN)
            dbl = el * jnp.sum(dst * st, axis=0, keepdims=True) + jnp.sum(dke * ke, axis=0, keepdims=True)
            db = dqd * qd - dkd * kd - dke * ke
            dbs.append(jnp.where(is_last, db + dbl, db))
            dqs.append(dqd * eb * scale)
            dks.append(dkd * enb + dke * ebl)
        dq_ref[...] = jnp.concatenate(dqs, axis=1)
        dk_ref[...] = jnp.concatenate(dks, axis=1)
        dv_ref[...] = jnp.concatenate(dvs, axis=1)
        tri_t = _gla_chunk_mask(not reverse).astype(BF16)
        dla_ref[...] = _exact_dot(tri_t, jnp.concatenate(dbs, axis=1), _NN)

    cmap = (lambda n: n) if reverse else (lambda n: nch - 1 - n)
    col = lambda cb: pl.BlockSpec((GLA_CHUNK, GLA_WIDTH), lambda n, cb=cb: (cmap(n), cb))
    wide = jax.ShapeDtypeStruct((length, GLA_WIDTH), F32)
    return pl.pallas_call(
        body, name=name, grid=(nch,),
        in_specs=[col(CB_GQ), col(CB_GK), col(CB_GV), col(0), col(0),
                  pl.BlockSpec((1, GLA_WIDTH, hd), lambda n: (cmap(n), 0, 0))],
        out_specs=[col(0)] * 4, out_shape=[wide] * 4,
        scratch_shapes=[pltpu.VMEM((GLA_WIDTH, hd), F32)],
        compiler_params=pltpu.CompilerParams(dimension_semantics=("arbitrary",)),
    )(pg, pg, pg, la, do, sprev)


def _log_sigmoid(x):
    return jnp.minimum(x, 0.0) - jnp.log(1.0 + jnp.exp(-jnp.abs(x)))


def _gla_alpha_padded(w_alpha):
    w = jnp.zeros((2, 128, GLA_WIDTH), w_alpha.dtype)
    w = w.at[0, 0:GLA_LOWRANK].set(w_alpha[0])
    return w.at[1, GLA_LOWRANK:2 * GLA_LOWRANK].set(w_alpha[1])


def _gla_branch_fwd(pg, w_alpha, b_alpha, norm_gain):
    wa = _gla_alpha_padded(w_alpha).astype(BF16)

    def gates(z, w, bias):
        return (_log_sigmoid(_dot(z, w[0]) + bias[0:1]) / GLA_TAU, _log_sigmoid(_dot(z, w[1]) + bias[1:2]) / GLA_TAU)
    la_f, la_b = _rowmap(gates, [(pg, CB_Z, 128)], [wa, b_alpha], [(GLA_WIDTH, F32), (GLA_WIDTH, F32)], tl=512,
                         name="gla_gates")
    o_f, sp_f = _gla_fwd(pg, la_f, False, name="gla_fwd")
    o_b, sp_b = _gla_fwd(pg, la_b, True, name="gla_fwd_rev")

    def post(of, ob, gate, gn):
        o = of + ob
        on = jnp.concatenate([o[:, s:s + GLA_HEAD_DIM] * _rms(o[:, s:s + GLA_HEAD_DIM]) * gn
                              for s in range(0, GLA_WIDTH, GLA_HEAD_DIM)], axis=1)
        return o, on * (gate * _sigmoid(gate))
    o, y = _rowmap(post, [o_f, o_b, (pg, CB_GG, GLA_WIDTH)], [norm_gain.reshape(1, -1)],
                   [(GLA_WIDTH, F32), (GLA_WIDTH, BF16)], tl=512, name="gla_post")
    return y, (wa, la_f, la_b, sp_f, sp_b, o)


def _gla_branch_bwd(pg, w_alpha, b_alpha, norm_gain, saved, dy):
    wa, la_f, la_b, sp_f, sp_b, o = saved

    def post_bwd(dyv, ov, gate, gn):
        s = _sigmoid(gate)
        dos, dgn, ons = [], [], []
        for c in range(0, GLA_WIDTH, GLA_HEAD_DIM):
            oh = ov[:, c:c + GLA_HEAD_DIM]
            r = _rms(oh)
            don = dyv[:, c:c + GLA_HEAD_DIM] * (gate[:, c:c + GLA_HEAD_DIM] * s[:, c:c + GLA_HEAD_DIM])
            gd = don * gn
            dos.append(r * gd - oh * (r * r * r) * jnp.mean(oh * gd, axis=-1, keepdims=True))
            dgn.append(jnp.sum(don * oh * r, axis=0, keepdims=True))
            ons.append(oh * r * gn)
        on = jnp.concatenate(ons, axis=1)
        dgate = dyv * on * (s * (1.0 + gate * (1.0 - s)))
        return jnp.concatenate(dos, axis=1), dgate, jnp.concatenate(dgn, axis=1)
    do, dgate, dgn = _rowmap(post_bwd, [dy, o, (pg, CB_GG, GLA_WIDTH)], [norm_gain.reshape(1, -1)],
                             [(GLA_WIDTH, F32), (GLA_WIDTH, F32)], [(1, GLA_WIDTH)], tl=512, name="gla_post_bwd")
    dq_f, dk_f, dv_f, dla_f = _gla_bwd(pg, la_f, do, sp_f, False, name="gla_bwd")
    dq_b, dk_b, dv_b, dla_b = _gla_bwd(pg, la_b, do, sp_b, True, name="gla_bwd_rev")

    def gates_bwd(z, dlf, dlb, w, bias):
        dz = jnp.zeros_like(z)
        dlogits, dbs = [], []
        for d, dl in ((0, dlf), (1, dlb)):
            logit = _dot(z, w[d]) + bias[d:d + 1]
            dlogit = dl * (1.0 / GLA_TAU) * jnp.exp(_log_sigmoid(-logit))
            dz = dz + _dot(dlogit, w[d], _NT)
            dlogits.append(dlogit)
            dbs.append(jnp.sum(dlogit, axis=0, keepdims=True))
        return dz, dlogits[0], dlogits[1], dbs[0], dbs[1]
    dz, dlg_f, dlg_b, dba_f, dba_b = _rowmap(
        gates_bwd, [(pg, CB_Z, 128), dla_f, dla_b], [wa, b_alpha], [(128, F32), (GLA_WIDTH, BF16), (GLA_WIDTH, BF16)],
        [(1, GLA_WIDTH), (1, GLA_WIDTH)], tl=512, name="gla_gates_bwd")
    dwa_f = _mm(dlg_f, (pg, CB_Z, 128), ta=True, name="gla_dwalpha")
    dwa_b = _mm(dlg_b, (pg, CB_Z, 128), ta=True, name="gla_dwalpha")
    grads = {'w_alpha': jnp.stack([dwa_f[:, 0:GLA_LOWRANK].T, dwa_b[:, GLA_LOWRANK:2 * GLA_LOWRANK].T]),
             'b_alpha': jnp.concatenate([dba_f, dba_b], axis=0),
             'norm': jnp.sum(dgn.reshape(GLA_HEADS, GLA_HEAD_DIM), axis=0)}
    return [dq_f, dq_b], [dk_f, dk_b], [dv_f, dv_b], dgate, dz, grads


def _rope_tables(length):
    half = ATTN_HEAD_DIM // 2
    inv_freq = ROPE_BASE ** (-jnp.arange(half // 2, dtype=F32) * 2.0 / half)
    t = jnp.arange(length, dtype=jnp.int32)
    def one(pos):
        ang = pos.astype(F32)[:, None] * inv_freq[None, :]
        c, s = jnp.cos(ang), jnp.sin(ang)
        return jnp.concatenate([c, c], axis=1), jnp.concatenate([-s, s], axis=1)
    c_r, s_r = one(t // GRID_W)
    c_c, s_c = one(t % GRID_W)
    return jnp.concatenate([c_r, c_c], axis=1), jnp.concatenate([s_r, s_c], axis=1)


def _rope_swap(y):
    w = y.shape[1]
    lane = lax.broadcasted_iota(jnp.int32, y.shape, 1)
    return jnp.where(lane % 32 < 16, pltpu.roll(y, w - 16, 1), pltpu.roll(y, 16, 1))


def _head_sums(x, ones):
    parts = [lax.dot_general(p, ones, _NN, preferred_element_type=F32) for p in _split3(x)]
    return parts[0] + parts[1] + parts[2]


def _head_ones(width):
    seg = np.arange(width) // ATTN_HEAD_DIM
    return jnp.asarray(seg[:, None] == seg[None, :], BF16)


def _qk_prep_fwd(pg, cb, width, gain, cos, sin, scale, *, name):
    heads = width // ATTN_HEAD_DIM
    def fn(x, c, s, g, ones):
        r = lax.rsqrt(_head_sums(x * x, ones) * (1.0 / ATTN_HEAD_DIM) + NORM_EPS)
        y = x * r * g
        return (y * c + _rope_swap(y) * s) * scale
    return _rowmap(fn, [(pg, cb, width), jnp.tile(cos, (1, heads)), jnp.tile(sin, (1, heads))],
                   [jnp.tile(gain, heads).reshape(1, -1), _head_ones(width)], [(width, BF16)], tl=512, name=name)[0]


def _qk_prep_bwd(pg, cb, width, gain, cos, sin, scale, dout, *, name):
    heads = width // ATTN_HEAD_DIM
    def fn(x, dov, c, s, g, ones):
        r = lax.rsqrt(_head_sums(x * x, ones) * (1.0 / ATTN_HEAD_DIM) + NORM_EPS)
        dos = dov * scale
        dy = dos * c + _rope_swap(dos * s)
        gd = dy * g
        dx = r * gd - x * (r * r * r) * (_head_sums(x * gd, ones) * (1.0 / ATTN_HEAD_DIM))
        return dx, jnp.sum(dy * x * r, axis=0, keepdims=True)
    dx, dg = _rowmap(fn, [(pg, cb, width), dout, jnp.tile(cos, (1, heads)), jnp.tile(sin, (1, heads))],
                     [jnp.tile(gain, heads).reshape(1, -1), _head_ones(width)], [(width, F32)], [(1, width)], tl=512,
                     name=name)
    return dx, jnp.sum(dg.reshape(heads, ATTN_HEAD_DIM), axis=0)


def _to_heads(x, heads):
    return jnp.transpose(x.reshape(x.shape[0], heads, ATTN_HEAD_DIM), (1, 0, 2))


def _from_heads(x):
    return jnp.transpose(x, (1, 0, 2)).reshape(x.shape[1], x.shape[0] * ATTN_HEAD_DIM)


ATTN_GROUP = ATTN_Q_HEADS // ATTN_KV_HEADS
ATTN_TQ = 256


def _attn_fwd(q, k, v, side=None):
    length = q.shape[1]
    tq = min(ATTN_TQ, length)
    grid = (ATTN_KV_HEADS, length // tq)

    def compute(refs):
        q_ref, k_ref, v_ref, o_ref = refs
        kk, vv = k_ref[0], v_ref[0]
        for g in range(ATTN_GROUP):
            s = _dot(q_ref[g], kk, _NT)
            p = jnp.exp(s - jnp.max(s, axis=-1, keepdims=True))
            o_ref[g] = _dot(p, vv) / jnp.sum(p, axis=-1, keepdims=True)

    def body(*refs):
        _carried(side, grid, refs, 3, 1, 0, compute)

    kv_spec = pl.BlockSpec((1, length, ATTN_HEAD_DIM), lambda h, i: (h, 0, 0))
    q_spec = pl.BlockSpec((ATTN_GROUP, tq, ATTN_HEAD_DIM), lambda h, i: (h, i, 0))
    (out,), gathered = _side_call(
        body, side, name="attn_fwd", grid=grid, in_specs=[q_spec, kv_spec, kv_spec], out_specs=[q_spec],
        out_shape=[jax.ShapeDtypeStruct(q.shape, F32)], scratch=[], args=[q, k, v], semantics=("parallel", "parallel"))
    return out, gathered


def _attn_bwd(q, k, v, o, do):
    length = q.shape[1]
    tq = min(ATTN_TQ, length)

    def body(q_ref, k_ref, v_ref, o_ref, do_ref, dq_ref, dk_ref, dv_ref):
        @pl.when(pl.program_id(1) == 0)
        def _():
            dk_ref[...] = jnp.zeros_like(dk_ref)
            dv_ref[...] = jnp.zeros_like(dv_ref)

        kk, vv = k_ref[0], v_ref[0]
        for g in range(ATTN_GROUP):
            qg, dog = q_ref[g], do_ref[g]
            s = _dot(qg, kk, _NT)
            p = jnp.exp(s - jnp.max(s, axis=-1, keepdims=True))
            p = p / jnp.sum(p, axis=-1, keepdims=True)
            dp = _dot(dog, vv, _NT)
            ds = p * (dp - jnp.sum(dog * o_ref[g], axis=-1, keepdims=True))
            dq_ref[g] = _dot(ds, kk)
            dk_ref[0] += _dot(ds, qg, _TN)
            dv_ref[0] += _dot(p, dog, _TN)

    kv_spec = pl.BlockSpec((1, length, ATTN_HEAD_DIM), lambda h, i: (h, 0, 0))
    q_spec = pl.BlockSpec((ATTN_GROUP, tq, ATTN_HEAD_DIM), lambda h, i: (h, i, 0))
    return pl.pallas_call(
        body, name="attn_bwd", grid=(ATTN_KV_HEADS, length // tq),
        in_specs=[q_spec, kv_spec, kv_spec, q_spec, q_spec], out_specs=[q_spec, kv_spec, kv_spec],
        out_shape=[jax.ShapeDtypeStruct(q.shape, F32), jax.ShapeDtypeStruct(k.shape, F32),
                   jax.ShapeDtypeStruct(k.shape, F32)],
        compiler_params=pltpu.CompilerParams(dimension_semantics=("parallel", "arbitrary")),
    )(q, k, v, o, do)


def _attn_branch_fwd(pg, q_gain, k_gain, side=None):
    cos, sin = _rope_tables(pg.shape[0])
    qp = _qk_prep_fwd(pg, CB_AQ, ATTN_WIDTH, q_gain, cos, sin, ATTN_HEAD_DIM ** -0.5, name="attn_q_prep")
    kp = _qk_prep_fwd(pg, CB_AK, ATTN_KV_WIDTH, k_gain, cos, sin, 1.0, name="attn_k_prep")
    qh, kh = _to_heads(qp, ATTN_Q_HEADS), _to_heads(kp, ATTN_KV_HEADS)
    vh = _to_heads(pg[:, P_OFF + 3200:P_OFF + 3328].astype(BF16), ATTN_KV_HEADS)
    oh, gathered = _attn_fwd(qh, kh, vh, side)
    return _from_heads(oh).astype(BF16), (cos, sin, qh, kh, vh, oh), gathered


def _attn_branch_bwd(pg, q_gain, k_gain, saved, dy):
    cos, sin, qh, kh, vh, oh = saved
    dqh, dkh, dvh = _attn_bwd(qh, kh, vh, oh, _to_heads(dy, ATTN_Q_HEADS))
    dq, dqg = _qk_prep_bwd(pg, CB_AQ, ATTN_WIDTH, q_gain, cos, sin, ATTN_HEAD_DIM ** -0.5, _from_heads(dqh),
                           name="attn_q_prep_bwd")
    dk, dkg = _qk_prep_bwd(pg, CB_AK, ATTN_KV_WIDTH, k_gain, cos, sin, 1.0, _from_heads(dkh), name="attn_k_prep_bwd")
    return dq, dk, _from_heads(dvh), {'q_norm': dqg, 'k_norm': dkg}


def _gate_cols():
    return [slice(i * D_MODEL, (i + 1) * D_MODEL) for i in range(3)]


def _mixer_fwd(x, lw, side_in=None, side_attn=None):
    h = _rmsnorm_fwd(x, lw['mix_norm'])
    pg, got_in = _mm(h, lw['w_pg'], side=side_in, name="mix_in") if side_in is not None else (_mm(h, lw['w_pg'], name="mix_in"), [])
    y_s5, s_s5 = _s5_fwd(pg, lw['s5'], lw['s5_w_glu'])
    y_gla, s_gla = _gla_branch_fwd(pg, lw['gla_w_alpha'], lw['gla_b_alpha'], lw['gla_norm'])
    y_att, s_att, got_attn = _attn_branch_fwd(pg, lw['attn_q_norm'], lw['attn_k_norm'], side_attn)
    ys = (y_s5, y_gla, y_att)
    br = [_mm(y, lw[n], name="mix_branch") for y, n in zip(ys, ('w_branch_s5', 'w_branch_gla', 'w_branch_attn'))]

    def merge(g0, g1, g2, b0, b1, b2, bias):
        acc = None
        for g, b, c in zip((g0, g1, g2), (b0, b1, b2), _gate_cols()):
            term = _sigmoid(g + bias[:, c]) * b
            acc = term if acc is None else acc + term
        return acc
    merged = _rowmap(merge, [(pg, 0, D_MODEL), (pg, 1, D_MODEL), (pg, 2, D_MODEL)] + br,
                     [lw['b_merge_gate'].reshape(1, -1)], [(D_MODEL, BF16)], tl=256, name="mix_merge")[0]
    x_out = _mm(merged, lw['w_out'], add=x, name="mix_out")
    return x_out, (x, h, pg, ys, (s_s5, s_gla, s_att), br, merged), (got_in, got_attn)


def _mixer_bwd(saved, lw, dx_out):
    x, h, pg, ys, (s_s5, s_gla, s_att), br, merged = saved
    grads = {'w_out': _mm(merged, dx_out, ta=True, out_dtype=BF16, name="mix_dwout")}
    dmerged = _mm(dx_out, lw['w_out'], tb=True, name="mix_dmerged")

    def merge_bwd(g0, g1, g2, b0, b1, b2, dm, bias):
        dbr, dgp = [], []
        for g, b, c in zip((g0, g1, g2), (b0, b1, b2), _gate_cols()):
            s = _sigmoid(g + bias[:, c])
            dbr.append(dm * s)
            dgp.append(dm * b * (s * (1.0 - s)))
        dgp = jnp.concatenate(dgp, axis=1)
        return dbr[0], dbr[1], dbr[2], dgp, jnp.sum(dgp, axis=0, keepdims=True)
    d0, d1, d2, dgpre, dbias = _rowmap(
        merge_bwd, [(pg, 0, D_MODEL), (pg, 1, D_MODEL), (pg, 2, D_MODEL)] + br + [dmerged],
        [lw['b_merge_gate'].reshape(1, -1)], [(D_MODEL, BF16)] * 3 + [(GATE_WIDTH, BF16)], [(1, GATE_WIDTH)], tl=256,
        name="mix_merge_bwd")
    grads['b_merge_gate'] = dbias[0]
    dys = []
    for y, dbr, n in zip(ys, (d0, d1, d2), ('w_branch_s5', 'w_branch_gla', 'w_branch_attn')):
        grads[n] = _mm(y, dbr, ta=True, out_dtype=BF16, name="mix_dwbranch")
        dys.append(_mm(dbr, lw[n], tb=True, name="mix_dy"))
    du, g_s5 = _s5_bwd(pg, lw['s5'], lw['s5_w_glu'], s_s5, dys[0])
    dgq, dgk, dgv, dgg, dz, g_gla = _gla_branch_bwd(pg, lw['gla_w_alpha'], lw['gla_b_alpha'], lw['gla_norm'], s_gla, dys[1])
    daq, dak, dav, g_att = _attn_branch_bwd(pg, lw['attn_q_norm'], lw['attn_k_norm'], s_att, dys[2])

    def assemble(dgp, u0, u1, u2, q0, q1, k0, k1, v0, v1, gg, aq, ak, av, z):
        pad = jnp.zeros((dgp.shape[0], IN_PAD - 3456), F32)
        parts = [dgp.astype(F32), u0 + u1 + u2, q0 + q1, k0 + k1, v0 + v1, gg, aq, ak, av, z, pad]
        return jnp.concatenate(parts, axis=1)
    dpg = _rowmap(assemble, [dgpre] + du + dgq + dgk + dgv + [dgg, daq, dak, dav, dz], [], [(PG_WIDTH, BF16)], tl=256,
                  name="mix_dpg")[0]
    grads['w_pg'] = _mm(h, dpg, ta=True, out_dtype=BF16, name="mix_dwpg")
    dh = _mm(dpg, lw['w_pg'], tb=True, name="mix_dh")
    dx, grads['mix_norm'] = _rmsnorm_bwd(x, lw['mix_norm'], dh, dx_out)
    grads['s5'], grads['gla'], grads['attn'] = g_s5, g_gla, g_att
    return dx, grads


def _loss_head(x, gain, target):
    width = x.shape[1]

    def fn(xv, tv, g):
        r = _rms(xv)
        err = xv * r * g - tv
        dy = err * (1.0 / width)
        gd = dy * g
        dx = r * gd - xv * (r * r * r) * jnp.mean(xv * gd, axis=-1, keepdims=True)
        loss = jnp.sum(0.5 * jnp.mean(err * err, axis=-1, keepdims=True), axis=0, keepdims=True)
        return dx, jnp.broadcast_to(loss, (1, 128)), jnp.sum(dy * xv * r, axis=0, keepdims=True)
    dx, loss, dgain = _rowmap(fn, [x, target], [gain.reshape(1, -1)], [(width, F32)], [(1, 128), (1, width)], tl=256,
                              name="loss_head")
    return loss[0, 0], dx, dgain[0]


def _row_tile(rows, cap=256):
    for t in range(cap - cap % 16, 0, -16):
        if rows % t == 0:
            return t
    return rows


def _reduce_adamw(parts, w, m, v, *, name):
    nparts, r, c = parts.shape
    tr = _row_tile(r)

    def body(p_ref, w_ref, m_ref, v_ref, g_ref, d_ref, m2_ref, v2_ref):
        g = p_ref[0].astype(F32)
        for j in range(1, nparts):
            g = g + p_ref[j].astype(F32)
        m2 = ADAM_B1 * m_ref[...] + (1.0 - ADAM_B1) * g
        v2 = ADAM_B2 * v_ref[...] + (1.0 - ADAM_B2) * (g * g)
        m_hat = m2 / (1.0 - ADAM_B1 ** ADAM_STEP)
        v_hat = v2 / (1.0 - ADAM_B2 ** ADAM_STEP)
        g_ref[...] = g
        d_ref[...] = -ADAM_LR * (m_hat / (jnp.sqrt(v_hat) + ADAM_EPS) + ADAM_WD * w_ref[...])
        m2_ref[...] = m2
        v2_ref[...] = v2

    flat = pl.BlockSpec((tr, c), lambda i: (i, 0))
    return pl.pallas_call(
        body, name=name, grid=(r // tr,), in_specs=[pl.BlockSpec((nparts, tr, c), lambda i: (0, i, 0)), flat, flat, flat],
        out_specs=[flat] * 4, out_shape=[jax.ShapeDtypeStruct((r, c), F32)] * 4,
        compiler_params=pltpu.CompilerParams(dimension_semantics=("parallel",)),
    )(parts, w, m, v)


def _all_gather(blocks, *, name):
    side = _SideGather(blocks)

    def body(*refs):
        start, finish = side.hooks(refs)
        start()
        finish()

    return pl.pallas_call(body, name=name, out_shape=side.out_shape, in_specs=side.in_specs, out_specs=side.out_specs,
                          scratch_shapes=side.scratch)(*blocks)


class _SideGather:
    def __init__(self, blocks):
        self.blocks = list(blocks)
        self.n = n = len(self.blocks)
        hbm = pl.BlockSpec(memory_space=pl.ANY)
        self.in_specs, self.out_specs = [hbm] * n, [hbm] * n
        self.out_shape = [jax.ShapeDtypeStruct((N_DEV,) + b.shape, b.dtype) for b in self.blocks]
        self.scratch = [pltpu.SemaphoreType.DMA((n, 7)), pltpu.SemaphoreType.DMA((n, 7)), pltpu.SemaphoreType.DMA((n,))]

    def hooks(self, refs):
        n = self.n
        x_refs, out_refs = refs[:n], refs[n:2 * n]
        send_sems, recv_sems, local_sems = refs[2 * n:]
        x, y, c = lax.axis_index("x"), lax.axis_index("y"), lax.axis_index("c")
        me, sibling = (x, y, c), (x, y, 1 - c)
        chips = [(1 - x, y), (x, 1 - y), (1 - x, 1 - y)]

        def slot(t, px, py, pc):
            return out_refs[t].at[4 * px + 2 * py + pc]

        def copy(t, k, blk, to, own=False):
            return pltpu.make_async_remote_copy(
                src_ref=x_refs[t] if own else slot(t, *blk), dst_ref=slot(t, *blk), send_sem=send_sems.at[t, k],
                recv_sem=recv_sems.at[t, k], device_id=to, device_id_type=pl.DeviceIdType.MESH)

        def mine(t):
            return pltpu.make_async_copy(x_refs[t], slot(t, *me), local_sems.at[t])

        def first(t):
            return [copy(t, 0, me, sibling, own=True)] + [copy(t, 1 + j, me, (*chip, c), own=True) for j, chip in enumerate(chips)]

        def start():
            for t in range(n):
                mine(t).start()
            for t in range(n):
                for cp in first(t):
                    cp.start()

        def finish():
            passed = []
            for j, chip in enumerate(chips):
                for t in range(n):
                    copy(t, 1 + j, (*chip, c), me).wait_recv()
                    passed.append(copy(t, 4 + j, (*chip, c), sibling))
                    passed[-1].start()
            for t in range(n):
                copy(t, 0, sibling, me).wait_recv()
            for j, chip in enumerate(chips):
                for t in range(n):
                    copy(t, 4 + j, (*chip, 1 - c), me).wait_recv()
            for t in range(n):
                for cp in first(t):
                    cp.wait_send()
            for cp in passed:
                cp.wait_send()
            for t in range(n):
                mine(t).wait()

        return start, finish


def _first_last_step(grid):
    ids = [pl.program_id(a) for a in range(len(grid))]
    first = functools.reduce(lambda p, q: p & q, [i == 0 for i in ids])
    last = functools.reduce(lambda p, q: p & q, [i == n - 1 for i, n in zip(ids, grid)])
    return first, last


def _carried(side, grid, refs, n_in, n_out, n_scratch, compute):
    if side is None:
        compute(refs)
        return
    n = side.n
    main = refs[:n_in] + refs[n_in + n:n_in + n + n_out] + refs[n_in + 2 * n + n_out:n_in + 2 * n + n_out + n_scratch]
    side_refs = refs[n_in:n_in + n] + refs[n_in + n + n_out:n_in + 2 * n + n_out] + refs[n_in + 2 * n + n_out + n_scratch:]
    start, finish = side.hooks(side_refs)
    first, last = _first_last_step(grid)
    pl.when(first)(start)
    compute(main)
    pl.when(last)(finish)


N_CHIP = N_DEV // 2


def _swap_with_sibling(arrays, *, name):
    n = len(arrays)

    def body(*refs):
        src_refs, out_refs = refs[:n], refs[n:2 * n]
        send_sems, recv_sems = refs[2 * n:]
        sibling = (lax.axis_index("x"), lax.axis_index("y"), 1 - lax.axis_index("c"))
        copies = [pltpu.make_async_remote_copy(
            src_ref=src_refs[t], dst_ref=out_refs[t], send_sem=send_sems.at[t], recv_sem=recv_sems.at[t],
            device_id=sibling, device_id_type=pl.DeviceIdType.MESH) for t in range(n)]
        for cp in copies:
            cp.start()
        for cp in copies:
            cp.wait()

    hbm = pl.BlockSpec(memory_space=pl.ANY)
    return pl.pallas_call(
        body, name=name, out_shape=[jax.ShapeDtypeStruct(a.shape, a.dtype) for a in arrays],
        in_specs=[hbm] * n, out_specs=[hbm] * n,
        scratch_shapes=[pltpu.SemaphoreType.DMA((n,)), pltpu.SemaphoreType.DMA((n,))],
    )(*arrays)


def _exchange_chips(stacks, *, name):
    n = len(stacks)

    def body(*refs):
        g_refs, out_refs = refs[:n], refs[n:2 * n]
        send_sems, recv_sems, local_sems = refs[2 * n:]
        x, y, c = lax.axis_index("x"), lax.axis_index("y"), lax.axis_index("c")
        me = 2 * x + y
        mine = [pltpu.make_async_copy(g_refs[t].at[me], out_refs[t].at[me], local_sems.at[t]) for t in range(n)]
        for cp in mine:
            cp.start()
        copies = []
        for k in range(1, N_CHIP):
            px, py = x ^ (k >> 1 & 1), y ^ (k & 1)
            for t in range(n):
                copies.append(pltpu.make_async_remote_copy(
                    src_ref=g_refs[t].at[2 * px + py], dst_ref=out_refs[t].at[me], send_sem=send_sems.at[t, k - 1],
                    recv_sem=recv_sems.at[t, k - 1], device_id=(px, py, c), device_id_type=pl.DeviceIdType.MESH))
        for cp in copies:
            cp.start()
        for cp in copies:
            cp.wait_recv()
        for cp in copies:
            cp.wait_send()
        for cp in mine:
            cp.wait()

    hbm = pl.BlockSpec(memory_space=pl.ANY)
    return pl.pallas_call(
        body, name=name, out_shape=[jax.ShapeDtypeStruct(s.shape, s.dtype) for s in stacks],
        in_specs=[hbm] * n, out_specs=[hbm] * n,
        scratch_shapes=[pltpu.SemaphoreType.DMA((n, N_CHIP - 1)), pltpu.SemaphoreType.DMA((n, N_CHIP - 1)),
                        pltpu.SemaphoreType.DMA((n,))],
    )(*stacks)


def _pair_sum(a, b):
    return _rowmap(lambda u, v: u.astype(F32) + v.astype(F32), [a, b], [], [(a.shape[1], BF16)], tl=_row_tile(a.shape[0], 512),
                   name="pair_sum")[0]


SMALL_COLS = 128


def _pack_small(arrays):
    flat = jnp.concatenate([a.astype(F32).reshape(-1, SMALL_COLS) for a in arrays], axis=0)
    return jnp.pad(flat, ((0, -flat.shape[0] % 256), (0, 0)))


def _unpack_small(packed, shapes):
    out, off = [], 0
    for s in shapes:
        r = math.prod(s) // SMALL_COLS
        out.append(packed[off:off + r].reshape(s))
        off += r
    return out


def _split_shards(full, axis):
    shape = full.shape
    split = full.reshape(shape[:axis] + (N_DEV, shape[axis] // N_DEV) + shape[axis + 1:])
    return jnp.moveaxis(split, axis, 0)


def _join_shards(stack, axis):
    moved = jnp.moveaxis(stack, 0, axis)
    shape = moved.shape
    return moved.reshape(shape[:axis] + (shape[axis] * shape[axis + 1],) + shape[axis + 2:])


def _w_in_padded(w_in):
    pad = jnp.zeros(w_in.shape[:-1] + (IN_PAD - IN_WIDTH,), w_in.dtype)
    return jnp.concatenate([w_in[..., :2560], w_in[..., 2592:], w_in[..., 2560:2592], pad], axis=-1)


def _w_in_unpadded(w):
    return jnp.concatenate([w[..., :2560], w[..., 3328:3360], w[..., 2560:3328]], axis=-1)


FFN1_W = ('ffn1_w_gate', 'ffn1_w_up', 'ffn1_w_down')
FFN2_W = ('ffn2_w_gate', 'ffn2_w_up', 'ffn2_w_down')
MIX_W = ('w_in', 'w_merge_gate', 's5_w_glu', 'gla_w_alpha', 'gla_b_alpha', 'w_branch_s5', 'w_branch_gla', 'w_branch_attn', 'w_out')


def _mixer_weights(full, w, s5, i):
    lw = {n: w[n][i] for n in ('mix_norm', 'gla_norm', 'attn_q_norm', 'attn_k_norm', 'b_merge_gate')}
    lw['s5'] = {'b_mat': s5['b_mat'][i], 'c_mat': s5['c_mat'][i], 'tabs': [t[i] for t in s5['tabs']],
                'tabs_adj': [t[i] for t in s5['tabs_adj']], 'd': w['s5_d'][i]}
    w_in = full['w_in']
    pad = jnp.zeros((D_MODEL, IN_PAD - IN_WIDTH), w_in.dtype)
    lw['w_pg'] = jnp.concatenate([full['w_merge_gate'], w_in[:, :2560], w_in[:, 2592:], w_in[:, 2560:2592], pad], axis=1)
    for n in ('s5_w_glu', 'gla_w_alpha', 'w_branch_s5', 'w_branch_gla', 'w_branch_attn', 'w_out'):
        lw[n] = full[n]
    lw['gla_b_alpha'] = full['gla_b_alpha'].astype(F32)
    return lw


def _step_local(x, target, w, shards):
    s5 = _s5_setup(w)
    full = [{} for _ in range(DEPTH)]

    def wanted(i, *groups):
        return _SideGather([shards[n][i] for names in groups for n in names])

    def arrived(i, stacks, *groups):
        names = [n for group in groups for n in group]
        for n, st in zip(names, stacks):
            full[i][n] = _join_shards(st, SHARD_AXIS[n] - 1)

    arrived(0, _all_gather([shards[n][0] for n in FFN1_W], name="gather_first"), FFN1_W)
    saved, lws = [], []
    for i in range(DEPTH):
        f = full[i]
        x, s1, got = _ffn_fwd(x, w['ffn1_norm'][i], f['ffn1_w_gate'], f['ffn1_w_up'], f['ffn1_w_down'],
                              wanted(i, MIX_W) if i == 0 else None)
        if i == 0:
            arrived(i, got, MIX_W)
        lws.append(_mixer_weights(f, w, s5, i))
        x, s2, (got_in, got_attn) = _mixer_fwd(x, lws[i], wanted(i, FFN2_W) if i == 0 else None,
                                               wanted(i + 1, FFN1_W, MIX_W, FFN2_W) if i == 0 else None)
        if i == 0:
            arrived(i, got_in, FFN2_W)
            arrived(i + 1, got_attn, FFN1_W, MIX_W, FFN2_W)
        x, s3, _ = _ffn_fwd(x, w['ffn2_norm'][i], f['ffn2_w_gate'], f['ffn2_w_up'], f['ffn2_w_down'])
        saved.append((s1, s2, s3))
    loss, dx, d_final = _loss_head(x, w['final_norm'], target)
    per_layer = [None] * DEPTH
    for i in reversed(range(DEPTH)):
        f, lw, (s1, s2, s3), g = full[i], lws[i], saved[i], {}
        dx, g['ffn2_norm'], g['ffn2_w_gate'], g['ffn2_w_up'], g['ffn2_w_down'] = _ffn_bwd(
            s3, w['ffn2_norm'][i], f['ffn2_w_gate'], f['ffn2_w_up'], f['ffn2_w_down'], dx)
        dx, gm = _mixer_bwd(s2, lw, dx)
        dx, g['ffn1_norm'], g['ffn1_w_gate'], g['ffn1_w_up'], g['ffn1_w_down'] = _ffn_bwd(
            s1, w['ffn1_norm'][i], f['ffn1_w_gate'], f['ffn1_w_up'], f['ffn1_w_down'], dx)
        g['w_merge_gate'] = gm['w_pg'][:, :GATE_WIDTH]
        g['w_in'] = _w_in_unpadded(gm['w_pg'][:, GATE_WIDTH:])
        for n in ('w_out', 'b_merge_gate', 'w_branch_s5', 'w_branch_gla', 'w_branch_attn', 'mix_norm'):
            g[n] = gm[n]
        g['s5_d'], g['s5_w_glu'], g['s5_raw'] = gm['s5']['d'], gm['s5']['w_glu'], gm['s5']['raw']
        g['gla_w_alpha'], g['gla_b_alpha'], g['gla_norm'] = gm['gla']['w_alpha'], gm['gla']['b_alpha'], gm['gla']['norm']
        g['attn_q_norm'], g['attn_k_norm'] = gm['attn']['q_norm'], gm['attn']['k_norm']
        per_layer[i] = g
    stacked = _s5_param_grads(w, [g['s5_raw'] for g in per_layer])
    stacked['final_norm'] = d_final
    return loss, dx, per_layer, stacked


def kernel(x, ffn1_norm, ffn1_w_gate, ffn1_w_up, ffn1_w_down, mix_norm, w_in, s5_lambda_re, s5_lambda_im, s5_log_dt, s5_b_re, s5_b_im, s5_c_re, s5_c_im, s5_d, s5_w_glu, gla_w_alpha, gla_b_alpha, gla_norm, attn_q_norm, attn_k_norm, w_branch_s5, w_branch_gla, w_branch_attn, w_merge_gate, b_merge_gate, w_out, ffn2_norm, ffn2_w_gate, ffn2_w_up, ffn2_w_down, final_norm, loss_target, m_ffn1_norm, m_ffn1_w_gate, m_ffn1_w_up, m_ffn1_w_down, m_mix_norm, m_w_in, m_s5_lambda_re, m_s5_lambda_im, m_s5_log_dt, m_s5_b_re, m_s5_b_im, m_s5_c_re, m_s5_c_im, m_s5_d, m_s5_w_glu, m_gla_w_alpha, m_gla_b_alpha, m_gla_norm, m_attn_q_norm, m_attn_k_norm, m_w_branch_s5, m_w_branch_gla, m_w_branch_attn, m_w_merge_gate, m_b_merge_gate, m_w_out, m_ffn2_norm, m_ffn2_w_gate, m_ffn2_w_up, m_ffn2_w_down, m_final_norm, v_ffn1_norm, v_ffn1_w_gate, v_ffn1_w_up, v_ffn1_w_down, v_mix_norm, v_w_in, v_s5_lambda_re, v_s5_lambda_im, v_s5_log_dt, v_s5_b_re, v_s5_b_im, v_s5_c_re, v_s5_c_im, v_s5_d, v_s5_w_glu, v_gla_w_alpha, v_gla_b_alpha, v_gla_norm, v_attn_q_norm, v_attn_k_norm, v_w_branch_s5, v_w_branch_gla, v_w_branch_attn, v_w_merge_gate, v_b_merge_gate, v_w_out, v_ffn2_norm, v_ffn2_w_gate, v_ffn2_w_up, v_ffn2_w_down, v_final_norm):
    return _train_step(x, ffn1_norm, ffn1_w_gate, ffn1_w_up, ffn1_w_down, mix_norm, w_in, s5_lambda_re, s5_lambda_im, s5_log_dt, s5_b_re, s5_b_im, s5_c_re, s5_c_im, s5_d, s5_w_glu, gla_w_alpha, gla_b_alpha, gla_norm, attn_q_norm, attn_k_norm, w_branch_s5, w_branch_gla, w_branch_attn, w_merge_gate, b_merge_gate, w_out, ffn2_norm, ffn2_w_gate, ffn2_w_up, ffn2_w_down, final_norm, loss_target, m_ffn1_norm, m_ffn1_w_gate, m_ffn1_w_up, m_ffn1_w_down, m_mix_norm, m_w_in, m_s5_lambda_re, m_s5_lambda_im, m_s5_log_dt, m_s5_b_re, m_s5_b_im, m_s5_c_re, m_s5_c_im, m_s5_d, m_s5_w_glu, m_gla_w_alpha, m_gla_b_alpha, m_gla_norm, m_attn_q_norm, m_attn_k_norm, m_w_branch_s5, m_w_branch_gla, m_w_branch_attn, m_w_merge_gate, m_b_merge_gate, m_w_out, m_ffn2_norm, m_ffn2_w_gate, m_ffn2_w_up, m_ffn2_w_down, m_final_norm, v_ffn1_norm, v_ffn1_w_gate, v_ffn1_w_up, v_ffn1_w_down, v_mix_norm, v_w_in, v_s5_lambda_re, v_s5_lambda_im, v_s5_log_dt, v_s5_b_re, v_s5_b_im, v_s5_c_re, v_s5_c_im, v_s5_d, v_s5_w_glu, v_gla_w_alpha, v_gla_b_alpha, v_gla_norm, v_attn_q_norm, v_attn_k_norm, v_w_branch_s5, v_w_branch_gla, v_w_branch_attn, v_w_merge_gate, v_b_merge_gate, v_w_out, v_ffn2_norm, v_ffn2_w_gate, v_ffn2_w_up, v_ffn2_w_down, v_final_norm)


def _train_step(*args):
    nw = len(W_NAMES)
    x, target = args[0][0], args[1 + nw][0]
    w = dict(zip(W_NAMES, args[1:1 + nw]))
    m = dict(zip(W_NAMES, args[2 + nw:2 + 2 * nw]))
    v = dict(zip(W_NAMES, args[2 + 2 * nw:2 + 3 * nw]))

    loss, dx, per_layer, stacked = _step_local(x, target, w, {n: w[n].astype(BF16) for n in SHARDED})
    loss = lax.psum(loss, ("x", "y", "c"))

    out = {}
    kinds = ('grad', 'delta', 'new_m', 'new_v')
    core = lax.axis_index("c")
    own, for_sibling = [], []
    for n in SHARDED:
        by_owner = jnp.stack([_split_shards(g[n], SHARD_AXIS[n] - 1) for g in per_layer], axis=1).astype(BF16)
        by_owner = by_owner.reshape((N_CHIP, 2) + by_owner.shape[1:])
        own.append(lax.dynamic_index_in_dim(by_owner, core, axis=1, keepdims=False))
        for_sibling.append(lax.dynamic_index_in_dim(by_owner, 1 - core, axis=1, keepdims=False))
    from_sibling = _swap_with_sibling(for_sibling, name="exchange_grads_sibling")
    chip_sums = [_pair_sum(a.reshape(-1, a.shape[-1]), b.reshape(-1, b.shape[-1])).reshape(a.shape)
                 for a, b in zip(own, from_sibling)]
    incoming = _exchange_chips(chip_sums, name="exchange_grads_chips")
    for n, parts in zip(SHARDED, incoming):
        shape = w[n].shape
        flat = lambda a: a.reshape(-1, shape[-1])
        res = _reduce_adamw(parts.reshape(N_CHIP, -1, shape[-1]), flat(w[n]), flat(m[n]), flat(v[n]), name="adamw_sharded")
        for kind, a in zip(kinds, res):
            out[kind + '_' + n] = a.reshape(shape)
    small = [stacked[n] if n in stacked else jnp.stack([g[n] for g in per_layer]) for n in REPLICATED]
    parts = _all_gather([_pack_small(small)], name="gather_small_grads")[0]
    res = _reduce_adamw(parts, *[_pack_small([d[n] for n in REPLICATED]) for d in (w, m, v)], name="adamw_replicated")
    for kind, packed in zip(kinds, res):
        for n, a in zip(REPLICATED, _unpack_small(packed, [w[n].shape for n in REPLICATED])):
            out[kind + '_' + n] = a
    return (loss, dx[None]) + tuple(out[kind + '_' + n] for kind in kinds for n in W_NAMES)
```

```python
import functools
import math

import jax
import jax.numpy as jnp
import numpy as np
from jax import lax
from jax.experimental import pallas as pl
from jax.experimental.pallas import tpu as pltpu

F32 = jnp.float32
BF16 = jnp.bfloat16

N_DEV = 8
D_MODEL = 1024
DEPTH = 2
GRID_W = 64
D_FF = 2816
NORM_EPS = 1e-6
S5_GROUPS = 32
S5_GROUP_CH = 16
S5_STATE = 64
S5_WIDTH = 512
S5_NSTATE = S5_GROUPS * S5_STATE
S5_LANE_BLOCK = 512
GLA_HEADS = 4
GLA_HEAD_DIM = 128
GLA_WIDTH = 512
GLA_LOWRANK = 16
GLA_TAU = 16.0
GLA_CHUNK = 64
ATTN_Q_HEADS = 8
ATTN_KV_HEADS = 2
ATTN_HEAD_DIM = 64
ATTN_WIDTH = 512
ATTN_KV_WIDTH = 128
ROPE_BASE = 10000.0
IN_SPLITS = (512, 512, 512, 512, 512, 16, 16, 512, 128, 128)
IN_WIDTH = sum(IN_SPLITS)
IN_PAD = 3584
GATE_WIDTH = 3 * D_MODEL
PG_WIDTH = GATE_WIDTH + IN_PAD
P_OFF = GATE_WIDTH
CB_U, CB_GQ, CB_GK, CB_GV, CB_GG, CB_AQ = (P_OFF // 512 + i for i in range(6))
CB_AK, CB_AV, CB_Z = (P_OFF + 3072) // 128, (P_OFF + 3200) // 128, (P_OFF + 3328) // 128
ADAM_LR = 0.001
ADAM_B1 = 0.9
ADAM_B2 = 0.999
ADAM_EPS = 1e-08
ADAM_WD = 0.01
ADAM_STEP = 10
PACK_COLS = 1024

W_NAMES = ['ffn1_norm', 'ffn1_w_gate', 'ffn1_w_up', 'ffn1_w_down', 'mix_norm', 'w_in', 's5_lambda_re', 's5_lambda_im',
           's5_log_dt', 's5_b_re', 's5_b_im', 's5_c_re', 's5_c_im', 's5_d', 's5_w_glu', 'gla_w_alpha', 'gla_b_alpha',
           'gla_norm', 'attn_q_norm', 'attn_k_norm', 'w_branch_s5', 'w_branch_gla', 'w_branch_attn', 'w_merge_gate',
           'b_merge_gate', 'w_out', 'ffn2_norm', 'ffn2_w_gate', 'ffn2_w_up', 'ffn2_w_down', 'final_norm']
SHARD_AXIS = {'ffn1_w_gate': 2, 'ffn1_w_up': 2, 'ffn1_w_down': 1, 'w_in': 2, 's5_w_glu': 1, 'gla_w_alpha': 3,
              'gla_b_alpha': 2, 'w_branch_s5': 2, 'w_branch_gla': 2, 'w_branch_attn': 2, 'w_merge_gate': 2,
              'w_out': 1, 'ffn2_w_gate': 2, 'ffn2_w_up': 2, 'ffn2_w_down': 1}
SHARDED = [n for n in W_NAMES if n in SHARD_AXIS]
REPLICATED = [n for n in W_NAMES if n not in SHARD_AXIS]


def _pick(dim, prefs):
    for p in prefs:
        if dim % p == 0:
            return p
    return dim


def _sigmoid(x):
    return 0.5 * jnp.tanh(0.5 * x) + 0.5


def _mm(a, b, *, ta=False, tb=False, out_dtype=F32, scale=None, add=None, side=None, name):
    a, a_cb, a_w = a if isinstance(a, tuple) else (a, 0, a.shape[1])
    b, b_cb, b_w = b if isinstance(b, tuple) else (b, 0, b.shape[1])
    m, k = (a_w, a.shape[0]) if ta else (a.shape[0], a_w)
    n = b.shape[0] if tb else b_w
    assert (b_w if tb else b.shape[0]) == k, (a.shape, b.shape, ta, tb)
    tm, tn, tk = _mm_tiles(m, n, k, a.dtype.itemsize, b.dtype.itemsize, jnp.dtype(out_dtype).itemsize)
    nk = k // tk
    dims = (((0 if ta else 1,), (1 if tb else 0,)), ((), ()))
    a_off = a_cb * (a_w // (tm if ta else tk))
    b_off = b_cb * (b_w // (tk if tb else tn))

    grid = (m // tm, n // tn, nk)
    n_in = 2 if add is None else 3

    def body(*refs):
        _carried(side, grid, refs, n_in, 1, int(nk > 1), compute)

    def compute(refs):
        a_ref, b_ref, *rest = refs
        add_ref = rest[0] if add is not None else None
        o_ref, *acc = rest[1:] if add is not None else rest

        def finish(res):
            res = res if scale is None else res * scale
            return (res if add_ref is None else res + add_ref[...]).astype(out_dtype)

        part = lax.dot_general(a_ref[...].astype(BF16), b_ref[...].astype(BF16), dims, preferred_element_type=F32)
        if nk == 1:
            o_ref[...] = finish(part)
            return
        acc_ref, = acc
        kk = pl.program_id(2)

        @pl.when(kk == 0)
        def _():
            acc_ref[...] = part

        @pl.when(kk > 0)
        def _():
            acc_ref[...] += part

        @pl.when(kk == nk - 1)
        def _():
            o_ref[...] = finish(acc_ref[...])

    a_spec = (pl.BlockSpec((tk, tm), lambda i, j, kk: (kk, i + a_off)) if ta
              else pl.BlockSpec((tm, tk), lambda i, j, kk: (i, kk + a_off)))
    b_spec = (pl.BlockSpec((tn, tk), lambda i, j, kk: (j, kk + b_off)) if tb
              else pl.BlockSpec((tk, tn), lambda i, j, kk: (kk, j + b_off)))
    o_spec = pl.BlockSpec((tm, tn), lambda i, j, kk: (i, j))
    (out,), gathered = _side_call(
        body, side, name=name, grid=grid, in_specs=[a_spec, b_spec] + ([o_spec] if add is not None else []),
        out_specs=[o_spec], out_shape=[jax.ShapeDtypeStruct((m, n), out_dtype)],
        scratch=[pltpu.VMEM((tm, tn), F32)] if nk > 1 else [], args=[a, b] + ([add] if add is not None else []),
        semantics=("parallel", "parallel", "arbitrary"))
    return out if side is None else (out, gathered)


MM_VMEM_BUDGET = 40 * 1024 * 1024


def _mm_tiles(m, n, k, a_bytes, b_bytes, out_bytes):
    tms = [t for t in (1024, 1408, 512, 256, 128) if m % t == 0] or [m]
    tns = [t for t in (512, 1408, 256, 128) if n % t == 0] or [n]
    tks = [k] + [t for t in (2048, 1024, 512, 256, 128) if k % t == 0 and t < k]
    for tk in tks:
        for tm in tms:
            for tn in tns:
                use = 2 * (tm * tk * a_bytes + tk * tn * b_bytes + tm * tn * out_bytes) + 2 * tm * tn * 4
                if use <= MM_VMEM_BUDGET:
                    return tm, tn, tk
    return tms[-1], tns[-1], tks[-1]


def _rowmap(fn, rows, consts, outs, reds=(), *, tl, name):
    rows = [r if isinstance(r, tuple) else (r, 0, r.shape[1]) for r in rows]
    length = rows[0][0].shape[0]
    tl = min(tl, length)
    nr, nc, no = len(rows), len(consts), len(outs)

    def body(*refs):
        res = fn(*[r[...] for r in refs[:nr + nc]])
        res = res if isinstance(res, tuple) else (res,)
        for o_ref, val in zip(refs[nr + nc:nr + nc + no], res[:no]):
            o_ref[...] = val.astype(o_ref.dtype)
        if reds:
            step = pl.program_id(0)
            red_refs = refs[nr + nc + no:]

            @pl.when(step == 0)
            def _():
                for d_ref, val in zip(red_refs, res[no:]):
                    d_ref[...] = val.astype(F32)

            @pl.when(step > 0)
            def _():
                for d_ref, val in zip(red_refs, res[no:]):
                    d_ref[...] += val.astype(F32)

    in_specs = [pl.BlockSpec((tl, w), lambda i, cb=cb: (i, cb)) for (_, cb, w) in rows]
    in_specs += [pl.BlockSpec(c.shape, lambda i, nd=c.ndim: (0,) * nd) for c in consts]
    out_specs = [pl.BlockSpec((tl, w), lambda i: (i, 0)) for (w, _) in outs]
    out_specs += [pl.BlockSpec(s, lambda i, nd=len(s): (0,) * nd) for s in reds]
    out_shape = [jax.ShapeDtypeStruct((length, w), dt) for (w, dt) in outs]
    out_shape += [jax.ShapeDtypeStruct(s, F32) for s in reds]
    res = pl.pallas_call(
        body, name=name, grid=(length // tl,), in_specs=in_specs, out_specs=out_specs, out_shape=out_shape,
        compiler_params=pltpu.CompilerParams(dimension_semantics=("arbitrary" if reds else "parallel",)),
    )(*[r[0] for r in rows], *consts)
    return res


def _rms(x):
    return lax.rsqrt(jnp.mean(x * x, axis=-1, keepdims=True) + NORM_EPS)


def _rmsnorm_fwd(x, gain):
    def fn(xv, g):
        return xv * _rms(xv) * g
    return _rowmap(fn, [x], [gain.reshape(1, -1)], [(x.shape[1], BF16)], tl=256, name="rmsnorm_fwd")[0]


def _rmsnorm_bwd(x, gain, dh, dres):
    def fn(xv, dhv, drv, g):
        r = _rms(xv)
        gd = dhv * g
        dx = r * gd - xv * (r * r * r) * jnp.mean(xv * gd, axis=-1, keepdims=True)
        return drv + dx, jnp.sum(dhv * xv * r, axis=0, keepdims=True)
    dx, dg = _rowmap(fn, [x, dh, dres], [gain.reshape(1, -1)], [(x.shape[1], F32)], [(1, x.shape[1])], tl=256,
                     name="rmsnorm_bwd")
    return dx, dg[0]


FFN_UNIT = D_FF // 2


def _side_call(body, side, *, name, grid, in_specs, out_specs, out_shape, scratch, args, semantics):
    if side is not None:
        in_specs, out_specs = in_specs + side.in_specs, out_specs + side.out_specs
        out_shape, scratch, args = out_shape + side.out_shape, scratch + side.scratch, list(args) + side.blocks
        semantics = ("arbitrary",) * len(grid)
    res = pl.pallas_call(body, name=name, grid=grid, in_specs=in_specs, out_specs=out_specs, out_shape=out_shape,
                         scratch_shapes=scratch, compiler_params=pltpu.CompilerParams(dimension_semantics=semantics))(*args)
    n_own = len(res) - (side.n if side is not None else 0)
    return res[:n_own], res[n_own:]


def _ffn_up(h, w_gate, w_up, side=None):
    length, k = h.shape
    tm = _pick(length, (512, 256, 128))
    grid = (D_FF // FFN_UNIT, length // tm)

    def compute(refs):
        h_ref, wg_ref, wu_ref, a_ref, g_ref, u_ref = refs
        hv = h_ref[...]
        g = jnp.dot(hv, wg_ref[...], preferred_element_type=F32)
        u = jnp.dot(hv, wu_ref[...], preferred_element_type=F32)
        a_ref[...] = (g * _sigmoid(g) * u).astype(BF16)
        g_ref[...] = g.astype(BF16)
        u_ref[...] = u.astype(BF16)

    def body(*refs):
        _carried(side, grid, refs, 3, 3, 0, compute)

    w_spec = pl.BlockSpec((k, FFN_UNIT), lambda j, i: (0, j))
    o_spec = pl.BlockSpec((tm, FFN_UNIT), lambda j, i: (i, j))
    return _side_call(
        body, side, name="ffn_up", grid=grid, in_specs=[pl.BlockSpec((tm, k), lambda j, i: (i, 0)), w_spec, w_spec],
        out_specs=[o_spec] * 3, out_shape=[jax.ShapeDtypeStruct((length, D_FF), BF16)] * 3, scratch=[],
        args=[h, w_gate, w_up], semantics=("parallel", "parallel"))


def _ffn_dgu(dxo, w_down, g, u):
    length, k = dxo.shape
    tm = _pick(length, (512, 256, 128))

    def body(d_ref, w_ref, g_ref, u_ref, dg_ref, du_ref):
        da = 0.5 * lax.dot_general(d_ref[...], w_ref[...], _NT, preferred_element_type=F32)
        gv = g_ref[...].astype(F32)
        s = _sigmoid(gv)
        dg_ref[...] = (da * u_ref[...].astype(F32) * (s * (1.0 + gv * (1.0 - s)))).astype(BF16)
        du_ref[...] = (da * (gv * s)).astype(BF16)

    o_spec = pl.BlockSpec((tm, FFN_UNIT), lambda j, i: (i, j))
    return pl.pallas_call(
        body, name="ffn_dgu", grid=(D_FF // FFN_UNIT, length // tm),
        in_specs=[pl.BlockSpec((tm, k), lambda j, i: (i, 0)), pl.BlockSpec((FFN_UNIT, k), lambda j, i: (j, 0)), o_spec, o_spec],
        out_specs=[o_spec] * 2, out_shape=[jax.ShapeDtypeStruct((length, D_FF), BF16)] * 2,
        compiler_params=pltpu.CompilerParams(dimension_semantics=("parallel", "parallel")),
    )(dxo, w_down, g, u)


def _ffn_fwd(x, gain, w_gate, w_up, w_down, side=None):
    h = _rmsnorm_fwd(x, gain)
    (a, g, u), gathered = _ffn_up(h, w_gate, w_up, side)
    x_out = _mm(a, w_down, scale=0.5, add=x, name="ffn_down")
    return x_out, (x, h, g, u, a), gathered


def _ffn_bwd(saved, gain, w_gate, w_up, w_down, dx_out):
    x, h, g, u, a = saved
    dxo = dx_out.astype(BF16)
    d_wdown = _mm(a, dxo, ta=True, scale=0.5, out_dtype=BF16, name="ffn_dwdown")
    dg, du = _ffn_dgu(dxo, w_down, g, u)
    d_wgate = _mm(h, dg, ta=True, out_dtype=BF16, name="ffn_dwgu")
    d_wup = _mm(h, du, ta=True, out_dtype=BF16, name="ffn_dwgu")
    dh = _mm(du, w_up, tb=True, add=_mm(dg, w_gate, tb=True, name="ffn_dh"), name="ffn_dh_add")
    dx, dgain = _rmsnorm_bwd(x, gain, dh, dx_out)
    return dx, dgain, d_wgate, d_wup, d_wdown


def _s5_col(n):
    return (n // S5_LANE_BLOCK) * 2 * S5_LANE_BLOCK + n % S5_LANE_BLOCK


def _s5_blocked(re, im):
    lead = re.shape[:-1]
    nb = S5_NSTATE // S5_LANE_BLOCK
    both = jnp.stack([re.reshape(*lead, nb, S5_LANE_BLOCK), im.reshape(*lead, nb, S5_LANE_BLOCK)], axis=-2)
    return both.reshape(*lead, 2 * S5_NSTATE)


def _s5_unblocked(z):
    lead = z.shape[:-1]
    nb = S5_NSTATE // S5_LANE_BLOCK
    both = z.reshape(*lead, nb, 2, S5_LANE_BLOCK)
    return both[..., 0, :].reshape(*lead, S5_NSTATE), both[..., 1, :].reshape(*lead, S5_NSTATE)


def _s5_tables(a_re, a_im, reverse):
    a = lax.complex(a_re, a_im)
    a2 = a * a
    a4 = a2 * a2
    rows = jnp.arange(8)
    pw = [a]
    for _ in range(7):
        pw.append(pw[-1] * a)
    pw = jnp.stack(pw)
    if reverse:
        pw = pw[::-1]
    tabs = []
    for coef, s in ((a, 1), (a2, 2), (a4, 4)):
        live = (rows <= 7 - s) if reverse else (rows >= s)
        tabs.append(jnp.where(live[:, None], coef[None, :], 0.0))
    tabs.append(pw)
    tabs = jnp.stack(tabs)
    return _s5_blocked(jnp.real(tabs), jnp.imag(tabs))


def _s5_scan_tile(v, tab_ref, prev, reverse):
    lb = S5_LANE_BLOCK
    vr, vi = v[:, :lb], v[:, lb:]
    for idx, s in enumerate((1, 2, 4)):
        cr, ci = tab_ref[idx, :, :lb], tab_ref[idx, :, lb:]
        sh = 8 - s if reverse else s
        sr, si = pltpu.roll(vr, sh, 0), pltpu.roll(vi, sh, 0)
        vr, vi = vr + cr * sr - ci * si, vi + cr * si + ci * sr
    row = 0 if reverse else 7
    pr = jnp.broadcast_to(prev[row:row + 1, :lb], (8, lb))
    pi = jnp.broadcast_to(prev[row:row + 1, lb:], (8, lb))
    cr, ci = tab_ref[3, :, :lb], tab_ref[3, :, lb:]
    return jnp.concatenate([vr + cr * pr - ci * pi, vi + cr * pi + ci * pr], axis=1)


def _s5_scan(v, tabs, reverse, *, name):
    length = v.shape[0]
    tb = min(512, length)
    ntb = length // tb
    nlb = S5_NSTATE // S5_LANE_BLOCK
    wb = 2 * S5_LANE_BLOCK
    ntile = tb // 8

    def body(tab_ref, v_ref, x_ref, carry_ref):
        @pl.when(pl.program_id(1) == 0)
        def _():
            carry_ref[...] = jnp.zeros_like(carry_ref)

        def step(i, prev):
            r0 = pl.multiple_of((ntile - 1 - i if reverse else i) * 8, 8)
            x = _s5_scan_tile(v_ref[pl.ds(r0, 8), :], tab_ref, prev, reverse)
            x_ref[pl.ds(r0, 8), :] = x
            return x

        carry_ref[...] = lax.fori_loop(0, ntile, step, carry_ref[...])

    tmap = (lambda c, t: (ntb - 1 - t, c)) if reverse else (lambda c, t: (t, c))
    return pl.pallas_call(
        body, name=name, grid=(nlb, ntb),
        in_specs=[pl.BlockSpec((4, 8, wb), lambda c, t: (0, 0, c)), pl.BlockSpec((tb, wb), tmap)],
        out_specs=pl.BlockSpec((tb, wb), tmap), out_shape=jax.ShapeDtypeStruct(v.shape, F32),
        scratch_shapes=[pltpu.VMEM((8, wb), F32)],
        compiler_params=pltpu.CompilerParams(dimension_semantics=("parallel", "arbitrary")),
    )(tabs, v)


def _s5_scan_adjoint(g, xs, tabs_conj, reverse, *, name):
    length = g.shape[0]
    tb = min(512, length)
    ntb = length // tb
    nlb = S5_NSTATE // S5_LANE_BLOCK
    lb = S5_LANE_BLOCK
    wb = 2 * lb
    ntile = tb // 8
    adj_rev = not reverse
    if reverse:
        edge = jnp.concatenate([xs[tb::tb], jnp.zeros((1, xs.shape[1]), F32)], axis=0)
    else:
        edge = jnp.concatenate([jnp.zeros((1, xs.shape[1]), F32), xs[tb - 1:length - 1:tb]], axis=0)
    edge = edge.reshape(ntb, 1, xs.shape[1])

    def body(tab_ref, g_ref, x_ref, edge_ref, lam_ref, da_ref, carry_ref):
        @pl.when(pl.program_id(1) == 0)
        def _():
            carry_ref[...] = jnp.zeros_like(carry_ref)
            da_ref[...] = jnp.zeros_like(da_ref)

        rows = lax.broadcasted_iota(jnp.int32, (8, wb), 0)

        def step(i, carry):
            prev, acc = carry
            k = ntile - 1 - i if adj_rev else i
            r0 = pl.multiple_of(k * 8, 8)
            lam = _s5_scan_tile(g_ref[pl.ds(r0, 8), :], tab_ref, prev, adj_rev)
            lam_ref[pl.ds(r0, 8), :] = lam
            x = x_ref[pl.ds(r0, 8), :]
            if reverse:
                kn = jnp.minimum(k + 1, ntile - 1)
                nb = x_ref[pl.ds(pl.multiple_of(kn * 8, 8), 8), :][0:1, :]
                nb = jnp.where(k == ntile - 1, edge_ref[0], nb)
                xp = jnp.where(rows == 7, jnp.broadcast_to(nb, (8, wb)), pltpu.roll(x, 7, 0))
            else:
                kn = jnp.maximum(k - 1, 0)
                nb = x_ref[pl.ds(pl.multiple_of(kn * 8, 8), 8), :][7:8, :]
                nb = jnp.where(k == 0, edge_ref[0], nb)
                xp = jnp.where(rows == 0, jnp.broadcast_to(nb, (8, wb)), pltpu.roll(x, 1, 0))
            xr, xi, lr, li = xp[:, :lb], xp[:, lb:], lam[:, :lb], lam[:, lb:]
            acc = acc + jnp.concatenate([xr * lr + xi * li, xr * li - xi * lr], axis=1)
            return lam, acc

        last, acc = lax.fori_loop(0, ntile, step, (carry_ref[...], da_ref[...]))
        carry_ref[...] = last
        da_ref[...] = acc

    tmap = (lambda c, t: (ntb - 1 - t, c)) if adj_rev else (lambda c, t: (t, c))
    emap = (lambda c, t: (ntb - 1 - t, 0, c)) if adj_rev else (lambda c, t: (t, 0, c))
    return pl.pallas_call(
        body, name=name, grid=(nlb, ntb),
        in_specs=[pl.BlockSpec((4, 8, wb), lambda c, t: (0, 0, c)), pl.BlockSpec((tb, wb), tmap),
                  pl.BlockSpec((tb, wb), tmap), pl.BlockSpec((1, 1, wb), emap)],
        out_specs=[pl.BlockSpec((tb, wb), tmap), pl.BlockSpec((8, wb), lambda c, t: (0, c))],
        out_shape=[jax.ShapeDtypeStruct(g.shape, F32), jax.ShapeDtypeStruct((8, g.shape[1]), F32)],
        scratch_shapes=[pltpu.VMEM((8, wb), F32)],
        compiler_params=pltpu.CompilerParams(dimension_semantics=("parallel", "arbitrary")),
    )(tabs_conj, g, xs, edge)


def _s5_prep(lam_re, lam_im, log_dt, b_re, b_im):
    lam = lax.complex(lam_re, lam_im)
    dt = jnp.exp(log_dt)[:, None]
    lam_bar = jnp.exp(lam * dt)
    b_bar = ((lam_bar - 1.0) / lam)[..., None] * lax.complex(b_re, b_im)
    return (jnp.real(lam_bar).reshape(-1), jnp.imag(lam_bar).reshape(-1), jnp.real(b_bar), jnp.imag(b_bar))


S5_NBLK = S5_NSTATE // S5_LANE_BLOCK
S5_BLK_GROUPS = S5_GROUPS // S5_NBLK
S5_BLK_CH = S5_BLK_GROUPS * S5_GROUP_CH


def _s5_in_matrix(bb_re, bb_im):
    eye = jnp.eye(S5_BLK_GROUPS, dtype=F32)
    def dense(bb):
        b4 = bb.reshape(S5_NBLK, S5_BLK_GROUPS, S5_STATE, S5_GROUP_CH)
        return jnp.einsum('cgph,gk->cghkp', b4, eye).reshape(S5_NBLK, S5_BLK_CH, S5_LANE_BLOCK)
    return jnp.concatenate([dense(bb_re), dense(bb_im)], axis=-1)


def _s5_block_diagonal(d):
    d5 = d.reshape(S5_NBLK, S5_BLK_GROUPS, S5_GROUP_CH, S5_BLK_GROUPS, S5_STATE)
    eye = jnp.eye(S5_BLK_GROUPS, dtype=F32)
    return jnp.swapaxes(jnp.sum(d5 * eye[None, :, None, :, None], axis=1), 1, 2)


def _s5_in_matrix_grad(d_mat):
    def diag(d):
        return jnp.swapaxes(_s5_block_diagonal(d), 2, 3).reshape(S5_GROUPS, S5_STATE, S5_GROUP_CH)
    return diag(d_mat[..., :S5_LANE_BLOCK]), diag(d_mat[..., S5_LANE_BLOCK:])


def _s5_out_matrix(c_re, c_im):
    eye = jnp.eye(S5_BLK_GROUPS, dtype=F32)
    def dense(cc):
        c4 = cc.reshape(S5_NBLK, S5_BLK_GROUPS, S5_GROUP_CH, S5_STATE)
        return jnp.einsum('cghp,gk->cgpkh', c4, eye).reshape(S5_NBLK, S5_LANE_BLOCK, S5_BLK_CH)
    return jnp.concatenate([dense(c_re), dense(-c_im)], axis=1)


def _s5_out_matrix_grad(d_mat_t):
    def diag(d):
        return _s5_block_diagonal(d).reshape(S5_GROUPS, S5_GROUP_CH, S5_STATE)
    return diag(d_mat_t[..., :S5_LANE_BLOCK]), -diag(d_mat_t[..., S5_LANE_BLOCK:])


def _gmm(a, b, *, tb=False, name):
    arr, cb0, wa = a
    nblk = b.shape[0]
    wn = b.shape[1] if tb else b.shape[2]
    length = arr.shape[0]
    tm = _pick(length, (1024, 512, 256, 128))
    dims = (((1,), (1 if tb else 0,)), ((), ()))

    def body(a_ref, b_ref, o_ref):
        o_ref[...] = lax.dot_general(a_ref[...].astype(BF16), b_ref[0].astype(BF16), dims, preferred_element_type=F32)

    return pl.pallas_call(
        body, name=name, grid=(nblk, length // tm),
        in_specs=[pl.BlockSpec((tm, wa), lambda c, i: (i, cb0 + c)), pl.BlockSpec((1,) + b.shape[1:], lambda c, i: (c, 0, 0))],
        out_specs=pl.BlockSpec((tm, wn), lambda c, i: (i, c)), out_shape=jax.ShapeDtypeStruct((length, nblk * wn), F32),
        compiler_params=pltpu.CompilerParams(dimension_semantics=("parallel", "parallel")),
    )(arr, b)


def _gmm_tn(a, g, *, nblk, name):
    arr_a, cb_a, wa = a
    arr_g, cb_g, wg = g
    length = arr_a.shape[0]
    dims = (((0,), (0,)), ((), ()))

    def body(a_ref, g_ref, o_ref):
        o_ref[0] = lax.dot_general(a_ref[...].astype(BF16), g_ref[...].astype(BF16), dims, preferred_element_type=F32)

    return pl.pallas_call(
        body, name=name, grid=(nblk,),
        in_specs=[pl.BlockSpec((length, wa), lambda c: (0, cb_a + c)), pl.BlockSpec((length, wg), lambda c: (0, cb_g + c))],
        out_specs=pl.BlockSpec((1, wa, wg), lambda c: (c, 0, 0)), out_shape=jax.ShapeDtypeStruct((nblk, wa, wg), F32),
        compiler_params=pltpu.CompilerParams(dimension_semantics=("parallel",)),
    )(arr_a, arr_g)


def _gelu_parts(x):
    k = math.sqrt(2.0 / math.pi)
    inner = k * (x + 0.044715 * x * x * x)
    th = jnp.tanh(inner)
    return th, k * (1.0 + 3.0 * 0.044715 * x * x)


S5_CB_U = CB_U * (512 // S5_BLK_CH)


def _s5_direction_fwd(pg, b_mat, c_mat, tabs, reverse, *, name):
    length = pg.shape[0]
    tb = min(512, length)
    ntb = length // tb
    wb = 2 * S5_LANE_BLOCK
    ntile = tb // 8

    def body(tab_ref, u_ref, b_ref, c_ref, x_ref, y_ref, carry_ref, bu_ref):
        @pl.when(pl.program_id(1) == 0)
        def _():
            carry_ref[...] = jnp.zeros_like(carry_ref)

        bu_ref[...] = jnp.dot(u_ref[...].astype(BF16), b_ref[0], preferred_element_type=F32)

        def step(i, prev):
            r0 = pl.multiple_of((ntile - 1 - i if reverse else i) * 8, 8)
            x = _s5_scan_tile(bu_ref[pl.ds(r0, 8), :], tab_ref, prev, reverse)
            x_ref[pl.ds(r0, 8), :] = x
            return x

        carry_ref[...] = lax.fori_loop(0, ntile, step, carry_ref[...])
        y_ref[...] = jnp.dot(x_ref[...].astype(BF16), c_ref[0], preferred_element_type=F32)

    tix = (lambda t: ntb - 1 - t) if reverse else (lambda t: t)
    return pl.pallas_call(
        body, name=name, grid=(S5_NBLK, ntb),
        in_specs=[pl.BlockSpec((4, 8, wb), lambda c, t: (0, 0, c)),
                  pl.BlockSpec((tb, S5_BLK_CH), lambda c, t: (tix(t), S5_CB_U + c)),
                  pl.BlockSpec((1, S5_BLK_CH, wb), lambda c, t: (c, 0, 0)),
                  pl.BlockSpec((1, wb, S5_BLK_CH), lambda c, t: (c, 0, 0))],
        out_specs=[pl.BlockSpec((tb, wb), lambda c, t: (tix(t), c)), pl.BlockSpec((tb, S5_BLK_CH), lambda c, t: (tix(t), c))],
        out_shape=[jax.ShapeDtypeStruct((length, S5_NBLK * wb), F32), jax.ShapeDtypeStruct((length, S5_WIDTH), F32)],
        scratch_shapes=[pltpu.VMEM((8, wb), F32), pltpu.VMEM((tb, wb), F32)],
        compiler_params=pltpu.CompilerParams(dimension_semantics=("parallel", "arbitrary")),
    )(tabs, pg, b_mat, c_mat)


def _s5_direction_bwd(pg, dy, xs, b_mat, c_mat, tabs_conj, reverse, *, name):
    length = pg.shape[0]
    tb = min(512, length)
    ntb = length // tb
    lb = S5_LANE_BLOCK
    wb = 2 * lb
    ntile = tb // 8
    adj_rev = not reverse
    if reverse:
        edge = jnp.concatenate([xs[tb::tb], jnp.zeros((1, xs.shape[1]), F32)], axis=0)
    else:
        edge = jnp.concatenate([jnp.zeros((1, xs.shape[1]), F32), xs[tb - 1:length - 1:tb]], axis=0)
    edge = edge.reshape(ntb, 1, xs.shape[1])

    def body(tab_ref, u_ref, dy_ref, x_ref, edge_ref, b_ref, c_ref, du_ref, db_ref, dc_ref, da_ref, carry_ref, g_ref, lam_ref):
        @pl.when(pl.program_id(1) == 0)
        def _():
            carry_ref[...] = jnp.zeros_like(carry_ref)
            da_ref[...] = jnp.zeros_like(da_ref)
            db_ref[...] = jnp.zeros_like(db_ref)
            dc_ref[...] = jnp.zeros_like(dc_ref)

        dyb = dy_ref[...].astype(BF16)
        g_ref[...] = lax.dot_general(dyb, c_ref[0], _NT, preferred_element_type=F32)
        rows = lax.broadcasted_iota(jnp.int32, (8, wb), 0)

        def step(i, carry):
            prev, acc = carry
            k = ntile - 1 - i if adj_rev else i
            r0 = pl.multiple_of(k * 8, 8)
            lam = _s5_scan_tile(g_ref[pl.ds(r0, 8), :], tab_ref, prev, adj_rev)
            lam_ref[pl.ds(r0, 8), :] = lam
            x = x_ref[pl.ds(r0, 8), :]
            if reverse:
                kn = jnp.minimum(k + 1, ntile - 1)
                nb = x_ref[pl.ds(pl.multiple_of(kn * 8, 8), 8), :][0:1, :]
                nb = jnp.where(k == ntile - 1, edge_ref[0], nb)
                xp = jnp.where(rows == 7, jnp.broadcast_to(nb, (8, wb)), pltpu.roll(x, 7, 0))
            else:
                kn = jnp.maximum(k - 1, 0)
                nb = x_ref[pl.ds(pl.multiple_of(kn * 8, 8), 8), :][7:8, :]
                nb = jnp.where(k == 0, edge_ref[0], nb)
                xp = jnp.where(rows == 0, jnp.broadcast_to(nb, (8, wb)), pltpu.roll(x, 1, 0))
            xr, xi, lr, li = xp[:, :lb], xp[:, lb:], lam[:, :lb], lam[:, lb:]
            return lam, acc + jnp.concatenate([xr * lr + xi * li, xr * li - xi * lr], axis=1)

        last, acc = lax.fori_loop(0, ntile, step, (carry_ref[...], da_ref[...]))
        carry_ref[...] = last
        da_ref[...] = acc
        lamb = lam_ref[...].astype(BF16)
        du_ref[...] = lax.dot_general(lamb, b_ref[0], _NT, preferred_element_type=F32)
        db_ref[0] += lax.dot_general(u_ref[...].astype(BF16), lamb, _TN, preferred_element_type=F32)
        dc_ref[0] += lax.dot_general(dyb, x_ref[...].astype(BF16), _TN, preferred_element_type=F32)

    tix = (lambda t: ntb - 1 - t) if adj_rev else (lambda t: t)
    wide = pl.BlockSpec((tb, wb), lambda c, t: (tix(t), c))
    mat = pl.BlockSpec((1, S5_BLK_CH, wb), lambda c, t: (c, 0, 0))
    return pl.pallas_call(
        body, name=name, grid=(S5_NBLK, ntb),
        in_specs=[pl.BlockSpec((4, 8, wb), lambda c, t: (0, 0, c)),
                  pl.BlockSpec((tb, S5_BLK_CH), lambda c, t: (tix(t), S5_CB_U + c)),
                  pl.BlockSpec((tb, S5_BLK_CH), lambda c, t: (tix(t), c)), wide,
                  pl.BlockSpec((1, 1, wb), lambda c, t: (tix(t), 0, c)), mat,
                  pl.BlockSpec((1, wb, S5_BLK_CH), lambda c, t: (c, 0, 0))],
        out_specs=[pl.BlockSpec((tb, S5_BLK_CH), lambda c, t: (tix(t), c)), mat, mat, pl.BlockSpec((8, wb), lambda c, t: (0, c))],
        out_shape=[jax.ShapeDtypeStruct((length, S5_WIDTH), F32), jax.ShapeDtypeStruct((S5_NBLK, S5_BLK_CH, wb), F32),
                   jax.ShapeDtypeStruct((S5_NBLK, S5_BLK_CH, wb), F32), jax.ShapeDtypeStruct((8, S5_NBLK * wb), F32)],
        scratch_shapes=[pltpu.VMEM((8, wb), F32), pltpu.VMEM((tb, wb), F32), pltpu.VMEM((tb, wb), F32)],
        compiler_params=pltpu.CompilerParams(dimension_semantics=("parallel", "arbitrary")),
    )(tabs_conj, pg, dy, xs, edge, b_mat, c_mat)


def _both(fn):
    return jax.vmap(jax.vmap(fn))


def _s5_setup(w):
    a_re, a_im, bb_re, bb_im = _both(_s5_prep)(w['s5_lambda_re'], w['s5_lambda_im'], w['s5_log_dt'], w['s5_b_re'], w['s5_b_im'])

    def tables(d, conj, reverse):
        return jax.vmap(lambda r, i: _s5_tables(r, -i if conj else i, reverse))(a_re[:, d], a_im[:, d])
    return {'b_mat': _both(_s5_in_matrix)(bb_re, bb_im).astype(BF16),
            'c_mat': _both(_s5_out_matrix)(w['s5_c_re'], w['s5_c_im']).astype(BF16),
            'tabs': [tables(0, False, False), tables(1, False, True)],
            'tabs_adj': [tables(0, True, True), tables(1, True, False)]}


def _s5_param_grads(w, raws):
    def stacked(k):
        return jnp.stack([jnp.stack([raws[i][d][k] for d in range(2)]) for i in range(DEPTH)])
    dbb_re, dbb_im = _both(_s5_in_matrix_grad)(stacked(0))
    dc_re, dc_im = _both(_s5_out_matrix_grad)(stacked(1))
    da_re, da_im = _s5_unblocked(jnp.sum(stacked(2), axis=2))
    _, vjp = jax.vjp(_both(_s5_prep), w['s5_lambda_re'], w['s5_lambda_im'], w['s5_log_dt'], w['s5_b_re'], w['s5_b_im'])
    g = vjp((da_re, da_im, dbb_re, dbb_im))
    return {'s5_lambda_re': g[0], 's5_lambda_im': g[1], 's5_log_dt': g[2], 's5_b_re': g[3], 's5_b_im': g[4],
            's5_c_re': dc_re, 's5_c_im': dc_im}


def _s5_fwd(p_in, prm, w_glu):
    dirs = []
    ys = []
    for d, reverse in ((0, False), (1, True)):
        xs, y_dir = _s5_direction_fwd(p_in, prm['b_mat'][d], prm['c_mat'][d], prm['tabs'][d], reverse,
                                      name="s5_fwd_rev" if reverse else "s5_fwd")
        ys.append(y_dir)
        dirs.append(xs)

    def post(yf, yb, u, dskip):
        ypre = yf + yb + dskip * u
        th, _ = _gelu_parts(ypre)
        return ypre, 0.5 * ypre * (1.0 + th)
    ypre, yg = _rowmap(post, [ys[0], ys[1], (p_in, CB_U, S5_WIDTH)], [prm['d'].reshape(1, -1)],
                       [(S5_WIDTH, F32), (S5_WIDTH, F32)], tl=512, name="s5_post")
    t = _mm(yg, w_glu, name="s5_glu_mm")

    def glu(ygv, tv):
        return ygv * _sigmoid(tv)
    y = _rowmap(glu, [yg, t], [], [(S5_WIDTH, BF16)], tl=512, name="s5_glu")[0]
    return y, (dirs, ypre, yg, t)


def _s5_bwd(pg, prm, w_glu, saved, dy):
    dirs, ypre, yg, t = saved

    def glu_bwd(dyv, ygv, tv):
        s = _sigmoid(tv)
        return dyv * ygv * s * (1.0 - s), dyv * s
    dt, dyg_direct = _rowmap(glu_bwd, [dy, yg, t], [], [(S5_WIDTH, BF16), (S5_WIDTH, F32)], tl=512, name="s5_glu_bwd")
    grads = {'w_glu': _mm(yg, dt, ta=True, out_dtype=BF16, name="s5_dwglu")}
    dyg_mm = _mm(dt, w_glu, tb=True, name="s5_dyg")

    def post_bwd(dyd, dym, yp, u, dskip):
        th, dinner = _gelu_parts(yp)
        dyp = (dyd + dym) * (0.5 * (1.0 + th) + 0.5 * yp * (1.0 - th * th) * dinner)
        return dyp, dyp * dskip, jnp.sum(dyp * u, axis=0, keepdims=True)
    dyp, du_skip, dd = _rowmap(post_bwd, [dyg_direct, dyg_mm, ypre, (pg, CB_U, S5_WIDTH)], [prm['d'].reshape(1, -1)],
                               [(S5_WIDTH, F32), (S5_WIDTH, F32)], [(1, S5_WIDTH)], tl=512, name="s5_post_bwd")
    grads['d'] = dd[0]
    du = [du_skip]
    grads['raw'] = []
    for d, reverse in ((0, False), (1, True)):
        du_dir, d_bmat, d_cmat_t, da = _s5_direction_bwd(pg, dyp, dirs[d], prm['b_mat'][d], prm['c_mat'][d], prm['tabs_adj'][d],
                                                         reverse, name="s5_bwd_rev" if reverse else "s5_bwd")
        du.append(du_dir)
        grads['raw'].append((d_bmat, d_cmat_t, da))
    return du, grads


def _split3(x):
    hi = x.astype(BF16)
    r = x - hi.astype(F32)
    mid = r.astype(BF16)
    return hi, mid, (r - mid.astype(F32)).astype(BF16)


def _exact_dot(ones, x, dims):
    parts = [lax.dot_general(ones, p, dims, preferred_element_type=F32) for p in _split3(x)]
    return parts[0] + parts[1] + parts[2]


_NN = (((1,), (0,)), ((), ()))
_NT = (((1,), (1,)), ((), ()))
_TN = (((0,), (0,)), ((), ()))


def _dot(a, b, dims=_NN):
    return lax.dot_general(a.astype(BF16), b.astype(BF16), dims, preferred_element_type=F32)


def _gla_chunk_mask(reverse):
    rows = lax.broadcasted_iota(jnp.int32, (GLA_CHUNK, GLA_CHUNK), 0)
    cols = lax.broadcasted_iota(jnp.int32, (GLA_CHUNK, GLA_CHUNK), 1)
    return (cols >= rows) if reverse else (cols <= rows)


def _gla_fwd(pg, la, reverse, *, name):
    length = la.shape[0]
    nch = length // GLA_CHUNK
    scale = GLA_HEAD_DIM ** -0.5
    last = 0 if reverse else GLA_CHUNK - 1
    hd = GLA_HEAD_DIM

    def body(q_ref, k_ref, v_ref, la_ref, o_ref, sp_ref, st_ref):
        @pl.when(pl.program_id(0) == 0)
        def _():
            st_ref[...] = jnp.zeros_like(st_ref)

        mask = _gla_chunk_mask(reverse)
        b = _exact_dot(mask.astype(BF16), la_ref[...], _NN)
        sp_ref[0] = st_ref[...]
        outs = []
        for h in range(GLA_HEADS):
            sl = slice(h * hd, (h + 1) * hd)
            bh = b[:, sl]
            bl = bh[last:last + 1, :]
            k = k_ref[:, sl]
            v = v_ref[:, sl]
            qd = q_ref[:, sl] * scale * jnp.exp(bh)
            kd = k * jnp.exp(-bh)
            ke = k * jnp.exp(bl - bh)
            st = st_ref[sl, :]
            p = jnp.where(mask, _dot(qd, kd, _NT), 0.0)
            outs.append(_dot(p, v) + _dot(qd, st, _NT))
            st_ref[sl, :] = st * jnp.exp(bl) + _dot(v, ke, _TN)
        o_ref[...] = jnp.concatenate(outs, axis=1)

    cmap = (lambda n: nch - 1 - n) if reverse else (lambda n: n)
    col = lambda cb: pl.BlockSpec((GLA_CHUNK, GLA_WIDTH), lambda n, cb=cb: (cmap(n), cb))
    return pl.pallas_call(
        body, name=name, grid=(nch,),
        in_specs=[col(CB_GQ), col(CB_GK), col(CB_GV), col(0)],
        out_specs=[col(0), pl.BlockSpec((1, GLA_WIDTH, hd), lambda n: (cmap(n), 0, 0))],
        out_shape=[jax.ShapeDtypeStruct((length, GLA_WIDTH), F32), jax.ShapeDtypeStruct((nch, GLA_WIDTH, hd), F32)],
        scratch_shapes=[pltpu.VMEM((GLA_WIDTH, hd), F32)],
        compiler_params=pltpu.CompilerParams(dimension_semantics=("arbitrary",)),
    )(pg, pg, pg, la)


def _gla_bwd(pg, la, do, sprev, reverse, *, name):
    length = la.shape[0]
    nch = length // GLA_CHUNK
    scale = GLA_HEAD_DIM ** -0.5
    last = 0 if reverse else GLA_CHUNK - 1
    hd = GLA_HEAD_DIM

    def body(q_ref, k_ref, v_ref, la_ref, do_ref, sp_ref, dq_ref, dk_ref, dv_ref, dla_ref, dst_ref):
        @pl.when(pl.program_id(0) == 0)
        def _():
            dst_ref[...] = jnp.zeros_like(dst_ref)

        mask = _gla_chunk_mask(reverse)
        tri = mask.astype(BF16)
        b = _exact_dot(tri, la_ref[...], _NN)
        is_last = lax.broadcasted_iota(jnp.int32, (GLA_CHUNK, hd), 0) == last
        dqs, dks, dvs, dbs = [], [], [], []
        for h in range(GLA_HEADS):
            sl = slice(h * hd, (h + 1) * hd)
            bh = b[:, sl]
            bl = bh[last:last + 1, :]
            eb, enb, ebl, el = jnp.exp(bh), jnp.exp(-bh), jnp.exp(bl - bh), jnp.exp(bl)
            k = k_ref[:, sl]
            v = v_ref[:, sl]
            dov = do_ref[:, sl]
            qd = q_ref[:, sl] * scale * eb
            kd = k * enb
            ke = k * ebl
            st = sp_ref[0, sl, :]
            dst = dst_ref[sl, :]
            p = jnp.where(mask, _dot(qd, kd, _NT), 0.0)
            dp = jnp.where(mask, _dot(dov, v, _NT), 0.0)
            dqd = _dot(dp, kd) + _dot(dov, st)
            dkd = _dot(dp, qd, _TN)
            dvs.append(_dot(p, dov, _TN) + _dot(ke, dst, _NT))
            dke = _dot(v, dst)
            dst_ref[sl, :] = dst * el + _dot(dov, qd, _TN)
            dbl = el * jnp.sum(dst * st, axis=0, keepdims=True) + jnp.sum(dke * ke, axis=0, keepdims=True)
            db = dqd * qd - dkd * kd - dke * ke
            dbs.append(jnp.where(is_last, db + dbl, db))
            dqs.append(dqd * eb * scale)
            dks.append(dkd * enb + dke * ebl)
        dq_ref[...] = jnp.concatenate(dqs, axis=1)
        dk_ref[...] = jnp.concatenate(dks, axis=1)
        dv_ref[...] = jnp.concatenate(dvs, axis=1)
        tri_t = _gla_chunk_mask(not reverse).astype(BF16)
        dla_ref[...] = _exact_dot(tri_t, jnp.concatenate(dbs, axis=1), _NN)

    cmap = (lambda n: n) if reverse else (lambda n: nch - 1 - n)
    col = lambda cb: pl.BlockSpec((GLA_CHUNK, GLA_WIDTH), lambda n, cb=cb: (cmap(n), cb))
    wide = jax.ShapeDtypeStruct((length, GLA_WIDTH), F32)
    return pl.pallas_call(
        body, name=name, grid=(nch,),
        in_specs=[col(CB_GQ), col(CB_GK), col(CB_GV), col(0), col(0),
                  pl.BlockSpec((1, GLA_WIDTH, hd), lambda n: (cmap(n), 0, 0))],
        out_specs=[col(0)] * 4, out_shape=[wide] * 4,
        scratch_shapes=[pltpu.VMEM((GLA_WIDTH, hd), F32)],
        compiler_params=pltpu.CompilerParams(dimension_semantics=("arbitrary",)),
    )(pg, pg, pg, la, do, sprev)


def _log_sigmoid(x):
    return jnp.minimum(x, 0.0) - jnp.log(1.0 + jnp.exp(-jnp.abs(x)))


def _gla_alpha_padded(w_alpha):
    w = jnp.zeros((2, 128, GLA_WIDTH), w_alpha.dtype)
    w = w.at[0, 0:GLA_LOWRANK].set(w_alpha[0])
    return w.at[1, GLA_LOWRANK:2 * GLA_LOWRANK].set(w_alpha[1])


def _gla_branch_fwd(pg, w_alpha, b_alpha, norm_gain):
    wa = _gla_alpha_padded(w_alpha).astype(BF16)

    def gates(z, w, bias):
        return (_log_sigmoid(_dot(z, w[0]) + bias[0:1]) / GLA_TAU, _log_sigmoid(_dot(z, w[1]) + bias[1:2]) / GLA_TAU)
    la_f, la_b = _rowmap(gates, [(pg, CB_Z, 128)], [wa, b_alpha], [(GLA_WIDTH, F32), (GLA_WIDTH, F32)], tl=512,
                         name="gla_gates")
    o_f, sp_f = _gla_fwd(pg, la_f, False, name="gla_fwd")
    o_b, sp_b = _gla_fwd(pg, la_b, True, name="gla_fwd_rev")

    def post(of, ob, gate, gn):
        o = of + ob
        on = jnp.concatenate([o[:, s:s + GLA_HEAD_DIM] * _rms(o[:, s:s + GLA_HEAD_DIM]) * gn
                              for s in range(0, GLA_WIDTH, GLA_HEAD_DIM)], axis=1)
        return o, on * (gate * _sigmoid(gate))
    o, y = _rowmap(post, [o_f, o_b, (pg, CB_GG, GLA_WIDTH)], [norm_gain.reshape(1, -1)],
                   [(GLA_WIDTH, F32), (GLA_WIDTH, BF16)], tl=512, name="gla_post")
    return y, (wa, la_f, la_b, sp_f, sp_b, o)


def _gla_branch_bwd(pg, w_alpha, b_alpha, norm_gain, saved, dy):
    wa, la_f, la_b, sp_f, sp_b, o = saved

    def post_bwd(dyv, ov, gate, gn):
        s = _sigmoid(gate)
        dos, dgn, ons = [], [], []
        for c in range(0, GLA_WIDTH, GLA_HEAD_DIM):
            oh = ov[:, c:c + GLA_HEAD_DIM]
            r = _rms(oh)
            don = dyv[:, c:c + GLA_HEAD_DIM] * (gate[:, c:c + GLA_HEAD_DIM] * s[:, c:c + GLA_HEAD_DIM])
            gd = don * gn
            dos.append(r * gd - oh * (r * r * r) * jnp.mean(oh * gd, axis=-1, keepdims=True))
            dgn.append(jnp.sum(don * oh * r, axis=0, keepdims=True))
            ons.append(oh * r * gn)
        on = jnp.concatenate(ons, axis=1)
        dgate = dyv * on * (s * (1.0 + gate * (1.0 - s)))
        return jnp.concatenate(dos, axis=1), dgate, jnp.concatenate(dgn, axis=1)
    do, dgate, dgn = _rowmap(post_bwd, [dy, o, (pg, CB_GG, GLA_WIDTH)], [norm_gain.reshape(1, -1)],
                             [(GLA_WIDTH, F32), (GLA_WIDTH, F32)], [(1, GLA_WIDTH)], tl=512, name="gla_post_bwd")
    dq_f, dk_f, dv_f, dla_f = _gla_bwd(pg, la_f, do, sp_f, False, name="gla_bwd")
    dq_b, dk_b, dv_b, dla_b = _gla_bwd(pg, la_b, do, sp_b, True, name="gla_bwd_rev")

    def gates_bwd(z, dlf, dlb, w, bias):
        dz = jnp.zeros_like(z)
        dlogits, dbs = [], []
        for d, dl in ((0, dlf), (1, dlb)):
            logit = _dot(z, w[d]) + bias[d:d + 1]
            dlogit = dl * (1.0 / GLA_TAU) * jnp.exp(_log_sigmoid(-logit))
            dz = dz + _dot(dlogit, w[d], _NT)
            dlogits.append(dlogit)
            dbs.append(jnp.sum(dlogit, axis=0, keepdims=True))
        return dz, dlogits[0], dlogits[1], dbs[0], dbs[1]
    dz, dlg_f, dlg_b, dba_f, dba_b = _rowmap(
        gates_bwd, [(pg, CB_Z, 128), dla_f, dla_b], [wa, b_alpha], [(128, F32), (GLA_WIDTH, BF16), (GLA_WIDTH, BF16)],
        [(1, GLA_WIDTH), (1, GLA_WIDTH)], tl=512, name="gla_gates_bwd")
    dwa_f = _mm(dlg_f, (pg, CB_Z, 128), ta=True, name="gla_dwalpha")
    dwa_b = _mm(dlg_b, (pg, CB_Z, 128), ta=True, name="gla_dwalpha")
    grads = {'w_alpha': jnp.stack([dwa_f[:, 0:GLA_LOWRANK].T, dwa_b[:, GLA_LOWRANK:2 * GLA_LOWRANK].T]),
             'b_alpha': jnp.concatenate([dba_f, dba_b], axis=0),
             'norm': jnp.sum(dgn.reshape(GLA_HEADS, GLA_HEAD_DIM), axis=0)}
    return [dq_f, dq_b], [dk_f, dk_b], [dv_f, dv_b], dgate, dz, grads


def _rope_tables(length):
    half = ATTN_HEAD_DIM // 2
    inv_freq = ROPE_BASE ** (-jnp.arange(half // 2, dtype=F32) * 2.0 / half)
    t = jnp.arange(length, dtype=jnp.int32)
    def one(pos):
        ang = pos.astype(F32)[:, None] * inv_freq[None, :]
        c, s = jnp.cos(ang), jnp.sin(ang)
        return jnp.concatenate([c, c], axis=1), jnp.concatenate([-s, s], axis=1)
    c_r, s_r = one(t // GRID_W)
    c_c, s_c = one(t % GRID_W)
    return jnp.concatenate([c_r, c_c], axis=1), jnp.concatenate([s_r, s_c], axis=1)


def _rope_swap(y):
    w = y.shape[1]
    lane = lax.broadcasted_iota(jnp.int32, y.shape, 1)
    return jnp.where(lane % 32 < 16, pltpu.roll(y, w - 16, 1), pltpu.roll(y, 16, 1))


def _head_sums(x, ones):
    parts = [lax.dot_general(p, ones, _NN, preferred_element_type=F32) for p in _split3(x)]
    return parts[0] + parts[1] + parts[2]


def _head_ones(width):
    seg = np.arange(width) // ATTN_HEAD_DIM
    return jnp.asarray(seg[:, None] == seg[None, :], BF16)


def _qk_prep_fwd(pg, cb, width, gain, cos, sin, scale, *, name):
    heads = width // ATTN_HEAD_DIM
    def fn(x, c, s, g, ones):
        r = lax.rsqrt(_head_sums(x * x, ones) * (1.0 / ATTN_HEAD_DIM) + NORM_EPS)
        y = x * r * g
        return (y * c + _rope_swap(y) * s) * scale
    return _rowmap(fn, [(pg, cb, width), jnp.tile(cos, (1, heads)), jnp.tile(sin, (1, heads))],
                   [jnp.tile(gain, heads).reshape(1, -1), _head_ones(width)], [(width, BF16)], tl=512, name=name)[0]


def _qk_prep_bwd(pg, cb, width, gain, cos, sin, scale, dout, *, name):
    heads = width // ATTN_HEAD_DIM
    def fn(x, dov, c, s, g, ones):
        r = lax.rsqrt(_head_sums(x * x, ones) * (1.0 / ATTN_HEAD_DIM) + NORM_EPS)
        dos = dov * scale
        dy = dos * c + _rope_swap(dos * s)
        gd = dy * g
        dx = r * gd - x * (r * r * r) * (_head_sums(x * gd, ones) * (1.0 / ATTN_HEAD_DIM))
        return dx, jnp.sum(dy * x * r, axis=0, keepdims=True)
    dx, dg = _rowmap(fn, [(pg, cb, width), dout, jnp.tile(cos, (1, heads)), jnp.tile(sin, (1, heads))],
                     [jnp.tile(gain, heads).reshape(1, -1), _head_ones(width)], [(width, F32)], [(1, width)], tl=512,
                     name=name)
    return dx, jnp.sum(dg.reshape(heads, ATTN_HEAD_DIM), axis=0)


def _to_heads(x, heads):
    return jnp.transpose(x.reshape(x.shape[0], heads, ATTN_HEAD_DIM), (1, 0, 2))


def _from_heads(x):
    return jnp.transpose(x, (1, 0, 2)).reshape(x.shape[1], x.shape[0] * ATTN_HEAD_DIM)


ATTN_GROUP = ATTN_Q_HEADS // ATTN_KV_HEADS
ATTN_TQ = 256


def _attn_fwd(q, k, v, side=None):
    length = q.shape[1]
    tq = min(ATTN_TQ, length)
    grid = (ATTN_KV_HEADS, length // tq)

    def compute(refs):
        q_ref, k_ref, v_ref, o_ref = refs
        kk, vv = k_ref[0], v_ref[0]
        for g in range(ATTN_GROUP):
            s = _dot(q_ref[g], kk, _NT)
            p = jnp.exp(s - jnp.max(s, axis=-1, keepdims=True))
            o_ref[g] = _dot(p, vv) / jnp.sum(p, axis=-1, keepdims=True)

    def body(*refs):
        _carried(side, grid, refs, 3, 1, 0, compute)

    kv_spec = pl.BlockSpec((1, length, ATTN_HEAD_DIM), lambda h, i: (h, 0, 0))
    q_spec = pl.BlockSpec((ATTN_GROUP, tq, ATTN_HEAD_DIM), lambda h, i: (h, i, 0))
    (out,), gathered = _side_call(
        body, side, name="attn_fwd", grid=grid, in_specs=[q_spec, kv_spec, kv_spec], out_specs=[q_spec],
        out_shape=[jax.ShapeDtypeStruct(q.shape, F32)], scratch=[], args=[q, k, v], semantics=("parallel", "parallel"))
    return out, gathered


def _attn_bwd(q, k, v, o, do, side=None):
    length = q.shape[1]
    tq = min(ATTN_TQ, length)
    grid = (ATTN_KV_HEADS, length // tq)

    def body(*refs):
        _carried(side, grid, refs, 5, 3, 0, compute)

    def compute(refs):
        q_ref, k_ref, v_ref, o_ref, do_ref, dq_ref, dk_ref, dv_ref = refs

        @pl.when(pl.program_id(1) == 0)
        def _():
            dk_ref[...] = jnp.zeros_like(dk_ref)
            dv_ref[...] = jnp.zeros_like(dv_ref)

        kk, vv = k_ref[0], v_ref[0]
        for g in range(ATTN_GROUP):
            qg, dog = q_ref[g], do_ref[g]
            s = _dot(qg, kk, _NT)
            p = jnp.exp(s - jnp.max(s, axis=-1, keepdims=True))
            p = p / jnp.sum(p, axis=-1, keepdims=True)
            dp = _dot(dog, vv, _NT)
            ds = p * (dp - jnp.sum(dog * o_ref[g], axis=-1, keepdims=True))
            dq_ref[g] = _dot(ds, kk)
            dk_ref[0] += _dot(ds, qg, _TN)
            dv_ref[0] += _dot(p, dog, _TN)

    kv_spec = pl.BlockSpec((1, length, ATTN_HEAD_DIM), lambda h, i: (h, 0, 0))
    q_spec = pl.BlockSpec((ATTN_GROUP, tq, ATTN_HEAD_DIM), lambda h, i: (h, i, 0))
    return _side_call(
        body, side, name="attn_bwd", grid=grid, in_specs=[q_spec, kv_spec, kv_spec, q_spec, q_spec],
        out_specs=[q_spec, kv_spec, kv_spec],
        out_shape=[jax.ShapeDtypeStruct(q.shape, F32), jax.ShapeDtypeStruct(k.shape, F32), jax.ShapeDtypeStruct(k.shape, F32)],
        scratch=[], args=[q, k, v, o, do], semantics=("parallel", "arbitrary"))


def _attn_branch_fwd(pg, q_gain, k_gain, side=None):
    cos, sin = _rope_tables(pg.shape[0])
    qp = _qk_prep_fwd(pg, CB_AQ, ATTN_WIDTH, q_gain, cos, sin, ATTN_HEAD_DIM ** -0.5, name="attn_q_prep")
    kp = _qk_prep_fwd(pg, CB_AK, ATTN_KV_WIDTH, k_gain, cos, sin, 1.0, name="attn_k_prep")
    qh, kh = _to_heads(qp, ATTN_Q_HEADS), _to_heads(kp, ATTN_KV_HEADS)
    vh = _to_heads(pg[:, P_OFF + 3200:P_OFF + 3328].astype(BF16), ATTN_KV_HEADS)
    oh, gathered = _attn_fwd(qh, kh, vh, side)
    return _from_heads(oh).astype(BF16), (cos, sin, qh, kh, vh, oh), gathered


def _attn_branch_bwd(pg, q_gain, k_gain, saved, dy, side=None):
    cos, sin, qh, kh, vh, oh = saved
    (dqh, dkh, dvh), carried = _attn_bwd(qh, kh, vh, oh, _to_heads(dy, ATTN_Q_HEADS), side)
    dq, dqg = _qk_prep_bwd(pg, CB_AQ, ATTN_WIDTH, q_gain, cos, sin, ATTN_HEAD_DIM ** -0.5, _from_heads(dqh),
                           name="attn_q_prep_bwd")
    dk, dkg = _qk_prep_bwd(pg, CB_AK, ATTN_KV_WIDTH, k_gain, cos, sin, 1.0, _from_heads(dkh), name="attn_k_prep_bwd")
    return dq, dk, _from_heads(dvh), {'q_norm': dqg, 'k_norm': dkg}, carried


def _gate_cols():
    return [slice(i * D_MODEL, (i + 1) * D_MODEL) for i in range(3)]


def _mixer_fwd(x, lw, side_in=None, after_in=None, side_attn=None):
    h = _rmsnorm_fwd(x, lw['mix_norm'])
    if side_in is None:
        pg = _mm(h, lw['w_pg'], name="mix_in")
    else:
        pg, got_in = _mm(h, lw['w_pg'], side=side_in, name="mix_in")
        after_in(got_in)
    y_s5, s_s5 = _s5_fwd(pg, lw['s5'], lw['s5_w_glu'])
    y_gla, s_gla = _gla_branch_fwd(pg, lw['gla_w_alpha'], lw['gla_b_alpha'], lw['gla_norm'])
    y_att, s_att, got_attn = _attn_branch_fwd(pg, lw['attn_q_norm'], lw['attn_k_norm'], side_attn)
    ys = (y_s5, y_gla, y_att)
    br = [_mm(y, lw[n], name="mix_branch") for y, n in zip(ys, ('w_branch_s5', 'w_branch_gla', 'w_branch_attn'))]

    def merge(g0, g1, g2, b0, b1, b2, bias):
        acc = None
        for g, b, c in zip((g0, g1, g2), (b0, b1, b2), _gate_cols()):
            term = _sigmoid(g + bias[:, c]) * b
            acc = term if acc is None else acc + term
        return acc
    merged = _rowmap(merge, [(pg, 0, D_MODEL), (pg, 1, D_MODEL), (pg, 2, D_MODEL)] + br,
                     [lw['b_merge_gate'].reshape(1, -1)], [(D_MODEL, BF16)], tl=256, name="mix_merge")[0]
    x_out = _mm(merged, lw['w_out'], add=x, name="mix_out")
    return x_out, (x, h, pg, ys, (s_s5, s_gla, s_att), br, merged), got_attn


def _mixer_bwd(saved, lw, dx_out, side=None):
    x, h, pg, ys, (s_s5, s_gla, s_att), br, merged = saved
    grads = {'w_out': _mm(merged, dx_out, ta=True, out_dtype=BF16, name="mix_dwout")}
    dmerged = _mm(dx_out, lw['w_out'], tb=True, name="mix_dmerged")

    def merge_bwd(g0, g1, g2, b0, b1, b2, dm, bias):
        dbr, dgp = [], []
        for g, b, c in zip((g0, g1, g2), (b0, b1, b2), _gate_cols()):
            s = _sigmoid(g + bias[:, c])
            dbr.append(dm * s)
            dgp.append(dm * b * (s * (1.0 - s)))
        dgp = jnp.concatenate(dgp, axis=1)
        return dbr[0], dbr[1], dbr[2], dgp, jnp.sum(dgp, axis=0, keepdims=True)
    d0, d1, d2, dgpre, dbias = _rowmap(
        merge_bwd, [(pg, 0, D_MODEL), (pg, 1, D_MODEL), (pg, 2, D_MODEL)] + br + [dmerged],
        [lw['b_merge_gate'].reshape(1, -1)], [(D_MODEL, BF16)] * 3 + [(GATE_WIDTH, BF16)], [(1, GATE_WIDTH)], tl=256,
        name="mix_merge_bwd")
    grads['b_merge_gate'] = dbias[0]
    dys = []
    for y, dbr, n in zip(ys, (d0, d1, d2), ('w_branch_s5', 'w_branch_gla', 'w_branch_attn')):
        grads[n] = _mm(y, dbr, ta=True, out_dtype=BF16, name="mix_dwbranch")
        dys.append(_mm(dbr, lw[n], tb=True, name="mix_dy"))
    du, g_s5 = _s5_bwd(pg, lw['s5'], lw['s5_w_glu'], s_s5, dys[0])
    dgq, dgk, dgv, dgg, dz, g_gla = _gla_branch_bwd(pg, lw['gla_w_alpha'], lw['gla_b_alpha'], lw['gla_norm'], s_gla, dys[1])
    daq, dak, dav, g_att, carried = _attn_branch_bwd(pg, lw['attn_q_norm'], lw['attn_k_norm'], s_att, dys[2], side)

    def assemble(dgp, u0, u1, u2, q0, q1, k0, k1, v0, v1, gg, aq, ak, av, z):
        pad = jnp.zeros((dgp.shape[0], IN_PAD - 3456), F32)
        parts = [dgp.astype(F32), u0 + u1 + u2, q0 + q1, k0 + k1, v0 + v1, gg, aq, ak, av, z, pad]
        return jnp.concatenate(parts, axis=1)
    dpg = _rowmap(assemble, [dgpre] + du + dgq + dgk + dgv + [dgg, daq, dak, dav, dz], [], [(PG_WIDTH, BF16)], tl=256,
                  name="mix_dpg")[0]
    grads['w_pg'] = _mm(h, dpg, ta=True, out_dtype=BF16, name="mix_dwpg")
    dh = _mm(dpg, lw['w_pg'], tb=True, name="mix_dh")
    dx, grads['mix_norm'] = _rmsnorm_bwd(x, lw['mix_norm'], dh, dx_out)
    grads['s5'], grads['gla'], grads['attn'] = g_s5, g_gla, g_att
    return dx, grads, carried


def _loss_head(x, gain, target):
    width = x.shape[1]

    def fn(xv, tv, g):
        r = _rms(xv)
        err = xv * r * g - tv
        dy = err * (1.0 / width)
        gd = dy * g
        dx = r * gd - xv * (r * r * r) * jnp.mean(xv * gd, axis=-1, keepdims=True)
        loss = jnp.sum(0.5 * jnp.mean(err * err, axis=-1, keepdims=True), axis=0, keepdims=True)
        return dx, jnp.broadcast_to(loss, (1, 128)), jnp.sum(dy * xv * r, axis=0, keepdims=True)
    dx, loss, dgain = _rowmap(fn, [x, target], [gain.reshape(1, -1)], [(width, F32)], [(1, 128), (1, width)], tl=256,
                              name="loss_head")
    return loss[0, 0], dx, dgain[0]


def _row_tile(rows, cap=256):
    for t in range(cap - cap % 16, 0, -16):
        if rows % t == 0:
            return t
    return rows


def _reduce_adamw(parts, w, m, v, *, name):
    r, c = w.shape
    if len(parts) > 1 and parts[0].shape[1] % 8:
        parts = [jnp.concatenate(parts, axis=1)]
    nparts, rows = parts[0].shape[0], parts[0].shape[1]
    tr = _row_tile(rows)
    per = rows // tr

    def body(*refs):
        p_refs, (w_ref, m_ref, v_ref, g_ref, d_ref, m2_ref, v2_ref) = refs[:len(parts)], refs[len(parts):]
        g = None
        for k, p_ref in enumerate(p_refs):
            gk = p_ref[0].astype(F32)
            for j in range(1, nparts):
                gk = gk + p_ref[j].astype(F32)
            g = gk if g is None else jnp.where(pl.program_id(0) // per == k, gk, g)
        m2 = ADAM_B1 * m_ref[...] + (1.0 - ADAM_B1) * g
        v2 = ADAM_B2 * v_ref[...] + (1.0 - ADAM_B2) * (g * g)
        m_hat = m2 / (1.0 - ADAM_B1 ** ADAM_STEP)
        v_hat = v2 / (1.0 - ADAM_B2 ** ADAM_STEP)
        g_ref[...] = g
        d_ref[...] = -ADAM_LR * (m_hat / (jnp.sqrt(v_hat) + ADAM_EPS) + ADAM_WD * w_ref[...])
        m2_ref[...] = m2
        v2_ref[...] = v2

    flat = pl.BlockSpec((tr, c), lambda i: (i, 0))
    p_specs = [pl.BlockSpec((nparts, tr, c), lambda i, k=k: (0, jnp.clip(i - k * per, 0, per - 1), 0)) for k in range(len(parts))]
    return pl.pallas_call(
        body, name=name, grid=(r // tr,), in_specs=p_specs + [flat, flat, flat],
        out_specs=[flat] * 4, out_shape=[jax.ShapeDtypeStruct((r, c), F32)] * 4,
        compiler_params=pltpu.CompilerParams(dimension_semantics=("parallel",)),
    )(*parts, w, m, v)


def _all_gather(blocks, *, name):
    side = _SideGather(blocks)

    def body(*refs):
        start, finish = side.hooks(refs)
        start()
        finish()

    return pl.pallas_call(body, name=name, out_shape=side.out_shape, in_specs=side.in_specs, out_specs=side.out_specs,
                          scratch_shapes=side.scratch)(*blocks)


class _SideGather:
    def __init__(self, blocks):
        self.blocks = list(blocks)
        self.n = n = len(self.blocks)
        hbm = pl.BlockSpec(memory_space=pl.ANY)
        self.in_specs, self.out_specs = [hbm] * n, [hbm] * n
        self.out_shape = [jax.ShapeDtypeStruct((N_DEV,) + b.shape, b.dtype) for b in self.blocks]
        self.scratch = [pltpu.SemaphoreType.DMA((n, 7)), pltpu.SemaphoreType.DMA((n, 7)), pltpu.SemaphoreType.DMA((n,))]

    def hooks(self, refs):
        n = self.n
        x_refs, out_refs = refs[:n], refs[n:2 * n]
        send_sems, recv_sems, local_sems = refs[2 * n:]
        x, y, c = lax.axis_index("x"), lax.axis_index("y"), lax.axis_index("c")
        me, sibling = (x, y, c), (x, y, 1 - c)
        chips = [(1 - x, y), (x, 1 - y), (1 - x, 1 - y)]

        def slot(t, px, py, pc):
            return out_refs[t].at[4 * px + 2 * py + pc]

        def copy(t, k, blk, to, own=False):
            return pltpu.make_async_remote_copy(
                src_ref=x_refs[t] if own else slot(t, *blk), dst_ref=slot(t, *blk), send_sem=send_sems.at[t, k],
                recv_sem=recv_sems.at[t, k], device_id=to, device_id_type=pl.DeviceIdType.MESH)

        def mine(t):
            return pltpu.make_async_copy(x_refs[t], slot(t, *me), local_sems.at[t])

        def first(t):
            return [copy(t, 0, me, sibling, own=True)] + [copy(t, 1 + j, me, (*chip, c), own=True) for j, chip in enumerate(chips)]

        def start():
            for t in range(n):
                mine(t).start()
            for t in range(n):
                for cp in first(t):
                    cp.start()

        def finish():
            passed = []
            for j, chip in enumerate(chips):
                for t in range(n):
                    copy(t, 1 + j, (*chip, c), me).wait_recv()
                    passed.append(copy(t, 4 + j, (*chip, c), sibling))
                    passed[-1].start()
            for t in range(n):
                copy(t, 0, sibling, me).wait_recv()
            for j, chip in enumerate(chips):
                for t in range(n):
                    copy(t, 4 + j, (*chip, 1 - c), me).wait_recv()
            for t in range(n):
                for cp in first(t):
                    cp.wait_send()
            for cp in passed:
                cp.wait_send()
            for t in range(n):
                mine(t).wait()

        return start, finish


def _first_last_step(grid):
    ids = [pl.program_id(a) for a in range(len(grid))]
    first = functools.reduce(lambda p, q: p & q, [i == 0 for i in ids])
    last = functools.reduce(lambda p, q: p & q, [i == n - 1 for i, n in zip(ids, grid)])
    return first, last


def _carried(side, grid, refs, n_in, n_out, n_scratch, compute):
    if side is None:
        compute(refs)
        return
    n = side.n
    main = refs[:n_in] + refs[n_in + n:n_in + n + n_out] + refs[n_in + 2 * n + n_out:n_in + 2 * n + n_out + n_scratch]
    side_refs = refs[n_in:n_in + n] + refs[n_in + n + n_out:n_in + 2 * n + n_out] + refs[n_in + 2 * n + n_out + n_scratch:]
    start, finish = side.hooks(side_refs)
    first, last = _first_last_step(grid)
    pl.when(first)(start)
    compute(main)
    pl.when(last)(finish)


N_CHIP = N_DEV // 2


def _swap_with_sibling(arrays, *, name):
    n = len(arrays)

    def body(*refs):
        src_refs, out_refs = refs[:n], refs[n:2 * n]
        send_sems, recv_sems = refs[2 * n:]
        sibling = (lax.axis_index("x"), lax.axis_index("y"), 1 - lax.axis_index("c"))
        copies = [pltpu.make_async_remote_copy(
            src_ref=src_refs[t], dst_ref=out_refs[t], send_sem=send_sems.at[t], recv_sem=recv_sems.at[t],
            device_id=sibling, device_id_type=pl.DeviceIdType.MESH) for t in range(n)]
        for cp in copies:
            cp.start()
        for cp in copies:
            cp.wait()

    hbm = pl.BlockSpec(memory_space=pl.ANY)
    return pl.pallas_call(
        body, name=name, out_shape=[jax.ShapeDtypeStruct(a.shape, a.dtype) for a in arrays],
        in_specs=[hbm] * n, out_specs=[hbm] * n,
        scratch_shapes=[pltpu.SemaphoreType.DMA((n,)), pltpu.SemaphoreType.DMA((n,))],
    )(*arrays)


def _exchange_chips(stacks, *, name):
    side = _SideChipExchange(stacks)

    def body(*refs):
        start, finish = side.hooks(refs)
        start()
        finish()

    return pl.pallas_call(body, name=name, out_shape=side.out_shape, in_specs=side.in_specs, out_specs=side.out_specs,
                          scratch_shapes=side.scratch)(*stacks)


class _SideChipExchange:
    def __init__(self, stacks):
        self.blocks = list(stacks)
        self.n = n = len(self.blocks)
        hbm = pl.BlockSpec(memory_space=pl.ANY)
        self.in_specs, self.out_specs = [hbm] * n, [hbm] * n
        self.out_shape = [jax.ShapeDtypeStruct(s.shape, s.dtype) for s in self.blocks]
        self.scratch = [pltpu.SemaphoreType.DMA((n, N_CHIP - 1)), pltpu.SemaphoreType.DMA((n, N_CHIP - 1)),
                        pltpu.SemaphoreType.DMA((n,))]

    def hooks(self, refs):
        n = self.n
        g_refs, out_refs = refs[:n], refs[n:2 * n]
        send_sems, recv_sems, local_sems = refs[2 * n:]
        x, y, c = lax.axis_index("x"), lax.axis_index("y"), lax.axis_index("c")
        me = 2 * x + y

        def copies():
            mine = [pltpu.make_async_copy(g_refs[t].at[me], out_refs[t].at[me], local_sems.at[t]) for t in range(n)]
            remote = []
            for k in range(1, N_CHIP):
                px, py = x ^ (k >> 1 & 1), y ^ (k & 1)
                for t in range(n):
                    remote.append(pltpu.make_async_remote_copy(
                        src_ref=g_refs[t].at[2 * px + py], dst_ref=out_refs[t].at[me], send_sem=send_sems.at[t, k - 1],
                        recv_sem=recv_sems.at[t, k - 1], device_id=(px, py, c), device_id_type=pl.DeviceIdType.MESH))
            return mine, remote

        def start():
            mine, remote = copies()
            for cp in mine + remote:
                cp.start()

        def finish():
            mine, remote = copies()
            for cp in remote:
                cp.wait_recv()
            for cp in remote:
                cp.wait_send()
            for cp in mine:
                cp.wait()

        return start, finish


def _pair_sum(a, b):
    return _rowmap(lambda u, v: u.astype(F32) + v.astype(F32), [a, b], [], [(a.shape[1], BF16)], tl=_row_tile(a.shape[0], 512),
                   name="pair_sum")[0]


SMALL_COLS = 128


def _pack_small(arrays):
    flat = jnp.concatenate([a.astype(F32).reshape(-1, SMALL_COLS) for a in arrays], axis=0)
    return jnp.pad(flat, ((0, -flat.shape[0] % 256), (0, 0)))


def _unpack_small(packed, shapes):
    out, off = [], 0
    for s in shapes:
        r = math.prod(s) // SMALL_COLS
        out.append(packed[off:off + r].reshape(s))
        off += r
    return out


def _split_shards(full, axis):
    shape = full.shape
    split = full.reshape(shape[:axis] + (N_DEV, shape[axis] // N_DEV) + shape[axis + 1:])
    return jnp.moveaxis(split, axis, 0)


def _join_shards(stack, axis):
    moved = jnp.moveaxis(stack, 0, axis)
    shape = moved.shape
    return moved.reshape(shape[:axis] + (shape[axis] * shape[axis + 1],) + shape[axis + 2:])


def _w_in_padded(w_in):
    pad = jnp.zeros(w_in.shape[:-1] + (IN_PAD - IN_WIDTH,), w_in.dtype)
    return jnp.concatenate([w_in[..., :2560], w_in[..., 2592:], w_in[..., 2560:2592], pad], axis=-1)


def _w_in_unpadded(w):
    return jnp.concatenate([w[..., :2560], w[..., 3328:3360], w[..., 2560:3328]], axis=-1)


FFN1_W = ('ffn1_w_gate', 'ffn1_w_up', 'ffn1_w_down')
FFN2_W = ('ffn2_w_gate', 'ffn2_w_up', 'ffn2_w_down')
MIX_IN_W = ('w_in', 'w_merge_gate')
MIX_REST_W = ('s5_w_glu', 'gla_w_alpha', 'gla_b_alpha', 'w_branch_s5', 'w_branch_gla', 'w_branch_attn', 'w_out')


def _mixer_weights(full, w, s5, i):
    lw = {n: w[n][i] for n in ('mix_norm', 'gla_norm', 'attn_q_norm', 'attn_k_norm', 'b_merge_gate')}
    lw['s5'] = {'b_mat': s5['b_mat'][i], 'c_mat': s5['c_mat'][i], 'tabs': [t[i] for t in s5['tabs']],
                'tabs_adj': [t[i] for t in s5['tabs_adj']], 'd': w['s5_d'][i]}
    w_in = full['w_in']
    pad = jnp.zeros((D_MODEL, IN_PAD - IN_WIDTH), w_in.dtype)
    lw['w_pg'] = jnp.concatenate([full['w_merge_gate'], w_in[:, :2560], w_in[:, 2592:], w_in[:, 2560:2592], pad], axis=1)
    return lw


def _mixer_weights_rest(full):
    lw = {n: full[n] for n in MIX_REST_W if n != 'gla_b_alpha'}
    lw['gla_b_alpha'] = full['gla_b_alpha'].astype(F32)
    return lw


def _chip_sums(grads):
    core = lax.axis_index("c")
    own, for_sibling = [], []
    for n in SHARDED:
        by_owner = _split_shards(grads[n], SHARD_AXIS[n] - 1).astype(BF16)
        by_owner = by_owner.reshape((N_CHIP, 2) + by_owner.shape[1:])
        own.append(lax.dynamic_index_in_dim(by_owner, core, axis=1, keepdims=False))
        for_sibling.append(lax.dynamic_index_in_dim(by_owner, 1 - core, axis=1, keepdims=False))
    from_sibling = _swap_with_sibling(for_sibling, name="exchange_grads_sibling")
    return [_pair_sum(a.reshape(-1, a.shape[-1]), b.reshape(-1, b.shape[-1])).reshape(a.shape)
            for a, b in zip(own, from_sibling)]


def _step_local(x, target, w, shards):
    s5 = _s5_setup(w)
    full = [{} for _ in range(DEPTH)]

    def wanted(i, *groups):
        return _SideGather([shards[n][i] for names in groups for n in names])

    def arrived(i, stacks, *groups):
        names = [n for group in groups for n in group]
        for n, st in zip(names, stacks):
            full[i][n] = _join_shards(st, SHARD_AXIS[n] - 1)

    arrived(0, _all_gather([shards[n][0] for n in FFN1_W], name="gather_first"), FFN1_W)
    saved, lws = [], []
    for i in range(DEPTH):
        f, first = full[i], i == 0
        x, s1, got = _ffn_fwd(x, w['ffn1_norm'][i], f['ffn1_w_gate'], f['ffn1_w_up'], f['ffn1_w_down'],
                              wanted(i, MIX_IN_W) if first else None)
        if first:
            arrived(i, got, MIX_IN_W)
        lw = _mixer_weights(f, w, s5, i)
        lws.append(lw)

        def after_in(got_in, i=i, lw=lw):
            arrived(i, got_in, MIX_REST_W, FFN2_W)
            lw.update(_mixer_weights_rest(full[i]))
        if not first:
            lw.update(_mixer_weights_rest(f))
        x, s2, got = _mixer_fwd(x, lw, wanted(i, MIX_REST_W, FFN2_W) if first else None, after_in,
                                wanted(i + 1, FFN1_W, MIX_IN_W, MIX_REST_W) if first else wanted(i, FFN2_W))
        if first:
            arrived(i + 1, got, FFN1_W, MIX_IN_W, MIX_REST_W)
        else:
            arrived(i, got, FFN2_W)
        x, s3, _ = _ffn_fwd(x, w['ffn2_norm'][i], f['ffn2_w_gate'], f['ffn2_w_up'], f['ffn2_w_down'])
        saved.append((s1, s2, s3))
    loss, dx, d_final = _loss_head(x, w['final_norm'], target)
    per_layer, incoming, pending = [None] * DEPTH, [None] * DEPTH, None
    for i in reversed(range(DEPTH)):
        f, lw, (s1, s2, s3), g = full[i], lws[i], saved[i], {}
        dx, g['ffn2_norm'], g['ffn2_w_gate'], g['ffn2_w_up'], g['ffn2_w_down'] = _ffn_bwd(
            s3, w['ffn2_norm'][i], f['ffn2_w_gate'], f['ffn2_w_up'], f['ffn2_w_down'], dx)
        dx, gm, carried = _mixer_bwd(s2, lw, dx, _SideChipExchange(pending[1]) if pending else None)
        if pending:
            incoming[pending[0]] = carried
        dx, g['ffn1_norm'], g['ffn1_w_gate'], g['ffn1_w_up'], g['ffn1_w_down'] = _ffn_bwd(
            s1, w['ffn1_norm'][i], f['ffn1_w_gate'], f['ffn1_w_up'], f['ffn1_w_down'], dx)
        g['w_merge_gate'] = gm['w_pg'][:, :GATE_WIDTH]
        g['w_in'] = _w_in_unpadded(gm['w_pg'][:, GATE_WIDTH:])
        for n in ('w_out', 'b_merge_gate', 'w_branch_s5', 'w_branch_gla', 'w_branch_attn', 'mix_norm'):
            g[n] = gm[n]
        g['s5_d'], g['s5_w_glu'], g['s5_raw'] = gm['s5']['d'], gm['s5']['w_glu'], gm['s5']['raw']
        g['gla_w_alpha'], g['gla_b_alpha'], g['gla_norm'] = gm['gla']['w_alpha'], gm['gla']['b_alpha'], gm['gla']['norm']
        g['attn_q_norm'], g['attn_k_norm'] = gm['attn']['q_norm'], gm['attn']['k_norm']
        per_layer[i] = g
        sums = _chip_sums(g)
        if i > 0:
            pending = (i, sums)
        else:
            incoming[i] = _exchange_chips(sums, name="exchange_grads_chips")
    stacked = _s5_param_grads(w, [g['s5_raw'] for g in per_layer])
    stacked['final_norm'] = d_final
    return loss, dx, per_layer, stacked, incoming


def kernel(x, ffn1_norm, ffn1_w_gate, ffn1_w_up, ffn1_w_down, mix_norm, w_in, s5_lambda_re, s5_lambda_im, s5_log_dt, s5_b_re, s5_b_im, s5_c_re, s5_c_im, s5_d, s5_w_glu, gla_w_alpha, gla_b_alpha, gla_norm, attn_q_norm, attn_k_norm, w_branch_s5, w_branch_gla, w_branch_attn, w_merge_gate, b_merge_gate, w_out, ffn2_norm, ffn2_w_gate, ffn2_w_up, ffn2_w_down, final_norm, loss_target, m_ffn1_norm, m_ffn1_w_gate, m_ffn1_w_up, m_ffn1_w_down, m_mix_norm, m_w_in, m_s5_lambda_re, m_s5_lambda_im, m_s5_log_dt, m_s5_b_re, m_s5_b_im, m_s5_c_re, m_s5_c_im, m_s5_d, m_s5_w_glu, m_gla_w_alpha, m_gla_b_alpha, m_gla_norm, m_attn_q_norm, m_attn_k_norm, m_w_branch_s5, m_w_branch_gla, m_w_branch_attn, m_w_merge_gate, m_b_merge_gate, m_w_out, m_ffn2_norm, m_ffn2_w_gate, m_ffn2_w_up, m_ffn2_w_down, m_final_norm, v_ffn1_norm, v_ffn1_w_gate, v_ffn1_w_up, v_ffn1_w_down, v_mix_norm, v_w_in, v_s5_lambda_re, v_s5_lambda_im, v_s5_log_dt, v_s5_b_re, v_s5_b_im, v_s5_c_re, v_s5_c_im, v_s5_d, v_s5_w_glu, v_gla_w_alpha, v_gla_b_alpha, v_gla_norm, v_attn_q_norm, v_attn_k_norm, v_w_branch_s5, v_w_branch_gla, v_w_branch_attn, v_w_merge_gate, v_b_merge_gate, v_w_out, v_ffn2_norm, v_ffn2_w_gate, v_ffn2_w_up, v_ffn2_w_down, v_final_norm):
    return _train_step(x, ffn1_norm, ffn1_w_gate, ffn1_w_up, ffn1_w_down, mix_norm, w_in, s5_lambda_re, s5_lambda_im, s5_log_dt, s5_b_re, s5_b_im, s5_c_re, s5_c_im, s5_d, s5_w_glu, gla_w_alpha, gla_b_alpha, gla_norm, attn_q_norm, attn_k_norm, w_branch_s5, w_branch_gla, w_branch_attn, w_merge_gate, b_merge_gate, w_out, ffn2_norm, ffn2_w_gate, ffn2_w_up, ffn2_w_down, final_norm, loss_target, m_ffn1_norm, m_ffn1_w_gate, m_ffn1_w_up, m_ffn1_w_down, m_mix_norm, m_w_in, m_s5_lambda_re, m_s5_lambda_im, m_s5_log_dt, m_s5_b_re, m_s5_b_im, m_s5_c_re, m_s5_c_im, m_s5_d, m_s5_w_glu, m_gla_w_alpha, m_gla_b_alpha, m_gla_norm, m_attn_q_norm, m_attn_k_norm, m_w_branch_s5, m_w_branch_gla, m_w_branch_attn, m_w_merge_gate, m_b_merge_gate, m_w_out, m_ffn2_norm, m_ffn2_w_gate, m_ffn2_w_up, m_ffn2_w_down, m_final_norm, v_ffn1_norm, v_ffn1_w_gate, v_ffn1_w_up, v_ffn1_w_down, v_mix_norm, v_w_in, v_s5_lambda_re, v_s5_lambda_im, v_s5_log_dt, v_s5_b_re, v_s5_b_im, v_s5_c_re, v_s5_c_im, v_s5_d, v_s5_w_glu, v_gla_w_alpha, v_gla_b_alpha, v_gla_norm, v_attn_q_norm, v_attn_k_norm, v_w_branch_s5, v_w_branch_gla, v_w_branch_attn, v_w_merge_gate, v_b_merge_gate, v_w_out, v_ffn2_norm, v_ffn2_w_gate, v_ffn2_w_up, v_ffn2_w_down, v_final_norm)


def _train_step(*args):
    nw = len(W_NAMES)
    x, target = args[0][0], args[1 + nw][0]
    w = dict(zip(W_NAMES, args[1:1 + nw]))
    m = dict(zip(W_NAMES, args[2 + nw:2 + 2 * nw]))
    v = dict(zip(W_NAMES, args[2 + 2 * nw:2 + 3 * nw]))

    loss, dx, per_layer, stacked, incoming = _step_local(x, target, w, {n: w[n].astype(BF16) for n in SHARDED})
    loss = lax.psum(loss, ("x", "y", "c"))

    out = {}
    kinds = ('grad', 'delta', 'new_m', 'new_v')
    for t, n in enumerate(SHARDED):
        shape = w[n].shape
        flat = lambda a: a.reshape(-1, shape[-1])
        parts = [incoming[i][t].reshape(N_CHIP, -1, shape[-1]) for i in range(DEPTH)]
        res = _reduce_adamw(parts, flat(w[n]), flat(m[n]), flat(v[n]), name="adamw_sharded")
        for kind, a in zip(kinds, res):
            out[kind + '_' + n] = a.reshape(shape)
    small = [stacked[n] if n in stacked else jnp.stack([g[n] for g in per_layer]) for n in REPLICATED]
    parts = _all_gather([_pack_small(small)], name="gather_small_grads")[0]
    res = _reduce_adamw([parts], *[_pack_small([d[n] for n in REPLICATED]) for d in (w, m, v)], name="adamw_replicated")
    for kind, packed in zip(kinds, res):
        for n, a in zip(REPLICATED, _unpack_small(packed, [w[n].shape for n in REPLICATED])):
            out[kind + '_' + n] = a
    return (loss, dx[None]) + tuple(out[kind + '_' + n] for kind in kinds for n in W_NAMES)
```

```python
import functools
import math

import jax
import jax.numpy as jnp
import numpy as np
from jax import lax
from jax.experimental import pallas as pl
from jax.experimental.pallas import tpu as pltpu

F32 = jnp.float32
BF16 = jnp.bfloat16

N_DEV = 8
D_MODEL = 1024
DEPTH = 2
GRID_W = 64
D_FF = 2816
NORM_EPS = 1e-6
S5_GROUPS = 32
S5_GROUP_CH = 16
S5_STATE = 64
S5_WIDTH = 512
S5_NSTATE = S5_GROUPS * S5_STATE
S5_LANE_BLOCK = 512
GLA_HEADS = 4
GLA_HEAD_DIM = 128
GLA_WIDTH = 512
GLA_LOWRANK = 16
GLA_TAU = 16.0
GLA_CHUNK = 64
ATTN_Q_HEADS = 8
ATTN_KV_HEADS = 2
ATTN_HEAD_DIM = 64
ATTN_WIDTH = 512
ATTN_KV_WIDTH = 128
ROPE_BASE = 10000.0
IN_SPLITS = (512, 512, 512, 512, 512, 16, 16, 512, 128, 128)
IN_WIDTH = sum(IN_SPLITS)
IN_PAD = 3584
GATE_WIDTH = 3 * D_MODEL
PG_WIDTH = GATE_WIDTH + IN_PAD
P_OFF = GATE_WIDTH
CB_U, CB_GQ, CB_GK, CB_GV, CB_GG, CB_AQ = (P_OFF // 512 + i for i in range(6))
CB_AK, CB_AV, CB_Z = (P_OFF + 3072) // 128, (P_OFF + 3200) // 128, (P_OFF + 3328) // 128
ADAM_LR = 0.001
ADAM_B1 = 0.9
ADAM_B2 = 0.999
ADAM_EPS = 1e-08
ADAM_WD = 0.01
ADAM_STEP = 10
PACK_COLS = 1024

W_NAMES = ['ffn1_norm', 'ffn1_w_gate', 'ffn1_w_up', 'ffn1_w_down', 'mix_norm', 'w_in', 's5_lambda_re', 's5_lambda_im',
           's5_log_dt', 's5_b_re', 's5_b_im', 's5_c_re', 's5_c_im', 's5_d', 's5_w_glu', 'gla_w_alpha', 'gla_b_alpha',
           'gla_norm', 'attn_q_norm', 'attn_k_norm', 'w_branch_s5', 'w_branch_gla', 'w_branch_attn', 'w_merge_gate',
           'b_merge_gate', 'w_out', 'ffn2_norm', 'ffn2_w_gate', 'ffn2_w_up', 'ffn2_w_down', 'final_norm']
SHARD_AXIS = {'ffn1_w_gate': 2, 'ffn1_w_up': 2, 'ffn1_w_down': 1, 'w_in': 2, 's5_w_glu': 1, 'gla_w_alpha': 3,
              'gla_b_alpha': 2, 'w_branch_s5': 2, 'w_branch_gla': 2, 'w_branch_attn': 2, 'w_merge_gate': 2,
              'w_out': 1, 'ffn2_w_gate': 2, 'ffn2_w_up': 2, 'ffn2_w_down': 1}
SHARDED = [n for n in W_NAMES if n in SHARD_AXIS]
REPLICATED = [n for n in W_NAMES if n not in SHARD_AXIS]


def _pick(dim, prefs):
    for p in prefs:
        if dim % p == 0:
            return p
    return dim


def _sigmoid(x):
    return 0.5 * jnp.tanh(0.5 * x) + 0.5


def _mm(a, b, *, ta=False, tb=False, out_dtype=F32, scale=None, add=None, side=None, name):
    a, a_cb, a_w = a if isinstance(a, tuple) else (a, 0, a.shape[1])
    b, b_cb, b_w = b if isinstance(b, tuple) else (b, 0, b.shape[1])
    m, k = (a_w, a.shape[0]) if ta else (a.shape[0], a_w)
    n = b.shape[0] if tb else b_w
    assert (b_w if tb else b.shape[0]) == k, (a.shape, b.shape, ta, tb)
    tm, tn, tk = _mm_tiles(m, n, k, a.dtype.itemsize, b.dtype.itemsize, jnp.dtype(out_dtype).itemsize)
    nk = k // tk
    dims = (((0 if ta else 1,), (1 if tb else 0,)), ((), ()))
    a_off = a_cb * (a_w // (tm if ta else tk))
    b_off = b_cb * (b_w // (tk if tb else tn))

    grid = (m // tm, n // tn, nk)
    n_in = 2 if add is None else 3

    def body(*refs):
        _carried(side, grid, refs, n_in, 1, int(nk > 1), compute)

    def compute(refs):
        a_ref, b_ref, *rest = refs
        add_ref = rest[0] if add is not None else None
        o_ref, *acc = rest[1:] if add is not None else rest

        def finish(res):
            res = res if scale is None else res * scale
            return (res if add_ref is None else res + add_ref[...]).astype(out_dtype)

        part = lax.dot_general(a_ref[...].astype(BF16), b_ref[...].astype(BF16), dims, preferred_element_type=F32)
        if nk == 1:
            o_ref[...] = finish(part)
            return
        acc_ref, = acc
        kk = pl.program_id(2)

        @pl.when(kk == 0)
        def _():
            acc_ref[...] = part

        @pl.when(kk > 0)
        def _():
            acc_ref[...] += part

        @pl.when(kk == nk - 1)
        def _():
            o_ref[...] = finish(acc_ref[...])

    a_spec = (pl.BlockSpec((tk, tm), lambda i, j, kk: (kk, i + a_off)) if ta
              else pl.BlockSpec((tm, tk), lambda i, j, kk: (i, kk + a_off)))
    b_spec = (pl.BlockSpec((tn, tk), lambda i, j, kk: (j, kk + b_off)) if tb
              else pl.BlockSpec((tk, tn), lambda i, j, kk: (kk, j + b_off)))
    o_spec = pl.BlockSpec((tm, tn), lambda i, j, kk: (i, j))
    (out,), gathered = _side_call(
        body, side, name=name, grid=grid, in_specs=[a_spec, b_spec] + ([o_spec] if add is not None else []),
        out_specs=[o_spec], out_shape=[jax.ShapeDtypeStruct((m, n), out_dtype)],
        scratch=[pltpu.VMEM((tm, tn), F32)] if nk > 1 else [], args=[a, b] + ([add] if add is not None else []),
        semantics=("parallel", "parallel", "arbitrary"))
    return out if side is None else (out, gathered)


MM_VMEM_BUDGET = 40 * 1024 * 1024


def _mm_tiles(m, n, k, a_bytes, b_bytes, out_bytes):
    tms = [t for t in (1024, 1408, 512, 256, 128) if m % t == 0] or [m]
    tns = [t for t in (512, 1408, 256, 128) if n % t == 0] or [n]
    tks = [k] + [t for t in (2048, 1024, 512, 256, 128) if k % t == 0 and t < k]
    for tk in tks:
        for tm in tms:
            for tn in tns:
                use = 2 * (tm * tk * a_bytes + tk * tn * b_bytes + tm * tn * out_bytes) + 2 * tm * tn * 4
                if use <= MM_VMEM_BUDGET:
                    return tm, tn, tk
    return tms[-1], tns[-1], tks[-1]


def _rowmap(fn, rows, consts, outs, reds=(), *, tl, name):
    rows = [r if isinstance(r, tuple) else (r, 0, r.shape[1]) for r in rows]
    length = rows[0][0].shape[0]
    tl = min(tl, length)
    nr, nc, no = len(rows), len(consts), len(outs)

    def body(*refs):
        res = fn(*[r[...] for r in refs[:nr + nc]])
        res = res if isinstance(res, tuple) else (res,)
        for o_ref, val in zip(refs[nr + nc:nr + nc + no], res[:no]):
            o_ref[...] = val.astype(o_ref.dtype)
        if reds:
            step = pl.program_id(0)
            red_refs = refs[nr + nc + no:]

            @pl.when(step == 0)
            def _():
                for d_ref, val in zip(red_refs, res[no:]):
                    d_ref[...] = val.astype(F32)

            @pl.when(step > 0)
            def _():
                for d_ref, val in zip(red_refs, res[no:]):
                    d_ref[...] += val.astype(F32)

    in_specs = [pl.BlockSpec((tl, w), lambda i, cb=cb: (i, cb)) for (_, cb, w) in rows]
    in_specs += [pl.BlockSpec(c.shape, lambda i, nd=c.ndim: (0,) * nd) for c in consts]
    out_specs = [pl.BlockSpec((tl, w), lambda i: (i, 0)) for (w, _) in outs]
    out_specs += [pl.BlockSpec(s, lambda i, nd=len(s): (0,) * nd) for s in reds]
    out_shape = [jax.ShapeDtypeStruct((length, w), dt) for (w, dt) in outs]
    out_shape += [jax.ShapeDtypeStruct(s, F32) for s in reds]
    res = pl.pallas_call(
        body, name=name, grid=(length // tl,), in_specs=in_specs, out_specs=out_specs, out_shape=out_shape,
        compiler_params=pltpu.CompilerParams(dimension_semantics=("arbitrary" if reds else "parallel",)),
    )(*[r[0] for r in rows], *consts)
    return res


def _rms(x):
    return lax.rsqrt(jnp.mean(x * x, axis=-1, keepdims=True) + NORM_EPS)


def _rmsnorm_fwd(x, gain):
    def fn(xv, g):
        return xv * _rms(xv) * g
    return _rowmap(fn, [x], [gain.reshape(1, -1)], [(x.shape[1], BF16)], tl=256, name="rmsnorm_fwd")[0]


def _rmsnorm_bwd(x, gain, dh, dres):
    def fn(xv, dhv, drv, g):
        r = _rms(xv)
        gd = dhv * g
        dx = r * gd - xv * (r * r * r) * jnp.mean(xv * gd, axis=-1, keepdims=True)
        return drv + dx, jnp.sum(dhv * xv * r, axis=0, keepdims=True)
    dx, dg = _rowmap(fn, [x, dh, dres], [gain.reshape(1, -1)], [(x.shape[1], F32)], [(1, x.shape[1])], tl=256,
                     name="rmsnorm_bwd")
    return dx, dg[0]


FFN_UNIT = D_FF // 2


def _row_halves(rows):
    return (slice(0, rows // 2), slice(rows // 2, rows))


def _side_call(body, side, *, name, grid, in_specs, out_specs, out_shape, scratch, args, semantics):
    if side is not None:
        in_specs, out_specs = in_specs + side.in_specs, out_specs + side.out_specs
        out_shape, scratch, args = out_shape + side.out_shape, scratch + side.scratch, list(args) + side.blocks
        semantics = ("arbitrary",) * len(grid)
    res = pl.pallas_call(body, name=name, grid=grid, in_specs=in_specs, out_specs=out_specs, out_shape=out_shape,
                         scratch_shapes=scratch, compiler_params=pltpu.CompilerParams(dimension_semantics=semantics))(*args)
    n_own = len(res) - (side.n if side is not None else 0)
    return res[:n_own], res[n_own:]


def _ffn_up(h, w_gate, w_up, side=None):
    length, k = h.shape
    tm = _pick(length, (512, 256, 128))
    grid = (D_FF // FFN_UNIT, length // tm)

    def compute(refs):
        h_ref, wg_ref, wu_ref, a_ref, g_ref, u_ref = refs
        for rows in _row_halves(tm):
            hv = h_ref[rows, :]
            g = jnp.dot(hv, wg_ref[...], preferred_element_type=F32)
            u = jnp.dot(hv, wu_ref[...], preferred_element_type=F32)
            a_ref[rows, :] = (g * _sigmoid(g) * u).astype(BF16)
            g_ref[rows, :] = g.astype(BF16)
            u_ref[rows, :] = u.astype(BF16)

    def body(*refs):
        _carried(side, grid, refs, 3, 3, 0, compute)

    w_spec = pl.BlockSpec((k, FFN_UNIT), lambda j, i: (0, j))
    o_spec = pl.BlockSpec((tm, FFN_UNIT), lambda j, i: (i, j))
    return _side_call(
        body, side, name="ffn_up", grid=grid, in_specs=[pl.BlockSpec((tm, k), lambda j, i: (i, 0)), w_spec, w_spec],
        out_specs=[o_spec] * 3, out_shape=[jax.ShapeDtypeStruct((length, D_FF), BF16)] * 3, scratch=[],
        args=[h, w_gate, w_up], semantics=("parallel", "parallel"))


def _ffn_dgu(dxo, w_down, g, u):
    length, k = dxo.shape
    tm = _pick(length, (512, 256, 128))

    def body(d_ref, w_ref, g_ref, u_ref, dg_ref, du_ref):
        for rows in _row_halves(tm):
            da = 0.5 * lax.dot_general(d_ref[rows, :], w_ref[...], _NT, preferred_element_type=F32)
            gv = g_ref[rows, :].astype(F32)
            s = _sigmoid(gv)
            dg_ref[rows, :] = (da * u_ref[rows, :].astype(F32) * (s * (1.0 + gv * (1.0 - s)))).astype(BF16)
            du_ref[rows, :] = (da * (gv * s)).astype(BF16)

    o_spec = pl.BlockSpec((tm, FFN_UNIT), lambda j, i: (i, j))
    return pl.pallas_call(
        body, name="ffn_dgu", grid=(D_FF // FFN_UNIT, length // tm),
        in_specs=[pl.BlockSpec((tm, k), lambda j, i: (i, 0)), pl.BlockSpec((FFN_UNIT, k), lambda j, i: (j, 0)), o_spec, o_spec],
        out_specs=[o_spec] * 2, out_shape=[jax.ShapeDtypeStruct((length, D_FF), BF16)] * 2,
        compiler_params=pltpu.CompilerParams(dimension_semantics=("parallel", "parallel")),
    )(dxo, w_down, g, u)


def _ffn_fwd(x, gain, w_gate, w_up, w_down, side=None):
    h = _rmsnorm_fwd(x, gain)
    (a, g, u), gathered = _ffn_up(h, w_gate, w_up, side)
    x_out = _mm(a, w_down, scale=0.5, add=x, name="ffn_down")
    return x_out, (x, h, g, u, a), gathered


def _ffn_bwd(saved, gain, w_gate, w_up, w_down, dx_out):
    x, h, g, u, a = saved
    dxo = dx_out.astype(BF16)
    d_wdown = _mm(a, dxo, ta=True, scale=0.5, out_dtype=BF16, name="ffn_dwdown")
    dg, du = _ffn_dgu(dxo, w_down, g, u)
    d_wgate = _mm(h, dg, ta=True, out_dtype=BF16, name="ffn_dwgu")
    d_wup = _mm(h, du, ta=True, out_dtype=BF16, name="ffn_dwgu")
    dh = _mm(du, w_up, tb=True, add=_mm(dg, w_gate, tb=True, name="ffn_dh"), name="ffn_dh_add")
    dx, dgain = _rmsnorm_bwd(x, gain, dh, dx_out)
    return dx, dgain, d_wgate, d_wup, d_wdown


def _s5_col(n):
    return (n // S5_LANE_BLOCK) * 2 * S5_LANE_BLOCK + n % S5_LANE_BLOCK


def _s5_blocked(re, im):
    lead = re.shape[:-1]
    nb = S5_NSTATE // S5_LANE_BLOCK
    both = jnp.stack([re.reshape(*lead, nb, S5_LANE_BLOCK), im.reshape(*lead, nb, S5_LANE_BLOCK)], axis=-2)
    return both.reshape(*lead, 2 * S5_NSTATE)


def _s5_unblocked(z):
    lead = z.shape[:-1]
    nb = S5_NSTATE // S5_LANE_BLOCK
    both = z.reshape(*lead, nb, 2, S5_LANE_BLOCK)
    return both[..., 0, :].reshape(*lead, S5_NSTATE), both[..., 1, :].reshape(*lead, S5_NSTATE)


def _s5_tables(a_re, a_im, reverse):
    a = lax.complex(a_re, a_im)
    a2 = a * a
    a4 = a2 * a2
    rows = jnp.arange(8)
    pw = [a]
    for _ in range(7):
        pw.append(pw[-1] * a)
    pw = jnp.stack(pw)
    if reverse:
        pw = pw[::-1]
    tabs = []
    for coef, s in ((a, 1), (a2, 2), (a4, 4)):
        live = (rows <= 7 - s) if reverse else (rows >= s)
        tabs.append(jnp.where(live[:, None], coef[None, :], 0.0))
    tabs.append(pw)
    tabs = jnp.stack(tabs)
    return _s5_blocked(jnp.real(tabs), jnp.imag(tabs))


def _s5_scan_tile(v, tab_ref, prev, reverse):
    lb = S5_LANE_BLOCK
    vr, vi = v[:, :lb], v[:, lb:]
    for idx, s in enumerate((1, 2, 4)):
        cr, ci = tab_ref[idx, :, :lb], tab_ref[idx, :, lb:]
        sh = 8 - s if reverse else s
        sr, si = pltpu.roll(vr, sh, 0), pltpu.roll(vi, sh, 0)
        vr, vi = vr + cr * sr - ci * si, vi + cr * si + ci * sr
    row = 0 if reverse else 7
    pr = jnp.broadcast_to(prev[row:row + 1, :lb], (8, lb))
    pi = jnp.broadcast_to(prev[row:row + 1, lb:], (8, lb))
    cr, ci = tab_ref[3, :, :lb], tab_ref[3, :, lb:]
    return jnp.concatenate([vr + cr * pr - ci * pi, vi + cr * pi + ci * pr], axis=1)


def _s5_scan(v, tabs, reverse, *, name):
    length = v.shape[0]
    tb = min(512, length)
    ntb = length // tb
    nlb = S5_NSTATE // S5_LANE_BLOCK
    wb = 2 * S5_LANE_BLOCK
    ntile = tb // 8

    def body(tab_ref, v_ref, x_ref, carry_ref):
        @pl.when(pl.program_id(1) == 0)
        def _():
            carry_ref[...] = jnp.zeros_like(carry_ref)

        def step(i, prev):
            r0 = pl.multiple_of((ntile - 1 - i if reverse else i) * 8, 8)
            x = _s5_scan_tile(v_ref[pl.ds(r0, 8), :], tab_ref, prev, reverse)
            x_ref[pl.ds(r0, 8), :] = x
            return x

        carry_ref[...] = lax.fori_loop(0, ntile, step, carry_ref[...])

    tmap = (lambda c, t: (ntb - 1 - t, c)) if reverse else (lambda c, t: (t, c))
    return pl.pallas_call(
        body, name=name, grid=(nlb, ntb),
        in_specs=[pl.BlockSpec((4, 8, wb), lambda c, t: (0, 0, c)), pl.BlockSpec((tb, wb), tmap)],
        out_specs=pl.BlockSpec((tb, wb), tmap), out_shape=jax.ShapeDtypeStruct(v.shape, F32),
        scratch_shapes=[pltpu.VMEM((8, wb), F32)],
        compiler_params=pltpu.CompilerParams(dimension_semantics=("parallel", "arbitrary")),
    )(tabs, v)


def _s5_scan_adjoint(g, xs, tabs_conj, reverse, *, name):
    length = g.shape[0]
    tb = min(512, length)
    ntb = length // tb
    nlb = S5_NSTATE // S5_LANE_BLOCK
    lb = S5_LANE_BLOCK
    wb = 2 * lb
    ntile = tb // 8
    adj_rev = not reverse
    if reverse:
        edge = jnp.concatenate([xs[tb::tb], jnp.zeros((1, xs.shape[1]), F32)], axis=0)
    else:
        edge = jnp.concatenate([jnp.zeros((1, xs.shape[1]), F32), xs[tb - 1:length - 1:tb]], axis=0)
    edge = edge.reshape(ntb, 1, xs.shape[1])

    def body(tab_ref, g_ref, x_ref, edge_ref, lam_ref, da_ref, carry_ref):
        @pl.when(pl.program_id(1) == 0)
        def _():
            carry_ref[...] = jnp.zeros_like(carry_ref)
            da_ref[...] = jnp.zeros_like(da_ref)

        rows = lax.broadcasted_iota(jnp.int32, (8, wb), 0)

        def step(i, carry):
            prev, acc = carry
            k = ntile - 1 - i if adj_rev else i
            r0 = pl.multiple_of(k * 8, 8)
            lam = _s5_scan_tile(g_ref[pl.ds(r0, 8), :], tab_ref, prev, adj_rev)
            lam_ref[pl.ds(r0, 8), :] = lam
            x = x_ref[pl.ds(r0, 8), :]
            if reverse:
                kn = jnp.minimum(k + 1, ntile - 1)
                nb = x_ref[pl.ds(pl.multiple_of(kn * 8, 8), 8), :][0:1, :]
                nb = jnp.where(k == ntile - 1, edge_ref[0], nb)
                xp = jnp.where(rows == 7, jnp.broadcast_to(nb, (8, wb)), pltpu.roll(x, 7, 0))
            else:
                kn = jnp.maximum(k - 1, 0)
                nb = x_ref[pl.ds(pl.multiple_of(kn * 8, 8), 8), :][7:8, :]
                nb = jnp.where(k == 0, edge_ref[0], nb)
                xp = jnp.where(rows == 0, jnp.broadcast_to(nb, (8, wb)), pltpu.roll(x, 1, 0))
            xr, xi, lr, li = xp[:, :lb], xp[:, lb:], lam[:, :lb], lam[:, lb:]
            acc = acc + jnp.concatenate([xr * lr + xi * li, xr * li - xi * lr], axis=1)
            return lam, acc

        last, acc = lax.fori_loop(0, ntile, step, (carry_ref[...], da_ref[...]))
        carry_ref[...] = last
        da_ref[...] = acc

    tmap = (lambda c, t: (ntb - 1 - t, c)) if adj_rev else (lambda c, t: (t, c))
    emap = (lambda c, t: (ntb - 1 - t, 0, c)) if adj_rev else (lambda c, t: (t, 0, c))
    return pl.pallas_call(
        body, name=name, grid=(nlb, ntb),
        in_specs=[pl.BlockSpec((4, 8, wb), lambda c, t: (0, 0, c)), pl.BlockSpec((tb, wb), tmap),
                  pl.BlockSpec((tb, wb), tmap), pl.BlockSpec((1, 1, wb), emap)],
        out_specs=[pl.BlockSpec((tb, wb), tmap), pl.BlockSpec((8, wb), lambda c, t: (0, c))],
        out_shape=[jax.ShapeDtypeStruct(g.shape, F32), jax.ShapeDtypeStruct((8, g.shape[1]), F32)],
        scratch_shapes=[pltpu.VMEM((8, wb), F32)],
        compiler_params=pltpu.CompilerParams(dimension_semantics=("parallel", "arbitrary")),
    )(tabs_conj, g, xs, edge)


def _s5_prep(lam_re, lam_im, log_dt, b_re, b_im):
    lam = lax.complex(lam_re, lam_im)
    dt = jnp.exp(log_dt)[:, None]
    lam_bar = jnp.exp(lam * dt)
    b_bar = ((lam_bar - 1.0) / lam)[..., None] * lax.complex(b_re, b_im)
    return (jnp.real(lam_bar).reshape(-1), jnp.imag(lam_bar).reshape(-1), jnp.real(b_bar), jnp.imag(b_bar))


S5_NBLK = S5_NSTATE // S5_LANE_BLOCK
S5_BLK_GROUPS = S5_GROUPS // S5_NBLK
S5_BLK_CH = S5_BLK_GROUPS * S5_GROUP_CH


def _s5_in_matrix(bb_re, bb_im):
    eye = jnp.eye(S5_BLK_GROUPS, dtype=F32)
    def dense(bb):
        b4 = bb.reshape(S5_NBLK, S5_BLK_GROUPS, S5_STATE, S5_GROUP_CH)
        return jnp.einsum('cgph,gk->cghkp', b4, eye).reshape(S5_NBLK, S5_BLK_CH, S5_LANE_BLOCK)
    return jnp.concatenate([dense(bb_re), dense(bb_im)], axis=-1)


def _s5_block_diagonal(d):
    d5 = d.reshape(S5_NBLK, S5_BLK_GROUPS, S5_GROUP_CH, S5_BLK_GROUPS, S5_STATE)
    eye = jnp.eye(S5_BLK_GROUPS, dtype=F32)
    return jnp.swapaxes(jnp.sum(d5 * eye[None, :, None, :, None], axis=1), 1, 2)


def _s5_in_matrix_grad(d_mat):
    def diag(d):
        return jnp.swapaxes(_s5_block_diagonal(d), 2, 3).reshape(S5_GROUPS, S5_STATE, S5_GROUP_CH)
    return diag(d_mat[..., :S5_LANE_BLOCK]), diag(d_mat[..., S5_LANE_BLOCK:])


def _s5_out_matrix(c_re, c_im):
    eye = jnp.eye(S5_BLK_GROUPS, dtype=F32)
    def dense(cc):
        c4 = cc.reshape(S5_NBLK, S5_BLK_GROUPS, S5_GROUP_CH, S5_STATE)
        return jnp.einsum('cghp,gk->cgpkh', c4, eye).reshape(S5_NBLK, S5_LANE_BLOCK, S5_BLK_CH)
    return jnp.concatenate([dense(c_re), dense(-c_im)], axis=1)


def _s5_out_matrix_grad(d_mat_t):
    def diag(d):
        return _s5_block_diagonal(d).reshape(S5_GROUPS, S5_GROUP_CH, S5_STATE)
    return diag(d_mat_t[..., :S5_LANE_BLOCK]), -diag(d_mat_t[..., S5_LANE_BLOCK:])


def _gmm(a, b, *, tb=False, name):
    arr, cb0, wa = a
    nblk = b.shape[0]
    wn = b.shape[1] if tb else b.shape[2]
    length = arr.shape[0]
    tm = _pick(length, (1024, 512, 256, 128))
    dims = (((1,), (1 if tb else 0,)), ((), ()))

    def body(a_ref, b_ref, o_ref):
        o_ref[...] = lax.dot_general(a_ref[...].astype(BF16), b_ref[0].astype(BF16), dims, preferred_element_type=F32)

    return pl.pallas_call(
        body, name=name, grid=(nblk, length // tm),
        in_specs=[pl.BlockSpec((tm, wa), lambda c, i: (i, cb0 + c)), pl.BlockSpec((1,) + b.shape[1:], lambda c, i: (c, 0, 0))],
        out_specs=pl.BlockSpec((tm, wn), lambda c, i: (i, c)), out_shape=jax.ShapeDtypeStruct((length, nblk * wn), F32),
        compiler_params=pltpu.CompilerParams(dimension_semantics=("parallel", "parallel")),
    )(arr, b)


def _gmm_tn(a, g, *, nblk, name):
    arr_a, cb_a, wa = a
    arr_g, cb_g, wg = g
    length = arr_a.shape[0]
    dims = (((0,), (0,)), ((), ()))

    def body(a_ref, g_ref, o_ref):
        o_ref[0] = lax.dot_general(a_ref[...].astype(BF16), g_ref[...].astype(BF16), dims, preferred_element_type=F32)

    return pl.pallas_call(
        body, name=name, grid=(nblk,),
        in_specs=[pl.BlockSpec((length, wa), lambda c: (0, cb_a + c)), pl.BlockSpec((length, wg), lambda c: (0, cb_g + c))],
        out_specs=pl.BlockSpec((1, wa, wg), lambda c: (c, 0, 0)), out_shape=jax.ShapeDtypeStruct((nblk, wa, wg), F32),
        compiler_params=pltpu.CompilerParams(dimension_semantics=("parallel",)),
    )(arr_a, arr_g)


def _gelu_parts(x):
    k = math.sqrt(2.0 / math.pi)
    inner = k * (x + 0.044715 * x * x * x)
    th = jnp.tanh(inner)
    return th, k * (1.0 + 3.0 * 0.044715 * x * x)


S5_CB_U = CB_U * (512 // S5_BLK_CH)


def _s5_direction_fwd(pg, b_mat, c_mat, tabs, reverse, *, name):
    length = pg.shape[0]
    tb = min(512, length)
    ntb = length // tb
    wb = 2 * S5_LANE_BLOCK
    ntile = tb // 8

    def body(tab_ref, u_ref, b_ref, c_ref, x_ref, y_ref, ends_ref, carry_ref, bu_ref):
        @pl.when(pl.program_id(1) == 0)
        def _():
            carry_ref[...] = jnp.zeros_like(carry_ref)

        bu_ref[...] = jnp.dot(u_ref[...].astype(BF16), b_ref[0], preferred_element_type=F32)

        def step(i, prev):
            r0 = pl.multiple_of((ntile - 1 - i if reverse else i) * 8, 8)
            x = _s5_scan_tile(bu_ref[pl.ds(r0, 8), :], tab_ref, prev, reverse)
            x_ref[pl.ds(r0, 8), :] = x
            return x

        carry_ref[...] = lax.fori_loop(0, ntile, step, carry_ref[...])
        y_ref[...] = jnp.dot(x_ref[...].astype(BF16), c_ref[0], preferred_element_type=F32)
        ends_ref[0, 0:8, :] = x_ref[0:8, :]
        ends_ref[0, 8:16, :] = x_ref[tb - 8:tb, :]

    tix = (lambda t: ntb - 1 - t) if reverse else (lambda t: t)
    return pl.pallas_call(
        body, name=name, grid=(S5_NBLK, ntb),
        in_specs=[pl.BlockSpec((4, 8, wb), lambda c, t: (0, 0, c)),
                  pl.BlockSpec((tb, S5_BLK_CH), lambda c, t: (tix(t), S5_CB_U + c)),
                  pl.BlockSpec((1, S5_BLK_CH, wb), lambda c, t: (c, 0, 0)),
                  pl.BlockSpec((1, wb, S5_BLK_CH), lambda c, t: (c, 0, 0))],
        out_specs=[pl.BlockSpec((tb, wb), lambda c, t: (tix(t), c)), pl.BlockSpec((tb, S5_BLK_CH), lambda c, t: (tix(t), c)),
                   pl.BlockSpec((1, 16, wb), lambda c, t: (tix(t), 0, c))],
        out_shape=[jax.ShapeDtypeStruct((length, S5_NBLK * wb), F32), jax.ShapeDtypeStruct((length, S5_WIDTH), F32),
                   jax.ShapeDtypeStruct((ntb, 16, S5_NBLK * wb), F32)],
        scratch_shapes=[pltpu.VMEM((8, wb), F32), pltpu.VMEM((tb, wb), F32)],
        compiler_params=pltpu.CompilerParams(dimension_semantics=("parallel", "arbitrary")),
    )(tabs, pg, b_mat, c_mat)


def _s5_direction_bwd(pg, dy, xs, ends, b_mat, c_mat, tabs_conj, reverse, *, name):
    length = pg.shape[0]
    tb = min(512, length)
    ntb = length // tb
    lb = S5_LANE_BLOCK
    wb = 2 * lb
    ntile = tb // 8
    adj_rev = not reverse
    if reverse:
        edge = jnp.concatenate([ends[1:, 0], jnp.zeros((1, xs.shape[1]), F32)], axis=0)
    else:
        edge = jnp.concatenate([jnp.zeros((1, xs.shape[1]), F32), ends[:-1, 15]], axis=0)
    edge = edge.reshape(ntb, 1, xs.shape[1])

    def body(tab_ref, u_ref, dy_ref, x_ref, edge_ref, b_ref, c_ref, du_ref, db_ref, dc_ref, da_ref, carry_ref, g_ref, lam_ref):
        @pl.when(pl.program_id(1) == 0)
        def _():
            carry_ref[...] = jnp.zeros_like(carry_ref)
            da_ref[...] = jnp.zeros_like(da_ref)
            db_ref[...] = jnp.zeros_like(db_ref)
            dc_ref[...] = jnp.zeros_like(dc_ref)

        dyb = dy_ref[...].astype(BF16)
        g_ref[...] = lax.dot_general(dyb, c_ref[0], _NT, preferred_element_type=F32)
        rows = lax.broadcasted_iota(jnp.int32, (8, wb), 0)

        def step(i, carry):
            prev, acc = carry
            k = ntile - 1 - i if adj_rev else i
            r0 = pl.multiple_of(k * 8, 8)
            lam = _s5_scan_tile(g_ref[pl.ds(r0, 8), :], tab_ref, prev, adj_rev)
            lam_ref[pl.ds(r0, 8), :] = lam
            x = x_ref[pl.ds(r0, 8), :]
            if reverse:
                kn = jnp.minimum(k + 1, ntile - 1)
                nb = x_ref[pl.ds(pl.multiple_of(kn * 8, 8), 8), :][0:1, :]
                nb = jnp.where(k == ntile - 1, edge_ref[0], nb)
                xp = jnp.where(rows == 7, jnp.broadcast_to(nb, (8, wb)), pltpu.roll(x, 7, 0))
            else:
                kn = jnp.maximum(k - 1, 0)
                nb = x_ref[pl.ds(pl.multiple_of(kn * 8, 8), 8), :][7:8, :]
                nb = jnp.where(k == 0, edge_ref[0], nb)
                xp = jnp.where(rows == 0, jnp.broadcast_to(nb, (8, wb)), pltpu.roll(x, 1, 0))
            xr, xi, lr, li = xp[:, :lb], xp[:, lb:], lam[:, :lb], lam[:, lb:]
            return lam, acc + jnp.concatenate([xr * lr + xi * li, xr * li - xi * lr], axis=1)

        last, acc = lax.fori_loop(0, ntile, step, (carry_ref[...], da_ref[...]))
        carry_ref[...] = last
        da_ref[...] = acc
        lamb = lam_ref[...].astype(BF16)
        du_ref[...] = lax.dot_general(lamb, b_ref[0], _NT, preferred_element_type=F32)
        db_ref[0] += lax.dot_general(u_ref[...].astype(BF16), lamb, _TN, preferred_element_type=F32)
        dc_ref[0] += lax.dot_general(dyb, x_ref[...].astype(BF16), _TN, preferred_element_type=F32)

    tix = (lambda t: ntb - 1 - t) if adj_rev else (lambda t: t)
    wide = pl.BlockSpec((tb, wb), lambda c, t: (tix(t), c))
    mat = pl.BlockSpec((1, S5_BLK_CH, wb), lambda c, t: (c, 0, 0))
    return pl.pallas_call(
        body, name=name, grid=(S5_NBLK, ntb),
        in_specs=[pl.BlockSpec((4, 8, wb), lambda c, t: (0, 0, c)),
                  pl.BlockSpec((tb, S5_BLK_CH), lambda c, t: (tix(t), S5_CB_U + c)),
                  pl.BlockSpec((tb, S5_BLK_CH), lambda c, t: (tix(t), c)), wide,
                  pl.BlockSpec((1, 1, wb), lambda c, t: (tix(t), 0, c)), mat,
                  pl.BlockSpec((1, wb, S5_BLK_CH), lambda c, t: (c, 0, 0))],
        out_specs=[pl.BlockSpec((tb, S5_BLK_CH), lambda c, t: (tix(t), c)), mat, mat, pl.BlockSpec((8, wb), lambda c, t: (0, c))],
        out_shape=[jax.ShapeDtypeStruct((length, S5_WIDTH), F32), jax.ShapeDtypeStruct((S5_NBLK, S5_BLK_CH, wb), F32),
                   jax.ShapeDtypeStruct((S5_NBLK, S5_BLK_CH, wb), F32), jax.ShapeDtypeStruct((8, S5_NBLK * wb), F32)],
        scratch_shapes=[pltpu.VMEM((8, wb), F32), pltpu.VMEM((tb, wb), F32), pltpu.VMEM((tb, wb), F32)],
        compiler_params=pltpu.CompilerParams(dimension_semantics=("parallel", "arbitrary")),
    )(tabs_conj, pg, dy, xs, edge, b_mat, c_mat)


def _both(fn):
    return jax.vmap(jax.vmap(fn))


def _s5_setup(w):
    a_re, a_im, bb_re, bb_im = _both(_s5_prep)(w['s5_lambda_re'], w['s5_lambda_im'], w['s5_log_dt'], w['s5_b_re'], w['s5_b_im'])

    def tables(d, conj, reverse):
        return jax.vmap(lambda r, i: _s5_tables(r, -i if conj else i, reverse))(a_re[:, d], a_im[:, d])
    return {'b_mat': _both(_s5_in_matrix)(bb_re, bb_im).astype(BF16),
            'c_mat': _both(_s5_out_matrix)(w['s5_c_re'], w['s5_c_im']).astype(BF16),
            'tabs': [tables(0, False, False), tables(1, False, True)],
            'tabs_adj': [tables(0, True, True), tables(1, True, False)]}


def _s5_param_grads(w, raws):
    def stacked(k):
        return jnp.stack([jnp.stack([raws[i][d][k] for d in range(2)]) for i in range(DEPTH)])
    dbb_re, dbb_im = _both(_s5_in_matrix_grad)(stacked(0))
    dc_re, dc_im = _both(_s5_out_matrix_grad)(stacked(1))
    da_re, da_im = _s5_unblocked(jnp.sum(stacked(2), axis=2))
    _, vjp = jax.vjp(_both(_s5_prep), w['s5_lambda_re'], w['s5_lambda_im'], w['s5_log_dt'], w['s5_b_re'], w['s5_b_im'])
    g = vjp((da_re, da_im, dbb_re, dbb_im))
    return {'s5_lambda_re': g[0], 's5_lambda_im': g[1], 's5_log_dt': g[2], 's5_b_re': g[3], 's5_b_im': g[4],
            's5_c_re': dc_re, 's5_c_im': dc_im}


def _s5_fwd(p_in, prm, w_glu):
    dirs = []
    ys = []
    for d, reverse in ((0, False), (1, True)):
        xs, y_dir, ends = _s5_direction_fwd(p_in, prm['b_mat'][d], prm['c_mat'][d], prm['tabs'][d], reverse,
                                            name="s5_fwd_rev" if reverse else "s5_fwd")
        ys.append(y_dir)
        dirs.append((xs, ends))

    def post(yf, yb, u, dskip):
        ypre = yf + yb + dskip * u
        th, _ = _gelu_parts(ypre)
        return ypre, 0.5 * ypre * (1.0 + th)
    ypre, yg = _rowmap(post, [ys[0], ys[1], (p_in, CB_U, S5_WIDTH)], [prm['d'].reshape(1, -1)],
                       [(S5_WIDTH, F32), (S5_WIDTH, F32)], tl=512, name="s5_post")
    t = _mm(yg, w_glu, name="s5_glu_mm")

    def glu(ygv, tv):
        return ygv * _sigmoid(tv)
    y = _rowmap(glu, [yg, t], [], [(S5_WIDTH, BF16)], tl=512, name="s5_glu")[0]
    return y, (dirs, ypre, yg, t)


def _s5_bwd(pg, prm, w_glu, saved, dy):
    dirs, ypre, yg, t = saved

    def glu_bwd(dyv, ygv, tv):
        s = _sigmoid(tv)
        return dyv * ygv * s * (1.0 - s), dyv * s
    dt, dyg_direct = _rowmap(glu_bwd, [dy, yg, t], [], [(S5_WIDTH, BF16), (S5_WIDTH, F32)], tl=512, name="s5_glu_bwd")
    grads = {'w_glu': _mm(yg, dt, ta=True, out_dtype=BF16, name="s5_dwglu")}
    dyg_mm = _mm(dt, w_glu, tb=True, name="s5_dyg")

    def post_bwd(dyd, dym, yp, u, dskip):
        th, dinner = _gelu_parts(yp)
        dyp = (dyd + dym) * (0.5 * (1.0 + th) + 0.5 * yp * (1.0 - th * th) * dinner)
        return dyp, dyp * dskip, jnp.sum(dyp * u, axis=0, keepdims=True)
    dyp, du_skip, dd = _rowmap(post_bwd, [dyg_direct, dyg_mm, ypre, (pg, CB_U, S5_WIDTH)], [prm['d'].reshape(1, -1)],
                               [(S5_WIDTH, F32), (S5_WIDTH, F32)], [(1, S5_WIDTH)], tl=512, name="s5_post_bwd")
    grads['d'] = dd[0]
    du = [du_skip]
    grads['raw'] = []
    for d, reverse in ((0, False), (1, True)):
        du_dir, d_bmat, d_cmat_t, da = _s5_direction_bwd(pg, dyp, *dirs[d], prm['b_mat'][d], prm['c_mat'][d], prm['tabs_adj'][d],
                                                         reverse, name="s5_bwd_rev" if reverse else "s5_bwd")
        du.append(du_dir)
        grads['raw'].append((d_bmat, d_cmat_t, da))
    return du, grads


def _split3(x):
    hi = x.astype(BF16)
    r = x - hi.astype(F32)
    mid = r.astype(BF16)
    return hi, mid, (r - mid.astype(F32)).astype(BF16)


def _exact_dot(ones, x, dims):
    parts = [lax.dot_general(ones, p, dims, preferred_element_type=F32) for p in _split3(x)]
    return parts[0] + parts[1] + parts[2]


_NN = (((1,), (0,)), ((), ()))
_NT = (((1,), (1,)), ((), ()))
_TN = (((0,), (0,)), ((), ()))


def _dot(a, b, dims=_NN):
    return lax.dot_general(a.astype(BF16), b.astype(BF16), dims, preferred_element_type=F32)


def _gla_chunk_mask(reverse):
    rows = lax.broadcasted_iota(jnp.int32, (GLA_CHUNK, GLA_CHUNK), 0)
    cols = lax.broadcasted_iota(jnp.int32, (GLA_CHUNK, GLA_CHUNK), 1)
    return (cols >= rows) if reverse else (cols <= rows)


def _gla_fwd(pg, la, reverse, *, name):
    length = la.shape[0]
    nch = length // GLA_CHUNK
    scale = GLA_HEAD_DIM ** -0.5
    last = 0 if reverse else GLA_CHUNK - 1
    hd = GLA_HEAD_DIM

    def body(q_ref, k_ref, v_ref, la_ref, o_ref, sp_ref, st_ref):
        @pl.when(pl.program_id(0) == 0)
        def _():
            st_ref[...] = jnp.zeros_like(st_ref)

        mask = _gla_chunk_mask(reverse)
        b = _exact_dot(mask.astype(BF16), la_ref[...], _NN)
        sp_ref[0] = st_ref[...]
        outs = []
        for h in range(GLA_HEADS):
            sl = slice(h * hd, (h + 1) * hd)
            bh = b[:, sl]
            bl = bh[last:last + 1, :]
            k = k_ref[:, sl]
            v = v_ref[:, sl]
            qd = q_ref[:, sl] * scale * jnp.exp(bh)
            kd = k * jnp.exp(-bh)
            ke = k * jnp.exp(bl - bh)
            st = st_ref[sl, :]
            p = jnp.where(mask, _dot(qd, kd, _NT), 0.0)
            outs.append(_dot(p, v) + _dot(qd, st, _NT))
            st_ref[sl, :] = st * jnp.exp(bl) + _dot(v, ke, _TN)
        o_ref[...] = jnp.concatenate(outs, axis=1)

    cmap = (lambda n: nch - 1 - n) if reverse else (lambda n: n)
    col = lambda cb: pl.BlockSpec((GLA_CHUNK, GLA_WIDTH), lambda n, cb=cb: (cmap(n), cb))
    return pl.pallas_call(
        body, name=name, grid=(nch,),
        in_specs=[col(CB_GQ), col(CB_GK), col(CB_GV), col(0)],
        out_specs=[col(0), pl.BlockSpec((1, GLA_WIDTH, hd), lambda n: (cmap(n), 0, 0))],
        out_shape=[jax.ShapeDtypeStruct((length, GLA_WIDTH), F32), jax.ShapeDtypeStruct((nch, GLA_WIDTH, hd), F32)],
        scratch_shapes=[pltpu.VMEM((GLA_WIDTH, hd), F32)],
        compiler_params=pltpu.CompilerParams(dimension_semantics=("arbitrary",)),
    )(pg, pg, pg, la)


def _gla_bwd(pg, la, do, sprev, reverse, *, name):
    length = la.shape[0]
    nch = length // GLA_CHUNK
    scale = GLA_HEAD_DIM ** -0.5
    last = 0 if reverse else GLA_CHUNK - 1
    hd = GLA_HEAD_DIM

    def body(q_ref, k_ref, v_ref, la_ref, do_ref, sp_ref, dq_ref, dk_ref, dv_ref, dla_ref, dst_ref):
        @pl.when(pl.program_id(0) == 0)
        def _():
            dst_ref[...] = jnp.zeros_like(dst_ref)

        mask = _gla_chunk_mask(reverse)
        tri = mask.astype(BF16)
        b = _exact_dot(tri, la_ref[...], _NN)
        is_last = lax.broadcasted_iota(jnp.int32, (GLA_CHUNK, hd), 0) == last
        dqs, dks, dvs, dbs = [], [], [], []
        for h in range(GLA_HEADS):
            sl = slice(h * hd, (h + 1) * hd)
            bh = b[:, sl]
            bl = bh[last:last + 1, :]
            eb, enb, ebl, el = jnp.exp(bh), jnp.exp(-bh), jnp.exp(bl - bh), jnp.exp(bl)
            k = k_ref[:, sl]
            v = v_ref[:, sl]
            dov = do_ref[:, sl]
            qd = q_ref[:, sl] * scale * eb
            kd = k * enb
            ke = k * ebl
            st = sp_ref[0, sl, :]
            dst = dst_ref[sl, :]
            p = jnp.where(mask, _dot(qd, kd, _NT), 0.0)
            dp = jnp.where(mask, _dot(dov, v, _NT), 0.0)
            dqd = _dot(dp, kd) + _dot(dov, st)
            dkd = _dot(dp, qd, _TN)
            dvs.append(_dot(p, dov, _TN) + _dot(ke, dst, _NT))
            dke = _dot(v, dst)
            dst_ref[sl, :] = dst * el + _dot(dov, qd, _TN)
            dbl = el * jnp.sum(dst * st, axis=0, keepdims=True) + jnp.sum(dke * ke, axis=0, keepdims=True)
            db = dqd * qd - dkd * kd - dke * ke
            dbs.append(jnp.where(is_last, db + dbl, db))
            dqs.append(dqd * eb * scale)
            dks.append(dkd * enb + dke * ebl)
        dq_ref[...] = jnp.concatenate(dqs, axis=1)
        dk_ref[...] = jnp.concatenate(dks, axis=1)
        dv_ref[...] = jnp.concatenate(dvs, axis=1)
        tri_t = _gla_chunk_mask(not reverse).astype(BF16)
        dla_ref[...] = _exact_dot(tri_t, jnp.concatenate(dbs, axis=1), _NN)

    cmap = (lambda n: n) if reverse else (lambda n: nch - 1 - n)
    col = lambda cb: pl.BlockSpec((GLA_CHUNK, GLA_WIDTH), lambda n, cb=cb: (cmap(n), cb))
    wide = jax.ShapeDtypeStruct((length, GLA_WIDTH), F32)
    return pl.pallas_call(
        body, name=name, grid=(nch,),
        in_specs=[col(CB_GQ), col(CB_GK), col(CB_GV), col(0), col(0),
                  pl.BlockSpec((1, GLA_WIDTH, hd), lambda n: (cmap(n), 0, 0))],
        out_specs=[col(0)] * 4, out_shape=[wide] * 4,
        scratch_shapes=[pltpu.VMEM((GLA_WIDTH, hd), F32)],
        compiler_params=pltpu.CompilerParams(dimension_semantics=("arbitrary",)),
    )(pg, pg, pg, la, do, sprev)


def _log_sigmoid(x):
    return jnp.minimum(x, 0.0) - jnp.log(1.0 + jnp.exp(-jnp.abs(x)))


def _gla_alpha_padded(w_alpha):
    w = jnp.zeros((2, 128, GLA_WIDTH), w_alpha.dtype)
    w = w.at[0, 0:GLA_LOWRANK].set(w_alpha[0])
    return w.at[1, GLA_LOWRANK:2 * GLA_LOWRANK].set(w_alpha[1])


def _gla_branch_fwd(pg, w_alpha, b_alpha, norm_gain):
    wa = _gla_alpha_padded(w_alpha).astype(BF16)

    def gates(z, w, bias):
        return (_log_sigmoid(_dot(z, w[0]) + bias[0:1]) / GLA_TAU, _log_sigmoid(_dot(z, w[1]) + bias[1:2]) / GLA_TAU)
    la_f, la_b = _rowmap(gates, [(pg, CB_Z, 128)], [wa, b_alpha], [(GLA_WIDTH, F32), (GLA_WIDTH, F32)], tl=512,
                         name="gla_gates")
    o_f, sp_f = _gla_fwd(pg, la_f, False, name="gla_fwd")
    o_b, sp_b = _gla_fwd(pg, la_b, True, name="gla_fwd_rev")

    def post(of, ob, gate, gn):
        o = of + ob
        on = jnp.concatenate([o[:, s:s + GLA_HEAD_DIM] * _rms(o[:, s:s + GLA_HEAD_DIM]) * gn
                              for s in range(0, GLA_WIDTH, GLA_HEAD_DIM)], axis=1)
        return o, on * (gate * _sigmoid(gate))
    o, y = _rowmap(post, [o_f, o_b, (pg, CB_GG, GLA_WIDTH)], [norm_gain.reshape(1, -1)],
                   [(GLA_WIDTH, F32), (GLA_WIDTH, BF16)], tl=512, name="gla_post")
    return y, (wa, la_f, la_b, sp_f, sp_b, o)


def _gla_branch_bwd(pg, w_alpha, b_alpha, norm_gain, saved, dy):
    wa, la_f, la_b, sp_f, sp_b, o = saved

    def post_bwd(dyv, ov, gate, gn):
        s = _sigmoid(gate)
        dos, dgn, ons = [], [], []
        for c in range(0, GLA_WIDTH, GLA_HEAD_DIM):
            oh = ov[:, c:c + GLA_HEAD_DIM]
            r = _rms(oh)
            don = dyv[:, c:c + GLA_HEAD_DIM] * (gate[:, c:c + GLA_HEAD_DIM] * s[:, c:c + GLA_HEAD_DIM])
            gd = don * gn
            dos.append(r * gd - oh * (r * r * r) * jnp.mean(oh * gd, axis=-1, keepdims=True))
            dgn.append(jnp.sum(don * oh * r, axis=0, keepdims=True))
            ons.append(oh * r * gn)
        on = jnp.concatenate(ons, axis=1)
        dgate = dyv * on * (s * (1.0 + gate * (1.0 - s)))
        return jnp.concatenate(dos, axis=1), dgate, jnp.concatenate(dgn, axis=1)
    do, dgate, dgn = _rowmap(post_bwd, [dy, o, (pg, CB_GG, GLA_WIDTH)], [norm_gain.reshape(1, -1)],
                             [(GLA_WIDTH, F32), (GLA_WIDTH, F32)], [(1, GLA_WIDTH)], tl=512, name="gla_post_bwd")
    dq_f, dk_f, dv_f, dla_f = _gla_bwd(pg, la_f, do, sp_f, False, name="gla_bwd")
    dq_b, dk_b, dv_b, dla_b = _gla_bwd(pg, la_b, do, sp_b, True, name="gla_bwd_rev")

    def gates_bwd(z, dlf, dlb, w, bias):
        dz = jnp.zeros_like(z)
        dlogits, dbs = [], []
        for d, dl in ((0, dlf), (1, dlb)):
            logit = _dot(z, w[d]) + bias[d:d + 1]
            dlogit = dl * (1.0 / GLA_TAU) * jnp.exp(_log_sigmoid(-logit))
            dz = dz + _dot(dlogit, w[d], _NT)
            dlogits.append(dlogit)
            dbs.append(jnp.sum(dlogit, axis=0, keepdims=True))
        return dz, dlogits[0], dlogits[1], dbs[0], dbs[1]
    dz, dlg_f, dlg_b, dba_f, dba_b = _rowmap(
        gates_bwd, [(pg, CB_Z, 128), dla_f, dla_b], [wa, b_alpha], [(128, F32), (GLA_WIDTH, BF16), (GLA_WIDTH, BF16)],
        [(1, GLA_WIDTH), (1, GLA_WIDTH)], tl=512, name="gla_gates_bwd")
    dwa_f = _mm(dlg_f, (pg, CB_Z, 128), ta=True, name="gla_dwalpha")
    dwa_b = _mm(dlg_b, (pg, CB_Z, 128), ta=True, name="gla_dwalpha")
    grads = {'w_alpha': jnp.stack([dwa_f[:, 0:GLA_LOWRANK].T, dwa_b[:, GLA_LOWRANK:2 * GLA_LOWRANK].T]),
             'b_alpha': jnp.concatenate([dba_f, dba_b], axis=0),
             'norm': jnp.sum(dgn.reshape(GLA_HEADS, GLA_HEAD_DIM), axis=0)}
    return [dq_f, dq_b], [dk_f, dk_b], [dv_f, dv_b], dgate, dz, grads


def _rope_tables(length):
    half = ATTN_HEAD_DIM // 2
    inv_freq = ROPE_BASE ** (-jnp.arange(half // 2, dtype=F32) * 2.0 / half)
    t = jnp.arange(length, dtype=jnp.int32)
    def one(pos):
        ang = pos.astype(F32)[:, None] * inv_freq[None, :]
        c, s = jnp.cos(ang), jnp.sin(ang)
        return jnp.concatenate([c, c], axis=1), jnp.concatenate([-s, s], axis=1)
    c_r, s_r = one(t // GRID_W)
    c_c, s_c = one(t % GRID_W)
    return jnp.concatenate([c_r, c_c], axis=1), jnp.concatenate([s_r, s_c], axis=1)


def _rope_swap(y):
    w = y.shape[1]
    lane = lax.broadcasted_iota(jnp.int32, y.shape, 1)
    return jnp.where(lane % 32 < 16, pltpu.roll(y, w - 16, 1), pltpu.roll(y, 16, 1))


def _head_sums(x, ones):
    parts = [lax.dot_general(p, ones, _NN, preferred_element_type=F32) for p in _split3(x)]
    return parts[0] + parts[1] + parts[2]


def _head_ones(width):
    seg = np.arange(width) // ATTN_HEAD_DIM
    return jnp.asarray(seg[:, None] == seg[None, :], BF16)


def _qk_prep_fwd(pg, cb, width, gain, cos, sin, scale, *, name):
    heads = width // ATTN_HEAD_DIM
    def fn(x, c, s, g, ones):
        r = lax.rsqrt(_head_sums(x * x, ones) * (1.0 / ATTN_HEAD_DIM) + NORM_EPS)
        y = x * r * g
        return (y * c + _rope_swap(y) * s) * scale
    return _rowmap(fn, [(pg, cb, width), jnp.tile(cos, (1, heads)), jnp.tile(sin, (1, heads))],
                   [jnp.tile(gain, heads).reshape(1, -1), _head_ones(width)], [(width, BF16)], tl=512, name=name)[0]


def _qk_prep_bwd(pg, cb, width, gain, cos, sin, scale, dout, *, name):
    heads = width // ATTN_HEAD_DIM
    def fn(x, dov, c, s, g, ones):
        r = lax.rsqrt(_head_sums(x * x, ones) * (1.0 / ATTN_HEAD_DIM) + NORM_EPS)
        dos = dov * scale
        dy = dos * c + _rope_swap(dos * s)
        gd = dy * g
        dx = r * gd - x * (r * r * r) * (_head_sums(x * gd, ones) * (1.0 / ATTN_HEAD_DIM))
        return dx, jnp.sum(dy * x * r, axis=0, keepdims=True)
    dx, dg = _rowmap(fn, [(pg, cb, width), dout, jnp.tile(cos, (1, heads)), jnp.tile(sin, (1, heads))],
                     [jnp.tile(gain, heads).reshape(1, -1), _head_ones(width)], [(width, F32)], [(1, width)], tl=512,
                     name=name)
    return dx, jnp.sum(dg.reshape(heads, ATTN_HEAD_DIM), axis=0)


def _to_heads(x, heads):
    return jnp.transpose(x.reshape(x.shape[0], heads, ATTN_HEAD_DIM), (1, 0, 2))


def _from_heads(x):
    return jnp.transpose(x, (1, 0, 2)).reshape(x.shape[1], x.shape[0] * ATTN_HEAD_DIM)


ATTN_GROUP = ATTN_Q_HEADS // ATTN_KV_HEADS
ATTN_TQ = 256


def _attn_fwd(q, k, v, side=None):
    length = q.shape[1]
    tq = min(ATTN_TQ, length)
    grid = (ATTN_KV_HEADS, length // tq)

    def compute(refs):
        q_ref, k_ref, v_ref, o_ref = refs
        kk, vv = k_ref[0], v_ref[0]
        for g in range(ATTN_GROUP):
            s = _dot(q_ref[g], kk, _NT)
            p = jnp.exp(s - jnp.max(s, axis=-1, keepdims=True))
            o_ref[g] = _dot(p, vv) / jnp.sum(p, axis=-1, keepdims=True)

    def body(*refs):
        _carried(side, grid, refs, 3, 1, 0, compute)

    kv_spec = pl.BlockSpec((1, length, ATTN_HEAD_DIM), lambda h, i: (h, 0, 0))
    q_spec = pl.BlockSpec((ATTN_GROUP, tq, ATTN_HEAD_DIM), lambda h, i: (h, i, 0))
    (out,), gathered = _side_call(
        body, side, name="attn_fwd", grid=grid, in_specs=[q_spec, kv_spec, kv_spec], out_specs=[q_spec],
        out_shape=[jax.ShapeDtypeStruct(q.shape, F32)], scratch=[], args=[q, k, v], semantics=("parallel", "parallel"))
    return out, gathered


def _attn_bwd(q, k, v, o, do, side=None):
    length = q.shape[1]
    tq = min(ATTN_TQ, length)
    grid = (ATTN_KV_HEADS, length // tq)

    def body(*refs):
        _carried(side, grid, refs, 5, 3, 0, compute)

    def compute(refs):
        q_ref, k_ref, v_ref, o_ref, do_ref, dq_ref, dk_ref, dv_ref = refs

        @pl.when(pl.program_id(1) == 0)
        def _():
            dk_ref[...] = jnp.zeros_like(dk_ref)
            dv_ref[...] = jnp.zeros_like(dv_ref)

        kk, vv = k_ref[0], v_ref[0]
        for g in range(ATTN_GROUP):
            qg, dog = q_ref[g], do_ref[g]
            s = _dot(qg, kk, _NT)
            p = jnp.exp(s - jnp.max(s, axis=-1, keepdims=True))
            p = p / jnp.sum(p, axis=-1, keepdims=True)
            dp = _dot(dog, vv, _NT)
            ds = p * (dp - jnp.sum(dog * o_ref[g], axis=-1, keepdims=True))
            dq_ref[g] = _dot(ds, kk)
            dk_ref[0] += _dot(ds, qg, _TN)
            dv_ref[0] += _dot(p, dog, _TN)

    kv_spec = pl.BlockSpec((1, length, ATTN_HEAD_DIM), lambda h, i: (h, 0, 0))
    q_spec = pl.BlockSpec((ATTN_GROUP, tq, ATTN_HEAD_DIM), lambda h, i: (h, i, 0))
    return _side_call(
        body, side, name="attn_bwd", grid=grid, in_specs=[q_spec, kv_spec, kv_spec, q_spec, q_spec],
        out_specs=[q_spec, kv_spec, kv_spec],
        out_shape=[jax.ShapeDtypeStruct(q.shape, F32), jax.ShapeDtypeStruct(k.shape, F32), jax.ShapeDtypeStruct(k.shape, F32)],
        scratch=[], args=[q, k, v, o, do], semantics=("parallel", "arbitrary"))


def _attn_branch_fwd(pg, q_gain, k_gain, side=None):
    cos, sin = _rope_tables(pg.shape[0])
    qp = _qk_prep_fwd(pg, CB_AQ, ATTN_WIDTH, q_gain, cos, sin, ATTN_HEAD_DIM ** -0.5, name="attn_q_prep")
    kp = _qk_prep_fwd(pg, CB_AK, ATTN_KV_WIDTH, k_gain, cos, sin, 1.0, name="attn_k_prep")
    qh, kh = _to_heads(qp, ATTN_Q_HEADS), _to_heads(kp, ATTN_KV_HEADS)
    vh = _to_heads(pg[:, P_OFF + 3200:P_OFF + 3328].astype(BF16), ATTN_KV_HEADS)
    oh, gathered = _attn_fwd(qh, kh, vh, side)
    return _from_heads(oh).astype(BF16), (cos, sin, qh, kh, vh, oh), gathered


def _attn_branch_bwd(pg, q_gain, k_gain, saved, dy, side=None):
    cos, sin, qh, kh, vh, oh = saved
    (dqh, dkh, dvh), carried = _attn_bwd(qh, kh, vh, oh, _to_heads(dy, ATTN_Q_HEADS), side)
    dq, dqg = _qk_prep_bwd(pg, CB_AQ, ATTN_WIDTH, q_gain, cos, sin, ATTN_HEAD_DIM ** -0.5, _from_heads(dqh),
                           name="attn_q_prep_bwd")
    dk, dkg = _qk_prep_bwd(pg, CB_AK, ATTN_KV_WIDTH, k_gain, cos, sin, 1.0, _from_heads(dkh), name="attn_k_prep_bwd")
    return dq, dk, _from_heads(dvh), {'q_norm': dqg, 'k_norm': dkg}, carried


def _gate_cols():
    return [slice(i * D_MODEL, (i + 1) * D_MODEL) for i in range(3)]


def _mixer_fwd(x, lw, side_in=None, after_in=None, side_attn=None):
    h = _rmsnorm_fwd(x, lw['mix_norm'])
    if side_in is None:
        pg = _mm(h, lw['w_pg'], name="mix_in")
    else:
        pg, got_in = _mm(h, lw['w_pg'], side=side_in, name="mix_in")
        after_in(got_in)
    y_s5, s_s5 = _s5_fwd(pg, lw['s5'], lw['s5_w_glu'])
    y_gla, s_gla = _gla_branch_fwd(pg, lw['gla_w_alpha'], lw['gla_b_alpha'], lw['gla_norm'])
    y_att, s_att, got_attn = _attn_branch_fwd(pg, lw['attn_q_norm'], lw['attn_k_norm'], side_attn)
    ys = (y_s5, y_gla, y_att)
    br = [_mm(y, lw[n], name="mix_branch") for y, n in zip(ys, ('w_branch_s5', 'w_branch_gla', 'w_branch_attn'))]

    def merge(g0, g1, g2, b0, b1, b2, bias):
        acc = None
        for g, b, c in zip((g0, g1, g2), (b0, b1, b2), _gate_cols()):
            term = _sigmoid(g + bias[:, c]) * b
            acc = term if acc is None else acc + term
        return acc
    merged = _rowmap(merge, [(pg, 0, D_MODEL), (pg, 1, D_MODEL), (pg, 2, D_MODEL)] + br,
                     [lw['b_merge_gate'].reshape(1, -1)], [(D_MODEL, BF16)], tl=256, name="mix_merge")[0]
    x_out = _mm(merged, lw['w_out'], add=x, name="mix_out")
    return x_out, (x, h, pg, ys, (s_s5, s_gla, s_att), br, merged), got_attn


def _mixer_bwd(saved, lw, dx_out, side=None):
    x, h, pg, ys, (s_s5, s_gla, s_att), br, merged = saved
    grads = {'w_out': _mm(merged, dx_out, ta=True, out_dtype=BF16, name="mix_dwout")}
    dmerged = _mm(dx_out, lw['w_out'], tb=True, name="mix_dmerged")

    def merge_bwd(g0, g1, g2, b0, b1, b2, dm, bias):
        dbr, dgp = [], []
        for g, b, c in zip((g0, g1, g2), (b0, b1, b2), _gate_cols()):
            s = _sigmoid(g + bias[:, c])
            dbr.append(dm * s)
            dgp.append(dm * b * (s * (1.0 - s)))
        dgp = jnp.concatenate(dgp, axis=1)
        return dbr[0], dbr[1], dbr[2], dgp, jnp.sum(dgp, axis=0, keepdims=True)
    d0, d1, d2, dgpre, dbias = _rowmap(
        merge_bwd, [(pg, 0, D_MODEL), (pg, 1, D_MODEL), (pg, 2, D_MODEL)] + br + [dmerged],
        [lw['b_merge_gate'].reshape(1, -1)], [(D_MODEL, BF16)] * 3 + [(GATE_WIDTH, BF16)], [(1, GATE_WIDTH)], tl=256,
        name="mix_merge_bwd")
    grads['b_merge_gate'] = dbias[0]
    dys = []
    for y, dbr, n in zip(ys, (d0, d1, d2), ('w_branch_s5', 'w_branch_gla', 'w_branch_attn')):
        grads[n] = _mm(y, dbr, ta=True, out_dtype=BF16, name="mix_dwbranch")
        dys.append(_mm(dbr, lw[n], tb=True, name="mix_dy"))
    du, g_s5 = _s5_bwd(pg, lw['s5'], lw['s5_w_glu'], s_s5, dys[0])
    dgq, dgk, dgv, dgg, dz, g_gla = _gla_branch_bwd(pg, lw['gla_w_alpha'], lw['gla_b_alpha'], lw['gla_norm'], s_gla, dys[1])
    daq, dak, dav, g_att, carried = _attn_branch_bwd(pg, lw['attn_q_norm'], lw['attn_k_norm'], s_att, dys[2], side)

    def assemble(dgp, u0, u1, u2, q0, q1, k0, k1, v0, v1, gg, aq, ak, av, z):
        pad = jnp.zeros((dgp.shape[0], IN_PAD - 3456), F32)
        parts = [dgp.astype(F32), u0 + u1 + u2, q0 + q1, k0 + k1, v0 + v1, gg, aq, ak, av, z, pad]
        return jnp.concatenate(parts, axis=1)
    dpg = _rowmap(assemble, [dgpre] + du + dgq + dgk + dgv + [dgg, daq, dak, dav, dz], [], [(PG_WIDTH, BF16)], tl=256,
                  name="mix_dpg")[0]
    grads['w_pg'] = _mm(h, dpg, ta=True, out_dtype=BF16, name="mix_dwpg")
    dh = _mm(dpg, lw['w_pg'], tb=True, name="mix_dh")
    dx, grads['mix_norm'] = _rmsnorm_bwd(x, lw['mix_norm'], dh, dx_out)
    grads['s5'], grads['gla'], grads['attn'] = g_s5, g_gla, g_att
    return dx, grads, carried


def _loss_head(x, gain, target):
    width = x.shape[1]

    def fn(xv, tv, g):
        r = _rms(xv)
        err = xv * r * g - tv
        dy = err * (1.0 / width)
        gd = dy * g
        dx = r * gd - xv * (r * r * r) * jnp.mean(xv * gd, axis=-1, keepdims=True)
        loss = jnp.sum(0.5 * jnp.mean(err * err, axis=-1, keepdims=True), axis=0, keepdims=True)
        return dx, jnp.broadcast_to(loss, (1, 128)), jnp.sum(dy * xv * r, axis=0, keepdims=True)
    dx, loss, dgain = _rowmap(fn, [x, target], [gain.reshape(1, -1)], [(width, F32)], [(1, 128), (1, width)], tl=256,
                              name="loss_head")
    return loss[0, 0], dx, dgain[0]


def _row_tile(rows, cap=256):
    for t in range(cap - cap % 16, 0, -16):
        if rows % t == 0:
            return t
    return rows


def _reduce_adamw(parts, w, m, v, *, name):
    r, c = w.shape
    if len(parts) > 1 and parts[0].shape[1] % 8:
        parts = [jnp.concatenate(parts, axis=1)]
    nparts, rows = parts[0].shape[0], parts[0].shape[1]
    tr = _row_tile(rows)
    per = rows // tr

    def body(*refs):
        p_refs, (w_ref, m_ref, v_ref, g_ref, d_ref, m2_ref, v2_ref) = refs[:len(parts)], refs[len(parts):]
        g = None
        for k, p_ref in enumerate(p_refs):
            gk = p_ref[0].astype(F32)
            for j in range(1, nparts):
                gk = gk + p_ref[j].astype(F32)
            g = gk if g is None else jnp.where(pl.program_id(0) // per == k, gk, g)
        m2 = ADAM_B1 * m_ref[...] + (1.0 - ADAM_B1) * g
        v2 = ADAM_B2 * v_ref[...] + (1.0 - ADAM_B2) * (g * g)
        m_hat = m2 / (1.0 - ADAM_B1 ** ADAM_STEP)
        v_hat = v2 / (1.0 - ADAM_B2 ** ADAM_STEP)
        g_ref[...] = g
        d_ref[...] = -ADAM_LR * (m_hat / (jnp.sqrt(v_hat) + ADAM_EPS) + ADAM_WD * w_ref[...])
        m2_ref[...] = m2
        v2_ref[...] = v2

    flat = pl.BlockSpec((tr, c), lambda i: (i, 0))
    p_specs = [pl.BlockSpec((nparts, tr, c), lambda i, k=k: (0, jnp.clip(i - k * per, 0, per - 1), 0)) for k in range(len(parts))]
    return pl.pallas_call(
        body, name=name, grid=(r // tr,), in_specs=p_specs + [flat, flat, flat],
        out_specs=[flat] * 4, out_shape=[jax.ShapeDtypeStruct((r, c), F32)] * 4,
        compiler_params=pltpu.CompilerParams(dimension_semantics=("parallel",)),
    )(*parts, w, m, v)


def _all_gather(blocks, *, name):
    side = _SideGather(blocks)

    def body(*refs):
        start, finish = side.hooks(refs)
        start()
        finish()

    return pl.pallas_call(body, name=name, out_shape=side.out_shape, in_specs=side.in_specs, out_specs=side.out_specs,
                          scratch_shapes=side.scratch)(*blocks)


class _SideGather:
    def __init__(self, blocks):
        self.blocks = list(blocks)
        self.n = n = len(self.blocks)
        hbm = pl.BlockSpec(memory_space=pl.ANY)
        self.in_specs, self.out_specs = [hbm] * n, [hbm] * n
        self.out_shape = [jax.ShapeDtypeStruct((N_DEV,) + b.shape, b.dtype) for b in self.blocks]
        self.scratch = [pltpu.SemaphoreType.DMA((n, 7)), pltpu.SemaphoreType.DMA((n, 7)), pltpu.SemaphoreType.DMA((n,))]

    def hooks(self, refs):
        n = self.n
        x_refs, out_refs = refs[:n], refs[n:2 * n]
        send_sems, recv_sems, local_sems = refs[2 * n:]
        x, y, c = lax.axis_index("x"), lax.axis_index("y"), lax.axis_index("c")
        me, sibling = (x, y, c), (x, y, 1 - c)
        chips = [(1 - x, y), (x, 1 - y), (1 - x, 1 - y)]

        def slot(t, px, py, pc):
            return out_refs[t].at[4 * px + 2 * py + pc]

        def copy(t, k, blk, to, own=False):
            return pltpu.make_async_remote_copy(
                src_ref=x_refs[t] if own else slot(t, *blk), dst_ref=slot(t, *blk), send_sem=send_sems.at[t, k],
                recv_sem=recv_sems.at[t, k], device_id=to, device_id_type=pl.DeviceIdType.MESH)

        def mine(t):
            return pltpu.make_async_copy(x_refs[t], slot(t, *me), local_sems.at[t])

        def first(t):
            return [copy(t, 0, me, sibling, own=True)] + [copy(t, 1 + j, me, (*chip, c), own=True) for j, chip in enumerate(chips)]

        def start():
            for t in range(n):
                mine(t).start()
            for t in range(n):
                for cp in first(t):
                    cp.start()

        def finish():
            passed = []
            for j, chip in enumerate(chips):
                for t in range(n):
                    copy(t, 1 + j, (*chip, c), me).wait_recv()
                    passed.append(copy(t, 4 + j, (*chip, c), sibling))
                    passed[-1].start()
            for t in range(n):
                copy(t, 0, sibling, me).wait_recv()
            for j, chip in enumerate(chips):
                for t in range(n):
                    copy(t, 4 + j, (*chip, 1 - c), me).wait_recv()
            for t in range(n):
                for cp in first(t):
                    cp.wait_send()
            for cp in passed:
                cp.wait_send()
            for t in range(n):
                mine(t).wait()

        return start, finish


def _first_last_step(grid):
    ids = [pl.program_id(a) for a in range(len(grid))]
    first = functools.reduce(lambda p, q: p & q, [i == 0 for i in ids])
    last = functools.reduce(lambda p, q: p & q, [i == n - 1 for i, n in zip(ids, grid)])
    return first, last


def _carried(side, grid, refs, n_in, n_out, n_scratch, compute):
    if side is None:
        compute(refs)
        return
    n = side.n
    main = refs[:n_in] + refs[n_in + n:n_in + n + n_out] + refs[n_in + 2 * n + n_out:n_in + 2 * n + n_out + n_scratch]
    side_refs = refs[n_in:n_in + n] + refs[n_in + n + n_out:n_in + 2 * n + n_out] + refs[n_in + 2 * n + n_out + n_scratch:]
    start, finish = side.hooks(side_refs)
    first, last = _first_last_step(grid)
    pl.when(first)(start)
    compute(main)
    pl.when(last)(finish)


N_CHIP = N_DEV // 2


def _swap_with_sibling(arrays, *, name):
    n = len(arrays)

    def body(*refs):
        src_refs, out_refs = refs[:n], refs[n:2 * n]
        send_sems, recv_sems = refs[2 * n:]
        sibling = (lax.axis_index("x"), lax.axis_index("y"), 1 - lax.axis_index("c"))
        copies = [pltpu.make_async_remote_copy(
            src_ref=src_refs[t], dst_ref=out_refs[t], send_sem=send_sems.at[t], recv_sem=recv_sems.at[t],
            device_id=sibling, device_id_type=pl.DeviceIdType.MESH) for t in range(n)]
        for cp in copies:
            cp.start()
        for cp in copies:
            cp.wait()

    hbm = pl.BlockSpec(memory_space=pl.ANY)
    return pl.pallas_call(
        body, name=name, out_shape=[jax.ShapeDtypeStruct(a.shape, a.dtype) for a in arrays],
        in_specs=[hbm] * n, out_specs=[hbm] * n,
        scratch_shapes=[pltpu.SemaphoreType.DMA((n,)), pltpu.SemaphoreType.DMA((n,))],
    )(*arrays)


def _exchange_chips(stacks, *, name):
    side = _SideChipExchange(stacks)

    def body(*refs):
        start, finish = side.hooks(refs)
        start()
        finish()

    return pl.pallas_call(body, name=name, out_shape=side.out_shape, in_specs=side.in_specs, out_specs=side.out_specs,
                          scratch_shapes=side.scratch)(*stacks)


class _SideChipExchange:
    def __init__(self, stacks):
        self.blocks = list(stacks)
        self.n = n = len(self.blocks)
        hbm = pl.BlockSpec(memory_space=pl.ANY)
        self.in_specs, self.out_specs = [hbm] * n, [hbm] * n
        self.out_shape = [jax.ShapeDtypeStruct(s.shape, s.dtype) for s in self.blocks]
        self.scratch = [pltpu.SemaphoreType.DMA((n, N_CHIP - 1)), pltpu.SemaphoreType.DMA((n, N_CHIP - 1)),
                        pltpu.SemaphoreType.DMA((n,))]

    def hooks(self, refs):
        n = self.n
        g_refs, out_refs = refs[:n], refs[n:2 * n]
        send_sems, recv_sems, local_sems = refs[2 * n:]
        x, y, c = lax.axis_index("x"), lax.axis_index("y"), lax.axis_index("c")
        me = 2 * x + y

        def copies():
            mine = [pltpu.make_async_copy(g_refs[t].at[me], out_refs[t].at[me], local_sems.at[t]) for t in range(n)]
            remote = []
            for k in range(1, N_CHIP):
                px, py = x ^ (k >> 1 & 1), y ^ (k & 1)
                for t in range(n):
                    remote.append(pltpu.make_async_remote_copy(
                        src_ref=g_refs[t].at[2 * px + py], dst_ref=out_refs[t].at[me], send_sem=send_sems.at[t, k - 1],
                        recv_sem=recv_sems.at[t, k - 1], device_id=(px, py, c), device_id_type=pl.DeviceIdType.MESH))
            return mine, remote

        def start():
            mine, remote = copies()
            for cp in mine + remote:
                cp.start()

        def finish():
            mine, remote = copies()
            for cp in remote:
                cp.wait_recv()
            for cp in remote:
                cp.wait_send()
            for cp in mine:
                cp.wait()

        return start, finish


def _pair_sum(a, b):
    return _rowmap(lambda u, v: u.astype(F32) + v.astype(F32), [a, b], [], [(a.shape[1], BF16)], tl=_row_tile(a.shape[0], 512),
                   name="pair_sum")[0]


SMALL_COLS = 128


def _pack_small(arrays):
    flat = jnp.concatenate([a.astype(F32).reshape(-1, SMALL_COLS) for a in arrays], axis=0)
    return jnp.pad(flat, ((0, -flat.shape[0] % 256), (0, 0)))


def _unpack_small(packed, shapes):
    out, off = [], 0
    for s in shapes:
        r = math.prod(s) // SMALL_COLS
        out.append(packed[off:off + r].reshape(s))
        off += r
    return out


def _split_shards(full, axis):
    shape = full.shape
    split = full.reshape(shape[:axis] + (N_DEV, shape[axis] // N_DEV) + shape[axis + 1:])
    return jnp.moveaxis(split, axis, 0)


def _join_shards(stack, axis):
    moved = jnp.moveaxis(stack, 0, axis)
    shape = moved.shape
    return moved.reshape(shape[:axis] + (shape[axis] * shape[axis + 1],) + shape[axis + 2:])


def _w_in_padded(w_in):
    pad = jnp.zeros(w_in.shape[:-1] + (IN_PAD - IN_WIDTH,), w_in.dtype)
    return jnp.concatenate([w_in[..., :2560], w_in[..., 2592:], w_in[..., 2560:2592], pad], axis=-1)


def _w_in_unpadded(w):
    return jnp.concatenate([w[..., :2560], w[..., 3328:3360], w[..., 2560:3328]], axis=-1)


FFN1_W = ('ffn1_w_gate', 'ffn1_w_up', 'ffn1_w_down')
FFN2_W = ('ffn2_w_gate', 'ffn2_w_up', 'ffn2_w_down')
MIX_IN_W = ('w_in', 'w_merge_gate')
MIX_REST_W = ('s5_w_glu', 'gla_w_alpha', 'gla_b_alpha', 'w_branch_s5', 'w_branch_gla', 'w_branch_attn', 'w_out')


def _mixer_weights(full, w, s5, i):
    lw = {n: w[n][i] for n in ('mix_norm', 'gla_norm', 'attn_q_norm', 'attn_k_norm', 'b_merge_gate')}
    lw['s5'] = {'b_mat': s5['b_mat'][i], 'c_mat': s5['c_mat'][i], 'tabs': [t[i] for t in s5['tabs']],
                'tabs_adj': [t[i] for t in s5['tabs_adj']], 'd': w['s5_d'][i]}
    w_in = full['w_in']
    pad = jnp.zeros((D_MODEL, IN_PAD - IN_WIDTH), w_in.dtype)
    lw['w_pg'] = jnp.concatenate([full['w_merge_gate'], w_in[:, :2560], w_in[:, 2592:], w_in[:, 2560:2592], pad], axis=1)
    return lw


def _mixer_weights_rest(full):
    lw = {n: full[n] for n in MIX_REST_W if n != 'gla_b_alpha'}
    lw['gla_b_alpha'] = full['gla_b_alpha'].astype(F32)
    return lw


def _chip_sums(grads):
    core = lax.axis_index("c")
    own, for_sibling = [], []
    for n in SHARDED:
        by_owner = _split_shards(grads[n], SHARD_AXIS[n] - 1).astype(BF16)
        by_owner = by_owner.reshape((N_CHIP, 2) + by_owner.shape[1:])
        own.append(lax.dynamic_index_in_dim(by_owner, core, axis=1, keepdims=False))
        for_sibling.append(lax.dynamic_index_in_dim(by_owner, 1 - core, axis=1, keepdims=False))
    from_sibling = _swap_with_sibling(for_sibling, name="exchange_grads_sibling")
    return [_pair_sum(a.reshape(-1, a.shape[-1]), b.reshape(-1, b.shape[-1])).reshape(a.shape)
            for a, b in zip(own, from_sibling)]


def _step_local(x, target, w, shards):
    s5 = _s5_setup(w)
    full = [{} for _ in range(DEPTH)]

    def wanted(i, *groups):
        return _SideGather([shards[n][i] for names in groups for n in names])

    def arrived(i, stacks, *groups):
        names = [n for group in groups for n in group]
        for n, st in zip(names, stacks):
            full[i][n] = _join_shards(st, SHARD_AXIS[n] - 1)

    arrived(0, _all_gather([shards[n][0] for n in FFN1_W], name="gather_first"), FFN1_W)
    saved, lws = [], []
    for i in range(DEPTH):
        f, first = full[i], i == 0
        x, s1, got = _ffn_fwd(x, w['ffn1_norm'][i], f['ffn1_w_gate'], f['ffn1_w_up'], f['ffn1_w_down'],
                              wanted(i, MIX_IN_W) if first else None)
        if first:
            arrived(i, got, MIX_IN_W)
        lw = _mixer_weights(f, w, s5, i)
        lws.append(lw)

        def after_in(got_in, i=i, lw=lw):
            arrived(i, got_in, MIX_REST_W, FFN2_W)
            lw.update(_mixer_weights_rest(full[i]))
        if not first:
            lw.update(_mixer_weights_rest(f))
        x, s2, got = _mixer_fwd(x, lw, wanted(i, MIX_REST_W, FFN2_W) if first else None, after_in,
                                wanted(i + 1, FFN1_W, MIX_IN_W, MIX_REST_W) if first else wanted(i, FFN2_W))
        if first:
            arrived(i + 1, got, FFN1_W, MIX_IN_W, MIX_REST_W)
        else:
            arrived(i, got, FFN2_W)
        x, s3, _ = _ffn_fwd(x, w['ffn2_norm'][i], f['ffn2_w_gate'], f['ffn2_w_up'], f['ffn2_w_down'])
        saved.append((s1, s2, s3))
    loss, dx, d_final = _loss_head(x, w['final_norm'], target)
    per_layer, incoming, pending = [None] * DEPTH, [None] * DEPTH, None
    for i in reversed(range(DEPTH)):
        f, lw, (s1, s2, s3), g = full[i], lws[i], saved[i], {}
        dx, g['ffn2_norm'], g['ffn2_w_gate'], g['ffn2_w_up'], g['ffn2_w_down'] = _ffn_bwd(
            s3, w['ffn2_norm'][i], f['ffn2_w_gate'], f['ffn2_w_up'], f['ffn2_w_down'], dx)
        dx, gm, carried = _mixer_bwd(s2, lw, dx, _SideChipExchange(pending[1]) if pending else None)
        if pending:
            incoming[pending[0]] = carried
        dx, g['ffn1_norm'], g['ffn1_w_gate'], g['ffn1_w_up'], g['ffn1_w_down'] = _ffn_bwd(
            s1, w['ffn1_norm'][i], f['ffn1_w_gate'], f['ffn1_w_up'], f['ffn1_w_down'], dx)
        g['w_merge_gate'] = gm['w_pg'][:, :GATE_WIDTH]
        g['w_in'] = _w_in_unpadded(gm['w_pg'][:, GATE_WIDTH:])
        for n in ('w_out', 'b_merge_gate', 'w_branch_s5', 'w_branch_gla', 'w_branch_attn', 'mix_norm'):
            g[n] = gm[n]
        g['s5_d'], g['s5_w_glu'], g['s5_raw'] = gm['s5']['d'], gm['s5']['w_glu'], gm['s5']['raw']
        g['gla_w_alpha'], g['gla_b_alpha'], g['gla_norm'] = gm['gla']['w_alpha'], gm['gla']['b_alpha'], gm['gla']['norm']
        g['attn_q_norm'], g['attn_k_norm'] = gm['attn']['q_norm'], gm['attn']['k_norm']
        per_layer[i] = g
        sums = _chip_sums(g)
        if i > 0:
            pending = (i, sums)
        else:
            incoming[i] = _exchange_chips(sums, name="exchange_grads_chips")
    stacked = _s5_param_grads(w, [g['s5_raw'] for g in per_layer])
    stacked['final_norm'] = d_final
    return loss, dx, per_layer, stacked, incoming


def kernel(x, ffn1_norm, ffn1_w_gate, ffn1_w_up, ffn1_w_down, mix_norm, w_in, s5_lambda_re, s5_lambda_im, s5_log_dt, s5_b_re, s5_b_im, s5_c_re, s5_c_im, s5_d, s5_w_glu, gla_w_alpha, gla_b_alpha, gla_norm, attn_q_norm, attn_k_norm, w_branch_s5, w_branch_gla, w_branch_attn, w_merge_gate, b_merge_gate, w_out, ffn2_norm, ffn2_w_gate, ffn2_w_up, ffn2_w_down, final_norm, loss_target, m_ffn1_norm, m_ffn1_w_gate, m_ffn1_w_up, m_ffn1_w_down, m_mix_norm, m_w_in, m_s5_lambda_re, m_s5_lambda_im, m_s5_log_dt, m_s5_b_re, m_s5_b_im, m_s5_c_re, m_s5_c_im, m_s5_d, m_s5_w_glu, m_gla_w_alpha, m_gla_b_alpha, m_gla_norm, m_attn_q_norm, m_attn_k_norm, m_w_branch_s5, m_w_branch_gla, m_w_branch_attn, m_w_merge_gate, m_b_merge_gate, m_w_out, m_ffn2_norm, m_ffn2_w_gate, m_ffn2_w_up, m_ffn2_w_down, m_final_norm, v_ffn1_norm, v_ffn1_w_gate, v_ffn1_w_up, v_ffn1_w_down, v_mix_norm, v_w_in, v_s5_lambda_re, v_s5_lambda_im, v_s5_log_dt, v_s5_b_re, v_s5_b_im, v_s5_c_re, v_s5_c_im, v_s5_d, v_s5_w_glu, v_gla_w_alpha, v_gla_b_alpha, v_gla_norm, v_attn_q_norm, v_attn_k_norm, v_w_branch_s5, v_w_branch_gla, v_w_branch_attn, v_w_merge_gate, v_b_merge_gate, v_w_out, v_ffn2_norm, v_ffn2_w_gate, v_ffn2_w_up, v_ffn2_w_down, v_final_norm):
    return _train_step(x, ffn1_norm, ffn1_w_gate, ffn1_w_up, ffn1_w_down, mix_norm, w_in, s5_lambda_re, s5_lambda_im, s5_log_dt, s5_b_re, s5_b_im, s5_c_re, s5_c_im, s5_d, s5_w_glu, gla_w_alpha, gla_b_alpha, gla_norm, attn_q_norm, attn_k_norm, w_branch_s5, w_branch_gla, w_branch_attn, w_merge_gate, b_merge_gate, w_out, ffn2_norm, ffn2_w_gate, ffn2_w_up, ffn2_w_down, final_norm, loss_target, m_ffn1_norm, m_ffn1_w_gate, m_ffn1_w_up, m_ffn1_w_down, m_mix_norm, m_w_in, m_s5_lambda_re, m_s5_lambda_im, m_s5_log_dt, m_s5_b_re, m_s5_b_im, m_s5_c_re, m_s5_c_im, m_s5_d, m_s5_w_glu, m_gla_w_alpha, m_gla_b_alpha, m_gla_norm, m_attn_q_norm, m_attn_k_norm, m_w_branch_s5, m_w_branch_gla, m_w_branch_attn, m_w_merge_gate, m_b_merge_gate, m_w_out, m_ffn2_norm, m_ffn2_w_gate, m_ffn2_w_up, m_ffn2_w_down, m_final_norm, v_ffn1_norm, v_ffn1_w_gate, v_ffn1_w_up, v_ffn1_w_down, v_mix_norm, v_w_in, v_s5_lambda_re, v_s5_lambda_im, v_s5_log_dt, v_s5_b_re, v_s5_b_im, v_s5_c_re, v_s5_c_im, v_s5_d, v_s5_w_glu, v_gla_w_alpha, v_gla_b_alpha, v_gla_norm, v_attn_q_norm, v_attn_k_norm, v_w_branch_s5, v_w_branch_gla, v_w_branch_attn, v_w_merge_gate, v_b_merge_gate, v_w_out, v_ffn2_norm, v_ffn2_w_gate, v_ffn2_w_up, v_ffn2_w_down, v_final_norm)


def _train_step(*args):
    nw = len(W_NAMES)
    x, target = args[0][0], args[1 + nw][0]
    w = dict(zip(W_NAMES, args[1:1 + nw]))
    m = dict(zip(W_NAMES, args[2 + nw:2 + 2 * nw]))
    v = dict(zip(W_NAMES, args[2 + 2 * nw:2 + 3 * nw]))

    loss, dx, per_layer, stacked, incoming = _step_local(x, target, w, {n: w[n].astype(BF16) for n in SHARDED})
    loss = lax.psum(loss, ("x", "y", "c"))

    out = {}
    kinds = ('grad', 'delta', 'new_m', 'new_v')
    for t, n in enumerate(SHARDED):
        shape = w[n].shape
        flat = lambda a: a.reshape(-1, shape[-1])
        parts = [incoming[i][t].reshape(N_CHIP, -1, shape[-1]) for i in range(DEPTH)]
        res = _reduce_adamw(parts, flat(w[n]), flat(m[n]), flat(v[n]), name="adamw_sharded")
        for kind, a in zip(kinds, res):
            out[kind + '_' + n] = a.reshape(shape)
    small = [stacked[n] if n in stacked else jnp.stack([g[n] for g in per_layer]) for n in REPLICATED]
    parts = _all_gather([_pack_small(small)], name="gather_small_grads")[0]
    res = _reduce_adamw([parts], *[_pack_small([d[n] for n in REPLICATED]) for d in (w, m, v)], name="adamw_replicated")
    for kind, packed in zip(kinds, res):
        for n, a in zip(REPLICATED, _unpack_small(packed, [w[n].shape for n in REPLICATED])):
            out[kind + '_' + n] = a
    return (loss, dx[None]) + tuple(out[kind + '_' + n] for kind in kinds for n in W_NAMES)
```

```python
import functools
import math

import jax
import jax.numpy as jnp
import numpy as np
from jax import lax
from jax.experimental import pallas as pl
from jax.experimental.pallas import tpu as pltpu

F32 = jnp.float32
BF16 = jnp.bfloat16

N_DEV = 8
D_MODEL = 1024
DEPTH = 2
GRID_W = 64
D_FF = 2816
NORM_EPS = 1e-6
S5_GROUPS = 32
S5_GROUP_CH = 16
S5_STATE = 64
S5_WIDTH = 512
S5_NSTATE = S5_GROUPS * S5_STATE
S5_LANE_BLOCK = 512
GLA_HEADS = 4
GLA_HEAD_DIM = 128
GLA_WIDTH = 512
GLA_LOWRANK = 16
GLA_TAU = 16.0
GLA_CHUNK = 64
ATTN_Q_HEADS = 8
ATTN_KV_HEADS = 2
ATTN_HEAD_DIM = 64
ATTN_WIDTH = 512
ATTN_KV_WIDTH = 128
ROPE_BASE = 10000.0
IN_SPLITS = (512, 512, 512, 512, 512, 16, 16, 512, 128, 128)
IN_WIDTH = sum(IN_SPLITS)
IN_PAD = 3584
GATE_WIDTH = 3 * D_MODEL
PG_WIDTH = GATE_WIDTH + IN_PAD
P_OFF = GATE_WIDTH
CB_U, CB_GQ, CB_GK, CB_GV, CB_GG, CB_AQ = (P_OFF // 512 + i for i in range(6))
CB_AK, CB_AV, CB_Z = (P_OFF + 3072) // 128, (P_OFF + 3200) // 128, (P_OFF + 3328) // 128
ADAM_LR = 0.001
ADAM_B1 = 0.9
ADAM_B2 = 0.999
ADAM_EPS = 1e-08
ADAM_WD = 0.01
ADAM_STEP = 10

W_NAMES = ['ffn1_norm', 'ffn1_w_gate', 'ffn1_w_up', 'ffn1_w_down', 'mix_norm', 'w_in', 's5_lambda_re', 's5_lambda_im',
           's5_log_dt', 's5_b_re', 's5_b_im', 's5_c_re', 's5_c_im', 's5_d', 's5_w_glu', 'gla_w_alpha', 'gla_b_alpha',
           'gla_norm', 'attn_q_norm', 'attn_k_norm', 'w_branch_s5', 'w_branch_gla', 'w_branch_attn', 'w_merge_gate',
           'b_merge_gate', 'w_out', 'ffn2_norm', 'ffn2_w_gate', 'ffn2_w_up', 'ffn2_w_down', 'final_norm']
SHARD_AXIS = {'ffn1_w_gate': 2, 'ffn1_w_up': 2, 'ffn1_w_down': 1, 'w_in': 2, 's5_w_glu': 1, 'gla_w_alpha': 3,
              'gla_b_alpha': 2, 'w_branch_s5': 2, 'w_branch_gla': 2, 'w_branch_attn': 2, 'w_merge_gate': 2,
              'w_out': 1, 'ffn2_w_gate': 2, 'ffn2_w_up': 2, 'ffn2_w_down': 1}
SHARDED = [n for n in W_NAMES if n in SHARD_AXIS]
REPLICATED = [n for n in W_NAMES if n not in SHARD_AXIS]


def _pick(dim, prefs):
    for p in prefs:
        if dim % p == 0:
            return p
    return dim


def _sigmoid(x):
    return 0.5 * jnp.tanh(0.5 * x) + 0.5


def _mm(a, b, *, ta=False, tb=False, out_dtype=F32, scale=None, add=None, side=None, name):
    a, a_cb, a_w = a if isinstance(a, tuple) else (a, 0, a.shape[1])
    b, b_cb, b_w = b if isinstance(b, tuple) else (b, 0, b.shape[1])
    m, k = (a_w, a.shape[0]) if ta else (a.shape[0], a_w)
    n = b.shape[0] if tb else b_w
    assert (b_w if tb else b.shape[0]) == k, (a.shape, b.shape, ta, tb)
    tm, tn, tk = _mm_tiles(m, n, k, a.dtype.itemsize, b.dtype.itemsize, jnp.dtype(out_dtype).itemsize)
    nk = k // tk
    dims = (((0 if ta else 1,), (1 if tb else 0,)), ((), ()))
    a_off = a_cb * (a_w // (tm if ta else tk))
    b_off = b_cb * (b_w // (tk if tb else tn))

    grid = (m // tm, n // tn, nk)
    n_in = 2 if add is None else 3

    def body(*refs):
        _carried(side, grid, refs, n_in, 1, int(nk > 1), compute)

    def compute(refs):
        a_ref, b_ref, *rest = refs
        add_ref = rest[0] if add is not None else None
        o_ref, *acc = rest[1:] if add is not None else rest

        def finish(res):
            res = res if scale is None else res * scale
            return (res if add_ref is None else res + add_ref[...]).astype(out_dtype)

        part = lax.dot_general(a_ref[...].astype(BF16), b_ref[...].astype(BF16), dims, preferred_element_type=F32)
        if nk == 1:
            o_ref[...] = finish(part)
            return
        acc_ref, = acc
        kk = pl.program_id(2)

        @pl.when(kk == 0)
        def _():
            acc_ref[...] = part

        @pl.when(kk > 0)
        def _():
            acc_ref[...] += part

        @pl.when(kk == nk - 1)
        def _():
            o_ref[...] = finish(acc_ref[...])

    a_spec = (pl.BlockSpec((tk, tm), lambda i, j, kk: (kk, i + a_off)) if ta
              else pl.BlockSpec((tm, tk), lambda i, j, kk: (i, kk + a_off)))
    b_spec = (pl.BlockSpec((tn, tk), lambda i, j, kk: (j, kk + b_off)) if tb
              else pl.BlockSpec((tk, tn), lambda i, j, kk: (kk, j + b_off)))
    o_spec = pl.BlockSpec((tm, tn), lambda i, j, kk: (i, j))
    (out,), gathered = _side_call(
        body, side, name=name, grid=grid, in_specs=[a_spec, b_spec] + ([o_spec] if add is not None else []),
        out_specs=[o_spec], out_shape=[jax.ShapeDtypeStruct((m, n), out_dtype)],
        scratch=[pltpu.VMEM((tm, tn), F32)] if nk > 1 else [], args=[a, b] + ([add] if add is not None else []),
        semantics=("parallel", "parallel", "arbitrary"))
    return out if side is None else (out, gathered)


MM_VMEM_BUDGET = 40 * 1024 * 1024


def _mm_tiles(m, n, k, a_bytes, b_bytes, out_bytes):
    tms = [t for t in (1024, 1408, 512, 256, 128) if m % t == 0] or [m]
    tns = [t for t in (512, 1408, 256, 128) if n % t == 0] or [n]
    tks = [k] + [t for t in (2048, 1024, 512, 256, 128) if k % t == 0 and t < k]
    for tk in tks:
        for tm in tms:
            for tn in tns:
                use = 2 * (tm * tk * a_bytes + tk * tn * b_bytes + tm * tn * out_bytes) + 2 * tm * tn * 4
                if use <= MM_VMEM_BUDGET:
                    return tm, tn, tk
    return tms[-1], tns[-1], tks[-1]


def _rowmap(fn, rows, consts, outs, reds=(), *, tl, name):
    rows = [r if isinstance(r, tuple) else (r, 0, r.shape[1]) for r in rows]
    length = rows[0][0].shape[0]
    tl = min(tl, length)
    nr, nc, no = len(rows), len(consts), len(outs)

    def body(*refs):
        res = fn(*[r[...] for r in refs[:nr + nc]])
        res = res if isinstance(res, tuple) else (res,)
        for o_ref, val in zip(refs[nr + nc:nr + nc + no], res[:no]):
            o_ref[...] = val.astype(o_ref.dtype)
        if reds:
            step = pl.program_id(0)
            red_refs = refs[nr + nc + no:]

            @pl.when(step == 0)
            def _():
                for d_ref, val in zip(red_refs, res[no:]):
                    d_ref[...] = val.astype(F32)

            @pl.when(step > 0)
            def _():
                for d_ref, val in zip(red_refs, res[no:]):
                    d_ref[...] += val.astype(F32)

    in_specs = [pl.BlockSpec((tl, w), lambda i, cb=cb: (i, cb)) for (_, cb, w) in rows]
    in_specs += [pl.BlockSpec(c.shape, lambda i, nd=c.ndim: (0,) * nd) for c in consts]
    out_specs = [pl.BlockSpec((tl, w), lambda i: (i, 0)) for (w, _) in outs]
    out_specs += [pl.BlockSpec(s, lambda i, nd=len(s): (0,) * nd) for s in reds]
    out_shape = [jax.ShapeDtypeStruct((length, w), dt) for (w, dt) in outs]
    out_shape += [jax.ShapeDtypeStruct(s, F32) for s in reds]
    res = pl.pallas_call(
        body, name=name, grid=(length // tl,), in_specs=in_specs, out_specs=out_specs, out_shape=out_shape,
        compiler_params=pltpu.CompilerParams(dimension_semantics=("arbitrary" if reds else "parallel",)),
    )(*[r[0] for r in rows], *consts)
    return res


def _rms(x):
    return lax.rsqrt(jnp.mean(x * x, axis=-1, keepdims=True) + NORM_EPS)


def _rmsnorm_fwd(x, gain):
    def fn(xv, g):
        return xv * _rms(xv) * g
    return _rowmap(fn, [x], [gain.reshape(1, -1)], [(x.shape[1], BF16)], tl=256, name="rmsnorm_fwd")[0]


def _rmsnorm_bwd(x, gain, dh, dres):
    def fn(xv, dhv, drv, g):
        r = _rms(xv)
        gd = dhv * g
        dx = r * gd - xv * (r * r * r) * jnp.mean(xv * gd, axis=-1, keepdims=True)
        return drv + dx, jnp.sum(dhv * xv * r, axis=0, keepdims=True)
    dx, dg = _rowmap(fn, [x, dh, dres], [gain.reshape(1, -1)], [(x.shape[1], F32)], [(1, x.shape[1])], tl=256,
                     name="rmsnorm_bwd")
    return dx, dg[0]


FFN_UNIT = D_FF // 2


def _row_halves(rows):
    return (slice(0, rows // 2), slice(rows // 2, rows))


def _side_call(body, side, *, name, grid, in_specs, out_specs, out_shape, scratch, args, semantics):
    if side is not None:
        in_specs, out_specs = in_specs + side.in_specs, out_specs + side.out_specs
        out_shape, scratch, args = out_shape + side.out_shape, scratch + side.scratch, list(args) + side.blocks
        semantics = ("arbitrary",) * len(grid)
    res = pl.pallas_call(body, name=name, grid=grid, in_specs=in_specs, out_specs=out_specs, out_shape=out_shape,
                         scratch_shapes=scratch, compiler_params=pltpu.CompilerParams(dimension_semantics=semantics))(*args)
    n_own = len(res) - (side.n if side is not None else 0)
    return res[:n_own], res[n_own:]


def _ffn_up(h, w_gate, w_up, side=None):
    length, k = h.shape
    tm = _pick(length, (512, 256, 128))
    grid = (D_FF // FFN_UNIT, length // tm)

    def compute(refs):
        h_ref, wg_ref, wu_ref, a_ref, g_ref, u_ref = refs
        for rows in _row_halves(tm):
            hv = h_ref[rows, :]
            g = jnp.dot(hv, wg_ref[...], preferred_element_type=F32)
            u = jnp.dot(hv, wu_ref[...], preferred_element_type=F32)
            a_ref[rows, :] = (g * _sigmoid(g) * u).astype(BF16)
            g_ref[rows, :] = g.astype(BF16)
            u_ref[rows, :] = u.astype(BF16)

    def body(*refs):
        _carried(side, grid, refs, 3, 3, 0, compute)

    w_spec = pl.BlockSpec((k, FFN_UNIT), lambda j, i: (0, j))
    o_spec = pl.BlockSpec((tm, FFN_UNIT), lambda j, i: (i, j))
    return _side_call(
        body, side, name="ffn_up", grid=grid, in_specs=[pl.BlockSpec((tm, k), lambda j, i: (i, 0)), w_spec, w_spec],
        out_specs=[o_spec] * 3, out_shape=[jax.ShapeDtypeStruct((length, D_FF), BF16)] * 3, scratch=[],
        args=[h, w_gate, w_up], semantics=("parallel", "parallel"))


def _ffn_dgu(dxo, w_down, g, u):
    length, k = dxo.shape
    tm = _pick(length, (512, 256, 128))

    def body(d_ref, w_ref, g_ref, u_ref, dg_ref, du_ref):
        for rows in _row_halves(tm):
            da = 0.5 * lax.dot_general(d_ref[rows, :], w_ref[...], _NT, preferred_element_type=F32)
            gv = g_ref[rows, :].astype(F32)
            s = _sigmoid(gv)
            dg_ref[rows, :] = (da * u_ref[rows, :].astype(F32) * (s * (1.0 + gv * (1.0 - s)))).astype(BF16)
            du_ref[rows, :] = (da * (gv * s)).astype(BF16)

    o_spec = pl.BlockSpec((tm, FFN_UNIT), lambda j, i: (i, j))
    return pl.pallas_call(
        body, name="ffn_dgu", grid=(D_FF // FFN_UNIT, length // tm),
        in_specs=[pl.BlockSpec((tm, k), lambda j, i: (i, 0)), pl.BlockSpec((FFN_UNIT, k), lambda j, i: (j, 0)), o_spec, o_spec],
        out_specs=[o_spec] * 2, out_shape=[jax.ShapeDtypeStruct((length, D_FF), BF16)] * 2,
        compiler_params=pltpu.CompilerParams(dimension_semantics=("parallel", "parallel")),
    )(dxo, w_down, g, u)


def _ffn_fwd(x, gain, w_gate, w_up, w_down, side=None):
    h = _rmsnorm_fwd(x, gain)
    (a, g, u), gathered = _ffn_up(h, w_gate, w_up, side)
    x_out = _mm(a, w_down, scale=0.5, add=x, name="ffn_down")
    return x_out, (x, h, g, u, a), gathered


def _ffn_bwd(saved, gain, w_gate, w_up, w_down, dx_out):
    x, h, g, u, a = saved
    dxo = dx_out.astype(BF16)
    d_wdown = _mm(a, dxo, ta=True, scale=0.5, out_dtype=BF16, name="ffn_dwdown")
    dg, du = _ffn_dgu(dxo, w_down, g, u)
    d_wgate = _mm(h, dg, ta=True, out_dtype=BF16, name="ffn_dwgu")
    d_wup = _mm(h, du, ta=True, out_dtype=BF16, name="ffn_dwgu")
    dh = _mm(du, w_up, tb=True, add=_mm(dg, w_gate, tb=True, name="ffn_dh"), name="ffn_dh_add")
    dx, dgain = _rmsnorm_bwd(x, gain, dh, dx_out)
    return dx, dgain, d_wgate, d_wup, d_wdown


def _s5_blocked(re, im):
    lead = re.shape[:-1]
    nb = S5_NSTATE // S5_LANE_BLOCK
    both = jnp.stack([re.reshape(*lead, nb, S5_LANE_BLOCK), im.reshape(*lead, nb, S5_LANE_BLOCK)], axis=-2)
    return both.reshape(*lead, 2 * S5_NSTATE)


def _s5_unblocked(z):
    lead = z.shape[:-1]
    nb = S5_NSTATE // S5_LANE_BLOCK
    both = z.reshape(*lead, nb, 2, S5_LANE_BLOCK)
    return both[..., 0, :].reshape(*lead, S5_NSTATE), both[..., 1, :].reshape(*lead, S5_NSTATE)


def _s5_tables(a_re, a_im, reverse):
    a = lax.complex(a_re, a_im)
    a2 = a * a
    a4 = a2 * a2
    rows = jnp.arange(8)
    pw = [a]
    for _ in range(7):
        pw.append(pw[-1] * a)
    pw = jnp.stack(pw)
    if reverse:
        pw = pw[::-1]
    tabs = []
    for coef, s in ((a, 1), (a2, 2), (a4, 4)):
        live = (rows <= 7 - s) if reverse else (rows >= s)
        tabs.append(jnp.where(live[:, None], coef[None, :], 0.0))
    tabs.append(pw)
    tabs = jnp.stack(tabs)
    return _s5_blocked(jnp.real(tabs), jnp.imag(tabs))


def _s5_scan_tile(v, tab_ref, prev, reverse):
    lb = S5_LANE_BLOCK
    vr, vi = v[:, :lb], v[:, lb:]
    for idx, s in enumerate((1, 2, 4)):
        cr, ci = tab_ref[idx, :, :lb], tab_ref[idx, :, lb:]
        sh = 8 - s if reverse else s
        sr, si = pltpu.roll(vr, sh, 0), pltpu.roll(vi, sh, 0)
        vr, vi = vr + cr * sr - ci * si, vi + cr * si + ci * sr
    row = 0 if reverse else 7
    pr = jnp.broadcast_to(prev[row:row + 1, :lb], (8, lb))
    pi = jnp.broadcast_to(prev[row:row + 1, lb:], (8, lb))
    cr, ci = tab_ref[3, :, :lb], tab_ref[3, :, lb:]
    return jnp.concatenate([vr + cr * pr - ci * pi, vi + cr * pi + ci * pr], axis=1)


def _s5_prep(lam_re, lam_im, log_dt, b_re, b_im):
    lam = lax.complex(lam_re, lam_im)
    dt = jnp.exp(log_dt)[:, None]
    lam_bar = jnp.exp(lam * dt)
    b_bar = ((lam_bar - 1.0) / lam)[..., None] * lax.complex(b_re, b_im)
    return (jnp.real(lam_bar).reshape(-1), jnp.imag(lam_bar).reshape(-1), jnp.real(b_bar), jnp.imag(b_bar))


S5_NBLK = S5_NSTATE // S5_LANE_BLOCK
S5_BLK_GROUPS = S5_GROUPS // S5_NBLK
S5_BLK_CH = S5_BLK_GROUPS * S5_GROUP_CH


def _s5_in_matrix(bb_re, bb_im):
    eye = jnp.eye(S5_BLK_GROUPS, dtype=F32)
    def dense(bb):
        b4 = bb.reshape(S5_NBLK, S5_BLK_GROUPS, S5_STATE, S5_GROUP_CH)
        return jnp.einsum('cgph,gk->cghkp', b4, eye).reshape(S5_NBLK, S5_BLK_CH, S5_LANE_BLOCK)
    return jnp.concatenate([dense(bb_re), dense(bb_im)], axis=-1)


def _s5_block_diagonal(d):
    d5 = d.reshape(S5_NBLK, S5_BLK_GROUPS, S5_GROUP_CH, S5_BLK_GROUPS, S5_STATE)
    eye = jnp.eye(S5_BLK_GROUPS, dtype=F32)
    return jnp.swapaxes(jnp.sum(d5 * eye[None, :, None, :, None], axis=1), 1, 2)


def _s5_in_matrix_grad(d_mat):
    def diag(d):
        return jnp.swapaxes(_s5_block_diagonal(d), 2, 3).reshape(S5_GROUPS, S5_STATE, S5_GROUP_CH)
    return diag(d_mat[..., :S5_LANE_BLOCK]), diag(d_mat[..., S5_LANE_BLOCK:])


def _s5_out_matrix(c_re, c_im):
    eye = jnp.eye(S5_BLK_GROUPS, dtype=F32)
    def dense(cc):
        c4 = cc.reshape(S5_NBLK, S5_BLK_GROUPS, S5_GROUP_CH, S5_STATE)
        return jnp.einsum('cghp,gk->cgpkh', c4, eye).reshape(S5_NBLK, S5_LANE_BLOCK, S5_BLK_CH)
    return jnp.concatenate([dense(c_re), dense(-c_im)], axis=1)


def _s5_out_matrix_grad(d_mat_t):
    def diag(d):
        return _s5_block_diagonal(d).reshape(S5_GROUPS, S5_GROUP_CH, S5_STATE)
    return diag(d_mat_t[..., :S5_LANE_BLOCK]), -diag(d_mat_t[..., S5_LANE_BLOCK:])


def _gelu_parts(x):
    k = math.sqrt(2.0 / math.pi)
    inner = k * (x + 0.044715 * x * x * x)
    th = jnp.tanh(inner)
    return th, k * (1.0 + 3.0 * 0.044715 * x * x)


S5_CB_U = CB_U * (512 // S5_BLK_CH)


def _s5_direction_fwd(pg, b_mat, c_mat, tabs, reverse, *, name):
    length = pg.shape[0]
    tb = min(512, length)
    ntb = length // tb
    wb = 2 * S5_LANE_BLOCK
    ntile = tb // 8

    def body(tab_ref, u_ref, b_ref, c_ref, x_ref, y_ref, ends_ref, carry_ref, bu_ref):
        @pl.when(pl.program_id(1) == 0)
        def _():
            carry_ref[...] = jnp.zeros_like(carry_ref)

        bu_ref[...] = jnp.dot(u_ref[...].astype(BF16), b_ref[0], preferred_element_type=F32)

        def step(i, prev):
            r0 = pl.multiple_of((ntile - 1 - i if reverse else i) * 8, 8)
            x = _s5_scan_tile(bu_ref[pl.ds(r0, 8), :], tab_ref, prev, reverse)
            x_ref[pl.ds(r0, 8), :] = x
            return x

        carry_ref[...] = lax.fori_loop(0, ntile, step, carry_ref[...])
        y_ref[...] = jnp.dot(x_ref[...].astype(BF16), c_ref[0], preferred_element_type=F32)
        ends_ref[0, 0:8, :] = x_ref[0:8, :]
        ends_ref[0, 8:16, :] = x_ref[tb - 8:tb, :]

    tix = (lambda t: ntb - 1 - t) if reverse else (lambda t: t)
    return pl.pallas_call(
        body, name=name, grid=(S5_NBLK, ntb),
        in_specs=[pl.BlockSpec((4, 8, wb), lambda c, t: (0, 0, c)),
                  pl.BlockSpec((tb, S5_BLK_CH), lambda c, t: (tix(t), S5_CB_U + c)),
                  pl.BlockSpec((1, S5_BLK_CH, wb), lambda c, t: (c, 0, 0)),
                  pl.BlockSpec((1, wb, S5_BLK_CH), lambda c, t: (c, 0, 0))],
        out_specs=[pl.BlockSpec((tb, wb), lambda c, t: (tix(t), c)), pl.BlockSpec((tb, S5_BLK_CH), lambda c, t: (tix(t), c)),
                   pl.BlockSpec((1, 16, wb), lambda c, t: (tix(t), 0, c))],
        out_shape=[jax.ShapeDtypeStruct((length, S5_NBLK * wb), F32), jax.ShapeDtypeStruct((length, S5_WIDTH), F32),
                   jax.ShapeDtypeStruct((ntb, 16, S5_NBLK * wb), F32)],
        scratch_shapes=[pltpu.VMEM((8, wb), F32), pltpu.VMEM((tb, wb), F32)],
        compiler_params=pltpu.CompilerParams(dimension_semantics=("parallel", "arbitrary")),
    )(tabs, pg, b_mat, c_mat)


def _s5_direction_bwd(pg, dy, xs, ends, b_mat, c_mat, tabs_conj, reverse, *, name):
    length = pg.shape[0]
    tb = min(512, length)
    ntb = length // tb
    lb = S5_LANE_BLOCK
    wb = 2 * lb
    ntile = tb // 8
    adj_rev = not reverse
    if reverse:
        edge = jnp.concatenate([ends[1:, 0], jnp.zeros((1, xs.shape[1]), F32)], axis=0)
    else:
        edge = jnp.concatenate([jnp.zeros((1, xs.shape[1]), F32), ends[:-1, 15]], axis=0)
    edge = edge.reshape(ntb, 1, xs.shape[1])

    def body(tab_ref, u_ref, dy_ref, x_ref, edge_ref, b_ref, c_ref, du_ref, db_ref, dc_ref, da_ref, carry_ref, g_ref, lam_ref):
        @pl.when(pl.program_id(1) == 0)
        def _():
            carry_ref[...] = jnp.zeros_like(carry_ref)
            da_ref[...] = jnp.zeros_like(da_ref)
            db_ref[...] = jnp.zeros_like(db_ref)
            dc_ref[...] = jnp.zeros_like(dc_ref)

        dyb = dy_ref[...].astype(BF16)
        g_ref[...] = lax.dot_general(dyb, c_ref[0], _NT, preferred_element_type=F32)
        rows = lax.broadcasted_iota(jnp.int32, (8, wb), 0)

        def step(i, carry):
            prev, acc = carry
            k = ntile - 1 - i if adj_rev else i
            r0 = pl.multiple_of(k * 8, 8)
            lam = _s5_scan_tile(g_ref[pl.ds(r0, 8), :], tab_ref, prev, adj_rev)
            lam_ref[pl.ds(r0, 8), :] = lam
            x = x_ref[pl.ds(r0, 8), :]
            if reverse:
                kn = jnp.minimum(k + 1, ntile - 1)
                nb = x_ref[pl.ds(pl.multiple_of(kn * 8, 8), 8), :][0:1, :]
                nb = jnp.where(k == ntile - 1, edge_ref[0], nb)
                xp = jnp.where(rows == 7, jnp.broadcast_to(nb, (8, wb)), pltpu.roll(x, 7, 0))
            else:
                kn = jnp.maximum(k - 1, 0)
                nb = x_ref[pl.ds(pl.multiple_of(kn * 8, 8), 8), :][7:8, :]
                nb = jnp.where(k == 0, edge_ref[0], nb)
                xp = jnp.where(rows == 0, jnp.broadcast_to(nb, (8, wb)), pltpu.roll(x, 1, 0))
            xr, xi, lr, li = xp[:, :lb], xp[:, lb:], lam[:, :lb], lam[:, lb:]
            return lam, acc + jnp.concatenate([xr * lr + xi * li, xr * li - xi * lr], axis=1)

        last, acc = lax.fori_loop(0, ntile, step, (carry_ref[...], da_ref[...]))
        carry_ref[...] = last
        da_ref[...] = acc
        lamb = lam_ref[...].astype(BF16)
        du_ref[...] = lax.dot_general(lamb, b_ref[0], _NT, preferred_element_type=F32)
        db_ref[0] += lax.dot_general(u_ref[...].astype(BF16), lamb, _TN, preferred_element_type=F32)
        dc_ref[0] += lax.dot_general(dyb, x_ref[...].astype(BF16), _TN, preferred_element_type=F32)

    tix = (lambda t: ntb - 1 - t) if adj_rev else (lambda t: t)
    wide = pl.BlockSpec((tb, wb), lambda c, t: (tix(t), c))
    mat = pl.BlockSpec((1, S5_BLK_CH, wb), lambda c, t: (c, 0, 0))
    return pl.pallas_call(
        body, name=name, grid=(S5_NBLK, ntb),
        in_specs=[pl.BlockSpec((4, 8, wb), lambda c, t: (0, 0, c)),
                  pl.BlockSpec((tb, S5_BLK_CH), lambda c, t: (tix(t), S5_CB_U + c)),
                  pl.BlockSpec((tb, S5_BLK_CH), lambda c, t: (tix(t), c)), wide,
                  pl.BlockSpec((1, 1, wb), lambda c, t: (tix(t), 0, c)), mat,
                  pl.BlockSpec((1, wb, S5_BLK_CH), lambda c, t: (c, 0, 0))],
        out_specs=[pl.BlockSpec((tb, S5_BLK_CH), lambda c, t: (tix(t), c)), mat, mat, pl.BlockSpec((8, wb), lambda c, t: (0, c))],
        out_shape=[jax.ShapeDtypeStruct((length, S5_WIDTH), F32), jax.ShapeDtypeStruct((S5_NBLK, S5_BLK_CH, wb), F32),
                   jax.ShapeDtypeStruct((S5_NBLK, S5_BLK_CH, wb), F32), jax.ShapeDtypeStruct((8, S5_NBLK * wb), F32)],
        scratch_shapes=[pltpu.VMEM((8, wb), F32), pltpu.VMEM((tb, wb), F32), pltpu.VMEM((tb, wb), F32)],
        compiler_params=pltpu.CompilerParams(dimension_semantics=("parallel", "arbitrary")),
    )(tabs_conj, pg, dy, xs, edge, b_mat, c_mat)


def _both(fn):
    return jax.vmap(jax.vmap(fn))


def _s5_setup(w):
    a_re, a_im, bb_re, bb_im = _both(_s5_prep)(w['s5_lambda_re'], w['s5_lambda_im'], w['s5_log_dt'], w['s5_b_re'], w['s5_b_im'])

    def tables(d, conj, reverse):
        return jax.vmap(lambda r, i: _s5_tables(r, -i if conj else i, reverse))(a_re[:, d], a_im[:, d])
    return {'b_mat': _both(_s5_in_matrix)(bb_re, bb_im).astype(BF16),
            'c_mat': _both(_s5_out_matrix)(w['s5_c_re'], w['s5_c_im']).astype(BF16),
            'tabs': [tables(0, False, False), tables(1, False, True)],
            'tabs_adj': [tables(0, True, True), tables(1, True, False)]}


def _s5_param_grads(w, raws):
    def stacked(k):
        return jnp.stack([jnp.stack([raws[i][d][k] for d in range(2)]) for i in range(DEPTH)])
    dbb_re, dbb_im = _both(_s5_in_matrix_grad)(stacked(0))
    dc_re, dc_im = _both(_s5_out_matrix_grad)(stacked(1))
    da_re, da_im = _s5_unblocked(jnp.sum(stacked(2), axis=2))
    _, vjp = jax.vjp(_both(_s5_prep), w['s5_lambda_re'], w['s5_lambda_im'], w['s5_log_dt'], w['s5_b_re'], w['s5_b_im'])
    g = vjp((da_re, da_im, dbb_re, dbb_im))
    return {'s5_lambda_re': g[0], 's5_lambda_im': g[1], 's5_log_dt': g[2], 's5_b_re': g[3], 's5_b_im': g[4],
            's5_c_re': dc_re, 's5_c_im': dc_im}


def _s5_fwd(p_in, prm, w_glu):
    dirs = []
    ys = []
    for d, reverse in ((0, False), (1, True)):
        xs, y_dir, ends = _s5_direction_fwd(p_in, prm['b_mat'][d], prm['c_mat'][d], prm['tabs'][d], reverse,
                                            name="s5_fwd_rev" if reverse else "s5_fwd")
        ys.append(y_dir)
        dirs.append((xs, ends))

    def post(yf, yb, u, dskip):
        ypre = yf + yb + dskip * u
        th, _ = _gelu_parts(ypre)
        return ypre, 0.5 * ypre * (1.0 + th)
    ypre, yg = _rowmap(post, [ys[0], ys[1], (p_in, CB_U, S5_WIDTH)], [prm['d'].reshape(1, -1)],
                       [(S5_WIDTH, F32), (S5_WIDTH, F32)], tl=512, name="s5_post")
    t = _mm(yg, w_glu, name="s5_glu_mm")

    def glu(ygv, tv):
        return ygv * _sigmoid(tv)
    y = _rowmap(glu, [yg, t], [], [(S5_WIDTH, BF16)], tl=512, name="s5_glu")[0]
    return y, (dirs, ypre, yg, t)


def _s5_bwd(pg, prm, w_glu, saved, dy):
    dirs, ypre, yg, t = saved

    def glu_bwd(dyv, ygv, tv):
        s = _sigmoid(tv)
        return dyv * ygv * s * (1.0 - s), dyv * s
    dt, dyg_direct = _rowmap(glu_bwd, [dy, yg, t], [], [(S5_WIDTH, BF16), (S5_WIDTH, F32)], tl=512, name="s5_glu_bwd")
    grads = {'w_glu': _mm(yg, dt, ta=True, out_dtype=BF16, name="s5_dwglu")}
    dyg_mm = _mm(dt, w_glu, tb=True, name="s5_dyg")

    def post_bwd(dyd, dym, yp, u, dskip):
        th, dinner = _gelu_parts(yp)
        dyp = (dyd + dym) * (0.5 * (1.0 + th) + 0.5 * yp * (1.0 - th * th) * dinner)
        return dyp, dyp * dskip, jnp.sum(dyp * u, axis=0, keepdims=True)
    dyp, du_skip, dd = _rowmap(post_bwd, [dyg_direct, dyg_mm, ypre, (pg, CB_U, S5_WIDTH)], [prm['d'].reshape(1, -1)],
                               [(S5_WIDTH, F32), (S5_WIDTH, F32)], [(1, S5_WIDTH)], tl=512, name="s5_post_bwd")
    grads['d'] = dd[0]
    du = [du_skip]
    grads['raw'] = []
    for d, reverse in ((0, False), (1, True)):
        du_dir, d_bmat, d_cmat_t, da = _s5_direction_bwd(pg, dyp, *dirs[d], prm['b_mat'][d], prm['c_mat'][d], prm['tabs_adj'][d],
                                                         reverse, name="s5_bwd_rev" if reverse else "s5_bwd")
        du.append(du_dir)
        grads['raw'].append((d_bmat, d_cmat_t, da))
    return du, grads


def _split3(x):
    hi = x.astype(BF16)
    r = x - hi.astype(F32)
    mid = r.astype(BF16)
    return hi, mid, (r - mid.astype(F32)).astype(BF16)


def _exact_dot(ones, x, dims):
    parts = [lax.dot_general(ones, p, dims, preferred_element_type=F32) for p in _split3(x)]
    return parts[0] + parts[1] + parts[2]


_NN = (((1,), (0,)), ((), ()))
_NT = (((1,), (1,)), ((), ()))
_TN = (((0,), (0,)), ((), ()))


def _dot(a, b, dims=_NN):
    return lax.dot_general(a.astype(BF16), b.astype(BF16), dims, preferred_element_type=F32)


def _gla_chunk_mask(reverse):
    rows = lax.broadcasted_iota(jnp.int32, (GLA_CHUNK, GLA_CHUNK), 0)
    cols = lax.broadcasted_iota(jnp.int32, (GLA_CHUNK, GLA_CHUNK), 1)
    return (cols >= rows) if reverse else (cols <= rows)


def _gla_fwd(pg, la, reverse, *, name):
    length = la.shape[0]
    nch = length // GLA_CHUNK
    scale = GLA_HEAD_DIM ** -0.5
    last = 0 if reverse else GLA_CHUNK - 1
    hd = GLA_HEAD_DIM

    def body(q_ref, k_ref, v_ref, la_ref, o_ref, sp_ref, st_ref):
        @pl.when(pl.program_id(0) == 0)
        def _():
            st_ref[...] = jnp.zeros_like(st_ref)

        mask = _gla_chunk_mask(reverse)
        b = _exact_dot(mask.astype(BF16), la_ref[...], _NN)
        sp_ref[0] = st_ref[...]
        outs = []
        for h in range(GLA_HEADS):
            sl = slice(h * hd, (h + 1) * hd)
            bh = b[:, sl]
            bl = bh[last:last + 1, :]
            k = k_ref[:, sl]
            v = v_ref[:, sl]
            qd = q_ref[:, sl] * scale * jnp.exp(bh)
            kd = k * jnp.exp(-bh)
            ke = k * jnp.exp(bl - bh)
            st = st_ref[sl, :]
            p = jnp.where(mask, _dot(qd, kd, _NT), 0.0)
            outs.append(_dot(p, v) + _dot(qd, st, _NT))
            st_ref[sl, :] = st * jnp.exp(bl) + _dot(v, ke, _TN)
        o_ref[...] = jnp.concatenate(outs, axis=1)

    cmap = (lambda n: nch - 1 - n) if reverse else (lambda n: n)
    col = lambda cb: pl.BlockSpec((GLA_CHUNK, GLA_WIDTH), lambda n, cb=cb: (cmap(n), cb))
    return pl.pallas_call(
        body, name=name, grid=(nch,),
        in_specs=[col(CB_GQ), col(CB_GK), col(CB_GV), col(0)],
        out_specs=[col(0), pl.BlockSpec((1, GLA_WIDTH, hd), lambda n: (cmap(n), 0, 0))],
        out_shape=[jax.ShapeDtypeStruct((length, GLA_WIDTH), F32), jax.ShapeDtypeStruct((nch, GLA_WIDTH, hd), F32)],
        scratch_shapes=[pltpu.VMEM((GLA_WIDTH, hd), F32)],
        compiler_params=pltpu.CompilerParams(dimension_semantics=("arbitrary",)),
    )(pg, pg, pg, la)


def _gla_bwd(pg, la, do, sprev, reverse, *, name):
    length = la.shape[0]
    nch = length // GLA_CHUNK
    scale = GLA_HEAD_DIM ** -0.5
    last = 0 if reverse else GLA_CHUNK - 1
    hd = GLA_HEAD_DIM

    def body(q_ref, k_ref, v_ref, la_ref, do_ref, sp_ref, dq_ref, dk_ref, dv_ref, dla_ref, dst_ref):
        @pl.when(pl.program_id(0) == 0)
        def _():
            dst_ref[...] = jnp.zeros_like(dst_ref)

        mask = _gla_chunk_mask(reverse)
        tri = mask.astype(BF16)
        b = _exact_dot(tri, la_ref[...], _NN)
        is_last = lax.broadcasted_iota(jnp.int32, (GLA_CHUNK, hd), 0) == last
        dqs, dks, dvs, dbs = [], [], [], []
        for h in range(GLA_HEADS):
            sl = slice(h * hd, (h + 1) * hd)
            bh = b[:, sl]
            bl = bh[last:last + 1, :]
            eb, enb, ebl, el = jnp.exp(bh), jnp.exp(-bh), jnp.exp(bl - bh), jnp.exp(bl)
            k = k_ref[:, sl]
            v = v_ref[:, sl]
            dov = do_ref[:, sl]
            qd = q_ref[:, sl] * scale * eb
            kd = k * enb
            ke = k * ebl
            st = sp_ref[0, sl, :]
            dst = dst_ref[sl, :]
            p = jnp.where(mask, _dot(qd, kd, _NT), 0.0)
            dp = jnp.where(mask, _dot(dov, v, _NT), 0.0)
            dqd = _dot(dp, kd) + _dot(dov, st)
            dkd = _dot(dp, qd, _TN)
            dvs.append(_dot(p, dov, _TN) + _dot(ke, dst, _NT))
            dke = _dot(v, dst)
            dst_ref[sl, :] = dst * el + _dot(dov, qd, _TN)
            dbl = el * jnp.sum(dst * st, axis=0, keepdims=True) + jnp.sum(dke * ke, axis=0, keepdims=True)
            db = dqd * qd - dkd * kd - dke * ke
            dbs.append(jnp.where(is_last, db + dbl, db))
            dqs.append(dqd * eb * scale)
            dks.append(dkd * enb + dke * ebl)
        dq_ref[...] = jnp.concatenate(dqs, axis=1)
        dk_ref[...] = jnp.concatenate(dks, axis=1)
        dv_ref[...] = jnp.concatenate(dvs, axis=1)
        tri_t = _gla_chunk_mask(not reverse).astype(BF16)
        dla_ref[...] = _exact_dot(tri_t, jnp.concatenate(dbs, axis=1), _NN)

    cmap = (lambda n: n) if reverse else (lambda n: nch - 1 - n)
    col = lambda cb: pl.BlockSpec((GLA_CHUNK, GLA_WIDTH), lambda n, cb=cb: (cmap(n), cb))
    wide = jax.ShapeDtypeStruct((length, GLA_WIDTH), F32)
    return pl.pallas_call(
        body, name=name, grid=(nch,),
        in_specs=[col(CB_GQ), col(CB_GK), col(CB_GV), col(0), col(0),
                  pl.BlockSpec((1, GLA_WIDTH, hd), lambda n: (cmap(n), 0, 0))],
        out_specs=[col(0)] * 4, out_shape=[wide] * 4,
        scratch_shapes=[pltpu.VMEM((GLA_WIDTH, hd), F32)],
        compiler_params=pltpu.CompilerParams(dimension_semantics=("arbitrary",)),
    )(pg, pg, pg, la, do, sprev)


def _log_sigmoid(x):
    return jnp.minimum(x, 0.0) - jnp.log(1.0 + jnp.exp(-jnp.abs(x)))


def _gla_alpha_padded(w_alpha):
    w = jnp.zeros((2, 128, GLA_WIDTH), w_alpha.dtype)
    w = w.at[0, 0:GLA_LOWRANK].set(w_alpha[0])
    return w.at[1, GLA_LOWRANK:2 * GLA_LOWRANK].set(w_alpha[1])


def _gla_branch_fwd(pg, w_alpha, b_alpha, norm_gain):
    wa = _gla_alpha_padded(w_alpha).astype(BF16)

    def gates(z, w, bias):
        return (_log_sigmoid(_dot(z, w[0]) + bias[0:1]) / GLA_TAU, _log_sigmoid(_dot(z, w[1]) + bias[1:2]) / GLA_TAU)
    la_f, la_b = _rowmap(gates, [(pg, CB_Z, 128)], [wa, b_alpha], [(GLA_WIDTH, F32), (GLA_WIDTH, F32)], tl=512,
                         name="gla_gates")
    o_f, sp_f = _gla_fwd(pg, la_f, False, name="gla_fwd")
    o_b, sp_b = _gla_fwd(pg, la_b, True, name="gla_fwd_rev")

    def post(of, ob, gate, gn):
        o = of + ob
        on = jnp.concatenate([o[:, s:s + GLA_HEAD_DIM] * _rms(o[:, s:s + GLA_HEAD_DIM]) * gn
                              for s in range(0, GLA_WIDTH, GLA_HEAD_DIM)], axis=1)
        return o, on * (gate * _sigmoid(gate))
    o, y = _rowmap(post, [o_f, o_b, (pg, CB_GG, GLA_WIDTH)], [norm_gain.reshape(1, -1)],
                   [(GLA_WIDTH, F32), (GLA_WIDTH, BF16)], tl=512, name="gla_post")
    return y, (wa, la_f, la_b, sp_f, sp_b, o)


def _gla_branch_bwd(pg, w_alpha, b_alpha, norm_gain, saved, dy):
    wa, la_f, la_b, sp_f, sp_b, o = saved

    def post_bwd(dyv, ov, gate, gn):
        s = _sigmoid(gate)
        dos, dgn, ons = [], [], []
        for c in range(0, GLA_WIDTH, GLA_HEAD_DIM):
            oh = ov[:, c:c + GLA_HEAD_DIM]
            r = _rms(oh)
            don = dyv[:, c:c + GLA_HEAD_DIM] * (gate[:, c:c + GLA_HEAD_DIM] * s[:, c:c + GLA_HEAD_DIM])
            gd = don * gn
            dos.append(r * gd - oh * (r * r * r) * jnp.mean(oh * gd, axis=-1, keepdims=True))
            dgn.append(jnp.sum(don * oh * r, axis=0, keepdims=True))
            ons.append(oh * r * gn)
        on = jnp.concatenate(ons, axis=1)
        dgate = dyv * on * (s * (1.0 + gate * (1.0 - s)))
        return jnp.concatenate(dos, axis=1), dgate, jnp.concatenate(dgn, axis=1)
    do, dgate, dgn = _rowmap(post_bwd, [dy, o, (pg, CB_GG, GLA_WIDTH)], [norm_gain.reshape(1, -1)],
                             [(GLA_WIDTH, F32), (GLA_WIDTH, F32)], [(1, GLA_WIDTH)], tl=512, name="gla_post_bwd")
    dq_f, dk_f, dv_f, dla_f = _gla_bwd(pg, la_f, do, sp_f, False, name="gla_bwd")
    dq_b, dk_b, dv_b, dla_b = _gla_bwd(pg, la_b, do, sp_b, True, name="gla_bwd_rev")

    def gates_bwd(z, dlf, dlb, w, bias):
        dz = jnp.zeros_like(z)
        dlogits, dbs = [], []
        for d, dl in ((0, dlf), (1, dlb)):
            logit = _dot(z, w[d]) + bias[d:d + 1]
            dlogit = dl * (1.0 / GLA_TAU) * jnp.exp(_log_sigmoid(-logit))
            dz = dz + _dot(dlogit, w[d], _NT)
            dlogits.append(dlogit)
            dbs.append(jnp.sum(dlogit, axis=0, keepdims=True))
        return dz, dlogits[0], dlogits[1], dbs[0], dbs[1]
    dz, dlg_f, dlg_b, dba_f, dba_b = _rowmap(
        gates_bwd, [(pg, CB_Z, 128), dla_f, dla_b], [wa, b_alpha], [(128, F32), (GLA_WIDTH, BF16), (GLA_WIDTH, BF16)],
        [(1, GLA_WIDTH), (1, GLA_WIDTH)], tl=512, name="gla_gates_bwd")
    dwa_f = _mm(dlg_f, (pg, CB_Z, 128), ta=True, name="gla_dwalpha")
    dwa_b = _mm(dlg_b, (pg, CB_Z, 128), ta=True, name="gla_dwalpha")
    grads = {'w_alpha': jnp.stack([dwa_f[:, 0:GLA_LOWRANK].T, dwa_b[:, GLA_LOWRANK:2 * GLA_LOWRANK].T]),
             'b_alpha': jnp.concatenate([dba_f, dba_b], axis=0),
             'norm': jnp.sum(dgn.reshape(GLA_HEADS, GLA_HEAD_DIM), axis=0)}
    return [dq_f, dq_b], [dk_f, dk_b], [dv_f, dv_b], dgate, dz, grads


def _rope_tables(length):
    half = ATTN_HEAD_DIM // 2
    inv_freq = ROPE_BASE ** (-jnp.arange(half // 2, dtype=F32) * 2.0 / half)
    t = jnp.arange(length, dtype=jnp.int32)
    def one(pos):
        ang = pos.astype(F32)[:, None] * inv_freq[None, :]
        c, s = jnp.cos(ang), jnp.sin(ang)
        return jnp.concatenate([c, c], axis=1), jnp.concatenate([-s, s], axis=1)
    c_r, s_r = one(t // GRID_W)
    c_c, s_c = one(t % GRID_W)
    return jnp.concatenate([c_r, c_c], axis=1), jnp.concatenate([s_r, s_c], axis=1)


def _rope_swap(y):
    w = y.shape[1]
    lane = lax.broadcasted_iota(jnp.int32, y.shape, 1)
    return jnp.where(lane % 32 < 16, pltpu.roll(y, w - 16, 1), pltpu.roll(y, 16, 1))


def _head_sums(x, ones):
    parts = [lax.dot_general(p, ones, _NN, preferred_element_type=F32) for p in _split3(x)]
    return parts[0] + parts[1] + parts[2]


def _head_ones(width):
    seg = np.arange(width) // ATTN_HEAD_DIM
    return jnp.asarray(seg[:, None] == seg[None, :], BF16)


def _qk_prep_fwd(pg, cb, width, gain, cos, sin, scale, *, name):
    heads = width // ATTN_HEAD_DIM
    def fn(x, c, s, g, ones):
        r = lax.rsqrt(_head_sums(x * x, ones) * (1.0 / ATTN_HEAD_DIM) + NORM_EPS)
        y = x * r * g
        return (y * c + _rope_swap(y) * s) * scale
    return _rowmap(fn, [(pg, cb, width), jnp.tile(cos, (1, heads)), jnp.tile(sin, (1, heads))],
                   [jnp.tile(gain, heads).reshape(1, -1), _head_ones(width)], [(width, BF16)], tl=512, name=name)[0]


def _qk_prep_bwd(pg, cb, width, gain, cos, sin, scale, dout, *, name):
    heads = width // ATTN_HEAD_DIM
    def fn(x, dov, c, s, g, ones):
        r = lax.rsqrt(_head_sums(x * x, ones) * (1.0 / ATTN_HEAD_DIM) + NORM_EPS)
        dos = dov * scale
        dy = dos * c + _rope_swap(dos * s)
        gd = dy * g
        dx = r * gd - x * (r * r * r) * (_head_sums(x * gd, ones) * (1.0 / ATTN_HEAD_DIM))
        return dx, jnp.sum(dy * x * r, axis=0, keepdims=True)
    dx, dg = _rowmap(fn, [(pg, cb, width), dout, jnp.tile(cos, (1, heads)), jnp.tile(sin, (1, heads))],
                     [jnp.tile(gain, heads).reshape(1, -1), _head_ones(width)], [(width, F32)], [(1, width)], tl=512,
                     name=name)
    return dx, jnp.sum(dg.reshape(heads, ATTN_HEAD_DIM), axis=0)


def _to_heads(x, heads):
    return jnp.transpose(x.reshape(x.shape[0], heads, ATTN_HEAD_DIM), (1, 0, 2))


def _from_heads(x):
    return jnp.transpose(x, (1, 0, 2)).reshape(x.shape[1], x.shape[0] * ATTN_HEAD_DIM)


ATTN_GROUP = ATTN_Q_HEADS // ATTN_KV_HEADS
ATTN_TQ = 256


def _attn_fwd(q, k, v, side=None):
    length = q.shape[1]
    tq = min(ATTN_TQ, length)
    grid = (ATTN_KV_HEADS, length // tq)

    def compute(refs):
        q_ref, k_ref, v_ref, o_ref = refs
        kk, vv = k_ref[0], v_ref[0]
        for g in range(ATTN_GROUP):
            s = _dot(q_ref[g], kk, _NT)
            p = jnp.exp(s - jnp.max(s, axis=-1, keepdims=True))
            o_ref[g] = _dot(p, vv) / jnp.sum(p, axis=-1, keepdims=True)

    def body(*refs):
        _carried(side, grid, refs, 3, 1, 0, compute)

    kv_spec = pl.BlockSpec((1, length, ATTN_HEAD_DIM), lambda h, i: (h, 0, 0))
    q_spec = pl.BlockSpec((ATTN_GROUP, tq, ATTN_HEAD_DIM), lambda h, i: (h, i, 0))
    (out,), gathered = _side_call(
        body, side, name="attn_fwd", grid=grid, in_specs=[q_spec, kv_spec, kv_spec], out_specs=[q_spec],
        out_shape=[jax.ShapeDtypeStruct(q.shape, F32)], scratch=[], args=[q, k, v], semantics=("parallel", "parallel"))
    return out, gathered


def _attn_bwd(q, k, v, o, do, side=None):
    length = q.shape[1]
    tq = min(ATTN_TQ, length)
    grid = (ATTN_KV_HEADS, length // tq)

    def body(*refs):
        _carried(side, grid, refs, 5, 3, 0, compute)

    def compute(refs):
        q_ref, k_ref, v_ref, o_ref, do_ref, dq_ref, dk_ref, dv_ref = refs

        @pl.when(pl.program_id(1) == 0)
        def _():
            dk_ref[...] = jnp.zeros_like(dk_ref)
            dv_ref[...] = jnp.zeros_like(dv_ref)

        kk, vv = k_ref[0], v_ref[0]
        for g in range(ATTN_GROUP):
            qg, dog = q_ref[g], do_ref[g]
            s = _dot(qg, kk, _NT)
            p = jnp.exp(s - jnp.max(s, axis=-1, keepdims=True))
            p = p / jnp.sum(p, axis=-1, keepdims=True)
            dp = _dot(dog, vv, _NT)
            ds = p * (dp - jnp.sum(dog * o_ref[g], axis=-1, keepdims=True))
            dq_ref[g] = _dot(ds, kk)
            dk_ref[0] += _dot(ds, qg, _TN)
            dv_ref[0] += _dot(p, dog, _TN)

    kv_spec = pl.BlockSpec((1, length, ATTN_HEAD_DIM), lambda h, i: (h, 0, 0))
    q_spec = pl.BlockSpec((ATTN_GROUP, tq, ATTN_HEAD_DIM), lambda h, i: (h, i, 0))
    return _side_call(
        body, side, name="attn_bwd", grid=grid, in_specs=[q_spec, kv_spec, kv_spec, q_spec, q_spec],
        out_specs=[q_spec, kv_spec, kv_spec],
        out_shape=[jax.ShapeDtypeStruct(q.shape, F32), jax.ShapeDtypeStruct(k.shape, F32), jax.ShapeDtypeStruct(k.shape, F32)],
        scratch=[], args=[q, k, v, o, do], semantics=("parallel", "arbitrary"))


def _attn_branch_fwd(pg, q_gain, k_gain, side=None):
    cos, sin = _rope_tables(pg.shape[0])
    qp = _qk_prep_fwd(pg, CB_AQ, ATTN_WIDTH, q_gain, cos, sin, ATTN_HEAD_DIM ** -0.5, name="attn_q_prep")
    kp = _qk_prep_fwd(pg, CB_AK, ATTN_KV_WIDTH, k_gain, cos, sin, 1.0, name="attn_k_prep")
    qh, kh = _to_heads(qp, ATTN_Q_HEADS), _to_heads(kp, ATTN_KV_HEADS)
    vh = _to_heads(pg[:, P_OFF + 3200:P_OFF + 3328].astype(BF16), ATTN_KV_HEADS)
    oh, gathered = _attn_fwd(qh, kh, vh, side)
    return _from_heads(oh).astype(BF16), (cos, sin, qh, kh, vh, oh), gathered


def _attn_branch_bwd(pg, q_gain, k_gain, saved, dy, side=None):
    cos, sin, qh, kh, vh, oh = saved
    (dqh, dkh, dvh), carried = _attn_bwd(qh, kh, vh, oh, _to_heads(dy, ATTN_Q_HEADS), side)
    dq, dqg = _qk_prep_bwd(pg, CB_AQ, ATTN_WIDTH, q_gain, cos, sin, ATTN_HEAD_DIM ** -0.5, _from_heads(dqh),
                           name="attn_q_prep_bwd")
    dk, dkg = _qk_prep_bwd(pg, CB_AK, ATTN_KV_WIDTH, k_gain, cos, sin, 1.0, _from_heads(dkh), name="attn_k_prep_bwd")
    return dq, dk, _from_heads(dvh), {'q_norm': dqg, 'k_norm': dkg}, carried


def _gate_cols():
    return [slice(i * D_MODEL, (i + 1) * D_MODEL) for i in range(3)]


def _mixer_fwd(x, lw, side_in=None, after_in=None, side_attn=None):
    h = _rmsnorm_fwd(x, lw['mix_norm'])
    if side_in is None:
        pg = _mm(h, lw['w_pg'], name="mix_in")
    else:
        pg, got_in = _mm(h, lw['w_pg'], side=side_in, name="mix_in")
        after_in(got_in)
    y_s5, s_s5 = _s5_fwd(pg, lw['s5'], lw['s5_w_glu'])
    y_gla, s_gla = _gla_branch_fwd(pg, lw['gla_w_alpha'], lw['gla_b_alpha'], lw['gla_norm'])
    y_att, s_att, got_attn = _attn_branch_fwd(pg, lw['attn_q_norm'], lw['attn_k_norm'], side_attn)
    ys = (y_s5, y_gla, y_att)
    br = [_mm(y, lw[n], name="mix_branch") for y, n in zip(ys, ('w_branch_s5', 'w_branch_gla', 'w_branch_attn'))]

    def merge(g0, g1, g2, b0, b1, b2, bias):
        acc = None
        for g, b, c in zip((g0, g1, g2), (b0, b1, b2), _gate_cols()):
            term = _sigmoid(g + bias[:, c]) * b
            acc = term if acc is None else acc + term
        return acc
    merged = _rowmap(merge, [(pg, 0, D_MODEL), (pg, 1, D_MODEL), (pg, 2, D_MODEL)] + br,
                     [lw['b_merge_gate'].reshape(1, -1)], [(D_MODEL, BF16)], tl=256, name="mix_merge")[0]
    x_out = _mm(merged, lw['w_out'], add=x, name="mix_out")
    return x_out, (x, h, pg, ys, (s_s5, s_gla, s_att), br, merged), got_attn


def _mixer_bwd(saved, lw, dx_out, side=None):
    x, h, pg, ys, (s_s5, s_gla, s_att), br, merged = saved
    grads = {'w_out': _mm(merged, dx_out, ta=True, out_dtype=BF16, name="mix_dwout")}
    dmerged = _mm(dx_out, lw['w_out'], tb=True, name="mix_dmerged")

    def merge_bwd(g0, g1, g2, b0, b1, b2, dm, bias):
        dbr, dgp = [], []
        for g, b, c in zip((g0, g1, g2), (b0, b1, b2), _gate_cols()):
            s = _sigmoid(g + bias[:, c])
            dbr.append(dm * s)
            dgp.append(dm * b * (s * (1.0 - s)))
        dgp = jnp.concatenate(dgp, axis=1)
        return dbr[0], dbr[1], dbr[2], dgp, jnp.sum(dgp, axis=0, keepdims=True)
    d0, d1, d2, dgpre, dbias = _rowmap(
        merge_bwd, [(pg, 0, D_MODEL), (pg, 1, D_MODEL), (pg, 2, D_MODEL)] + br + [dmerged],
        [lw['b_merge_gate'].reshape(1, -1)], [(D_MODEL, BF16)] * 3 + [(GATE_WIDTH, BF16)], [(1, GATE_WIDTH)], tl=256,
        name="mix_merge_bwd")
    grads['b_merge_gate'] = dbias[0]
    dys = []
    for y, dbr, n in zip(ys, (d0, d1, d2), ('w_branch_s5', 'w_branch_gla', 'w_branch_attn')):
        grads[n] = _mm(y, dbr, ta=True, out_dtype=BF16, name="mix_dwbranch")
        dys.append(_mm(dbr, lw[n], tb=True, name="mix_dy"))
    du, g_s5 = _s5_bwd(pg, lw['s5'], lw['s5_w_glu'], s_s5, dys[0])
    dgq, dgk, dgv, dgg, dz, g_gla = _gla_branch_bwd(pg, lw['gla_w_alpha'], lw['gla_b_alpha'], lw['gla_norm'], s_gla, dys[1])
    daq, dak, dav, g_att, carried = _attn_branch_bwd(pg, lw['attn_q_norm'], lw['attn_k_norm'], s_att, dys[2], side)

    def assemble(dgp, u0, u1, u2, q0, q1, k0, k1, v0, v1, gg, aq, ak, av, z):
        pad = jnp.zeros((dgp.shape[0], IN_PAD - 3456), F32)
        parts = [dgp.astype(F32), u0 + u1 + u2, q0 + q1, k0 + k1, v0 + v1, gg, aq, ak, av, z, pad]
        return jnp.concatenate(parts, axis=1)
    dpg = _rowmap(assemble, [dgpre] + du + dgq + dgk + dgv + [dgg, daq, dak, dav, dz], [], [(PG_WIDTH, BF16)], tl=256,
                  name="mix_dpg")[0]
    grads['w_pg'] = _mm(h, dpg, ta=True, out_dtype=BF16, name="mix_dwpg")
    dh = _mm(dpg, lw['w_pg'], tb=True, name="mix_dh")
    dx, grads['mix_norm'] = _rmsnorm_bwd(x, lw['mix_norm'], dh, dx_out)
    grads['s5'], grads['gla'], grads['attn'] = g_s5, g_gla, g_att
    return dx, grads, carried


def _loss_head(x, gain, target):
    width = x.shape[1]

    def fn(xv, tv, g):
        r = _rms(xv)
        err = xv * r * g - tv
        dy = err * (1.0 / width)
        gd = dy * g
        dx = r * gd - xv * (r * r * r) * jnp.mean(xv * gd, axis=-1, keepdims=True)
        loss = jnp.sum(0.5 * jnp.mean(err * err, axis=-1, keepdims=True), axis=0, keepdims=True)
        return dx, jnp.broadcast_to(loss, (1, 128)), jnp.sum(dy * xv * r, axis=0, keepdims=True)
    dx, loss, dgain = _rowmap(fn, [x, target], [gain.reshape(1, -1)], [(width, F32)], [(1, 128), (1, width)], tl=256,
                              name="loss_head")
    return loss[0, 0], dx, dgain[0]


def _row_tile(rows, cap=256):
    for t in range(cap - cap % 16, 0, -16):
        if rows % t == 0:
            return t
    return rows


def _reduce_adamw(parts, w, m, v, *, name):
    r, c = w.shape
    if len(parts) > 1 and parts[0].shape[1] % 8:
        parts = [jnp.concatenate(parts, axis=1)]
    nparts, rows = parts[0].shape[0], parts[0].shape[1]
    tr = _row_tile(rows)
    per = rows // tr

    def body(*refs):
        p_refs, (w_ref, m_ref, v_ref, g_ref, d_ref, m2_ref, v2_ref) = refs[:len(parts)], refs[len(parts):]
        g = None
        for k, p_ref in enumerate(p_refs):
            gk = p_ref[0].astype(F32)
            for j in range(1, nparts):
                gk = gk + p_ref[j].astype(F32)
            g = gk if g is None else jnp.where(pl.program_id(0) // per == k, gk, g)
        m2 = ADAM_B1 * m_ref[...] + (1.0 - ADAM_B1) * g
        v2 = ADAM_B2 * v_ref[...] + (1.0 - ADAM_B2) * (g * g)
        m_hat = m2 / (1.0 - ADAM_B1 ** ADAM_STEP)
        v_hat = v2 / (1.0 - ADAM_B2 ** ADAM_STEP)
        g_ref[...] = g
        d_ref[...] = -ADAM_LR * (m_hat / (jnp.sqrt(v_hat) + ADAM_EPS) + ADAM_WD * w_ref[...])
        m2_ref[...] = m2
        v2_ref[...] = v2

    flat = pl.BlockSpec((tr, c), lambda i: (i, 0))
    p_specs = [pl.BlockSpec((nparts, tr, c), lambda i, k=k: (0, jnp.clip(i - k * per, 0, per - 1), 0)) for k in range(len(parts))]
    return pl.pallas_call(
        body, name=name, grid=(r // tr,), in_specs=p_specs + [flat, flat, flat],
        out_specs=[flat] * 4, out_shape=[jax.ShapeDtypeStruct((r, c), F32)] * 4,
        compiler_params=pltpu.CompilerParams(dimension_semantics=("parallel",)),
    )(*parts, w, m, v)


def _all_gather(blocks, *, name):
    side = _SideGather(blocks)

    def body(*refs):
        start, finish = side.hooks(refs)
        start()
        finish()

    return pl.pallas_call(body, name=name, out_shape=side.out_shape, in_specs=side.in_specs, out_specs=side.out_specs,
                          scratch_shapes=side.scratch)(*blocks)


class _SideGather:
    def __init__(self, blocks):
        self.blocks = list(blocks)
        self.n = n = len(self.blocks)
        hbm = pl.BlockSpec(memory_space=pl.ANY)
        self.in_specs, self.out_specs = [hbm] * n, [hbm] * n
        self.out_shape = [jax.ShapeDtypeStruct((N_DEV,) + b.shape, b.dtype) for b in self.blocks]
        self.scratch = [pltpu.SemaphoreType.DMA((n, 7)), pltpu.SemaphoreType.DMA((n, 7)), pltpu.SemaphoreType.DMA((n,))]

    def hooks(self, refs):
        n = self.n
        x_refs, out_refs = refs[:n], refs[n:2 * n]
        send_sems, recv_sems, local_sems = refs[2 * n:]
        x, y, c = lax.axis_index("x"), lax.axis_index("y"), lax.axis_index("c")
        me, sibling = (x, y, c), (x, y, 1 - c)
        chips = [(1 - x, y), (x, 1 - y), (1 - x, 1 - y)]

        def slot(t, px, py, pc):
            return out_refs[t].at[4 * px + 2 * py + pc]

        def copy(t, k, blk, to, own=False):
            return pltpu.make_async_remote_copy(
                src_ref=x_refs[t] if own else slot(t, *blk), dst_ref=slot(t, *blk), send_sem=send_sems.at[t, k],
                recv_sem=recv_sems.at[t, k], device_id=to, device_id_type=pl.DeviceIdType.MESH)

        def mine(t):
            return pltpu.make_async_copy(x_refs[t], slot(t, *me), local_sems.at[t])

        def first(t):
            return [copy(t, 0, me, sibling, own=True)] + [copy(t, 1 + j, me, (*chip, c), own=True) for j, chip in enumerate(chips)]

        def start():
            for t in range(n):
                mine(t).start()
            for t in range(n):
                for cp in first(t):
                    cp.start()

        def finish():
            passed = []
            for j, chip in enumerate(chips):
                for t in range(n):
                    copy(t, 1 + j, (*chip, c), me).wait_recv()
                    passed.append(copy(t, 4 + j, (*chip, c), sibling))
                    passed[-1].start()
            for t in range(n):
                copy(t, 0, sibling, me).wait_recv()
            for j, chip in enumerate(chips):
                for t in range(n):
                    copy(t, 4 + j, (*chip, 1 - c), me).wait_recv()
            for t in range(n):
                for cp in first(t):
                    cp.wait_send()
            for cp in passed:
                cp.wait_send()
            for t in range(n):
                mine(t).wait()

        return start, finish


def _first_last_step(grid):
    ids = [pl.program_id(a) for a in range(len(grid))]
    first = functools.reduce(lambda p, q: p & q, [i == 0 for i in ids])
    last = functools.reduce(lambda p, q: p & q, [i == n - 1 for i, n in zip(ids, grid)])
    return first, last


def _carried(side, grid, refs, n_in, n_out, n_scratch, compute):
    if side is None:
        compute(refs)
        return
    n = side.n
    main = refs[:n_in] + refs[n_in + n:n_in + n + n_out] + refs[n_in + 2 * n + n_out:n_in + 2 * n + n_out + n_scratch]
    side_refs = refs[n_in:n_in + n] + refs[n_in + n + n_out:n_in + 2 * n + n_out] + refs[n_in + 2 * n + n_out + n_scratch:]
    start, finish = side.hooks(side_refs)
    first, last = _first_last_step(grid)
    pl.when(first)(start)
    compute(main)
    pl.when(last)(finish)


N_CHIP = N_DEV // 2


def _swap_with_sibling(arrays, *, name):
    n = len(arrays)

    def body(*refs):
        src_refs, out_refs = refs[:n], refs[n:2 * n]
        send_sems, recv_sems = refs[2 * n:]
        sibling = (lax.axis_index("x"), lax.axis_index("y"), 1 - lax.axis_index("c"))
        copies = [pltpu.make_async_remote_copy(
            src_ref=src_refs[t], dst_ref=out_refs[t], send_sem=send_sems.at[t], recv_sem=recv_sems.at[t],
            device_id=sibling, device_id_type=pl.DeviceIdType.MESH) for t in range(n)]
        for cp in copies:
            cp.start()
        for cp in copies:
            cp.wait()

    hbm = pl.BlockSpec(memory_space=pl.ANY)
    return pl.pallas_call(
        body, name=name, out_shape=[jax.ShapeDtypeStruct(a.shape, a.dtype) for a in arrays],
        in_specs=[hbm] * n, out_specs=[hbm] * n,
        scratch_shapes=[pltpu.SemaphoreType.DMA((n,)), pltpu.SemaphoreType.DMA((n,))],
    )(*arrays)


def _exchange_chips(stacks, *, name):
    side = _SideChipExchange(stacks)

    def body(*refs):
        start, finish = side.hooks(refs)
        start()
        finish()

    return pl.pallas_call(body, name=name, out_shape=side.out_shape, in_specs=side.in_specs, out_specs=side.out_specs,
                          scratch_shapes=side.scratch)(*stacks)


class _SideChipExchange:
    def __init__(self, stacks):
        self.blocks = list(stacks)
        self.n = n = len(self.blocks)
        hbm = pl.BlockSpec(memory_space=pl.ANY)
        self.in_specs, self.out_specs = [hbm] * n, [hbm] * n
        self.out_shape = [jax.ShapeDtypeStruct(s.shape, s.dtype) for s in self.blocks]
        self.scratch = [pltpu.SemaphoreType.DMA((n, N_CHIP - 1)), pltpu.SemaphoreType.DMA((n, N_CHIP - 1)),
                        pltpu.SemaphoreType.DMA((n,))]

    def hooks(self, refs):
        n = self.n
        g_refs, out_refs = refs[:n], refs[n:2 * n]
        send_sems, recv_sems, local_sems = refs[2 * n:]
        x, y, c = lax.axis_index("x"), lax.axis_index("y"), lax.axis_index("c")
        me = 2 * x + y

        def copies():
            mine = [pltpu.make_async_copy(g_refs[t].at[me], out_refs[t].at[me], local_sems.at[t]) for t in range(n)]
            remote = []
            for k in range(1, N_CHIP):
                px, py = x ^ (k >> 1 & 1), y ^ (k & 1)
                for t in range(n):
                    remote.append(pltpu.make_async_remote_copy(
                        src_ref=g_refs[t].at[2 * px + py], dst_ref=out_refs[t].at[me], send_sem=send_sems.at[t, k - 1],
                        recv_sem=recv_sems.at[t, k - 1], device_id=(px, py, c), device_id_type=pl.DeviceIdType.MESH))
            return mine, remote

        def start():
            mine, remote = copies()
            for cp in mine + remote:
                cp.start()

        def finish():
            mine, remote = copies()
            for cp in remote:
                cp.wait_recv()
            for cp in remote:
                cp.wait_send()
            for cp in mine:
                cp.wait()

        return start, finish


def _pair_sum(a, b):
    return _rowmap(lambda u, v: u.astype(F32) + v.astype(F32), [a, b], [], [(a.shape[1], BF16)], tl=_row_tile(a.shape[0], 512),
                   name="pair_sum")[0]


SMALL_COLS = 128


def _pack_small(arrays):
    flat = jnp.concatenate([a.astype(F32).reshape(-1, SMALL_COLS) for a in arrays], axis=0)
    return jnp.pad(flat, ((0, -flat.shape[0] % 256), (0, 0)))


def _unpack_small(packed, shapes):
    out, off = [], 0
    for s in shapes:
        r = math.prod(s) // SMALL_COLS
        out.append(packed[off:off + r].reshape(s))
        off += r
    return out


def _split_shards(full, axis):
    shape = full.shape
    split = full.reshape(shape[:axis] + (N_DEV, shape[axis] // N_DEV) + shape[axis + 1:])
    return jnp.moveaxis(split, axis, 0)


def _join_shards(stack, axis):
    moved = jnp.moveaxis(stack, 0, axis)
    shape = moved.shape
    return moved.reshape(shape[:axis] + (shape[axis] * shape[axis + 1],) + shape[axis + 2:])


def _w_in_unpadded(w):
    return jnp.concatenate([w[..., :2560], w[..., 3328:3360], w[..., 2560:3328]], axis=-1)


FFN1_W = ('ffn1_w_gate', 'ffn1_w_up', 'ffn1_w_down')
FFN2_W = ('ffn2_w_gate', 'ffn2_w_up', 'ffn2_w_down')
MIX_IN_W = ('w_in', 'w_merge_gate')
MIX_REST_W = ('s5_w_glu', 'gla_w_alpha', 'gla_b_alpha', 'w_branch_s5', 'w_branch_gla', 'w_branch_attn', 'w_out')


def _mixer_weights(full, w, s5, i):
    lw = {n: w[n][i] for n in ('mix_norm', 'gla_norm', 'attn_q_norm', 'attn_k_norm', 'b_merge_gate')}
    lw['s5'] = {'b_mat': s5['b_mat'][i], 'c_mat': s5['c_mat'][i], 'tabs': [t[i] for t in s5['tabs']],
                'tabs_adj': [t[i] for t in s5['tabs_adj']], 'd': w['s5_d'][i]}
    w_in = full['w_in']
    pad = jnp.zeros((D_MODEL, IN_PAD - IN_WIDTH), w_in.dtype)
    lw['w_pg'] = jnp.concatenate([full['w_merge_gate'], w_in[:, :2560], w_in[:, 2592:], w_in[:, 2560:2592], pad], axis=1)
    return lw


def _mixer_weights_rest(full):
    lw = {n: full[n] for n in MIX_REST_W if n != 'gla_b_alpha'}
    lw['gla_b_alpha'] = full['gla_b_alpha'].astype(F32)
    return lw


def _chip_sums(grads, names):
    core = lax.axis_index("c")
    own, for_sibling = [], []
    for n in names:
        by_owner = _split_shards(grads[n], SHARD_AXIS[n] - 1).astype(BF16)
        by_owner = by_owner.reshape((N_CHIP, 2) + by_owner.shape[1:])
        own.append(lax.dynamic_index_in_dim(by_owner, core, axis=1, keepdims=False))
        for_sibling.append(lax.dynamic_index_in_dim(by_owner, 1 - core, axis=1, keepdims=False))
    from_sibling = _swap_with_sibling(for_sibling, name="exchange_grads_sibling")
    return [_pair_sum(a.reshape(-1, a.shape[-1]), b.reshape(-1, b.shape[-1])).reshape(a.shape)
            for a, b in zip(own, from_sibling)]


def _step_local(x, target, w, shards):
    s5 = _s5_setup(w)
    full = [{} for _ in range(DEPTH)]

    def wanted(i, *groups):
        return _SideGather([shards[n][i] for names in groups for n in names])

    def arrived(i, stacks, *groups):
        names = [n for group in groups for n in group]
        for n, st in zip(names, stacks):
            full[i][n] = _join_shards(st, SHARD_AXIS[n] - 1)

    arrived(0, _all_gather([shards[n][0] for n in FFN1_W], name="gather_first"), FFN1_W)
    saved, lws = [], []
    for i in range(DEPTH):
        f, first = full[i], i == 0
        x, s1, got = _ffn_fwd(x, w['ffn1_norm'][i], f['ffn1_w_gate'], f['ffn1_w_up'], f['ffn1_w_down'],
                              wanted(i, MIX_IN_W) if first else None)
        if first:
            arrived(i, got, MIX_IN_W)
        lw = _mixer_weights(f, w, s5, i)
        lws.append(lw)

        def after_in(got_in, i=i, lw=lw):
            arrived(i, got_in, MIX_REST_W, FFN2_W)
            lw.update(_mixer_weights_rest(full[i]))
        if not first:
            lw.update(_mixer_weights_rest(f))
        x, s2, got = _mixer_fwd(x, lw, wanted(i, MIX_REST_W, FFN2_W) if first else None, after_in,
                                wanted(i + 1, FFN1_W, MIX_IN_W, MIX_REST_W) if first else wanted(i, FFN2_W))
        if first:
            arrived(i + 1, got, FFN1_W, MIX_IN_W, MIX_REST_W)
        else:
            arrived(i, got, FFN2_W)
        x, s3, _ = _ffn_fwd(x, w['ffn2_norm'][i], f['ffn2_w_gate'], f['ffn2_w_up'], f['ffn2_w_down'])
        saved.append((s1, s2, s3))
    loss, dx, d_final = _loss_head(x, w['final_norm'], target)
    per_layer, incoming = [None] * DEPTH, [{} for _ in range(DEPTH)]
    later = [n for n in SHARDED if n not in FFN2_W]
    pending = []
    for i in reversed(range(DEPTH)):
        f, lw, (s1, s2, s3), g = full[i], lws[i], saved[i], {}
        dx, g['ffn2_norm'], g['ffn2_w_gate'], g['ffn2_w_up'], g['ffn2_w_down'] = _ffn_bwd(
            s3, w['ffn2_norm'][i], f['ffn2_w_gate'], f['ffn2_w_up'], f['ffn2_w_down'], dx)
        pending.append((i, FFN2_W, _chip_sums(g, FFN2_W)))
        dx, gm, carried = _mixer_bwd(s2, lw, dx, _SideChipExchange([s for _, _, sums in pending for s in sums]))
        for layer, names, _ in pending:
            incoming[layer].update(zip(names, carried[:len(names)]))
            carried = carried[len(names):]
        dx, g['ffn1_norm'], g['ffn1_w_gate'], g['ffn1_w_up'], g['ffn1_w_down'] = _ffn_bwd(
            s1, w['ffn1_norm'][i], f['ffn1_w_gate'], f['ffn1_w_up'], f['ffn1_w_down'], dx)
        g['w_merge_gate'] = gm['w_pg'][:, :GATE_WIDTH]
        g['w_in'] = _w_in_unpadded(gm['w_pg'][:, GATE_WIDTH:])
        for n in ('w_out', 'b_merge_gate', 'w_branch_s5', 'w_branch_gla', 'w_branch_attn', 'mix_norm'):
            g[n] = gm[n]
        g['s5_d'], g['s5_w_glu'], g['s5_raw'] = gm['s5']['d'], gm['s5']['w_glu'], gm['s5']['raw']
        g['gla_w_alpha'], g['gla_b_alpha'], g['gla_norm'] = gm['gla']['w_alpha'], gm['gla']['b_alpha'], gm['gla']['norm']
        g['attn_q_norm'], g['attn_k_norm'] = gm['attn']['q_norm'], gm['attn']['k_norm']
        per_layer[i] = g
        pending = [(i, later, _chip_sums(g, later))]
    incoming[0].update(zip(later, _exchange_chips(pending[0][2], name="exchange_grads_chips")))
    stacked = _s5_param_grads(w, [g['s5_raw'] for g in per_layer])
    stacked['final_norm'] = d_final
    return loss, dx, per_layer, stacked, incoming


def kernel(x, ffn1_norm, ffn1_w_gate, ffn1_w_up, ffn1_w_down, mix_norm, w_in, s5_lambda_re, s5_lambda_im, s5_log_dt, s5_b_re, s5_b_im, s5_c_re, s5_c_im, s5_d, s5_w_glu, gla_w_alpha, gla_b_alpha, gla_norm, attn_q_norm, attn_k_norm, w_branch_s5, w_branch_gla, w_branch_attn, w_merge_gate, b_merge_gate, w_out, ffn2_norm, ffn2_w_gate, ffn2_w_up, ffn2_w_down, final_norm, loss_target, m_ffn1_norm, m_ffn1_w_gate, m_ffn1_w_up, m_ffn1_w_down, m_mix_norm, m_w_in, m_s5_lambda_re, m_s5_lambda_im, m_s5_log_dt, m_s5_b_re, m_s5_b_im, m_s5_c_re, m_s5_c_im, m_s5_d, m_s5_w_glu, m_gla_w_alpha, m_gla_b_alpha, m_gla_norm, m_attn_q_norm, m_attn_k_norm, m_w_branch_s5, m_w_branch_gla, m_w_branch_attn, m_w_merge_gate, m_b_merge_gate, m_w_out, m_ffn2_norm, m_ffn2_w_gate, m_ffn2_w_up, m_ffn2_w_down, m_final_norm, v_ffn1_norm, v_ffn1_w_gate, v_ffn1_w_up, v_ffn1_w_down, v_mix_norm, v_w_in, v_s5_lambda_re, v_s5_lambda_im, v_s5_log_dt, v_s5_b_re, v_s5_b_im, v_s5_c_re, v_s5_c_im, v_s5_d, v_s5_w_glu, v_gla_w_alpha, v_gla_b_alpha, v_gla_norm, v_attn_q_norm, v_attn_k_norm, v_w_branch_s5, v_w_branch_gla, v_w_branch_attn, v_w_merge_gate, v_b_merge_gate, v_w_out, v_ffn2_norm, v_ffn2_w_gate, v_ffn2_w_up, v_ffn2_w_down, v_final_norm):
    return _train_step(x, ffn1_norm, ffn1_w_gate, ffn1_w_up, ffn1_w_down, mix_norm, w_in, s5_lambda_re, s5_lambda_im, s5_log_dt, s5_b_re, s5_b_im, s5_c_re, s5_c_im, s5_d, s5_w_glu, gla_w_alpha, gla_b_alpha, gla_norm, attn_q_norm, attn_k_norm, w_branch_s5, w_branch_gla, w_branch_attn, w_merge_gate, b_merge_gate, w_out, ffn2_norm, ffn2_w_gate, ffn2_w_up, ffn2_w_down, final_norm, loss_target, m_ffn1_norm, m_ffn1_w_gate, m_ffn1_w_up, m_ffn1_w_down, m_mix_norm, m_w_in, m_s5_lambda_re, m_s5_lambda_im, m_s5_log_dt, m_s5_b_re, m_s5_b_im, m_s5_c_re, m_s5_c_im, m_s5_d, m_s5_w_glu, m_gla_w_alpha, m_gla_b_alpha, m_gla_norm, m_attn_q_norm, m_attn_k_norm, m_w_branch_s5, m_w_branch_gla, m_w_branch_attn, m_w_merge_gate, m_b_merge_gate, m_w_out, m_ffn2_norm, m_ffn2_w_gate, m_ffn2_w_up, m_ffn2_w_down, m_final_norm, v_ffn1_norm, v_ffn1_w_gate, v_ffn1_w_up, v_ffn1_w_down, v_mix_norm, v_w_in, v_s5_lambda_re, v_s5_lambda_im, v_s5_log_dt, v_s5_b_re, v_s5_b_im, v_s5_c_re, v_s5_c_im, v_s5_d, v_s5_w_glu, v_gla_w_alpha, v_gla_b_alpha, v_gla_norm, v_attn_q_norm, v_attn_k_norm, v_w_branch_s5, v_w_branch_gla, v_w_branch_attn, v_w_merge_gate, v_b_merge_gate, v_w_out, v_ffn2_norm, v_ffn2_w_gate, v_ffn2_w_up, v_ffn2_w_down, v_final_norm)


def _train_step(*args):
    nw = len(W_NAMES)
    x, target = args[0][0], args[1 + nw][0]
    w = dict(zip(W_NAMES, args[1:1 + nw]))
    m = dict(zip(W_NAMES, args[2 + nw:2 + 2 * nw]))
    v = dict(zip(W_NAMES, args[2 + 2 * nw:2 + 3 * nw]))

    loss, dx, per_layer, stacked, incoming = _step_local(x, target, w, {n: w[n].astype(BF16) for n in SHARDED})
    loss = lax.psum(loss, ("x", "y", "c"))

    out = {}
    kinds = ('grad', 'delta', 'new_m', 'new_v')
    for n in SHARDED:
        shape = w[n].shape
        flat = lambda a: a.reshape(-1, shape[-1])
        parts = [incoming[i][n].reshape(N_CHIP, -1, shape[-1]) for i in range(DEPTH)]
        res = _reduce_adamw(parts, flat(w[n]), flat(m[n]), flat(v[n]), name="adamw_sharded")
        for kind, a in zip(kinds, res):
            out[kind + '_' + n] = a.reshape(shape)
    small = [stacked[n] if n in stacked else jnp.stack([g[n] for g in per_layer]) for n in REPLICATED]
    parts = _all_gather([_pack_small(small)], name="gather_small_grads")[0]
    res = _reduce_adamw([parts], *[_pack_small([d[n] for n in REPLICATED]) for d in (w, m, v)], name="adamw_replicated")
    for kind, packed in zip(kinds, res):
        for n, a in zip(REPLICATED, _unpack_small(packed, [w[n].shape for n in REPLICATED])):
            out[kind + '_' + n] = a
    return (loss, dx[None]) + tuple(out[kind + '_' + n] for kind in kinds for n in W_NAMES)
```

```python
import functools
import math

import jax
import jax.numpy as jnp
import numpy as np
from jax import lax
from jax.experimental import pallas as pl
from jax.experimental.pallas import tpu as pltpu

F32 = jnp.float32
BF16 = jnp.bfloat16

N_DEV = 8
D_MODEL = 1024
DEPTH = 2
GRID_W = 64
D_FF = 2816
NORM_EPS = 1e-6
S5_GROUPS = 32
S5_GROUP_CH = 16
S5_STATE = 64
S5_WIDTH = 512
S5_NSTATE = S5_GROUPS * S5_STATE
S5_LANE_BLOCK = 512
GLA_HEADS = 4
GLA_HEAD_DIM = 128
GLA_WIDTH = 512
GLA_LOWRANK = 16
GLA_TAU = 16.0
GLA_CHUNK = 64
ATTN_Q_HEADS = 8
ATTN_KV_HEADS = 2
ATTN_HEAD_DIM = 64
ATTN_WIDTH = 512
ATTN_KV_WIDTH = 128
ROPE_BASE = 10000.0
IN_SPLITS = (512, 512, 512, 512, 512, 16, 16, 512, 128, 128)
IN_WIDTH = sum(IN_SPLITS)
IN_PAD = 3584
GATE_WIDTH = 3 * D_MODEL
PG_WIDTH = GATE_WIDTH + IN_PAD
P_OFF = GATE_WIDTH
CB_U, CB_GQ, CB_GK, CB_GV, CB_GG, CB_AQ = (P_OFF // 512 + i for i in range(6))
CB_AK, CB_AV, CB_Z = (P_OFF + 3072) // 128, (P_OFF + 3200) // 128, (P_OFF + 3328) // 128
ADAM_LR = 0.001
ADAM_B1 = 0.9
ADAM_B2 = 0.999
ADAM_EPS = 1e-08
ADAM_WD = 0.01
ADAM_STEP = 10

W_NAMES = ['ffn1_norm', 'ffn1_w_gate', 'ffn1_w_up', 'ffn1_w_down', 'mix_norm', 'w_in', 's5_lambda_re', 's5_lambda_im',
           's5_log_dt', 's5_b_re', 's5_b_im', 's5_c_re', 's5_c_im', 's5_d', 's5_w_glu', 'gla_w_alpha', 'gla_b_alpha',
           'gla_norm', 'attn_q_norm', 'attn_k_norm', 'w_branch_s5', 'w_branch_gla', 'w_branch_attn', 'w_merge_gate',
           'b_merge_gate', 'w_out', 'ffn2_norm', 'ffn2_w_gate', 'ffn2_w_up', 'ffn2_w_down', 'final_norm']
SHARD_AXIS = {'ffn1_w_gate': 2, 'ffn1_w_up': 2, 'ffn1_w_down': 1, 'w_in': 2, 's5_w_glu': 1, 'gla_w_alpha': 3,
              'gla_b_alpha': 2, 'w_branch_s5': 2, 'w_branch_gla': 2, 'w_branch_attn': 2, 'w_merge_gate': 2,
              'w_out': 1, 'ffn2_w_gate': 2, 'ffn2_w_up': 2, 'ffn2_w_down': 1}
SHARDED = [n for n in W_NAMES if n in SHARD_AXIS]
REPLICATED = [n for n in W_NAMES if n not in SHARD_AXIS]


def _pick(dim, prefs):
    for p in prefs:
        if dim % p == 0:
            return p
    return dim


def _sigmoid(x):
    return 0.5 * jnp.tanh(0.5 * x) + 0.5


def _mm(a, b, *, ta=False, tb=False, out_dtype=F32, scale=None, add=None, side=None, name):
    a, a_cb, a_w = a if isinstance(a, tuple) else (a, 0, a.shape[1])
    b, b_cb, b_w = b if isinstance(b, tuple) else (b, 0, b.shape[1])
    m, k = (a_w, a.shape[0]) if ta else (a.shape[0], a_w)
    n = b.shape[0] if tb else b_w
    assert (b_w if tb else b.shape[0]) == k, (a.shape, b.shape, ta, tb)
    tm, tn, tk = _mm_tiles(m, n, k, a.dtype.itemsize, b.dtype.itemsize, jnp.dtype(out_dtype).itemsize)
    nk = k // tk
    dims = (((0 if ta else 1,), (1 if tb else 0,)), ((), ()))
    a_off = a_cb * (a_w // (tm if ta else tk))
    b_off = b_cb * (b_w // (tk if tb else tn))

    grid = (m // tm, n // tn, nk)
    n_in = 2 if add is None else 3

    def body(*refs):
        _carried(side, grid, refs, n_in, 1, int(nk > 1), compute)

    def compute(refs):
        a_ref, b_ref, *rest = refs
        add_ref = rest[0] if add is not None else None
        o_ref, *acc = rest[1:] if add is not None else rest

        def finish(res):
            res = res if scale is None else res * scale
            return (res if add_ref is None else res + add_ref[...]).astype(out_dtype)

        part = lax.dot_general(a_ref[...].astype(BF16), b_ref[...].astype(BF16), dims, preferred_element_type=F32)
        if nk == 1:
            o_ref[...] = finish(part)
            return
        acc_ref, = acc
        kk = pl.program_id(2)

        @pl.when(kk == 0)
        def _():
            acc_ref[...] = part

        @pl.when(kk > 0)
        def _():
            acc_ref[...] += part

        @pl.when(kk == nk - 1)
        def _():
            o_ref[...] = finish(acc_ref[...])

    a_spec = (pl.BlockSpec((tk, tm), lambda i, j, kk: (kk, i + a_off)) if ta
              else pl.BlockSpec((tm, tk), lambda i, j, kk: (i, kk + a_off)))
    b_spec = (pl.BlockSpec((tn, tk), lambda i, j, kk: (j, kk + b_off)) if tb
              else pl.BlockSpec((tk, tn), lambda i, j, kk: (kk, j + b_off)))
    o_spec = pl.BlockSpec((tm, tn), lambda i, j, kk: (i, j))
    (out,), gathered = _side_call(
        body, side, name=name, grid=grid, in_specs=[a_spec, b_spec] + ([o_spec] if add is not None else []),
        out_specs=[o_spec], out_shape=[jax.ShapeDtypeStruct((m, n), out_dtype)],
        scratch=[pltpu.VMEM((tm, tn), F32)] if nk > 1 else [], args=[a, b] + ([add] if add is not None else []),
        semantics=("parallel", "parallel", "arbitrary"))
    return out if side is None else (out, gathered)


MM_VMEM_BUDGET = 40 * 1024 * 1024


def _mm_tiles(m, n, k, a_bytes, b_bytes, out_bytes):
    tms = [t for t in (1024, 1408, 512, 256, 128) if m % t == 0] or [m]
    tns = [t for t in (1664, 1408, 512, 256, 128) if n % t == 0] or [n]
    tks = [k] + [t for t in (2048, 1024, 512, 256, 128) if k % t == 0 and t < k]
    for tk in tks:
        for tm in tms:
            for tn in tns:
                use = 2 * (tm * tk * a_bytes + tk * tn * b_bytes + tm * tn * out_bytes) + 2 * tm * tn * 4
                if use <= MM_VMEM_BUDGET:
                    return tm, tn, tk
    return tms[-1], tns[-1], tks[-1]


def _rowmap(fn, rows, consts, outs, reds=(), *, tl, name):
    rows = [r if isinstance(r, tuple) else (r, 0, r.shape[1]) for r in rows]
    length = rows[0][0].shape[0]
    tl = min(tl, length)
    nr, nc, no = len(rows), len(consts), len(outs)

    def body(*refs):
        res = fn(*[r[...] for r in refs[:nr + nc]])
        res = res if isinstance(res, tuple) else (res,)
        for o_ref, val in zip(refs[nr + nc:nr + nc + no], res[:no]):
            o_ref[...] = val.astype(o_ref.dtype)
        if reds:
            step = pl.program_id(0)
            red_refs = refs[nr + nc + no:]

            @pl.when(step == 0)
            def _():
                for d_ref, val in zip(red_refs, res[no:]):
                    d_ref[...] = val.astype(F32)

            @pl.when(step > 0)
            def _():
                for d_ref, val in zip(red_refs, res[no:]):
                    d_ref[...] += val.astype(F32)

    in_specs = [pl.BlockSpec((tl, w), lambda i, cb=cb: (i, cb)) for (_, cb, w) in rows]
    in_specs += [pl.BlockSpec(c.shape, lambda i, nd=c.ndim: (0,) * nd) for c in consts]
    out_specs = [pl.BlockSpec((tl, w), lambda i: (i, 0)) for (w, _) in outs]
    out_specs += [pl.BlockSpec(s, lambda i, nd=len(s): (0,) * nd) for s in reds]
    out_shape = [jax.ShapeDtypeStruct((length, w), dt) for (w, dt) in outs]
    out_shape += [jax.ShapeDtypeStruct(s, F32) for s in reds]
    res = pl.pallas_call(
        body, name=name, grid=(length // tl,), in_specs=in_specs, out_specs=out_specs, out_shape=out_shape,
        compiler_params=pltpu.CompilerParams(dimension_semantics=("arbitrary" if reds else "parallel",)),
    )(*[r[0] for r in rows], *consts)
    return res


def _rms(x):
    return lax.rsqrt(jnp.mean(x * x, axis=-1, keepdims=True) + NORM_EPS)


def _rmsnorm_fwd(x, gain):
    def fn(xv, g):
        return xv * _rms(xv) * g
    return _rowmap(fn, [x], [gain.reshape(1, -1)], [(x.shape[1], BF16)], tl=256, name="rmsnorm_fwd")[0]


def _rmsnorm_bwd(x, gain, dh, dres):
    def fn(xv, dhv, drv, g):
        r = _rms(xv)
        gd = dhv * g
        dx = r * gd - xv * (r * r * r) * jnp.mean(xv * gd, axis=-1, keepdims=True)
        return drv + dx, jnp.sum(dhv * xv * r, axis=0, keepdims=True)
    dx, dg = _rowmap(fn, [x, dh, dres], [gain.reshape(1, -1)], [(x.shape[1], F32)], [(1, x.shape[1])], tl=256,
                     name="rmsnorm_bwd")
    return dx, dg[0]


FFN_UNIT = D_FF // 2


def _side_call(body, side, *, name, grid, in_specs, out_specs, out_shape, scratch, args, semantics):
    if side is not None:
        in_specs, out_specs = in_specs + side.in_specs, out_specs + side.out_specs
        out_shape, scratch, args = out_shape + side.out_shape, scratch + side.scratch, list(args) + side.blocks
        semantics = ("arbitrary",) * len(grid)
    res = pl.pallas_call(body, name=name, grid=grid, in_specs=in_specs, out_specs=out_specs, out_shape=out_shape,
                         scratch_shapes=scratch, compiler_params=pltpu.CompilerParams(dimension_semantics=semantics))(*args)
    n_own = len(res) - (side.n if side is not None else 0)
    return res[:n_own], res[n_own:]


def _ffn_up(h, w_gate, w_up, side=None):
    length, k = h.shape
    tm = _pick(length, (512, 256, 128))
    grid = (D_FF // FFN_UNIT, length // tm)

    def compute(refs):
        h_ref, wg_ref, wu_ref, a_ref, g_ref, u_ref = refs
        hv = h_ref[...]
        g = jnp.dot(hv, wg_ref[...], preferred_element_type=F32)
        u = jnp.dot(hv, wu_ref[...], preferred_element_type=F32)
        a_ref[...] = (g * _sigmoid(g) * u).astype(BF16)
        g_ref[...] = g.astype(BF16)
        u_ref[...] = u.astype(BF16)

    def body(*refs):
        _carried(side, grid, refs, 3, 3, 0, compute)

    w_spec = pl.BlockSpec((k, FFN_UNIT), lambda j, i: (0, j))
    o_spec = pl.BlockSpec((tm, FFN_UNIT), lambda j, i: (i, j))
    return _side_call(
        body, side, name="ffn_up", grid=grid, in_specs=[pl.BlockSpec((tm, k), lambda j, i: (i, 0)), w_spec, w_spec],
        out_specs=[o_spec] * 3, out_shape=[jax.ShapeDtypeStruct((length, D_FF), BF16)] * 3, scratch=[],
        args=[h, w_gate, w_up], semantics=("parallel", "parallel"))


def _ffn_dgu(dxo, w_down, g, u):
    length, k = dxo.shape
    tm = _pick(length, (512, 256, 128))

    def body(d_ref, w_ref, g_ref, u_ref, dg_ref, du_ref):
        da = 0.5 * lax.dot_general(d_ref[...], w_ref[...], _NT, preferred_element_type=F32)
        gv = g_ref[...].astype(F32)
        s = _sigmoid(gv)
        dg_ref[...] = (da * u_ref[...].astype(F32) * (s * (1.0 + gv * (1.0 - s)))).astype(BF16)
        du_ref[...] = (da * (gv * s)).astype(BF16)

    o_spec = pl.BlockSpec((tm, FFN_UNIT), lambda j, i: (i, j))
    return pl.pallas_call(
        body, name="ffn_dgu", grid=(D_FF // FFN_UNIT, length // tm),
        in_specs=[pl.BlockSpec((tm, k), lambda j, i: (i, 0)), pl.BlockSpec((FFN_UNIT, k), lambda j, i: (j, 0)), o_spec, o_spec],
        out_specs=[o_spec] * 2, out_shape=[jax.ShapeDtypeStruct((length, D_FF), BF16)] * 2,
        compiler_params=pltpu.CompilerParams(dimension_semantics=("parallel", "parallel")),
    )(dxo, w_down, g, u)


def _ffn_fwd(x, gain, w_gate, w_up, w_down, side=None):
    h = _rmsnorm_fwd(x, gain)
    (a, g, u), gathered = _ffn_up(h, w_gate, w_up, side)
    x_out = _mm(a, w_down, scale=0.5, add=x, name="ffn_down")
    return x_out, (x, h, g, u, a), gathered


def _ffn_bwd(saved, gain, w_gate, w_up, w_down, dx_out):
    x, h, g, u, a = saved
    dxo = dx_out.astype(BF16)
    d_wdown = _mm(a, dxo, ta=True, scale=0.5, out_dtype=BF16, name="ffn_dwdown")
    dg, du = _ffn_dgu(dxo, w_down, g, u)
    d_wgate = _mm(h, dg, ta=True, out_dtype=BF16, name="ffn_dwgu")
    d_wup = _mm(h, du, ta=True, out_dtype=BF16, name="ffn_dwgu")
    dh = _mm(du, w_up, tb=True, add=_mm(dg, w_gate, tb=True, name="ffn_dh"), name="ffn_dh_add")
    dx, dgain = _rmsnorm_bwd(x, gain, dh, dx_out)
    return dx, dgain, d_wgate, d_wup, d_wdown


def _s5_blocked(re, im):
    lead = re.shape[:-1]
    nb = S5_NSTATE // S5_LANE_BLOCK
    both = jnp.stack([re.reshape(*lead, nb, S5_LANE_BLOCK), im.reshape(*lead, nb, S5_LANE_BLOCK)], axis=-2)
    return both.reshape(*lead, 2 * S5_NSTATE)


def _s5_unblocked(z):
    lead = z.shape[:-1]
    nb = S5_NSTATE // S5_LANE_BLOCK
    both = z.reshape(*lead, nb, 2, S5_LANE_BLOCK)
    return both[..., 0, :].reshape(*lead, S5_NSTATE), both[..., 1, :].reshape(*lead, S5_NSTATE)


def _s5_tables(a_re, a_im, reverse):
    a = lax.complex(a_re, a_im)
    a2 = a * a
    a4 = a2 * a2
    rows = jnp.arange(8)
    pw = [a]
    for _ in range(7):
        pw.append(pw[-1] * a)
    pw = jnp.stack(pw)
    if reverse:
        pw = pw[::-1]
    tabs = []
    for coef, s in ((a, 1), (a2, 2), (a4, 4)):
        live = (rows <= 7 - s) if reverse else (rows >= s)
        tabs.append(jnp.where(live[:, None], coef[None, :], 0.0))
    tabs.append(pw)
    tabs = jnp.stack(tabs)
    return _s5_blocked(jnp.real(tabs), jnp.imag(tabs))


def _s5_scan_tile(v, tab_ref, prev, reverse):
    lb = S5_LANE_BLOCK
    vr, vi = v[:, :lb], v[:, lb:]
    for idx, s in enumerate((1, 2, 4)):
        cr, ci = tab_ref[idx, :, :lb], tab_ref[idx, :, lb:]
        sh = 8 - s if reverse else s
        sr, si = pltpu.roll(vr, sh, 0), pltpu.roll(vi, sh, 0)
        vr, vi = vr + cr * sr - ci * si, vi + cr * si + ci * sr
    row = 0 if reverse else 7
    pr = jnp.broadcast_to(prev[row:row + 1, :lb], (8, lb))
    pi = jnp.broadcast_to(prev[row:row + 1, lb:], (8, lb))
    cr, ci = tab_ref[3, :, :lb], tab_ref[3, :, lb:]
    return jnp.concatenate([vr + cr * pr - ci * pi, vi + cr * pi + ci * pr], axis=1)


def _s5_prep(lam_re, lam_im, log_dt, b_re, b_im):
    lam = lax.complex(lam_re, lam_im)
    dt = jnp.exp(log_dt)[:, None]
    lam_bar = jnp.exp(lam * dt)
    b_bar = ((lam_bar - 1.0) / lam)[..., None] * lax.complex(b_re, b_im)
    return (jnp.real(lam_bar).reshape(-1), jnp.imag(lam_bar).reshape(-1), jnp.real(b_bar), jnp.imag(b_bar))


S5_NBLK = S5_NSTATE // S5_LANE_BLOCK
S5_BLK_GROUPS = S5_GROUPS // S5_NBLK
S5_BLK_CH = S5_BLK_GROUPS * S5_GROUP_CH


def _s5_in_matrix(bb_re, bb_im):
    eye = jnp.eye(S5_BLK_GROUPS, dtype=F32)
    def dense(bb):
        b4 = bb.reshape(S5_NBLK, S5_BLK_GROUPS, S5_STATE, S5_GROUP_CH)
        return jnp.einsum('cgph,gk->cghkp', b4, eye).reshape(S5_NBLK, S5_BLK_CH, S5_LANE_BLOCK)
    return jnp.concatenate([dense(bb_re), dense(bb_im)], axis=-1)


def _s5_block_diagonal(d):
    d5 = d.reshape(S5_NBLK, S5_BLK_GROUPS, S5_GROUP_CH, S5_BLK_GROUPS, S5_STATE)
    eye = jnp.eye(S5_BLK_GROUPS, dtype=F32)
    return jnp.swapaxes(jnp.sum(d5 * eye[None, :, None, :, None], axis=1), 1, 2)


def _s5_in_matrix_grad(d_mat):
    def diag(d):
        return jnp.swapaxes(_s5_block_diagonal(d), 2, 3).reshape(S5_GROUPS, S5_STATE, S5_GROUP_CH)
    return diag(d_mat[..., :S5_LANE_BLOCK]), diag(d_mat[..., S5_LANE_BLOCK:])


def _s5_out_matrix(c_re, c_im):
    eye = jnp.eye(S5_BLK_GROUPS, dtype=F32)
    def dense(cc):
        c4 = cc.reshape(S5_NBLK, S5_BLK_GROUPS, S5_GROUP_CH, S5_STATE)
        return jnp.einsum('cghp,gk->cgpkh', c4, eye).reshape(S5_NBLK, S5_LANE_BLOCK, S5_BLK_CH)
    return jnp.concatenate([dense(c_re), dense(-c_im)], axis=1)


def _s5_out_matrix_grad(d_mat_t):
    def diag(d):
        return _s5_block_diagonal(d).reshape(S5_GROUPS, S5_GROUP_CH, S5_STATE)
    return diag(d_mat_t[..., :S5_LANE_BLOCK]), -diag(d_mat_t[..., S5_LANE_BLOCK:])


def _gelu_parts(x):
    k = math.sqrt(2.0 / math.pi)
    inner = k * (x + 0.044715 * x * x * x)
    th = jnp.tanh(inner)
    return th, k * (1.0 + 3.0 * 0.044715 * x * x)


S5_CB_U = CB_U * (512 // S5_BLK_CH)


def _s5_direction_fwd(pg, b_mat, c_mat, tabs, reverse, *, name):
    length = pg.shape[0]
    tb = min(512, length)
    ntb = length // tb
    wb = 2 * S5_LANE_BLOCK
    ntile = tb // 8

    def body(tab_ref, u_ref, b_ref, c_ref, x_ref, y_ref, ends_ref, carry_ref, bu_ref):
        @pl.when(pl.program_id(1) == 0)
        def _():
            carry_ref[...] = jnp.zeros_like(carry_ref)

        bu_ref[...] = jnp.dot(u_ref[...].astype(BF16), b_ref[0], preferred_element_type=F32)

        def step(i, prev):
            r0 = pl.multiple_of((ntile - 1 - i if reverse else i) * 8, 8)
            x = _s5_scan_tile(bu_ref[pl.ds(r0, 8), :], tab_ref, prev, reverse)
            x_ref[pl.ds(r0, 8), :] = x
            return x

        carry_ref[...] = lax.fori_loop(0, ntile, step, carry_ref[...])
        y_ref[...] = jnp.dot(x_ref[...].astype(BF16), c_ref[0], preferred_element_type=F32)
        ends_ref[0, 0:8, :] = x_ref[0:8, :]
        ends_ref[0, 8:16, :] = x_ref[tb - 8:tb, :]

    tix = (lambda t: ntb - 1 - t) if reverse else (lambda t: t)
    return pl.pallas_call(
        body, name=name, grid=(S5_NBLK, ntb),
        in_specs=[pl.BlockSpec((4, 8, wb), lambda c, t: (0, 0, c)),
                  pl.BlockSpec((tb, S5_BLK_CH), lambda c, t: (tix(t), S5_CB_U + c)),
                  pl.BlockSpec((1, S5_BLK_CH, wb), lambda c, t: (c, 0, 0)),
                  pl.BlockSpec((1, wb, S5_BLK_CH), lambda c, t: (c, 0, 0))],
        out_specs=[pl.BlockSpec((tb, wb), lambda c, t: (tix(t), c)), pl.BlockSpec((tb, S5_BLK_CH), lambda c, t: (tix(t), c)),
                   pl.BlockSpec((1, 16, wb), lambda c, t: (tix(t), 0, c))],
        out_shape=[jax.ShapeDtypeStruct((length, S5_NBLK * wb), F32), jax.ShapeDtypeStruct((length, S5_WIDTH), F32),
                   jax.ShapeDtypeStruct((ntb, 16, S5_NBLK * wb), F32)],
        scratch_shapes=[pltpu.VMEM((8, wb), F32), pltpu.VMEM((tb, wb), F32)],
        compiler_params=pltpu.CompilerParams(dimension_semantics=("parallel", "arbitrary")),
    )(tabs, pg, b_mat, c_mat)


def _s5_direction_bwd(pg, dy, xs, ends, b_mat, c_mat, tabs_conj, reverse, *, name):
    length = pg.shape[0]
    tb = min(512, length)
    ntb = length // tb
    lb = S5_LANE_BLOCK
    wb = 2 * lb
    ntile = tb // 8
    adj_rev = not reverse
    if reverse:
        edge = jnp.concatenate([ends[1:, 0], jnp.zeros((1, xs.shape[1]), F32)], axis=0)
    else:
        edge = jnp.concatenate([jnp.zeros((1, xs.shape[1]), F32), ends[:-1, 15]], axis=0)
    edge = edge.reshape(ntb, 1, xs.shape[1])

    def body(tab_ref, u_ref, dy_ref, x_ref, edge_ref, b_ref, c_ref, du_ref, db_ref, dc_ref, da_ref, carry_ref, g_ref, lam_ref):
        @pl.when(pl.program_id(1) == 0)
        def _():
            carry_ref[...] = jnp.zeros_like(carry_ref)
            da_ref[...] = jnp.zeros_like(da_ref)
            db_ref[...] = jnp.zeros_like(db_ref)
            dc_ref[...] = jnp.zeros_like(dc_ref)

        dyb = dy_ref[...].astype(BF16)
        g_ref[...] = lax.dot_general(dyb, c_ref[0], _NT, preferred_element_type=F32)
        rows = lax.broadcasted_iota(jnp.int32, (8, wb), 0)

        def step(i, carry):
            prev, acc = carry
            k = ntile - 1 - i if adj_rev else i
            r0 = pl.multiple_of(k * 8, 8)
            lam = _s5_scan_tile(g_ref[pl.ds(r0, 8), :], tab_ref, prev, adj_rev)
            lam_ref[pl.ds(r0, 8), :] = lam
            x = x_ref[pl.ds(r0, 8), :]
            if reverse:
                kn = jnp.minimum(k + 1, ntile - 1)
                nb = x_ref[pl.ds(pl.multiple_of(kn * 8, 8), 8), :][0:1, :]
                nb = jnp.where(k == ntile - 1, edge_ref[0], nb)
                xp = jnp.where(rows == 7, jnp.broadcast_to(nb, (8, wb)), pltpu.roll(x, 7, 0))
            else:
                kn = jnp.maximum(k - 1, 0)
                nb = x_ref[pl.ds(pl.multiple_of(kn * 8, 8), 8), :][7:8, :]
                nb = jnp.where(k == 0, edge_ref[0], nb)
                xp = jnp.where(rows == 0, jnp.broadcast_to(nb, (8, wb)), pltpu.roll(x, 1, 0))
            xr, xi, lr, li = xp[:, :lb], xp[:, lb:], lam[:, :lb], lam[:, lb:]
            return lam, acc + jnp.concatenate([xr * lr + xi * li, xr * li - xi * lr], axis=1)

        last, acc = lax.fori_loop(0, ntile, step, (carry_ref[...], da_ref[...]))
        carry_ref[...] = last
        da_ref[...] = acc
        lamb = lam_ref[...].astype(BF16)
        du_ref[...] = lax.dot_general(lamb, b_ref[0], _NT, preferred_element_type=F32)
        db_ref[0] += lax.dot_general(u_ref[...].astype(BF16), lamb, _TN, preferred_element_type=F32)
        dc_ref[0] += lax.dot_general(dyb, x_ref[...].astype(BF16), _TN, preferred_element_type=F32)

    tix = (lambda t: ntb - 1 - t) if adj_rev else (lambda t: t)
    wide = pl.BlockSpec((tb, wb), lambda c, t: (tix(t), c))
    mat = pl.BlockSpec((1, S5_BLK_CH, wb), lambda c, t: (c, 0, 0))
    return pl.pallas_call(
        body, name=name, grid=(S5_NBLK, ntb),
        in_specs=[pl.BlockSpec((4, 8, wb), lambda c, t: (0, 0, c)),
                  pl.BlockSpec((tb, S5_BLK_CH), lambda c, t: (tix(t), S5_CB_U + c)),
                  pl.BlockSpec((tb, S5_BLK_CH), lambda c, t: (tix(t), c)), wide,
                  pl.BlockSpec((1, 1, wb), lambda c, t: (tix(t), 0, c)), mat,
                  pl.BlockSpec((1, wb, S5_BLK_CH), lambda c, t: (c, 0, 0))],
        out_specs=[pl.BlockSpec((tb, S5_BLK_CH), lambda c, t: (tix(t), c)), mat, mat, pl.BlockSpec((8, wb), lambda c, t: (0, c))],
        out_shape=[jax.ShapeDtypeStruct((length, S5_WIDTH), F32), jax.ShapeDtypeStruct((S5_NBLK, S5_BLK_CH, wb), F32),
                   jax.ShapeDtypeStruct((S5_NBLK, S5_BLK_CH, wb), F32), jax.ShapeDtypeStruct((8, S5_NBLK * wb), F32)],
        scratch_shapes=[pltpu.VMEM((8, wb), F32), pltpu.VMEM((tb, wb), F32), pltpu.VMEM((tb, wb), F32)],
        compiler_params=pltpu.CompilerParams(dimension_semantics=("parallel", "arbitrary")),
    )(tabs_conj, pg, dy, xs, edge, b_mat, c_mat)


def _both(fn):
    return jax.vmap(jax.vmap(fn))


def _s5_setup(w):
    a_re, a_im, bb_re, bb_im = _both(_s5_prep)(w['s5_lambda_re'], w['s5_lambda_im'], w['s5_log_dt'], w['s5_b_re'], w['s5_b_im'])

    def tables(d, conj, reverse):
        return jax.vmap(lambda r, i: _s5_tables(r, -i if conj else i, reverse))(a_re[:, d], a_im[:, d])
    return {'b_mat': _both(_s5_in_matrix)(bb_re, bb_im).astype(BF16),
            'c_mat': _both(_s5_out_matrix)(w['s5_c_re'], w['s5_c_im']).astype(BF16),
            'tabs': [tables(0, False, False), tables(1, False, True)],
            'tabs_adj': [tables(0, True, True), tables(1, True, False)]}


def _s5_param_grads(w, raws):
    def stacked(k):
        return jnp.stack([jnp.stack([raws[i][d][k] for d in range(2)]) for i in range(DEPTH)])
    dbb_re, dbb_im = _both(_s5_in_matrix_grad)(stacked(0))
    dc_re, dc_im = _both(_s5_out_matrix_grad)(stacked(1))
    da_re, da_im = _s5_unblocked(jnp.sum(stacked(2), axis=2))
    _, vjp = jax.vjp(_both(_s5_prep), w['s5_lambda_re'], w['s5_lambda_im'], w['s5_log_dt'], w['s5_b_re'], w['s5_b_im'])
    g = vjp((da_re, da_im, dbb_re, dbb_im))
    return {'s5_lambda_re': g[0], 's5_lambda_im': g[1], 's5_log_dt': g[2], 's5_b_re': g[3], 's5_b_im': g[4],
            's5_c_re': dc_re, 's5_c_im': dc_im}


def _s5_fwd(p_in, prm, w_glu):
    dirs = []
    ys = []
    for d, reverse in ((0, False), (1, True)):
        xs, y_dir, ends = _s5_direction_fwd(p_in, prm['b_mat'][d], prm['c_mat'][d], prm['tabs'][d], reverse,
                                            name="s5_fwd_rev" if reverse else "s5_fwd")
        ys.append(y_dir)
        dirs.append((xs, ends))

    def post(yf, yb, u, dskip):
        ypre = yf + yb + dskip * u
        th, _ = _gelu_parts(ypre)
        return ypre, 0.5 * ypre * (1.0 + th)
    ypre, yg = _rowmap(post, [ys[0], ys[1], (p_in, CB_U, S5_WIDTH)], [prm['d'].reshape(1, -1)],
                       [(S5_WIDTH, F32), (S5_WIDTH, F32)], tl=512, name="s5_post")
    t = _mm(yg, w_glu, name="s5_glu_mm")

    def glu(ygv, tv):
        return ygv * _sigmoid(tv)
    y = _rowmap(glu, [yg, t], [], [(S5_WIDTH, BF16)], tl=512, name="s5_glu")[0]
    return y, (dirs, ypre, yg, t)


def _s5_bwd(pg, prm, w_glu, saved, dy):
    dirs, ypre, yg, t = saved

    def glu_bwd(dyv, ygv, tv):
        s = _sigmoid(tv)
        return dyv * ygv * s * (1.0 - s), dyv * s
    dt, dyg_direct = _rowmap(glu_bwd, [dy, yg, t], [], [(S5_WIDTH, BF16), (S5_WIDTH, F32)], tl=512, name="s5_glu_bwd")
    grads = {'w_glu': _mm(yg, dt, ta=True, out_dtype=BF16, name="s5_dwglu")}
    dyg_mm = _mm(dt, w_glu, tb=True, name="s5_dyg")

    def post_bwd(dyd, dym, yp, u, dskip):
        th, dinner = _gelu_parts(yp)
        dyp = (dyd + dym) * (0.5 * (1.0 + th) + 0.5 * yp * (1.0 - th * th) * dinner)
        return dyp, dyp * dskip, jnp.sum(dyp * u, axis=0, keepdims=True)
    dyp, du_skip, dd = _rowmap(post_bwd, [dyg_direct, dyg_mm, ypre, (pg, CB_U, S5_WIDTH)], [prm['d'].reshape(1, -1)],
                               [(S5_WIDTH, F32), (S5_WIDTH, F32)], [(1, S5_WIDTH)], tl=512, name="s5_post_bwd")
    grads['d'] = dd[0]
    du = [du_skip]
    grads['raw'] = []
    for d, reverse in ((0, False), (1, True)):
        du_dir, d_bmat, d_cmat_t, da = _s5_direction_bwd(pg, dyp, *dirs[d], prm['b_mat'][d], prm['c_mat'][d], prm['tabs_adj'][d],
                                                         reverse, name="s5_bwd_rev" if reverse else "s5_bwd")
        du.append(du_dir)
        grads['raw'].append((d_bmat, d_cmat_t, da))
    return du, grads


def _split3(x):
    hi = x.astype(BF16)
    r = x - hi.astype(F32)
    mid = r.astype(BF16)
    return hi, mid, (r - mid.astype(F32)).astype(BF16)


def _exact_dot(ones, x, dims):
    parts = [lax.dot_general(ones, p, dims, preferred_element_type=F32) for p in _split3(x)]
    return parts[0] + parts[1] + parts[2]


_NN = (((1,), (0,)), ((), ()))
_NT = (((1,), (1,)), ((), ()))
_TN = (((0,), (0,)), ((), ()))


def _dot(a, b, dims=_NN):
    return lax.dot_general(a.astype(BF16), b.astype(BF16), dims, preferred_element_type=F32)


def _gla_chunk_mask(reverse):
    rows = lax.broadcasted_iota(jnp.int32, (GLA_CHUNK, GLA_CHUNK), 0)
    cols = lax.broadcasted_iota(jnp.int32, (GLA_CHUNK, GLA_CHUNK), 1)
    return (cols >= rows) if reverse else (cols <= rows)


def _gla_fwd(pg, la, reverse, *, name):
    length = la.shape[0]
    nch = length // GLA_CHUNK
    scale = GLA_HEAD_DIM ** -0.5
    last = 0 if reverse else GLA_CHUNK - 1
    hd = GLA_HEAD_DIM

    def body(q_ref, k_ref, v_ref, la_ref, o_ref, sp_ref, st_ref):
        @pl.when(pl.program_id(0) == 0)
        def _():
            st_ref[...] = jnp.zeros_like(st_ref)

        mask = _gla_chunk_mask(reverse)
        b = _exact_dot(mask.astype(BF16), la_ref[...], _NN)
        sp_ref[0] = st_ref[...]
        outs = []
        for h in range(GLA_HEADS):
            sl = slice(h * hd, (h + 1) * hd)
            bh = b[:, sl]
            bl = bh[last:last + 1, :]
            k = k_ref[:, sl]
            v = v_ref[:, sl]
            qd = q_ref[:, sl] * scale * jnp.exp(bh)
            kd = k * jnp.exp(-bh)
            ke = k * jnp.exp(bl - bh)
            st = st_ref[sl, :]
            p = jnp.where(mask, _dot(qd, kd, _NT), 0.0)
            outs.append(_dot(p, v) + _dot(qd, st, _NT))
            st_ref[sl, :] = st * jnp.exp(bl) + _dot(v, ke, _TN)
        o_ref[...] = jnp.concatenate(outs, axis=1)

    cmap = (lambda n: nch - 1 - n) if reverse else (lambda n: n)
    col = lambda cb: pl.BlockSpec((GLA_CHUNK, GLA_WIDTH), lambda n, cb=cb: (cmap(n), cb))
    return pl.pallas_call(
        body, name=name, grid=(nch,),
        in_specs=[col(CB_GQ), col(CB_GK), col(CB_GV), col(0)],
        out_specs=[col(0), pl.BlockSpec((1, GLA_WIDTH, hd), lambda n: (cmap(n), 0, 0))],
        out_shape=[jax.ShapeDtypeStruct((length, GLA_WIDTH), F32), jax.ShapeDtypeStruct((nch, GLA_WIDTH, hd), F32)],
        scratch_shapes=[pltpu.VMEM((GLA_WIDTH, hd), F32)],
        compiler_params=pltpu.CompilerParams(dimension_semantics=("arbitrary",)),
    )(pg, pg, pg, la)


def _gla_bwd(pg, la, do, sprev, reverse, *, name):
    length = la.shape[0]
    nch = length // GLA_CHUNK
    scale = GLA_HEAD_DIM ** -0.5
    last = 0 if reverse else GLA_CHUNK - 1
    hd = GLA_HEAD_DIM

    def body(q_ref, k_ref, v_ref, la_ref, do_ref, sp_ref, dq_ref, dk_ref, dv_ref, dla_ref, dst_ref):
        @pl.when(pl.program_id(0) == 0)
        def _():
            dst_ref[...] = jnp.zeros_like(dst_ref)

        mask = _gla_chunk_mask(reverse)
        tri = mask.astype(BF16)
        b = _exact_dot(tri, la_ref[...], _NN)
        is_last = lax.broadcasted_iota(jnp.int32, (GLA_CHUNK, hd), 0) == last
        dqs, dks, dvs, dbs = [], [], [], []
        for h in range(GLA_HEADS):
            sl = slice(h * hd, (h + 1) * hd)
            bh = b[:, sl]
            bl = bh[last:last + 1, :]
            eb, enb, ebl, el = jnp.exp(bh), jnp.exp(-bh), jnp.exp(bl - bh), jnp.exp(bl)
            k = k_ref[:, sl]
            v = v_ref[:, sl]
            dov = do_ref[:, sl]
            qd = q_ref[:, sl] * scale * eb
            kd = k * enb
            ke = k * ebl
            st = sp_ref[0, sl, :]
            dst = dst_ref[sl, :]
            p = jnp.where(mask, _dot(qd, kd, _NT), 0.0)
            dp = jnp.where(mask, _dot(dov, v, _NT), 0.0)
            dqd = _dot(dp, kd) + _dot(dov, st)
            dkd = _dot(dp, qd, _TN)
            dvs.append(_dot(p, dov, _TN) + _dot(ke, dst, _NT))
            dke = _dot(v, dst)
            dst_ref[sl, :] = dst * el + _dot(dov, qd, _TN)
            dbl = el * jnp.sum(dst * st, axis=0, keepdims=True) + jnp.sum(dke * ke, axis=0, keepdims=True)
            db = dqd * qd - dkd * kd - dke * ke
            dbs.append(jnp.where(is_last, db + dbl, db))
            dqs.append(dqd * eb * scale)
            dks.append(dkd * enb + dke * ebl)
        dq_ref[...] = jnp.concatenate(dqs, axis=1)
        dk_ref[...] = jnp.concatenate(dks, axis=1)
        dv_ref[...] = jnp.concatenate(dvs, axis=1)
        tri_t = _gla_chunk_mask(not reverse).astype(BF16)
        dla_ref[...] = _exact_dot(tri_t, jnp.concatenate(dbs, axis=1), _NN)

    cmap = (lambda n: n) if reverse else (lambda n: nch - 1 - n)
    col = lambda cb: pl.BlockSpec((GLA_CHUNK, GLA_WIDTH), lambda n, cb=cb: (cmap(n), cb))
    wide = jax.ShapeDtypeStruct((length, GLA_WIDTH), F32)
    return pl.pallas_call(
        body, name=name, grid=(nch,),
        in_specs=[col(CB_GQ), col(CB_GK), col(CB_GV), col(0), col(0),
                  pl.BlockSpec((1, GLA_WIDTH, hd), lambda n: (cmap(n), 0, 0))],
        out_specs=[col(0)] * 4, out_shape=[wide] * 4,
        scratch_shapes=[pltpu.VMEM((GLA_WIDTH, hd), F32)],
        compiler_params=pltpu.CompilerParams(dimension_semantics=("arbitrary",)),
    )(pg, pg, pg, la, do, sprev)


def _log_sigmoid(x):
    return jnp.minimum(x, 0.0) - jnp.log(1.0 + jnp.exp(-jnp.abs(x)))


def _gla_alpha_padded(w_alpha):
    w = jnp.zeros((2, 128, GLA_WIDTH), w_alpha.dtype)
    w = w.at[0, 0:GLA_LOWRANK].set(w_alpha[0])
    return w.at[1, GLA_LOWRANK:2 * GLA_LOWRANK].set(w_alpha[1])


def _gla_branch_fwd(pg, w_alpha, b_alpha, norm_gain):
    wa = _gla_alpha_padded(w_alpha).astype(BF16)

    def gates(z, w, bias):
        return (_log_sigmoid(_dot(z, w[0]) + bias[0:1]) / GLA_TAU, _log_sigmoid(_dot(z, w[1]) + bias[1:2]) / GLA_TAU)
    la_f, la_b = _rowmap(gates, [(pg, CB_Z, 128)], [wa, b_alpha], [(GLA_WIDTH, F32), (GLA_WIDTH, F32)], tl=512,
                         name="gla_gates")
    o_f, sp_f = _gla_fwd(pg, la_f, False, name="gla_fwd")
    o_b, sp_b = _gla_fwd(pg, la_b, True, name="gla_fwd_rev")

    def post(of, ob, gate, gn):
        o = of + ob
        on = jnp.concatenate([o[:, s:s + GLA_HEAD_DIM] * _rms(o[:, s:s + GLA_HEAD_DIM]) * gn
                              for s in range(0, GLA_WIDTH, GLA_HEAD_DIM)], axis=1)
        return o, on * (gate * _sigmoid(gate))
    o, y = _rowmap(post, [o_f, o_b, (pg, CB_GG, GLA_WIDTH)], [norm_gain.reshape(1, -1)],
                   [(GLA_WIDTH, F32), (GLA_WIDTH, BF16)], tl=512, name="gla_post")
    return y, (wa, la_f, la_b, sp_f, sp_b, o)


def _gla_branch_bwd(pg, w_alpha, b_alpha, norm_gain, saved, dy):
    wa, la_f, la_b, sp_f, sp_b, o = saved

    def post_bwd(dyv, ov, gate, gn):
        s = _sigmoid(gate)
        dos, dgn, ons = [], [], []
        for c in range(0, GLA_WIDTH, GLA_HEAD_DIM):
            oh = ov[:, c:c + GLA_HEAD_DIM]
            r = _rms(oh)
            don = dyv[:, c:c + GLA_HEAD_DIM] * (gate[:, c:c + GLA_HEAD_DIM] * s[:, c:c + GLA_HEAD_DIM])
            gd = don * gn
            dos.append(r * gd - oh * (r * r * r) * jnp.mean(oh * gd, axis=-1, keepdims=True))
            dgn.append(jnp.sum(don * oh * r, axis=0, keepdims=True))
            ons.append(oh * r * gn)
        on = jnp.concatenate(ons, axis=1)
        dgate = dyv * on * (s * (1.0 + gate * (1.0 - s)))
        return jnp.concatenate(dos, axis=1), dgate, jnp.concatenate(dgn, axis=1)
    do, dgate, dgn = _rowmap(post_bwd, [dy, o, (pg, CB_GG, GLA_WIDTH)], [norm_gain.reshape(1, -1)],
                             [(GLA_WIDTH, F32), (GLA_WIDTH, F32)], [(1, GLA_WIDTH)], tl=512, name="gla_post_bwd")
    dq_f, dk_f, dv_f, dla_f = _gla_bwd(pg, la_f, do, sp_f, False, name="gla_bwd")
    dq_b, dk_b, dv_b, dla_b = _gla_bwd(pg, la_b, do, sp_b, True, name="gla_bwd_rev")

    def gates_bwd(z, dlf, dlb, w, bias):
        dz = jnp.zeros_like(z)
        dlogits, dbs = [], []
        for d, dl in ((0, dlf), (1, dlb)):
            logit = _dot(z, w[d]) + bias[d:d + 1]
            dlogit = dl * (1.0 / GLA_TAU) * jnp.exp(_log_sigmoid(-logit))
            dz = dz + _dot(dlogit, w[d], _NT)
            dlogits.append(dlogit)
            dbs.append(jnp.sum(dlogit, axis=0, keepdims=True))
        return dz, dlogits[0], dlogits[1], dbs[0], dbs[1]
    dz, dlg_f, dlg_b, dba_f, dba_b = _rowmap(
        gates_bwd, [(pg, CB_Z, 128), dla_f, dla_b], [wa, b_alpha], [(128, F32), (GLA_WIDTH, BF16), (GLA_WIDTH, BF16)],
        [(1, GLA_WIDTH), (1, GLA_WIDTH)], tl=512, name="gla_gates_bwd")
    dwa_f = _mm(dlg_f, (pg, CB_Z, 128), ta=True, name="gla_dwalpha")
    dwa_b = _mm(dlg_b, (pg, CB_Z, 128), ta=True, name="gla_dwalpha")
    grads = {'w_alpha': jnp.stack([dwa_f[:, 0:GLA_LOWRANK].T, dwa_b[:, GLA_LOWRANK:2 * GLA_LOWRANK].T]),
             'b_alpha': jnp.concatenate([dba_f, dba_b], axis=0),
             'norm': jnp.sum(dgn.reshape(GLA_HEADS, GLA_HEAD_DIM), axis=0)}
    return [dq_f, dq_b], [dk_f, dk_b], [dv_f, dv_b], dgate, dz, grads


def _rope_tables(length):
    half = ATTN_HEAD_DIM // 2
    inv_freq = ROPE_BASE ** (-jnp.arange(half // 2, dtype=F32) * 2.0 / half)
    t = jnp.arange(length, dtype=jnp.int32)
    def one(pos):
        ang = pos.astype(F32)[:, None] * inv_freq[None, :]
        c, s = jnp.cos(ang), jnp.sin(ang)
        return jnp.concatenate([c, c], axis=1), jnp.concatenate([-s, s], axis=1)
    c_r, s_r = one(t // GRID_W)
    c_c, s_c = one(t % GRID_W)
    return jnp.concatenate([c_r, c_c], axis=1), jnp.concatenate([s_r, s_c], axis=1)


def _rope_swap(y):
    w = y.shape[1]
    lane = lax.broadcasted_iota(jnp.int32, y.shape, 1)
    return jnp.where(lane % 32 < 16, pltpu.roll(y, w - 16, 1), pltpu.roll(y, 16, 1))


def _head_sums(x, ones):
    parts = [lax.dot_general(p, ones, _NN, preferred_element_type=F32) for p in _split3(x)]
    return parts[0] + parts[1] + parts[2]


def _head_ones(width):
    seg = np.arange(width) // ATTN_HEAD_DIM
    return jnp.asarray(seg[:, None] == seg[None, :], BF16)


def _qk_prep_fwd(pg, cb, width, gain, cos, sin, scale, *, name):
    heads = width // ATTN_HEAD_DIM
    def fn(x, c, s, g, ones):
        r = lax.rsqrt(_head_sums(x * x, ones) * (1.0 / ATTN_HEAD_DIM) + NORM_EPS)
        y = x * r * g
        return (y * c + _rope_swap(y) * s) * scale
    return _rowmap(fn, [(pg, cb, width), jnp.tile(cos, (1, heads)), jnp.tile(sin, (1, heads))],
                   [jnp.tile(gain, heads).reshape(1, -1), _head_ones(width)], [(width, BF16)], tl=512, name=name)[0]


def _qk_prep_bwd(pg, cb, width, gain, cos, sin, scale, dout, *, name):
    heads = width // ATTN_HEAD_DIM
    def fn(x, dov, c, s, g, ones):
        r = lax.rsqrt(_head_sums(x * x, ones) * (1.0 / ATTN_HEAD_DIM) + NORM_EPS)
        dos = dov * scale
        dy = dos * c + _rope_swap(dos * s)
        gd = dy * g
        dx = r * gd - x * (r * r * r) * (_head_sums(x * gd, ones) * (1.0 / ATTN_HEAD_DIM))
        return dx, jnp.sum(dy * x * r, axis=0, keepdims=True)
    dx, dg = _rowmap(fn, [(pg, cb, width), dout, jnp.tile(cos, (1, heads)), jnp.tile(sin, (1, heads))],
                     [jnp.tile(gain, heads).reshape(1, -1), _head_ones(width)], [(width, F32)], [(1, width)], tl=512,
                     name=name)
    return dx, jnp.sum(dg.reshape(heads, ATTN_HEAD_DIM), axis=0)


def _to_heads(x, heads):
    return jnp.transpose(x.reshape(x.shape[0], heads, ATTN_HEAD_DIM), (1, 0, 2))


def _from_heads(x):
    return jnp.transpose(x, (1, 0, 2)).reshape(x.shape[1], x.shape[0] * ATTN_HEAD_DIM)


ATTN_GROUP = ATTN_Q_HEADS // ATTN_KV_HEADS
ATTN_TQ = 256


def _attn_fwd(q, k, v, side=None):
    length = q.shape[1]
    tq = min(ATTN_TQ, length)
    grid = (ATTN_KV_HEADS, length // tq)

    def compute(refs):
        q_ref, k_ref, v_ref, o_ref = refs
        kk, vv = k_ref[0], v_ref[0]
        for g in range(ATTN_GROUP):
            s = _dot(q_ref[g], kk, _NT)
            p = jnp.exp(s - jnp.max(s, axis=-1, keepdims=True))
            o_ref[g] = _dot(p, vv) / jnp.sum(p, axis=-1, keepdims=True)

    def body(*refs):
        _carried(side, grid, refs, 3, 1, 0, compute)

    kv_spec = pl.BlockSpec((1, length, ATTN_HEAD_DIM), lambda h, i: (h, 0, 0))
    q_spec = pl.BlockSpec((ATTN_GROUP, tq, ATTN_HEAD_DIM), lambda h, i: (h, i, 0))
    (out,), gathered = _side_call(
        body, side, name="attn_fwd", grid=grid, in_specs=[q_spec, kv_spec, kv_spec], out_specs=[q_spec],
        out_shape=[jax.ShapeDtypeStruct(q.shape, F32)], scratch=[], args=[q, k, v], semantics=("parallel", "parallel"))
    return out, gathered


def _attn_bwd(q, k, v, o, do, side=None):
    length = q.shape[1]
    tq = min(ATTN_TQ, length)
    grid = (ATTN_KV_HEADS, length // tq)

    def body(*refs):
        _carried(side, grid, refs, 5, 3, 0, compute)

    def compute(refs):
        q_ref, k_ref, v_ref, o_ref, do_ref, dq_ref, dk_ref, dv_ref = refs

        @pl.when(pl.program_id(1) == 0)
        def _():
            dk_ref[...] = jnp.zeros_like(dk_ref)
            dv_ref[...] = jnp.zeros_like(dv_ref)

        kk, vv = k_ref[0], v_ref[0]
        for g in range(ATTN_GROUP):
            qg, dog = q_ref[g], do_ref[g]
            s = _dot(qg, kk, _NT)
            p = jnp.exp(s - jnp.max(s, axis=-1, keepdims=True))
            p = p * (1.0 / jnp.sum(p, axis=-1, keepdims=True))
            dp = _dot(dog, vv, _NT)
            ds = p * (dp - jnp.sum(dog * o_ref[g], axis=-1, keepdims=True))
            dq_ref[g] = _dot(ds, kk)
            dk_ref[0] += _dot(ds, qg, _TN)
            dv_ref[0] += _dot(p, dog, _TN)

    kv_spec = pl.BlockSpec((1, length, ATTN_HEAD_DIM), lambda h, i: (h, 0, 0))
    q_spec = pl.BlockSpec((ATTN_GROUP, tq, ATTN_HEAD_DIM), lambda h, i: (h, i, 0))
    return _side_call(
        body, side, name="attn_bwd", grid=grid, in_specs=[q_spec, kv_spec, kv_spec, q_spec, q_spec],
        out_specs=[q_spec, kv_spec, kv_spec],
        out_shape=[jax.ShapeDtypeStruct(q.shape, F32), jax.ShapeDtypeStruct(k.shape, F32), jax.ShapeDtypeStruct(k.shape, F32)],
        scratch=[], args=[q, k, v, o, do], semantics=("parallel", "arbitrary"))


def _attn_branch_fwd(pg, q_gain, k_gain, side=None):
    cos, sin = _rope_tables(pg.shape[0])
    qp = _qk_prep_fwd(pg, CB_AQ, ATTN_WIDTH, q_gain, cos, sin, ATTN_HEAD_DIM ** -0.5, name="attn_q_prep")
    kp = _qk_prep_fwd(pg, CB_AK, ATTN_KV_WIDTH, k_gain, cos, sin, 1.0, name="attn_k_prep")
    qh, kh = _to_heads(qp, ATTN_Q_HEADS), _to_heads(kp, ATTN_KV_HEADS)
    vh = _to_heads(pg[:, P_OFF + 3200:P_OFF + 3328].astype(BF16), ATTN_KV_HEADS)
    oh, gathered = _attn_fwd(qh, kh, vh, side)
    return _from_heads(oh).astype(BF16), (cos, sin, qh, kh, vh, oh), gathered


def _attn_branch_bwd(pg, q_gain, k_gain, saved, dy, side=None):
    cos, sin, qh, kh, vh, oh = saved
    (dqh, dkh, dvh), carried = _attn_bwd(qh, kh, vh, oh, _to_heads(dy, ATTN_Q_HEADS), side)
    dq, dqg = _qk_prep_bwd(pg, CB_AQ, ATTN_WIDTH, q_gain, cos, sin, ATTN_HEAD_DIM ** -0.5, _from_heads(dqh),
                           name="attn_q_prep_bwd")
    dk, dkg = _qk_prep_bwd(pg, CB_AK, ATTN_KV_WIDTH, k_gain, cos, sin, 1.0, _from_heads(dkh), name="attn_k_prep_bwd")
    return dq, dk, _from_heads(dvh), {'q_norm': dqg, 'k_norm': dkg}, carried


def _gate_cols():
    return [slice(i * D_MODEL, (i + 1) * D_MODEL) for i in range(3)]


def _mixer_fwd(x, lw, side_in=None, after_in=None, side_attn=None):
    h = _rmsnorm_fwd(x, lw['mix_norm'])
    if side_in is None:
        pg = _mm(h, lw['w_pg'], name="mix_in")
    else:
        pg, got_in = _mm(h, lw['w_pg'], side=side_in, name="mix_in")
        after_in(got_in)
    y_s5, s_s5 = _s5_fwd(pg, lw['s5'], lw['s5_w_glu'])
    y_gla, s_gla = _gla_branch_fwd(pg, lw['gla_w_alpha'], lw['gla_b_alpha'], lw['gla_norm'])
    y_att, s_att, got_attn = _attn_branch_fwd(pg, lw['attn_q_norm'], lw['attn_k_norm'], side_attn)
    ys = (y_s5, y_gla, y_att)
    br = [_mm(y, lw[n], name="mix_branch") for y, n in zip(ys, ('w_branch_s5', 'w_branch_gla', 'w_branch_attn'))]

    def merge(g0, g1, g2, b0, b1, b2, bias):
        acc = None
        for g, b, c in zip((g0, g1, g2), (b0, b1, b2), _gate_cols()):
            term = _sigmoid(g + bias[:, c]) * b
            acc = term if acc is None else acc + term
        return acc
    merged = _rowmap(merge, [(pg, 0, D_MODEL), (pg, 1, D_MODEL), (pg, 2, D_MODEL)] + br,
                     [lw['b_merge_gate'].reshape(1, -1)], [(D_MODEL, BF16)], tl=256, name="mix_merge")[0]
    x_out = _mm(merged, lw['w_out'], add=x, name="mix_out")
    return x_out, (x, h, pg, ys, (s_s5, s_gla, s_att), br, merged), got_attn


def _mixer_bwd(saved, lw, dx_out, side=None):
    x, h, pg, ys, (s_s5, s_gla, s_att), br, merged = saved
    grads = {'w_out': _mm(merged, dx_out, ta=True, out_dtype=BF16, name="mix_dwout")}
    dmerged = _mm(dx_out, lw['w_out'], tb=True, name="mix_dmerged")

    def merge_bwd(g0, g1, g2, b0, b1, b2, dm, bias):
        dbr, dgp = [], []
        for g, b, c in zip((g0, g1, g2), (b0, b1, b2), _gate_cols()):
            s = _sigmoid(g + bias[:, c])
            dbr.append(dm * s)
            dgp.append(dm * b * (s * (1.0 - s)))
        dgp = jnp.concatenate(dgp, axis=1)
        return dbr[0], dbr[1], dbr[2], dgp, jnp.sum(dgp, axis=0, keepdims=True)
    d0, d1, d2, dgpre, dbias = _rowmap(
        merge_bwd, [(pg, 0, D_MODEL), (pg, 1, D_MODEL), (pg, 2, D_MODEL)] + br + [dmerged],
        [lw['b_merge_gate'].reshape(1, -1)], [(D_MODEL, BF16)] * 3 + [(GATE_WIDTH, BF16)], [(1, GATE_WIDTH)], tl=256,
        name="mix_merge_bwd")
    grads['b_merge_gate'] = dbias[0]
    dys = []
    for y, dbr, n in zip(ys, (d0, d1, d2), ('w_branch_s5', 'w_branch_gla', 'w_branch_attn')):
        grads[n] = _mm(y, dbr, ta=True, out_dtype=BF16, name="mix_dwbranch")
        dys.append(_mm(dbr, lw[n], tb=True, name="mix_dy"))
    du, g_s5 = _s5_bwd(pg, lw['s5'], lw['s5_w_glu'], s_s5, dys[0])
    dgq, dgk, dgv, dgg, dz, g_gla = _gla_branch_bwd(pg, lw['gla_w_alpha'], lw['gla_b_alpha'], lw['gla_norm'], s_gla, dys[1])
    daq, dak, dav, g_att, carried = _attn_branch_bwd(pg, lw['attn_q_norm'], lw['attn_k_norm'], s_att, dys[2], side)

    def assemble(dgp, u0, u1, u2, q0, q1, k0, k1, v0, v1, gg, aq, ak, av, z):
        pad = jnp.zeros((dgp.shape[0], IN_PAD - 3456), F32)
        parts = [dgp.astype(F32), u0 + u1 + u2, q0 + q1, k0 + k1, v0 + v1, gg, aq, ak, av, z, pad]
        return jnp.concatenate(parts, axis=1)
    dpg = _rowmap(assemble, [dgpre] + du + dgq + dgk + dgv + [dgg, daq, dak, dav, dz], [], [(PG_WIDTH, BF16)], tl=256,
                  name="mix_dpg")[0]
    grads['w_pg'] = _mm(h, dpg, ta=True, out_dtype=BF16, name="mix_dwpg")
    dh = _mm(dpg, lw['w_pg'], tb=True, name="mix_dh")
    dx, grads['mix_norm'] = _rmsnorm_bwd(x, lw['mix_norm'], dh, dx_out)
    grads['s5'], grads['gla'], grads['attn'] = g_s5, g_gla, g_att
    return dx, grads, carried


def _loss_head(x, gain, target):
    width = x.shape[1]

    def fn(xv, tv, g):
        r = _rms(xv)
        err = xv * r * g - tv
        dy = err * (1.0 / width)
        gd = dy * g
        dx = r * gd - xv * (r * r * r) * jnp.mean(xv * gd, axis=-1, keepdims=True)
        loss = jnp.sum(0.5 * jnp.mean(err * err, axis=-1, keepdims=True), axis=0, keepdims=True)
        return dx, jnp.broadcast_to(loss, (1, 128)), jnp.sum(dy * xv * r, axis=0, keepdims=True)
    dx, loss, dgain = _rowmap(fn, [x, target], [gain.reshape(1, -1)], [(width, F32)], [(1, 128), (1, width)], tl=256,
                              name="loss_head")
    return loss[0, 0], dx, dgain[0]


def _row_tile(rows, cap=256):
    for t in range(cap - cap % 16, 0, -16):
        if rows % t == 0:
            return t
    return rows


def _reduce_adamw(parts, w, m, v, *, name):
    r, c = w.shape
    if len(parts) > 1 and parts[0].shape[1] % 8:
        parts = [jnp.concatenate(parts, axis=1)]
    nparts, rows = parts[0].shape[0], parts[0].shape[1]
    tr = _row_tile(rows)
    per = rows // tr

    def body(*refs):
        p_refs, (w_ref, m_ref, v_ref, g_ref, d_ref, m2_ref, v2_ref) = refs[:len(parts)], refs[len(parts):]
        g = None
        for k, p_ref in enumerate(p_refs):
            gk = p_ref[0].astype(F32)
            for j in range(1, nparts):
                gk = gk + p_ref[j].astype(F32)
            g = gk if g is None else jnp.where(pl.program_id(0) // per == k, gk, g)
        m2 = ADAM_B1 * m_ref[...] + (1.0 - ADAM_B1) * g
        v2 = ADAM_B2 * v_ref[...] + (1.0 - ADAM_B2) * (g * g)
        m_hat = m2 / (1.0 - ADAM_B1 ** ADAM_STEP)
        v_hat = v2 / (1.0 - ADAM_B2 ** ADAM_STEP)
        g_ref[...] = g
        d_ref[...] = -ADAM_LR * (m_hat / (jnp.sqrt(v_hat) + ADAM_EPS) + ADAM_WD * w_ref[...])
        m2_ref[...] = m2
        v2_ref[...] = v2

    flat = pl.BlockSpec((tr, c), lambda i: (i, 0))
    p_specs = [pl.BlockSpec((nparts, tr, c), lambda i, k=k: (0, jnp.clip(i - k * per, 0, per - 1), 0)) for k in range(len(parts))]
    return pl.pallas_call(
        body, name=name, grid=(r // tr,), in_specs=p_specs + [flat, flat, flat],
        out_specs=[flat] * 4, out_shape=[jax.ShapeDtypeStruct((r, c), F32)] * 4,
        compiler_params=pltpu.CompilerParams(dimension_semantics=("parallel",)),
    )(*parts, w, m, v)


def _all_gather(blocks, *, name):
    side = _SideGather(blocks)

    def body(*refs):
        start, finish = side.hooks(refs)
        start()
        finish()

    return pl.pallas_call(body, name=name, out_shape=side.out_shape, in_specs=side.in_specs, out_specs=side.out_specs,
                          scratch_shapes=side.scratch)(*blocks)


class _SideGather:
    def __init__(self, blocks):
        self.blocks = list(blocks)
        self.n = n = len(self.blocks)
        hbm = pl.BlockSpec(memory_space=pl.ANY)
        self.in_specs, self.out_specs = [hbm] * n, [hbm] * n
        self.out_shape = [jax.ShapeDtypeStruct((N_DEV,) + b.shape, b.dtype) for b in self.blocks]
        self.scratch = [pltpu.SemaphoreType.DMA((n, 7)), pltpu.SemaphoreType.DMA((n, 7)), pltpu.SemaphoreType.DMA((n,))]

    def hooks(self, refs):
        n = self.n
        x_refs, out_refs = refs[:n], refs[n:2 * n]
        send_sems, recv_sems, local_sems = refs[2 * n:]
        x, y, c = lax.axis_index("x"), lax.axis_index("y"), lax.axis_index("c")
        me, sibling = (x, y, c), (x, y, 1 - c)
        chips = [(1 - x, y), (x, 1 - y), (1 - x, 1 - y)]

        def slot(t, px, py, pc):
            return out_refs[t].at[4 * px + 2 * py + pc]

        def copy(t, k, blk, to, own=False):
            return pltpu.make_async_remote_copy(
                src_ref=x_refs[t] if own else slot(t, *blk), dst_ref=slot(t, *blk), send_sem=send_sems.at[t, k],
                recv_sem=recv_sems.at[t, k], device_id=to, device_id_type=pl.DeviceIdType.MESH)

        def mine(t):
            return pltpu.make_async_copy(x_refs[t], slot(t, *me), local_sems.at[t])

        def first(t):
            return [copy(t, 0, me, sibling, own=True)] + [copy(t, 1 + j, me, (*chip, c), own=True) for j, chip in enumerate(chips)]

        def start():
            for t in range(n):
                mine(t).start()
            for t in range(n):
                for cp in first(t):
                    cp.start()

        def finish():
            passed = []
            for j, chip in enumerate(chips):
                for t in range(n):
                    copy(t, 1 + j, (*chip, c), me).wait_recv()
                    passed.append(copy(t, 4 + j, (*chip, c), sibling))
                    passed[-1].start()
            for t in range(n):
                copy(t, 0, sibling, me).wait_recv()
            for j, chip in enumerate(chips):
                for t in range(n):
                    copy(t, 4 + j, (*chip, 1 - c), me).wait_recv()
            for t in range(n):
                for cp in first(t):
                    cp.wait_send()
            for cp in passed:
                cp.wait_send()
            for t in range(n):
                mine(t).wait()

        return start, finish


def _first_last_step(grid):
    ids = [pl.program_id(a) for a in range(len(grid))]
    first = functools.reduce(lambda p, q: p & q, [i == 0 for i in ids])
    last = functools.reduce(lambda p, q: p & q, [i == n - 1 for i, n in zip(ids, grid)])
    return first, last


def _carried(side, grid, refs, n_in, n_out, n_scratch, compute):
    if side is None:
        compute(refs)
        return
    n = side.n
    main = refs[:n_in] + refs[n_in + n:n_in + n + n_out] + refs[n_in + 2 * n + n_out:n_in + 2 * n + n_out + n_scratch]
    side_refs = refs[n_in:n_in + n] + refs[n_in + n + n_out:n_in + 2 * n + n_out] + refs[n_in + 2 * n + n_out + n_scratch:]
    start, finish = side.hooks(side_refs)
    first, last = _first_last_step(grid)
    pl.when(first)(start)
    compute(main)
    pl.when(last)(finish)


N_CHIP = N_DEV // 2


def _swap_with_sibling(arrays, *, name):
    n = len(arrays)

    def body(*refs):
        src_refs, out_refs = refs[:n], refs[n:2 * n]
        send_sems, recv_sems = refs[2 * n:]
        sibling = (lax.axis_index("x"), lax.axis_index("y"), 1 - lax.axis_index("c"))
        copies = [pltpu.make_async_remote_copy(
            src_ref=src_refs[t], dst_ref=out_refs[t], send_sem=send_sems.at[t], recv_sem=recv_sems.at[t],
            device_id=sibling, device_id_type=pl.DeviceIdType.MESH) for t in range(n)]
        for cp in copies:
            cp.start()
        for cp in copies:
            cp.wait()

    hbm = pl.BlockSpec(memory_space=pl.ANY)
    return pl.pallas_call(
        body, name=name, out_shape=[jax.ShapeDtypeStruct(a.shape, a.dtype) for a in arrays],
        in_specs=[hbm] * n, out_specs=[hbm] * n,
        scratch_shapes=[pltpu.SemaphoreType.DMA((n,)), pltpu.SemaphoreType.DMA((n,))],
    )(*arrays)


def _exchange_chips(stacks, *, name):
    side = _SideChipExchange(stacks)

    def body(*refs):
        start, finish = side.hooks(refs)
        start()
        finish()

    return pl.pallas_call(body, name=name, out_shape=side.out_shape, in_specs=side.in_specs, out_specs=side.out_specs,
                          scratch_shapes=side.scratch)(*stacks)


class _SideChipExchange:
    def __init__(self, stacks):
        self.blocks = list(stacks)
        self.n = n = len(self.blocks)
        hbm = pl.BlockSpec(memory_space=pl.ANY)
        self.in_specs, self.out_specs = [hbm] * n, [hbm] * n
        self.out_shape = [jax.ShapeDtypeStruct(s.shape, s.dtype) for s in self.blocks]
        self.scratch = [pltpu.SemaphoreType.DMA((n, N_CHIP - 1)), pltpu.SemaphoreType.DMA((n, N_CHIP - 1)),
                        pltpu.SemaphoreType.DMA((n,))]

    def hooks(self, refs):
        n = self.n
        g_refs, out_refs = refs[:n], refs[n:2 * n]
        send_sems, recv_sems, local_sems = refs[2 * n:]
        x, y, c = lax.axis_index("x"), lax.axis_index("y"), lax.axis_index("c")
        me = 2 * x + y

        def copies():
            mine = [pltpu.make_async_copy(g_refs[t].at[me], out_refs[t].at[me], local_sems.at[t]) for t in range(n)]
            remote = []
            for k in range(1, N_CHIP):
                px, py = x ^ (k >> 1 & 1), y ^ (k & 1)
                for t in range(n):
                    remote.append(pltpu.make_async_remote_copy(
                        src_ref=g_refs[t].at[2 * px + py], dst_ref=out_refs[t].at[me], send_sem=send_sems.at[t, k - 1],
                        recv_sem=recv_sems.at[t, k - 1], device_id=(px, py, c), device_id_type=pl.DeviceIdType.MESH))
            return mine, remote

        def start():
            mine, remote = copies()
            for cp in mine + remote:
                cp.start()

        def finish():
            mine, remote = copies()
            for cp in remote:
                cp.wait_recv()
            for cp in remote:
                cp.wait_send()
            for cp in mine:
                cp.wait()

        return start, finish


def _pair_sum(a, b):
    return _rowmap(lambda u, v: u.astype(F32) + v.astype(F32), [a, b], [], [(a.shape[1], BF16)], tl=_row_tile(a.shape[0], 512),
                   name="pair_sum")[0]


SMALL_COLS = 128


def _pack_small(arrays):
    flat = jnp.concatenate([a.astype(F32).reshape(-1, SMALL_COLS) for a in arrays], axis=0)
    return jnp.pad(flat, ((0, -flat.shape[0] % 256), (0, 0)))


def _unpack_small(packed, shapes):
    out, off = [], 0
    for s in shapes:
        r = math.prod(s) // SMALL_COLS
        out.append(packed[off:off + r].reshape(s))
        off += r
    return out


def _split_shards(full, axis):
    shape = full.shape
    split = full.reshape(shape[:axis] + (N_DEV, shape[axis] // N_DEV) + shape[axis + 1:])
    return jnp.moveaxis(split, axis, 0)


def _join_shards(stack, axis):
    moved = jnp.moveaxis(stack, 0, axis)
    shape = moved.shape
    return moved.reshape(shape[:axis] + (shape[axis] * shape[axis + 1],) + shape[axis + 2:])


def _w_in_unpadded(w):
    return jnp.concatenate([w[..., :2560], w[..., 3328:3360], w[..., 2560:3328]], axis=-1)


FFN1_W = ('ffn1_w_gate', 'ffn1_w_up', 'ffn1_w_down')
FFN2_W = ('ffn2_w_gate', 'ffn2_w_up', 'ffn2_w_down')
MIX_IN_W = ('w_in', 'w_merge_gate')
MIX_REST_W = ('s5_w_glu', 'gla_w_alpha', 'gla_b_alpha', 'w_branch_s5', 'w_branch_gla', 'w_branch_attn', 'w_out')


def _mixer_weights(full, w, s5, i):
    lw = {n: w[n][i] for n in ('mix_norm', 'gla_norm', 'attn_q_norm', 'attn_k_norm', 'b_merge_gate')}
    lw['s5'] = {'b_mat': s5['b_mat'][i], 'c_mat': s5['c_mat'][i], 'tabs': [t[i] for t in s5['tabs']],
                'tabs_adj': [t[i] for t in s5['tabs_adj']], 'd': w['s5_d'][i]}
    w_in = full['w_in']
    pad = jnp.zeros((D_MODEL, IN_PAD - IN_WIDTH), w_in.dtype)
    lw['w_pg'] = jnp.concatenate([full['w_merge_gate'], w_in[:, :2560], w_in[:, 2592:], w_in[:, 2560:2592], pad], axis=1)
    return lw


def _mixer_weights_rest(full):
    lw = {n: full[n] for n in MIX_REST_W if n != 'gla_b_alpha'}
    lw['gla_b_alpha'] = full['gla_b_alpha'].astype(F32)
    return lw


def _chip_sums(grads, names):
    core = lax.axis_index("c")
    own, for_sibling = [], []
    for n in names:
        by_owner = _split_shards(grads[n], SHARD_AXIS[n] - 1).astype(BF16)
        by_owner = by_owner.reshape((N_CHIP, 2) + by_owner.shape[1:])
        own.append(lax.dynamic_index_in_dim(by_owner, core, axis=1, keepdims=False))
        for_sibling.append(lax.dynamic_index_in_dim(by_owner, 1 - core, axis=1, keepdims=False))
    from_sibling = _swap_with_sibling(for_sibling, name="exchange_grads_sibling")
    return [_pair_sum(a.reshape(-1, a.shape[-1]), b.reshape(-1, b.shape[-1])).reshape(a.shape)
            for a, b in zip(own, from_sibling)]


def _step_local(x, target, w, shards):
    s5 = _s5_setup(w)
    full = [{} for _ in range(DEPTH)]

    def wanted(i, *groups):
        return _SideGather([shards[n][i] for names in groups for n in names])

    def arrived(i, stacks, *groups):
        names = [n for group in groups for n in group]
        for n, st in zip(names, stacks):
            full[i][n] = _join_shards(st, SHARD_AXIS[n] - 1)

    arrived(0, _all_gather([shards[n][0] for n in FFN1_W], name="gather_first"), FFN1_W)
    saved, lws = [], []
    for i in range(DEPTH):
        f, first = full[i], i == 0
        x, s1, got = _ffn_fwd(x, w['ffn1_norm'][i], f['ffn1_w_gate'], f['ffn1_w_up'], f['ffn1_w_down'],
                              wanted(i, MIX_IN_W) if first else None)
        if first:
            arrived(i, got, MIX_IN_W)
        lw = _mixer_weights(f, w, s5, i)
        lws.append(lw)

        def after_in(got_in, i=i, lw=lw):
            arrived(i, got_in, MIX_REST_W, FFN2_W)
            lw.update(_mixer_weights_rest(full[i]))
        if not first:
            lw.update(_mixer_weights_rest(f))
        x, s2, got = _mixer_fwd(x, lw, wanted(i, MIX_REST_W, FFN2_W) if first else None, after_in,
                                wanted(i + 1, FFN1_W, MIX_IN_W, MIX_REST_W) if first else wanted(i, FFN2_W))
        if first:
            arrived(i + 1, got, FFN1_W, MIX_IN_W, MIX_REST_W)
        else:
            arrived(i, got, FFN2_W)
        x, s3, _ = _ffn_fwd(x, w['ffn2_norm'][i], f['ffn2_w_gate'], f['ffn2_w_up'], f['ffn2_w_down'])
        saved.append((s1, s2, s3))
    loss, dx, d_final = _loss_head(x, w['final_norm'], target)
    per_layer, incoming = [None] * DEPTH, [{} for _ in range(DEPTH)]
    later = [n for n in SHARDED if n not in FFN2_W]
    pending = []
    for i in reversed(range(DEPTH)):
        f, lw, (s1, s2, s3), g = full[i], lws[i], saved[i], {}
        dx, g['ffn2_norm'], g['ffn2_w_gate'], g['ffn2_w_up'], g['ffn2_w_down'] = _ffn_bwd(
            s3, w['ffn2_norm'][i], f['ffn2_w_gate'], f['ffn2_w_up'], f['ffn2_w_down'], dx)
        pending.append((i, FFN2_W, _chip_sums(g, FFN2_W)))
        dx, gm, carried = _mixer_bwd(s2, lw, dx, _SideChipExchange([s for _, _, sums in pending for s in sums]))
        for layer, names, _ in pending:
            incoming[layer].update(zip(names, carried[:len(names)]))
            carried = carried[len(names):]
        dx, g['ffn1_norm'], g['ffn1_w_gate'], g['ffn1_w_up'], g['ffn1_w_down'] = _ffn_bwd(
            s1, w['ffn1_norm'][i], f['ffn1_w_gate'], f['ffn1_w_up'], f['ffn1_w_down'], dx)
        g['w_merge_gate'] = gm['w_pg'][:, :GATE_WIDTH]
        g['w_in'] = _w_in_unpadded(gm['w_pg'][:, GATE_WIDTH:])
        for n in ('w_out', 'b_merge_gate', 'w_branch_s5', 'w_branch_gla', 'w_branch_attn', 'mix_norm'):
            g[n] = gm[n]
        g['s5_d'], g['s5_w_glu'], g['s5_raw'] = gm['s5']['d'], gm['s5']['w_glu'], gm['s5']['raw']
        g['gla_w_alpha'], g['gla_b_alpha'], g['gla_norm'] = gm['gla']['w_alpha'], gm['gla']['b_alpha'], gm['gla']['norm']
        g['attn_q_norm'], g['attn_k_norm'] = gm['attn']['q_norm'], gm['attn']['k_norm']
        per_layer[i] = g
        pending = [(i, later, _chip_sums(g, later))]
    incoming[0].update(zip(later, _exchange_chips(pending[0][2], name="exchange_grads_chips")))
    stacked = _s5_param_grads(w, [g['s5_raw'] for g in per_layer])
    stacked['final_norm'] = d_final
    return loss, dx, per_layer, stacked, incoming


def kernel(x, ffn1_norm, ffn1_w_gate, ffn1_w_up, ffn1_w_down, mix_norm, w_in, s5_lambda_re, s5_lambda_im, s5_log_dt, s5_b_re, s5_b_im, s5_c_re, s5_c_im, s5_d, s5_w_glu, gla_w_alpha, gla_b_alpha, gla_norm, attn_q_norm, attn_k_norm, w_branch_s5, w_branch_gla, w_branch_attn, w_merge_gate, b_merge_gate, w_out, ffn2_norm, ffn2_w_gate, ffn2_w_up, ffn2_w_down, final_norm, loss_target, m_ffn1_norm, m_ffn1_w_gate, m_ffn1_w_up, m_ffn1_w_down, m_mix_norm, m_w_in, m_s5_lambda_re, m_s5_lambda_im, m_s5_log_dt, m_s5_b_re, m_s5_b_im, m_s5_c_re, m_s5_c_im, m_s5_d, m_s5_w_glu, m_gla_w_alpha, m_gla_b_alpha, m_gla_norm, m_attn_q_norm, m_attn_k_norm, m_w_branch_s5, m_w_branch_gla, m_w_branch_attn, m_w_merge_gate, m_b_merge_gate, m_w_out, m_ffn2_norm, m_ffn2_w_gate, m_ffn2_w_up, m_ffn2_w_down, m_final_norm, v_ffn1_norm, v_ffn1_w_gate, v_ffn1_w_up, v_ffn1_w_down, v_mix_norm, v_w_in, v_s5_lambda_re, v_s5_lambda_im, v_s5_log_dt, v_s5_b_re, v_s5_b_im, v_s5_c_re, v_s5_c_im, v_s5_d, v_s5_w_glu, v_gla_w_alpha, v_gla_b_alpha, v_gla_norm, v_attn_q_norm, v_attn_k_norm, v_w_branch_s5, v_w_branch_gla, v_w_branch_attn, v_w_merge_gate, v_b_merge_gate, v_w_out, v_ffn2_norm, v_ffn2_w_gate, v_ffn2_w_up, v_ffn2_w_down, v_final_norm):
    return _train_step(x, ffn1_norm, ffn1_w_gate, ffn1_w_up, ffn1_w_down, mix_norm, w_in, s5_lambda_re, s5_lambda_im, s5_log_dt, s5_b_re, s5_b_im, s5_c_re, s5_c_im, s5_d, s5_w_glu, gla_w_alpha, gla_b_alpha, gla_norm, attn_q_norm, attn_k_norm, w_branch_s5, w_branch_gla, w_branch_attn, w_merge_gate, b_merge_gate, w_out, ffn2_norm, ffn2_w_gate, ffn2_w_up, ffn2_w_down, final_norm, loss_target, m_ffn1_norm, m_ffn1_w_gate, m_ffn1_w_up, m_ffn1_w_down, m_mix_norm, m_w_in, m_s5_lambda_re, m_s5_lambda_im, m_s5_log_dt, m_s5_b_re, m_s5_b_im, m_s5_c_re, m_s5_c_im, m_s5_d, m_s5_w_glu, m_gla_w_alpha, m_gla_b_alpha, m_gla_norm, m_attn_q_norm, m_attn_k_norm, m_w_branch_s5, m_w_branch_gla, m_w_branch_attn, m_w_merge_gate, m_b_merge_gate, m_w_out, m_ffn2_norm, m_ffn2_w_gate, m_ffn2_w_up, m_ffn2_w_down, m_final_norm, v_ffn1_norm, v_ffn1_w_gate, v_ffn1_w_up, v_ffn1_w_down, v_mix_norm, v_w_in, v_s5_lambda_re, v_s5_lambda_im, v_s5_log_dt, v_s5_b_re, v_s5_b_im, v_s5_c_re, v_s5_c_im, v_s5_d, v_s5_w_glu, v_gla_w_alpha, v_gla_b_alpha, v_gla_norm, v_attn_q_norm, v_attn_k_norm, v_w_branch_s5, v_w_branch_gla, v_w_branch_attn, v_w_merge_gate, v_b_merge_gate, v_w_out, v_ffn2_norm, v_ffn2_w_gate, v_ffn2_w_up, v_ffn2_w_down, v_final_norm)


def _train_step(*args):
    nw = len(W_NAMES)
    x, target = args[0][0], args[1 + nw][0]
    w = dict(zip(W_NAMES, args[1:1 + nw]))
    m = dict(zip(W_NAMES, args[2 + nw:2 + 2 * nw]))
    v = dict(zip(W_NAMES, args[2 + 2 * nw:2 + 3 * nw]))

    loss, dx, per_layer, stacked, incoming = _step_local(x, target, w, {n: w[n].astype(BF16) for n in SHARDED})
    loss = lax.psum(loss, ("x", "y", "c"))

    out = {}
    kinds = ('grad', 'delta', 'new_m', 'new_v')
    for n in SHARDED:
        shape = w[n].shape
        flat = lambda a: a.reshape(-1, shape[-1])
        parts = [incoming[i][n].reshape(N_CHIP, -1, shape[-1]) for i in range(DEPTH)]
        res = _reduce_adamw(parts, flat(w[n]), flat(m[n]), flat(v[n]), name="adamw_sharded")
        for kind, a in zip(kinds, res):
            out[kind + '_' + n] = a.reshape(shape)
    small = [stacked[n] if n in stacked else jnp.stack([g[n] for g in per_layer]) for n in REPLICATED]
    parts = _all_gather([_pack_small(small)], name="gather_small_grads")[0]
    res = _reduce_adamw([parts], *[_pack_small([d[n] for n in REPLICATED]) for d in (w, m, v)], name="adamw_replicated")
    for kind, packed in zip(kinds, res):
        for n, a in zip(REPLICATED, _unpack_small(packed, [w[n].shape for n in REPLICATED])):
            out[kind + '_' + n] = a
    return (loss, dx[None]) + tuple(out[kind + '_' + n] for kind in kinds for n in W_NAMES)
```

```python
import functools
import math

import jax
import jax.numpy as jnp
import numpy as np
from jax import lax
from jax.experimental import pallas as pl
from jax.experimental.pallas import tpu as pltpu

F32 = jnp.float32
BF16 = jnp.bfloat16

N_DEV = 8
D_MODEL = 1024
DEPTH = 2
GRID_W = 64
D_FF = 2816
NORM_EPS = 1e-6
S5_GROUPS = 32
S5_GROUP_CH = 16
S5_STATE = 64
S5_WIDTH = 512
S5_NSTATE = S5_GROUPS * S5_STATE
S5_LANE_BLOCK = 512
GLA_HEADS = 4
GLA_HEAD_DIM = 128
GLA_WIDTH = 512
GLA_LOWRANK = 16
GLA_TAU = 16.0
GLA_CHUNK = 64
ATTN_Q_HEADS = 8
ATTN_KV_HEADS = 2
ATTN_HEAD_DIM = 64
ATTN_WIDTH = 512
ATTN_KV_WIDTH = 128
ROPE_BASE = 10000.0
IN_SPLITS = (512, 512, 512, 512, 512, 16, 16, 512, 128, 128)
IN_WIDTH = sum(IN_SPLITS)
IN_PAD = 3584
GATE_WIDTH = 3 * D_MODEL
PG_WIDTH = GATE_WIDTH + IN_PAD
P_OFF = GATE_WIDTH
CB_U, CB_GQ, CB_GK, CB_GV, CB_GG, CB_AQ = (P_OFF // 512 + i for i in range(6))
CB_AK, CB_AV, CB_Z = (P_OFF + 3072) // 128, (P_OFF + 3200) // 128, (P_OFF + 3328) // 128
ADAM_LR = 0.001
ADAM_B1 = 0.9
ADAM_B2 = 0.999
ADAM_EPS = 1e-08
ADAM_WD = 0.01
ADAM_STEP = 10

W_NAMES = ['ffn1_norm', 'ffn1_w_gate', 'ffn1_w_up', 'ffn1_w_down', 'mix_norm', 'w_in', 's5_lambda_re', 's5_lambda_im',
           's5_log_dt', 's5_b_re', 's5_b_im', 's5_c_re', 's5_c_im', 's5_d', 's5_w_glu', 'gla_w_alpha', 'gla_b_alpha',
           'gla_norm', 'attn_q_norm', 'attn_k_norm', 'w_branch_s5', 'w_branch_gla', 'w_branch_attn', 'w_merge_gate',
           'b_merge_gate', 'w_out', 'ffn2_norm', 'ffn2_w_gate', 'ffn2_w_up', 'ffn2_w_down', 'final_norm']
SHARD_AXIS = {'ffn1_w_gate': 2, 'ffn1_w_up': 2, 'ffn1_w_down': 1, 'w_in': 2, 's5_w_glu': 1, 'gla_w_alpha': 3,
              'gla_b_alpha': 2, 'w_branch_s5': 2, 'w_branch_gla': 2, 'w_branch_attn': 2, 'w_merge_gate': 2,
              'w_out': 1, 'ffn2_w_gate': 2, 'ffn2_w_up': 2, 'ffn2_w_down': 1}
SHARDED = [n for n in W_NAMES if n in SHARD_AXIS]
REPLICATED = [n for n in W_NAMES if n not in SHARD_AXIS]


def _pick(dim, prefs):
    for p in prefs:
        if dim % p == 0:
            return p
    return dim


def _sigmoid(x):
    return 0.5 * jnp.tanh(0.5 * x) + 0.5


def _mm(a, b, *, ta=False, tb=False, out_dtype=F32, scale=None, add=None, side=None, name):
    a, a_cb, a_w = a if isinstance(a, tuple) else (a, 0, a.shape[1])
    b, b_cb, b_w = b if isinstance(b, tuple) else (b, 0, b.shape[1])
    m, k = (a_w, a.shape[0]) if ta else (a.shape[0], a_w)
    n = b.shape[0] if tb else b_w
    assert (b_w if tb else b.shape[0]) == k, (a.shape, b.shape, ta, tb)
    tm, tn, tk = _mm_tiles(m, n, k, a.dtype.itemsize, b.dtype.itemsize, jnp.dtype(out_dtype).itemsize)
    nk = k // tk
    dims = (((0 if ta else 1,), (1 if tb else 0,)), ((), ()))
    a_off = a_cb * (a_w // (tm if ta else tk))
    b_off = b_cb * (b_w // (tk if tb else tn))

    grid = (m // tm, n // tn, nk)
    n_in = 2 if add is None else 3

    def body(*refs):
        _carried(side, grid, refs, n_in, 1, int(nk > 1), compute)

    def compute(refs):
        a_ref, b_ref, *rest = refs
        add_ref = rest[0] if add is not None else None
        o_ref, *acc = rest[1:] if add is not None else rest

        def finish(res):
            res = res if scale is None else res * scale
            return (res if add_ref is None else res + add_ref[...]).astype(out_dtype)

        part = lax.dot_general(a_ref[...].astype(BF16), b_ref[...].astype(BF16), dims, preferred_element_type=F32)
        if nk == 1:
            o_ref[...] = finish(part)
            return
        acc_ref, = acc
        kk = pl.program_id(2)

        @pl.when(kk == 0)
        def _():
            acc_ref[...] = part

        @pl.when(kk > 0)
        def _():
            acc_ref[...] += part

        @pl.when(kk == nk - 1)
        def _():
            o_ref[...] = finish(acc_ref[...])

    a_spec = (pl.BlockSpec((tk, tm), lambda i, j, kk: (kk, i + a_off)) if ta
              else pl.BlockSpec((tm, tk), lambda i, j, kk: (i, kk + a_off)))
    b_spec = (pl.BlockSpec((tn, tk), lambda i, j, kk: (j, kk + b_off)) if tb
              else pl.BlockSpec((tk, tn), lambda i, j, kk: (kk, j + b_off)))
    o_spec = pl.BlockSpec((tm, tn), lambda i, j, kk: (i, j))
    (out,), gathered = _side_call(
        body, side, name=name, grid=grid, in_specs=[a_spec, b_spec] + ([o_spec] if add is not None else []),
        out_specs=[o_spec], out_shape=[jax.ShapeDtypeStruct((m, n), out_dtype)],
        scratch=[pltpu.VMEM((tm, tn), F32)] if nk > 1 else [], args=[a, b] + ([add] if add is not None else []),
        semantics=("parallel", "parallel", "arbitrary"))
    return out if side is None else (out, gathered)


MM_VMEM_BUDGET = 40 * 1024 * 1024


def _mm_tiles(m, n, k, a_bytes, b_bytes, out_bytes):
    tms = [t for t in (1024, 1408, 512, 256, 128) if m % t == 0] or [m]
    tns = [t for t in (1664, 1408, 512, 256, 128) if n % t == 0] or [n]
    tks = [k] + [t for t in (2048, 1024, 512, 256, 128) if k % t == 0 and t < k]
    for tk in tks:
        for tm in tms:
            for tn in tns:
                use = 2 * (tm * tk * a_bytes + tk * tn * b_bytes + tm * tn * out_bytes) + 2 * tm * tn * 4
                if use <= MM_VMEM_BUDGET:
                    return tm, tn, tk
    return tms[-1], tns[-1], tks[-1]


def _rowmap(fn, rows, consts, outs, reds=(), *, tl, name):
    rows = [r if isinstance(r, tuple) else (r, 0, r.shape[1]) for r in rows]
    length = rows[0][0].shape[0]
    tl = min(tl, length)
    nr, nc, no = len(rows), len(consts), len(outs)

    def body(*refs):
        res = fn(*[r[...] for r in refs[:nr + nc]])
        res = res if isinstance(res, tuple) else (res,)
        for o_ref, val in zip(refs[nr + nc:nr + nc + no], res[:no]):
            o_ref[...] = val.astype(o_ref.dtype)
        if reds:
            step = pl.program_id(0)
            red_refs = refs[nr + nc + no:]

            @pl.when(step == 0)
            def _():
                for d_ref, val in zip(red_refs, res[no:]):
                    d_ref[...] = val.astype(F32)

            @pl.when(step > 0)
            def _():
                for d_ref, val in zip(red_refs, res[no:]):
                    d_ref[...] += val.astype(F32)

    in_specs = [pl.BlockSpec((tl, w), lambda i, cb=cb: (i, cb)) for (_, cb, w) in rows]
    in_specs += [pl.BlockSpec(c.shape, lambda i, nd=c.ndim: (0,) * nd) for c in consts]
    out_specs = [pl.BlockSpec((tl, w), lambda i: (i, 0)) for (w, _) in outs]
    out_specs += [pl.BlockSpec(s, lambda i, nd=len(s): (0,) * nd) for s in reds]
    out_shape = [jax.ShapeDtypeStruct((length, w), dt) for (w, dt) in outs]
    out_shape += [jax.ShapeDtypeStruct(s, F32) for s in reds]
    res = pl.pallas_call(
        body, name=name, grid=(length // tl,), in_specs=in_specs, out_specs=out_specs, out_shape=out_shape,
        compiler_params=pltpu.CompilerParams(dimension_semantics=("arbitrary" if reds else "parallel",)),
    )(*[r[0] for r in rows], *consts)
    return res


def _rms(x):
    return lax.rsqrt(jnp.mean(x * x, axis=-1, keepdims=True) + NORM_EPS)


def _rmsnorm_fwd(x, gain):
    def fn(xv, g):
        return xv * _rms(xv) * g
    return _rowmap(fn, [x], [gain.reshape(1, -1)], [(x.shape[1], BF16)], tl=256, name="rmsnorm_fwd")[0]


def _rmsnorm_bwd(x, gain, dh, dres):
    def fn(xv, dhv, drv, g):
        r = _rms(xv)
        gd = dhv * g
        dx = r * gd - xv * (r * r * r) * jnp.mean(xv * gd, axis=-1, keepdims=True)
        return drv + dx, jnp.sum(dhv * xv * r, axis=0, keepdims=True)
    dx, dg = _rowmap(fn, [x, dh, dres], [gain.reshape(1, -1)], [(x.shape[1], F32)], [(1, x.shape[1])], tl=256,
                     name="rmsnorm_bwd")
    return dx, dg[0]


FFN_UNIT = D_FF // 2


def _side_call(body, side, *, name, grid, in_specs, out_specs, out_shape, scratch, args, semantics):
    if side is not None:
        in_specs, out_specs = in_specs + side.in_specs, out_specs + side.out_specs
        out_shape, scratch, args = out_shape + side.out_shape, scratch + side.scratch, list(args) + side.blocks
        semantics = ("arbitrary",) * len(grid)
    res = pl.pallas_call(body, name=name, grid=grid, in_specs=in_specs, out_specs=out_specs, out_shape=out_shape,
                         scratch_shapes=scratch, compiler_params=pltpu.CompilerParams(dimension_semantics=semantics))(*args)
    n_own = len(res) - (side.n if side is not None else 0)
    return res[:n_own], res[n_own:]


def _ffn_up(h, w_gate, w_up, side=None):
    length, k = h.shape
    tm = _pick(length, (512, 256, 128))
    grid = (D_FF // FFN_UNIT, length // tm)

    def compute(refs):
        h_ref, wg_ref, wu_ref, a_ref, g_ref, u_ref = refs
        hv = h_ref[...]
        g = jnp.dot(hv, wg_ref[...], preferred_element_type=F32)
        u = jnp.dot(hv, wu_ref[...], preferred_element_type=F32)
        a_ref[...] = (g * _sigmoid(g) * u).astype(BF16)
        g_ref[...] = g.astype(BF16)
        u_ref[...] = u.astype(BF16)

    def body(*refs):
        _carried(side, grid, refs, 3, 3, 0, compute)

    w_spec = pl.BlockSpec((k, FFN_UNIT), lambda j, i: (0, j))
    o_spec = pl.BlockSpec((tm, FFN_UNIT), lambda j, i: (i, j))
    return _side_call(
        body, side, name="ffn_up", grid=grid, in_specs=[pl.BlockSpec((tm, k), lambda j, i: (i, 0)), w_spec, w_spec],
        out_specs=[o_spec] * 3, out_shape=[jax.ShapeDtypeStruct((length, D_FF), BF16)] * 3, scratch=[],
        args=[h, w_gate, w_up], semantics=("parallel", "parallel"))


def _ffn_dgu(dxo, w_down, g, u):
    length, k = dxo.shape
    tm = _pick(length, (512, 256, 128))

    def body(d_ref, w_ref, g_ref, u_ref, dg_ref, du_ref):
        da = 0.5 * lax.dot_general(d_ref[...], w_ref[...], _NT, preferred_element_type=F32)
        gv = g_ref[...].astype(F32)
        s = _sigmoid(gv)
        dg_ref[...] = (da * u_ref[...].astype(F32) * (s * (1.0 + gv * (1.0 - s)))).astype(BF16)
        du_ref[...] = (da * (gv * s)).astype(BF16)

    o_spec = pl.BlockSpec((tm, FFN_UNIT), lambda j, i: (i, j))
    return pl.pallas_call(
        body, name="ffn_dgu", grid=(D_FF // FFN_UNIT, length // tm),
        in_specs=[pl.BlockSpec((tm, k), lambda j, i: (i, 0)), pl.BlockSpec((FFN_UNIT, k), lambda j, i: (j, 0)), o_spec, o_spec],
        out_specs=[o_spec] * 2, out_shape=[jax.ShapeDtypeStruct((length, D_FF), BF16)] * 2,
        compiler_params=pltpu.CompilerParams(dimension_semantics=("parallel", "parallel")),
    )(dxo, w_down, g, u)


def _ffn_fwd(x, gain, w_gate, w_up, w_down, side=None):
    h = _rmsnorm_fwd(x, gain)
    (a, g, u), gathered = _ffn_up(h, w_gate, w_up, side)
    x_out = _mm(a, w_down, scale=0.5, add=x, name="ffn_down")
    return x_out, (x, h, g, u, a), gathered


def _ffn_bwd(saved, gain, w_gate, w_up, w_down, dx_out):
    x, h, g, u, a = saved
    dxo = dx_out.astype(BF16)
    d_wdown = _mm(a, dxo, ta=True, scale=0.5, out_dtype=BF16, name="ffn_dwdown")
    dg, du = _ffn_dgu(dxo, w_down, g, u)
    d_wgate = _mm(h, dg, ta=True, out_dtype=BF16, name="ffn_dwgu")
    d_wup = _mm(h, du, ta=True, out_dtype=BF16, name="ffn_dwgu")
    dh = _mm(du, w_up, tb=True, add=_mm(dg, w_gate, tb=True, name="ffn_dh"), name="ffn_dh_add")
    dx, dgain = _rmsnorm_bwd(x, gain, dh, dx_out)
    return dx, dgain, d_wgate, d_wup, d_wdown


def _s5_blocked(re, im):
    lead = re.shape[:-1]
    nb = S5_NSTATE // S5_LANE_BLOCK
    both = jnp.stack([re.reshape(*lead, nb, S5_LANE_BLOCK), im.reshape(*lead, nb, S5_LANE_BLOCK)], axis=-2)
    return both.reshape(*lead, 2 * S5_NSTATE)


def _s5_unblocked(z):
    lead = z.shape[:-1]
    nb = S5_NSTATE // S5_LANE_BLOCK
    both = z.reshape(*lead, nb, 2, S5_LANE_BLOCK)
    return both[..., 0, :].reshape(*lead, S5_NSTATE), both[..., 1, :].reshape(*lead, S5_NSTATE)


def _s5_tables(a_re, a_im, reverse):
    a = lax.complex(a_re, a_im)
    a2 = a * a
    a4 = a2 * a2
    rows = jnp.arange(8)
    pw = [a]
    for _ in range(7):
        pw.append(pw[-1] * a)
    pw = jnp.stack(pw)
    if reverse:
        pw = pw[::-1]
    tabs = []
    for coef, s in ((a, 1), (a2, 2), (a4, 4)):
        live = (rows <= 7 - s) if reverse else (rows >= s)
        tabs.append(jnp.where(live[:, None], coef[None, :], 0.0))
    tabs.append(pw)
    tabs = jnp.stack(tabs)
    return _s5_blocked(jnp.real(tabs), jnp.imag(tabs))


def _s5_scan_tile(v, tab_ref, prev, reverse):
    lb = S5_LANE_BLOCK
    vr, vi = v[:, :lb], v[:, lb:]
    for idx, s in enumerate((1, 2, 4)):
        cr, ci = tab_ref[idx, :, :lb], tab_ref[idx, :, lb:]
        sh = 8 - s if reverse else s
        sr, si = pltpu.roll(vr, sh, 0), pltpu.roll(vi, sh, 0)
        vr, vi = vr + cr * sr - ci * si, vi + cr * si + ci * sr
    row = 0 if reverse else 7
    pr = jnp.broadcast_to(prev[row:row + 1, :lb], (8, lb))
    pi = jnp.broadcast_to(prev[row:row + 1, lb:], (8, lb))
    cr, ci = tab_ref[3, :, :lb], tab_ref[3, :, lb:]
    return jnp.concatenate([vr + cr * pr - ci * pi, vi + cr * pi + ci * pr], axis=1)


def _s5_prep(lam_re, lam_im, log_dt, b_re, b_im):
    lam = lax.complex(lam_re, lam_im)
    dt = jnp.exp(log_dt)[:, None]
    lam_bar = jnp.exp(lam * dt)
    b_bar = ((lam_bar - 1.0) / lam)[..., None] * lax.complex(b_re, b_im)
    return (jnp.real(lam_bar).reshape(-1), jnp.imag(lam_bar).reshape(-1), jnp.real(b_bar), jnp.imag(b_bar))


S5_NBLK = S5_NSTATE // S5_LANE_BLOCK
S5_BLK_GROUPS = S5_GROUPS // S5_NBLK
S5_BLK_CH = S5_BLK_GROUPS * S5_GROUP_CH


def _s5_in_matrix(bb_re, bb_im):
    eye = jnp.eye(S5_BLK_GROUPS, dtype=F32)
    def dense(bb):
        b4 = bb.reshape(S5_NBLK, S5_BLK_GROUPS, S5_STATE, S5_GROUP_CH)
        return jnp.einsum('cgph,gk->cghkp', b4, eye).reshape(S5_NBLK, S5_BLK_CH, S5_LANE_BLOCK)
    return jnp.concatenate([dense(bb_re), dense(bb_im)], axis=-1)


def _s5_block_diagonal(d):
    d5 = d.reshape(S5_NBLK, S5_BLK_GROUPS, S5_GROUP_CH, S5_BLK_GROUPS, S5_STATE)
    eye = jnp.eye(S5_BLK_GROUPS, dtype=F32)
    return jnp.swapaxes(jnp.sum(d5 * eye[None, :, None, :, None], axis=1), 1, 2)


def _s5_in_matrix_grad(d_mat):
    def diag(d):
        return jnp.swapaxes(_s5_block_diagonal(d), 2, 3).reshape(S5_GROUPS, S5_STATE, S5_GROUP_CH)
    return diag(d_mat[..., :S5_LANE_BLOCK]), diag(d_mat[..., S5_LANE_BLOCK:])


def _s5_out_matrix(c_re, c_im):
    eye = jnp.eye(S5_BLK_GROUPS, dtype=F32)
    def dense(cc):
        c4 = cc.reshape(S5_NBLK, S5_BLK_GROUPS, S5_GROUP_CH, S5_STATE)
        return jnp.einsum('cghp,gk->cgpkh', c4, eye).reshape(S5_NBLK, S5_LANE_BLOCK, S5_BLK_CH)
    return jnp.concatenate([dense(c_re), dense(-c_im)], axis=1)


def _s5_out_matrix_grad(d_mat_t):
    def diag(d):
        return _s5_block_diagonal(d).reshape(S5_GROUPS, S5_GROUP_CH, S5_STATE)
    return diag(d_mat_t[..., :S5_LANE_BLOCK]), -diag(d_mat_t[..., S5_LANE_BLOCK:])


def _gelu_parts(x):
    k = math.sqrt(2.0 / math.pi)
    inner = k * (x + 0.044715 * x * x * x)
    th = jnp.tanh(inner)
    return th, k * (1.0 + 3.0 * 0.044715 * x * x)


S5_CB_U = CB_U * (512 // S5_BLK_CH)


def _s5_direction_fwd(pg, b_mat, c_mat, tabs, reverse, *, name, side=None):
    length = pg.shape[0]
    tb = min(512, length)
    ntb = length // tb
    wb = 2 * S5_LANE_BLOCK
    ntile = tb // 8
    grid = (S5_NBLK, ntb)

    def body(*refs):
        _carried(side, grid, refs, 4, 3, 2, compute)

    def compute(refs):
        tab_ref, u_ref, b_ref, c_ref, x_ref, y_ref, ends_ref, carry_ref, bu_ref = refs

        @pl.when(pl.program_id(1) == 0)
        def _():
            carry_ref[...] = jnp.zeros_like(carry_ref)

        bu_ref[...] = jnp.dot(u_ref[...].astype(BF16), b_ref[0], preferred_element_type=F32)

        def step(i, prev):
            r0 = pl.multiple_of((ntile - 1 - i if reverse else i) * 8, 8)
            x = _s5_scan_tile(bu_ref[pl.ds(r0, 8), :], tab_ref, prev, reverse)
            x_ref[pl.ds(r0, 8), :] = x
            return x

        carry_ref[...] = lax.fori_loop(0, ntile, step, carry_ref[...])
        y_ref[...] = jnp.dot(x_ref[...].astype(BF16), c_ref[0], preferred_element_type=F32)
        ends_ref[0, 0:8, :] = x_ref[0:8, :]
        ends_ref[0, 8:16, :] = x_ref[tb - 8:tb, :]

    tix = (lambda t: ntb - 1 - t) if reverse else (lambda t: t)
    (xs, y, ends), gathered = _side_call(
        body, side, name=name, grid=grid,
        in_specs=[pl.BlockSpec((4, 8, wb), lambda c, t: (0, 0, c)),
                  pl.BlockSpec((tb, S5_BLK_CH), lambda c, t: (tix(t), S5_CB_U + c)),
                  pl.BlockSpec((1, S5_BLK_CH, wb), lambda c, t: (c, 0, 0)),
                  pl.BlockSpec((1, wb, S5_BLK_CH), lambda c, t: (c, 0, 0))],
        out_specs=[pl.BlockSpec((tb, wb), lambda c, t: (tix(t), c)), pl.BlockSpec((tb, S5_BLK_CH), lambda c, t: (tix(t), c)),
                   pl.BlockSpec((1, 16, wb), lambda c, t: (tix(t), 0, c))],
        out_shape=[jax.ShapeDtypeStruct((length, S5_NBLK * wb), F32), jax.ShapeDtypeStruct((length, S5_WIDTH), F32),
                   jax.ShapeDtypeStruct((ntb, 16, S5_NBLK * wb), F32)],
        scratch=[pltpu.VMEM((8, wb), F32), pltpu.VMEM((tb, wb), F32)], args=[tabs, pg, b_mat, c_mat],
        semantics=("parallel", "arbitrary"))
    return xs, y, ends, gathered


def _s5_direction_bwd(pg, dy, xs, ends, b_mat, c_mat, tabs_conj, reverse, *, name):
    length = pg.shape[0]
    tb = min(512, length)
    ntb = length // tb
    lb = S5_LANE_BLOCK
    wb = 2 * lb
    ntile = tb // 8
    adj_rev = not reverse
    if reverse:
        edge = jnp.concatenate([ends[1:, 0], jnp.zeros((1, xs.shape[1]), F32)], axis=0)
    else:
        edge = jnp.concatenate([jnp.zeros((1, xs.shape[1]), F32), ends[:-1, 15]], axis=0)
    edge = edge.reshape(ntb, 1, xs.shape[1])

    def body(tab_ref, u_ref, dy_ref, x_ref, edge_ref, b_ref, c_ref, du_ref, db_ref, dc_ref, da_ref, carry_ref, g_ref, lam_ref):
        @pl.when(pl.program_id(1) == 0)
        def _():
            carry_ref[...] = jnp.zeros_like(carry_ref)
            da_ref[...] = jnp.zeros_like(da_ref)
            db_ref[...] = jnp.zeros_like(db_ref)
            dc_ref[...] = jnp.zeros_like(dc_ref)

        dyb = dy_ref[...].astype(BF16)
        g_ref[...] = lax.dot_general(dyb, c_ref[0], _NT, preferred_element_type=F32)
        rows = lax.broadcasted_iota(jnp.int32, (8, wb), 0)

        def step(i, carry):
            prev, acc = carry
            k = ntile - 1 - i if adj_rev else i
            r0 = pl.multiple_of(k * 8, 8)
            lam = _s5_scan_tile(g_ref[pl.ds(r0, 8), :], tab_ref, prev, adj_rev)
            lam_ref[pl.ds(r0, 8), :] = lam
            x = x_ref[pl.ds(r0, 8), :]
            if reverse:
                kn = jnp.minimum(k + 1, ntile - 1)
                nb = x_ref[pl.ds(pl.multiple_of(kn * 8, 8), 8), :][0:1, :]
                nb = jnp.where(k == ntile - 1, edge_ref[0], nb)
                xp = jnp.where(rows == 7, jnp.broadcast_to(nb, (8, wb)), pltpu.roll(x, 7, 0))
            else:
                kn = jnp.maximum(k - 1, 0)
                nb = x_ref[pl.ds(pl.multiple_of(kn * 8, 8), 8), :][7:8, :]
                nb = jnp.where(k == 0, edge_ref[0], nb)
                xp = jnp.where(rows == 0, jnp.broadcast_to(nb, (8, wb)), pltpu.roll(x, 1, 0))
            xr, xi, lr, li = xp[:, :lb], xp[:, lb:], lam[:, :lb], lam[:, lb:]
            return lam, acc + jnp.concatenate([xr * lr + xi * li, xr * li - xi * lr], axis=1)

        last, acc = lax.fori_loop(0, ntile, step, (carry_ref[...], da_ref[...]))
        carry_ref[...] = last
        da_ref[...] = acc
        lamb = lam_ref[...].astype(BF16)
        du_ref[...] = lax.dot_general(lamb, b_ref[0], _NT, preferred_element_type=F32)
        db_ref[0] += lax.dot_general(u_ref[...].astype(BF16), lamb, _TN, preferred_element_type=F32)
        dc_ref[0] += lax.dot_general(dyb, x_ref[...].astype(BF16), _TN, preferred_element_type=F32)

    tix = (lambda t: ntb - 1 - t) if adj_rev else (lambda t: t)
    wide = pl.BlockSpec((tb, wb), lambda c, t: (tix(t), c))
    mat = pl.BlockSpec((1, S5_BLK_CH, wb), lambda c, t: (c, 0, 0))
    return pl.pallas_call(
        body, name=name, grid=(S5_NBLK, ntb),
        in_specs=[pl.BlockSpec((4, 8, wb), lambda c, t: (0, 0, c)),
                  pl.BlockSpec((tb, S5_BLK_CH), lambda c, t: (tix(t), S5_CB_U + c)),
                  pl.BlockSpec((tb, S5_BLK_CH), lambda c, t: (tix(t), c)), wide,
                  pl.BlockSpec((1, 1, wb), lambda c, t: (tix(t), 0, c)), mat,
                  pl.BlockSpec((1, wb, S5_BLK_CH), lambda c, t: (c, 0, 0))],
        out_specs=[pl.BlockSpec((tb, S5_BLK_CH), lambda c, t: (tix(t), c)), mat, mat, pl.BlockSpec((8, wb), lambda c, t: (0, c))],
        out_shape=[jax.ShapeDtypeStruct((length, S5_WIDTH), F32), jax.ShapeDtypeStruct((S5_NBLK, S5_BLK_CH, wb), F32),
                   jax.ShapeDtypeStruct((S5_NBLK, S5_BLK_CH, wb), F32), jax.ShapeDtypeStruct((8, S5_NBLK * wb), F32)],
        scratch_shapes=[pltpu.VMEM((8, wb), F32), pltpu.VMEM((tb, wb), F32), pltpu.VMEM((tb, wb), F32)],
        compiler_params=pltpu.CompilerParams(dimension_semantics=("parallel", "arbitrary")),
    )(tabs_conj, pg, dy, xs, edge, b_mat, c_mat)


def _both(fn):
    return jax.vmap(jax.vmap(fn))


def _s5_setup(w):
    a_re, a_im, bb_re, bb_im = _both(_s5_prep)(w['s5_lambda_re'], w['s5_lambda_im'], w['s5_log_dt'], w['s5_b_re'], w['s5_b_im'])

    def tables(d, conj, reverse):
        return jax.vmap(lambda r, i: _s5_tables(r, -i if conj else i, reverse))(a_re[:, d], a_im[:, d])
    return {'b_mat': _both(_s5_in_matrix)(bb_re, bb_im).astype(BF16),
            'c_mat': _both(_s5_out_matrix)(w['s5_c_re'], w['s5_c_im']).astype(BF16),
            'tabs': [tables(0, False, False), tables(1, False, True)],
            'tabs_adj': [tables(0, True, True), tables(1, True, False)]}


def _s5_param_grads(w, raws):
    def stacked(k):
        return jnp.stack([jnp.stack([raws[i][d][k] for d in range(2)]) for i in range(DEPTH)])
    dbb_re, dbb_im = _both(_s5_in_matrix_grad)(stacked(0))
    dc_re, dc_im = _both(_s5_out_matrix_grad)(stacked(1))
    da_re, da_im = _s5_unblocked(jnp.sum(stacked(2), axis=2))
    _, vjp = jax.vjp(_both(_s5_prep), w['s5_lambda_re'], w['s5_lambda_im'], w['s5_log_dt'], w['s5_b_re'], w['s5_b_im'])
    g = vjp((da_re, da_im, dbb_re, dbb_im))
    return {'s5_lambda_re': g[0], 's5_lambda_im': g[1], 's5_log_dt': g[2], 's5_b_re': g[3], 's5_b_im': g[4],
            's5_c_re': dc_re, 's5_c_im': dc_im}


def _s5_fwd(p_in, prm, w_glu, side=None):
    dirs = []
    ys = []
    gathered = []
    for d, reverse in ((0, False), (1, True)):
        xs, y_dir, ends, got = _s5_direction_fwd(p_in, prm['b_mat'][d], prm['c_mat'][d], prm['tabs'][d], reverse,
                                                 name="s5_fwd_rev" if reverse else "s5_fwd", side=None if reverse else side)
        gathered += got
        ys.append(y_dir)
        dirs.append((xs, ends))

    def post(yf, yb, u, dskip):
        ypre = yf + yb + dskip * u
        th, _ = _gelu_parts(ypre)
        return ypre, 0.5 * ypre * (1.0 + th)
    ypre, yg = _rowmap(post, [ys[0], ys[1], (p_in, CB_U, S5_WIDTH)], [prm['d'].reshape(1, -1)],
                       [(S5_WIDTH, F32), (S5_WIDTH, F32)], tl=512, name="s5_post")
    t = _mm(yg, w_glu, name="s5_glu_mm")

    def glu(ygv, tv):
        return ygv * _sigmoid(tv)
    y = _rowmap(glu, [yg, t], [], [(S5_WIDTH, BF16)], tl=512, name="s5_glu")[0]
    return y, (dirs, ypre, yg, t), gathered


def _s5_bwd(pg, prm, w_glu, saved, dy):
    dirs, ypre, yg, t = saved

    def glu_bwd(dyv, ygv, tv):
        s = _sigmoid(tv)
        return dyv * ygv * s * (1.0 - s), dyv * s
    dt, dyg_direct = _rowmap(glu_bwd, [dy, yg, t], [], [(S5_WIDTH, BF16), (S5_WIDTH, F32)], tl=512, name="s5_glu_bwd")
    grads = {'w_glu': _mm(yg, dt, ta=True, out_dtype=BF16, name="s5_dwglu")}
    dyg_mm = _mm(dt, w_glu, tb=True, name="s5_dyg")

    def post_bwd(dyd, dym, yp, u, dskip):
        th, dinner = _gelu_parts(yp)
        dyp = (dyd + dym) * (0.5 * (1.0 + th) + 0.5 * yp * (1.0 - th * th) * dinner)
        return dyp, dyp * dskip, jnp.sum(dyp * u, axis=0, keepdims=True)
    dyp, du_skip, dd = _rowmap(post_bwd, [dyg_direct, dyg_mm, ypre, (pg, CB_U, S5_WIDTH)], [prm['d'].reshape(1, -1)],
                               [(S5_WIDTH, F32), (S5_WIDTH, F32)], [(1, S5_WIDTH)], tl=512, name="s5_post_bwd")
    grads['d'] = dd[0]
    du = [du_skip]
    grads['raw'] = []
    for d, reverse in ((0, False), (1, True)):
        du_dir, d_bmat, d_cmat_t, da = _s5_direction_bwd(pg, dyp, *dirs[d], prm['b_mat'][d], prm['c_mat'][d], prm['tabs_adj'][d],
                                                         reverse, name="s5_bwd_rev" if reverse else "s5_bwd")
        du.append(du_dir)
        grads['raw'].append((d_bmat, d_cmat_t, da))
    return du, grads


def _split3(x):
    hi = x.astype(BF16)
    r = x - hi.astype(F32)
    mid = r.astype(BF16)
    return hi, mid, (r - mid.astype(F32)).astype(BF16)


def _exact_dot(ones, x, dims):
    parts = [lax.dot_general(ones, p, dims, preferred_element_type=F32) for p in _split3(x)]
    return parts[0] + parts[1] + parts[2]


_NN = (((1,), (0,)), ((), ()))
_NT = (((1,), (1,)), ((), ()))
_TN = (((0,), (0,)), ((), ()))


def _dot(a, b, dims=_NN):
    return lax.dot_general(a.astype(BF16), b.astype(BF16), dims, preferred_element_type=F32)


def _gla_chunk_mask(reverse):
    rows = lax.broadcasted_iota(jnp.int32, (GLA_CHUNK, GLA_CHUNK), 0)
    cols = lax.broadcasted_iota(jnp.int32, (GLA_CHUNK, GLA_CHUNK), 1)
    return (cols >= rows) if reverse else (cols <= rows)


def _gla_fwd(pg, la, reverse, *, name):
    length = la.shape[0]
    nch = length // GLA_CHUNK
    scale = GLA_HEAD_DIM ** -0.5
    last = 0 if reverse else GLA_CHUNK - 1
    hd = GLA_HEAD_DIM

    def body(q_ref, k_ref, v_ref, la_ref, o_ref, sp_ref, st_ref):
        @pl.when(pl.program_id(0) == 0)
        def _():
            st_ref[...] = jnp.zeros_like(st_ref)

        mask = _gla_chunk_mask(reverse)
        b = _exact_dot(mask.astype(BF16), la_ref[...], _NN)
        sp_ref[0] = st_ref[...]
        outs = []
        for h in range(GLA_HEADS):
            sl = slice(h * hd, (h + 1) * hd)
            bh = b[:, sl]
            bl = bh[last:last + 1, :]
            k = k_ref[:, sl]
            v = v_ref[:, sl]
            qd = q_ref[:, sl] * scale * jnp.exp(bh)
            kd = k * jnp.exp(-bh)
            ke = k * jnp.exp(bl - bh)
            st = st_ref[sl, :]
            p = jnp.where(mask, _dot(qd, kd, _NT), 0.0)
            outs.append(_dot(p, v) + _dot(qd, st, _NT))
            st_ref[sl, :] = st * jnp.exp(bl) + _dot(v, ke, _TN)
        o_ref[...] = jnp.concatenate(outs, axis=1)

    cmap = (lambda n: nch - 1 - n) if reverse else (lambda n: n)
    col = lambda cb: pl.BlockSpec((GLA_CHUNK, GLA_WIDTH), lambda n, cb=cb: (cmap(n), cb))
    return pl.pallas_call(
        body, name=name, grid=(nch,),
        in_specs=[col(CB_GQ), col(CB_GK), col(CB_GV), col(0)],
        out_specs=[col(0), pl.BlockSpec((1, GLA_WIDTH, hd), lambda n: (cmap(n), 0, 0))],
        out_shape=[jax.ShapeDtypeStruct((length, GLA_WIDTH), F32), jax.ShapeDtypeStruct((nch, GLA_WIDTH, hd), F32)],
        scratch_shapes=[pltpu.VMEM((GLA_WIDTH, hd), F32)],
        compiler_params=pltpu.CompilerParams(dimension_semantics=("arbitrary",)),
    )(pg, pg, pg, la)


def _gla_bwd(pg, la, do, sprev, reverse, *, name):
    length = la.shape[0]
    nch = length // GLA_CHUNK
    scale = GLA_HEAD_DIM ** -0.5
    last = 0 if reverse else GLA_CHUNK - 1
    hd = GLA_HEAD_DIM

    def body(q_ref, k_ref, v_ref, la_ref, do_ref, sp_ref, dq_ref, dk_ref, dv_ref, dla_ref, dst_ref):
        @pl.when(pl.program_id(0) == 0)
        def _():
            dst_ref[...] = jnp.zeros_like(dst_ref)

        mask = _gla_chunk_mask(reverse)
        tri = mask.astype(BF16)
        b = _exact_dot(tri, la_ref[...], _NN)
        is_last = lax.broadcasted_iota(jnp.int32, (GLA_CHUNK, hd), 0) == last
        dqs, dks, dvs, dbs = [], [], [], []
        for h in range(GLA_HEADS):
            sl = slice(h * hd, (h + 1) * hd)
            bh = b[:, sl]
            bl = bh[last:last + 1, :]
            eb, enb, ebl, el = jnp.exp(bh), jnp.exp(-bh), jnp.exp(bl - bh), jnp.exp(bl)
            k = k_ref[:, sl]
            v = v_ref[:, sl]
            dov = do_ref[:, sl]
            qd = q_ref[:, sl] * scale * eb
            kd = k * enb
            ke = k * ebl
            st = sp_ref[0, sl, :]
            dst = dst_ref[sl, :]
            p = jnp.where(mask, _dot(qd, kd, _NT), 0.0)
            dp = jnp.where(mask, _dot(dov, v, _NT), 0.0)
            dqd = _dot(dp, kd) + _dot(dov, st)
            dkd = _dot(dp, qd, _TN)
            dvs.append(_dot(p, dov, _TN) + _dot(ke, dst, _NT))
            dke = _dot(v, dst)
            dst_ref[sl, :] = dst * el + _dot(dov, qd, _TN)
            dbl = el * jnp.sum(dst * st, axis=0, keepdims=True) + jnp.sum(dke * ke, axis=0, keepdims=True)
            db = dqd * qd - dkd * kd - dke * ke
            dbs.append(jnp.where(is_last, db + dbl, db))
            dqs.append(dqd * eb * scale)
            dks.append(dkd * enb + dke * ebl)
        dq_ref[...] = jnp.concatenate(dqs, axis=1)
        dk_ref[...] = jnp.concatenate(dks, axis=1)
        dv_ref[...] = jnp.concatenate(dvs, axis=1)
        tri_t = _gla_chunk_mask(not reverse).astype(BF16)
        dla_ref[...] = _exact_dot(tri_t, jnp.concatenate(dbs, axis=1), _NN)

    cmap = (lambda n: n) if reverse else (lambda n: nch - 1 - n)
    col = lambda cb: pl.BlockSpec((GLA_CHUNK, GLA_WIDTH), lambda n, cb=cb: (cmap(n), cb))
    wide = jax.ShapeDtypeStruct((length, GLA_WIDTH), F32)
    return pl.pallas_call(
        body, name=name, grid=(nch,),
        in_specs=[col(CB_GQ), col(CB_GK), col(CB_GV), col(0), col(0),
                  pl.BlockSpec((1, GLA_WIDTH, hd), lambda n: (cmap(n), 0, 0))],
        out_specs=[col(0)] * 4, out_shape=[wide] * 4,
        scratch_shapes=[pltpu.VMEM((GLA_WIDTH, hd), F32)],
        compiler_params=pltpu.CompilerParams(dimension_semantics=("arbitrary",)),
    )(pg, pg, pg, la, do, sprev)


def _log_sigmoid(x):
    return jnp.minimum(x, 0.0) - jnp.log(1.0 + jnp.exp(-jnp.abs(x)))


def _gla_alpha_padded(w_alpha):
    w = jnp.zeros((2, 128, GLA_WIDTH), w_alpha.dtype)
    w = w.at[0, 0:GLA_LOWRANK].set(w_alpha[0])
    return w.at[1, GLA_LOWRANK:2 * GLA_LOWRANK].set(w_alpha[1])


def _gla_branch_fwd(pg, w_alpha, b_alpha, norm_gain):
    wa = _gla_alpha_padded(w_alpha).astype(BF16)

    def gates(z, w, bias):
        return (_log_sigmoid(_dot(z, w[0]) + bias[0:1]) / GLA_TAU, _log_sigmoid(_dot(z, w[1]) + bias[1:2]) / GLA_TAU)
    la_f, la_b = _rowmap(gates, [(pg, CB_Z, 128)], [wa, b_alpha], [(GLA_WIDTH, F32), (GLA_WIDTH, F32)], tl=512,
                         name="gla_gates")
    o_f, sp_f = _gla_fwd(pg, la_f, False, name="gla_fwd")
    o_b, sp_b = _gla_fwd(pg, la_b, True, name="gla_fwd_rev")

    def post(of, ob, gate, gn):
        o = of + ob
        on = jnp.concatenate([o[:, s:s + GLA_HEAD_DIM] * _rms(o[:, s:s + GLA_HEAD_DIM]) * gn
                              for s in range(0, GLA_WIDTH, GLA_HEAD_DIM)], axis=1)
        return o, on * (gate * _sigmoid(gate))
    o, y = _rowmap(post, [o_f, o_b, (pg, CB_GG, GLA_WIDTH)], [norm_gain.reshape(1, -1)],
                   [(GLA_WIDTH, F32), (GLA_WIDTH, BF16)], tl=512, name="gla_post")
    return y, (wa, la_f, la_b, sp_f, sp_b, o)


def _gla_branch_bwd(pg, w_alpha, b_alpha, norm_gain, saved, dy):
    wa, la_f, la_b, sp_f, sp_b, o = saved

    def post_bwd(dyv, ov, gate, gn):
        s = _sigmoid(gate)
        dos, dgn, ons = [], [], []
        for c in range(0, GLA_WIDTH, GLA_HEAD_DIM):
            oh = ov[:, c:c + GLA_HEAD_DIM]
            r = _rms(oh)
            don = dyv[:, c:c + GLA_HEAD_DIM] * (gate[:, c:c + GLA_HEAD_DIM] * s[:, c:c + GLA_HEAD_DIM])
            gd = don * gn
            dos.append(r * gd - oh * (r * r * r) * jnp.mean(oh * gd, axis=-1, keepdims=True))
            dgn.append(jnp.sum(don * oh * r, axis=0, keepdims=True))
            ons.append(oh * r * gn)
        on = jnp.concatenate(ons, axis=1)
        dgate = dyv * on * (s * (1.0 + gate * (1.0 - s)))
        return jnp.concatenate(dos, axis=1), dgate, jnp.concatenate(dgn, axis=1)
    do, dgate, dgn = _rowmap(post_bwd, [dy, o, (pg, CB_GG, GLA_WIDTH)], [norm_gain.reshape(1, -1)],
                             [(GLA_WIDTH, F32), (GLA_WIDTH, F32)], [(1, GLA_WIDTH)], tl=512, name="gla_post_bwd")
    dq_f, dk_f, dv_f, dla_f = _gla_bwd(pg, la_f, do, sp_f, False, name="gla_bwd")
    dq_b, dk_b, dv_b, dla_b = _gla_bwd(pg, la_b, do, sp_b, True, name="gla_bwd_rev")

    def gates_bwd(z, dlf, dlb, w, bias):
        dz = jnp.zeros_like(z)
        dlogits, dbs = [], []
        for d, dl in ((0, dlf), (1, dlb)):
            logit = _dot(z, w[d]) + bias[d:d + 1]
            dlogit = dl * (1.0 / GLA_TAU) * jnp.exp(_log_sigmoid(-logit))
            dz = dz + _dot(dlogit, w[d], _NT)
            dlogits.append(dlogit)
            dbs.append(jnp.sum(dlogit, axis=0, keepdims=True))
        return dz, dlogits[0], dlogits[1], dbs[0], dbs[1]
    dz, dlg_f, dlg_b, dba_f, dba_b = _rowmap(
        gates_bwd, [(pg, CB_Z, 128), dla_f, dla_b], [wa, b_alpha], [(128, F32), (GLA_WIDTH, BF16), (GLA_WIDTH, BF16)],
        [(1, GLA_WIDTH), (1, GLA_WIDTH)], tl=512, name="gla_gates_bwd")
    dwa_f = _mm(dlg_f, (pg, CB_Z, 128), ta=True, name="gla_dwalpha")
    dwa_b = _mm(dlg_b, (pg, CB_Z, 128), ta=True, name="gla_dwalpha")
    grads = {'w_alpha': jnp.stack([dwa_f[:, 0:GLA_LOWRANK].T, dwa_b[:, GLA_LOWRANK:2 * GLA_LOWRANK].T]),
             'b_alpha': jnp.concatenate([dba_f, dba_b], axis=0),
             'norm': jnp.sum(dgn.reshape(GLA_HEADS, GLA_HEAD_DIM), axis=0)}
    return [dq_f, dq_b], [dk_f, dk_b], [dv_f, dv_b], dgate, dz, grads


def _rope_tables(length):
    half = ATTN_HEAD_DIM // 2
    inv_freq = ROPE_BASE ** (-jnp.arange(half // 2, dtype=F32) * 2.0 / half)
    t = jnp.arange(length, dtype=jnp.int32)
    def one(pos):
        ang = pos.astype(F32)[:, None] * inv_freq[None, :]
        c, s = jnp.cos(ang), jnp.sin(ang)
        return jnp.concatenate([c, c], axis=1), jnp.concatenate([-s, s], axis=1)
    c_r, s_r = one(t // GRID_W)
    c_c, s_c = one(t % GRID_W)
    return jnp.concatenate([c_r, c_c], axis=1), jnp.concatenate([s_r, s_c], axis=1)


def _rope_swap(y):
    w = y.shape[1]
    lane = lax.broadcasted_iota(jnp.int32, y.shape, 1)
    return jnp.where(lane % 32 < 16, pltpu.roll(y, w - 16, 1), pltpu.roll(y, 16, 1))


def _head_sums(x, ones):
    parts = [lax.dot_general(p, ones, _NN, preferred_element_type=F32) for p in _split3(x)]
    return parts[0] + parts[1] + parts[2]


def _head_ones(width):
    seg = np.arange(width) // ATTN_HEAD_DIM
    return jnp.asarray(seg[:, None] == seg[None, :], BF16)


def _qk_prep_fwd(pg, cb, width, gain, cos, sin, scale, *, name):
    heads = width // ATTN_HEAD_DIM
    def fn(x, c, s, g, ones):
        r = lax.rsqrt(_head_sums(x * x, ones) * (1.0 / ATTN_HEAD_DIM) + NORM_EPS)
        y = x * r * g
        return (y * c + _rope_swap(y) * s) * scale
    return _rowmap(fn, [(pg, cb, width), jnp.tile(cos, (1, heads)), jnp.tile(sin, (1, heads))],
                   [jnp.tile(gain, heads).reshape(1, -1), _head_ones(width)], [(width, BF16)], tl=512, name=name)[0]


def _qk_prep_bwd(pg, cb, width, gain, cos, sin, scale, dout, *, name):
    heads = width // ATTN_HEAD_DIM
    def fn(x, dov, c, s, g, ones):
        r = lax.rsqrt(_head_sums(x * x, ones) * (1.0 / ATTN_HEAD_DIM) + NORM_EPS)
        dos = dov * scale
        dy = dos * c + _rope_swap(dos * s)
        gd = dy * g
        dx = r * gd - x * (r * r * r) * (_head_sums(x * gd, ones) * (1.0 / ATTN_HEAD_DIM))
        return dx, jnp.sum(dy * x * r, axis=0, keepdims=True)
    dx, dg = _rowmap(fn, [(pg, cb, width), dout, jnp.tile(cos, (1, heads)), jnp.tile(sin, (1, heads))],
                     [jnp.tile(gain, heads).reshape(1, -1), _head_ones(width)], [(width, F32)], [(1, width)], tl=512,
                     name=name)
    return dx, jnp.sum(dg.reshape(heads, ATTN_HEAD_DIM), axis=0)


def _to_heads(x, heads):
    return jnp.transpose(x.reshape(x.shape[0], heads, ATTN_HEAD_DIM), (1, 0, 2))


def _from_heads(x):
    return jnp.transpose(x, (1, 0, 2)).reshape(x.shape[1], x.shape[0] * ATTN_HEAD_DIM)


ATTN_GROUP = ATTN_Q_HEADS // ATTN_KV_HEADS
ATTN_TQ = 256


def _attn_fwd(q, k, v, side=None):
    length = q.shape[1]
    tq = min(ATTN_TQ, length)
    grid = (ATTN_KV_HEADS, length // tq)

    def compute(refs):
        q_ref, k_ref, v_ref, o_ref = refs
        kk, vv = k_ref[0], v_ref[0]
        for g in range(ATTN_GROUP):
            s = _dot(q_ref[g], kk, _NT)
            p = jnp.exp(s - jnp.max(s, axis=-1, keepdims=True))
            o_ref[g] = _dot(p, vv) / jnp.sum(p, axis=-1, keepdims=True)

    def body(*refs):
        _carried(side, grid, refs, 3, 1, 0, compute)

    kv_spec = pl.BlockSpec((1, length, ATTN_HEAD_DIM), lambda h, i: (h, 0, 0))
    q_spec = pl.BlockSpec((ATTN_GROUP, tq, ATTN_HEAD_DIM), lambda h, i: (h, i, 0))
    (out,), gathered = _side_call(
        body, side, name="attn_fwd", grid=grid, in_specs=[q_spec, kv_spec, kv_spec], out_specs=[q_spec],
        out_shape=[jax.ShapeDtypeStruct(q.shape, F32)], scratch=[], args=[q, k, v], semantics=("parallel", "parallel"))
    return out, gathered


def _attn_bwd(q, k, v, o, do, side=None):
    length = q.shape[1]
    tq = min(ATTN_TQ, length)
    grid = (ATTN_KV_HEADS, length // tq)

    def body(*refs):
        _carried(side, grid, refs, 5, 3, 0, compute)

    def compute(refs):
        q_ref, k_ref, v_ref, o_ref, do_ref, dq_ref, dk_ref, dv_ref = refs

        @pl.when(pl.program_id(1) == 0)
        def _():
            dk_ref[...] = jnp.zeros_like(dk_ref)
            dv_ref[...] = jnp.zeros_like(dv_ref)

        kk, vv = k_ref[0], v_ref[0]
        for g in range(ATTN_GROUP):
            qg, dog = q_ref[g], do_ref[g]
            s = _dot(qg, kk, _NT)
            p = jnp.exp(s - jnp.max(s, axis=-1, keepdims=True))
            p = p * (1.0 / jnp.sum(p, axis=-1, keepdims=True))
            dp = _dot(dog, vv, _NT)
            ds = p * (dp - jnp.sum(dog * o_ref[g], axis=-1, keepdims=True))
            dq_ref[g] = _dot(ds, kk)
            dk_ref[0] += _dot(ds, qg, _TN)
            dv_ref[0] += _dot(p, dog, _TN)

    kv_spec = pl.BlockSpec((1, length, ATTN_HEAD_DIM), lambda h, i: (h, 0, 0))
    q_spec = pl.BlockSpec((ATTN_GROUP, tq, ATTN_HEAD_DIM), lambda h, i: (h, i, 0))
    return _side_call(
        body, side, name="attn_bwd", grid=grid, in_specs=[q_spec, kv_spec, kv_spec, q_spec, q_spec],
        out_specs=[q_spec, kv_spec, kv_spec],
        out_shape=[jax.ShapeDtypeStruct(q.shape, F32), jax.ShapeDtypeStruct(k.shape, F32), jax.ShapeDtypeStruct(k.shape, F32)],
        scratch=[], args=[q, k, v, o, do], semantics=("parallel", "arbitrary"))


def _attn_branch_fwd(pg, q_gain, k_gain, side=None):
    cos, sin = _rope_tables(pg.shape[0])
    qp = _qk_prep_fwd(pg, CB_AQ, ATTN_WIDTH, q_gain, cos, sin, ATTN_HEAD_DIM ** -0.5, name="attn_q_prep")
    kp = _qk_prep_fwd(pg, CB_AK, ATTN_KV_WIDTH, k_gain, cos, sin, 1.0, name="attn_k_prep")
    qh, kh = _to_heads(qp, ATTN_Q_HEADS), _to_heads(kp, ATTN_KV_HEADS)
    vh = _to_heads(pg[:, P_OFF + 3200:P_OFF + 3328].astype(BF16), ATTN_KV_HEADS)
    oh, gathered = _attn_fwd(qh, kh, vh, side)
    return _from_heads(oh).astype(BF16), (cos, sin, qh, kh, vh, oh), gathered


def _attn_branch_bwd(pg, q_gain, k_gain, saved, dy, side=None):
    cos, sin, qh, kh, vh, oh = saved
    (dqh, dkh, dvh), carried = _attn_bwd(qh, kh, vh, oh, _to_heads(dy, ATTN_Q_HEADS), side)
    dq, dqg = _qk_prep_bwd(pg, CB_AQ, ATTN_WIDTH, q_gain, cos, sin, ATTN_HEAD_DIM ** -0.5, _from_heads(dqh),
                           name="attn_q_prep_bwd")
    dk, dkg = _qk_prep_bwd(pg, CB_AK, ATTN_KV_WIDTH, k_gain, cos, sin, 1.0, _from_heads(dkh), name="attn_k_prep_bwd")
    return dq, dk, _from_heads(dvh), {'q_norm': dqg, 'k_norm': dkg}, carried


def _gate_cols():
    return [slice(i * D_MODEL, (i + 1) * D_MODEL) for i in range(3)]


def _mixer_fwd(x, lw, side_in=None, after_in=None, side_s5=None, side_attn=None):
    h = _rmsnorm_fwd(x, lw['mix_norm'])
    if side_in is None:
        pg = _mm(h, lw['w_pg'], name="mix_in")
    else:
        pg, got_in = _mm(h, lw['w_pg'], side=side_in, name="mix_in")
        after_in(got_in)
    y_s5, s_s5, got_s5 = _s5_fwd(pg, lw['s5'], lw['s5_w_glu'], side_s5)
    y_gla, s_gla = _gla_branch_fwd(pg, lw['gla_w_alpha'], lw['gla_b_alpha'], lw['gla_norm'])
    y_att, s_att, got_attn = _attn_branch_fwd(pg, lw['attn_q_norm'], lw['attn_k_norm'], side_attn)
    ys = (y_s5, y_gla, y_att)
    br = [_mm(y, lw[n], name="mix_branch") for y, n in zip(ys, ('w_branch_s5', 'w_branch_gla', 'w_branch_attn'))]

    def merge(g0, g1, g2, b0, b1, b2, bias):
        acc = None
        for g, b, c in zip((g0, g1, g2), (b0, b1, b2), _gate_cols()):
            term = _sigmoid(g + bias[:, c]) * b
            acc = term if acc is None else acc + term
        return acc
    merged = _rowmap(merge, [(pg, 0, D_MODEL), (pg, 1, D_MODEL), (pg, 2, D_MODEL)] + br,
                     [lw['b_merge_gate'].reshape(1, -1)], [(D_MODEL, BF16)], tl=256, name="mix_merge")[0]
    x_out = _mm(merged, lw['w_out'], add=x, name="mix_out")
    return x_out, (x, h, pg, ys, (s_s5, s_gla, s_att), br, merged), (got_s5, got_attn)


def _mixer_bwd(saved, lw, dx_out, side=None):
    x, h, pg, ys, (s_s5, s_gla, s_att), br, merged = saved
    grads = {'w_out': _mm(merged, dx_out, ta=True, out_dtype=BF16, name="mix_dwout")}
    dmerged = _mm(dx_out, lw['w_out'], tb=True, name="mix_dmerged")

    def merge_bwd(g0, g1, g2, b0, b1, b2, dm, bias):
        dbr, dgp = [], []
        for g, b, c in zip((g0, g1, g2), (b0, b1, b2), _gate_cols()):
            s = _sigmoid(g + bias[:, c])
            dbr.append(dm * s)
            dgp.append(dm * b * (s * (1.0 - s)))
        dgp = jnp.concatenate(dgp, axis=1)
        return dbr[0], dbr[1], dbr[2], dgp, jnp.sum(dgp, axis=0, keepdims=True)
    d0, d1, d2, dgpre, dbias = _rowmap(
        merge_bwd, [(pg, 0, D_MODEL), (pg, 1, D_MODEL), (pg, 2, D_MODEL)] + br + [dmerged],
        [lw['b_merge_gate'].reshape(1, -1)], [(D_MODEL, BF16)] * 3 + [(GATE_WIDTH, BF16)], [(1, GATE_WIDTH)], tl=256,
        name="mix_merge_bwd")
    grads['b_merge_gate'] = dbias[0]
    dys = []
    for y, dbr, n in zip(ys, (d0, d1, d2), ('w_branch_s5', 'w_branch_gla', 'w_branch_attn')):
        grads[n] = _mm(y, dbr, ta=True, out_dtype=BF16, name="mix_dwbranch")
        dys.append(_mm(dbr, lw[n], tb=True, name="mix_dy"))
    du, g_s5 = _s5_bwd(pg, lw['s5'], lw['s5_w_glu'], s_s5, dys[0])
    dgq, dgk, dgv, dgg, dz, g_gla = _gla_branch_bwd(pg, lw['gla_w_alpha'], lw['gla_b_alpha'], lw['gla_norm'], s_gla, dys[1])
    daq, dak, dav, g_att, carried = _attn_branch_bwd(pg, lw['attn_q_norm'], lw['attn_k_norm'], s_att, dys[2], side)

    def assemble(dgp, u0, u1, u2, q0, q1, k0, k1, v0, v1, gg, aq, ak, av, z):
        pad = jnp.zeros((dgp.shape[0], IN_PAD - 3456), F32)
        parts = [dgp.astype(F32), u0 + u1 + u2, q0 + q1, k0 + k1, v0 + v1, gg, aq, ak, av, z, pad]
        return jnp.concatenate(parts, axis=1)
    dpg = _rowmap(assemble, [dgpre] + du + dgq + dgk + dgv + [dgg, daq, dak, dav, dz], [], [(PG_WIDTH, BF16)], tl=256,
                  name="mix_dpg")[0]
    grads['w_pg'] = _mm(h, dpg, ta=True, out_dtype=BF16, name="mix_dwpg")
    dh = _mm(dpg, lw['w_pg'], tb=True, name="mix_dh")
    dx, grads['mix_norm'] = _rmsnorm_bwd(x, lw['mix_norm'], dh, dx_out)
    grads['s5'], grads['gla'], grads['attn'] = g_s5, g_gla, g_att
    return dx, grads, carried


def _loss_head(x, gain, target):
    width = x.shape[1]

    def fn(xv, tv, g):
        r = _rms(xv)
        err = xv * r * g - tv
        dy = err * (1.0 / width)
        gd = dy * g
        dx = r * gd - xv * (r * r * r) * jnp.mean(xv * gd, axis=-1, keepdims=True)
        loss = jnp.sum(0.5 * jnp.mean(err * err, axis=-1, keepdims=True), axis=0, keepdims=True)
        return dx, jnp.broadcast_to(loss, (1, 128)), jnp.sum(dy * xv * r, axis=0, keepdims=True)
    dx, loss, dgain = _rowmap(fn, [x, target], [gain.reshape(1, -1)], [(width, F32)], [(1, 128), (1, width)], tl=256,
                              name="loss_head")
    return loss[0, 0], dx, dgain[0]


def _row_tile(rows, cap=256):
    for t in range(cap - cap % 16, 0, -16):
        if rows % t == 0:
            return t
    return rows


def _reduce_adamw(parts, w, m, v, *, name):
    r, c = w.shape
    if len(parts) > 1 and parts[0].shape[1] % 8:
        parts = [jnp.concatenate(parts, axis=1)]
    nparts, rows = parts[0].shape[0], parts[0].shape[1]
    tr = _row_tile(rows)
    per = rows // tr

    def body(*refs):
        p_refs, (w_ref, m_ref, v_ref, g_ref, d_ref, m2_ref, v2_ref) = refs[:len(parts)], refs[len(parts):]
        g = None
        for k, p_ref in enumerate(p_refs):
            gk = p_ref[0].astype(F32)
            for j in range(1, nparts):
                gk = gk + p_ref[j].astype(F32)
            g = gk if g is None else jnp.where(pl.program_id(0) // per == k, gk, g)
        m2 = ADAM_B1 * m_ref[...] + (1.0 - ADAM_B1) * g
        v2 = ADAM_B2 * v_ref[...] + (1.0 - ADAM_B2) * (g * g)
        m_hat = m2 / (1.0 - ADAM_B1 ** ADAM_STEP)
        v_hat = v2 / (1.0 - ADAM_B2 ** ADAM_STEP)
        g_ref[...] = g
        d_ref[...] = -ADAM_LR * (m_hat / (jnp.sqrt(v_hat) + ADAM_EPS) + ADAM_WD * w_ref[...])
        m2_ref[...] = m2
        v2_ref[...] = v2

    flat = pl.BlockSpec((tr, c), lambda i: (i, 0))
    p_specs = [pl.BlockSpec((nparts, tr, c), lambda i, k=k: (0, jnp.clip(i - k * per, 0, per - 1), 0)) for k in range(len(parts))]
    return pl.pallas_call(
        body, name=name, grid=(r // tr,), in_specs=p_specs + [flat, flat, flat],
        out_specs=[flat] * 4, out_shape=[jax.ShapeDtypeStruct((r, c), F32)] * 4,
        compiler_params=pltpu.CompilerParams(dimension_semantics=("parallel",)),
    )(*parts, w, m, v)


def _all_gather(blocks, *, name):
    side = _SideGather(blocks)

    def body(*refs):
        start, finish = side.hooks(refs)
        start()
        finish()

    return pl.pallas_call(body, name=name, out_shape=side.out_shape, in_specs=side.in_specs, out_specs=side.out_specs,
                          scratch_shapes=side.scratch)(*blocks)


class _SideGather:
    def __init__(self, blocks):
        self.blocks = list(blocks)
        self.n = n = len(self.blocks)
        hbm = pl.BlockSpec(memory_space=pl.ANY)
        self.in_specs, self.out_specs = [hbm] * n, [hbm] * n
        self.out_shape = [jax.ShapeDtypeStruct((N_DEV,) + b.shape, b.dtype) for b in self.blocks]
        self.scratch = [pltpu.SemaphoreType.DMA((n, 7)), pltpu.SemaphoreType.DMA((n, 7)), pltpu.SemaphoreType.DMA((n,))]

    def hooks(self, refs):
        n = self.n
        x_refs, out_refs = refs[:n], refs[n:2 * n]
        send_sems, recv_sems, local_sems = refs[2 * n:]
        x, y, c = lax.axis_index("x"), lax.axis_index("y"), lax.axis_index("c")
        me, sibling = (x, y, c), (x, y, 1 - c)
        chips = [(1 - x, y), (x, 1 - y), (1 - x, 1 - y)]

        def slot(t, px, py, pc):
            return out_refs[t].at[4 * px + 2 * py + pc]

        def copy(t, k, blk, to, own=False):
            return pltpu.make_async_remote_copy(
                src_ref=x_refs[t] if own else slot(t, *blk), dst_ref=slot(t, *blk), send_sem=send_sems.at[t, k],
                recv_sem=recv_sems.at[t, k], device_id=to, device_id_type=pl.DeviceIdType.MESH)

        def mine(t):
            return pltpu.make_async_copy(x_refs[t], slot(t, *me), local_sems.at[t])

        def first(t):
            return [copy(t, 0, me, sibling, own=True)] + [copy(t, 1 + j, me, (*chip, c), own=True) for j, chip in enumerate(chips)]

        def start():
            for t in range(n):
                mine(t).start()
            for t in range(n):
                for cp in first(t):
                    cp.start()

        def finish():
            passed = []
            for j, chip in enumerate(chips):
                for t in range(n):
                    copy(t, 1 + j, (*chip, c), me).wait_recv()
                    passed.append(copy(t, 4 + j, (*chip, c), sibling))
                    passed[-1].start()
            for t in range(n):
                copy(t, 0, sibling, me).wait_recv()
            for j, chip in enumerate(chips):
                for t in range(n):
                    copy(t, 4 + j, (*chip, 1 - c), me).wait_recv()
            for t in range(n):
                for cp in first(t):
                    cp.wait_send()
            for cp in passed:
                cp.wait_send()
            for t in range(n):
                mine(t).wait()

        return start, finish


def _first_last_step(grid):
    ids = [pl.program_id(a) for a in range(len(grid))]
    first = functools.reduce(lambda p, q: p & q, [i == 0 for i in ids])
    last = functools.reduce(lambda p, q: p & q, [i == n - 1 for i, n in zip(ids, grid)])
    return first, last


def _carried(side, grid, refs, n_in, n_out, n_scratch, compute):
    if side is None:
        compute(refs)
        return
    n = side.n
    main = refs[:n_in] + refs[n_in + n:n_in + n + n_out] + refs[n_in + 2 * n + n_out:n_in + 2 * n + n_out + n_scratch]
    side_refs = refs[n_in:n_in + n] + refs[n_in + n + n_out:n_in + 2 * n + n_out] + refs[n_in + 2 * n + n_out + n_scratch:]
    start, finish = side.hooks(side_refs)
    first, last = _first_last_step(grid)
    pl.when(first)(start)
    compute(main)
    pl.when(last)(finish)


N_CHIP = N_DEV // 2


def _swap_with_sibling(arrays, *, name):
    n = len(arrays)

    def body(*refs):
        src_refs, out_refs = refs[:n], refs[n:2 * n]
        send_sems, recv_sems = refs[2 * n:]
        sibling = (lax.axis_index("x"), lax.axis_index("y"), 1 - lax.axis_index("c"))
        copies = [pltpu.make_async_remote_copy(
            src_ref=src_refs[t], dst_ref=out_refs[t], send_sem=send_sems.at[t], recv_sem=recv_sems.at[t],
            device_id=sibling, device_id_type=pl.DeviceIdType.MESH) for t in range(n)]
        for cp in copies:
            cp.start()
        for cp in copies:
            cp.wait()

    hbm = pl.BlockSpec(memory_space=pl.ANY)
    return pl.pallas_call(
        body, name=name, out_shape=[jax.ShapeDtypeStruct(a.shape, a.dtype) for a in arrays],
        in_specs=[hbm] * n, out_specs=[hbm] * n,
        scratch_shapes=[pltpu.SemaphoreType.DMA((n,)), pltpu.SemaphoreType.DMA((n,))],
    )(*arrays)


def _exchange_chips(stacks, *, name):
    side = _SideChipExchange(stacks)

    def body(*refs):
        start, finish = side.hooks(refs)
        start()
        finish()

    return pl.pallas_call(body, name=name, out_shape=side.out_shape, in_specs=side.in_specs, out_specs=side.out_specs,
                          scratch_shapes=side.scratch)(*stacks)


class _SideChipExchange:
    def __init__(self, stacks):
        self.blocks = list(stacks)
        self.n = n = len(self.blocks)
        hbm = pl.BlockSpec(memory_space=pl.ANY)
        self.in_specs, self.out_specs = [hbm] * n, [hbm] * n
        self.out_shape = [jax.ShapeDtypeStruct(s.shape, s.dtype) for s in self.blocks]
        self.scratch = [pltpu.SemaphoreType.DMA((n, N_CHIP - 1)), pltpu.SemaphoreType.DMA((n, N_CHIP - 1)),
                        pltpu.SemaphoreType.DMA((n,))]

    def hooks(self, refs):
        n = self.n
        g_refs, out_refs = refs[:n], refs[n:2 * n]
        send_sems, recv_sems, local_sems = refs[2 * n:]
        x, y, c = lax.axis_index("x"), lax.axis_index("y"), lax.axis_index("c")
        me = 2 * x + y

        def copies():
            mine = [pltpu.make_async_copy(g_refs[t].at[me], out_refs[t].at[me], local_sems.at[t]) for t in range(n)]
            remote = []
            for k in range(1, N_CHIP):
                px, py = x ^ (k >> 1 & 1), y ^ (k & 1)
                for t in range(n):
                    remote.append(pltpu.make_async_remote_copy(
                        src_ref=g_refs[t].at[2 * px + py], dst_ref=out_refs[t].at[me], send_sem=send_sems.at[t, k - 1],
                        recv_sem=recv_sems.at[t, k - 1], device_id=(px, py, c), device_id_type=pl.DeviceIdType.MESH))
            return mine, remote

        def start():
            mine, remote = copies()
            for cp in mine + remote:
                cp.start()

        def finish():
            mine, remote = copies()
            for cp in remote:
                cp.wait_recv()
            for cp in remote:
                cp.wait_send()
            for cp in mine:
                cp.wait()

        return start, finish


def _pair_sum(a, b):
    return _rowmap(lambda u, v: u.astype(F32) + v.astype(F32), [a, b], [], [(a.shape[1], BF16)], tl=_row_tile(a.shape[0], 512),
                   name="pair_sum")[0]


SMALL_COLS = 128


def _pack_small(arrays):
    flat = jnp.concatenate([a.astype(F32).reshape(-1, SMALL_COLS) for a in arrays], axis=0)
    return jnp.pad(flat, ((0, -flat.shape[0] % 256), (0, 0)))


def _unpack_small(packed, shapes):
    out, off = [], 0
    for s in shapes:
        r = math.prod(s) // SMALL_COLS
        out.append(packed[off:off + r].reshape(s))
        off += r
    return out


def _split_shards(full, axis):
    shape = full.shape
    split = full.reshape(shape[:axis] + (N_DEV, shape[axis] // N_DEV) + shape[axis + 1:])
    return jnp.moveaxis(split, axis, 0)


def _join_shards(stack, axis):
    moved = jnp.moveaxis(stack, 0, axis)
    shape = moved.shape
    return moved.reshape(shape[:axis] + (shape[axis] * shape[axis + 1],) + shape[axis + 2:])


def _w_in_unpadded(w):
    return jnp.concatenate([w[..., :2560], w[..., 3328:3360], w[..., 2560:3328]], axis=-1)


FFN1_W = ('ffn1_w_gate', 'ffn1_w_up', 'ffn1_w_down')
FFN2_W = ('ffn2_w_gate', 'ffn2_w_up', 'ffn2_w_down')
MIX_IN_W = ('w_in', 'w_merge_gate')
MIX_REST_W = ('s5_w_glu', 'gla_w_alpha', 'gla_b_alpha', 'w_branch_s5', 'w_branch_gla', 'w_branch_attn', 'w_out')


def _mixer_weights(full, w, s5, i):
    lw = {n: w[n][i] for n in ('mix_norm', 'gla_norm', 'attn_q_norm', 'attn_k_norm', 'b_merge_gate')}
    lw['s5'] = {'b_mat': s5['b_mat'][i], 'c_mat': s5['c_mat'][i], 'tabs': [t[i] for t in s5['tabs']],
                'tabs_adj': [t[i] for t in s5['tabs_adj']], 'd': w['s5_d'][i]}
    w_in = full['w_in']
    pad = jnp.zeros((D_MODEL, IN_PAD - IN_WIDTH), w_in.dtype)
    lw['w_pg'] = jnp.concatenate([full['w_merge_gate'], w_in[:, :2560], w_in[:, 2592:], w_in[:, 2560:2592], pad], axis=1)
    return lw


def _mixer_weights_rest(full):
    lw = {n: full[n] for n in MIX_REST_W if n != 'gla_b_alpha'}
    lw['gla_b_alpha'] = full['gla_b_alpha'].astype(F32)
    return lw


def _chip_sums(grads, names):
    core = lax.axis_index("c")
    own, for_sibling = [], []
    for n in names:
        by_owner = _split_shards(grads[n], SHARD_AXIS[n] - 1).astype(BF16)
        by_owner = by_owner.reshape((N_CHIP, 2) + by_owner.shape[1:])
        own.append(lax.dynamic_index_in_dim(by_owner, core, axis=1, keepdims=False))
        for_sibling.append(lax.dynamic_index_in_dim(by_owner, 1 - core, axis=1, keepdims=False))
    from_sibling = _swap_with_sibling(for_sibling, name="exchange_grads_sibling")
    return [_pair_sum(a.reshape(-1, a.shape[-1]), b.reshape(-1, b.shape[-1])).reshape(a.shape)
            for a, b in zip(own, from_sibling)]


def _step_local(x, target, w, shards):
    s5 = _s5_setup(w)
    full = [{} for _ in range(DEPTH)]

    def wanted(i, *groups):
        return _SideGather([shards[n][i] for names in groups for n in names])

    def arrived(i, stacks, *groups):
        names = [n for group in groups for n in group]
        for n, st in zip(names, stacks):
            full[i][n] = _join_shards(st, SHARD_AXIS[n] - 1)

    arrived(0, _all_gather([shards[n][0] for n in FFN1_W], name="gather_first"), FFN1_W)
    saved, lws = [], []
    for i in range(DEPTH):
        f, first = full[i], i == 0
        x, s1, got = _ffn_fwd(x, w['ffn1_norm'][i], f['ffn1_w_gate'], f['ffn1_w_up'], f['ffn1_w_down'],
                              wanted(i, MIX_IN_W) if first else None)
        if first:
            arrived(i, got, MIX_IN_W)
        lw = _mixer_weights(f, w, s5, i)
        lws.append(lw)

        def after_in(got_in, i=i, lw=lw):
            arrived(i, got_in, MIX_REST_W)
            lw.update(_mixer_weights_rest(full[i]))
        if not first:
            lw.update(_mixer_weights_rest(f))
        x, s2, (got_s5, got) = _mixer_fwd(x, lw, wanted(i, MIX_REST_W) if first else None, after_in,
                                          wanted(i, FFN2_W) if first else None,
                                          wanted(i + 1, FFN1_W, MIX_IN_W, MIX_REST_W) if first else wanted(i, FFN2_W))
        if first:
            arrived(i, got_s5, FFN2_W)
            arrived(i + 1, got, FFN1_W, MIX_IN_W, MIX_REST_W)
        else:
            arrived(i, got, FFN2_W)
        x, s3, _ = _ffn_fwd(x, w['ffn2_norm'][i], f['ffn2_w_gate'], f['ffn2_w_up'], f['ffn2_w_down'])
        saved.append((s1, s2, s3))
    loss, dx, d_final = _loss_head(x, w['final_norm'], target)
    per_layer, incoming = [None] * DEPTH, [{} for _ in range(DEPTH)]
    later = [n for n in SHARDED if n not in FFN2_W]
    pending = []
    for i in reversed(range(DEPTH)):
        f, lw, (s1, s2, s3), g = full[i], lws[i], saved[i], {}
        dx, g['ffn2_norm'], g['ffn2_w_gate'], g['ffn2_w_up'], g['ffn2_w_down'] = _ffn_bwd(
            s3, w['ffn2_norm'][i], f['ffn2_w_gate'], f['ffn2_w_up'], f['ffn2_w_down'], dx)
        pending.append((i, FFN2_W, _chip_sums(g, FFN2_W)))
        dx, gm, carried = _mixer_bwd(s2, lw, dx, _SideChipExchange([s for _, _, sums in pending for s in sums]))
        for layer, names, _ in pending:
            incoming[layer].update(zip(names, carried[:len(names)]))
            carried = carried[len(names):]
        dx, g['ffn1_norm'], g['ffn1_w_gate'], g['ffn1_w_up'], g['ffn1_w_down'] = _ffn_bwd(
            s1, w['ffn1_norm'][i], f['ffn1_w_gate'], f['ffn1_w_up'], f['ffn1_w_down'], dx)
        g['w_merge_gate'] = gm['w_pg'][:, :GATE_WIDTH]
        g['w_in'] = _w_in_unpadded(gm['w_pg'][:, GATE_WIDTH:])
        for n in ('w_out', 'b_merge_gate', 'w_branch_s5', 'w_branch_gla', 'w_branch_attn', 'mix_norm'):
            g[n] = gm[n]
        g['s5_d'], g['s5_w_glu'], g['s5_raw'] = gm['s5']['d'], gm['s5']['w_glu'], gm['s5']['raw']
        g['gla_w_alpha'], g['gla_b_alpha'], g['gla_norm'] = gm['gla']['w_alpha'], gm['gla']['b_alpha'], gm['gla']['norm']
        g['attn_q_norm'], g['attn_k_norm'] = gm['attn']['q_norm'], gm['attn']['k_norm']
        per_layer[i] = g
        pending = [(i, later, _chip_sums(g, later))]
    incoming[0].update(zip(later, _exchange_chips(pending[0][2], name="exchange_grads_chips")))
    stacked = _s5_param_grads(w, [g['s5_raw'] for g in per_layer])
    stacked['final_norm'] = d_final
    return loss, dx, per_layer, stacked, incoming


def kernel(x, ffn1_norm, ffn1_w_gate, ffn1_w_up, ffn1_w_down, mix_norm, w_in, s5_lambda_re, s5_lambda_im, s5_log_dt, s5_b_re, s5_b_im, s5_c_re, s5_c_im, s5_d, s5_w_glu, gla_w_alpha, gla_b_alpha, gla_norm, attn_q_norm, attn_k_norm, w_branch_s5, w_branch_gla, w_branch_attn, w_merge_gate, b_merge_gate, w_out, ffn2_norm, ffn2_w_gate, ffn2_w_up, ffn2_w_down, final_norm, loss_target, m_ffn1_norm, m_ffn1_w_gate, m_ffn1_w_up, m_ffn1_w_down, m_mix_norm, m_w_in, m_s5_lambda_re, m_s5_lambda_im, m_s5_log_dt, m_s5_b_re, m_s5_b_im, m_s5_c_re, m_s5_c_im, m_s5_d, m_s5_w_glu, m_gla_w_alpha, m_gla_b_alpha, m_gla_norm, m_attn_q_norm, m_attn_k_norm, m_w_branch_s5, m_w_branch_gla, m_w_branch_attn, m_w_merge_gate, m_b_merge_gate, m_w_out, m_ffn2_norm, m_ffn2_w_gate, m_ffn2_w_up, m_ffn2_w_down, m_final_norm, v_ffn1_norm, v_ffn1_w_gate, v_ffn1_w_up, v_ffn1_w_down, v_mix_norm, v_w_in, v_s5_lambda_re, v_s5_lambda_im, v_s5_log_dt, v_s5_b_re, v_s5_b_im, v_s5_c_re, v_s5_c_im, v_s5_d, v_s5_w_glu, v_gla_w_alpha, v_gla_b_alpha, v_gla_norm, v_attn_q_norm, v_attn_k_norm, v_w_branch_s5, v_w_branch_gla, v_w_branch_attn, v_w_merge_gate, v_b_merge_gate, v_w_out, v_ffn2_norm, v_ffn2_w_gate, v_ffn2_w_up, v_ffn2_w_down, v_final_norm):
    return _train_step(x, ffn1_norm, ffn1_w_gate, ffn1_w_up, ffn1_w_down, mix_norm, w_in, s5_lambda_re, s5_lambda_im, s5_log_dt, s5_b_re, s5_b_im, s5_c_re, s5_c_im, s5_d, s5_w_glu, gla_w_alpha, gla_b_alpha, gla_norm, attn_q_norm, attn_k_norm, w_branch_s5, w_branch_gla, w_branch_attn, w_merge_gate, b_merge_gate, w_out, ffn2_norm, ffn2_w_gate, ffn2_w_up, ffn2_w_down, final_norm, loss_target, m_ffn1_norm, m_ffn1_w_gate, m_ffn1_w_up, m_ffn1_w_down, m_mix_norm, m_w_in, m_s5_lambda_re, m_s5_lambda_im, m_s5_log_dt, m_s5_b_re, m_s5_b_im, m_s5_c_re, m_s5_c_im, m_s5_d, m_s5_w_glu, m_gla_w_alpha, m_gla_b_alpha, m_gla_norm, m_attn_q_norm, m_attn_k_norm, m_w_branch_s5, m_w_branch_gla, m_w_branch_attn, m_w_merge_gate, m_b_merge_gate, m_w_out, m_ffn2_norm, m_ffn2_w_gate, m_ffn2_w_up, m_ffn2_w_down, m_final_norm, v_ffn1_norm, v_ffn1_w_gate, v_ffn1_w_up, v_ffn1_w_down, v_mix_norm, v_w_in, v_s5_lambda_re, v_s5_lambda_im, v_s5_log_dt, v_s5_b_re, v_s5_b_im, v_s5_c_re, v_s5_c_im, v_s5_d, v_s5_w_glu, v_gla_w_alpha, v_gla_b_alpha, v_gla_norm, v_attn_q_norm, v_attn_k_norm, v_w_branch_s5, v_w_branch_gla, v_w_branch_attn, v_w_merge_gate, v_b_merge_gate, v_w_out, v_ffn2_norm, v_ffn2_w_gate, v_ffn2_w_up, v_ffn2_w_down, v_final_norm)


def _train_step(*args):
    nw = len(W_NAMES)
    x, target = args[0][0], args[1 + nw][0]
    w = dict(zip(W_NAMES, args[1:1 + nw]))
    m = dict(zip(W_NAMES, args[2 + nw:2 + 2 * nw]))
    v = dict(zip(W_NAMES, args[2 + 2 * nw:2 + 3 * nw]))

    loss, dx, per_layer, stacked, incoming = _step_local(x, target, w, {n: w[n].astype(BF16) for n in SHARDED})
    loss = lax.psum(loss, ("x", "y", "c"))

    out = {}
    kinds = ('grad', 'delta', 'new_m', 'new_v')
    for n in SHARDED:
        shape = w[n].shape
        flat = lambda a: a.reshape(-1, shape[-1])
        parts = [incoming[i][n].reshape(N_CHIP, -1, shape[-1]) for i in range(DEPTH)]
        res = _reduce_adamw(parts, flat(w[n]), flat(m[n]), flat(v[n]), name="adamw_sharded")
        for kind, a in zip(kinds, res):
            out[kind + '_' + n] = a.reshape(shape)
    small = [stacked[n] if n in stacked else jnp.stack([g[n] for g in per_layer]) for n in REPLICATED]
    parts = _all_gather([_pack_small(small)], name="gather_small_grads")[0]
    res = _reduce_adamw([parts], *[_pack_small([d[n] for n in REPLICATED]) for d in (w, m, v)], name="adamw_replicated")
    for kind, packed in zip(kinds, res):
        for n, a in zip(REPLICATED, _unpack_small(packed, [w[n].shape for n in REPLICATED])):
            out[kind + '_' + n] = a
    return (loss, dx[None]) + tuple(out[kind + '_' + n] for kind in kinds for n in W_NAMES)
```

```python
import functools
import math

import jax
import jax.numpy as jnp
import numpy as np
from jax import lax
from jax.experimental import pallas as pl
from jax.experimental.pallas import tpu as pltpu

F32 = jnp.float32
BF16 = jnp.bfloat16

N_DEV = 8
D_MODEL = 1024
DEPTH = 2
GRID_W = 64
D_FF = 2816
NORM_EPS = 1e-6
S5_GROUPS = 32
S5_GROUP_CH = 16
S5_STATE = 64
S5_WIDTH = 512
S5_NSTATE = S5_GROUPS * S5_STATE
S5_LANE_BLOCK = 512
GLA_HEADS = 4
GLA_HEAD_DIM = 128
GLA_WIDTH = 512
GLA_LOWRANK = 16
GLA_TAU = 16.0
GLA_CHUNK = 64
ATTN_Q_HEADS = 8
ATTN_KV_HEADS = 2
ATTN_HEAD_DIM = 64
ATTN_WIDTH = 512
ATTN_KV_WIDTH = 128
ROPE_BASE = 10000.0
IN_SPLITS = (512, 512, 512, 512, 512, 16, 16, 512, 128, 128)
IN_WIDTH = sum(IN_SPLITS)
IN_PAD = 3584
GATE_WIDTH = 3 * D_MODEL
PG_WIDTH = GATE_WIDTH + IN_PAD
P_OFF = GATE_WIDTH
CB_U, CB_GQ, CB_GK, CB_GV, CB_GG, CB_AQ = (P_OFF // 512 + i for i in range(6))
CB_AK, CB_AV, CB_Z = (P_OFF + 3072) // 128, (P_OFF + 3200) // 128, (P_OFF + 3328) // 128
ADAM_LR = 0.001
ADAM_B1 = 0.9
ADAM_B2 = 0.999
ADAM_EPS = 1e-08
ADAM_WD = 0.01
ADAM_STEP = 10

W_NAMES = ['ffn1_norm', 'ffn1_w_gate', 'ffn1_w_up', 'ffn1_w_down', 'mix_norm', 'w_in', 's5_lambda_re', 's5_lambda_im',
           's5_log_dt', 's5_b_re', 's5_b_im', 's5_c_re', 's5_c_im', 's5_d', 's5_w_glu', 'gla_w_alpha', 'gla_b_alpha',
           'gla_norm', 'attn_q_norm', 'attn_k_norm', 'w_branch_s5', 'w_branch_gla', 'w_branch_attn', 'w_merge_gate',
           'b_merge_gate', 'w_out', 'ffn2_norm', 'ffn2_w_gate', 'ffn2_w_up', 'ffn2_w_down', 'final_norm']
SHARD_AXIS = {'ffn1_w_gate': 2, 'ffn1_w_up': 2, 'ffn1_w_down': 1, 'w_in': 2, 's5_w_glu': 1, 'gla_w_alpha': 3,
              'gla_b_alpha': 2, 'w_branch_s5': 2, 'w_branch_gla': 2, 'w_branch_attn': 2, 'w_merge_gate': 2,
              'w_out': 1, 'ffn2_w_gate': 2, 'ffn2_w_up': 2, 'ffn2_w_down': 1}
SHARDED = [n for n in W_NAMES if n in SHARD_AXIS]
REPLICATED = [n for n in W_NAMES if n not in SHARD_AXIS]


def _pick(dim, prefs):
    for p in prefs:
        if dim % p == 0:
            return p
    return dim


def _sigmoid(x):
    return 0.5 * jnp.tanh(0.5 * x) + 0.5


def _mm(a, b, *, ta=False, tb=False, out_dtype=F32, scale=None, add=None, side=None, name):
    a, a_cb, a_w = a if isinstance(a, tuple) else (a, 0, a.shape[1])
    b, b_cb, b_w = b if isinstance(b, tuple) else (b, 0, b.shape[1])
    m, k = (a_w, a.shape[0]) if ta else (a.shape[0], a_w)
    n = b.shape[0] if tb else b_w
    assert (b_w if tb else b.shape[0]) == k, (a.shape, b.shape, ta, tb)
    tm, tn, tk = _mm_tiles(m, n, k, a.dtype.itemsize, b.dtype.itemsize, jnp.dtype(out_dtype).itemsize)
    nk = k // tk
    dims = (((0 if ta else 1,), (1 if tb else 0,)), ((), ()))
    a_off = a_cb * (a_w // (tm if ta else tk))
    b_off = b_cb * (b_w // (tk if tb else tn))

    grid = (m // tm, n // tn, nk)
    n_in = 2 if add is None else 3

    def body(*refs):
        _carried(side, grid, refs, n_in, 1, int(nk > 1), compute)

    def compute(refs):
        a_ref, b_ref, *rest = refs
        add_ref = rest[0] if add is not None else None
        o_ref, *acc = rest[1:] if add is not None else rest

        def finish(res):
            res = res if scale is None else res * scale
            return (res if add_ref is None else res + add_ref[...]).astype(out_dtype)

        part = lax.dot_general(a_ref[...].astype(BF16), b_ref[...].astype(BF16), dims, preferred_element_type=F32)
        if nk == 1:
            o_ref[...] = finish(part)
            return
        acc_ref, = acc
        kk = pl.program_id(2)

        @pl.when(kk == 0)
        def _():
            acc_ref[...] = part

        @pl.when(kk > 0)
        def _():
            acc_ref[...] += part

        @pl.when(kk == nk - 1)
        def _():
            o_ref[...] = finish(acc_ref[...])

    a_spec = (pl.BlockSpec((tk, tm), lambda i, j, kk: (kk, i + a_off)) if ta
              else pl.BlockSpec((tm, tk), lambda i, j, kk: (i, kk + a_off)))
    b_spec = (pl.BlockSpec((tn, tk), lambda i, j, kk: (j, kk + b_off)) if tb
              else pl.BlockSpec((tk, tn), lambda i, j, kk: (kk, j + b_off)))
    o_spec = pl.BlockSpec((tm, tn), lambda i, j, kk: (i, j))
    (out,), gathered = _side_call(
        body, side, name=name, grid=grid, in_specs=[a_spec, b_spec] + ([o_spec] if add is not None else []),
        out_specs=[o_spec], out_shape=[jax.ShapeDtypeStruct((m, n), out_dtype)],
        scratch=[pltpu.VMEM((tm, tn), F32)] if nk > 1 else [], args=[a, b] + ([add] if add is not None else []),
        semantics=("parallel", "parallel", "arbitrary"))
    return out if side is None else (out, gathered)


MM_VMEM_BUDGET = 40 * 1024 * 1024


def _mm_tiles(m, n, k, a_bytes, b_bytes, out_bytes):
    tms = [t for t in (1024, 1408, 512, 256, 128) if m % t == 0] or [m]
    tns = [t for t in (1664, 1408, 512, 256, 128) if n % t == 0] or [n]
    tks = [k] + [t for t in (2048, 1024, 512, 256, 128) if k % t == 0 and t < k]
    for tk in tks:
        for tm in tms:
            for tn in tns:
                use = 2 * (tm * tk * a_bytes + tk * tn * b_bytes + tm * tn * out_bytes) + 2 * tm * tn * 4
                if use <= MM_VMEM_BUDGET:
                    return tm, tn, tk
    return tms[-1], tns[-1], tks[-1]


def _rowmap(fn, rows, consts, outs, reds=(), *, tl, name):
    rows = [r if isinstance(r, tuple) else (r, 0, r.shape[1]) for r in rows]
    length = rows[0][0].shape[0]
    tl = min(tl, length)
    nr, nc, no = len(rows), len(consts), len(outs)

    def body(*refs):
        res = fn(*[r[...] for r in refs[:nr + nc]])
        res = res if isinstance(res, tuple) else (res,)
        for o_ref, val in zip(refs[nr + nc:nr + nc + no], res[:no]):
            o_ref[...] = val.astype(o_ref.dtype)
        if reds:
            step = pl.program_id(0)
            red_refs = refs[nr + nc + no:]

            @pl.when(step == 0)
            def _():
                for d_ref, val in zip(red_refs, res[no:]):
                    d_ref[...] = val.astype(F32)

            @pl.when(step > 0)
            def _():
                for d_ref, val in zip(red_refs, res[no:]):
                    d_ref[...] += val.astype(F32)

    in_specs = [pl.BlockSpec((tl, w), lambda i, cb=cb: (i, cb)) for (_, cb, w) in rows]
    in_specs += [pl.BlockSpec(c.shape, lambda i, nd=c.ndim: (0,) * nd) for c in consts]
    out_specs = [pl.BlockSpec((tl, w), lambda i: (i, 0)) for (w, _) in outs]
    out_specs += [pl.BlockSpec(s, lambda i, nd=len(s): (0,) * nd) for s in reds]
    out_shape = [jax.ShapeDtypeStruct((length, w), dt) for (w, dt) in outs]
    out_shape += [jax.ShapeDtypeStruct(s, F32) for s in reds]
    res = pl.pallas_call(
        body, name=name, grid=(length // tl,), in_specs=in_specs, out_specs=out_specs, out_shape=out_shape,
        compiler_params=pltpu.CompilerParams(dimension_semantics=("arbitrary" if reds else "parallel",)),
    )(*[r[0] for r in rows], *consts)
    return res


def _rms(x):
    return lax.rsqrt(jnp.mean(x * x, axis=-1, keepdims=True) + NORM_EPS)


def _rmsnorm_fwd(x, gain):
    def fn(xv, g):
        return xv * _rms(xv) * g
    return _rowmap(fn, [x], [gain.reshape(1, -1)], [(x.shape[1], BF16)], tl=256, name="rmsnorm_fwd")[0]


def _rmsnorm_bwd(x, gain, dh, dres):
    def fn(xv, dhv, drv, g):
        r = _rms(xv)
        gd = dhv * g
        dx = r * gd - xv * (r * r * r) * jnp.mean(xv * gd, axis=-1, keepdims=True)
        return drv + dx, jnp.sum(dhv * xv * r, axis=0, keepdims=True)
    dx, dg = _rowmap(fn, [x, dh, dres], [gain.reshape(1, -1)], [(x.shape[1], F32)], [(1, x.shape[1])], tl=256,
                     name="rmsnorm_bwd")
    return dx, dg[0]


FFN_UNIT = D_FF // 2


def _side_call(body, side, *, name, grid, in_specs, out_specs, out_shape, scratch, args, semantics):
    if side is not None:
        in_specs, out_specs = in_specs + side.in_specs, out_specs + side.out_specs
        out_shape, scratch, args = out_shape + side.out_shape, scratch + side.scratch, list(args) + side.blocks
        semantics = ("arbitrary",) * len(grid)
    res = pl.pallas_call(body, name=name, grid=grid, in_specs=in_specs, out_specs=out_specs, out_shape=out_shape,
                         scratch_shapes=scratch, compiler_params=pltpu.CompilerParams(dimension_semantics=semantics))(*args)
    n_own = len(res) - (side.n if side is not None else 0)
    return res[:n_own], res[n_own:]


def _ffn_up(h, w_gate, w_up, side=None):
    length, k = h.shape
    tm = _pick(length, (512, 256, 128))
    grid = (D_FF // FFN_UNIT, length // tm)

    def compute(refs):
        h_ref, wg_ref, wu_ref, a_ref, g_ref, u_ref = refs
        hv = h_ref[...]
        g = jnp.dot(hv, wg_ref[...], preferred_element_type=F32)
        u = jnp.dot(hv, wu_ref[...], preferred_element_type=F32)
        a_ref[...] = (g * _sigmoid(g) * u).astype(BF16)
        g_ref[...] = g.astype(BF16)
        u_ref[...] = u.astype(BF16)

    def body(*refs):
        _carried(side, grid, refs, 3, 3, 0, compute)

    w_spec = pl.BlockSpec((k, FFN_UNIT), lambda j, i: (0, j))
    o_spec = pl.BlockSpec((tm, FFN_UNIT), lambda j, i: (i, j))
    return _side_call(
        body, side, name="ffn_up", grid=grid, in_specs=[pl.BlockSpec((tm, k), lambda j, i: (i, 0)), w_spec, w_spec],
        out_specs=[o_spec] * 3, out_shape=[jax.ShapeDtypeStruct((length, D_FF), BF16)] * 3, scratch=[],
        args=[h, w_gate, w_up], semantics=("parallel", "parallel"))


def _ffn_dgu(dxo, w_down, g, u):
    length, k = dxo.shape
    tm = _pick(length, (512, 256, 128))

    def body(d_ref, w_ref, g_ref, u_ref, dg_ref, du_ref):
        da = 0.5 * lax.dot_general(d_ref[...], w_ref[...], _NT, preferred_element_type=F32)
        gv = g_ref[...].astype(F32)
        s = _sigmoid(gv)
        dg_ref[...] = (da * u_ref[...].astype(F32) * (s * (1.0 + gv * (1.0 - s)))).astype(BF16)
        du_ref[...] = (da * (gv * s)).astype(BF16)

    o_spec = pl.BlockSpec((tm, FFN_UNIT), lambda j, i: (i, j))
    return pl.pallas_call(
        body, name="ffn_dgu", grid=(D_FF // FFN_UNIT, length // tm),
        in_specs=[pl.BlockSpec((tm, k), lambda j, i: (i, 0)), pl.BlockSpec((FFN_UNIT, k), lambda j, i: (j, 0)), o_spec, o_spec],
        out_specs=[o_spec] * 2, out_shape=[jax.ShapeDtypeStruct((length, D_FF), BF16)] * 2,
        compiler_params=pltpu.CompilerParams(dimension_semantics=("parallel", "parallel")),
    )(dxo, w_down, g, u)


def _ffn_fwd(x, gain, w_gate, w_up, w_down, side=None):
    h = _rmsnorm_fwd(x, gain)
    (a, g, u), gathered = _ffn_up(h, w_gate, w_up, side)
    x_out = _mm(a, w_down, scale=0.5, add=x, name="ffn_down")
    return x_out, (x, h, g, u, a), gathered


def _ffn_bwd(saved, gain, w_gate, w_up, w_down, dx_out):
    x, h, g, u, a = saved
    dxo = dx_out.astype(BF16)
    d_wdown = _mm(a, dxo, ta=True, scale=0.5, out_dtype=BF16, name="ffn_dwdown")
    dg, du = _ffn_dgu(dxo, w_down, g, u)
    d_wgate = _mm(h, dg, ta=True, out_dtype=BF16, name="ffn_dwgu")
    d_wup = _mm(h, du, ta=True, out_dtype=BF16, name="ffn_dwgu")
    dh = _mm(du, w_up, tb=True, add=_mm(dg, w_gate, tb=True, name="ffn_dh"), name="ffn_dh_add")
    dx, dgain = _rmsnorm_bwd(x, gain, dh, dx_out)
    return dx, dgain, d_wgate, d_wup, d_wdown


def _s5_blocked(re, im):
    lead = re.shape[:-1]
    nb = S5_NSTATE // S5_LANE_BLOCK
    both = jnp.stack([re.reshape(*lead, nb, S5_LANE_BLOCK), im.reshape(*lead, nb, S5_LANE_BLOCK)], axis=-2)
    return both.reshape(*lead, 2 * S5_NSTATE)


def _s5_unblocked(z):
    lead = z.shape[:-1]
    nb = S5_NSTATE // S5_LANE_BLOCK
    both = z.reshape(*lead, nb, 2, S5_LANE_BLOCK)
    return both[..., 0, :].reshape(*lead, S5_NSTATE), both[..., 1, :].reshape(*lead, S5_NSTATE)


def _s5_tables(a_re, a_im, reverse):
    a = lax.complex(a_re, a_im)
    a2 = a * a
    a4 = a2 * a2
    rows = jnp.arange(8)
    pw = [a]
    for _ in range(7):
        pw.append(pw[-1] * a)
    pw = jnp.stack(pw)
    if reverse:
        pw = pw[::-1]
    tabs = []
    for coef, s in ((a, 1), (a2, 2), (a4, 4)):
        live = (rows <= 7 - s) if reverse else (rows >= s)
        tabs.append(jnp.where(live[:, None], coef[None, :], 0.0))
    tabs.append(pw)
    tabs = jnp.stack(tabs)
    return _s5_blocked(jnp.real(tabs), jnp.imag(tabs))


def _s5_scan_tile(v, tab_ref, prev, reverse):
    lb = S5_LANE_BLOCK
    vr, vi = v[:, :lb], v[:, lb:]
    for idx, s in enumerate((1, 2, 4)):
        cr, ci = tab_ref[idx, :, :lb], tab_ref[idx, :, lb:]
        sh = 8 - s if reverse else s
        sr, si = pltpu.roll(vr, sh, 0), pltpu.roll(vi, sh, 0)
        vr, vi = vr + cr * sr - ci * si, vi + cr * si + ci * sr
    row = 0 if reverse else 7
    pr = jnp.broadcast_to(prev[row:row + 1, :lb], (8, lb))
    pi = jnp.broadcast_to(prev[row:row + 1, lb:], (8, lb))
    cr, ci = tab_ref[3, :, :lb], tab_ref[3, :, lb:]
    return jnp.concatenate([vr + cr * pr - ci * pi, vi + cr * pi + ci * pr], axis=1)


def _s5_prep(lam_re, lam_im, log_dt, b_re, b_im):
    lam = lax.complex(lam_re, lam_im)
    dt = jnp.exp(log_dt)[:, None]
    lam_bar = jnp.exp(lam * dt)
    b_bar = ((lam_bar - 1.0) / lam)[..., None] * lax.complex(b_re, b_im)
    return (jnp.real(lam_bar).reshape(-1), jnp.imag(lam_bar).reshape(-1), jnp.real(b_bar), jnp.imag(b_bar))


S5_NBLK = S5_NSTATE // S5_LANE_BLOCK
S5_BLK_GROUPS = S5_GROUPS // S5_NBLK
S5_BLK_CH = S5_BLK_GROUPS * S5_GROUP_CH


def _s5_in_matrix(bb_re, bb_im):
    eye = jnp.eye(S5_BLK_GROUPS, dtype=F32)
    def dense(bb):
        b4 = bb.reshape(S5_NBLK, S5_BLK_GROUPS, S5_STATE, S5_GROUP_CH)
        return jnp.einsum('cgph,gk->cghkp', b4, eye).reshape(S5_NBLK, S5_BLK_CH, S5_LANE_BLOCK)
    return jnp.concatenate([dense(bb_re), dense(bb_im)], axis=-1)


def _s5_block_diagonal(d):
    d5 = d.reshape(S5_NBLK, S5_BLK_GROUPS, S5_GROUP_CH, S5_BLK_GROUPS, S5_STATE)
    eye = jnp.eye(S5_BLK_GROUPS, dtype=F32)
    return jnp.swapaxes(jnp.sum(d5 * eye[None, :, None, :, None], axis=1), 1, 2)


def _s5_in_matrix_grad(d_mat):
    def diag(d):
        return jnp.swapaxes(_s5_block_diagonal(d), 2, 3).reshape(S5_GROUPS, S5_STATE, S5_GROUP_CH)
    return diag(d_mat[..., :S5_LANE_BLOCK]), diag(d_mat[..., S5_LANE_BLOCK:])


def _s5_out_matrix(c_re, c_im):
    eye = jnp.eye(S5_BLK_GROUPS, dtype=F32)
    def dense(cc):
        c4 = cc.reshape(S5_NBLK, S5_BLK_GROUPS, S5_GROUP_CH, S5_STATE)
        return jnp.einsum('cghp,gk->cgpkh', c4, eye).reshape(S5_NBLK, S5_LANE_BLOCK, S5_BLK_CH)
    return jnp.concatenate([dense(c_re), dense(-c_im)], axis=1)


def _s5_out_matrix_grad(d_mat_t):
    def diag(d):
        return _s5_block_diagonal(d).reshape(S5_GROUPS, S5_GROUP_CH, S5_STATE)
    return diag(d_mat_t[..., :S5_LANE_BLOCK]), -diag(d_mat_t[..., S5_LANE_BLOCK:])


def _gelu_parts(x):
    k = math.sqrt(2.0 / math.pi)
    inner = k * (x + 0.044715 * x * x * x)
    th = jnp.tanh(inner)
    return th, k * (1.0 + 3.0 * 0.044715 * x * x)


S5_CB_U = CB_U * (512 // S5_BLK_CH)


def _s5_direction_fwd(pg, b_mat, c_mat, tabs, reverse, *, name, side=None):
    length = pg.shape[0]
    tb = min(512, length)
    ntb = length // tb
    wb = 2 * S5_LANE_BLOCK
    ntile = tb // 8
    grid = (S5_NBLK, ntb)

    def body(*refs):
        _carried(side, grid, refs, 4, 3, 2, compute)

    def compute(refs):
        tab_ref, u_ref, b_ref, c_ref, x_ref, y_ref, ends_ref, carry_ref, bu_ref = refs

        @pl.when(pl.program_id(1) == 0)
        def _():
            carry_ref[...] = jnp.zeros_like(carry_ref)

        bu_ref[...] = jnp.dot(u_ref[...].astype(BF16), b_ref[0], preferred_element_type=F32)

        def step(i, prev):
            r0 = pl.multiple_of((ntile - 1 - i if reverse else i) * 8, 8)
            x = _s5_scan_tile(bu_ref[pl.ds(r0, 8), :], tab_ref, prev, reverse)
            x_ref[pl.ds(r0, 8), :] = x
            return x

        carry_ref[...] = lax.fori_loop(0, ntile, step, carry_ref[...])
        y_ref[...] = jnp.dot(x_ref[...].astype(BF16), c_ref[0], preferred_element_type=F32)
        ends_ref[0, 0:8, :] = x_ref[0:8, :]
        ends_ref[0, 8:16, :] = x_ref[tb - 8:tb, :]

    tix = (lambda t: ntb - 1 - t) if reverse else (lambda t: t)
    (xs, y, ends), gathered = _side_call(
        body, side, name=name, grid=grid,
        in_specs=[pl.BlockSpec((4, 8, wb), lambda c, t: (0, 0, c)),
                  pl.BlockSpec((tb, S5_BLK_CH), lambda c, t: (tix(t), S5_CB_U + c)),
                  pl.BlockSpec((1, S5_BLK_CH, wb), lambda c, t: (c, 0, 0)),
                  pl.BlockSpec((1, wb, S5_BLK_CH), lambda c, t: (c, 0, 0))],
        out_specs=[pl.BlockSpec((tb, wb), lambda c, t: (tix(t), c)), pl.BlockSpec((tb, S5_BLK_CH), lambda c, t: (tix(t), c)),
                   pl.BlockSpec((1, 16, wb), lambda c, t: (tix(t), 0, c))],
        out_shape=[jax.ShapeDtypeStruct((length, S5_NBLK * wb), F32), jax.ShapeDtypeStruct((length, S5_WIDTH), F32),
                   jax.ShapeDtypeStruct((ntb, 16, S5_NBLK * wb), F32)],
        scratch=[pltpu.VMEM((8, wb), F32), pltpu.VMEM((tb, wb), F32)], args=[tabs, pg, b_mat, c_mat],
        semantics=("parallel", "arbitrary"))
    return xs, y, ends, gathered


def _s5_direction_bwd(pg, dy, xs, ends, b_mat, c_mat, tabs_conj, reverse, *, name):
    length = pg.shape[0]
    tb = min(512, length)
    ntb = length // tb
    lb = S5_LANE_BLOCK
    wb = 2 * lb
    ntile = tb // 8
    adj_rev = not reverse
    if reverse:
        edge = jnp.concatenate([ends[1:, 0], jnp.zeros((1, xs.shape[1]), F32)], axis=0)
    else:
        edge = jnp.concatenate([jnp.zeros((1, xs.shape[1]), F32), ends[:-1, 15]], axis=0)
    edge = edge.reshape(ntb, 1, xs.shape[1])

    def body(tab_ref, u_ref, dy_ref, x_ref, edge_ref, b_ref, c_ref, du_ref, db_ref, dc_ref, da_ref, carry_ref, g_ref, lam_ref):
        @pl.when(pl.program_id(1) == 0)
        def _():
            carry_ref[...] = jnp.zeros_like(carry_ref)
            da_ref[...] = jnp.zeros_like(da_ref)
            db_ref[...] = jnp.zeros_like(db_ref)
            dc_ref[...] = jnp.zeros_like(dc_ref)

        dyb = dy_ref[...].astype(BF16)
        g_ref[...] = lax.dot_general(dyb, c_ref[0], _NT, preferred_element_type=F32)
        rows = lax.broadcasted_iota(jnp.int32, (8, wb), 0)

        def step(i, carry):
            prev, acc = carry
            k = ntile - 1 - i if adj_rev else i
            r0 = pl.multiple_of(k * 8, 8)
            lam = _s5_scan_tile(g_ref[pl.ds(r0, 8), :], tab_ref, prev, adj_rev)
            lam_ref[pl.ds(r0, 8), :] = lam
            x = x_ref[pl.ds(r0, 8), :]
            if reverse:
                kn = jnp.minimum(k + 1, ntile - 1)
                nb = x_ref[pl.ds(pl.multiple_of(kn * 8, 8), 8), :][0:1, :]
                nb = jnp.where(k == ntile - 1, edge_ref[0], nb)
                xp = jnp.where(rows == 7, jnp.broadcast_to(nb, (8, wb)), pltpu.roll(x, 7, 0))
            else:
                kn = jnp.maximum(k - 1, 0)
                nb = x_ref[pl.ds(pl.multiple_of(kn * 8, 8), 8), :][7:8, :]
                nb = jnp.where(k == 0, edge_ref[0], nb)
                xp = jnp.where(rows == 0, jnp.broadcast_to(nb, (8, wb)), pltpu.roll(x, 1, 0))
            xr, xi, lr, li = xp[:, :lb], xp[:, lb:], lam[:, :lb], lam[:, lb:]
            return lam, acc + jnp.concatenate([xr * lr + xi * li, xr * li - xi * lr], axis=1)

        last, acc = lax.fori_loop(0, ntile, step, (carry_ref[...], da_ref[...]))
        carry_ref[...] = last
        da_ref[...] = acc
        lamb = lam_ref[...].astype(BF16)
        du_ref[...] = lax.dot_general(lamb, b_ref[0], _NT, preferred_element_type=F32)
        db_ref[0] += lax.dot_general(u_ref[...].astype(BF16), lamb, _TN, preferred_element_type=F32)
        dc_ref[0] += lax.dot_general(dyb, x_ref[...].astype(BF16), _TN, preferred_element_type=F32)

    tix = (lambda t: ntb - 1 - t) if adj_rev else (lambda t: t)
    wide = pl.BlockSpec((tb, wb), lambda c, t: (tix(t), c))
    mat = pl.BlockSpec((1, S5_BLK_CH, wb), lambda c, t: (c, 0, 0))
    return pl.pallas_call(
        body, name=name, grid=(S5_NBLK, ntb),
        in_specs=[pl.BlockSpec((4, 8, wb), lambda c, t: (0, 0, c)),
                  pl.BlockSpec((tb, S5_BLK_CH), lambda c, t: (tix(t), S5_CB_U + c)),
                  pl.BlockSpec((tb, S5_BLK_CH), lambda c, t: (tix(t), c)), wide,
                  pl.BlockSpec((1, 1, wb), lambda c, t: (tix(t), 0, c)), mat,
                  pl.BlockSpec((1, wb, S5_BLK_CH), lambda c, t: (c, 0, 0))],
        out_specs=[pl.BlockSpec((tb, S5_BLK_CH), lambda c, t: (tix(t), c)), mat, mat, pl.BlockSpec((8, wb), lambda c, t: (0, c))],
        out_shape=[jax.ShapeDtypeStruct((length, S5_WIDTH), F32), jax.ShapeDtypeStruct((S5_NBLK, S5_BLK_CH, wb), F32),
                   jax.ShapeDtypeStruct((S5_NBLK, S5_BLK_CH, wb), F32), jax.ShapeDtypeStruct((8, S5_NBLK * wb), F32)],
        scratch_shapes=[pltpu.VMEM((8, wb), F32), pltpu.VMEM((tb, wb), F32), pltpu.VMEM((tb, wb), F32)],
        compiler_params=pltpu.CompilerParams(dimension_semantics=("parallel", "arbitrary")),
    )(tabs_conj, pg, dy, xs, edge, b_mat, c_mat)


def _both(fn):
    return jax.vmap(jax.vmap(fn))


def _s5_setup(w):
    a_re, a_im, bb_re, bb_im = _both(_s5_prep)(w['s5_lambda_re'], w['s5_lambda_im'], w['s5_log_dt'], w['s5_b_re'], w['s5_b_im'])

    def tables(d, conj, reverse):
        return jax.vmap(lambda r, i: _s5_tables(r, -i if conj else i, reverse))(a_re[:, d], a_im[:, d])
    return {'b_mat': _both(_s5_in_matrix)(bb_re, bb_im).astype(BF16),
            'c_mat': _both(_s5_out_matrix)(w['s5_c_re'], w['s5_c_im']).astype(BF16),
            'tabs': [tables(0, False, False), tables(1, False, True)],
            'tabs_adj': [tables(0, True, True), tables(1, True, False)]}


def _s5_param_grads(w, raws):
    def stacked(k):
        return jnp.stack([jnp.stack([raws[i][d][k] for d in range(2)]) for i in range(DEPTH)])
    dbb_re, dbb_im = _both(_s5_in_matrix_grad)(stacked(0))
    dc_re, dc_im = _both(_s5_out_matrix_grad)(stacked(1))
    da_re, da_im = _s5_unblocked(jnp.sum(stacked(2), axis=2))
    _, vjp = jax.vjp(_both(_s5_prep), w['s5_lambda_re'], w['s5_lambda_im'], w['s5_log_dt'], w['s5_b_re'], w['s5_b_im'])
    g = vjp((da_re, da_im, dbb_re, dbb_im))
    return {'s5_lambda_re': g[0], 's5_lambda_im': g[1], 's5_log_dt': g[2], 's5_b_re': g[3], 's5_b_im': g[4],
            's5_c_re': dc_re, 's5_c_im': dc_im}


def _s5_fwd(p_in, prm, w_glu, side=None):
    dirs = []
    ys = []
    gathered = []
    for d, reverse in ((0, False), (1, True)):
        xs, y_dir, ends, got = _s5_direction_fwd(p_in, prm['b_mat'][d], prm['c_mat'][d], prm['tabs'][d], reverse,
                                                 name="s5_fwd_rev" if reverse else "s5_fwd", side=None if reverse else side)
        gathered += got
        ys.append(y_dir)
        dirs.append((xs, ends))

    def post(yf, yb, u, dskip):
        ypre = yf + yb + dskip * u
        th, _ = _gelu_parts(ypre)
        return ypre, 0.5 * ypre * (1.0 + th)
    ypre, yg = _rowmap(post, [ys[0], ys[1], (p_in, CB_U, S5_WIDTH)], [prm['d'].reshape(1, -1)],
                       [(S5_WIDTH, F32), (S5_WIDTH, F32)], tl=512, name="s5_post")
    t = _mm(yg, w_glu, name="s5_glu_mm")

    def glu(ygv, tv):
        return ygv * _sigmoid(tv)
    y = _rowmap(glu, [yg, t], [], [(S5_WIDTH, BF16)], tl=512, name="s5_glu")[0]
    return y, (dirs, ypre, yg, t), gathered


def _s5_bwd(pg, prm, w_glu, saved, dy):
    dirs, ypre, yg, t = saved

    def glu_bwd(dyv, ygv, tv):
        s = _sigmoid(tv)
        return dyv * ygv * s * (1.0 - s), dyv * s
    dt, dyg_direct = _rowmap(glu_bwd, [dy, yg, t], [], [(S5_WIDTH, BF16), (S5_WIDTH, F32)], tl=512, name="s5_glu_bwd")
    grads = {'w_glu': _mm(yg, dt, ta=True, out_dtype=BF16, name="s5_dwglu")}
    dyg_mm = _mm(dt, w_glu, tb=True, name="s5_dyg")

    def post_bwd(dyd, dym, yp, u, dskip):
        th, dinner = _gelu_parts(yp)
        dyp = (dyd + dym) * (0.5 * (1.0 + th) + 0.5 * yp * (1.0 - th * th) * dinner)
        return dyp, dyp * dskip, jnp.sum(dyp * u, axis=0, keepdims=True)
    dyp, du_skip, dd = _rowmap(post_bwd, [dyg_direct, dyg_mm, ypre, (pg, CB_U, S5_WIDTH)], [prm['d'].reshape(1, -1)],
                               [(S5_WIDTH, F32), (S5_WIDTH, F32)], [(1, S5_WIDTH)], tl=512, name="s5_post_bwd")
    grads['d'] = dd[0]
    du = [du_skip]
    grads['raw'] = []
    for d, reverse in ((0, False), (1, True)):
        du_dir, d_bmat, d_cmat_t, da = _s5_direction_bwd(pg, dyp, *dirs[d], prm['b_mat'][d], prm['c_mat'][d], prm['tabs_adj'][d],
                                                         reverse, name="s5_bwd_rev" if reverse else "s5_bwd")
        du.append(du_dir)
        grads['raw'].append((d_bmat, d_cmat_t, da))
    return du, grads


def _split3(x):
    hi = x.astype(BF16)
    r = x - hi.astype(F32)
    mid = r.astype(BF16)
    return hi, mid, (r - mid.astype(F32)).astype(BF16)


def _exact_dot(ones, x, dims):
    parts = [lax.dot_general(ones, p, dims, preferred_element_type=F32) for p in _split3(x)]
    return parts[0] + parts[1] + parts[2]


_NN = (((1,), (0,)), ((), ()))
_NT = (((1,), (1,)), ((), ()))
_TN = (((0,), (0,)), ((), ()))


def _dot(a, b, dims=_NN):
    return lax.dot_general(a.astype(BF16), b.astype(BF16), dims, preferred_element_type=F32)


def _gla_chunk_mask(reverse):
    rows = lax.broadcasted_iota(jnp.int32, (GLA_CHUNK, GLA_CHUNK), 0)
    cols = lax.broadcasted_iota(jnp.int32, (GLA_CHUNK, GLA_CHUNK), 1)
    return (cols >= rows) if reverse else (cols <= rows)


def _gla_fwd(pg, la, reverse, *, name):
    length = la.shape[0]
    nch = length // GLA_CHUNK
    scale = GLA_HEAD_DIM ** -0.5
    last = 0 if reverse else GLA_CHUNK - 1
    hd = GLA_HEAD_DIM

    def body(q_ref, k_ref, v_ref, la_ref, o_ref, sp_ref, st_ref):
        @pl.when(pl.program_id(0) == 0)
        def _():
            st_ref[...] = jnp.zeros_like(st_ref)

        mask = _gla_chunk_mask(reverse)
        b = _exact_dot(mask.astype(BF16), la_ref[...], _NN)
        sp_ref[0] = st_ref[...]
        outs = []
        for h in range(GLA_HEADS):
            sl = slice(h * hd, (h + 1) * hd)
            bh = b[:, sl]
            bl = bh[last:last + 1, :]
            k = k_ref[:, sl]
            v = v_ref[:, sl]
            qd = q_ref[:, sl] * scale * jnp.exp(bh)
            kd = k * jnp.exp(-bh)
            ke = k * jnp.exp(bl - bh)
            st = st_ref[sl, :]
            p = jnp.where(mask, _dot(qd, kd, _NT), 0.0)
            outs.append(_dot(p, v) + _dot(qd, st, _NT))
            st_ref[sl, :] = st * jnp.exp(bl) + _dot(v, ke, _TN)
        o_ref[...] = jnp.concatenate(outs, axis=1)

    cmap = (lambda n: nch - 1 - n) if reverse else (lambda n: n)
    col = lambda cb: pl.BlockSpec((GLA_CHUNK, GLA_WIDTH), lambda n, cb=cb: (cmap(n), cb))
    return pl.pallas_call(
        body, name=name, grid=(nch,),
        in_specs=[col(CB_GQ), col(CB_GK), col(CB_GV), col(0)],
        out_specs=[col(0), pl.BlockSpec((1, GLA_WIDTH, hd), lambda n: (cmap(n), 0, 0))],
        out_shape=[jax.ShapeDtypeStruct((length, GLA_WIDTH), F32), jax.ShapeDtypeStruct((nch, GLA_WIDTH, hd), F32)],
        scratch_shapes=[pltpu.VMEM((GLA_WIDTH, hd), F32)],
        compiler_params=pltpu.CompilerParams(dimension_semantics=("arbitrary",)),
    )(pg, pg, pg, la)


def _gla_bwd(pg, la, do, sprev, reverse, *, name):
    length = la.shape[0]
    nch = length // GLA_CHUNK
    scale = GLA_HEAD_DIM ** -0.5
    last = 0 if reverse else GLA_CHUNK - 1
    hd = GLA_HEAD_DIM

    def body(q_ref, k_ref, v_ref, la_ref, do_ref, sp_ref, dq_ref, dk_ref, dv_ref, dla_ref, dst_ref):
        @pl.when(pl.program_id(0) == 0)
        def _():
            dst_ref[...] = jnp.zeros_like(dst_ref)

        mask = _gla_chunk_mask(reverse)
        tri = mask.astype(BF16)
        b = _exact_dot(tri, la_ref[...], _NN)
        is_last = lax.broadcasted_iota(jnp.int32, (GLA_CHUNK, hd), 0) == last
        dqs, dks, dvs, dbs = [], [], [], []
        for h in range(GLA_HEADS):
            sl = slice(h * hd, (h + 1) * hd)
            bh = b[:, sl]
            bl = bh[last:last + 1, :]
            eb, enb, ebl, el = jnp.exp(bh), jnp.exp(-bh), jnp.exp(bl - bh), jnp.exp(bl)
            k = k_ref[:, sl]
            v = v_ref[:, sl]
            dov = do_ref[:, sl]
            qd = q_ref[:, sl] * scale * eb
            kd = k * enb
            ke = k * ebl
            st = sp_ref[0, sl, :]
            dst = dst_ref[sl, :]
            p = jnp.where(mask, _dot(qd, kd, _NT), 0.0)
            dp = jnp.where(mask, _dot(dov, v, _NT), 0.0)
            dqd = _dot(dp, kd) + _dot(dov, st)
            dkd = _dot(dp, qd, _TN)
            dvs.append(_dot(p, dov, _TN) + _dot(ke, dst, _NT))
            dke = _dot(v, dst)
            dst_ref[sl, :] = dst * el + _dot(dov, qd, _TN)
            dbl = el * jnp.sum(dst * st, axis=0, keepdims=True) + jnp.sum(dke * ke, axis=0, keepdims=True)
            db = dqd * qd - dkd * kd - dke * ke
            dbs.append(jnp.where(is_last, db + dbl, db))
            dqs.append(dqd * eb * scale)
            dks.append(dkd * enb + dke * ebl)
        dq_ref[...] = jnp.concatenate(dqs, axis=1)
        dk_ref[...] = jnp.concatenate(dks, axis=1)
        dv_ref[...] = jnp.concatenate(dvs, axis=1)
        tri_t = _gla_chunk_mask(not reverse).astype(BF16)
        dla_ref[...] = _exact_dot(tri_t, jnp.concatenate(dbs, axis=1), _NN)

    cmap = (lambda n: n) if reverse else (lambda n: nch - 1 - n)
    col = lambda cb: pl.BlockSpec((GLA_CHUNK, GLA_WIDTH), lambda n, cb=cb: (cmap(n), cb))
    wide = jax.ShapeDtypeStruct((length, GLA_WIDTH), F32)
    return pl.pallas_call(
        body, name=name, grid=(nch,),
        in_specs=[col(CB_GQ), col(CB_GK), col(CB_GV), col(0), col(0),
                  pl.BlockSpec((1, GLA_WIDTH, hd), lambda n: (cmap(n), 0, 0))],
        out_specs=[col(0)] * 4, out_shape=[wide] * 4,
        scratch_shapes=[pltpu.VMEM((GLA_WIDTH, hd), F32)],
        compiler_params=pltpu.CompilerParams(dimension_semantics=("arbitrary",)),
    )(pg, pg, pg, la, do, sprev)


def _log_sigmoid(x):
    return jnp.minimum(x, 0.0) - jnp.log(1.0 + jnp.exp(-jnp.abs(x)))


def _gla_alpha_padded(w_alpha):
    w = jnp.zeros((2, 128, GLA_WIDTH), w_alpha.dtype)
    w = w.at[0, 0:GLA_LOWRANK].set(w_alpha[0])
    return w.at[1, GLA_LOWRANK:2 * GLA_LOWRANK].set(w_alpha[1])


def _gla_branch_fwd(pg, w_alpha, b_alpha, norm_gain):
    wa = _gla_alpha_padded(w_alpha).astype(BF16)

    def gates(z, w, bias):
        return (_log_sigmoid(_dot(z, w[0]) + bias[0:1]) / GLA_TAU, _log_sigmoid(_dot(z, w[1]) + bias[1:2]) / GLA_TAU)
    la_f, la_b = _rowmap(gates, [(pg, CB_Z, 128)], [wa, b_alpha], [(GLA_WIDTH, F32), (GLA_WIDTH, F32)], tl=512,
                         name="gla_gates")
    o_f, sp_f = _gla_fwd(pg, la_f, False, name="gla_fwd")
    o_b, sp_b = _gla_fwd(pg, la_b, True, name="gla_fwd_rev")

    def post(of, ob, gate, gn):
        o = of + ob
        on = jnp.concatenate([o[:, s:s + GLA_HEAD_DIM] * _rms(o[:, s:s + GLA_HEAD_DIM]) * gn
                              for s in range(0, GLA_WIDTH, GLA_HEAD_DIM)], axis=1)
        return o, on * (gate * _sigmoid(gate))
    o, y = _rowmap(post, [o_f, o_b, (pg, CB_GG, GLA_WIDTH)], [norm_gain.reshape(1, -1)],
                   [(GLA_WIDTH, F32), (GLA_WIDTH, BF16)], tl=512, name="gla_post")
    return y, (wa, la_f, la_b, sp_f, sp_b, o)


def _gla_branch_bwd(pg, w_alpha, b_alpha, norm_gain, saved, dy):
    wa, la_f, la_b, sp_f, sp_b, o = saved

    def post_bwd(dyv, ov, gate, gn):
        s = _sigmoid(gate)
        dos, dgn, ons = [], [], []
        for c in range(0, GLA_WIDTH, GLA_HEAD_DIM):
            oh = ov[:, c:c + GLA_HEAD_DIM]
            r = _rms(oh)
            don = dyv[:, c:c + GLA_HEAD_DIM] * (gate[:, c:c + GLA_HEAD_DIM] * s[:, c:c + GLA_HEAD_DIM])
            gd = don * gn
            dos.append(r * gd - oh * (r * r * r) * jnp.mean(oh * gd, axis=-1, keepdims=True))
            dgn.append(jnp.sum(don * oh * r, axis=0, keepdims=True))
            ons.append(oh * r * gn)
        on = jnp.concatenate(ons, axis=1)
        dgate = dyv * on * (s * (1.0 + gate * (1.0 - s)))
        return jnp.concatenate(dos, axis=1), dgate, jnp.concatenate(dgn, axis=1)
    do, dgate, dgn = _rowmap(post_bwd, [dy, o, (pg, CB_GG, GLA_WIDTH)], [norm_gain.reshape(1, -1)],
                             [(GLA_WIDTH, F32), (GLA_WIDTH, F32)], [(1, GLA_WIDTH)], tl=512, name="gla_post_bwd")
    dq_f, dk_f, dv_f, dla_f = _gla_bwd(pg, la_f, do, sp_f, False, name="gla_bwd")
    dq_b, dk_b, dv_b, dla_b = _gla_bwd(pg, la_b, do, sp_b, True, name="gla_bwd_rev")

    def gates_bwd(z, dlf, dlb, w, bias):
        dz = jnp.zeros_like(z)
        dlogits, dbs = [], []
        for d, dl in ((0, dlf), (1, dlb)):
            logit = _dot(z, w[d]) + bias[d:d + 1]
            dlogit = dl * (1.0 / GLA_TAU) * jnp.exp(_log_sigmoid(-logit))
            dz = dz + _dot(dlogit, w[d], _NT)
            dlogits.append(dlogit)
            dbs.append(jnp.sum(dlogit, axis=0, keepdims=True))
        return dz, dlogits[0], dlogits[1], dbs[0], dbs[1]
    dz, dlg_f, dlg_b, dba_f, dba_b = _rowmap(
        gates_bwd, [(pg, CB_Z, 128), dla_f, dla_b], [wa, b_alpha], [(128, F32), (GLA_WIDTH, BF16), (GLA_WIDTH, BF16)],
        [(1, GLA_WIDTH), (1, GLA_WIDTH)], tl=512, name="gla_gates_bwd")
    dwa_f = _mm(dlg_f, (pg, CB_Z, 128), ta=True, name="gla_dwalpha")
    dwa_b = _mm(dlg_b, (pg, CB_Z, 128), ta=True, name="gla_dwalpha")
    grads = {'w_alpha': jnp.stack([dwa_f[:, 0:GLA_LOWRANK].T, dwa_b[:, GLA_LOWRANK:2 * GLA_LOWRANK].T]),
             'b_alpha': jnp.concatenate([dba_f, dba_b], axis=0),
             'norm': jnp.sum(dgn.reshape(GLA_HEADS, GLA_HEAD_DIM), axis=0)}
    return [dq_f, dq_b], [dk_f, dk_b], [dv_f, dv_b], dgate, dz, grads


def _rope_tables(length):
    half = ATTN_HEAD_DIM // 2
    inv_freq = ROPE_BASE ** (-jnp.arange(half // 2, dtype=F32) * 2.0 / half)
    t = jnp.arange(length, dtype=jnp.int32)
    def one(pos):
        ang = pos.astype(F32)[:, None] * inv_freq[None, :]
        c, s = jnp.cos(ang), jnp.sin(ang)
        return jnp.concatenate([c, c], axis=1), jnp.concatenate([-s, s], axis=1)
    c_r, s_r = one(t // GRID_W)
    c_c, s_c = one(t % GRID_W)
    return jnp.concatenate([c_r, c_c], axis=1), jnp.concatenate([s_r, s_c], axis=1)


def _rope_swap(y):
    w = y.shape[1]
    lane = lax.broadcasted_iota(jnp.int32, y.shape, 1)
    return jnp.where(lane % 32 < 16, pltpu.roll(y, w - 16, 1), pltpu.roll(y, 16, 1))


def _head_sums(x, ones):
    parts = [lax.dot_general(p, ones, _NN, preferred_element_type=F32) for p in _split3(x)]
    return parts[0] + parts[1] + parts[2]


def _head_ones(width):
    seg = np.arange(width) // ATTN_HEAD_DIM
    return jnp.asarray(seg[:, None] == seg[None, :], BF16)


def _qk_prep_fwd(pg, cb, width, gain, cos, sin, scale, *, name):
    heads = width // ATTN_HEAD_DIM
    def fn(x, c, s, g, ones):
        r = lax.rsqrt(_head_sums(x * x, ones) * (1.0 / ATTN_HEAD_DIM) + NORM_EPS)
        y = x * r * g
        return (y * c + _rope_swap(y) * s) * scale
    return _rowmap(fn, [(pg, cb, width), jnp.tile(cos, (1, heads)), jnp.tile(sin, (1, heads))],
                   [jnp.tile(gain, heads).reshape(1, -1), _head_ones(width)], [(width, BF16)], tl=512, name=name)[0]


def _qk_prep_bwd(pg, cb, width, gain, cos, sin, scale, dout, *, name):
    heads = width // ATTN_HEAD_DIM
    def fn(x, dov, c, s, g, ones):
        r = lax.rsqrt(_head_sums(x * x, ones) * (1.0 / ATTN_HEAD_DIM) + NORM_EPS)
        dos = dov * scale
        dy = dos * c + _rope_swap(dos * s)
        gd = dy * g
        dx = r * gd - x * (r * r * r) * (_head_sums(x * gd, ones) * (1.0 / ATTN_HEAD_DIM))
        return dx, jnp.sum(dy * x * r, axis=0, keepdims=True)
    dx, dg = _rowmap(fn, [(pg, cb, width), dout, jnp.tile(cos, (1, heads)), jnp.tile(sin, (1, heads))],
                     [jnp.tile(gain, heads).reshape(1, -1), _head_ones(width)], [(width, F32)], [(1, width)], tl=512,
                     name=name)
    return dx, jnp.sum(dg.reshape(heads, ATTN_HEAD_DIM), axis=0)


def _to_heads(x, heads):
    return jnp.transpose(x.reshape(x.shape[0], heads, ATTN_HEAD_DIM), (1, 0, 2))


def _from_heads(x):
    return jnp.transpose(x, (1, 0, 2)).reshape(x.shape[1], x.shape[0] * ATTN_HEAD_DIM)


ATTN_GROUP = ATTN_Q_HEADS // ATTN_KV_HEADS
ATTN_TQ = 256


def _attn_fwd(q, k, v, side=None):
    length = q.shape[1]
    tq = min(ATTN_TQ, length)
    grid = (ATTN_KV_HEADS, length // tq)

    def compute(refs):
        q_ref, k_ref, v_ref, o_ref = refs
        kk, vv = k_ref[0], v_ref[0]
        for g in range(ATTN_GROUP):
            s = _dot(q_ref[g], kk, _NT)
            p = jnp.exp(s - jnp.max(s, axis=-1, keepdims=True))
            o_ref[g] = _dot(p, vv) / jnp.sum(p, axis=-1, keepdims=True)

    def body(*refs):
        _carried(side, grid, refs, 3, 1, 0, compute)

    kv_spec = pl.BlockSpec((1, length, ATTN_HEAD_DIM), lambda h, i: (h, 0, 0))
    q_spec = pl.BlockSpec((ATTN_GROUP, tq, ATTN_HEAD_DIM), lambda h, i: (h, i, 0))
    (out,), gathered = _side_call(
        body, side, name="attn_fwd", grid=grid, in_specs=[q_spec, kv_spec, kv_spec], out_specs=[q_spec],
        out_shape=[jax.ShapeDtypeStruct(q.shape, F32)], scratch=[], args=[q, k, v], semantics=("parallel", "parallel"))
    return out, gathered


def _attn_bwd(q, k, v, o, do, side=None):
    length = q.shape[1]
    tq = min(ATTN_TQ, length)
    grid = (ATTN_KV_HEADS, length // tq)

    def body(*refs):
        _carried(side, grid, refs, 5, 3, 0, compute)

    def compute(refs):
        q_ref, k_ref, v_ref, o_ref, do_ref, dq_ref, dk_ref, dv_ref = refs

        @pl.when(pl.program_id(1) == 0)
        def _():
            dk_ref[...] = jnp.zeros_like(dk_ref)
            dv_ref[...] = jnp.zeros_like(dv_ref)

        kk, vv = k_ref[0], v_ref[0]
        for g in range(ATTN_GROUP):
            qg, dog = q_ref[g], do_ref[g]
            s = _dot(qg, kk, _NT)
            p = jnp.exp(s - jnp.max(s, axis=-1, keepdims=True))
            p = p * (1.0 / jnp.sum(p, axis=-1, keepdims=True))
            dp = _dot(dog, vv, _NT)
            ds = p * (dp - jnp.sum(dog * o_ref[g], axis=-1, keepdims=True))
            dq_ref[g] = _dot(ds, kk)
            dk_ref[0] += _dot(ds, qg, _TN)
            dv_ref[0] += _dot(p, dog, _TN)

    kv_spec = pl.BlockSpec((1, length, ATTN_HEAD_DIM), lambda h, i: (h, 0, 0))
    q_spec = pl.BlockSpec((ATTN_GROUP, tq, ATTN_HEAD_DIM), lambda h, i: (h, i, 0))
    return _side_call(
        body, side, name="attn_bwd", grid=grid, in_specs=[q_spec, kv_spec, kv_spec, q_spec, q_spec],
        out_specs=[q_spec, kv_spec, kv_spec],
        out_shape=[jax.ShapeDtypeStruct(q.shape, F32), jax.ShapeDtypeStruct(k.shape, F32), jax.ShapeDtypeStruct(k.shape, F32)],
        scratch=[], args=[q, k, v, o, do], semantics=("parallel", "arbitrary"))


def _attn_branch_fwd(pg, q_gain, k_gain, side=None):
    cos, sin = _rope_tables(pg.shape[0])
    qp = _qk_prep_fwd(pg, CB_AQ, ATTN_WIDTH, q_gain, cos, sin, ATTN_HEAD_DIM ** -0.5, name="attn_q_prep")
    kp = _qk_prep_fwd(pg, CB_AK, ATTN_KV_WIDTH, k_gain, cos, sin, 1.0, name="attn_k_prep")
    qh, kh = _to_heads(qp, ATTN_Q_HEADS), _to_heads(kp, ATTN_KV_HEADS)
    vh = _to_heads(pg[:, P_OFF + 3200:P_OFF + 3328].astype(BF16), ATTN_KV_HEADS)
    oh, gathered = _attn_fwd(qh, kh, vh, side)
    return _from_heads(oh).astype(BF16), (cos, sin, qh, kh, vh, oh), gathered


def _attn_branch_bwd(pg, q_gain, k_gain, saved, dy, side=None):
    cos, sin, qh, kh, vh, oh = saved
    (dqh, dkh, dvh), carried = _attn_bwd(qh, kh, vh, oh, _to_heads(dy, ATTN_Q_HEADS), side)
    dq, dqg = _qk_prep_bwd(pg, CB_AQ, ATTN_WIDTH, q_gain, cos, sin, ATTN_HEAD_DIM ** -0.5, _from_heads(dqh),
                           name="attn_q_prep_bwd")
    dk, dkg = _qk_prep_bwd(pg, CB_AK, ATTN_KV_WIDTH, k_gain, cos, sin, 1.0, _from_heads(dkh), name="attn_k_prep_bwd")
    return dq, dk, _from_heads(dvh), {'q_norm': dqg, 'k_norm': dkg}, carried


def _gate_cols():
    return [slice(i * D_MODEL, (i + 1) * D_MODEL) for i in range(3)]


def _mixer_fwd(x, lw, side_in=None, after_in=None, side_s5=None, side_attn=None):
    h = _rmsnorm_fwd(x, lw['mix_norm'])
    if side_in is None:
        pg = _mm(h, lw['w_pg'], name="mix_in")
    else:
        pg, got_in = _mm(h, lw['w_pg'], side=side_in, name="mix_in")
        after_in(got_in)
    y_s5, s_s5, got_s5 = _s5_fwd(pg, lw['s5'], lw['s5_w_glu'], side_s5)
    y_gla, s_gla = _gla_branch_fwd(pg, lw['gla_w_alpha'], lw['gla_b_alpha'], lw['gla_norm'])
    y_att, s_att, got_attn = _attn_branch_fwd(pg, lw['attn_q_norm'], lw['attn_k_norm'], side_attn)
    ys = (y_s5, y_gla, y_att)
    br = [_mm(y, lw[n], name="mix_branch") for y, n in zip(ys, ('w_branch_s5', 'w_branch_gla', 'w_branch_attn'))]

    def merge(g0, g1, g2, b0, b1, b2, bias):
        acc = None
        for g, b, c in zip((g0, g1, g2), (b0, b1, b2), _gate_cols()):
            term = _sigmoid(g + bias[:, c]) * b
            acc = term if acc is None else acc + term
        return acc
    merged = _rowmap(merge, [(pg, 0, D_MODEL), (pg, 1, D_MODEL), (pg, 2, D_MODEL)] + br,
                     [lw['b_merge_gate'].reshape(1, -1)], [(D_MODEL, BF16)], tl=256, name="mix_merge")[0]
    x_out = _mm(merged, lw['w_out'], add=x, name="mix_out")
    return x_out, (x, h, pg, ys, (s_s5, s_gla, s_att), br, merged), (got_s5, got_attn)


def _mixer_bwd(saved, lw, dx_out, side=None):
    x, h, pg, ys, (s_s5, s_gla, s_att), br, merged = saved
    grads = {'w_out': _mm(merged, dx_out, ta=True, out_dtype=BF16, name="mix_dwout")}
    dmerged = _mm(dx_out, lw['w_out'], tb=True, name="mix_dmerged")

    def merge_bwd(g0, g1, g2, b0, b1, b2, dm, bias):
        dbr, dgp = [], []
        for g, b, c in zip((g0, g1, g2), (b0, b1, b2), _gate_cols()):
            s = _sigmoid(g + bias[:, c])
            dbr.append(dm * s)
            dgp.append(dm * b * (s * (1.0 - s)))
        dgp = jnp.concatenate(dgp, axis=1)
        return dbr[0], dbr[1], dbr[2], dgp, jnp.sum(dgp, axis=0, keepdims=True)
    d0, d1, d2, dgpre, dbias = _rowmap(
        merge_bwd, [(pg, 0, D_MODEL), (pg, 1, D_MODEL), (pg, 2, D_MODEL)] + br + [dmerged],
        [lw['b_merge_gate'].reshape(1, -1)], [(D_MODEL, BF16)] * 3 + [(GATE_WIDTH, BF16)], [(1, GATE_WIDTH)], tl=256,
        name="mix_merge_bwd")
    grads['b_merge_gate'] = dbias[0]
    dys = []
    for y, dbr, n in zip(ys, (d0, d1, d2), ('w_branch_s5', 'w_branch_gla', 'w_branch_attn')):
        grads[n] = _mm(y, dbr, ta=True, out_dtype=BF16, name="mix_dwbranch")
        dys.append(_mm(dbr, lw[n], tb=True, name="mix_dy"))
    du, g_s5 = _s5_bwd(pg, lw['s5'], lw['s5_w_glu'], s_s5, dys[0])
    dgq, dgk, dgv, dgg, dz, g_gla = _gla_branch_bwd(pg, lw['gla_w_alpha'], lw['gla_b_alpha'], lw['gla_norm'], s_gla, dys[1])
    daq, dak, dav, g_att, carried = _attn_branch_bwd(pg, lw['attn_q_norm'], lw['attn_k_norm'], s_att, dys[2], side)

    def assemble(dgp, u0, u1, u2, q0, q1, k0, k1, v0, v1, gg, aq, ak, av, z):
        pad = jnp.zeros((dgp.shape[0], IN_PAD - 3456), F32)
        parts = [dgp.astype(F32), u0 + u1 + u2, q0 + q1, k0 + k1, v0 + v1, gg, aq, ak, av, z, pad]
        return jnp.concatenate(parts, axis=1)
    dpg = _rowmap(assemble, [dgpre] + du + dgq + dgk + dgv + [dgg, daq, dak, dav, dz], [], [(PG_WIDTH, BF16)], tl=256,
                  name="mix_dpg")[0]
    grads['w_pg'] = _mm(h, dpg, ta=True, out_dtype=BF16, name="mix_dwpg")
    dh = _mm(dpg, lw['w_pg'], tb=True, name="mix_dh")
    dx, grads['mix_norm'] = _rmsnorm_bwd(x, lw['mix_norm'], dh, dx_out)
    grads['s5'], grads['gla'], grads['attn'] = g_s5, g_gla, g_att
    return dx, grads, carried


def _loss_head(x, gain, target):
    width = x.shape[1]

    def fn(xv, tv, g):
        r = _rms(xv)
        err = xv * r * g - tv
        dy = err * (1.0 / width)
        gd = dy * g
        dx = r * gd - xv * (r * r * r) * jnp.mean(xv * gd, axis=-1, keepdims=True)
        loss = jnp.sum(0.5 * jnp.mean(err * err, axis=-1, keepdims=True), axis=0, keepdims=True)
        return dx, jnp.broadcast_to(loss, (1, 128)), jnp.sum(dy * xv * r, axis=0, keepdims=True)
    dx, loss, dgain = _rowmap(fn, [x, target], [gain.reshape(1, -1)], [(width, F32)], [(1, 128), (1, width)], tl=256,
                              name="loss_head")
    return loss[0, 0], dx, dgain[0]


def _row_tile(rows, cap=256):
    for t in range(cap - cap % 16, 0, -16):
        if rows % t == 0:
            return t
    return rows


def _reduce_adamw(parts, w, m, v, *, name):
    r, c = w.shape
    if len(parts) > 1 and parts[0].shape[1] % 8:
        parts = [jnp.concatenate(parts, axis=1)]
    nparts, rows = parts[0].shape[0], parts[0].shape[1]
    tr = _row_tile(rows)
    per = rows // tr

    def body(*refs):
        p_refs, (w_ref, m_ref, v_ref, g_ref, d_ref, m2_ref, v2_ref) = refs[:len(parts)], refs[len(parts):]
        g = None
        for k, p_ref in enumerate(p_refs):
            gk = p_ref[0].astype(F32)
            for j in range(1, nparts):
                gk = gk + p_ref[j].astype(F32)
            g = gk if g is None else jnp.where(pl.program_id(0) // per == k, gk, g)
        m2 = ADAM_B1 * m_ref[...] + (1.0 - ADAM_B1) * g
        v2 = ADAM_B2 * v_ref[...] + (1.0 - ADAM_B2) * (g * g)
        m_hat = m2 / (1.0 - ADAM_B1 ** ADAM_STEP)
        v_hat = v2 / (1.0 - ADAM_B2 ** ADAM_STEP)
        g_ref[...] = g
        d_ref[...] = -ADAM_LR * (m_hat / (jnp.sqrt(v_hat) + ADAM_EPS) + ADAM_WD * w_ref[...])
        m2_ref[...] = m2
        v2_ref[...] = v2

    flat = pl.BlockSpec((tr, c), lambda i: (i, 0))
    p_specs = [pl.BlockSpec((nparts, tr, c), lambda i, k=k: (0, jnp.clip(i - k * per, 0, per - 1), 0)) for k in range(len(parts))]
    return pl.pallas_call(
        body, name=name, grid=(r // tr,), in_specs=p_specs + [flat, flat, flat],
        out_specs=[flat] * 4, out_shape=[jax.ShapeDtypeStruct((r, c), F32)] * 4,
        compiler_params=pltpu.CompilerParams(dimension_semantics=("parallel",)),
    )(*parts, w, m, v)


def _all_gather(blocks, *, name):
    side = _SideGather(blocks)

    def body(*refs):
        start, finish = side.hooks(refs)
        start()
        finish()

    return pl.pallas_call(body, name=name, out_shape=side.out_shape, in_specs=side.in_specs, out_specs=side.out_specs,
                          scratch_shapes=side.scratch)(*blocks)


class _SideGather:
    def __init__(self, blocks):
        self.blocks = list(blocks)
        self.n = n = len(self.blocks)
        hbm = pl.BlockSpec(memory_space=pl.ANY)
        self.in_specs, self.out_specs = [hbm] * n, [hbm] * n
        self.out_shape = [jax.ShapeDtypeStruct((N_DEV,) + b.shape, b.dtype) for b in self.blocks]
        self.scratch = [pltpu.SemaphoreType.DMA((n, 7)), pltpu.SemaphoreType.DMA((n, 7)), pltpu.SemaphoreType.DMA((n,))]

    def hooks(self, refs):
        n = self.n
        x_refs, out_refs = refs[:n], refs[n:2 * n]
        send_sems, recv_sems, local_sems = refs[2 * n:]
        x, y, c = lax.axis_index("x"), lax.axis_index("y"), lax.axis_index("c")
        me, sibling = (x, y, c), (x, y, 1 - c)
        chips = [(1 - x, y), (x, 1 - y), (1 - x, 1 - y)]

        def slot(t, px, py, pc):
            return out_refs[t].at[4 * px + 2 * py + pc]

        def copy(t, k, blk, to, own=False):
            return pltpu.make_async_remote_copy(
                src_ref=x_refs[t] if own else slot(t, *blk), dst_ref=slot(t, *blk), send_sem=send_sems.at[t, k],
                recv_sem=recv_sems.at[t, k], device_id=to, device_id_type=pl.DeviceIdType.MESH)

        def mine(t):
            return pltpu.make_async_copy(x_refs[t], slot(t, *me), local_sems.at[t])

        def first(t):
            return [copy(t, 0, me, sibling, own=True)] + [copy(t, 1 + j, me, (*chip, c), own=True) for j, chip in enumerate(chips)]

        def start():
            for t in range(n):
                mine(t).start()
            for t in range(n):
                for cp in first(t):
                    cp.start()

        def finish():
            passed = []
            for j, chip in enumerate(chips):
                for t in range(n):
                    copy(t, 1 + j, (*chip, c), me).wait_recv()
                    passed.append(copy(t, 4 + j, (*chip, c), sibling))
                    passed[-1].start()
            for t in range(n):
                copy(t, 0, sibling, me).wait_recv()
            for j, chip in enumerate(chips):
                for t in range(n):
                    copy(t, 4 + j, (*chip, 1 - c), me).wait_recv()
            for t in range(n):
                for cp in first(t):
                    cp.wait_send()
            for cp in passed:
                cp.wait_send()
            for t in range(n):
                mine(t).wait()

        return start, finish


def _first_last_step(grid):
    ids = [pl.program_id(a) for a in range(len(grid))]
    first = functools.reduce(lambda p, q: p & q, [i == 0 for i in ids])
    last = functools.reduce(lambda p, q: p & q, [i == n - 1 for i, n in zip(ids, grid)])
    return first, last


def _carried(side, grid, refs, n_in, n_out, n_scratch, compute):
    if side is None:
        compute(refs)
        return
    n = side.n
    main = refs[:n_in] + refs[n_in + n:n_in + n + n_out] + refs[n_in + 2 * n + n_out:n_in + 2 * n + n_out + n_scratch]
    side_refs = refs[n_in:n_in + n] + refs[n_in + n + n_out:n_in + 2 * n + n_out] + refs[n_in + 2 * n + n_out + n_scratch:]
    start, finish = side.hooks(side_refs)
    first, last = _first_last_step(grid)
    pl.when(first)(start)
    compute(main)
    pl.when(last)(finish)


N_CHIP = N_DEV // 2


def _swap_with_sibling(arrays, *, name):
    n = len(arrays)

    def body(*refs):
        src_refs, out_refs = refs[:n], refs[n:2 * n]
        send_sems, recv_sems = refs[2 * n:]
        sibling = (lax.axis_index("x"), lax.axis_index("y"), 1 - lax.axis_index("c"))
        copies = [pltpu.make_async_remote_copy(
            src_ref=src_refs[t], dst_ref=out_refs[t], send_sem=send_sems.at[t], recv_sem=recv_sems.at[t],
            device_id=sibling, device_id_type=pl.DeviceIdType.MESH) for t in range(n)]
        for cp in copies:
            cp.start()
        for cp in copies:
            cp.wait()

    hbm = pl.BlockSpec(memory_space=pl.ANY)
    return pl.pallas_call(
        body, name=name, out_shape=[jax.ShapeDtypeStruct(a.shape, a.dtype) for a in arrays],
        in_specs=[hbm] * n, out_specs=[hbm] * n,
        scratch_shapes=[pltpu.SemaphoreType.DMA((n,)), pltpu.SemaphoreType.DMA((n,))],
    )(*arrays)


def _exchange_chips(stacks, *, name):
    side = _SideChipExchange(stacks)

    def body(*refs):
        start, finish = side.hooks(refs)
        start()
        finish()

    return pl.pallas_call(body, name=name, out_shape=side.out_shape, in_specs=side.in_specs, out_specs=side.out_specs,
                          scratch_shapes=side.scratch)(*stacks)


class _SideChipExchange:
    def __init__(self, stacks):
        self.blocks = list(stacks)
        self.n = n = len(self.blocks)
        hbm = pl.BlockSpec(memory_space=pl.ANY)
        self.in_specs, self.out_specs = [hbm] * n, [hbm] * n
        self.out_shape = [jax.ShapeDtypeStruct(s.shape, s.dtype) for s in self.blocks]
        self.scratch = [pltpu.SemaphoreType.DMA((n, N_DEV - 1)), pltpu.SemaphoreType.DMA((n, N_DEV - 1)),
                        pltpu.SemaphoreType.DMA((n,))]

    def hooks(self, refs):
        n = self.n
        g_refs, out_refs = refs[:n], refs[n:2 * n]
        send_sems, recv_sems, local_sems = refs[2 * n:]
        x, y, c = lax.axis_index("x"), lax.axis_index("y"), lax.axis_index("c")

        def copies():
            mine, remote = [], []
            for t in range(n):
                peers = self.blocks[t].shape[0]
                me = 2 * x + y if peers == N_CHIP else 4 * x + 2 * y + c
                mine.append(pltpu.make_async_copy(g_refs[t].at[me], out_refs[t].at[me], local_sems.at[t]))
                for k in range(1, peers):
                    if peers == N_CHIP:
                        px, py, pc = x ^ (k >> 1 & 1), y ^ (k & 1), c
                        slot = 2 * px + py
                    else:
                        px, py, pc = x ^ (k >> 2 & 1), y ^ (k >> 1 & 1), c ^ (k & 1)
                        slot = 4 * px + 2 * py + pc
                    remote.append(pltpu.make_async_remote_copy(
                        src_ref=g_refs[t].at[slot], dst_ref=out_refs[t].at[me], send_sem=send_sems.at[t, k - 1],
                        recv_sem=recv_sems.at[t, k - 1], device_id=(px, py, pc), device_id_type=pl.DeviceIdType.MESH))
            return mine, remote

        def start():
            mine, remote = copies()
            for cp in mine + remote:
                cp.start()

        def finish():
            mine, remote = copies()
            for cp in remote:
                cp.wait_recv()
            for cp in remote:
                cp.wait_send()
            for cp in mine:
                cp.wait()

        return start, finish


def _pair_sum(a, b):
    return _rowmap(lambda u, v: u.astype(F32) + v.astype(F32), [a, b], [], [(a.shape[1], BF16)], tl=_row_tile(a.shape[0], 512),
                   name="pair_sum")[0]


SMALL_COLS = 128


def _pack_small(arrays):
    flat = jnp.concatenate([a.astype(F32).reshape(-1, SMALL_COLS) for a in arrays], axis=0)
    return jnp.pad(flat, ((0, -flat.shape[0] % 256), (0, 0)))


def _unpack_small(packed, shapes):
    out, off = [], 0
    for s in shapes:
        r = math.prod(s) // SMALL_COLS
        out.append(packed[off:off + r].reshape(s))
        off += r
    return out


def _split_shards(full, axis):
    shape = full.shape
    split = full.reshape(shape[:axis] + (N_DEV, shape[axis] // N_DEV) + shape[axis + 1:])
    return jnp.moveaxis(split, axis, 0)


def _join_shards(stack, axis):
    moved = jnp.moveaxis(stack, 0, axis)
    shape = moved.shape
    return moved.reshape(shape[:axis] + (shape[axis] * shape[axis + 1],) + shape[axis + 2:])


def _w_in_unpadded(w):
    return jnp.concatenate([w[..., :2560], w[..., 3328:3360], w[..., 2560:3328]], axis=-1)


FFN1_W = ('ffn1_w_gate', 'ffn1_w_up', 'ffn1_w_down')
FFN2_W = ('ffn2_w_gate', 'ffn2_w_up', 'ffn2_w_down')
MIX_IN_W = ('w_in', 'w_merge_gate')
MIX_REST_W = ('s5_w_glu', 'gla_w_alpha', 'gla_b_alpha', 'w_branch_s5', 'w_branch_gla', 'w_branch_attn', 'w_out')


def _mixer_weights(full, w, s5, i):
    lw = {n: w[n][i] for n in ('mix_norm', 'gla_norm', 'attn_q_norm', 'attn_k_norm', 'b_merge_gate')}
    lw['s5'] = {'b_mat': s5['b_mat'][i], 'c_mat': s5['c_mat'][i], 'tabs': [t[i] for t in s5['tabs']],
                'tabs_adj': [t[i] for t in s5['tabs_adj']], 'd': w['s5_d'][i]}
    w_in = full['w_in']
    pad = jnp.zeros((D_MODEL, IN_PAD - IN_WIDTH), w_in.dtype)
    lw['w_pg'] = jnp.concatenate([full['w_merge_gate'], w_in[:, :2560], w_in[:, 2592:], w_in[:, 2560:2592], pad], axis=1)
    return lw


def _mixer_weights_rest(full):
    lw = {n: full[n] for n in MIX_REST_W if n != 'gla_b_alpha'}
    lw['gla_b_alpha'] = full['gla_b_alpha'].astype(F32)
    return lw


def _chip_sums(grads, names):
    core = lax.axis_index("c")
    own, for_sibling = [], []
    for n in names:
        by_owner = _split_shards(grads[n], SHARD_AXIS[n] - 1).astype(BF16)
        by_owner = by_owner.reshape((N_CHIP, 2) + by_owner.shape[1:])
        own.append(lax.dynamic_index_in_dim(by_owner, core, axis=1, keepdims=False))
        for_sibling.append(lax.dynamic_index_in_dim(by_owner, 1 - core, axis=1, keepdims=False))
    from_sibling = _swap_with_sibling(for_sibling, name="exchange_grads_sibling")
    return [_pair_sum(a.reshape(-1, a.shape[-1]), b.reshape(-1, b.shape[-1])).reshape(a.shape)
            for a, b in zip(own, from_sibling)]


def _step_local(x, target, w, shards):
    s5 = _s5_setup(w)
    full = [{} for _ in range(DEPTH)]

    def wanted(i, *groups):
        return _SideGather([shards[n][i] for names in groups for n in names])

    def arrived(i, stacks, *groups):
        names = [n for group in groups for n in group]
        for n, st in zip(names, stacks):
            full[i][n] = _join_shards(st, SHARD_AXIS[n] - 1)

    arrived(0, _all_gather([shards[n][0] for n in FFN1_W], name="gather_first"), FFN1_W)
    saved, lws = [], []
    for i in range(DEPTH):
        f, first = full[i], i == 0
        x, s1, got = _ffn_fwd(x, w['ffn1_norm'][i], f['ffn1_w_gate'], f['ffn1_w_up'], f['ffn1_w_down'],
                              wanted(i, MIX_IN_W) if first else None)
        if first:
            arrived(i, got, MIX_IN_W)
        lw = _mixer_weights(f, w, s5, i)
        lws.append(lw)

        def after_in(got_in, i=i, lw=lw):
            arrived(i, got_in, MIX_REST_W)
            lw.update(_mixer_weights_rest(full[i]))
        if not first:
            lw.update(_mixer_weights_rest(f))
        x, s2, (got_s5, got) = _mixer_fwd(x, lw, wanted(i, MIX_REST_W) if first else None, after_in,
                                          wanted(i, FFN2_W) if first else None,
                                          wanted(i + 1, FFN1_W, MIX_IN_W, MIX_REST_W) if first else wanted(i, FFN2_W))
        if first:
            arrived(i, got_s5, FFN2_W)
            arrived(i + 1, got, FFN1_W, MIX_IN_W, MIX_REST_W)
        else:
            arrived(i, got, FFN2_W)
        x, s3, _ = _ffn_fwd(x, w['ffn2_norm'][i], f['ffn2_w_gate'], f['ffn2_w_up'], f['ffn2_w_down'])
        saved.append((s1, s2, s3))
    loss, dx, d_final = _loss_head(x, w['final_norm'], target)
    per_layer, incoming = [None] * DEPTH, [{} for _ in range(DEPTH)]
    later = [n for n in SHARDED if n not in FFN2_W]
    pending = []
    for i in reversed(range(DEPTH)):
        f, lw, (s1, s2, s3), g = full[i], lws[i], saved[i], {}
        dx, g['ffn2_norm'], g['ffn2_w_gate'], g['ffn2_w_up'], g['ffn2_w_down'] = _ffn_bwd(
            s3, w['ffn2_norm'][i], f['ffn2_w_gate'], f['ffn2_w_up'], f['ffn2_w_down'], dx)
        pending.append((i, FFN2_W, [_split_shards(g[n], SHARD_AXIS[n] - 1).astype(BF16) for n in FFN2_W]))
        dx, gm, carried = _mixer_bwd(s2, lw, dx, _SideChipExchange([s for _, _, sums in pending for s in sums]))
        for layer, names, _ in pending:
            incoming[layer].update(zip(names, carried[:len(names)]))
            carried = carried[len(names):]
        dx, g['ffn1_norm'], g['ffn1_w_gate'], g['ffn1_w_up'], g['ffn1_w_down'] = _ffn_bwd(
            s1, w['ffn1_norm'][i], f['ffn1_w_gate'], f['ffn1_w_up'], f['ffn1_w_down'], dx)
        g['w_merge_gate'] = gm['w_pg'][:, :GATE_WIDTH]
        g['w_in'] = _w_in_unpadded(gm['w_pg'][:, GATE_WIDTH:])
        for n in ('w_out', 'b_merge_gate', 'w_branch_s5', 'w_branch_gla', 'w_branch_attn', 'mix_norm'):
            g[n] = gm[n]
        g['s5_d'], g['s5_w_glu'], g['s5_raw'] = gm['s5']['d'], gm['s5']['w_glu'], gm['s5']['raw']
        g['gla_w_alpha'], g['gla_b_alpha'], g['gla_norm'] = gm['gla']['w_alpha'], gm['gla']['b_alpha'], gm['gla']['norm']
        g['attn_q_norm'], g['attn_k_norm'] = gm['attn']['q_norm'], gm['attn']['k_norm']
        per_layer[i] = g
        pending = [(i, later, _chip_sums(g, later))]
    incoming[0].update(zip(later, _exchange_chips(pending[0][2], name="exchange_grads_chips")))
    stacked = _s5_param_grads(w, [g['s5_raw'] for g in per_layer])
    stacked['final_norm'] = d_final
    return loss, dx, per_layer, stacked, incoming


def kernel(x, ffn1_norm, ffn1_w_gate, ffn1_w_up, ffn1_w_down, mix_norm, w_in, s5_lambda_re, s5_lambda_im, s5_log_dt, s5_b_re, s5_b_im, s5_c_re, s5_c_im, s5_d, s5_w_glu, gla_w_alpha, gla_b_alpha, gla_norm, attn_q_norm, attn_k_norm, w_branch_s5, w_branch_gla, w_branch_attn, w_merge_gate, b_merge_gate, w_out, ffn2_norm, ffn2_w_gate, ffn2_w_up, ffn2_w_down, final_norm, loss_target, m_ffn1_norm, m_ffn1_w_gate, m_ffn1_w_up, m_ffn1_w_down, m_mix_norm, m_w_in, m_s5_lambda_re, m_s5_lambda_im, m_s5_log_dt, m_s5_b_re, m_s5_b_im, m_s5_c_re, m_s5_c_im, m_s5_d, m_s5_w_glu, m_gla_w_alpha, m_gla_b_alpha, m_gla_norm, m_attn_q_norm, m_attn_k_norm, m_w_branch_s5, m_w_branch_gla, m_w_branch_attn, m_w_merge_gate, m_b_merge_gate, m_w_out, m_ffn2_norm, m_ffn2_w_gate, m_ffn2_w_up, m_ffn2_w_down, m_final_norm, v_ffn1_norm, v_ffn1_w_gate, v_ffn1_w_up, v_ffn1_w_down, v_mix_norm, v_w_in, v_s5_lambda_re, v_s5_lambda_im, v_s5_log_dt, v_s5_b_re, v_s5_b_im, v_s5_c_re, v_s5_c_im, v_s5_d, v_s5_w_glu, v_gla_w_alpha, v_gla_b_alpha, v_gla_norm, v_attn_q_norm, v_attn_k_norm, v_w_branch_s5, v_w_branch_gla, v_w_branch_attn, v_w_merge_gate, v_b_merge_gate, v_w_out, v_ffn2_norm, v_ffn2_w_gate, v_ffn2_w_up, v_ffn2_w_down, v_final_norm):
    return _train_step(x, ffn1_norm, ffn1_w_gate, ffn1_w_up, ffn1_w_down, mix_norm, w_in, s5_lambda_re, s5_lambda_im, s5_log_dt, s5_b_re, s5_b_im, s5_c_re, s5_c_im, s5_d, s5_w_glu, gla_w_alpha, gla_b_alpha, gla_norm, attn_q_norm, attn_k_norm, w_branch_s5, w_branch_gla, w_branch_attn, w_merge_gate, b_merge_gate, w_out, ffn2_norm, ffn2_w_gate, ffn2_w_up, ffn2_w_down, final_norm, loss_target, m_ffn1_norm, m_ffn1_w_gate, m_ffn1_w_up, m_ffn1_w_down, m_mix_norm, m_w_in, m_s5_lambda_re, m_s5_lambda_im, m_s5_log_dt, m_s5_b_re, m_s5_b_im, m_s5_c_re, m_s5_c_im, m_s5_d, m_s5_w_glu, m_gla_w_alpha, m_gla_b_alpha, m_gla_norm, m_attn_q_norm, m_attn_k_norm, m_w_branch_s5, m_w_branch_gla, m_w_branch_attn, m_w_merge_gate, m_b_merge_gate, m_w_out, m_ffn2_norm, m_ffn2_w_gate, m_ffn2_w_up, m_ffn2_w_down, m_final_norm, v_ffn1_norm, v_ffn1_w_gate, v_ffn1_w_up, v_ffn1_w_down, v_mix_norm, v_w_in, v_s5_lambda_re, v_s5_lambda_im, v_s5_log_dt, v_s5_b_re, v_s5_b_im, v_s5_c_re, v_s5_c_im, v_s5_d, v_s5_w_glu, v_gla_w_alpha, v_gla_b_alpha, v_gla_norm, v_attn_q_norm, v_attn_k_norm, v_w_branch_s5, v_w_branch_gla, v_w_branch_attn, v_w_merge_gate, v_b_merge_gate, v_w_out, v_ffn2_norm, v_ffn2_w_gate, v_ffn2_w_up, v_ffn2_w_down, v_final_norm)


def _train_step(*args):
    nw = len(W_NAMES)
    x, target = args[0][0], args[1 + nw][0]
    w = dict(zip(W_NAMES, args[1:1 + nw]))
    m = dict(zip(W_NAMES, args[2 + nw:2 + 2 * nw]))
    v = dict(zip(W_NAMES, args[2 + 2 * nw:2 + 3 * nw]))

    loss, dx, per_layer, stacked, incoming = _step_local(x, target, w, {n: w[n].astype(BF16) for n in SHARDED})
    loss = lax.psum(loss, ("x", "y", "c"))

    out = {}
    kinds = ('grad', 'delta', 'new_m', 'new_v')
    for n in SHARDED:
        shape = w[n].shape
        flat = lambda a: a.reshape(-1, shape[-1])
        parts = [incoming[i][n].reshape(incoming[i][n].shape[0], -1, shape[-1]) for i in range(DEPTH)]
        res = _reduce_adamw(parts, flat(w[n]), flat(m[n]), flat(v[n]), name="adamw_sharded")
        for kind, a in zip(kinds, res):
            out[kind + '_' + n] = a.reshape(shape)
    small = [stacked[n] if n in stacked else jnp.stack([g[n] for g in per_layer]) for n in REPLICATED]
    parts = _all_gather([_pack_small(small)], name="gather_small_grads")[0]
    res = _reduce_adamw([parts], *[_pack_small([d[n] for n in REPLICATED]) for d in (w, m, v)], name="adamw_replicated")
    for kind, packed in zip(kinds, res):
        for n, a in zip(REPLICATED, _unpack_small(packed, [w[n].shape for n in REPLICATED])):
            out[kind + '_' + n] = a
    return (loss, dx[None]) + tuple(out[kind + '_' + n] for kind in kinds for n in W_NAMES)
```

```python
import functools
import math

import jax
import jax.numpy as jnp
import numpy as np
from jax import lax
from jax.experimental import pallas as pl
from jax.experimental.pallas import tpu as pltpu

F32 = jnp.float32
BF16 = jnp.bfloat16

N_DEV = 8
D_MODEL = 1024
DEPTH = 2
GRID_W = 64
D_FF = 2816
NORM_EPS = 1e-6
S5_GROUPS = 32
S5_GROUP_CH = 16
S5_STATE = 64
S5_WIDTH = 512
S5_NSTATE = S5_GROUPS * S5_STATE
S5_LANE_BLOCK = 512
GLA_HEADS = 4
GLA_HEAD_DIM = 128
GLA_WIDTH = 512
GLA_LOWRANK = 16
GLA_TAU = 16.0
GLA_CHUNK = 64
ATTN_Q_HEADS = 8
ATTN_KV_HEADS = 2
ATTN_HEAD_DIM = 64
ATTN_WIDTH = 512
ATTN_KV_WIDTH = 128
ROPE_BASE = 10000.0
IN_SPLITS = (512, 512, 512, 512, 512, 16, 16, 512, 128, 128)
IN_WIDTH = sum(IN_SPLITS)
IN_PAD = 3584
GATE_WIDTH = 3 * D_MODEL
PG_WIDTH = GATE_WIDTH + IN_PAD
P_OFF = GATE_WIDTH
CB_U, CB_GQ, CB_GK, CB_GV, CB_GG, CB_AQ = (P_OFF // 512 + i for i in range(6))
CB_AK, CB_AV, CB_Z = (P_OFF + 3072) // 128, (P_OFF + 3200) // 128, (P_OFF + 3328) // 128
ADAM_LR = 0.001
ADAM_B1 = 0.9
ADAM_B2 = 0.999
ADAM_EPS = 1e-08
ADAM_WD = 0.01
ADAM_STEP = 10

W_NAMES = ['ffn1_norm', 'ffn1_w_gate', 'ffn1_w_up', 'ffn1_w_down', 'mix_norm', 'w_in', 's5_lambda_re', 's5_lambda_im',
           's5_log_dt', 's5_b_re', 's5_b_im', 's5_c_re', 's5_c_im', 's5_d', 's5_w_glu', 'gla_w_alpha', 'gla_b_alpha',
           'gla_norm', 'attn_q_norm', 'attn_k_norm', 'w_branch_s5', 'w_branch_gla', 'w_branch_attn', 'w_merge_gate',
           'b_merge_gate', 'w_out', 'ffn2_norm', 'ffn2_w_gate', 'ffn2_w_up', 'ffn2_w_down', 'final_norm']
SHARD_AXIS = {'ffn1_w_gate': 2, 'ffn1_w_up': 2, 'ffn1_w_down': 1, 'w_in': 2, 's5_w_glu': 1, 'gla_w_alpha': 3,
              'gla_b_alpha': 2, 'w_branch_s5': 2, 'w_branch_gla': 2, 'w_branch_attn': 2, 'w_merge_gate': 2,
              'w_out': 1, 'ffn2_w_gate': 2, 'ffn2_w_up': 2, 'ffn2_w_down': 1}
SHARDED = [n for n in W_NAMES if n in SHARD_AXIS]
REPLICATED = [n for n in W_NAMES if n not in SHARD_AXIS]


def _pick(dim, prefs):
    for p in prefs:
        if dim % p == 0:
            return p
    return dim


def _sigmoid(x):
    return 0.5 * jnp.tanh(0.5 * x) + 0.5


def _mm(a, b, *, ta=False, tb=False, out_dtype=F32, scale=None, add=None, side=None, name):
    a, a_cb, a_w = a if isinstance(a, tuple) else (a, 0, a.shape[1])
    b, b_cb, b_w = b if isinstance(b, tuple) else (b, 0, b.shape[1])
    m, k = (a_w, a.shape[0]) if ta else (a.shape[0], a_w)
    n = b.shape[0] if tb else b_w
    assert (b_w if tb else b.shape[0]) == k, (a.shape, b.shape, ta, tb)
    tm, tn, tk = _mm_tiles(m, n, k, a.dtype.itemsize, b.dtype.itemsize, jnp.dtype(out_dtype).itemsize)
    nk = k // tk
    dims = (((0 if ta else 1,), (1 if tb else 0,)), ((), ()))
    a_off = a_cb * (a_w // (tm if ta else tk))
    b_off = b_cb * (b_w // (tk if tb else tn))

    grid = (m // tm, n // tn, nk)
    n_in = 2 if add is None else 3

    def body(*refs):
        _carried(side, grid, refs, n_in, 1, int(nk > 1), compute)

    def compute(refs):
        a_ref, b_ref, *rest = refs
        add_ref = rest[0] if add is not None else None
        o_ref, *acc = rest[1:] if add is not None else rest

        def finish(res):
            res = res if scale is None else res * scale
            return (res if add_ref is None else res + add_ref[...]).astype(out_dtype)

        part = lax.dot_general(a_ref[...].astype(BF16), b_ref[...].astype(BF16), dims, preferred_element_type=F32)
        if nk == 1:
            o_ref[...] = finish(part)
            return
        acc_ref, = acc
        kk = pl.program_id(2)

        @pl.when(kk == 0)
        def _():
            acc_ref[...] = part

        @pl.when(kk > 0)
        def _():
            acc_ref[...] += part

        @pl.when(kk == nk - 1)
        def _():
            o_ref[...] = finish(acc_ref[...])

    a_spec = (pl.BlockSpec((tk, tm), lambda i, j, kk: (kk, i + a_off)) if ta
              else pl.BlockSpec((tm, tk), lambda i, j, kk: (i, kk + a_off)))
    b_spec = (pl.BlockSpec((tn, tk), lambda i, j, kk: (j, kk + b_off)) if tb
              else pl.BlockSpec((tk, tn), lambda i, j, kk: (kk, j + b_off)))
    o_spec = pl.BlockSpec((tm, tn), lambda i, j, kk: (i, j))
    (out,), gathered = _side_call(
        body, side, name=name, grid=grid, in_specs=[a_spec, b_spec] + ([o_spec] if add is not None else []),
        out_specs=[o_spec], out_shape=[jax.ShapeDtypeStruct((m, n), out_dtype)],
        scratch=[pltpu.VMEM((tm, tn), F32)] if nk > 1 else [], args=[a, b] + ([add] if add is not None else []),
        semantics=("parallel", "parallel", "arbitrary"))
    return out if side is None else (out, gathered)


MM_VMEM_BUDGET = 40 * 1024 * 1024


def _mm_tiles(m, n, k, a_bytes, b_bytes, out_bytes):
    tms = [t for t in (1024, 1408, 512, 256, 128) if m % t == 0] or [m]
    tns = [t for t in (1664, 1408, 512, 256, 128) if n % t == 0] or [n]
    tks = [k] + [t for t in (2048, 1024, 512, 256, 128) if k % t == 0 and t < k]
    for tk in tks:
        for tm in tms:
            for tn in tns:
                use = 2 * (tm * tk * a_bytes + tk * tn * b_bytes + tm * tn * out_bytes) + 2 * tm * tn * 4
                if use <= MM_VMEM_BUDGET:
                    return tm, tn, tk
    return tms[-1], tns[-1], tks[-1]


def _rowmap(fn, rows, consts, outs, reds=(), *, tl, name):
    rows = [r if isinstance(r, tuple) else (r, 0, r.shape[1]) for r in rows]
    length = rows[0][0].shape[0]
    tl = min(tl, length)
    nr, nc, no = len(rows), len(consts), len(outs)

    def body(*refs):
        res = fn(*[r[...] for r in refs[:nr + nc]])
        res = res if isinstance(res, tuple) else (res,)
        for o_ref, val in zip(refs[nr + nc:nr + nc + no], res[:no]):
            o_ref[...] = val.astype(o_ref.dtype)
        if reds:
            step = pl.program_id(0)
            red_refs = refs[nr + nc + no:]

            @pl.when(step == 0)
            def _():
                for d_ref, val in zip(red_refs, res[no:]):
                    d_ref[...] = val.astype(F32)

            @pl.when(step > 0)
            def _():
                for d_ref, val in zip(red_refs, res[no:]):
                    d_ref[...] += val.astype(F32)

    in_specs = [pl.BlockSpec((tl, w), lambda i, cb=cb: (i, cb)) for (_, cb, w) in rows]
    in_specs += [pl.BlockSpec(c.shape, lambda i, nd=c.ndim: (0,) * nd) for c in consts]
    out_specs = [pl.BlockSpec((tl, w), lambda i: (i, 0)) for (w, _) in outs]
    out_specs += [pl.BlockSpec(s, lambda i, nd=len(s): (0,) * nd) for s in reds]
    out_shape = [jax.ShapeDtypeStruct((length, w), dt) for (w, dt) in outs]
    out_shape += [jax.ShapeDtypeStruct(s, F32) for s in reds]
    res = pl.pallas_call(
        body, name=name, grid=(length // tl,), in_specs=in_specs, out_specs=out_specs, out_shape=out_shape,
        compiler_params=pltpu.CompilerParams(dimension_semantics=("arbitrary" if reds else "parallel",)),
    )(*[r[0] for r in rows], *consts)
    return res


def _rms(x):
    return lax.rsqrt(jnp.mean(x * x, axis=-1, keepdims=True) + NORM_EPS)


def _rmsnorm_fwd(x, gain):
    def fn(xv, g):
        return xv * _rms(xv) * g
    return _rowmap(fn, [x], [gain.reshape(1, -1)], [(x.shape[1], BF16)], tl=256, name="rmsnorm_fwd")[0]


def _rmsnorm_bwd(x, gain, dh, dres):
    def fn(xv, dhv, drv, g):
        r = _rms(xv)
        gd = dhv * g
        dx = r * gd - xv * (r * r * r) * jnp.mean(xv * gd, axis=-1, keepdims=True)
        return drv + dx, jnp.sum(dhv * xv * r, axis=0, keepdims=True)
    dx, dg = _rowmap(fn, [x, dh, dres], [gain.reshape(1, -1)], [(x.shape[1], F32)], [(1, x.shape[1])], tl=256,
                     name="rmsnorm_bwd")
    return dx, dg[0]


FFN_UNIT = D_FF // 2


def _side_call(body, side, *, name, grid, in_specs, out_specs, out_shape, scratch, args, semantics):
    if side is not None:
        in_specs, out_specs = in_specs + side.in_specs, out_specs + side.out_specs
        out_shape, scratch, args = out_shape + side.out_shape, scratch + side.scratch, list(args) + side.blocks
        semantics = ("arbitrary",) * len(grid)
    res = pl.pallas_call(body, name=name, grid=grid, in_specs=in_specs, out_specs=out_specs, out_shape=out_shape,
                         scratch_shapes=scratch, compiler_params=pltpu.CompilerParams(dimension_semantics=semantics))(*args)
    n_own = len(res) - (side.n if side is not None else 0)
    return res[:n_own], res[n_own:]


def _ffn_up(h, w_gate, w_up, side=None):
    length, k = h.shape
    tm = _pick(length, (512, 256, 128))
    grid = (D_FF // FFN_UNIT, length // tm)

    def compute(refs):
        h_ref, wg_ref, wu_ref, a_ref, g_ref, u_ref = refs
        hv = h_ref[...]
        g = jnp.dot(hv, wg_ref[...], preferred_element_type=F32)
        u = jnp.dot(hv, wu_ref[...], preferred_element_type=F32)
        a_ref[...] = (g * _sigmoid(g) * u).astype(BF16)
        g_ref[...] = g.astype(BF16)
        u_ref[...] = u.astype(BF16)

    def body(*refs):
        _carried(side, grid, refs, 3, 3, 0, compute)

    w_spec = pl.BlockSpec((k, FFN_UNIT), lambda j, i: (0, j))
    o_spec = pl.BlockSpec((tm, FFN_UNIT), lambda j, i: (i, j))
    return _side_call(
        body, side, name="ffn_up", grid=grid, in_specs=[pl.BlockSpec((tm, k), lambda j, i: (i, 0)), w_spec, w_spec],
        out_specs=[o_spec] * 3, out_shape=[jax.ShapeDtypeStruct((length, D_FF), BF16)] * 3, scratch=[],
        args=[h, w_gate, w_up], semantics=("parallel", "parallel"))


def _ffn_dgu(dxo, w_down, g, u):
    length, k = dxo.shape
    tm = _pick(length, (512, 256, 128))

    def body(d_ref, w_ref, g_ref, u_ref, dg_ref, du_ref):
        da = 0.5 * lax.dot_general(d_ref[...], w_ref[...], _NT, preferred_element_type=F32)
        gv = g_ref[...].astype(F32)
        s = _sigmoid(gv)
        dg_ref[...] = (da * u_ref[...].astype(F32) * (s * (1.0 + gv * (1.0 - s)))).astype(BF16)
        du_ref[...] = (da * (gv * s)).astype(BF16)

    o_spec = pl.BlockSpec((tm, FFN_UNIT), lambda j, i: (i, j))
    return pl.pallas_call(
        body, name="ffn_dgu", grid=(D_FF // FFN_UNIT, length // tm),
        in_specs=[pl.BlockSpec((tm, k), lambda j, i: (i, 0)), pl.BlockSpec((FFN_UNIT, k), lambda j, i: (j, 0)), o_spec, o_spec],
        out_specs=[o_spec] * 2, out_shape=[jax.ShapeDtypeStruct((length, D_FF), BF16)] * 2,
        compiler_params=pltpu.CompilerParams(dimension_semantics=("parallel", "parallel")),
    )(dxo, w_down, g, u)


def _ffn_fwd(x, gain, w_gate, w_up, w_down, side=None):
    h = _rmsnorm_fwd(x, gain)
    (a, g, u), gathered = _ffn_up(h, w_gate, w_up, side)
    x_out = _mm(a, w_down, scale=0.5, add=x, name="ffn_down")
    return x_out, (x, h, g, u, a), gathered


def _ffn_bwd(saved, gain, w_gate, w_up, w_down, dx_out):
    x, h, g, u, a = saved
    dxo = dx_out.astype(BF16)
    d_wdown = _mm(a, dxo, ta=True, scale=0.5, out_dtype=BF16, name="ffn_dwdown")
    dg, du = _ffn_dgu(dxo, w_down, g, u)
    d_wgate = _mm(h, dg, ta=True, out_dtype=BF16, name="ffn_dwgu")
    d_wup = _mm(h, du, ta=True, out_dtype=BF16, name="ffn_dwgu")
    dh = _mm(du, w_up, tb=True, add=_mm(dg, w_gate, tb=True, name="ffn_dh"), name="ffn_dh_add")
    dx, dgain = _rmsnorm_bwd(x, gain, dh, dx_out)
    return dx, dgain, d_wgate, d_wup, d_wdown


def _s5_blocked(re, im):
    lead = re.shape[:-1]
    nb = S5_NSTATE // S5_LANE_BLOCK
    both = jnp.stack([re.reshape(*lead, nb, S5_LANE_BLOCK), im.reshape(*lead, nb, S5_LANE_BLOCK)], axis=-2)
    return both.reshape(*lead, 2 * S5_NSTATE)


def _s5_unblocked(z):
    lead = z.shape[:-1]
    nb = S5_NSTATE // S5_LANE_BLOCK
    both = z.reshape(*lead, nb, 2, S5_LANE_BLOCK)
    return both[..., 0, :].reshape(*lead, S5_NSTATE), both[..., 1, :].reshape(*lead, S5_NSTATE)


def _s5_tables(a_re, a_im, reverse):
    a = lax.complex(a_re, a_im)
    a2 = a * a
    a4 = a2 * a2
    rows = jnp.arange(8)
    pw = [a]
    for _ in range(7):
        pw.append(pw[-1] * a)
    pw = jnp.stack(pw)
    if reverse:
        pw = pw[::-1]
    tabs = []
    for coef, s in ((a, 1), (a2, 2), (a4, 4)):
        live = (rows <= 7 - s) if reverse else (rows >= s)
        tabs.append(jnp.where(live[:, None], coef[None, :], 0.0))
    tabs.append(pw)
    tabs = jnp.stack(tabs)
    return _s5_blocked(jnp.real(tabs), jnp.imag(tabs))


def _s5_scan_tile(v, tab_ref, prev, reverse):
    lb = S5_LANE_BLOCK
    vr, vi = v[:, :lb], v[:, lb:]
    for idx, s in enumerate((1, 2, 4)):
        cr, ci = tab_ref[idx, :, :lb], tab_ref[idx, :, lb:]
        sh = 8 - s if reverse else s
        sr, si = pltpu.roll(vr, sh, 0), pltpu.roll(vi, sh, 0)
        vr, vi = vr + cr * sr - ci * si, vi + cr * si + ci * sr
    row = 0 if reverse else 7
    pr = jnp.broadcast_to(prev[row:row + 1, :lb], (8, lb))
    pi = jnp.broadcast_to(prev[row:row + 1, lb:], (8, lb))
    cr, ci = tab_ref[3, :, :lb], tab_ref[3, :, lb:]
    return jnp.concatenate([vr + cr * pr - ci * pi, vi + cr * pi + ci * pr], axis=1)


def _s5_prep(lam_re, lam_im, log_dt, b_re, b_im):
    lam = lax.complex(lam_re, lam_im)
    dt = jnp.exp(log_dt)[:, None]
    lam_bar = jnp.exp(lam * dt)
    b_bar = ((lam_bar - 1.0) / lam)[..., None] * lax.complex(b_re, b_im)
    return (jnp.real(lam_bar).reshape(-1), jnp.imag(lam_bar).reshape(-1), jnp.real(b_bar), jnp.imag(b_bar))


S5_NBLK = S5_NSTATE // S5_LANE_BLOCK
S5_BLK_GROUPS = S5_GROUPS // S5_NBLK
S5_BLK_CH = S5_BLK_GROUPS * S5_GROUP_CH


def _s5_in_matrix(bb_re, bb_im):
    eye = jnp.eye(S5_BLK_GROUPS, dtype=F32)
    def dense(bb):
        b4 = bb.reshape(S5_NBLK, S5_BLK_GROUPS, S5_STATE, S5_GROUP_CH)
        return jnp.einsum('cgph,gk->cghkp', b4, eye).reshape(S5_NBLK, S5_BLK_CH, S5_LANE_BLOCK)
    return jnp.concatenate([dense(bb_re), dense(bb_im)], axis=-1)


def _s5_block_diagonal(d):
    d5 = d.reshape(S5_NBLK, S5_BLK_GROUPS, S5_GROUP_CH, S5_BLK_GROUPS, S5_STATE)
    eye = jnp.eye(S5_BLK_GROUPS, dtype=F32)
    return jnp.swapaxes(jnp.sum(d5 * eye[None, :, None, :, None], axis=1), 1, 2)


def _s5_in_matrix_grad(d_mat):
    def diag(d):
        return jnp.swapaxes(_s5_block_diagonal(d), 2, 3).reshape(S5_GROUPS, S5_STATE, S5_GROUP_CH)
    return diag(d_mat[..., :S5_LANE_BLOCK]), diag(d_mat[..., S5_LANE_BLOCK:])


def _s5_out_matrix(c_re, c_im):
    eye = jnp.eye(S5_BLK_GROUPS, dtype=F32)
    def dense(cc):
        c4 = cc.reshape(S5_NBLK, S5_BLK_GROUPS, S5_GROUP_CH, S5_STATE)
        return jnp.einsum('cghp,gk->cgpkh', c4, eye).reshape(S5_NBLK, S5_LANE_BLOCK, S5_BLK_CH)
    return jnp.concatenate([dense(c_re), dense(-c_im)], axis=1)


def _s5_out_matrix_grad(d_mat_t):
    def diag(d):
        return _s5_block_diagonal(d).reshape(S5_GROUPS, S5_GROUP_CH, S5_STATE)
    return diag(d_mat_t[..., :S5_LANE_BLOCK]), -diag(d_mat_t[..., S5_LANE_BLOCK:])


def _gelu_parts(x):
    k = math.sqrt(2.0 / math.pi)
    inner = k * (x + 0.044715 * x * x * x)
    th = jnp.tanh(inner)
    return th, k * (1.0 + 3.0 * 0.044715 * x * x)


S5_CB_U = CB_U * (512 // S5_BLK_CH)


def _s5_direction_fwd(pg, b_mat, c_mat, tabs, reverse, *, name, side=None):
    length = pg.shape[0]
    tb = min(512, length)
    ntb = length // tb
    wb = 2 * S5_LANE_BLOCK
    ntile = tb // 8
    grid = (S5_NBLK, ntb)

    def body(*refs):
        _carried(side, grid, refs, 4, 3, 2, compute)

    def compute(refs):
        tab_ref, u_ref, b_ref, c_ref, x_ref, y_ref, ends_ref, carry_ref, bu_ref = refs

        @pl.when(pl.program_id(1) == 0)
        def _():
            carry_ref[...] = jnp.zeros_like(carry_ref)

        bu_ref[...] = jnp.dot(u_ref[...].astype(BF16), b_ref[0], preferred_element_type=F32)

        def step(i, prev):
            r0 = pl.multiple_of((ntile - 1 - i if reverse else i) * 8, 8)
            x = _s5_scan_tile(bu_ref[pl.ds(r0, 8), :], tab_ref, prev, reverse)
            x_ref[pl.ds(r0, 8), :] = x
            return x

        carry_ref[...] = lax.fori_loop(0, ntile, step, carry_ref[...])
        y_ref[...] = jnp.dot(x_ref[...].astype(BF16), c_ref[0], preferred_element_type=F32)
        ends_ref[0, 0:8, :] = x_ref[0:8, :]
        ends_ref[0, 8:16, :] = x_ref[tb - 8:tb, :]

    tix = (lambda t: ntb - 1 - t) if reverse else (lambda t: t)
    (xs, y, ends), gathered = _side_call(
        body, side, name=name, grid=grid,
        in_specs=[pl.BlockSpec((4, 8, wb), lambda c, t: (0, 0, c)),
                  pl.BlockSpec((tb, S5_BLK_CH), lambda c, t: (tix(t), S5_CB_U + c)),
                  pl.BlockSpec((1, S5_BLK_CH, wb), lambda c, t: (c, 0, 0)),
                  pl.BlockSpec((1, wb, S5_BLK_CH), lambda c, t: (c, 0, 0))],
        out_specs=[pl.BlockSpec((tb, wb), lambda c, t: (tix(t), c)), pl.BlockSpec((tb, S5_BLK_CH), lambda c, t: (tix(t), c)),
                   pl.BlockSpec((1, 16, wb), lambda c, t: (tix(t), 0, c))],
        out_shape=[jax.ShapeDtypeStruct((length, S5_NBLK * wb), F32), jax.ShapeDtypeStruct((length, S5_WIDTH), F32),
                   jax.ShapeDtypeStruct((ntb, 16, S5_NBLK * wb), F32)],
        scratch=[pltpu.VMEM((8, wb), F32), pltpu.VMEM((tb, wb), F32)], args=[tabs, pg, b_mat, c_mat],
        semantics=("parallel", "arbitrary"))
    return xs, y, ends, gathered


def _s5_direction_bwd(pg, dy, xs, ends, b_mat, c_mat, tabs_conj, reverse, *, name):
    length = pg.shape[0]
    tb = min(512, length)
    ntb = length // tb
    lb = S5_LANE_BLOCK
    wb = 2 * lb
    ntile = tb // 8
    adj_rev = not reverse
    if reverse:
        edge = jnp.concatenate([ends[1:, 0], jnp.zeros((1, xs.shape[1]), F32)], axis=0)
    else:
        edge = jnp.concatenate([jnp.zeros((1, xs.shape[1]), F32), ends[:-1, 15]], axis=0)
    edge = edge.reshape(ntb, 1, xs.shape[1])

    def body(tab_ref, u_ref, dy_ref, x_ref, edge_ref, b_ref, c_ref, du_ref, db_ref, dc_ref, da_ref, carry_ref, g_ref, lam_ref):
        @pl.when(pl.program_id(1) == 0)
        def _():
            carry_ref[...] = jnp.zeros_like(carry_ref)
            da_ref[...] = jnp.zeros_like(da_ref)
            db_ref[...] = jnp.zeros_like(db_ref)
            dc_ref[...] = jnp.zeros_like(dc_ref)

        dyb = dy_ref[...].astype(BF16)
        g_ref[...] = lax.dot_general(dyb, c_ref[0], _NT, preferred_element_type=F32)
        rows = lax.broadcasted_iota(jnp.int32, (8, wb), 0)

        def step(i, carry):
            prev, acc = carry
            k = ntile - 1 - i if adj_rev else i
            r0 = pl.multiple_of(k * 8, 8)
            lam = _s5_scan_tile(g_ref[pl.ds(r0, 8), :], tab_ref, prev, adj_rev)
            lam_ref[pl.ds(r0, 8), :] = lam
            x = x_ref[pl.ds(r0, 8), :]
            if reverse:
                kn = jnp.minimum(k + 1, ntile - 1)
                nb = x_ref[pl.ds(pl.multiple_of(kn * 8, 8), 8), :][0:1, :]
                nb = jnp.where(k == ntile - 1, edge_ref[0], nb)
                xp = jnp.where(rows == 7, jnp.broadcast_to(nb, (8, wb)), pltpu.roll(x, 7, 0))
            else:
                kn = jnp.maximum(k - 1, 0)
                nb = x_ref[pl.ds(pl.multiple_of(kn * 8, 8), 8), :][7:8, :]
                nb = jnp.where(k == 0, edge_ref[0], nb)
                xp = jnp.where(rows == 0, jnp.broadcast_to(nb, (8, wb)), pltpu.roll(x, 1, 0))
            xr, xi, lr, li = xp[:, :lb], xp[:, lb:], lam[:, :lb], lam[:, lb:]
            return lam, acc + jnp.concatenate([xr * lr + xi * li, xr * li - xi * lr], axis=1)

        last, acc = lax.fori_loop(0, ntile, step, (carry_ref[...], da_ref[...]))
        carry_ref[...] = last
        da_ref[...] = acc
        lamb = lam_ref[...].astype(BF16)
        du_ref[...] = lax.dot_general(lamb, b_ref[0], _NT, preferred_element_type=F32)
        db_ref[0] += lax.dot_general(u_ref[...].astype(BF16), lamb, _TN, preferred_element_type=F32)
        dc_ref[0] += lax.dot_general(dyb, x_ref[...].astype(BF16), _TN, preferred_element_type=F32)

    tix = (lambda t: ntb - 1 - t) if adj_rev else (lambda t: t)
    wide = pl.BlockSpec((tb, wb), lambda c, t: (tix(t), c))
    mat = pl.BlockSpec((1, S5_BLK_CH, wb), lambda c, t: (c, 0, 0))
    return pl.pallas_call(
        body, name=name, grid=(S5_NBLK, ntb),
        in_specs=[pl.BlockSpec((4, 8, wb), lambda c, t: (0, 0, c)),
                  pl.BlockSpec((tb, S5_BLK_CH), lambda c, t: (tix(t), S5_CB_U + c)),
                  pl.BlockSpec((tb, S5_BLK_CH), lambda c, t: (tix(t), c)), wide,
                  pl.BlockSpec((1, 1, wb), lambda c, t: (tix(t), 0, c)), mat,
                  pl.BlockSpec((1, wb, S5_BLK_CH), lambda c, t: (c, 0, 0))],
        out_specs=[pl.BlockSpec((tb, S5_BLK_CH), lambda c, t: (tix(t), c)), mat, mat, pl.BlockSpec((8, wb), lambda c, t: (0, c))],
        out_shape=[jax.ShapeDtypeStruct((length, S5_WIDTH), F32), jax.ShapeDtypeStruct((S5_NBLK, S5_BLK_CH, wb), F32),
                   jax.ShapeDtypeStruct((S5_NBLK, S5_BLK_CH, wb), F32), jax.ShapeDtypeStruct((8, S5_NBLK * wb), F32)],
        scratch_shapes=[pltpu.VMEM((8, wb), F32), pltpu.VMEM((tb, wb), F32), pltpu.VMEM((tb, wb), F32)],
        compiler_params=pltpu.CompilerParams(dimension_semantics=("parallel", "arbitrary")),
    )(tabs_conj, pg, dy, xs, edge, b_mat, c_mat)


def _both(fn):
    return jax.vmap(jax.vmap(fn))


def _s5_setup(w):
    a_re, a_im, bb_re, bb_im = _both(_s5_prep)(w['s5_lambda_re'], w['s5_lambda_im'], w['s5_log_dt'], w['s5_b_re'], w['s5_b_im'])

    def tables(d, conj, reverse):
        return jax.vmap(lambda r, i: _s5_tables(r, -i if conj else i, reverse))(a_re[:, d], a_im[:, d])
    return {'b_mat': _both(_s5_in_matrix)(bb_re, bb_im).astype(BF16),
            'c_mat': _both(_s5_out_matrix)(w['s5_c_re'], w['s5_c_im']).astype(BF16),
            'tabs': [tables(0, False, False), tables(1, False, True)],
            'tabs_adj': [tables(0, True, True), tables(1, True, False)]}


def _s5_param_grads(w, raws):
    def stacked(k):
        return jnp.stack([jnp.stack([raws[i][d][k] for d in range(2)]) for i in range(DEPTH)])
    dbb_re, dbb_im = _both(_s5_in_matrix_grad)(stacked(0))
    dc_re, dc_im = _both(_s5_out_matrix_grad)(stacked(1))
    da_re, da_im = _s5_unblocked(jnp.sum(stacked(2), axis=2))
    _, vjp = jax.vjp(_both(_s5_prep), w['s5_lambda_re'], w['s5_lambda_im'], w['s5_log_dt'], w['s5_b_re'], w['s5_b_im'])
    g = vjp((da_re, da_im, dbb_re, dbb_im))
    return {'s5_lambda_re': g[0], 's5_lambda_im': g[1], 's5_log_dt': g[2], 's5_b_re': g[3], 's5_b_im': g[4],
            's5_c_re': dc_re, 's5_c_im': dc_im}


def _s5_fwd(p_in, prm, w_glu, side=None):
    dirs = []
    ys = []
    gathered = []
    for d, reverse in ((0, False), (1, True)):
        xs, y_dir, ends, got = _s5_direction_fwd(p_in, prm['b_mat'][d], prm['c_mat'][d], prm['tabs'][d], reverse,
                                                 name="s5_fwd_rev" if reverse else "s5_fwd", side=None if reverse else side)
        gathered += got
        ys.append(y_dir)
        dirs.append((xs, ends))

    def post(yf, yb, u, dskip):
        ypre = yf + yb + dskip * u
        th, _ = _gelu_parts(ypre)
        return ypre, 0.5 * ypre * (1.0 + th)
    ypre, yg = _rowmap(post, [ys[0], ys[1], (p_in, CB_U, S5_WIDTH)], [prm['d'].reshape(1, -1)],
                       [(S5_WIDTH, F32), (S5_WIDTH, F32)], tl=512, name="s5_post")
    t = _mm(yg, w_glu, name="s5_glu_mm")

    def glu(ygv, tv):
        return ygv * _sigmoid(tv)
    y = _rowmap(glu, [yg, t], [], [(S5_WIDTH, BF16)], tl=512, name="s5_glu")[0]
    return y, (dirs, ypre, yg, t), gathered


def _s5_bwd(pg, prm, w_glu, saved, dy):
    dirs, ypre, yg, t = saved

    def glu_bwd(dyv, ygv, tv):
        s = _sigmoid(tv)
        return dyv * ygv * s * (1.0 - s), dyv * s
    dt, dyg_direct = _rowmap(glu_bwd, [dy, yg, t], [], [(S5_WIDTH, BF16), (S5_WIDTH, F32)], tl=512, name="s5_glu_bwd")
    grads = {'w_glu': _mm(yg, dt, ta=True, out_dtype=BF16, name="s5_dwglu")}
    dyg_mm = _mm(dt, w_glu, tb=True, name="s5_dyg")

    def post_bwd(dyd, dym, yp, u, dskip):
        th, dinner = _gelu_parts(yp)
        dyp = (dyd + dym) * (0.5 * (1.0 + th) + 0.5 * yp * (1.0 - th * th) * dinner)
        return dyp, dyp * dskip, jnp.sum(dyp * u, axis=0, keepdims=True)
    dyp, du_skip, dd = _rowmap(post_bwd, [dyg_direct, dyg_mm, ypre, (pg, CB_U, S5_WIDTH)], [prm['d'].reshape(1, -1)],
                               [(S5_WIDTH, F32), (S5_WIDTH, F32)], [(1, S5_WIDTH)], tl=512, name="s5_post_bwd")
    grads['d'] = dd[0]
    du = [du_skip]
    grads['raw'] = []
    for d, reverse in ((0, False), (1, True)):
        du_dir, d_bmat, d_cmat_t, da = _s5_direction_bwd(pg, dyp, *dirs[d], prm['b_mat'][d], prm['c_mat'][d], prm['tabs_adj'][d],
                                                         reverse, name="s5_bwd_rev" if reverse else "s5_bwd")
        du.append(du_dir)
        grads['raw'].append((d_bmat, d_cmat_t, da))
    return du, grads


def _split3(x):
    hi = x.astype(BF16)
    r = x - hi.astype(F32)
    mid = r.astype(BF16)
    return hi, mid, (r - mid.astype(F32)).astype(BF16)


def _exact_dot(ones, x, dims):
    parts = [lax.dot_general(ones, p, dims, preferred_element_type=F32) for p in _split3(x)]
    return parts[0] + parts[1] + parts[2]


_NN = (((1,), (0,)), ((), ()))
_NT = (((1,), (1,)), ((), ()))
_TN = (((0,), (0,)), ((), ()))


def _dot(a, b, dims=_NN):
    return lax.dot_general(a.astype(BF16), b.astype(BF16), dims, preferred_element_type=F32)


def _gla_chunk_mask(reverse):
    rows = lax.broadcasted_iota(jnp.int32, (GLA_CHUNK, GLA_CHUNK), 0)
    cols = lax.broadcasted_iota(jnp.int32, (GLA_CHUNK, GLA_CHUNK), 1)
    return (cols >= rows) if reverse else (cols <= rows)


def _gla_fwd(pg, la, reverse, *, name):
    length = la.shape[0]
    nch = length // GLA_CHUNK
    scale = GLA_HEAD_DIM ** -0.5
    last = 0 if reverse else GLA_CHUNK - 1
    hd = GLA_HEAD_DIM

    def body(q_ref, k_ref, v_ref, la_ref, o_ref, sp_ref, st_ref):
        @pl.when(pl.program_id(0) == 0)
        def _():
            st_ref[...] = jnp.zeros_like(st_ref)

        mask = _gla_chunk_mask(reverse)
        b = _exact_dot(mask.astype(BF16), la_ref[...], _NN)
        sp_ref[0] = st_ref[...]
        outs = []
        for h in range(GLA_HEADS):
            sl = slice(h * hd, (h + 1) * hd)
            bh = b[:, sl]
            bl = bh[last:last + 1, :]
            k = k_ref[:, sl]
            v = v_ref[:, sl]
            qd = q_ref[:, sl] * scale * jnp.exp(bh)
            kd = k * jnp.exp(-bh)
            ke = k * jnp.exp(bl - bh)
            st = st_ref[sl, :]
            p = jnp.where(mask, _dot(qd, kd, _NT), 0.0)
            outs.append(_dot(p, v) + _dot(qd, st, _NT))
            st_ref[sl, :] = st * jnp.exp(bl) + _dot(v, ke, _TN)
        o_ref[...] = jnp.concatenate(outs, axis=1)

    cmap = (lambda n: nch - 1 - n) if reverse else (lambda n: n)
    col = lambda cb: pl.BlockSpec((GLA_CHUNK, GLA_WIDTH), lambda n, cb=cb: (cmap(n), cb))
    return pl.pallas_call(
        body, name=name, grid=(nch,),
        in_specs=[col(CB_GQ), col(CB_GK), col(CB_GV), col(0)],
        out_specs=[col(0), pl.BlockSpec((1, GLA_WIDTH, hd), lambda n: (cmap(n), 0, 0))],
        out_shape=[jax.ShapeDtypeStruct((length, GLA_WIDTH), F32), jax.ShapeDtypeStruct((nch, GLA_WIDTH, hd), F32)],
        scratch_shapes=[pltpu.VMEM((GLA_WIDTH, hd), F32)],
        compiler_params=pltpu.CompilerParams(dimension_semantics=("arbitrary",)),
    )(pg, pg, pg, la)


def _gla_bwd(pg, la, do, sprev, reverse, *, name):
    length = la.shape[0]
    nch = length // GLA_CHUNK
    scale = GLA_HEAD_DIM ** -0.5
    last = 0 if reverse else GLA_CHUNK - 1
    hd = GLA_HEAD_DIM

    def body(q_ref, k_ref, v_ref, la_ref, do_ref, sp_ref, dq_ref, dk_ref, dv_ref, dla_ref, dst_ref):
        @pl.when(pl.program_id(0) == 0)
        def _():
            dst_ref[...] = jnp.zeros_like(dst_ref)

        mask = _gla_chunk_mask(reverse)
        tri = mask.astype(BF16)
        b = _exact_dot(tri, la_ref[...], _NN)
        is_last = lax.broadcasted_iota(jnp.int32, (GLA_CHUNK, hd), 0) == last
        dqs, dks, dvs, dbs = [], [], [], []
        for h in range(GLA_HEADS):
            sl = slice(h * hd, (h + 1) * hd)
            bh = b[:, sl]
            bl = bh[last:last + 1, :]
            eb, enb, ebl, el = jnp.exp(bh), jnp.exp(-bh), jnp.exp(bl - bh), jnp.exp(bl)
            k = k_ref[:, sl]
            v = v_ref[:, sl]
            dov = do_ref[:, sl]
            qd = q_ref[:, sl] * scale * eb
            kd = k * enb
            ke = k * ebl
            st = sp_ref[0, sl, :]
            dst = dst_ref[sl, :]
            p = jnp.where(mask, _dot(qd, kd, _NT), 0.0)
            dp = jnp.where(mask, _dot(dov, v, _NT), 0.0)
            dqd = _dot(dp, kd) + _dot(dov, st)
            dkd = _dot(dp, qd, _TN)
            dvs.append(_dot(p, dov, _TN) + _dot(ke, dst, _NT))
            dke = _dot(v, dst)
            dst_ref[sl, :] = dst * el + _dot(dov, qd, _TN)
            dbl = el * jnp.sum(dst * st, axis=0, keepdims=True) + jnp.sum(dke * ke, axis=0, keepdims=True)
            db = dqd * qd - dkd * kd - dke * ke
            dbs.append(jnp.where(is_last, db + dbl, db))
            dqs.append(dqd * eb * scale)
            dks.append(dkd * enb + dke * ebl)
        dq_ref[...] = jnp.concatenate(dqs, axis=1)
        dk_ref[...] = jnp.concatenate(dks, axis=1)
        dv_ref[...] = jnp.concatenate(dvs, axis=1)
        tri_t = _gla_chunk_mask(not reverse).astype(BF16)
        dla_ref[...] = _exact_dot(tri_t, jnp.concatenate(dbs, axis=1), _NN)

    cmap = (lambda n: n) if reverse else (lambda n: nch - 1 - n)
    col = lambda cb: pl.BlockSpec((GLA_CHUNK, GLA_WIDTH), lambda n, cb=cb: (cmap(n), cb))
    wide = jax.ShapeDtypeStruct((length, GLA_WIDTH), F32)
    return pl.pallas_call(
        body, name=name, grid=(nch,),
        in_specs=[col(CB_GQ), col(CB_GK), col(CB_GV), col(0), col(0),
                  pl.BlockSpec((1, GLA_WIDTH, hd), lambda n: (cmap(n), 0, 0))],
        out_specs=[col(0)] * 4, out_shape=[wide] * 4,
        scratch_shapes=[pltpu.VMEM((GLA_WIDTH, hd), F32)],
        compiler_params=pltpu.CompilerParams(dimension_semantics=("arbitrary",)),
    )(pg, pg, pg, la, do, sprev)


def _log_sigmoid(x):
    return jnp.minimum(x, 0.0) - jnp.log(1.0 + jnp.exp(-jnp.abs(x)))


def _gla_alpha_padded(w_alpha):
    w = jnp.zeros((2, 128, GLA_WIDTH), w_alpha.dtype)
    w = w.at[0, 0:GLA_LOWRANK].set(w_alpha[0])
    return w.at[1, GLA_LOWRANK:2 * GLA_LOWRANK].set(w_alpha[1])


def _gla_branch_fwd(pg, w_alpha, b_alpha, norm_gain):
    wa = _gla_alpha_padded(w_alpha).astype(BF16)

    def gates(z, w, bias):
        return (_log_sigmoid(_dot(z, w[0]) + bias[0:1]) / GLA_TAU, _log_sigmoid(_dot(z, w[1]) + bias[1:2]) / GLA_TAU)
    la_f, la_b = _rowmap(gates, [(pg, CB_Z, 128)], [wa, b_alpha], [(GLA_WIDTH, F32), (GLA_WIDTH, F32)], tl=512,
                         name="gla_gates")
    o_f, sp_f = _gla_fwd(pg, la_f, False, name="gla_fwd")
    o_b, sp_b = _gla_fwd(pg, la_b, True, name="gla_fwd_rev")

    def post(of, ob, gate, gn):
        o = of + ob
        on = jnp.concatenate([o[:, s:s + GLA_HEAD_DIM] * _rms(o[:, s:s + GLA_HEAD_DIM]) * gn
                              for s in range(0, GLA_WIDTH, GLA_HEAD_DIM)], axis=1)
        return o, on * (gate * _sigmoid(gate))
    o, y = _rowmap(post, [o_f, o_b, (pg, CB_GG, GLA_WIDTH)], [norm_gain.reshape(1, -1)],
                   [(GLA_WIDTH, F32), (GLA_WIDTH, BF16)], tl=512, name="gla_post")
    return y, (wa, la_f, la_b, sp_f, sp_b, o)


def _gla_branch_bwd(pg, w_alpha, b_alpha, norm_gain, saved, dy):
    wa, la_f, la_b, sp_f, sp_b, o = saved

    def post_bwd(dyv, ov, gate, gn):
        s = _sigmoid(gate)
        dos, dgn, ons = [], [], []
        for c in range(0, GLA_WIDTH, GLA_HEAD_DIM):
            oh = ov[:, c:c + GLA_HEAD_DIM]
            r = _rms(oh)
            don = dyv[:, c:c + GLA_HEAD_DIM] * (gate[:, c:c + GLA_HEAD_DIM] * s[:, c:c + GLA_HEAD_DIM])
            gd = don * gn
            dos.append(r * gd - oh * (r * r * r) * jnp.mean(oh * gd, axis=-1, keepdims=True))
            dgn.append(jnp.sum(don * oh * r, axis=0, keepdims=True))
            ons.append(oh * r * gn)
        on = jnp.concatenate(ons, axis=1)
        dgate = dyv * on * (s * (1.0 + gate * (1.0 - s)))
        return jnp.concatenate(dos, axis=1), dgate, jnp.concatenate(dgn, axis=1)
    do, dgate, dgn = _rowmap(post_bwd, [dy, o, (pg, CB_GG, GLA_WIDTH)], [norm_gain.reshape(1, -1)],
                             [(GLA_WIDTH, F32), (GLA_WIDTH, F32)], [(1, GLA_WIDTH)], tl=512, name="gla_post_bwd")
    dq_f, dk_f, dv_f, dla_f = _gla_bwd(pg, la_f, do, sp_f, False, name="gla_bwd")
    dq_b, dk_b, dv_b, dla_b = _gla_bwd(pg, la_b, do, sp_b, True, name="gla_bwd_rev")

    def gates_bwd(z, dlf, dlb, w, bias):
        dz = jnp.zeros_like(z)
        dlogits, dbs = [], []
        for d, dl in ((0, dlf), (1, dlb)):
            logit = _dot(z, w[d]) + bias[d:d + 1]
            dlogit = dl * (1.0 / GLA_TAU) * jnp.exp(_log_sigmoid(-logit))
            dz = dz + _dot(dlogit, w[d], _NT)
            dlogits.append(dlogit)
            dbs.append(jnp.sum(dlogit, axis=0, keepdims=True))
        return dz, dlogits[0], dlogits[1], dbs[0], dbs[1]
    dz, dlg_f, dlg_b, dba_f, dba_b = _rowmap(
        gates_bwd, [(pg, CB_Z, 128), dla_f, dla_b], [wa, b_alpha], [(128, F32), (GLA_WIDTH, BF16), (GLA_WIDTH, BF16)],
        [(1, GLA_WIDTH), (1, GLA_WIDTH)], tl=512, name="gla_gates_bwd")
    dwa_f = _mm(dlg_f, (pg, CB_Z, 128), ta=True, name="gla_dwalpha")
    dwa_b = _mm(dlg_b, (pg, CB_Z, 128), ta=True, name="gla_dwalpha")
    grads = {'w_alpha': jnp.stack([dwa_f[:, 0:GLA_LOWRANK].T, dwa_b[:, GLA_LOWRANK:2 * GLA_LOWRANK].T]),
             'b_alpha': jnp.concatenate([dba_f, dba_b], axis=0),
             'norm': jnp.sum(dgn.reshape(GLA_HEADS, GLA_HEAD_DIM), axis=0)}
    return [dq_f, dq_b], [dk_f, dk_b], [dv_f, dv_b], dgate, dz, grads


def _rope_tables(length):
    half = ATTN_HEAD_DIM // 2
    inv_freq = ROPE_BASE ** (-jnp.arange(half // 2, dtype=F32) * 2.0 / half)
    t = jnp.arange(length, dtype=jnp.int32)
    def one(pos):
        ang = pos.astype(F32)[:, None] * inv_freq[None, :]
        c, s = jnp.cos(ang), jnp.sin(ang)
        return jnp.concatenate([c, c], axis=1), jnp.concatenate([-s, s], axis=1)
    c_r, s_r = one(t // GRID_W)
    c_c, s_c = one(t % GRID_W)
    return jnp.concatenate([c_r, c_c], axis=1), jnp.concatenate([s_r, s_c], axis=1)


def _rope_swap(y):
    w = y.shape[1]
    lane = lax.broadcasted_iota(jnp.int32, y.shape, 1)
    return jnp.where(lane % 32 < 16, pltpu.roll(y, w - 16, 1), pltpu.roll(y, 16, 1))


def _head_sums(x, ones):
    parts = [lax.dot_general(p, ones, _NN, preferred_element_type=F32) for p in _split3(x)]
    return parts[0] + parts[1] + parts[2]


def _head_ones(width):
    seg = np.arange(width) // ATTN_HEAD_DIM
    return jnp.asarray(seg[:, None] == seg[None, :], BF16)


def _qk_prep_fwd(pg, cb, width, gain, cos, sin, scale, *, name):
    heads = width // ATTN_HEAD_DIM
    def fn(x, c, s, g, ones):
        r = lax.rsqrt(_head_sums(x * x, ones) * (1.0 / ATTN_HEAD_DIM) + NORM_EPS)
        y = x * r * g
        return (y * c + _rope_swap(y) * s) * scale
    return _rowmap(fn, [(pg, cb, width), jnp.tile(cos, (1, heads)), jnp.tile(sin, (1, heads))],
                   [jnp.tile(gain, heads).reshape(1, -1), _head_ones(width)], [(width, BF16)], tl=512, name=name)[0]


def _qk_prep_bwd(pg, cb, width, gain, cos, sin, scale, dout, *, name):
    heads = width // ATTN_HEAD_DIM
    def fn(x, dov, c, s, g, ones):
        r = lax.rsqrt(_head_sums(x * x, ones) * (1.0 / ATTN_HEAD_DIM) + NORM_EPS)
        dos = dov * scale
        dy = dos * c + _rope_swap(dos * s)
        gd = dy * g
        dx = r * gd - x * (r * r * r) * (_head_sums(x * gd, ones) * (1.0 / ATTN_HEAD_DIM))
        return dx, jnp.sum(dy * x * r, axis=0, keepdims=True)
    dx, dg = _rowmap(fn, [(pg, cb, width), dout, jnp.tile(cos, (1, heads)), jnp.tile(sin, (1, heads))],
                     [jnp.tile(gain, heads).reshape(1, -1), _head_ones(width)], [(width, F32)], [(1, width)], tl=512,
                     name=name)
    return dx, jnp.sum(dg.reshape(heads, ATTN_HEAD_DIM), axis=0)


def _to_heads(x, heads):
    return jnp.transpose(x.reshape(x.shape[0], heads, ATTN_HEAD_DIM), (1, 0, 2))


def _from_heads(x):
    return jnp.transpose(x, (1, 0, 2)).reshape(x.shape[1], x.shape[0] * ATTN_HEAD_DIM)


ATTN_GROUP = ATTN_Q_HEADS // ATTN_KV_HEADS
ATTN_TQ = 256


def _attn_fwd(q, k, v, side=None):
    length = q.shape[1]
    tq = min(ATTN_TQ, length)
    grid = (ATTN_KV_HEADS, length // tq)

    def compute(refs):
        q_ref, k_ref, v_ref, o_ref = refs
        kk, vv = k_ref[0], v_ref[0]
        for g in range(ATTN_GROUP):
            s = _dot(q_ref[g], kk, _NT)
            p = jnp.exp(s - jnp.max(s, axis=-1, keepdims=True))
            o_ref[g] = _dot(p, vv) / jnp.sum(p, axis=-1, keepdims=True)

    def body(*refs):
        _carried(side, grid, refs, 3, 1, 0, compute)

    kv_spec = pl.BlockSpec((1, length, ATTN_HEAD_DIM), lambda h, i: (h, 0, 0))
    q_spec = pl.BlockSpec((ATTN_GROUP, tq, ATTN_HEAD_DIM), lambda h, i: (h, i, 0))
    (out,), gathered = _side_call(
        body, side, name="attn_fwd", grid=grid, in_specs=[q_spec, kv_spec, kv_spec], out_specs=[q_spec],
        out_shape=[jax.ShapeDtypeStruct(q.shape, F32)], scratch=[], args=[q, k, v], semantics=("parallel", "parallel"))
    return out, gathered


def _attn_bwd(q, k, v, o, do, side=None):
    length = q.shape[1]
    tq = min(ATTN_TQ, length)
    grid = (ATTN_KV_HEADS, length // tq)

    def body(*refs):
        _carried(side, grid, refs, 5, 3, 0, compute)

    def compute(refs):
        q_ref, k_ref, v_ref, o_ref, do_ref, dq_ref, dk_ref, dv_ref = refs

        @pl.when(pl.program_id(1) == 0)
        def _():
            dk_ref[...] = jnp.zeros_like(dk_ref)
            dv_ref[...] = jnp.zeros_like(dv_ref)

        kk, vv = k_ref[0], v_ref[0]
        for g in range(ATTN_GROUP):
            qg, dog = q_ref[g], do_ref[g]
            s = _dot(qg, kk, _NT)
            p = jnp.exp(s - jnp.max(s, axis=-1, keepdims=True))
            p = p * (1.0 / jnp.sum(p, axis=-1, keepdims=True))
            dp = _dot(dog, vv, _NT)
            ds = p * (dp - jnp.sum(dog * o_ref[g], axis=-1, keepdims=True))
            dq_ref[g] = _dot(ds, kk)
            dk_ref[0] += _dot(ds, qg, _TN)
            dv_ref[0] += _dot(p, dog, _TN)

    kv_spec = pl.BlockSpec((1, length, ATTN_HEAD_DIM), lambda h, i: (h, 0, 0))
    q_spec = pl.BlockSpec((ATTN_GROUP, tq, ATTN_HEAD_DIM), lambda h, i: (h, i, 0))
    return _side_call(
        body, side, name="attn_bwd", grid=grid, in_specs=[q_spec, kv_spec, kv_spec, q_spec, q_spec],
        out_specs=[q_spec, kv_spec, kv_spec],
        out_shape=[jax.ShapeDtypeStruct(q.shape, F32), jax.ShapeDtypeStruct(k.shape, F32), jax.ShapeDtypeStruct(k.shape, F32)],
        scratch=[], args=[q, k, v, o, do], semantics=("parallel", "arbitrary"))


def _attn_branch_fwd(pg, q_gain, k_gain, side=None):
    cos, sin = _rope_tables(pg.shape[0])
    qp = _qk_prep_fwd(pg, CB_AQ, ATTN_WIDTH, q_gain, cos, sin, ATTN_HEAD_DIM ** -0.5, name="attn_q_prep")
    kp = _qk_prep_fwd(pg, CB_AK, ATTN_KV_WIDTH, k_gain, cos, sin, 1.0, name="attn_k_prep")
    qh, kh = _to_heads(qp, ATTN_Q_HEADS), _to_heads(kp, ATTN_KV_HEADS)
    vh = _to_heads(pg[:, P_OFF + 3200:P_OFF + 3328].astype(BF16), ATTN_KV_HEADS)
    oh, gathered = _attn_fwd(qh, kh, vh, side)
    return _from_heads(oh).astype(BF16), (cos, sin, qh, kh, vh, oh), gathered


def _attn_branch_bwd(pg, q_gain, k_gain, saved, dy, side=None):
    cos, sin, qh, kh, vh, oh = saved
    (dqh, dkh, dvh), carried = _attn_bwd(qh, kh, vh, oh, _to_heads(dy, ATTN_Q_HEADS), side)
    dq, dqg = _qk_prep_bwd(pg, CB_AQ, ATTN_WIDTH, q_gain, cos, sin, ATTN_HEAD_DIM ** -0.5, _from_heads(dqh),
                           name="attn_q_prep_bwd")
    dk, dkg = _qk_prep_bwd(pg, CB_AK, ATTN_KV_WIDTH, k_gain, cos, sin, 1.0, _from_heads(dkh), name="attn_k_prep_bwd")
    return dq, dk, _from_heads(dvh), {'q_norm': dqg, 'k_norm': dkg}, carried


def _gate_cols():
    return [slice(i * D_MODEL, (i + 1) * D_MODEL) for i in range(3)]


def _mixer_fwd(x, lw, side_in=None, after_in=None, side_s5=None, side_attn=None):
    h = _rmsnorm_fwd(x, lw['mix_norm'])
    if side_in is None:
        pg = _mm(h, lw['w_pg'], name="mix_in")
    else:
        pg, got_in = _mm(h, lw['w_pg'], side=side_in, name="mix_in")
        after_in(got_in)
    y_s5, s_s5, got_s5 = _s5_fwd(pg, lw['s5'], lw['s5_w_glu'], side_s5)
    y_gla, s_gla = _gla_branch_fwd(pg, lw['gla_w_alpha'], lw['gla_b_alpha'], lw['gla_norm'])
    y_att, s_att, got_attn = _attn_branch_fwd(pg, lw['attn_q_norm'], lw['attn_k_norm'], side_attn)
    ys = (y_s5, y_gla, y_att)
    br = [_mm(y, lw[n], name="mix_branch") for y, n in zip(ys, ('w_branch_s5', 'w_branch_gla', 'w_branch_attn'))]

    def merge(g0, g1, g2, b0, b1, b2, bias):
        acc = None
        for g, b, c in zip((g0, g1, g2), (b0, b1, b2), _gate_cols()):
            term = _sigmoid(g + bias[:, c]) * b
            acc = term if acc is None else acc + term
        return acc
    merged = _rowmap(merge, [(pg, 0, D_MODEL), (pg, 1, D_MODEL), (pg, 2, D_MODEL)] + br,
                     [lw['b_merge_gate'].reshape(1, -1)], [(D_MODEL, BF16)], tl=256, name="mix_merge")[0]
    x_out = _mm(merged, lw['w_out'], add=x, name="mix_out")
    return x_out, (x, h, pg, ys, (s_s5, s_gla, s_att), br, merged), (got_s5, got_attn)


def _mixer_bwd(saved, lw, dx_out, side=None):
    x, h, pg, ys, (s_s5, s_gla, s_att), br, merged = saved
    grads = {'w_out': _mm(merged, dx_out, ta=True, out_dtype=BF16, name="mix_dwout")}
    dmerged = _mm(dx_out, lw['w_out'], tb=True, name="mix_dmerged")

    def merge_bwd(g0, g1, g2, b0, b1, b2, dm, bias):
        dbr, dgp = [], []
        for g, b, c in zip((g0, g1, g2), (b0, b1, b2), _gate_cols()):
            s = _sigmoid(g + bias[:, c])
            dbr.append(dm * s)
            dgp.append(dm * b * (s * (1.0 - s)))
        dgp = jnp.concatenate(dgp, axis=1)
        return dbr[0], dbr[1], dbr[2], dgp, jnp.sum(dgp, axis=0, keepdims=True)
    d0, d1, d2, dgpre, dbias = _rowmap(
        merge_bwd, [(pg, 0, D_MODEL), (pg, 1, D_MODEL), (pg, 2, D_MODEL)] + br + [dmerged],
        [lw['b_merge_gate'].reshape(1, -1)], [(D_MODEL, BF16)] * 3 + [(GATE_WIDTH, BF16)], [(1, GATE_WIDTH)], tl=256,
        name="mix_merge_bwd")
    grads['b_merge_gate'] = dbias[0]
    dys = []
    for y, dbr, n in zip(ys, (d0, d1, d2), ('w_branch_s5', 'w_branch_gla', 'w_branch_attn')):
        grads[n] = _mm(y, dbr, ta=True, out_dtype=BF16, name="mix_dwbranch")
        dys.append(_mm(dbr, lw[n], tb=True, name="mix_dy"))
    du, g_s5 = _s5_bwd(pg, lw['s5'], lw['s5_w_glu'], s_s5, dys[0])
    dgq, dgk, dgv, dgg, dz, g_gla = _gla_branch_bwd(pg, lw['gla_w_alpha'], lw['gla_b_alpha'], lw['gla_norm'], s_gla, dys[1])
    daq, dak, dav, g_att, carried = _attn_branch_bwd(pg, lw['attn_q_norm'], lw['attn_k_norm'], s_att, dys[2], side)

    def assemble(dgp, u0, u1, u2, q0, q1, k0, k1, v0, v1, gg, aq, ak, av, z):
        pad = jnp.zeros((dgp.shape[0], IN_PAD - 3456), F32)
        parts = [dgp.astype(F32), u0 + u1 + u2, q0 + q1, k0 + k1, v0 + v1, gg, aq, ak, av, z, pad]
        return jnp.concatenate(parts, axis=1)
    dpg = _rowmap(assemble, [dgpre] + du + dgq + dgk + dgv + [dgg, daq, dak, dav, dz], [], [(PG_WIDTH, BF16)], tl=256,
                  name="mix_dpg")[0]
    grads['w_pg'] = _mm(h, dpg, ta=True, out_dtype=BF16, name="mix_dwpg")
    dh = _mm(dpg, lw['w_pg'], tb=True, name="mix_dh")
    dx, grads['mix_norm'] = _rmsnorm_bwd(x, lw['mix_norm'], dh, dx_out)
    grads['s5'], grads['gla'], grads['attn'] = g_s5, g_gla, g_att
    return dx, grads, carried


def _loss_head(x, gain, target):
    width = x.shape[1]

    def fn(xv, tv, g):
        r = _rms(xv)
        err = xv * r * g - tv
        dy = err * (1.0 / width)
        gd = dy * g
        dx = r * gd - xv * (r * r * r) * jnp.mean(xv * gd, axis=-1, keepdims=True)
        loss = jnp.sum(0.5 * jnp.mean(err * err, axis=-1, keepdims=True), axis=0, keepdims=True)
        return dx, jnp.broadcast_to(loss, (1, 128)), jnp.sum(dy * xv * r, axis=0, keepdims=True)
    dx, loss, dgain = _rowmap(fn, [x, target], [gain.reshape(1, -1)], [(width, F32)], [(1, 128), (1, width)], tl=256,
                              name="loss_head")
    return loss[0, 0], dx, dgain[0]


def _row_tile(rows, cap=256):
    for t in range(cap - cap % 16, 0, -16):
        if rows % t == 0:
            return t
    return rows


def _reduce_adamw(parts, w, m, v, *, name):
    r, c = w.shape
    if len(parts) > 1 and parts[0].shape[1] % 8:
        parts = [jnp.concatenate(parts, axis=1)]
    nparts, rows = parts[0].shape[0], parts[0].shape[1]
    tr = _row_tile(rows)
    per = rows // tr

    def body(*refs):
        p_refs, (w_ref, m_ref, v_ref, g_ref, d_ref, m2_ref, v2_ref) = refs[:len(parts)], refs[len(parts):]
        g = None
        for k, p_ref in enumerate(p_refs):
            gk = p_ref[0].astype(F32)
            for j in range(1, nparts):
                gk = gk + p_ref[j].astype(F32)
            g = gk if g is None else jnp.where(pl.program_id(0) // per == k, gk, g)
        m2 = ADAM_B1 * m_ref[...] + (1.0 - ADAM_B1) * g
        v2 = ADAM_B2 * v_ref[...] + (1.0 - ADAM_B2) * (g * g)
        m_hat = m2 / (1.0 - ADAM_B1 ** ADAM_STEP)
        v_hat = v2 / (1.0 - ADAM_B2 ** ADAM_STEP)
        g_ref[...] = g
        d_ref[...] = -ADAM_LR * (m_hat / (jnp.sqrt(v_hat) + ADAM_EPS) + ADAM_WD * w_ref[...])
        m2_ref[...] = m2
        v2_ref[...] = v2

    flat = pl.BlockSpec((tr, c), lambda i: (i, 0))
    p_specs = [pl.BlockSpec((nparts, tr, c), lambda i, k=k: (0, jnp.clip(i - k * per, 0, per - 1), 0)) for k in range(len(parts))]
    return pl.pallas_call(
        body, name=name, grid=(r // tr,), in_specs=p_specs + [flat, flat, flat],
        out_specs=[flat] * 4, out_shape=[jax.ShapeDtypeStruct((r, c), F32)] * 4,
        compiler_params=pltpu.CompilerParams(dimension_semantics=("parallel",)),
    )(*parts, w, m, v)


def _all_gather(blocks, *, name):
    side = _SideGather(blocks)

    def body(*refs):
        start, finish = side.hooks(refs)
        start()
        finish()

    return pl.pallas_call(body, name=name, out_shape=side.out_shape, in_specs=side.in_specs, out_specs=side.out_specs,
                          scratch_shapes=side.scratch)(*blocks)


class _SideGather:
    def __init__(self, blocks):
        self.blocks = list(blocks)
        self.n = n = len(self.blocks)
        hbm = pl.BlockSpec(memory_space=pl.ANY)
        self.in_specs, self.out_specs = [hbm] * n, [hbm] * n
        self.out_shape = [jax.ShapeDtypeStruct((N_DEV,) + b.shape, b.dtype) for b in self.blocks]
        self.scratch = [pltpu.SemaphoreType.DMA((n, 7)), pltpu.SemaphoreType.DMA((n, 7)), pltpu.SemaphoreType.DMA((n,))]

    def hooks(self, refs):
        n = self.n
        x_refs, out_refs = refs[:n], refs[n:2 * n]
        send_sems, recv_sems, local_sems = refs[2 * n:]
        x, y, c = lax.axis_index("x"), lax.axis_index("y"), lax.axis_index("c")
        me, sibling = (x, y, c), (x, y, 1 - c)
        chips = [(1 - x, y), (x, 1 - y), (1 - x, 1 - y)]

        def slot(t, px, py, pc):
            return out_refs[t].at[4 * px + 2 * py + pc]

        def copy(t, k, blk, to, own=False):
            return pltpu.make_async_remote_copy(
                src_ref=x_refs[t] if own else slot(t, *blk), dst_ref=slot(t, *blk), send_sem=send_sems.at[t, k],
                recv_sem=recv_sems.at[t, k], device_id=to, device_id_type=pl.DeviceIdType.MESH)

        def mine(t):
            return pltpu.make_async_copy(x_refs[t], slot(t, *me), local_sems.at[t])

        def first(t):
            return [copy(t, 0, me, sibling, own=True)] + [copy(t, 1 + j, me, (*chip, c), own=True) for j, chip in enumerate(chips)]

        def start():
            for t in range(n):
                mine(t).start()
            for t in range(n):
                for cp in first(t):
                    cp.start()

        def finish():
            passed = []
            for j, chip in enumerate(chips):
                for t in range(n):
                    copy(t, 1 + j, (*chip, c), me).wait_recv()
                    passed.append(copy(t, 4 + j, (*chip, c), sibling))
                    passed[-1].start()
            for t in range(n):
                copy(t, 0, sibling, me).wait_recv()
            for j, chip in enumerate(chips):
                for t in range(n):
                    copy(t, 4 + j, (*chip, 1 - c), me).wait_recv()
            for t in range(n):
                for cp in first(t):
                    cp.wait_send()
            for cp in passed:
                cp.wait_send()
            for t in range(n):
                mine(t).wait()

        return start, finish


def _first_last_step(grid):
    ids = [pl.program_id(a) for a in range(len(grid))]
    first = functools.reduce(lambda p, q: p & q, [i == 0 for i in ids])
    last = functools.reduce(lambda p, q: p & q, [i == n - 1 for i, n in zip(ids, grid)])
    return first, last


def _carried(side, grid, refs, n_in, n_out, n_scratch, compute):
    if side is None:
        compute(refs)
        return
    n = side.n
    main = refs[:n_in] + refs[n_in + n:n_in + n + n_out] + refs[n_in + 2 * n + n_out:n_in + 2 * n + n_out + n_scratch]
    side_refs = refs[n_in:n_in + n] + refs[n_in + n + n_out:n_in + 2 * n + n_out] + refs[n_in + 2 * n + n_out + n_scratch:]
    start, finish = side.hooks(side_refs)
    first, last = _first_last_step(grid)
    pl.when(first)(start)
    compute(main)
    pl.when(last)(finish)


N_CHIP = N_DEV // 2


def _swap_with_sibling(arrays, *, name):
    n = len(arrays)

    def body(*refs):
        src_refs, out_refs = refs[:n], refs[n:2 * n]
        send_sems, recv_sems = refs[2 * n:]
        sibling = (lax.axis_index("x"), lax.axis_index("y"), 1 - lax.axis_index("c"))
        copies = [pltpu.make_async_remote_copy(
            src_ref=src_refs[t], dst_ref=out_refs[t], send_sem=send_sems.at[t], recv_sem=recv_sems.at[t],
            device_id=sibling, device_id_type=pl.DeviceIdType.MESH) for t in range(n)]
        for cp in copies:
            cp.start()
        for cp in copies:
            cp.wait()

    hbm = pl.BlockSpec(memory_space=pl.ANY)
    return pl.pallas_call(
        body, name=name, out_shape=[jax.ShapeDtypeStruct(a.shape, a.dtype) for a in arrays],
        in_specs=[hbm] * n, out_specs=[hbm] * n,
        scratch_shapes=[pltpu.SemaphoreType.DMA((n,)), pltpu.SemaphoreType.DMA((n,))],
    )(*arrays)


def _exchange_chips(stacks, *, name):
    side = _SideChipExchange(stacks)

    def body(*refs):
        start, finish = side.hooks(refs)
        start()
        finish()

    return pl.pallas_call(body, name=name, out_shape=side.out_shape, in_specs=side.in_specs, out_specs=side.out_specs,
                          scratch_shapes=side.scratch)(*stacks)


class _SideChipExchange:
    def __init__(self, stacks):
        self.blocks = list(stacks)
        self.n = n = len(self.blocks)
        hbm = pl.BlockSpec(memory_space=pl.ANY)
        self.in_specs, self.out_specs = [hbm] * n, [hbm] * n
        self.out_shape = [jax.ShapeDtypeStruct(s.shape, s.dtype) for s in self.blocks]
        self.scratch = [pltpu.SemaphoreType.DMA((n, N_DEV - 1)), pltpu.SemaphoreType.DMA((n, N_DEV - 1)),
                        pltpu.SemaphoreType.DMA((n,))]

    def hooks(self, refs):
        n = self.n
        g_refs, out_refs = refs[:n], refs[n:2 * n]
        send_sems, recv_sems, local_sems = refs[2 * n:]
        x, y, c = lax.axis_index("x"), lax.axis_index("y"), lax.axis_index("c")

        def copies():
            mine, remote = [], []
            for t in range(n):
                peers = self.blocks[t].shape[0]
                me = 2 * x + y if peers == N_CHIP else 4 * x + 2 * y + c
                mine.append(pltpu.make_async_copy(g_refs[t].at[me], out_refs[t].at[me], local_sems.at[t]))
                for k in range(1, peers):
                    if peers == N_CHIP:
                        px, py, pc = x ^ (k >> 1 & 1), y ^ (k & 1), c
                        slot = 2 * px + py
                    else:
                        px, py, pc = x ^ (k >> 2 & 1), y ^ (k >> 1 & 1), c ^ (k & 1)
                        slot = 4 * px + 2 * py + pc
                    remote.append(pltpu.make_async_remote_copy(
                        src_ref=g_refs[t].at[slot], dst_ref=out_refs[t].at[me], send_sem=send_sems.at[t, k - 1],
                        recv_sem=recv_sems.at[t, k - 1], device_id=(px, py, pc), device_id_type=pl.DeviceIdType.MESH))
            return mine, remote

        def start():
            mine, remote = copies()
            for cp in mine + remote:
                cp.start()

        def finish():
            mine, remote = copies()
            for cp in remote:
                cp.wait_recv()
            for cp in remote:
                cp.wait_send()
            for cp in mine:
                cp.wait()

        return start, finish


def _pair_sum(a, b):
    return _rowmap(lambda u, v: u.astype(F32) + v.astype(F32), [a, b], [], [(a.shape[1], BF16)], tl=_row_tile(a.shape[0], 512),
                   name="pair_sum")[0]


SMALL_COLS = 128
S5_MATRICES = ('s5_b_re', 's5_b_im', 's5_c_re', 's5_c_im')


def _pack_small(arrays):
    flat = jnp.concatenate([a.astype(F32).reshape(-1, SMALL_COLS) for a in arrays], axis=0)
    return jnp.pad(flat, ((0, -flat.shape[0] % 256), (0, 0)))


def _unpack_small(packed, shapes):
    out, off = [], 0
    for s in shapes:
        r = math.prod(s) // SMALL_COLS
        out.append(packed[off:off + r].reshape(s))
        off += r
    return out


def _split_shards(full, axis):
    shape = full.shape
    split = full.reshape(shape[:axis] + (N_DEV, shape[axis] // N_DEV) + shape[axis + 1:])
    return jnp.moveaxis(split, axis, 0)


def _join_shards(stack, axis):
    moved = jnp.moveaxis(stack, 0, axis)
    shape = moved.shape
    return moved.reshape(shape[:axis] + (shape[axis] * shape[axis + 1],) + shape[axis + 2:])


def _w_in_unpadded(w):
    return jnp.concatenate([w[..., :2560], w[..., 3328:3360], w[..., 2560:3328]], axis=-1)


FFN1_W = ('ffn1_w_gate', 'ffn1_w_up', 'ffn1_w_down')
FFN2_W = ('ffn2_w_gate', 'ffn2_w_up', 'ffn2_w_down')
MIX_IN_W = ('w_in', 'w_merge_gate')
MIX_REST_W = ('s5_w_glu', 'gla_w_alpha', 'gla_b_alpha', 'w_branch_s5', 'w_branch_gla', 'w_branch_attn', 'w_out')


def _mixer_weights(full, w, s5, i):
    lw = {n: w[n][i] for n in ('mix_norm', 'gla_norm', 'attn_q_norm', 'attn_k_norm', 'b_merge_gate')}
    lw['s5'] = {'b_mat': s5['b_mat'][i], 'c_mat': s5['c_mat'][i], 'tabs': [t[i] for t in s5['tabs']],
                'tabs_adj': [t[i] for t in s5['tabs_adj']], 'd': w['s5_d'][i]}
    w_in = full['w_in']
    pad = jnp.zeros((D_MODEL, IN_PAD - IN_WIDTH), w_in.dtype)
    lw['w_pg'] = jnp.concatenate([full['w_merge_gate'], w_in[:, :2560], w_in[:, 2592:], w_in[:, 2560:2592], pad], axis=1)
    return lw


def _mixer_weights_rest(full):
    lw = {n: full[n] for n in MIX_REST_W if n != 'gla_b_alpha'}
    lw['gla_b_alpha'] = full['gla_b_alpha'].astype(F32)
    return lw


def _chip_sums(grads, names):
    core = lax.axis_index("c")
    own, for_sibling = [], []
    for n in names:
        by_owner = _split_shards(grads[n], SHARD_AXIS[n] - 1).astype(BF16)
        by_owner = by_owner.reshape((N_CHIP, 2) + by_owner.shape[1:])
        own.append(lax.dynamic_index_in_dim(by_owner, core, axis=1, keepdims=False))
        for_sibling.append(lax.dynamic_index_in_dim(by_owner, 1 - core, axis=1, keepdims=False))
    from_sibling = _swap_with_sibling(for_sibling, name="exchange_grads_sibling")
    return [_pair_sum(a.reshape(-1, a.shape[-1]), b.reshape(-1, b.shape[-1])).reshape(a.shape)
            for a, b in zip(own, from_sibling)]


def _step_local(x, target, w, shards):
    s5 = _s5_setup(w)
    full = [{} for _ in range(DEPTH)]

    def wanted(i, *groups):
        return _SideGather([shards[n][i] for names in groups for n in names])

    def arrived(i, stacks, *groups):
        names = [n for group in groups for n in group]
        for n, st in zip(names, stacks):
            full[i][n] = _join_shards(st, SHARD_AXIS[n] - 1)

    arrived(0, _all_gather([shards[n][0] for n in FFN1_W], name="gather_first"), FFN1_W)
    saved, lws = [], []
    for i in range(DEPTH):
        f, first = full[i], i == 0
        x, s1, got = _ffn_fwd(x, w['ffn1_norm'][i], f['ffn1_w_gate'], f['ffn1_w_up'], f['ffn1_w_down'],
                              wanted(i, MIX_IN_W) if first else None)
        if first:
            arrived(i, got, MIX_IN_W)
        lw = _mixer_weights(f, w, s5, i)
        lws.append(lw)

        def after_in(got_in, i=i, lw=lw):
            arrived(i, got_in, MIX_REST_W)
            lw.update(_mixer_weights_rest(full[i]))
        if not first:
            lw.update(_mixer_weights_rest(f))
        x, s2, (got_s5, got) = _mixer_fwd(x, lw, wanted(i, MIX_REST_W) if first else None, after_in,
                                          wanted(i, FFN2_W) if first else None,
                                          wanted(i + 1, FFN1_W, MIX_IN_W, MIX_REST_W) if first else wanted(i, FFN2_W))
        if first:
            arrived(i, got_s5, FFN2_W)
            arrived(i + 1, got, FFN1_W, MIX_IN_W, MIX_REST_W)
        else:
            arrived(i, got, FFN2_W)
        x, s3, _ = _ffn_fwd(x, w['ffn2_norm'][i], f['ffn2_w_gate'], f['ffn2_w_up'], f['ffn2_w_down'])
        saved.append((s1, s2, s3))
    loss, dx, d_final = _loss_head(x, w['final_norm'], target)
    per_layer, incoming = [None] * DEPTH, [{} for _ in range(DEPTH)]
    later = [n for n in SHARDED if n not in FFN2_W]
    pending = []
    for i in reversed(range(DEPTH)):
        f, lw, (s1, s2, s3), g = full[i], lws[i], saved[i], {}
        dx, g['ffn2_norm'], g['ffn2_w_gate'], g['ffn2_w_up'], g['ffn2_w_down'] = _ffn_bwd(
            s3, w['ffn2_norm'][i], f['ffn2_w_gate'], f['ffn2_w_up'], f['ffn2_w_down'], dx)
        pending.append((i, FFN2_W, [_split_shards(g[n], SHARD_AXIS[n] - 1).astype(BF16) for n in FFN2_W]))
        dx, gm, carried = _mixer_bwd(s2, lw, dx, _SideChipExchange([s for _, _, sums in pending for s in sums]))
        for layer, names, _ in pending:
            incoming[layer].update(zip(names, carried[:len(names)]))
            carried = carried[len(names):]
        dx, g['ffn1_norm'], g['ffn1_w_gate'], g['ffn1_w_up'], g['ffn1_w_down'] = _ffn_bwd(
            s1, w['ffn1_norm'][i], f['ffn1_w_gate'], f['ffn1_w_up'], f['ffn1_w_down'], dx)
        g['w_merge_gate'] = gm['w_pg'][:, :GATE_WIDTH]
        g['w_in'] = _w_in_unpadded(gm['w_pg'][:, GATE_WIDTH:])
        for n in ('w_out', 'b_merge_gate', 'w_branch_s5', 'w_branch_gla', 'w_branch_attn', 'mix_norm'):
            g[n] = gm[n]
        g['s5_d'], g['s5_w_glu'], g['s5_raw'] = gm['s5']['d'], gm['s5']['w_glu'], gm['s5']['raw']
        g['gla_w_alpha'], g['gla_b_alpha'], g['gla_norm'] = gm['gla']['w_alpha'], gm['gla']['b_alpha'], gm['gla']['norm']
        g['attn_q_norm'], g['attn_k_norm'] = gm['attn']['q_norm'], gm['attn']['k_norm']
        per_layer[i] = g
        pending = [(i, later, _chip_sums(g, later))]
    incoming[0].update(zip(later, _exchange_chips(pending[0][2], name="exchange_grads_chips")))
    stacked = _s5_param_grads(w, [g['s5_raw'] for g in per_layer])
    stacked['final_norm'] = d_final
    return loss, dx, per_layer, stacked, incoming


def kernel(x, ffn1_norm, ffn1_w_gate, ffn1_w_up, ffn1_w_down, mix_norm, w_in, s5_lambda_re, s5_lambda_im, s5_log_dt, s5_b_re, s5_b_im, s5_c_re, s5_c_im, s5_d, s5_w_glu, gla_w_alpha, gla_b_alpha, gla_norm, attn_q_norm, attn_k_norm, w_branch_s5, w_branch_gla, w_branch_attn, w_merge_gate, b_merge_gate, w_out, ffn2_norm, ffn2_w_gate, ffn2_w_up, ffn2_w_down, final_norm, loss_target, m_ffn1_norm, m_ffn1_w_gate, m_ffn1_w_up, m_ffn1_w_down, m_mix_norm, m_w_in, m_s5_lambda_re, m_s5_lambda_im, m_s5_log_dt, m_s5_b_re, m_s5_b_im, m_s5_c_re, m_s5_c_im, m_s5_d, m_s5_w_glu, m_gla_w_alpha, m_gla_b_alpha, m_gla_norm, m_attn_q_norm, m_attn_k_norm, m_w_branch_s5, m_w_branch_gla, m_w_branch_attn, m_w_merge_gate, m_b_merge_gate, m_w_out, m_ffn2_norm, m_ffn2_w_gate, m_ffn2_w_up, m_ffn2_w_down, m_final_norm, v_ffn1_norm, v_ffn1_w_gate, v_ffn1_w_up, v_ffn1_w_down, v_mix_norm, v_w_in, v_s5_lambda_re, v_s5_lambda_im, v_s5_log_dt, v_s5_b_re, v_s5_b_im, v_s5_c_re, v_s5_c_im, v_s5_d, v_s5_w_glu, v_gla_w_alpha, v_gla_b_alpha, v_gla_norm, v_attn_q_norm, v_attn_k_norm, v_w_branch_s5, v_w_branch_gla, v_w_branch_attn, v_w_merge_gate, v_b_merge_gate, v_w_out, v_ffn2_norm, v_ffn2_w_gate, v_ffn2_w_up, v_ffn2_w_down, v_final_norm):
    return _train_step(x, ffn1_norm, ffn1_w_gate, ffn1_w_up, ffn1_w_down, mix_norm, w_in, s5_lambda_re, s5_lambda_im, s5_log_dt, s5_b_re, s5_b_im, s5_c_re, s5_c_im, s5_d, s5_w_glu, gla_w_alpha, gla_b_alpha, gla_norm, attn_q_norm, attn_k_norm, w_branch_s5, w_branch_gla, w_branch_attn, w_merge_gate, b_merge_gate, w_out, ffn2_norm, ffn2_w_gate, ffn2_w_up, ffn2_w_down, final_norm, loss_target, m_ffn1_norm, m_ffn1_w_gate, m_ffn1_w_up, m_ffn1_w_down, m_mix_norm, m_w_in, m_s5_lambda_re, m_s5_lambda_im, m_s5_log_dt, m_s5_b_re, m_s5_b_im, m_s5_c_re, m_s5_c_im, m_s5_d, m_s5_w_glu, m_gla_w_alpha, m_gla_b_alpha, m_gla_norm, m_attn_q_norm, m_attn_k_norm, m_w_branch_s5, m_w_branch_gla, m_w_branch_attn, m_w_merge_gate, m_b_merge_gate, m_w_out, m_ffn2_norm, m_ffn2_w_gate, m_ffn2_w_up, m_ffn2_w_down, m_final_norm, v_ffn1_norm, v_ffn1_w_gate, v_ffn1_w_up, v_ffn1_w_down, v_mix_norm, v_w_in, v_s5_lambda_re, v_s5_lambda_im, v_s5_log_dt, v_s5_b_re, v_s5_b_im, v_s5_c_re, v_s5_c_im, v_s5_d, v_s5_w_glu, v_gla_w_alpha, v_gla_b_alpha, v_gla_norm, v_attn_q_norm, v_attn_k_norm, v_w_branch_s5, v_w_branch_gla, v_w_branch_attn, v_w_merge_gate, v_b_merge_gate, v_w_out, v_ffn2_norm, v_ffn2_w_gate, v_ffn2_w_up, v_ffn2_w_down, v_final_norm)


def _train_step(*args):
    nw = len(W_NAMES)
    x, target = args[0][0], args[1 + nw][0]
    w = dict(zip(W_NAMES, args[1:1 + nw]))
    m = dict(zip(W_NAMES, args[2 + nw:2 + 2 * nw]))
    v = dict(zip(W_NAMES, args[2 + 2 * nw:2 + 3 * nw]))

    loss, dx, per_layer, stacked, incoming = _step_local(x, target, w, {n: w[n].astype(BF16) for n in SHARDED})
    loss = lax.psum(loss, ("x", "y", "c"))

    out = {}
    kinds = ('grad', 'delta', 'new_m', 'new_v')
    for n in SHARDED:
        shape = w[n].shape
        flat = lambda a: a.reshape(-1, shape[-1])
        parts = [incoming[i][n].reshape(incoming[i][n].shape[0], -1, shape[-1]) for i in range(DEPTH)]
        res = _reduce_adamw(parts, flat(w[n]), flat(m[n]), flat(v[n]), name="adamw_sharded")
        for kind, a in zip(kinds, res):
            out[kind + '_' + n] = a.reshape(shape)
    small = {n: stacked[n] if n in stacked else jnp.stack([g[n] for g in per_layer]) for n in REPLICATED}
    groups = [([n for n in REPLICATED if n not in S5_MATRICES], F32), (list(S5_MATRICES), BF16)]
    gathered = _all_gather([_pack_small([small[n] for n in names]).astype(dt) for names, dt in groups], name="gather_small_grads")
    for (names, _), parts in zip(groups, gathered):
        res = _reduce_adamw([parts], *[_pack_small([d[n] for n in names]) for d in (w, m, v)], name="adamw_replicated")
        for kind, packed in zip(kinds, res):
            for n, a in zip(names, _unpack_small(packed, [w[n].shape for n in names])):
                out[kind + '_' + n] = a
    return (loss, dx[None]) + tuple(out[kind + '_' + n] for kind in kinds for n in W_NAMES)
```

```python
import functools
import math

import jax
import jax.numpy as jnp
import numpy as np
from jax import lax
from jax.experimental import pallas as pl
from jax.experimental.pallas import tpu as pltpu

F32 = jnp.float32
BF16 = jnp.bfloat16

N_DEV = 8
D_MODEL = 1024
DEPTH = 2
GRID_W = 64
D_FF = 2816
NORM_EPS = 1e-6
S5_GROUPS = 32
S5_GROUP_CH = 16
S5_STATE = 64
S5_WIDTH = 512
S5_NSTATE = S5_GROUPS * S5_STATE
S5_LANE_BLOCK = 512
GLA_HEADS = 4
GLA_HEAD_DIM = 128
GLA_WIDTH = 512
GLA_LOWRANK = 16
GLA_TAU = 16.0
GLA_CHUNK = 64
ATTN_Q_HEADS = 8
ATTN_KV_HEADS = 2
ATTN_HEAD_DIM = 64
ATTN_WIDTH = 512
ATTN_KV_WIDTH = 128
ROPE_BASE = 10000.0
IN_SPLITS = (512, 512, 512, 512, 512, 16, 16, 512, 128, 128)
IN_WIDTH = sum(IN_SPLITS)
IN_PAD = 3584
GATE_WIDTH = 3 * D_MODEL
PG_WIDTH = GATE_WIDTH + IN_PAD
P_OFF = GATE_WIDTH
CB_U, CB_GQ, CB_GK, CB_GV, CB_GG, CB_AQ = (P_OFF // 512 + i for i in range(6))
CB_AK, CB_AV, CB_Z = (P_OFF + 3072) // 128, (P_OFF + 3200) // 128, (P_OFF + 3328) // 128
ADAM_LR = 0.001
ADAM_B1 = 0.9
ADAM_B2 = 0.999
ADAM_EPS = 1e-08
ADAM_WD = 0.01
ADAM_STEP = 10

W_NAMES = ['ffn1_norm', 'ffn1_w_gate', 'ffn1_w_up', 'ffn1_w_down', 'mix_norm', 'w_in', 's5_lambda_re', 's5_lambda_im',
           's5_log_dt', 's5_b_re', 's5_b_im', 's5_c_re', 's5_c_im', 's5_d', 's5_w_glu', 'gla_w_alpha', 'gla_b_alpha',
           'gla_norm', 'attn_q_norm', 'attn_k_norm', 'w_branch_s5', 'w_branch_gla', 'w_branch_attn', 'w_merge_gate',
           'b_merge_gate', 'w_out', 'ffn2_norm', 'ffn2_w_gate', 'ffn2_w_up', 'ffn2_w_down', 'final_norm']
SHARD_AXIS = {'ffn1_w_gate': 2, 'ffn1_w_up': 2, 'ffn1_w_down': 1, 'w_in': 2, 's5_w_glu': 1, 'gla_w_alpha': 3,
              'gla_b_alpha': 2, 'w_branch_s5': 2, 'w_branch_gla': 2, 'w_branch_attn': 2, 'w_merge_gate': 2,
              'w_out': 1, 'ffn2_w_gate': 2, 'ffn2_w_up': 2, 'ffn2_w_down': 1}
SHARDED = [n for n in W_NAMES if n in SHARD_AXIS]
REPLICATED = [n for n in W_NAMES if n not in SHARD_AXIS]


def _pick(dim, prefs):
    for p in prefs:
        if dim % p == 0:
            return p
    return dim


def _sigmoid(x):
    return 0.5 * jnp.tanh(0.5 * x) + 0.5


def _mm(a, b, *, ta=False, tb=False, out_dtype=F32, scale=None, add=None, side=None, name):
    a, a_cb, a_w = a if isinstance(a, tuple) else (a, 0, a.shape[1])
    b, b_cb, b_w = b if isinstance(b, tuple) else (b, 0, b.shape[1])
    m, k = (a_w, a.shape[0]) if ta else (a.shape[0], a_w)
    n = b.shape[0] if tb else b_w
    assert (b_w if tb else b.shape[0]) == k, (a.shape, b.shape, ta, tb)
    tm, tn, tk = _mm_tiles(m, n, k, a.dtype.itemsize, b.dtype.itemsize, jnp.dtype(out_dtype).itemsize)
    nk = k // tk
    dims = (((0 if ta else 1,), (1 if tb else 0,)), ((), ()))
    a_off = a_cb * (a_w // (tm if ta else tk))
    b_off = b_cb * (b_w // (tk if tb else tn))

    grid = (m // tm, n // tn, nk)
    n_in = 2 if add is None else 3

    def body(*refs):
        _carried(side, grid, refs, n_in, 1, int(nk > 1), compute)

    def compute(refs):
        a_ref, b_ref, *rest = refs
        add_ref = rest[0] if add is not None else None
        o_ref, *acc = rest[1:] if add is not None else rest

        def finish(res):
            res = res if scale is None else res * scale
            return (res if add_ref is None else res + add_ref[...]).astype(out_dtype)

        part = lax.dot_general(a_ref[...].astype(BF16), b_ref[...].astype(BF16), dims, preferred_element_type=F32)
        if nk == 1:
            o_ref[...] = finish(part)
            return
        acc_ref, = acc
        kk = pl.program_id(2)

        @pl.when(kk == 0)
        def _():
            acc_ref[...] = part

        @pl.when(kk > 0)
        def _():
            acc_ref[...] += part

        @pl.when(kk == nk - 1)
        def _():
            o_ref[...] = finish(acc_ref[...])

    a_spec = (pl.BlockSpec((tk, tm), lambda i, j, kk: (kk, i + a_off)) if ta
              else pl.BlockSpec((tm, tk), lambda i, j, kk: (i, kk + a_off)))
    b_spec = (pl.BlockSpec((tn, tk), lambda i, j, kk: (j, kk + b_off)) if tb
              else pl.BlockSpec((tk, tn), lambda i, j, kk: (kk, j + b_off)))
    o_spec = pl.BlockSpec((tm, tn), lambda i, j, kk: (i, j))
    (out,), gathered = _side_call(
        body, side, name=name, grid=grid, in_specs=[a_spec, b_spec] + ([o_spec] if add is not None else []),
        out_specs=[o_spec], out_shape=[jax.ShapeDtypeStruct((m, n), out_dtype)],
        scratch=[pltpu.VMEM((tm, tn), F32)] if nk > 1 else [], args=[a, b] + ([add] if add is not None else []),
        semantics=("parallel", "parallel", "arbitrary"))
    return out if side is None else (out, gathered)


MM_VMEM_BUDGET = 40 * 1024 * 1024


def _mm_tiles(m, n, k, a_bytes, b_bytes, out_bytes):
    tms = [t for t in (1024, 1408, 512, 256, 128) if m % t == 0] or [m]
    tns = [t for t in (1664, 1408, 512, 256, 128) if n % t == 0] or [n]
    tks = [k] + [t for t in (2048, 1024, 512, 256, 128) if k % t == 0 and t < k]
    for tk in tks:
        for tm in tms:
            for tn in tns:
                use = 2 * (tm * tk * a_bytes + tk * tn * b_bytes + tm * tn * out_bytes) + 2 * tm * tn * 4
                if use <= MM_VMEM_BUDGET:
                    return tm, tn, tk
    return tms[-1], tns[-1], tks[-1]


def _rowmap(fn, rows, consts, outs, reds=(), *, tl, name):
    rows = [r if isinstance(r, tuple) else (r, 0, r.shape[1]) for r in rows]
    length = rows[0][0].shape[0]
    tl = min(tl, length)
    nr, nc, no = len(rows), len(consts), len(outs)

    def body(*refs):
        res = fn(*[r[...] for r in refs[:nr + nc]])
        res = res if isinstance(res, tuple) else (res,)
        for o_ref, val in zip(refs[nr + nc:nr + nc + no], res[:no]):
            o_ref[...] = val.astype(o_ref.dtype)
        if reds:
            step = pl.program_id(0)
            red_refs = refs[nr + nc + no:]

            @pl.when(step == 0)
            def _():
                for d_ref, val in zip(red_refs, res[no:]):
                    d_ref[...] = val.astype(F32)

            @pl.when(step > 0)
            def _():
                for d_ref, val in zip(red_refs, res[no:]):
                    d_ref[...] += val.astype(F32)

    in_specs = [pl.BlockSpec((tl, w), lambda i, cb=cb: (i, cb)) for (_, cb, w) in rows]
    in_specs += [pl.BlockSpec(c.shape, lambda i, nd=c.ndim: (0,) * nd) for c in consts]
    out_specs = [pl.BlockSpec((tl, w), lambda i: (i, 0)) for (w, _) in outs]
    out_specs += [pl.BlockSpec(s, lambda i, nd=len(s): (0,) * nd) for s in reds]
    out_shape = [jax.ShapeDtypeStruct((length, w), dt) for (w, dt) in outs]
    out_shape += [jax.ShapeDtypeStruct(s, F32) for s in reds]
    res = pl.pallas_call(
        body, name=name, grid=(length // tl,), in_specs=in_specs, out_specs=out_specs, out_shape=out_shape,
        compiler_params=pltpu.CompilerParams(dimension_semantics=("arbitrary" if reds else "parallel",)),
    )(*[r[0] for r in rows], *consts)
    return res


def _rms(x):
    return lax.rsqrt(jnp.mean(x * x, axis=-1, keepdims=True) + NORM_EPS)


def _rmsnorm_fwd(x, gain):
    def fn(xv, g):
        return xv * _rms(xv) * g
    return _rowmap(fn, [x], [gain.reshape(1, -1)], [(x.shape[1], BF16)], tl=256, name="rmsnorm_fwd")[0]


def _rmsnorm_bwd(x, gain, dh, dres):
    def fn(xv, dhv, drv, g):
        r = _rms(xv)
        gd = dhv * g
        dx = r * gd - xv * (r * r * r) * jnp.mean(xv * gd, axis=-1, keepdims=True)
        return drv + dx, jnp.sum(dhv * xv * r, axis=0, keepdims=True)
    dx, dg = _rowmap(fn, [x, dh, dres], [gain.reshape(1, -1)], [(x.shape[1], F32)], [(1, x.shape[1])], tl=256,
                     name="rmsnorm_bwd")
    return dx, dg[0]


FFN_UNIT = D_FF // 2


def _side_call(body, side, *, name, grid, in_specs, out_specs, out_shape, scratch, args, semantics):
    if side is not None:
        in_specs, out_specs = in_specs + side.in_specs, out_specs + side.out_specs
        out_shape, scratch, args = out_shape + side.out_shape, scratch + side.scratch, list(args) + side.blocks
        semantics = ("arbitrary",) * len(grid)
    res = pl.pallas_call(body, name=name, grid=grid, in_specs=in_specs, out_specs=out_specs, out_shape=out_shape,
                         scratch_shapes=scratch, compiler_params=pltpu.CompilerParams(dimension_semantics=semantics))(*args)
    n_own = len(res) - (side.n if side is not None else 0)
    return res[:n_own], res[n_own:]


def _ffn_up(h, w_gate, w_up, side=None):
    length, k = h.shape
    tm = _pick(length, (512, 256, 128))
    grid = (D_FF // FFN_UNIT, length // tm)

    def compute(refs):
        h_ref, wg_ref, wu_ref, a_ref, g_ref, u_ref = refs
        hv = h_ref[...]
        g = jnp.dot(hv, wg_ref[...], preferred_element_type=F32)
        u = jnp.dot(hv, wu_ref[...], preferred_element_type=F32)
        a_ref[...] = (g * _sigmoid(g) * u).astype(BF16)
        g_ref[...] = g.astype(BF16)
        u_ref[...] = u.astype(BF16)

    def body(*refs):
        _carried(side, grid, refs, 3, 3, 0, compute)

    w_spec = pl.BlockSpec((k, FFN_UNIT), lambda j, i: (0, j))
    o_spec = pl.BlockSpec((tm, FFN_UNIT), lambda j, i: (i, j))
    return _side_call(
        body, side, name="ffn_up", grid=grid, in_specs=[pl.BlockSpec((tm, k), lambda j, i: (i, 0)), w_spec, w_spec],
        out_specs=[o_spec] * 3, out_shape=[jax.ShapeDtypeStruct((length, D_FF), BF16)] * 3, scratch=[],
        args=[h, w_gate, w_up], semantics=("parallel", "parallel"))


def _ffn_dgu(dxo, w_down, g, u):
    length, k = dxo.shape
    tm = _pick(length, (512, 256, 128))

    def body(d_ref, w_ref, g_ref, u_ref, dg_ref, du_ref):
        da = 0.5 * lax.dot_general(d_ref[...], w_ref[...], _NT, preferred_element_type=F32)
        gv = g_ref[...].astype(F32)
        s = _sigmoid(gv)
        dg_ref[...] = (da * u_ref[...].astype(F32) * (s * (1.0 + gv * (1.0 - s)))).astype(BF16)
        du_ref[...] = (da * (gv * s)).astype(BF16)

    o_spec = pl.BlockSpec((tm, FFN_UNIT), lambda j, i: (i, j))
    return pl.pallas_call(
        body, name="ffn_dgu", grid=(D_FF // FFN_UNIT, length // tm),
        in_specs=[pl.BlockSpec((tm, k), lambda j, i: (i, 0)), pl.BlockSpec((FFN_UNIT, k), lambda j, i: (j, 0)), o_spec, o_spec],
        out_specs=[o_spec] * 2, out_shape=[jax.ShapeDtypeStruct((length, D_FF), BF16)] * 2,
        compiler_params=pltpu.CompilerParams(dimension_semantics=("parallel", "parallel")),
    )(dxo, w_down, g, u)


def _ffn_fwd(x, gain, w_gate, w_up, w_down, side=None):
    h = _rmsnorm_fwd(x, gain)
    (a, g, u), gathered = _ffn_up(h, w_gate, w_up, side)
    x_out = _mm(a, w_down, scale=0.5, add=x, name="ffn_down")
    return x_out, (x, h, g, u, a), gathered


def _ffn_bwd(saved, gain, w_gate, w_up, w_down, dx_out):
    x, h, g, u, a = saved
    dxo = dx_out.astype(BF16)
    d_wdown = _mm(a, dxo, ta=True, scale=0.5, out_dtype=BF16, name="ffn_dwdown")
    dg, du = _ffn_dgu(dxo, w_down, g, u)
    d_wgate = _mm(h, dg, ta=True, out_dtype=BF16, name="ffn_dwgu")
    d_wup = _mm(h, du, ta=True, out_dtype=BF16, name="ffn_dwgu")
    dh = _mm(du, w_up, tb=True, add=_mm(dg, w_gate, tb=True, name="ffn_dh"), name="ffn_dh_add")
    dx, dgain = _rmsnorm_bwd(x, gain, dh, dx_out)
    return dx, dgain, d_wgate, d_wup, d_wdown


def _s5_blocked(re, im):
    lead = re.shape[:-1]
    nb = S5_NSTATE // S5_LANE_BLOCK
    both = jnp.stack([re.reshape(*lead, nb, S5_LANE_BLOCK), im.reshape(*lead, nb, S5_LANE_BLOCK)], axis=-2)
    return both.reshape(*lead, 2 * S5_NSTATE)


def _s5_unblocked(z):
    lead = z.shape[:-1]
    nb = S5_NSTATE // S5_LANE_BLOCK
    both = z.reshape(*lead, nb, 2, S5_LANE_BLOCK)
    return both[..., 0, :].reshape(*lead, S5_NSTATE), both[..., 1, :].reshape(*lead, S5_NSTATE)


def _s5_tables(a_re, a_im, reverse):
    a = lax.complex(a_re, a_im)
    a2 = a * a
    a4 = a2 * a2
    rows = jnp.arange(8)
    pw = [a]
    for _ in range(7):
        pw.append(pw[-1] * a)
    pw = jnp.stack(pw)
    if reverse:
        pw = pw[::-1]
    tabs = []
    for coef, s in ((a, 1), (a2, 2), (a4, 4)):
        live = (rows <= 7 - s) if reverse else (rows >= s)
        tabs.append(jnp.where(live[:, None], coef[None, :], 0.0))
    tabs.append(pw)
    tabs = jnp.stack(tabs)
    return _s5_blocked(jnp.real(tabs), jnp.imag(tabs))


def _s5_scan_tile(v, tab_ref, prev, reverse):
    lb = S5_LANE_BLOCK
    vr, vi = v[:, :lb], v[:, lb:]
    for idx, s in enumerate((1, 2, 4)):
        cr, ci = tab_ref[idx, :, :lb], tab_ref[idx, :, lb:]
        sh = 8 - s if reverse else s
        sr, si = pltpu.roll(vr, sh, 0), pltpu.roll(vi, sh, 0)
        vr, vi = vr + cr * sr - ci * si, vi + cr * si + ci * sr
    row = 0 if reverse else 7
    pr = jnp.broadcast_to(prev[row:row + 1, :lb], (8, lb))
    pi = jnp.broadcast_to(prev[row:row + 1, lb:], (8, lb))
    cr, ci = tab_ref[3, :, :lb], tab_ref[3, :, lb:]
    return jnp.concatenate([vr + cr * pr - ci * pi, vi + cr * pi + ci * pr], axis=1)


def _s5_prep(lam_re, lam_im, log_dt, b_re, b_im):
    lam = lax.complex(lam_re, lam_im)
    dt = jnp.exp(log_dt)[:, None]
    lam_bar = jnp.exp(lam * dt)
    b_bar = ((lam_bar - 1.0) / lam)[..., None] * lax.complex(b_re, b_im)
    return (jnp.real(lam_bar).reshape(-1), jnp.imag(lam_bar).reshape(-1), jnp.real(b_bar), jnp.imag(b_bar))


S5_NBLK = S5_NSTATE // S5_LANE_BLOCK
S5_BLK_GROUPS = S5_GROUPS // S5_NBLK
S5_BLK_CH = S5_BLK_GROUPS * S5_GROUP_CH


def _s5_in_matrix(bb_re, bb_im):
    eye = jnp.eye(S5_BLK_GROUPS, dtype=F32)
    def dense(bb):
        b4 = bb.reshape(S5_NBLK, S5_BLK_GROUPS, S5_STATE, S5_GROUP_CH)
        return jnp.einsum('cgph,gk->cghkp', b4, eye).reshape(S5_NBLK, S5_BLK_CH, S5_LANE_BLOCK)
    return jnp.concatenate([dense(bb_re), dense(bb_im)], axis=-1)


def _s5_block_diagonal(d):
    d5 = d.reshape(S5_NBLK, S5_BLK_GROUPS, S5_GROUP_CH, S5_BLK_GROUPS, S5_STATE)
    eye = jnp.eye(S5_BLK_GROUPS, dtype=F32)
    return jnp.swapaxes(jnp.sum(d5 * eye[None, :, None, :, None], axis=1), 1, 2)


def _s5_in_matrix_grad(d_mat):
    def diag(d):
        return jnp.swapaxes(_s5_block_diagonal(d), 2, 3).reshape(S5_GROUPS, S5_STATE, S5_GROUP_CH)
    return diag(d_mat[..., :S5_LANE_BLOCK]), diag(d_mat[..., S5_LANE_BLOCK:])


def _s5_out_matrix(c_re, c_im):
    eye = jnp.eye(S5_BLK_GROUPS, dtype=F32)
    def dense(cc):
        c4 = cc.reshape(S5_NBLK, S5_BLK_GROUPS, S5_GROUP_CH, S5_STATE)
        return jnp.einsum('cghp,gk->cgpkh', c4, eye).reshape(S5_NBLK, S5_LANE_BLOCK, S5_BLK_CH)
    return jnp.concatenate([dense(c_re), dense(-c_im)], axis=1)


def _s5_out_matrix_grad(d_mat_t):
    def diag(d):
        return _s5_block_diagonal(d).reshape(S5_GROUPS, S5_GROUP_CH, S5_STATE)
    return diag(d_mat_t[..., :S5_LANE_BLOCK]), -diag(d_mat_t[..., S5_LANE_BLOCK:])


def _gelu_parts(x):
    k = math.sqrt(2.0 / math.pi)
    inner = k * (x + 0.044715 * x * x * x)
    th = jnp.tanh(inner)
    return th, k * (1.0 + 3.0 * 0.044715 * x * x)


S5_CB_U = CB_U * (512 // S5_BLK_CH)


def _s5_direction_fwd(pg, b_mat, c_mat, tabs, reverse, *, name, side=None):
    length = pg.shape[0]
    tb = min(512, length)
    ntb = length // tb
    wb = 2 * S5_LANE_BLOCK
    ntile = tb // 8
    grid = (S5_NBLK, ntb)

    def body(*refs):
        _carried(side, grid, refs, 4, 3, 2, compute)

    def compute(refs):
        tab_ref, u_ref, b_ref, c_ref, x_ref, y_ref, ends_ref, carry_ref, bu_ref = refs

        @pl.when(pl.program_id(1) == 0)
        def _():
            carry_ref[...] = jnp.zeros_like(carry_ref)

        bu_ref[...] = jnp.dot(u_ref[...].astype(BF16), b_ref[0], preferred_element_type=F32)

        def step(i, prev):
            r0 = pl.multiple_of((ntile - 1 - i if reverse else i) * 8, 8)
            x = _s5_scan_tile(bu_ref[pl.ds(r0, 8), :], tab_ref, prev, reverse)
            x_ref[pl.ds(r0, 8), :] = x
            return x

        carry_ref[...] = lax.fori_loop(0, ntile, step, carry_ref[...])
        y_ref[...] = jnp.dot(x_ref[...].astype(BF16), c_ref[0], preferred_element_type=F32)
        ends_ref[0, 0:8, :] = x_ref[0:8, :]
        ends_ref[0, 8:16, :] = x_ref[tb - 8:tb, :]

    tix = (lambda t: ntb - 1 - t) if reverse else (lambda t: t)
    (xs, y, ends), gathered = _side_call(
        body, side, name=name, grid=grid,
        in_specs=[pl.BlockSpec((4, 8, wb), lambda c, t: (0, 0, c)),
                  pl.BlockSpec((tb, S5_BLK_CH), lambda c, t: (tix(t), S5_CB_U + c)),
                  pl.BlockSpec((1, S5_BLK_CH, wb), lambda c, t: (c, 0, 0)),
                  pl.BlockSpec((1, wb, S5_BLK_CH), lambda c, t: (c, 0, 0))],
        out_specs=[pl.BlockSpec((tb, wb), lambda c, t: (tix(t), c)), pl.BlockSpec((tb, S5_BLK_CH), lambda c, t: (tix(t), c)),
                   pl.BlockSpec((1, 16, wb), lambda c, t: (tix(t), 0, c))],
        out_shape=[jax.ShapeDtypeStruct((length, S5_NBLK * wb), F32), jax.ShapeDtypeStruct((length, S5_WIDTH), F32),
                   jax.ShapeDtypeStruct((ntb, 16, S5_NBLK * wb), F32)],
        scratch=[pltpu.VMEM((8, wb), F32), pltpu.VMEM((tb, wb), F32)], args=[tabs, pg, b_mat, c_mat],
        semantics=("parallel", "arbitrary"))
    return xs, y, ends, gathered


def _s5_direction_bwd(pg, dy, xs, ends, b_mat, c_mat, tabs_conj, reverse, *, name):
    length = pg.shape[0]
    tb = min(512, length)
    ntb = length // tb
    lb = S5_LANE_BLOCK
    wb = 2 * lb
    ntile = tb // 8
    adj_rev = not reverse
    if reverse:
        edge = jnp.concatenate([ends[1:, 0], jnp.zeros((1, xs.shape[1]), F32)], axis=0)
    else:
        edge = jnp.concatenate([jnp.zeros((1, xs.shape[1]), F32), ends[:-1, 15]], axis=0)
    edge = edge.reshape(ntb, 1, xs.shape[1])

    def body(tab_ref, u_ref, dy_ref, x_ref, edge_ref, b_ref, c_ref, du_ref, db_ref, dc_ref, da_ref, carry_ref, g_ref, lam_ref):
        @pl.when(pl.program_id(1) == 0)
        def _():
            carry_ref[...] = jnp.zeros_like(carry_ref)
            da_ref[...] = jnp.zeros_like(da_ref)
            db_ref[...] = jnp.zeros_like(db_ref)
            dc_ref[...] = jnp.zeros_like(dc_ref)

        dyb = dy_ref[...].astype(BF16)
        g_ref[...] = lax.dot_general(dyb, c_ref[0], _NT, preferred_element_type=F32)
        rows = lax.broadcasted_iota(jnp.int32, (8, wb), 0)

        def step(i, carry):
            prev, acc = carry
            k = ntile - 1 - i if adj_rev else i
            r0 = pl.multiple_of(k * 8, 8)
            lam = _s5_scan_tile(g_ref[pl.ds(r0, 8), :], tab_ref, prev, adj_rev)
            lam_ref[pl.ds(r0, 8), :] = lam
            x = x_ref[pl.ds(r0, 8), :]
            if reverse:
                kn = jnp.minimum(k + 1, ntile - 1)
                nb = x_ref[pl.ds(pl.multiple_of(kn * 8, 8), 8), :][0:1, :]
                nb = jnp.where(k == ntile - 1, edge_ref[0], nb)
                xp = jnp.where(rows == 7, jnp.broadcast_to(nb, (8, wb)), pltpu.roll(x, 7, 0))
            else:
                kn = jnp.maximum(k - 1, 0)
                nb = x_ref[pl.ds(pl.multiple_of(kn * 8, 8), 8), :][7:8, :]
                nb = jnp.where(k == 0, edge_ref[0], nb)
                xp = jnp.where(rows == 0, jnp.broadcast_to(nb, (8, wb)), pltpu.roll(x, 1, 0))
            xr, xi, lr, li = xp[:, :lb], xp[:, lb:], lam[:, :lb], lam[:, lb:]
            return lam, acc + jnp.concatenate([xr * lr + xi * li, xr * li - xi * lr], axis=1)

        last, acc = lax.fori_loop(0, ntile, step, (carry_ref[...], da_ref[...]))
        carry_ref[...] = last
        da_ref[...] = acc
        lamb = lam_ref[...].astype(BF16)
        du_ref[...] = lax.dot_general(lamb, b_ref[0], _NT, preferred_element_type=F32)
        db_ref[0] += lax.dot_general(u_ref[...].astype(BF16), lamb, _TN, preferred_element_type=F32)
        dc_ref[0] += lax.dot_general(dyb, x_ref[...].astype(BF16), _TN, preferred_element_type=F32)

    tix = (lambda t: ntb - 1 - t) if adj_rev else (lambda t: t)
    wide = pl.BlockSpec((tb, wb), lambda c, t: (tix(t), c))
    mat = pl.BlockSpec((1, S5_BLK_CH, wb), lambda c, t: (c, 0, 0))
    return pl.pallas_call(
        body, name=name, grid=(S5_NBLK, ntb),
        in_specs=[pl.BlockSpec((4, 8, wb), lambda c, t: (0, 0, c)),
                  pl.BlockSpec((tb, S5_BLK_CH), lambda c, t: (tix(t), S5_CB_U + c)),
                  pl.BlockSpec((tb, S5_BLK_CH), lambda c, t: (tix(t), c)), wide,
                  pl.BlockSpec((1, 1, wb), lambda c, t: (tix(t), 0, c)), mat,
                  pl.BlockSpec((1, wb, S5_BLK_CH), lambda c, t: (c, 0, 0))],
        out_specs=[pl.BlockSpec((tb, S5_BLK_CH), lambda c, t: (tix(t), c)), mat, mat, pl.BlockSpec((8, wb), lambda c, t: (0, c))],
        out_shape=[jax.ShapeDtypeStruct((length, S5_WIDTH), F32), jax.ShapeDtypeStruct((S5_NBLK, S5_BLK_CH, wb), F32),
                   jax.ShapeDtypeStruct((S5_NBLK, S5_BLK_CH, wb), F32), jax.ShapeDtypeStruct((8, S5_NBLK * wb), F32)],
        scratch_shapes=[pltpu.VMEM((8, wb), F32), pltpu.VMEM((tb, wb), F32), pltpu.VMEM((tb, wb), F32)],
        compiler_params=pltpu.CompilerParams(dimension_semantics=("parallel", "arbitrary")),
    )(tabs_conj, pg, dy, xs, edge, b_mat, c_mat)


def _both(fn):
    return jax.vmap(jax.vmap(fn))


def _s5_setup(w):
    a_re, a_im, bb_re, bb_im = _both(_s5_prep)(w['s5_lambda_re'], w['s5_lambda_im'], w['s5_log_dt'], w['s5_b_re'], w['s5_b_im'])

    def tables(d, conj, reverse):
        return jax.vmap(lambda r, i: _s5_tables(r, -i if conj else i, reverse))(a_re[:, d], a_im[:, d])
    return {'b_mat': _both(_s5_in_matrix)(bb_re, bb_im).astype(BF16),
            'c_mat': _both(_s5_out_matrix)(w['s5_c_re'], w['s5_c_im']).astype(BF16),
            'tabs': [tables(0, False, False), tables(1, False, True)],
            'tabs_adj': [tables(0, True, True), tables(1, True, False)]}


def _s5_param_grads(w, raws):
    def stacked(k):
        return jnp.stack([jnp.stack([raws[i][d][k] for d in range(2)]) for i in range(DEPTH)])
    dbb_re, dbb_im = _both(_s5_in_matrix_grad)(stacked(0))
    dc_re, dc_im = _both(_s5_out_matrix_grad)(stacked(1))
    da_re, da_im = _s5_unblocked(jnp.sum(stacked(2), axis=2))
    _, vjp = jax.vjp(_both(_s5_prep), w['s5_lambda_re'], w['s5_lambda_im'], w['s5_log_dt'], w['s5_b_re'], w['s5_b_im'])
    g = vjp((da_re, da_im, dbb_re, dbb_im))
    return {'s5_lambda_re': g[0], 's5_lambda_im': g[1], 's5_log_dt': g[2], 's5_b_re': g[3], 's5_b_im': g[4],
            's5_c_re': dc_re, 's5_c_im': dc_im}


def _s5_fwd(p_in, prm, w_glu, side=None):
    dirs = []
    ys = []
    gathered = []
    for d, reverse in ((0, False), (1, True)):
        xs, y_dir, ends, got = _s5_direction_fwd(p_in, prm['b_mat'][d], prm['c_mat'][d], prm['tabs'][d], reverse,
                                                 name="s5_fwd_rev" if reverse else "s5_fwd", side=None if reverse else side)
        gathered += got
        ys.append(y_dir)
        dirs.append((xs, ends))

    def post(yf, yb, u, dskip):
        ypre = yf + yb + dskip * u
        th, _ = _gelu_parts(ypre)
        return ypre, 0.5 * ypre * (1.0 + th)
    ypre, yg = _rowmap(post, [ys[0], ys[1], (p_in, CB_U, S5_WIDTH)], [prm['d'].reshape(1, -1)],
                       [(S5_WIDTH, F32), (S5_WIDTH, F32)], tl=512, name="s5_post")
    t = _mm(yg, w_glu, name="s5_glu_mm")

    def glu(ygv, tv):
        return ygv * _sigmoid(tv)
    y = _rowmap(glu, [yg, t], [], [(S5_WIDTH, BF16)], tl=512, name="s5_glu")[0]
    return y, (dirs, ypre, yg, t), gathered


def _s5_bwd(pg, prm, w_glu, saved, dy):
    dirs, ypre, yg, t = saved

    def glu_bwd(dyv, ygv, tv):
        s = _sigmoid(tv)
        return dyv * ygv * s * (1.0 - s), dyv * s
    dt, dyg_direct = _rowmap(glu_bwd, [dy, yg, t], [], [(S5_WIDTH, BF16), (S5_WIDTH, F32)], tl=512, name="s5_glu_bwd")
    grads = {'w_glu': _mm(yg, dt, ta=True, out_dtype=BF16, name="s5_dwglu")}
    dyg_mm = _mm(dt, w_glu, tb=True, name="s5_dyg")

    def post_bwd(dyd, dym, yp, u, dskip):
        th, dinner = _gelu_parts(yp)
        dyp = (dyd + dym) * (0.5 * (1.0 + th) + 0.5 * yp * (1.0 - th * th) * dinner)
        return dyp, dyp * dskip, jnp.sum(dyp * u, axis=0, keepdims=True)
    dyp, du_skip, dd = _rowmap(post_bwd, [dyg_direct, dyg_mm, ypre, (pg, CB_U, S5_WIDTH)], [prm['d'].reshape(1, -1)],
                               [(S5_WIDTH, F32), (S5_WIDTH, F32)], [(1, S5_WIDTH)], tl=512, name="s5_post_bwd")
    grads['d'] = dd[0]
    du = [du_skip]
    grads['raw'] = []
    for d, reverse in ((0, False), (1, True)):
        du_dir, d_bmat, d_cmat_t, da = _s5_direction_bwd(pg, dyp, *dirs[d], prm['b_mat'][d], prm['c_mat'][d], prm['tabs_adj'][d],
                                                         reverse, name="s5_bwd_rev" if reverse else "s5_bwd")
        du.append(du_dir)
        grads['raw'].append((d_bmat, d_cmat_t, da))
    return du, grads


def _split3(x):
    hi = x.astype(BF16)
    r = x - hi.astype(F32)
    mid = r.astype(BF16)
    return hi, mid, (r - mid.astype(F32)).astype(BF16)


def _exact_dot(ones, x, dims):
    parts = [lax.dot_general(ones, p, dims, preferred_element_type=F32) for p in _split3(x)]
    return parts[0] + parts[1] + parts[2]


_NN = (((1,), (0,)), ((), ()))
_NT = (((1,), (1,)), ((), ()))
_TN = (((0,), (0,)), ((), ()))


def _dot(a, b, dims=_NN):
    return lax.dot_general(a.astype(BF16), b.astype(BF16), dims, preferred_element_type=F32)


def _gla_chunk_mask(reverse):
    rows = lax.broadcasted_iota(jnp.int32, (GLA_CHUNK, GLA_CHUNK), 0)
    cols = lax.broadcasted_iota(jnp.int32, (GLA_CHUNK, GLA_CHUNK), 1)
    return (cols >= rows) if reverse else (cols <= rows)


def _gla_fwd(pg, la, reverse, *, name):
    length = la.shape[0]
    nch = length // GLA_CHUNK
    scale = GLA_HEAD_DIM ** -0.5
    last = 0 if reverse else GLA_CHUNK - 1
    hd = GLA_HEAD_DIM

    def body(q_ref, k_ref, v_ref, la_ref, o_ref, sp_ref, st_ref):
        @pl.when(pl.program_id(0) == 0)
        def _():
            st_ref[...] = jnp.zeros_like(st_ref)

        mask = _gla_chunk_mask(reverse)
        b = _exact_dot(mask.astype(BF16), la_ref[...], _NN)
        sp_ref[0] = st_ref[...]
        outs = []
        for h in range(GLA_HEADS):
            sl = slice(h * hd, (h + 1) * hd)
            bh = b[:, sl]
            bl = bh[last:last + 1, :]
            k = k_ref[:, sl]
            v = v_ref[:, sl]
            qd = q_ref[:, sl] * scale * jnp.exp(bh)
            kd = k * jnp.exp(-bh)
            ke = k * jnp.exp(bl - bh)
            st = st_ref[sl, :]
            p = jnp.where(mask, _dot(qd, kd, _NT), 0.0)
            outs.append(_dot(p, v) + _dot(qd, st, _NT))
            st_ref[sl, :] = st * jnp.exp(bl) + _dot(v, ke, _TN)
        o_ref[...] = jnp.concatenate(outs, axis=1)

    cmap = (lambda n: nch - 1 - n) if reverse else (lambda n: n)
    col = lambda cb: pl.BlockSpec((GLA_CHUNK, GLA_WIDTH), lambda n, cb=cb: (cmap(n), cb))
    return pl.pallas_call(
        body, name=name, grid=(nch,),
        in_specs=[col(CB_GQ), col(CB_GK), col(CB_GV), col(0)],
        out_specs=[col(0), pl.BlockSpec((1, GLA_WIDTH, hd), lambda n: (cmap(n), 0, 0))],
        out_shape=[jax.ShapeDtypeStruct((length, GLA_WIDTH), F32), jax.ShapeDtypeStruct((nch, GLA_WIDTH, hd), F32)],
        scratch_shapes=[pltpu.VMEM((GLA_WIDTH, hd), F32)],
        compiler_params=pltpu.CompilerParams(dimension_semantics=("arbitrary",)),
    )(pg, pg, pg, la)


def _gla_bwd(pg, la, do, sprev, reverse, *, name):
    length = la.shape[0]
    nch = length // GLA_CHUNK
    scale = GLA_HEAD_DIM ** -0.5
    last = 0 if reverse else GLA_CHUNK - 1
    hd = GLA_HEAD_DIM

    def body(q_ref, k_ref, v_ref, la_ref, do_ref, sp_ref, dq_ref, dk_ref, dv_ref, dla_ref, dst_ref):
        @pl.when(pl.program_id(0) == 0)
        def _():
            dst_ref[...] = jnp.zeros_like(dst_ref)

        mask = _gla_chunk_mask(reverse)
        tri = mask.astype(BF16)
        b = _exact_dot(tri, la_ref[...], _NN)
        is_last = lax.broadcasted_iota(jnp.int32, (GLA_CHUNK, hd), 0) == last
        dqs, dks, dvs, dbs = [], [], [], []
        for h in range(GLA_HEADS):
            sl = slice(h * hd, (h + 1) * hd)
            bh = b[:, sl]
            bl = bh[last:last + 1, :]
            eb, enb, ebl, el = jnp.exp(bh), jnp.exp(-bh), jnp.exp(bl - bh), jnp.exp(bl)
            k = k_ref[:, sl]
            v = v_ref[:, sl]
            dov = do_ref[:, sl]
            qd = q_ref[:, sl] * scale * eb
            kd = k * enb
            ke = k * ebl
            st = sp_ref[0, sl, :]
            dst = dst_ref[sl, :]
            p = jnp.where(mask, _dot(qd, kd, _NT), 0.0)
            dp = jnp.where(mask, _dot(dov, v, _NT), 0.0)
            dqd = _dot(dp, kd) + _dot(dov, st)
            dkd = _dot(dp, qd, _TN)
            dvs.append(_dot(p, dov, _TN) + _dot(ke, dst, _NT))
            dke = _dot(v, dst)
            dst_ref[sl, :] = dst * el + _dot(dov, qd, _TN)
            dbl = el * jnp.sum(dst * st, axis=0, keepdims=True) + jnp.sum(dke * ke, axis=0, keepdims=True)
            db = dqd * qd - dkd * kd - dke * ke
            dbs.append(jnp.where(is_last, db + dbl, db))
            dqs.append(dqd * eb * scale)
            dks.append(dkd * enb + dke * ebl)
        dq_ref[...] = jnp.concatenate(dqs, axis=1)
        dk_ref[...] = jnp.concatenate(dks, axis=1)
        dv_ref[...] = jnp.concatenate(dvs, axis=1)
        tri_t = _gla_chunk_mask(not reverse).astype(BF16)
        dla_ref[...] = _exact_dot(tri_t, jnp.concatenate(dbs, axis=1), _NN)

    cmap = (lambda n: n) if reverse else (lambda n: nch - 1 - n)
    col = lambda cb: pl.BlockSpec((GLA_CHUNK, GLA_WIDTH), lambda n, cb=cb: (cmap(n), cb))
    wide = jax.ShapeDtypeStruct((length, GLA_WIDTH), F32)
    return pl.pallas_call(
        body, name=name, grid=(nch,),
        in_specs=[col(CB_GQ), col(CB_GK), col(CB_GV), col(0), col(0),
                  pl.BlockSpec((1, GLA_WIDTH, hd), lambda n: (cmap(n), 0, 0))],
        out_specs=[col(0)] * 4, out_shape=[wide] * 4,
        scratch_shapes=[pltpu.VMEM((GLA_WIDTH, hd), F32)],
        compiler_params=pltpu.CompilerParams(dimension_semantics=("arbitrary",)),
    )(pg, pg, pg, la, do, sprev)


def _log_sigmoid(x):
    return jnp.minimum(x, 0.0) - jnp.log(1.0 + jnp.exp(-jnp.abs(x)))


def _gla_alpha_padded(w_alpha):
    w = jnp.zeros((2, 128, GLA_WIDTH), w_alpha.dtype)
    w = w.at[0, 0:GLA_LOWRANK].set(w_alpha[0])
    return w.at[1, GLA_LOWRANK:2 * GLA_LOWRANK].set(w_alpha[1])


def _gla_branch_fwd(pg, w_alpha, b_alpha, norm_gain):
    wa = _gla_alpha_padded(w_alpha).astype(BF16)

    def gates(z, w, bias):
        return (_log_sigmoid(_dot(z, w[0]) + bias[0:1]) / GLA_TAU, _log_sigmoid(_dot(z, w[1]) + bias[1:2]) / GLA_TAU)
    la_f, la_b = _rowmap(gates, [(pg, CB_Z, 128)], [wa, b_alpha], [(GLA_WIDTH, F32), (GLA_WIDTH, F32)], tl=512,
                         name="gla_gates")
    o_f, sp_f = _gla_fwd(pg, la_f, False, name="gla_fwd")
    o_b, sp_b = _gla_fwd(pg, la_b, True, name="gla_fwd_rev")

    def post(of, ob, gate, gn):
        o = of + ob
        on = jnp.concatenate([o[:, s:s + GLA_HEAD_DIM] * _rms(o[:, s:s + GLA_HEAD_DIM]) * gn
                              for s in range(0, GLA_WIDTH, GLA_HEAD_DIM)], axis=1)
        return o, on * (gate * _sigmoid(gate))
    o, y = _rowmap(post, [o_f, o_b, (pg, CB_GG, GLA_WIDTH)], [norm_gain.reshape(1, -1)],
                   [(GLA_WIDTH, F32), (GLA_WIDTH, BF16)], tl=512, name="gla_post")
    return y, (wa, la_f, la_b, sp_f, sp_b, o)


def _gla_branch_bwd(pg, w_alpha, b_alpha, norm_gain, saved, dy):
    wa, la_f, la_b, sp_f, sp_b, o = saved

    def post_bwd(dyv, ov, gate, gn):
        s = _sigmoid(gate)
        dos, dgn, ons = [], [], []
        for c in range(0, GLA_WIDTH, GLA_HEAD_DIM):
            oh = ov[:, c:c + GLA_HEAD_DIM]
            r = _rms(oh)
            don = dyv[:, c:c + GLA_HEAD_DIM] * (gate[:, c:c + GLA_HEAD_DIM] * s[:, c:c + GLA_HEAD_DIM])
            gd = don * gn
            dos.append(r * gd - oh * (r * r * r) * jnp.mean(oh * gd, axis=-1, keepdims=True))
            dgn.append(jnp.sum(don * oh * r, axis=0, keepdims=True))
            ons.append(oh * r * gn)
        on = jnp.concatenate(ons, axis=1)
        dgate = dyv * on * (s * (1.0 + gate * (1.0 - s)))
        return jnp.concatenate(dos, axis=1), dgate, jnp.concatenate(dgn, axis=1)
    do, dgate, dgn = _rowmap(post_bwd, [dy, o, (pg, CB_GG, GLA_WIDTH)], [norm_gain.reshape(1, -1)],
                             [(GLA_WIDTH, F32), (GLA_WIDTH, F32)], [(1, GLA_WIDTH)], tl=512, name="gla_post_bwd")
    dq_f, dk_f, dv_f, dla_f = _gla_bwd(pg, la_f, do, sp_f, False, name="gla_bwd")
    dq_b, dk_b, dv_b, dla_b = _gla_bwd(pg, la_b, do, sp_b, True, name="gla_bwd_rev")

    def gates_bwd(z, dlf, dlb, w, bias):
        dz = jnp.zeros_like(z)
        dlogits, dbs = [], []
        for d, dl in ((0, dlf), (1, dlb)):
            logit = _dot(z, w[d]) + bias[d:d + 1]
            dlogit = dl * (1.0 / GLA_TAU) * jnp.exp(_log_sigmoid(-logit))
            dz = dz + _dot(dlogit, w[d], _NT)
            dlogits.append(dlogit)
            dbs.append(jnp.sum(dlogit, axis=0, keepdims=True))
        return dz, dlogits[0], dlogits[1], dbs[0], dbs[1]
    dz, dlg_f, dlg_b, dba_f, dba_b = _rowmap(
        gates_bwd, [(pg, CB_Z, 128), dla_f, dla_b], [wa, b_alpha], [(128, F32), (GLA_WIDTH, BF16), (GLA_WIDTH, BF16)],
        [(1, GLA_WIDTH), (1, GLA_WIDTH)], tl=512, name="gla_gates_bwd")
    dwa_f = _mm(dlg_f, (pg, CB_Z, 128), ta=True, name="gla_dwalpha")
    dwa_b = _mm(dlg_b, (pg, CB_Z, 128), ta=True, name="gla_dwalpha")
    grads = {'w_alpha': jnp.stack([dwa_f[:, 0:GLA_LOWRANK].T, dwa_b[:, GLA_LOWRANK:2 * GLA_LOWRANK].T]),
             'b_alpha': jnp.concatenate([dba_f, dba_b], axis=0),
             'norm': jnp.sum(dgn.reshape(GLA_HEADS, GLA_HEAD_DIM), axis=0)}
    return [dq_f, dq_b], [dk_f, dk_b], [dv_f, dv_b], dgate, dz, grads


def _rope_tables(length):
    half = ATTN_HEAD_DIM // 2
    inv_freq = ROPE_BASE ** (-jnp.arange(half // 2, dtype=F32) * 2.0 / half)
    t = jnp.arange(length, dtype=jnp.int32)
    def one(pos):
        ang = pos.astype(F32)[:, None] * inv_freq[None, :]
        c, s = jnp.cos(ang), jnp.sin(ang)
        return jnp.concatenate([c, c], axis=1), jnp.concatenate([-s, s], axis=1)
    c_r, s_r = one(t // GRID_W)
    c_c, s_c = one(t % GRID_W)
    return jnp.concatenate([c_r, c_c], axis=1), jnp.concatenate([s_r, s_c], axis=1)


def _rope_swap(y):
    w = y.shape[1]
    lane = lax.broadcasted_iota(jnp.int32, y.shape, 1)
    return jnp.where(lane % 32 < 16, pltpu.roll(y, w - 16, 1), pltpu.roll(y, 16, 1))


def _head_sums(x, ones):
    parts = [lax.dot_general(p, ones, _NN, preferred_element_type=F32) for p in _split3(x)]
    return parts[0] + parts[1] + parts[2]


def _head_ones(width):
    seg = np.arange(width) // ATTN_HEAD_DIM
    return jnp.asarray(seg[:, None] == seg[None, :], BF16)


def _qk_prep_fwd(pg, cb, width, gain, cos, sin, scale, *, name):
    heads = width // ATTN_HEAD_DIM
    def fn(x, c, s, g, ones):
        r = lax.rsqrt(_head_sums(x * x, ones) * (1.0 / ATTN_HEAD_DIM) + NORM_EPS)
        y = x * r * g
        return (y * c + _rope_swap(y) * s) * scale
    return _rowmap(fn, [(pg, cb, width), jnp.tile(cos, (1, heads)), jnp.tile(sin, (1, heads))],
                   [jnp.tile(gain, heads).reshape(1, -1), _head_ones(width)], [(width, BF16)], tl=512, name=name)[0]


def _qk_prep_bwd(pg, cb, width, gain, cos, sin, scale, dout, *, name):
    heads = width // ATTN_HEAD_DIM
    def fn(x, dov, c, s, g, ones):
        r = lax.rsqrt(_head_sums(x * x, ones) * (1.0 / ATTN_HEAD_DIM) + NORM_EPS)
        dos = dov * scale
        dy = dos * c + _rope_swap(dos * s)
        gd = dy * g
        dx = r * gd - x * (r * r * r) * (_head_sums(x * gd, ones) * (1.0 / ATTN_HEAD_DIM))
        return dx, jnp.sum(dy * x * r, axis=0, keepdims=True)
    dx, dg = _rowmap(fn, [(pg, cb, width), dout, jnp.tile(cos, (1, heads)), jnp.tile(sin, (1, heads))],
                     [jnp.tile(gain, heads).reshape(1, -1), _head_ones(width)], [(width, F32)], [(1, width)], tl=512,
                     name=name)
    return dx, jnp.sum(dg.reshape(heads, ATTN_HEAD_DIM), axis=0)


def _to_heads(x, heads):
    return jnp.transpose(x.reshape(x.shape[0], heads, ATTN_HEAD_DIM), (1, 0, 2))


def _from_heads(x):
    return jnp.transpose(x, (1, 0, 2)).reshape(x.shape[1], x.shape[0] * ATTN_HEAD_DIM)


ATTN_GROUP = ATTN_Q_HEADS // ATTN_KV_HEADS
ATTN_TQ = 256


def _attn_fwd(q, k, v, side=None):
    length = q.shape[1]
    tq = min(ATTN_TQ, length)
    grid = (ATTN_KV_HEADS, length // tq)

    def compute(refs):
        q_ref, k_ref, v_ref, o_ref = refs
        kk, vv = k_ref[0], v_ref[0]
        for g in range(ATTN_GROUP):
            s = _dot(q_ref[g], kk, _NT)
            p = jnp.exp(s - jnp.max(s, axis=-1, keepdims=True))
            o_ref[g] = _dot(p, vv) / jnp.sum(p, axis=-1, keepdims=True)

    def body(*refs):
        _carried(side, grid, refs, 3, 1, 0, compute)

    kv_spec = pl.BlockSpec((1, length, ATTN_HEAD_DIM), lambda h, i: (h, 0, 0))
    q_spec = pl.BlockSpec((ATTN_GROUP, tq, ATTN_HEAD_DIM), lambda h, i: (h, i, 0))
    (out,), gathered = _side_call(
        body, side, name="attn_fwd", grid=grid, in_specs=[q_spec, kv_spec, kv_spec], out_specs=[q_spec],
        out_shape=[jax.ShapeDtypeStruct(q.shape, F32)], scratch=[], args=[q, k, v], semantics=("parallel", "parallel"))
    return out, gathered


def _attn_bwd(q, k, v, o, do, side=None):
    length = q.shape[1]
    tq = min(ATTN_TQ, length)
    grid = (ATTN_KV_HEADS, length // tq)

    def body(*refs):
        _carried(side, grid, refs, 5, 3, 0, compute)

    def compute(refs):
        q_ref, k_ref, v_ref, o_ref, do_ref, dq_ref, dk_ref, dv_ref = refs

        @pl.when(pl.program_id(1) == 0)
        def _():
            dk_ref[...] = jnp.zeros_like(dk_ref)
            dv_ref[...] = jnp.zeros_like(dv_ref)

        kk, vv = k_ref[0], v_ref[0]
        for g in range(ATTN_GROUP):
            qg, dog = q_ref[g], do_ref[g]
            s = _dot(qg, kk, _NT)
            p = jnp.exp(s - jnp.max(s, axis=-1, keepdims=True))
            p = p * (1.0 / jnp.sum(p, axis=-1, keepdims=True))
            dp = _dot(dog, vv, _NT)
            ds = p * (dp - jnp.sum(dog * o_ref[g], axis=-1, keepdims=True))
            dq_ref[g] = _dot(ds, kk)
            dk_ref[0] += _dot(ds, qg, _TN)
            dv_ref[0] += _dot(p, dog, _TN)

    kv_spec = pl.BlockSpec((1, length, ATTN_HEAD_DIM), lambda h, i: (h, 0, 0))
    q_spec = pl.BlockSpec((ATTN_GROUP, tq, ATTN_HEAD_DIM), lambda h, i: (h, i, 0))
    return _side_call(
        body, side, name="attn_bwd", grid=grid, in_specs=[q_spec, kv_spec, kv_spec, q_spec, q_spec],
        out_specs=[q_spec, kv_spec, kv_spec],
        out_shape=[jax.ShapeDtypeStruct(q.shape, F32), jax.ShapeDtypeStruct(k.shape, F32), jax.ShapeDtypeStruct(k.shape, F32)],
        scratch=[], args=[q, k, v, o, do], semantics=("parallel", "arbitrary"))


def _attn_branch_fwd(pg, q_gain, k_gain, side=None):
    cos, sin = _rope_tables(pg.shape[0])
    qp = _qk_prep_fwd(pg, CB_AQ, ATTN_WIDTH, q_gain, cos, sin, ATTN_HEAD_DIM ** -0.5, name="attn_q_prep")
    kp = _qk_prep_fwd(pg, CB_AK, ATTN_KV_WIDTH, k_gain, cos, sin, 1.0, name="attn_k_prep")
    qh, kh = _to_heads(qp, ATTN_Q_HEADS), _to_heads(kp, ATTN_KV_HEADS)
    vh = _to_heads(pg[:, P_OFF + 3200:P_OFF + 3328].astype(BF16), ATTN_KV_HEADS)
    oh, gathered = _attn_fwd(qh, kh, vh, side)
    return _from_heads(oh).astype(BF16), (cos, sin, qh, kh, vh, oh), gathered


def _attn_branch_bwd(pg, q_gain, k_gain, saved, dy, side=None):
    cos, sin, qh, kh, vh, oh = saved
    (dqh, dkh, dvh), carried = _attn_bwd(qh, kh, vh, oh, _to_heads(dy, ATTN_Q_HEADS), side)
    dq, dqg = _qk_prep_bwd(pg, CB_AQ, ATTN_WIDTH, q_gain, cos, sin, ATTN_HEAD_DIM ** -0.5, _from_heads(dqh),
                           name="attn_q_prep_bwd")
    dk, dkg = _qk_prep_bwd(pg, CB_AK, ATTN_KV_WIDTH, k_gain, cos, sin, 1.0, _from_heads(dkh), name="attn_k_prep_bwd")
    return dq, dk, _from_heads(dvh), {'q_norm': dqg, 'k_norm': dkg}, carried


def _gate_cols():
    return [slice(i * D_MODEL, (i + 1) * D_MODEL) for i in range(3)]


def _mixer_fwd(x, lw, side_in=None, after_in=None, side_s5=None, side_attn=None):
    h = _rmsnorm_fwd(x, lw['mix_norm'])
    if side_in is None:
        pg = _mm(h, lw['w_pg'], name="mix_in")
    else:
        pg, got_in = _mm(h, lw['w_pg'], side=side_in, name="mix_in")
        after_in(got_in)
    y_s5, s_s5, got_s5 = _s5_fwd(pg, lw['s5'], lw['s5_w_glu'], side_s5)
    y_gla, s_gla = _gla_branch_fwd(pg, lw['gla_w_alpha'], lw['gla_b_alpha'], lw['gla_norm'])
    y_att, s_att, got_attn = _attn_branch_fwd(pg, lw['attn_q_norm'], lw['attn_k_norm'], side_attn)
    ys = (y_s5, y_gla, y_att)
    br = [_mm(y, lw[n], name="mix_branch") for y, n in zip(ys, ('w_branch_s5', 'w_branch_gla', 'w_branch_attn'))]

    def merge(g0, g1, g2, b0, b1, b2, bias):
        acc = None
        for g, b, c in zip((g0, g1, g2), (b0, b1, b2), _gate_cols()):
            term = _sigmoid(g + bias[:, c]) * b
            acc = term if acc is None else acc + term
        return acc
    merged = _rowmap(merge, [(pg, 0, D_MODEL), (pg, 1, D_MODEL), (pg, 2, D_MODEL)] + br,
                     [lw['b_merge_gate'].reshape(1, -1)], [(D_MODEL, BF16)], tl=256, name="mix_merge")[0]
    x_out = _mm(merged, lw['w_out'], add=x, name="mix_out")
    return x_out, (x, h, pg, ys, (s_s5, s_gla, s_att), br, merged), (got_s5, got_attn)


def _mixer_bwd(saved, lw, dx_out, side=None):
    x, h, pg, ys, (s_s5, s_gla, s_att), br, merged = saved
    grads = {'w_out': _mm(merged, dx_out, ta=True, out_dtype=BF16, name="mix_dwout")}
    dmerged = _mm(dx_out, lw['w_out'], tb=True, name="mix_dmerged")

    def merge_bwd(g0, g1, g2, b0, b1, b2, dm, bias):
        dbr, dgp = [], []
        for g, b, c in zip((g0, g1, g2), (b0, b1, b2), _gate_cols()):
            s = _sigmoid(g + bias[:, c])
            dbr.append(dm * s)
            dgp.append(dm * b * (s * (1.0 - s)))
        dgp = jnp.concatenate(dgp, axis=1)
        return dbr[0], dbr[1], dbr[2], dgp, jnp.sum(dgp, axis=0, keepdims=True)
    d0, d1, d2, dgpre, dbias = _rowmap(
        merge_bwd, [(pg, 0, D_MODEL), (pg, 1, D_MODEL), (pg, 2, D_MODEL)] + br + [dmerged],
        [lw['b_merge_gate'].reshape(1, -1)], [(D_MODEL, BF16)] * 3 + [(GATE_WIDTH, BF16)], [(1, GATE_WIDTH)], tl=256,
        name="mix_merge_bwd")
    grads['b_merge_gate'] = dbias[0]
    dys = []
    for y, dbr, n in zip(ys, (d0, d1, d2), ('w_branch_s5', 'w_branch_gla', 'w_branch_attn')):
        grads[n] = _mm(y, dbr, ta=True, out_dtype=BF16, name="mix_dwbranch")
        dys.append(_mm(dbr, lw[n], tb=True, name="mix_dy"))
    du, g_s5 = _s5_bwd(pg, lw['s5'], lw['s5_w_glu'], s_s5, dys[0])
    dgq, dgk, dgv, dgg, dz, g_gla = _gla_branch_bwd(pg, lw['gla_w_alpha'], lw['gla_b_alpha'], lw['gla_norm'], s_gla, dys[1])
    daq, dak, dav, g_att, carried = _attn_branch_bwd(pg, lw['attn_q_norm'], lw['attn_k_norm'], s_att, dys[2], side)

    def assemble(dgp, u0, u1, u2, q0, q1, k0, k1, v0, v1, gg, aq, ak, av, z):
        pad = jnp.zeros((dgp.shape[0], IN_PAD - 3456), F32)
        parts = [dgp.astype(F32), u0 + u1 + u2, q0 + q1, k0 + k1, v0 + v1, gg, aq, ak, av, z, pad]
        return jnp.concatenate(parts, axis=1)
    dpg = _rowmap(assemble, [dgpre] + du + dgq + dgk + dgv + [dgg, daq, dak, dav, dz], [], [(PG_WIDTH, BF16)], tl=256,
                  name="mix_dpg")[0]
    grads['w_pg'] = _mm(h, dpg, ta=True, out_dtype=BF16, name="mix_dwpg")
    dh = _mm(dpg, lw['w_pg'], tb=True, name="mix_dh")
    dx, grads['mix_norm'] = _rmsnorm_bwd(x, lw['mix_norm'], dh, dx_out)
    grads['s5'], grads['gla'], grads['attn'] = g_s5, g_gla, g_att
    return dx, grads, carried


def _loss_head(x, gain, target):
    width = x.shape[1]

    def fn(xv, tv, g):
        r = _rms(xv)
        err = xv * r * g - tv
        dy = err * (1.0 / width)
        gd = dy * g
        dx = r * gd - xv * (r * r * r) * jnp.mean(xv * gd, axis=-1, keepdims=True)
        loss = jnp.sum(0.5 * jnp.mean(err * err, axis=-1, keepdims=True), axis=0, keepdims=True)
        return dx, jnp.broadcast_to(loss, (1, 128)), jnp.sum(dy * xv * r, axis=0, keepdims=True)
    dx, loss, dgain = _rowmap(fn, [x, target], [gain.reshape(1, -1)], [(width, F32)], [(1, 128), (1, width)], tl=256,
                              name="loss_head")
    return loss[0, 0], dx, dgain[0]


def _row_tile(rows, cap=256):
    for t in range(cap - cap % 16, 0, -16):
        if rows % t == 0:
            return t
    return rows


def _reduce_adamw(parts, w, m, v, *, name):
    r, c = w.shape
    if len(parts) > 1 and parts[0].shape[1] % 8:
        parts = [jnp.concatenate(parts, axis=1)]
    nparts, rows = parts[0].shape[0], parts[0].shape[1]
    tr = _row_tile(rows)
    per = rows // tr

    def body(*refs):
        p_refs, (w_ref, m_ref, v_ref, g_ref, d_ref, m2_ref, v2_ref) = refs[:len(parts)], refs[len(parts):]
        g = None
        for k, p_ref in enumerate(p_refs):
            gk = p_ref[0].astype(F32)
            for j in range(1, nparts):
                gk = gk + p_ref[j].astype(F32)
            g = gk if g is None else jnp.where(pl.program_id(0) // per == k, gk, g)
        m2 = ADAM_B1 * m_ref[...] + (1.0 - ADAM_B1) * g
        v2 = ADAM_B2 * v_ref[...] + (1.0 - ADAM_B2) * (g * g)
        m_hat = m2 / (1.0 - ADAM_B1 ** ADAM_STEP)
        v_hat = v2 / (1.0 - ADAM_B2 ** ADAM_STEP)
        g_ref[...] = g
        d_ref[...] = -ADAM_LR * (m_hat / (jnp.sqrt(v_hat) + ADAM_EPS) + ADAM_WD * w_ref[...])
        m2_ref[...] = m2
        v2_ref[...] = v2

    flat = pl.BlockSpec((tr, c), lambda i: (i, 0))
    p_specs = [pl.BlockSpec((nparts, tr, c), lambda i, k=k: (0, jnp.clip(i - k * per, 0, per - 1), 0)) for k in range(len(parts))]
    return pl.pallas_call(
        body, name=name, grid=(r // tr,), in_specs=p_specs + [flat, flat, flat],
        out_specs=[flat] * 4, out_shape=[jax.ShapeDtypeStruct((r, c), F32)] * 4,
        compiler_params=pltpu.CompilerParams(dimension_semantics=("parallel",)),
    )(*parts, w, m, v)


def _all_gather(blocks, *, name):
    side = _SideGather(blocks)

    def body(*refs):
        start, finish = side.hooks(refs)
        start()
        finish()

    return pl.pallas_call(body, name=name, out_shape=side.out_shape, in_specs=side.in_specs, out_specs=side.out_specs,
                          scratch_shapes=side.scratch)(*blocks)


class _SideGather:
    def __init__(self, blocks):
        self.blocks = list(blocks)
        self.n = n = len(self.blocks)
        hbm = pl.BlockSpec(memory_space=pl.ANY)
        self.in_specs, self.out_specs = [hbm] * n, [hbm] * n
        self.out_shape = [jax.ShapeDtypeStruct((N_DEV,) + b.shape, b.dtype) for b in self.blocks]
        self.scratch = [pltpu.SemaphoreType.DMA((n, 7)), pltpu.SemaphoreType.DMA((n, 7)), pltpu.SemaphoreType.DMA((n,))]

    def hooks(self, refs):
        n = self.n
        x_refs, out_refs = refs[:n], refs[n:2 * n]
        send_sems, recv_sems, local_sems = refs[2 * n:]
        x, y, c = lax.axis_index("x"), lax.axis_index("y"), lax.axis_index("c")
        me, sibling = (x, y, c), (x, y, 1 - c)
        chips = [(1 - x, y), (x, 1 - y), (1 - x, 1 - y)]

        def slot(t, px, py, pc):
            return out_refs[t].at[4 * px + 2 * py + pc]

        def copy(t, k, blk, to, own=False):
            return pltpu.make_async_remote_copy(
                src_ref=x_refs[t] if own else slot(t, *blk), dst_ref=slot(t, *blk), send_sem=send_sems.at[t, k],
                recv_sem=recv_sems.at[t, k], device_id=to, device_id_type=pl.DeviceIdType.MESH)

        def mine(t):
            return pltpu.make_async_copy(x_refs[t], slot(t, *me), local_sems.at[t])

        def first(t):
            return [copy(t, 0, me, sibling, own=True)] + [copy(t, 1 + j, me, (*chip, c), own=True) for j, chip in enumerate(chips)]

        def start():
            for t in range(n):
                mine(t).start()
            for t in range(n):
                for cp in first(t):
                    cp.start()

        def finish():
            passed = []
            for j, chip in enumerate(chips):
                for t in range(n):
                    copy(t, 1 + j, (*chip, c), me).wait_recv()
                    passed.append(copy(t, 4 + j, (*chip, c), sibling))
                    passed[-1].start()
            for t in range(n):
                copy(t, 0, sibling, me).wait_recv()
            for j, chip in enumerate(chips):
                for t in range(n):
                    copy(t, 4 + j, (*chip, 1 - c), me).wait_recv()
            for t in range(n):
                for cp in first(t):
                    cp.wait_send()
            for cp in passed:
                cp.wait_send()
            for t in range(n):
                mine(t).wait()

        return start, finish


def _first_last_step(grid):
    ids = [pl.program_id(a) for a in range(len(grid))]
    first = functools.reduce(lambda p, q: p & q, [i == 0 for i in ids])
    last = functools.reduce(lambda p, q: p & q, [i == n - 1 for i, n in zip(ids, grid)])
    return first, last


def _carried(side, grid, refs, n_in, n_out, n_scratch, compute):
    if side is None:
        compute(refs)
        return
    n = side.n
    main = refs[:n_in] + refs[n_in + n:n_in + n + n_out] + refs[n_in + 2 * n + n_out:n_in + 2 * n + n_out + n_scratch]
    side_refs = refs[n_in:n_in + n] + refs[n_in + n + n_out:n_in + 2 * n + n_out] + refs[n_in + 2 * n + n_out + n_scratch:]
    start, finish = side.hooks(side_refs)
    first, last = _first_last_step(grid)
    pl.when(first)(start)
    compute(main)
    pl.when(last)(finish)


N_CHIP = N_DEV // 2


def _swap_with_sibling(arrays, *, name):
    n = len(arrays)

    def body(*refs):
        src_refs, out_refs = refs[:n], refs[n:2 * n]
        send_sems, recv_sems = refs[2 * n:]
        sibling = (lax.axis_index("x"), lax.axis_index("y"), 1 - lax.axis_index("c"))
        copies = [pltpu.make_async_remote_copy(
            src_ref=src_refs[t], dst_ref=out_refs[t], send_sem=send_sems.at[t], recv_sem=recv_sems.at[t],
            device_id=sibling, device_id_type=pl.DeviceIdType.MESH) for t in range(n)]
        for cp in copies:
            cp.start()
        for cp in copies:
            cp.wait()

    hbm = pl.BlockSpec(memory_space=pl.ANY)
    return pl.pallas_call(
        body, name=name, out_shape=[jax.ShapeDtypeStruct(a.shape, a.dtype) for a in arrays],
        in_specs=[hbm] * n, out_specs=[hbm] * n,
        scratch_shapes=[pltpu.SemaphoreType.DMA((n,)), pltpu.SemaphoreType.DMA((n,))],
    )(*arrays)


def _exchange_chips(stacks, *, name, gather=()):
    side, extra = _SideChipExchange(stacks), _SideGather(gather)
    n, m = side.n, extra.n

    def body(*refs):
        start, finish = side.hooks(refs[:n] + refs[n + m:2 * n + m] + refs[2 * (n + m):2 * (n + m) + 3])
        also_start, also_finish = extra.hooks(refs[n:n + m] + refs[2 * n + m:2 * (n + m)] + refs[2 * (n + m) + 3:])
        start()
        also_start()
        also_finish()
        finish()

    res = pl.pallas_call(body, name=name, out_shape=side.out_shape + extra.out_shape, in_specs=side.in_specs + extra.in_specs,
                         out_specs=side.out_specs + extra.out_specs, scratch_shapes=side.scratch + extra.scratch)(*stacks, *gather)
    return res[:n], res[n:]


class _SideChipExchange:
    def __init__(self, stacks):
        self.blocks = list(stacks)
        self.n = n = len(self.blocks)
        hbm = pl.BlockSpec(memory_space=pl.ANY)
        self.in_specs, self.out_specs = [hbm] * n, [hbm] * n
        self.out_shape = [jax.ShapeDtypeStruct(s.shape, s.dtype) for s in self.blocks]
        self.scratch = [pltpu.SemaphoreType.DMA((n, N_DEV - 1)), pltpu.SemaphoreType.DMA((n, N_DEV - 1)),
                        pltpu.SemaphoreType.DMA((n,))]

    def hooks(self, refs):
        n = self.n
        g_refs, out_refs = refs[:n], refs[n:2 * n]
        send_sems, recv_sems, local_sems = refs[2 * n:]
        x, y, c = lax.axis_index("x"), lax.axis_index("y"), lax.axis_index("c")

        def copies():
            mine, remote = [], []
            for t in range(n):
                peers = self.blocks[t].shape[0]
                me = 2 * x + y if peers == N_CHIP else 4 * x + 2 * y + c
                mine.append(pltpu.make_async_copy(g_refs[t].at[me], out_refs[t].at[me], local_sems.at[t]))
                for k in range(1, peers):
                    if peers == N_CHIP:
                        px, py, pc = x ^ (k >> 1 & 1), y ^ (k & 1), c
                        slot = 2 * px + py
                    else:
                        px, py, pc = x ^ (k >> 2 & 1), y ^ (k >> 1 & 1), c ^ (k & 1)
                        slot = 4 * px + 2 * py + pc
                    remote.append(pltpu.make_async_remote_copy(
                        src_ref=g_refs[t].at[slot], dst_ref=out_refs[t].at[me], send_sem=send_sems.at[t, k - 1],
                        recv_sem=recv_sems.at[t, k - 1], device_id=(px, py, pc), device_id_type=pl.DeviceIdType.MESH))
            return mine, remote

        def start():
            mine, remote = copies()
            for cp in mine + remote:
                cp.start()

        def finish():
            mine, remote = copies()
            for cp in remote:
                cp.wait_recv()
            for cp in remote:
                cp.wait_send()
            for cp in mine:
                cp.wait()

        return start, finish


def _pair_sum(a, b):
    return _rowmap(lambda u, v: u.astype(F32) + v.astype(F32), [a, b], [], [(a.shape[1], BF16)], tl=_row_tile(a.shape[0], 512),
                   name="pair_sum")[0]


SMALL_COLS = 128
S5_MATRICES = ('s5_b_re', 's5_b_im', 's5_c_re', 's5_c_im')
SMALL_GROUPS = [([n for n in REPLICATED if n not in S5_MATRICES], F32), (list(S5_MATRICES), BF16)]


def _pack_small(arrays):
    flat = jnp.concatenate([a.astype(F32).reshape(-1, SMALL_COLS) for a in arrays], axis=0)
    return jnp.pad(flat, ((0, -flat.shape[0] % 256), (0, 0)))


def _unpack_small(packed, shapes):
    out, off = [], 0
    for s in shapes:
        r = math.prod(s) // SMALL_COLS
        out.append(packed[off:off + r].reshape(s))
        off += r
    return out


def _split_shards(full, axis):
    shape = full.shape
    split = full.reshape(shape[:axis] + (N_DEV, shape[axis] // N_DEV) + shape[axis + 1:])
    return jnp.moveaxis(split, axis, 0)


def _join_shards(stack, axis):
    moved = jnp.moveaxis(stack, 0, axis)
    shape = moved.shape
    return moved.reshape(shape[:axis] + (shape[axis] * shape[axis + 1],) + shape[axis + 2:])


def _w_in_unpadded(w):
    return jnp.concatenate([w[..., :2560], w[..., 3328:3360], w[..., 2560:3328]], axis=-1)


FFN1_W = ('ffn1_w_gate', 'ffn1_w_up', 'ffn1_w_down')
FFN2_W = ('ffn2_w_gate', 'ffn2_w_up', 'ffn2_w_down')
MIX_IN_W = ('w_in', 'w_merge_gate')
MIX_REST_W = ('s5_w_glu', 'gla_w_alpha', 'gla_b_alpha', 'w_branch_s5', 'w_branch_gla', 'w_branch_attn', 'w_out')


def _mixer_weights(full, w, s5, i):
    lw = {n: w[n][i] for n in ('mix_norm', 'gla_norm', 'attn_q_norm', 'attn_k_norm', 'b_merge_gate')}
    lw['s5'] = {'b_mat': s5['b_mat'][i], 'c_mat': s5['c_mat'][i], 'tabs': [t[i] for t in s5['tabs']],
                'tabs_adj': [t[i] for t in s5['tabs_adj']], 'd': w['s5_d'][i]}
    w_in = full['w_in']
    pad = jnp.zeros((D_MODEL, IN_PAD - IN_WIDTH), w_in.dtype)
    lw['w_pg'] = jnp.concatenate([full['w_merge_gate'], w_in[:, :2560], w_in[:, 2592:], w_in[:, 2560:2592], pad], axis=1)
    return lw


def _mixer_weights_rest(full):
    lw = {n: full[n] for n in MIX_REST_W if n != 'gla_b_alpha'}
    lw['gla_b_alpha'] = full['gla_b_alpha'].astype(F32)
    return lw


def _chip_sums(grads, names):
    core = lax.axis_index("c")
    own, for_sibling = [], []
    for n in names:
        by_owner = _split_shards(grads[n], SHARD_AXIS[n] - 1).astype(BF16)
        by_owner = by_owner.reshape((N_CHIP, 2) + by_owner.shape[1:])
        own.append(lax.dynamic_index_in_dim(by_owner, core, axis=1, keepdims=False))
        for_sibling.append(lax.dynamic_index_in_dim(by_owner, 1 - core, axis=1, keepdims=False))
    from_sibling = _swap_with_sibling(for_sibling, name="exchange_grads_sibling")
    return [_pair_sum(a.reshape(-1, a.shape[-1]), b.reshape(-1, b.shape[-1])).reshape(a.shape)
            for a, b in zip(own, from_sibling)]


def _step_local(x, target, w, shards):
    s5 = _s5_setup(w)
    full = [{} for _ in range(DEPTH)]

    def wanted(i, *groups):
        return _SideGather([shards[n][i] for names in groups for n in names])

    def arrived(i, stacks, *groups):
        names = [n for group in groups for n in group]
        for n, st in zip(names, stacks):
            full[i][n] = _join_shards(st, SHARD_AXIS[n] - 1)

    arrived(0, _all_gather([shards[n][0] for n in FFN1_W], name="gather_first"), FFN1_W)
    saved, lws = [], []
    for i in range(DEPTH):
        f, first = full[i], i == 0
        x, s1, got = _ffn_fwd(x, w['ffn1_norm'][i], f['ffn1_w_gate'], f['ffn1_w_up'], f['ffn1_w_down'],
                              wanted(i, MIX_IN_W) if first else None)
        if first:
            arrived(i, got, MIX_IN_W)
        lw = _mixer_weights(f, w, s5, i)
        lws.append(lw)

        def after_in(got_in, i=i, lw=lw):
            arrived(i, got_in, MIX_REST_W)
            lw.update(_mixer_weights_rest(full[i]))
        if not first:
            lw.update(_mixer_weights_rest(f))
        x, s2, (got_s5, got) = _mixer_fwd(x, lw, wanted(i, MIX_REST_W) if first else None, after_in,
                                          wanted(i, FFN2_W) if first else None,
                                          wanted(i + 1, FFN1_W, MIX_IN_W, MIX_REST_W) if first else wanted(i, FFN2_W))
        if first:
            arrived(i, got_s5, FFN2_W)
            arrived(i + 1, got, FFN1_W, MIX_IN_W, MIX_REST_W)
        else:
            arrived(i, got, FFN2_W)
        x, s3, _ = _ffn_fwd(x, w['ffn2_norm'][i], f['ffn2_w_gate'], f['ffn2_w_up'], f['ffn2_w_down'])
        saved.append((s1, s2, s3))
    loss, dx, d_final = _loss_head(x, w['final_norm'], target)
    per_layer, incoming = [None] * DEPTH, [{} for _ in range(DEPTH)]
    later = [n for n in SHARDED if n not in FFN2_W]
    pending = []
    for i in reversed(range(DEPTH)):
        f, lw, (s1, s2, s3), g = full[i], lws[i], saved[i], {}
        dx, g['ffn2_norm'], g['ffn2_w_gate'], g['ffn2_w_up'], g['ffn2_w_down'] = _ffn_bwd(
            s3, w['ffn2_norm'][i], f['ffn2_w_gate'], f['ffn2_w_up'], f['ffn2_w_down'], dx)
        pending.append((i, FFN2_W, [_split_shards(g[n], SHARD_AXIS[n] - 1).astype(BF16) for n in FFN2_W]))
        dx, gm, carried = _mixer_bwd(s2, lw, dx, _SideChipExchange([s for _, _, sums in pending for s in sums]))
        for layer, names, _ in pending:
            incoming[layer].update(zip(names, carried[:len(names)]))
            carried = carried[len(names):]
        dx, g['ffn1_norm'], g['ffn1_w_gate'], g['ffn1_w_up'], g['ffn1_w_down'] = _ffn_bwd(
            s1, w['ffn1_norm'][i], f['ffn1_w_gate'], f['ffn1_w_up'], f['ffn1_w_down'], dx)
        g['w_merge_gate'] = gm['w_pg'][:, :GATE_WIDTH]
        g['w_in'] = _w_in_unpadded(gm['w_pg'][:, GATE_WIDTH:])
        for n in ('w_out', 'b_merge_gate', 'w_branch_s5', 'w_branch_gla', 'w_branch_attn', 'mix_norm'):
            g[n] = gm[n]
        g['s5_d'], g['s5_w_glu'], g['s5_raw'] = gm['s5']['d'], gm['s5']['w_glu'], gm['s5']['raw']
        g['gla_w_alpha'], g['gla_b_alpha'], g['gla_norm'] = gm['gla']['w_alpha'], gm['gla']['b_alpha'], gm['gla']['norm']
        g['attn_q_norm'], g['attn_k_norm'] = gm['attn']['q_norm'], gm['attn']['k_norm']
        per_layer[i] = g
        pending = [(i, later, _chip_sums(g, later))]
    stacked = _s5_param_grads(w, [g['s5_raw'] for g in per_layer])
    stacked['final_norm'] = d_final
    small = {n: stacked[n] if n in stacked else jnp.stack([g[n] for g in per_layer]) for n in REPLICATED}
    packs = [_pack_small([small[n] for n in names]).astype(dt) for names, dt in SMALL_GROUPS]
    tail, small_parts = _exchange_chips(pending[0][2], name="exchange_grads_chips", gather=packs)
    incoming[0].update(zip(later, tail))
    return loss, dx, incoming, small_parts


def kernel(x, ffn1_norm, ffn1_w_gate, ffn1_w_up, ffn1_w_down, mix_norm, w_in, s5_lambda_re, s5_lambda_im, s5_log_dt, s5_b_re, s5_b_im, s5_c_re, s5_c_im, s5_d, s5_w_glu, gla_w_alpha, gla_b_alpha, gla_norm, attn_q_norm, attn_k_norm, w_branch_s5, w_branch_gla, w_branch_attn, w_merge_gate, b_merge_gate, w_out, ffn2_norm, ffn2_w_gate, ffn2_w_up, ffn2_w_down, final_norm, loss_target, m_ffn1_norm, m_ffn1_w_gate, m_ffn1_w_up, m_ffn1_w_down, m_mix_norm, m_w_in, m_s5_lambda_re, m_s5_lambda_im, m_s5_log_dt, m_s5_b_re, m_s5_b_im, m_s5_c_re, m_s5_c_im, m_s5_d, m_s5_w_glu, m_gla_w_alpha, m_gla_b_alpha, m_gla_norm, m_attn_q_norm, m_attn_k_norm, m_w_branch_s5, m_w_branch_gla, m_w_branch_attn, m_w_merge_gate, m_b_merge_gate, m_w_out, m_ffn2_norm, m_ffn2_w_gate, m_ffn2_w_up, m_ffn2_w_down, m_final_norm, v_ffn1_norm, v_ffn1_w_gate, v_ffn1_w_up, v_ffn1_w_down, v_mix_norm, v_w_in, v_s5_lambda_re, v_s5_lambda_im, v_s5_log_dt, v_s5_b_re, v_s5_b_im, v_s5_c_re, v_s5_c_im, v_s5_d, v_s5_w_glu, v_gla_w_alpha, v_gla_b_alpha, v_gla_norm, v_attn_q_norm, v_attn_k_norm, v_w_branch_s5, v_w_branch_gla, v_w_branch_attn, v_w_merge_gate, v_b_merge_gate, v_w_out, v_ffn2_norm, v_ffn2_w_gate, v_ffn2_w_up, v_ffn2_w_down, v_final_norm):
    return _train_step(x, ffn1_norm, ffn1_w_gate, ffn1_w_up, ffn1_w_down, mix_norm, w_in, s5_lambda_re, s5_lambda_im, s5_log_dt, s5_b_re, s5_b_im, s5_c_re, s5_c_im, s5_d, s5_w_glu, gla_w_alpha, gla_b_alpha, gla_norm, attn_q_norm, attn_k_norm, w_branch_s5, w_branch_gla, w_branch_attn, w_merge_gate, b_merge_gate, w_out, ffn2_norm, ffn2_w_gate, ffn2_w_up, ffn2_w_down, final_norm, loss_target, m_ffn1_norm, m_ffn1_w_gate, m_ffn1_w_up, m_ffn1_w_down, m_mix_norm, m_w_in, m_s5_lambda_re, m_s5_lambda_im, m_s5_log_dt, m_s5_b_re, m_s5_b_im, m_s5_c_re, m_s5_c_im, m_s5_d, m_s5_w_glu, m_gla_w_alpha, m_gla_b_alpha, m_gla_norm, m_attn_q_norm, m_attn_k_norm, m_w_branch_s5, m_w_branch_gla, m_w_branch_attn, m_w_merge_gate, m_b_merge_gate, m_w_out, m_ffn2_norm, m_ffn2_w_gate, m_ffn2_w_up, m_ffn2_w_down, m_final_norm, v_ffn1_norm, v_ffn1_w_gate, v_ffn1_w_up, v_ffn1_w_down, v_mix_norm, v_w_in, v_s5_lambda_re, v_s5_lambda_im, v_s5_log_dt, v_s5_b_re, v_s5_b_im, v_s5_c_re, v_s5_c_im, v_s5_d, v_s5_w_glu, v_gla_w_alpha, v_gla_b_alpha, v_gla_norm, v_attn_q_norm, v_attn_k_norm, v_w_branch_s5, v_w_branch_gla, v_w_branch_attn, v_w_merge_gate, v_b_merge_gate, v_w_out, v_ffn2_norm, v_ffn2_w_gate, v_ffn2_w_up, v_ffn2_w_down, v_final_norm)


def _train_step(*args):
    nw = len(W_NAMES)
    x, target = args[0][0], args[1 + nw][0]
    w = dict(zip(W_NAMES, args[1:1 + nw]))
    m = dict(zip(W_NAMES, args[2 + nw:2 + 2 * nw]))
    v = dict(zip(W_NAMES, args[2 + 2 * nw:2 + 3 * nw]))

    loss, dx, incoming, small_parts = _step_local(x, target, w, {n: w[n].astype(BF16) for n in SHARDED})
    loss = lax.psum(loss, ("x", "y", "c"))

    out = {}
    kinds = ('grad', 'delta', 'new_m', 'new_v')
    for n in SHARDED:
        shape = w[n].shape
        flat = lambda a: a.reshape(-1, shape[-1])
        parts = [incoming[i][n].reshape(incoming[i][n].shape[0], -1, shape[-1]) for i in range(DEPTH)]
        res = _reduce_adamw(parts, flat(w[n]), flat(m[n]), flat(v[n]), name="adamw_sharded")
        for kind, a in zip(kinds, res):
            out[kind + '_' + n] = a.reshape(shape)
    for (names, _), parts in zip(SMALL_GROUPS, small_parts):
        res = _reduce_adamw([parts], *[_pack_small([d[n] for n in names]) for d in (w, m, v)], name="adamw_replicated")
        for kind, packed in zip(kinds, res):
            for n, a in zip(names, _unpack_small(packed, [w[n].shape for n in names])):
                out[kind + '_' + n] = a
    return (loss, dx[None]) + tuple(out[kind + '_' + n] for kind in kinds for n in W_NAMES)
```
